```python
import math
import jax, jax.numpy as jnp
from jax import lax
import numpy as np

D_MODEL = 1024
BATCH = 8
SEQ = 8192
DEPTH = 1

HEAD_DIM = 64
SB_HEADS = D_MODEL // (2 * HEAD_DIM)
SWA_HEADS = D_MODEL // (2 * HEAD_DIM)
SWA_KV_HEADS = max(1, SWA_HEADS // 4)
SB_WIDTH = SB_HEADS * HEAD_DIM
SWA_WIDTH = SWA_HEADS * HEAD_DIM
SWA_KV_WIDTH = SWA_KV_HEADS * HEAD_DIM
MIX_WIDTH = SB_WIDTH + SWA_WIDTH
IN_COLS = 3 * SB_WIDTH + SWA_WIDTH + 2 * SWA_KV_WIDTH
BLOCK = 128
WINDOW = 128
REL_BUCKETS = 32
REL_MAX_DIST = 128
D_FF = -((-8 * D_MODEL) // (3 * 256)) * 256
ALPHA = (2 * DEPTH) ** 0.25
BETA_INIT = (8 * DEPTH) ** -0.25
LN_EPS = 1e-5
RMS_EPS = 1e-6

kernel_name = "stickbreak_swa_sink_hybrid_deepnorm"


def layer_norm(x, g, b):
    xf = x.astype(jnp.float32)
    mu = jnp.mean(xf, axis=-1, keepdims=True)
    var = jnp.mean(jnp.square(xf - mu), axis=-1, keepdims=True)
    return ((xf - mu) * lax.rsqrt(var + LN_EPS)).astype(x.dtype) * g + b


def rms_norm(x, g):
    xf = x.astype(jnp.float32)
    y = xf * lax.rsqrt(jnp.mean(jnp.square(xf), axis=-1, keepdims=True) + RMS_EPS)
    return y.astype(x.dtype) * g


def t5_causal_bucket(distance):
    exact = REL_BUCKETS // 2
    d = jnp.maximum(distance, 0)
    d_f = jnp.maximum(d, 1).astype(jnp.float32)
    large = exact + (jnp.log(d_f / exact) / math.log(REL_MAX_DIST / exact)
                     * (REL_BUCKETS - exact)).astype(jnp.int32)
    large = jnp.minimum(large, REL_BUCKETS - 1)
    return jnp.where(d < exact, d, large)


def stick_breaking_attention(q, k, v):
    B, S, H, Dh = q.shape
    nb = S // BLOCK
    scale = Dh ** -0.5
    qb = q.reshape(B, nb, BLOCK, H, Dh).transpose(1, 0, 3, 2, 4)
    key_pos = jnp.arange(S)

    def one_block(args):
        q_blk, i = args
        z = jnp.einsum('bhqd,bkhd->bhqk', q_blk, k).astype(jnp.float32) * scale
        q_pos = i * BLOCK + jnp.arange(BLOCK)
        causal = key_pos[None, :] < q_pos[:, None]
        log_beta = jax.nn.log_sigmoid(z)
        log_1m_beta = jnp.where(causal, jax.nn.log_sigmoid(-z), 0.0)
        suffix = lax.cumsum(log_1m_beta, axis=3, reverse=True) - log_1m_beta
        attn = jnp.where(causal, jnp.exp(log_beta + suffix), 0.0)
        return jnp.einsum('bhqk,bkhd->bqhd', attn.astype(v.dtype), v)

    out = lax.map(one_block, (qb, jnp.arange(nb)))
    return out.transpose(1, 0, 2, 3, 4).reshape(B, S, H * Dh).astype(v.dtype)


def sliding_window_attention(q, k, v, sinks, rel_bias):
    B, S, H, Dh = q.shape
    KVH = k.shape[2]
    G = H // KVH
    nb = S // BLOCK
    scale = Dh ** -0.5
    qb = q.reshape(B, nb, BLOCK, KVH, G, Dh)

    def band(t):
        tb = t.reshape(B, nb, BLOCK, KVH, Dh)
        prev = jnp.pad(tb[:, :-1], ((0, 0), (1, 0), (0, 0), (0, 0), (0, 0)))
        return jnp.concatenate([prev, tb], axis=2)

    kb, vb = band(k), band(v)
    logits = jnp.einsum('bnqhgd,bnchd->bnhgqc', qb, kb).astype(jnp.float32) * scale

    qi = jnp.arange(BLOCK)[:, None]
    cj = jnp.arange(2 * BLOCK)[None, :]
    dist = qi + BLOCK - cj
    key_abs = jnp.arange(nb)[:, None, None] * BLOCK - BLOCK + cj[None]
    valid = (dist >= 0)[None] & (dist < WINDOW)[None] & (key_abs >= 0)
    bias = rel_bias.astype(jnp.float32)[t5_causal_bucket(dist)]
    bias = bias.transpose(2, 0, 1).reshape(KVH, G, BLOCK, 2 * BLOCK)
    logits = jnp.where(valid[None, :, None, None], logits + bias, -jnp.inf)

    sink = sinks.astype(jnp.float32).reshape(1, 1, KVH, G, 1, 1)
    m = jnp.maximum(jnp.max(logits, axis=-1, keepdims=True), sink)
    p = jnp.exp(logits - m)
    denom = jnp.sum(p, axis=-1, keepdims=True) + jnp.exp(sink - m)
    o = jnp.einsum('bnhgqc,bnchd->bnqhgd', (p / denom).astype(v.dtype), vb)
    return o.reshape(B, S, H * Dh).astype(v.dtype)


def hybrid_mixer(h, w_in, sb_norm_g, swa_norm_g, sinks, rel_bias, w_out):
    B, S, _ = h.shape
    proj = h @ w_in
    o1 = SB_WIDTH
    o2 = o1 + SB_WIDTH
    o3 = o2 + SB_WIDTH
    o4 = o3 + SWA_WIDTH
    o5 = o4 + SWA_KV_WIDTH
    q_sb, k_sb, v_sb, q_sw, k_sw, v_sw = jnp.split(proj, [o1, o2, o3, o4, o5], axis=-1)
    sb_out = stick_breaking_attention(
        q_sb.reshape(B, S, SB_HEADS, HEAD_DIM),
        k_sb.reshape(B, S, SB_HEADS, HEAD_DIM),
        v_sb.reshape(B, S, SB_HEADS, HEAD_DIM))
    swa_out = sliding_window_attention(
        q_sw.reshape(B, S, SWA_HEADS, HEAD_DIM),
        k_sw.reshape(B, S, SWA_KV_HEADS, HEAD_DIM),
        v_sw.reshape(B, S, SWA_KV_HEADS, HEAD_DIM),
        sinks, rel_bias)
    merged = jnp.concatenate([rms_norm(sb_out, sb_norm_g),
                              rms_norm(swa_out, swa_norm_g)], axis=-1)
    return merged @ w_out


def swiglu_ffn(h, w_gate_up, w_down):
    gate, up = jnp.split(h @ w_gate_up, 2, axis=-1)
    return (jax.nn.silu(gate) * up) @ w_down


def _fwd_setup_inputs(seed: int = 0) -> dict:
    key = jax.random.key(seed)
    ks = jax.random.split(key, 16)
    f32 = jnp.float32
    n = lambda k, shape, s: jax.random.normal(k, shape, f32) * s
    return {
        "x": n(ks[0], (BATCH, SEQ, D_MODEL), 1.0),
        "ln_in_g": 1.0 + n(ks[1], (D_MODEL,), 0.02),
        "ln_in_b": n(ks[2], (D_MODEL,), 0.02),
        "w_in": n(ks[3], (DEPTH, D_MODEL, IN_COLS), D_MODEL ** -0.5),
        "sb_norm_g": 1.0 + n(ks[4], (DEPTH, SB_WIDTH), 0.02),
        "swa_norm_g": 1.0 + n(ks[5], (DEPTH, SWA_WIDTH), 0.02),
        "sinks": n(ks[6], (DEPTH, SWA_HEADS), 0.5),
        "rel_bias": n(ks[7], (REL_BUCKETS, SWA_HEADS), 0.5),
        "w_out": n(ks[8], (DEPTH, MIX_WIDTH, D_MODEL), MIX_WIDTH ** -0.5 * BETA_INIT),
        "ln1_g": 1.0 + n(ks[9], (DEPTH, D_MODEL), 0.02),
        "ln1_b": n(ks[10], (DEPTH, D_MODEL), 0.02),
        "w_gate_up": n(ks[11], (DEPTH, D_MODEL, 2 * D_FF), D_MODEL ** -0.5),
        "w_down": n(ks[12], (DEPTH, D_FF, D_MODEL), D_FF ** -0.5 * BETA_INIT),
        "ln2_g": 1.0 + n(ks[13], (DEPTH, D_MODEL), 0.02),
        "ln2_b": n(ks[14], (DEPTH, D_MODEL), 0.02),
    }


def _fwd_reference(x, ln_in_g, ln_in_b, w_in, sb_norm_g, swa_norm_g, sinks, rel_bias,
              w_out, ln1_g, ln1_b, w_gate_up, w_down, ln2_g, ln2_b):
    h = layer_norm(x, ln_in_g, ln_in_b)
    for l in range(DEPTH):
        mix = hybrid_mixer(h, w_in[l], sb_norm_g[l], swa_norm_g[l], sinks[l], rel_bias, w_out[l])
        h = layer_norm(ALPHA * h + mix, ln1_g[l], ln1_b[l])
        ffn = swiglu_ffn(h, w_gate_up[l], w_down[l])
        h = layer_norm(ALPHA * h + ffn, ln2_g[l], ln2_b[l])
    return h


import jax as _jax
import jax.numpy as _jnp

TWIN_FORMAT = 'train_step'
FWD_PARAMS = ['x', 'ln_in_g', 'ln_in_b', 'w_in', 'sb_norm_g', 'swa_norm_g', 'sinks', 'rel_bias', 'w_out', 'ln1_g', 'ln1_b', 'w_gate_up', 'w_down', 'ln2_g', 'ln2_b']
TWIN_WEIGHTS = ['ln_in_g', 'ln_in_b', 'w_in', 'sb_norm_g', 'swa_norm_g', 'sinks', 'rel_bias', 'w_out', 'ln1_g', 'ln1_b', 'w_gate_up', 'w_down', 'ln2_g', 'ln2_b']
TWIN_DIFF_INPUT = 'x'
TWIN_INPUTS = ['x', 'ln_in_g', 'ln_in_b', 'w_in', 'sb_norm_g', 'swa_norm_g', 'sinks', 'rel_bias', 'w_out', 'ln1_g', 'ln1_b', 'w_gate_up', 'w_down', 'ln2_g', 'ln2_b', 'loss_target', 'm_ln_in_g', 'm_ln_in_b', 'm_w_in', 'm_sb_norm_g', 'm_swa_norm_g', 'm_sinks', 'm_rel_bias', 'm_w_out', 'm_ln1_g', 'm_ln1_b', 'm_w_gate_up', 'm_w_down', 'm_ln2_g', 'm_ln2_b', 'v_ln_in_g', 'v_ln_in_b', 'v_w_in', 'v_sb_norm_g', 'v_swa_norm_g', 'v_sinks', 'v_rel_bias', 'v_w_out', 'v_ln1_g', 'v_ln1_b', 'v_w_gate_up', 'v_w_down', 'v_ln2_g', 'v_ln2_b']
TWIN_OUTPUTS = ['loss', 'grad_x', 'grad_ln_in_g', 'grad_ln_in_b', 'grad_w_in', 'grad_sb_norm_g', 'grad_swa_norm_g', 'grad_sinks', 'grad_rel_bias', 'grad_w_out', 'grad_ln1_g', 'grad_ln1_b', 'grad_w_gate_up', 'grad_w_down', 'grad_ln2_g', 'grad_ln2_b', 'delta_ln_in_g', 'delta_ln_in_b', 'delta_w_in', 'delta_sb_norm_g', 'delta_swa_norm_g', 'delta_sinks', 'delta_rel_bias', 'delta_w_out', 'delta_ln1_g', 'delta_ln1_b', 'delta_w_gate_up', 'delta_w_down', 'delta_ln2_g', 'delta_ln2_b', 'new_m_ln_in_g', 'new_m_ln_in_b', 'new_m_w_in', 'new_m_sb_norm_g', 'new_m_swa_norm_g', 'new_m_sinks', 'new_m_rel_bias', 'new_m_w_out', 'new_m_ln1_g', 'new_m_ln1_b', 'new_m_w_gate_up', 'new_m_w_down', 'new_m_ln2_g', 'new_m_ln2_b', 'new_v_ln_in_g', 'new_v_ln_in_b', 'new_v_w_in', 'new_v_sb_norm_g', 'new_v_swa_norm_g', 'new_v_sinks', 'new_v_rel_bias', 'new_v_w_out', 'new_v_ln1_g', 'new_v_ln1_b', 'new_v_w_gate_up', 'new_v_w_down', 'new_v_ln2_g', 'new_v_ln2_b']
TWIN_LEAF_KINDS = {'loss': 'loss', 'grad_x': 'grad_x', 'grad_ln_in_g': 'grad_w', 'grad_ln_in_b': 'grad_w', 'grad_w_in': 'grad_w', 'grad_sb_norm_g': 'grad_w', 'grad_swa_norm_g': 'grad_w', 'grad_sinks': 'grad_w', 'grad_rel_bias': 'grad_w', 'grad_w_out': 'grad_w', 'grad_ln1_g': 'grad_w', 'grad_ln1_b': 'grad_w', 'grad_w_gate_up': 'grad_w', 'grad_w_down': 'grad_w', 'grad_ln2_g': 'grad_w', 'grad_ln2_b': 'grad_w', 'delta_ln_in_g': 'delta_w', 'delta_ln_in_b': 'delta_w', 'delta_w_in': 'delta_w', 'delta_sb_norm_g': 'delta_w', 'delta_swa_norm_g': 'delta_w', 'delta_sinks': 'delta_w', 'delta_rel_bias': 'delta_w', 'delta_w_out': 'delta_w', 'delta_ln1_g': 'delta_w', 'delta_ln1_b': 'delta_w', 'delta_w_gate_up': 'delta_w', 'delta_w_down': 'delta_w', 'delta_ln2_g': 'delta_w', 'delta_ln2_b': 'delta_w', 'new_m_ln_in_g': 'new_m', 'new_m_ln_in_b': 'new_m', 'new_m_w_in': 'new_m', 'new_m_sb_norm_g': 'new_m', 'new_m_swa_norm_g': 'new_m', 'new_m_sinks': 'new_m', 'new_m_rel_bias': 'new_m', 'new_m_w_out': 'new_m', 'new_m_ln1_g': 'new_m', 'new_m_ln1_b': 'new_m', 'new_m_w_gate_up': 'new_m', 'new_m_w_down': 'new_m', 'new_m_ln2_g': 'new_m', 'new_m_ln2_b': 'new_m', 'new_v_ln_in_g': 'new_v', 'new_v_ln_in_b': 'new_v', 'new_v_w_in': 'new_v', 'new_v_sb_norm_g': 'new_v', 'new_v_swa_norm_g': 'new_v', 'new_v_sinks': 'new_v', 'new_v_rel_bias': 'new_v', 'new_v_w_out': 'new_v', 'new_v_ln1_g': 'new_v', 'new_v_ln1_b': 'new_v', 'new_v_w_gate_up': 'new_v', 'new_v_w_down': 'new_v', 'new_v_ln2_g': 'new_v', 'new_v_ln2_b': 'new_v'}


def _forward(args):
    return _fwd_reference(*[args[k] for k in FWD_PARAMS])


def _output_shape():
    def fwd():
        inp = _fwd_setup_inputs(0)
        return _fwd_reference(*[inp[k] for k in FWD_PARAMS])
    out = _jax.eval_shape(fwd)
    return out.shape, out.dtype

N_MICROBATCH = 1
ADAM_LR = 0.001
ADAM_B1 = 0.9
ADAM_B2 = 0.999
ADAM_EPS = 1e-08
ADAM_WD = 0.01
ADAM_STEP = 10
PER_EXAMPLE_BATCH_AXIS = {'x': 0, 'loss_target': 0}
SHARED_INPUTS = []
_WEIGHT_DTYPES = {'ln_in_g': _jnp.float32, 'ln_in_b': _jnp.float32, 'w_in': _jnp.float32, 'sb_norm_g': _jnp.float32, 'swa_norm_g': _jnp.float32, 'sinks': _jnp.float32, 'rel_bias': _jnp.float32, 'w_out': _jnp.float32, 'ln1_g': _jnp.float32, 'ln1_b': _jnp.float32, 'w_gate_up': _jnp.float32, 'w_down': _jnp.float32, 'ln2_g': _jnp.float32, 'ln2_b': _jnp.float32}
MOMENT_SCALE = {'ln_in_g': 1.510195e+00, 'ln_in_b': 2.155728e+00, 'w_in': 1.138501e-01, 'sb_norm_g': 1.147342e-01, 'swa_norm_g': 1.203575e-01, 'sinks': 4.529059e-02, 'rel_bias': 1.811578e-01, 'w_out': 2.012763e-01, 'ln1_g': 2.092933e+00, 'ln1_b': 9.849614e-01, 'w_gate_up': 4.394504e-02, 'w_down': 1.207391e-01, 'ln2_g': 6.402535e+01, 'ln2_b': 2.441889e+00}


def _to_microbatches(a, axis):
    t = _jnp.moveaxis(a, axis, 0)
    t = t.reshape((N_MICROBATCH, t.shape[0] // N_MICROBATCH) + t.shape[1:])
    return _jnp.moveaxis(t, 1, axis + 1)


def setup_inputs(seed: int = 0) -> dict:
    inp = _fwd_setup_inputs(seed)
    key = _jax.random.fold_in(_jax.random.key(seed), 7919)
    shape, _ = _output_shape()
    out = dict(inp)
    out["loss_target"] = _jax.random.normal(_jax.random.fold_in(key, 0), shape, _jnp.float32)
    for i, name in enumerate(TWIN_WEIGHTS):
        w = inp[name].astype(_jnp.float32)
        if MOMENT_SCALE is None:
            s = _jnp.sqrt(_jnp.mean(_jnp.square(w)) + 1e-30)
        else:
            s = MOMENT_SCALE[name]
        km, kv = _jax.random.split(_jax.random.fold_in(key, i + 1))
        out[name] = w
        out["m_" + name] = s * _jax.random.normal(km, w.shape, _jnp.float32)
        out["v_" + name] = (s * s) * _jax.random.uniform(kv, w.shape, _jnp.float32, 0.5, 1.5)
    if N_MICROBATCH > 1:
        for name, axis in PER_EXAMPLE_BATCH_AXIS.items():
            out[name] = _to_microbatches(out[name], axis)
    return {'x': out['x'], 'ln_in_g': out['ln_in_g'], 'ln_in_b': out['ln_in_b'], 'w_in': out['w_in'], 'sb_norm_g': out['sb_norm_g'], 'swa_norm_g': out['swa_norm_g'], 'sinks': out['sinks'], 'rel_bias': out['rel_bias'], 'w_out': out['w_out'], 'ln1_g': out['ln1_g'], 'ln1_b': out['ln1_b'], 'w_gate_up': out['w_gate_up'], 'w_down': out['w_down'], 'ln2_g': out['ln2_g'], 'ln2_b': out['ln2_b'], 'loss_target': out['loss_target'], 'm_ln_in_g': out['m_ln_in_g'], 'm_ln_in_b': out['m_ln_in_b'], 'm_w_in': out['m_w_in'], 'm_sb_norm_g': out['m_sb_norm_g'], 'm_swa_norm_g': out['m_swa_norm_g'], 'm_sinks': out['m_sinks'], 'm_rel_bias': out['m_rel_bias'], 'm_w_out': out['m_w_out'], 'm_ln1_g': out['m_ln1_g'], 'm_ln1_b': out['m_ln1_b'], 'm_w_gate_up': out['m_w_gate_up'], 'm_w_down': out['m_w_down'], 'm_ln2_g': out['m_ln2_g'], 'm_ln2_b': out['m_ln2_b'], 'v_ln_in_g': out['v_ln_in_g'], 'v_ln_in_b': out['v_ln_in_b'], 'v_w_in': out['v_w_in'], 'v_sb_norm_g': out['v_sb_norm_g'], 'v_swa_norm_g': out['v_swa_norm_g'], 'v_sinks': out['v_sinks'], 'v_rel_bias': out['v_rel_bias'], 'v_w_out': out['v_w_out'], 'v_ln1_g': out['v_ln1_g'], 'v_ln1_b': out['v_ln1_b'], 'v_w_gate_up': out['v_w_gate_up'], 'v_w_down': out['v_w_down'], 'v_ln2_g': out['v_ln2_g'], 'v_ln2_b': out['v_ln2_b']}


def _loss(weights, diff, rest, loss_target):
    with _jax.named_scope("forward"):
        args = {**rest, TWIN_DIFF_INPUT: diff, **{k: w.astype(_WEIGHT_DTYPES[k]) for k, w in weights.items()}}
        y = _forward(args)
    with _jax.named_scope("loss_head"):
        err = _jnp.square(y.astype(_jnp.float32) - loss_target)
        return 0.5 * _jnp.sum(_jnp.mean(err, axis=-1)) if err.ndim else 0.5 * err


def _adamw(w, g, m, v):
    m = ADAM_B1 * m + (1.0 - ADAM_B1) * g
    v = ADAM_B2 * v + (1.0 - ADAM_B2) * _jnp.square(g)
    m_hat = m / (1.0 - ADAM_B1 ** ADAM_STEP)
    v_hat = v / (1.0 - ADAM_B2 ** ADAM_STEP)
    delta = -ADAM_LR * (m_hat / (_jnp.sqrt(v_hat) + ADAM_EPS) + ADAM_WD * w)
    return delta, m, v


def reference(x, ln_in_g, ln_in_b, w_in, sb_norm_g, swa_norm_g, sinks, rel_bias, w_out, ln1_g, ln1_b, w_gate_up, w_down, ln2_g, ln2_b, loss_target, m_ln_in_g, m_ln_in_b, m_w_in, m_sb_norm_g, m_swa_norm_g, m_sinks, m_rel_bias, m_w_out, m_ln1_g, m_ln1_b, m_w_gate_up, m_w_down, m_ln2_g, m_ln2_b, v_ln_in_g, v_ln_in_b, v_w_in, v_sb_norm_g, v_swa_norm_g, v_sinks, v_rel_bias, v_w_out, v_ln1_g, v_ln1_b, v_w_gate_up, v_w_down, v_ln2_g, v_ln2_b):
    given = dict(x=x, ln_in_g=ln_in_g, ln_in_b=ln_in_b, w_in=w_in, sb_norm_g=sb_norm_g, swa_norm_g=swa_norm_g, sinks=sinks, rel_bias=rel_bias, w_out=w_out, ln1_g=ln1_g, ln1_b=ln1_b, w_gate_up=w_gate_up, w_down=w_down, ln2_g=ln2_g, ln2_b=ln2_b, loss_target=loss_target, m_ln_in_g=m_ln_in_g, m_ln_in_b=m_ln_in_b, m_w_in=m_w_in, m_sb_norm_g=m_sb_norm_g, m_swa_norm_g=m_swa_norm_g, m_sinks=m_sinks, m_rel_bias=m_rel_bias, m_w_out=m_w_out, m_ln1_g=m_ln1_g, m_ln1_b=m_ln1_b, m_w_gate_up=m_w_gate_up, m_w_down=m_w_down, m_ln2_g=m_ln2_g, m_ln2_b=m_ln2_b, v_ln_in_g=v_ln_in_g, v_ln_in_b=v_ln_in_b, v_w_in=v_w_in, v_sb_norm_g=v_sb_norm_g, v_swa_norm_g=v_swa_norm_g, v_sinks=v_sinks, v_rel_bias=v_rel_bias, v_w_out=v_w_out, v_ln1_g=v_ln1_g, v_ln1_b=v_ln1_b, v_w_gate_up=v_w_gate_up, v_w_down=v_w_down, v_ln2_g=v_ln2_g, v_ln2_b=v_ln2_b)
    weights = {n: given[n] for n in TWIN_WEIGHTS}
    shared = {n: given[n] for n in SHARED_INPUTS}
    per_example = {n: given[n] for n in ['x']}
    grad_fn = _jax.value_and_grad(_loss, argnums=(0, 1))

    def one_microbatch(ex, loss_target):
        ex = dict(ex)
        diff = ex.pop(TWIN_DIFF_INPUT)
        return grad_fn(weights, diff, {**shared, **ex}, loss_target)

    if N_MICROBATCH == 1:
        loss, (grad_w, grad_x) = one_microbatch(per_example, given["loss_target"])
    else:
        def body(carry, xs):
            loss_sum, grad_sum = carry
            l_k, (gw_k, gx_k) = one_microbatch(xs[0], xs[1])
            with _jax.named_scope("update"):
                return (loss_sum + l_k, _jax.tree.map(_jnp.add, grad_sum, gw_k)), gx_k

        init = (_jnp.zeros((), _jnp.float32), _jax.tree.map(_jnp.zeros_like, weights))
        (loss, grad_w), grad_x = _jax.lax.scan(body, init, (per_example, given["loss_target"]))
    with _jax.named_scope("update"):
        delta_w, new_m, new_v = {}, {}, {}
        for n in TWIN_WEIGHTS:
            delta_w[n], new_m[n], new_v[n] = _adamw(weights[n], grad_w[n], given["m_" + n], given["v_" + n])
    return (loss, grad_x, *[grad_w[n] for n in TWIN_WEIGHTS], *[delta_w[n] for n in TWIN_WEIGHTS],
            *[new_m[n] for n in TWIN_WEIGHTS], *[new_v[n] for n in TWIN_WEIGHTS])
```

```python
import functools
import math

import numpy as np
import jax
import jax.numpy as jnp
from jax import lax
from jax.experimental import pallas as pl
from jax.experimental.pallas import tpu as pltpu

F32 = jnp.float32
BF16 = jnp.bfloat16
MESH = pl.DeviceIdType.MESH

N_DEV = 8
LANES = 128
HEAD_DIM = 64
SCALE = HEAD_DIM ** -0.5
SWA_BLOCK = 128
REL_BUCKETS = 32
REL_MAX_DIST = 128
ALPHA = 2.0 ** 0.25
LN_EPS = 1e-5
RMS_EPS = 1e-6
ADAM_LR = 0.001
ADAM_B1 = 0.9
ADAM_B2 = 0.999
ADAM_EPS = 1e-08
ADAM_WD = 0.01
ADAM_STEP = 10

ROW_TILE = 512
SB_TILE = 256
FF_CHUNK = 256
MIB = 1024 * 1024


def _params(vmem_mib=48):
    return pltpu.CompilerParams(vmem_limit_bytes=vmem_mib * MIB)


def _dot(a, b):
    return jnp.dot(a, b, preferred_element_type=F32)


def _dot_nt(a, b):
    return lax.dot_general(a, b, (((1,), (1,)), ((), ())), preferred_element_type=F32)


def _dot_tn(a, b):
    return lax.dot_general(a, b, (((0,), (0,)), ((), ())), preferred_element_type=F32)


def _ln_hat(x):
    mu = jnp.mean(x, axis=-1, keepdims=True)
    xc = x - mu
    var = jnp.mean(xc * xc, axis=-1, keepdims=True)
    r = lax.rsqrt(var + LN_EPS)
    return xc * r, r


def _ln_bwd(dxhat, xhat, r):
    return r * (dxhat - jnp.mean(dxhat, axis=-1, keepdims=True)
                - xhat * jnp.mean(dxhat * xhat, axis=-1, keepdims=True))


def _colsum(a):
    return jnp.sum(a, axis=0, keepdims=True)


def _rowsum(a):
    return jnp.sum(a, axis=1, keepdims=True)


def _full(shape):
    return pl.BlockSpec(shape, lambda *_: (0,) * len(shape))


def _exchange(name, arrays, kinds):
    n = len(arrays)
    out_shapes = []
    for a, kind in zip(arrays, kinds):
        blk = a.shape if kind == "gather" else a.shape[1:]
        out_shapes.append(jax.ShapeDtypeStruct((N_DEV,) + tuple(blk), a.dtype))

    def body(*refs):
        ins, outs = refs[:n], refs[n:2 * n]
        send_sems, recv_sems, local_sems = refs[2 * n:]
        x, y, c = lax.axis_index("x"), lax.axis_index("y"), lax.axis_index("c")
        me = 4 * x + 2 * y + c

        def peer_of(k):
            px = 1 - x if (k >> 2) & 1 else x
            py = 1 - y if (k >> 1) & 1 else y
            pc = 1 - c if k & 1 else c
            return (px, py, pc), 4 * px + 2 * py + pc

        def src_for(t, dev_lin):
            return ins[t] if kinds[t] == "gather" else ins[t].at[dev_lin]

        local = []
        for t in range(n):
            cp = pltpu.make_async_copy(src_for(t, me), outs[t].at[me], local_sems.at[t])
            cp.start()
            local.append(cp)
        sends = []
        for k in range(1, N_DEV):
            peer, peer_lin = peer_of(k)
            for t in range(n):
                cp = pltpu.make_async_remote_copy(
                    src_ref=src_for(t, peer_lin), dst_ref=outs[t].at[me],
                    send_sem=send_sems.at[t, k - 1], recv_sem=recv_sems.at[t, k - 1],
                    device_id=peer, device_id_type=MESH)
                cp.start()
                sends.append(cp)
        for k in range(1, N_DEV):
            peer, peer_lin = peer_of(k)
            for t in range(n):
                pltpu.make_async_remote_copy(
                    src_ref=src_for(t, peer_lin), dst_ref=outs[t].at[peer_lin],
                    send_sem=send_sems.at[t, k - 1], recv_sem=recv_sems.at[t, k - 1],
                    device_id=peer, device_id_type=MESH).wait_recv()
        for cp in sends:
            cp.wait_send()
        for cp in local:
            cp.wait()

    any_spec = pl.BlockSpec(memory_space=pl.ANY)
    return pl.pallas_call(
        body, name=name, out_shape=out_shapes,
        in_specs=[any_spec] * n, out_specs=[any_spec] * n,
        scratch_shapes=[pltpu.SemaphoreType.DMA((n, N_DEV - 1)),
                        pltpu.SemaphoreType.DMA((n, N_DEV - 1)),
                        pltpu.SemaphoreType.DMA((n,))],
    )(*arrays)


def _ln_proj(x, g, b, w_in):
    s, d = x.shape
    cols = w_in.shape[1]
    tm = min(ROW_TILE, s)

    def body(x_ref, g_ref, b_ref, w_ref, h_ref, p_ref):
        xhat, _ = _ln_hat(x_ref[...])
        h = (xhat * g_ref[...] + b_ref[...]).astype(BF16)
        h_ref[...] = h
        p_ref[...] = _dot(h, w_ref[...]).astype(BF16)

    row = lambda width: pl.BlockSpec((tm, width), lambda i: (i, 0))
    return pl.pallas_call(
        body, name="ln_proj", grid=(s // tm,),
        in_specs=[row(d), _full((1, d)), _full((1, d)), _full((d, cols))],
        out_specs=[row(d), row(cols)],
        out_shape=[jax.ShapeDtypeStruct((s, d), BF16), jax.ShapeDtypeStruct((s, cols), BF16)],
        compiler_params=_params(),
    )(x, g, b, w_in)


def _sb_tile_consts(t):
    row = lax.broadcasted_iota(jnp.int32, (t, t), 0)
    col = lax.broadcasted_iota(jnp.int32, (t, t), 1)
    return row, col


def _sb_scores(qh, k_t, upper, carry_l, causal):
    z = _dot_nt(qh, k_t) * SCALE
    sp = jnp.log(1.0 + jnp.exp(-jnp.abs(z)))
    lb = jnp.minimum(z, 0.0) - sp
    l1 = -(jnp.maximum(z, 0.0) + sp)
    if causal is not None:
        l1 = jnp.where(causal, l1, 0.0)
    hi = l1.astype(BF16)
    lo = (l1 - hi.astype(F32)).astype(BF16)
    suf = _dot(hi, upper) + _dot(lo, upper) + carry_l
    a = jnp.exp(lb + suf)
    if causal is not None:
        a = jnp.where(causal, a, 0.0)
    return lb, l1, a


def _sb_fwd(proj, n_pairs):
    s = proj.shape[0]
    t = min(SB_TILE, s)
    nq = s // t

    def body(q_ref, k_ref, v_ref, o_ref):
        i = pl.program_id(1)
        lane = lax.broadcasted_iota(jnp.int32, (1, LANES), 1)
        first = lane < HEAD_DIM
        q2 = q_ref[...]
        zero = jnp.zeros_like(q2)
        qs = (jnp.where(first, q2, zero), jnp.where(first, zero, q2))
        row, col = _sb_tile_consts(t)
        upper = (row > col).astype(BF16)
        causal = col < row

        def visit(off, carry, mask):
            k_t = k_ref[pl.ds(off, t), :]
            v_t = v_ref[pl.ds(off, t), :]
            out = []
            for hh in range(2):
                c_l, acc = carry[hh]
                _, l1, a = _sb_scores(qs[hh], k_t, upper, c_l, mask)
                out.append((c_l + _rowsum(l1), acc + _dot(a.astype(BF16), v_t)))
            return tuple(out)

        init = tuple((jnp.zeros((t, 1), F32), jnp.zeros((t, LANES), F32)) for _ in range(2))
        carry = visit(pl.multiple_of(i * t, t), init, causal)

        def step(j, carry):
            return visit(pl.multiple_of((i - 1 - j) * t, t), carry, None)

        carry = lax.fori_loop(0, i, step, carry)
        o_ref[...] = jnp.where(first, carry[0][1], carry[1][1])

    return pl.pallas_call(
        body, name="sb_fwd", grid=(n_pairs, nq),
        in_specs=[pl.BlockSpec((t, LANES), lambda h, i: (i, h)),
                  pl.BlockSpec((s, LANES), lambda h, i: (0, n_pairs + h)),
                  pl.BlockSpec((s, LANES), lambda h, i: (0, 2 * n_pairs + h))],
        out_specs=pl.BlockSpec((t, LANES), lambda h, i: (i, h)),
        out_shape=jax.ShapeDtypeStruct((s, n_pairs * LANES), F32),
        compiler_params=_params(),
    )(proj, proj, proj)


def _swa_bucket_table():
    qi = np.arange(SWA_BLOCK)[:, None]
    cj = np.arange(2 * SWA_BLOCK)[None, :]
    dist = qi + SWA_BLOCK - cj
    exact = REL_BUCKETS // 2
    d = np.maximum(dist, 0)
    d_f = np.maximum(d, 1).astype(np.float32)
    large = exact + (np.log(d_f / np.float32(exact)) / np.float32(math.log(REL_MAX_DIST / exact))
                     * np.float32(REL_BUCKETS - exact)).astype(np.int32)
    large = np.minimum(large, REL_BUCKETS - 1)
    return np.where(d < exact, d, large).astype(np.int32)


def _swa_build_bias(bucket_ref, rb_ref, bias_ref, n_heads):
    bk = bucket_ref[...]
    for h in range(n_heads):
        acc = jnp.zeros(bk.shape, F32)
        for b in range(REL_BUCKETS):
            acc = jnp.where(bk == b, rb_ref[b, h], acc)
        bias_ref[h] = acc


def _swa_valid(i):
    row = lax.broadcasted_iota(jnp.int32, (SWA_BLOCK, 2 * SWA_BLOCK), 0)
    col = lax.broadcasted_iota(jnp.int32, (SWA_BLOCK, 2 * SWA_BLOCK), 1)
    dist = row + SWA_BLOCK - col
    return (dist >= 0) & (dist < SWA_BLOCK) & ((col >= SWA_BLOCK) | (i > 0))


def _swa_place(blk, h, group, sel):
    if (h % 2) != group:
        blk = pltpu.roll(blk.astype(F32), HEAD_DIM, axis=1).astype(BF16)
    return jnp.where(sel, blk, jnp.zeros_like(blk))


def _swa_probs(q_pos, kcat, bias_h, valid, sink):
    logits = _dot_nt(q_pos, kcat) * SCALE + bias_h
    logits = jnp.where(valid, logits, -jnp.inf)
    m = jnp.maximum(jnp.max(logits, axis=1, keepdims=True), sink)
    p = jnp.exp(logits - m)
    es = jnp.exp(sink - m)
    denom = _rowsum(p) + es
    return p / denom, es / denom


def _swa_specs(n_heads, qcol, kcol, vcol):
    width = n_heads * HEAD_DIM
    prev = lambda col: pl.BlockSpec((SWA_BLOCK, LANES), lambda i: (jnp.maximum(i - 1, 0), col))
    cur = lambda col: pl.BlockSpec((SWA_BLOCK, LANES), lambda i: (i, col))
    return [pl.BlockSpec((SWA_BLOCK, width), lambda i: (i, qcol)),
            prev(kcol), cur(kcol), prev(vcol), cur(vcol),
            _full((SWA_BLOCK, 2 * SWA_BLOCK)),
            pl.BlockSpec(memory_space=pltpu.SMEM), pl.BlockSpec(memory_space=pltpu.SMEM)]


def _swa_fwd(proj, bucket, rel_bias, sinks, n_heads, qcol, kcol, vcol):
    s = proj.shape[0]
    width = n_heads * HEAD_DIM
    n_groups = max(1, n_heads // 4)
    per_group = n_heads // n_groups

    def body(q_ref, kp_ref, kc_ref, vp_ref, vc_ref, bucket_ref, rb_ref, sk_ref, o_ref, bias_ref):
        i = pl.program_id(0)

        @pl.when(i == 0)
        def _():
            _swa_build_bias(bucket_ref, rb_ref, bias_ref, n_heads)

        lane = lax.broadcasted_iota(jnp.int32, (1, LANES), 1)
        first = lane < HEAD_DIM
        valid = _swa_valid(i)
        kcat = jnp.concatenate([kp_ref[...], kc_ref[...]], axis=0)
        vcat = jnp.concatenate([vp_ref[...], vc_ref[...]], axis=0)
        for j in range(n_heads // 2):
            halves = []
            for h in (2 * j, 2 * j + 1):
                group = h // per_group
                sel = first if group == 0 else jnp.logical_not(first)
                q_pos = _swa_place(q_ref[:, j * LANES:(j + 1) * LANES], h, group, sel)
                prob, _ = _swa_probs(q_pos, kcat, bias_ref[h], valid, sk_ref[0, h])
                o2 = _dot(prob.astype(BF16), vcat)
                if (h % 2) != group:
                    o2 = pltpu.roll(o2, HEAD_DIM, axis=1)
                halves.append(o2)
            o_ref[:, j * LANES:(j + 1) * LANES] = jnp.where(first, halves[0], halves[1])

    return pl.pallas_call(
        body, name="swa_fwd", grid=(s // SWA_BLOCK,),
        in_specs=_swa_specs(n_heads, qcol, kcol, vcol),
        out_specs=pl.BlockSpec((SWA_BLOCK, width), lambda i: (i, 0)),
        out_shape=jax.ShapeDtypeStruct((s, width), F32),
        scratch_shapes=[pltpu.VMEM((n_heads, SWA_BLOCK, 2 * SWA_BLOCK), F32)],
        compiler_params=_params(),
    )(proj, proj, proj, proj, proj, bucket, rel_bias, sinks)


def _rms_fwd(o, g):
    r = lax.rsqrt(jnp.mean(o * o, axis=-1, keepdims=True) + RMS_EPS)
    n = o * r
    return n, r, n * g


def _mix_ln1(sb_out, sw_out, x, g_in, b_in, sb_g, sw_g, w_out):
    s, d = x.shape
    wsb, wsw = sb_out.shape[1], sw_out.shape[1]
    tm = min(ROW_TILE, s)

    def body(sb_ref, sw_ref, x_ref, gi_ref, bi_ref, sbg_ref, swg_ref, w_ref, mg_ref, u_ref):
        _, _, m_sb = _rms_fwd(sb_ref[...], sbg_ref[...])
        _, _, m_sw = _rms_fwd(sw_ref[...], swg_ref[...])
        m_sb = m_sb.astype(BF16)
        m_sw = m_sw.astype(BF16)
        mg_ref[:, :wsb] = m_sb
        mg_ref[:, wsb:] = m_sw
        mix = _dot(m_sb, w_ref[:wsb, :]) + _dot(m_sw, w_ref[wsb:, :])
        xhat, _ = _ln_hat(x_ref[...])
        h0 = xhat * gi_ref[...] + bi_ref[...]
        u_ref[...] = ALPHA * h0 + mix

    row = lambda width: pl.BlockSpec((tm, width), lambda i: (i, 0))
    return pl.pallas_call(
        body, name="mix_ln1", grid=(s // tm,),
        in_specs=[row(wsb), row(wsw), row(d), _full((1, d)), _full((1, d)),
                  _full((1, wsb)), _full((1, wsw)), _full((wsb + wsw, d))],
        out_specs=[row(wsb + wsw), row(d)],
        out_shape=[jax.ShapeDtypeStruct((s, wsb + wsw), BF16), jax.ShapeDtypeStruct((s, d), F32)],
        compiler_params=_params(),
    )(sb_out, sw_out, x, g_in, b_in, sb_g, sw_g, w_out)


def _ffn_fwd(u1, g1, b1, w_gu, w_down, g2, b2, target):
    s, d = u1.shape
    dff = w_down.shape[0]
    tm = min(ROW_TILE, s)
    n_c = dff // FF_CHUNK

    def body(u_ref, g1_ref, b1_ref, wg_ref, wu_ref, wd_ref, g2_ref, b2_ref, t_ref,
             du_ref, st_ref, h1_ref, h1b_ref, acc_ref):
        i, c = pl.program_id(0), pl.program_id(1)

        @pl.when((i == 0) & (c == 0))
        def _():
            st_ref[...] = jnp.zeros_like(st_ref)

        @pl.when(c == 0)
        def _():
            xhat, _ = _ln_hat(u_ref[...])
            h1 = xhat * g1_ref[...] + b1_ref[...]
            h1_ref[...] = h1
            h1b_ref[...] = h1.astype(BF16)
            acc_ref[...] = jnp.zeros_like(acc_ref)

        h1b = h1b_ref[...]
        gate = _dot(h1b, wg_ref[...])
        up = _dot(h1b, wu_ref[...])
        act = gate * jax.nn.sigmoid(gate) * up
        acc_ref[...] += _dot(act.astype(BF16), wd_ref[...])

        @pl.when(c == n_c - 1)
        def _():
            u2 = ALPHA * h1_ref[...] + acc_ref[...]
            xhat2, r2 = _ln_hat(u2)
            diff = xhat2 * g2_ref[...] + b2_ref[...] - t_ref[...]
            dh2 = diff * (1.0 / d)
            st_ref[0:1, :] += _colsum(dh2 * xhat2)
            st_ref[1:2, :] += _colsum(dh2)
            st_ref[2:3, :] += jnp.broadcast_to(_colsum(_rowsum(diff * diff)) * (0.5 / d), (1, d))
            du_ref[...] = _ln_bwd(dh2 * g2_ref[...], xhat2, r2)

    row = pl.BlockSpec((tm, d), lambda i, c: (i, 0))
    vec = pl.BlockSpec((1, d), lambda i, c: (0, 0))
    return pl.pallas_call(
        body, name="ffn_fwd", grid=(s // tm, n_c),
        in_specs=[row, vec, vec,
                  pl.BlockSpec((d, FF_CHUNK), lambda i, c: (0, c)),
                  pl.BlockSpec((d, FF_CHUNK), lambda i, c: (0, n_c + c)),
                  pl.BlockSpec((FF_CHUNK, d), lambda i, c: (c, 0)),
                  vec, vec, row],
        out_specs=[row, pl.BlockSpec((8, d), lambda i, c: (0, 0))],
        out_shape=[jax.ShapeDtypeStruct((s, d), F32), jax.ShapeDtypeStruct((8, d), F32)],
        scratch_shapes=[pltpu.VMEM((tm, d), F32), pltpu.VMEM((tm, d), BF16), pltpu.VMEM((tm, d), F32)],
        compiler_params=_params(),
    )(u1, g1, b1, w_gu, w_gu, w_down, g2, b2, target)


def _ffn_bwd(u1, du2, g1, b1, w_gu, w_down):
    s, d = u1.shape
    dff = w_down.shape[0]
    tm = min(ROW_TILE, s)
    n_c = dff // FF_CHUNK

    def body(u_ref, du2_ref, g1_ref, b1_ref, wg_ref, wu_ref, wd_ref,
             du1_ref, h1b_ref, act_ref, dg_ref, dup_ref, st_ref, xhat_ref, rstd_ref, du2b_ref, acc_ref):
        i, c = pl.program_id(0), pl.program_id(1)

        @pl.when((i == 0) & (c == 0))
        def _():
            st_ref[...] = jnp.zeros_like(st_ref)

        @pl.when(c == 0)
        def _():
            xhat, r = _ln_hat(u_ref[...])
            xhat_ref[...] = xhat
            rstd_ref[...] = jnp.broadcast_to(r, rstd_ref.shape)
            h1b_ref[...] = (xhat * g1_ref[...] + b1_ref[...]).astype(BF16)
            du2b_ref[...] = du2_ref[...].astype(BF16)
            acc_ref[...] = jnp.zeros_like(acc_ref)

        h1b = h1b_ref[...]
        gate = _dot(h1b, wg_ref[...])
        up = _dot(h1b, wu_ref[...])
        sg = jax.nn.sigmoid(gate)
        silu = gate * sg
        dact = _dot_nt(du2b_ref[...], wd_ref[...])
        dup = (dact * silu).astype(BF16)
        dgate = (dact * up * (sg * (1.0 + gate * (1.0 - sg)))).astype(BF16)
        act_ref[...] = (silu * up).astype(BF16)
        dg_ref[...] = dgate
        dup_ref[...] = dup
        acc_ref[...] += _dot_nt(dgate, wg_ref[...]) + _dot_nt(dup, wu_ref[...])

        @pl.when(c == n_c - 1)
        def _():
            dh1 = acc_ref[...] + ALPHA * du2_ref[...]
            xhat = xhat_ref[...]
            st_ref[0:1, :] += _colsum(dh1 * xhat)
            st_ref[1:2, :] += _colsum(dh1)
            du1_ref[...] = _ln_bwd(dh1 * g1_ref[...], xhat, rstd_ref[:, 0:1])

    row = pl.BlockSpec((tm, d), lambda i, c: (i, 0))
    vec = pl.BlockSpec((1, d), lambda i, c: (0, 0))
    chunk = pl.BlockSpec((tm, FF_CHUNK), lambda i, c: (i, c))
    return pl.pallas_call(
        body, name="ffn_bwd", grid=(s // tm, n_c),
        in_specs=[row, row, vec, vec,
                  pl.BlockSpec((d, FF_CHUNK), lambda i, c: (0, c)),
                  pl.BlockSpec((d, FF_CHUNK), lambda i, c: (0, n_c + c)),
                  pl.BlockSpec((FF_CHUNK, d), lambda i, c: (c, 0))],
        out_specs=[row, row, chunk, chunk, chunk, pl.BlockSpec((8, d), lambda i, c: (0, 0))],
        out_shape=[jax.ShapeDtypeStruct((s, d), F32), jax.ShapeDtypeStruct((s, d), BF16),
                   jax.ShapeDtypeStruct((s, dff), BF16), jax.ShapeDtypeStruct((s, dff), BF16),
                   jax.ShapeDtypeStruct((s, dff), BF16), jax.ShapeDtypeStruct((8, d), F32)],
        scratch_shapes=[pltpu.VMEM((tm, d), F32), pltpu.VMEM((tm, LANES), F32),
                        pltpu.VMEM((tm, d), BF16), pltpu.VMEM((tm, d), F32)],
        compiler_params=_params(),
    )(u1, du2, g1, b1, w_gu, w_gu, w_down)


def _rms_bwd(dm, o, g):
    n, r, _ = _rms_fwd(o, g)
    dn = dm * g
    return r * (dn - n * jnp.mean(dn * n, axis=-1, keepdims=True)), _colsum(dm * n)


def _mix_bwd(du1, w_out, sb_out, sw_out, sb_g, sw_g):
    s, d = du1.shape
    wsb, wsw = sb_out.shape[1], sw_out.shape[1]
    tm = min(ROW_TILE, s)

    def body(du_ref, w_ref, sb_ref, sw_ref, sbg_ref, swg_ref, dsb_ref, dsw_ref, st_ref):
        i = pl.program_id(0)

        @pl.when(i == 0)
        def _():
            st_ref[...] = jnp.zeros_like(st_ref)

        dmerged = _dot_nt(du_ref[...].astype(BF16), w_ref[...])
        dsb, gsb = _rms_bwd(dmerged[:, :wsb], sb_ref[...], sbg_ref[...])
        dsw, gsw = _rms_bwd(dmerged[:, wsb:], sw_ref[...], swg_ref[...])
        dsb_ref[...] = dsb.astype(BF16)
        dsw_ref[...] = dsw.astype(BF16)
        st_ref[0:1, :wsb] += gsb
        st_ref[0:1, wsb:] += gsw

    row = lambda width: pl.BlockSpec((tm, width), lambda i: (i, 0))
    return pl.pallas_call(
        body, name="mix_bwd", grid=(s // tm,),
        in_specs=[row(d), _full((wsb + wsw, d)), row(wsb), row(wsw), _full((1, wsb)), _full((1, wsw))],
        out_specs=[row(wsb), row(wsw), _full((8, wsb + wsw))],
        out_shape=[jax.ShapeDtypeStruct((s, wsb), BF16), jax.ShapeDtypeStruct((s, wsw), BF16),
                   jax.ShapeDtypeStruct((8, wsb + wsw), F32)],
        compiler_params=_params(),
    )(du1, w_out, sb_out, sw_out, sb_g, sw_g)


def _sb_bwd(proj, dout, out, n_pairs):
    s = proj.shape[0]
    t = min(SB_TILE, s)
    nq = s // t
    width = n_pairs * LANES

    def body(q_ref, k_ref, v_ref, do_ref, o_ref, dq_ref, dk_ref, dv_ref):
        i = pl.program_id(1)

        @pl.when(i == 0)
        def _():
            dk_ref[...] = jnp.zeros_like(dk_ref)
            dv_ref[...] = jnp.zeros_like(dv_ref)

        lane = lax.broadcasted_iota(jnp.int32, (1, LANES), 1)
        first = lane < HEAD_DIM
        q2 = q_ref[...]
        do2 = do_ref[...]
        zero = jnp.zeros_like(q2)
        qs = (jnp.where(first, q2, zero), jnp.where(first, zero, q2))
        dos = (jnp.where(first, do2, zero), jnp.where(first, zero, do2))
        prod = do2.astype(F32) * o_ref[...]
        totals = (_rowsum(jnp.where(first, prod, 0.0)), _rowsum(jnp.where(first, 0.0, prod)))
        row, col = _sb_tile_consts(t)
        upper = (row > col).astype(BF16)
        incl = (row >= col).astype(BF16)
        causal = col < row

        def visit(off, carry, mask):
            k_t = k_ref[pl.ds(off, t), :]
            v_t = v_ref[pl.ds(off, t), :]
            out = []
            dk_t = jnp.zeros((t, LANES), F32)
            dv_t = jnp.zeros((t, LANES), F32)
            for hh in range(2):
                c_l, c_e, dq = carry[hh]
                lb, l1, a = _sb_scores(qs[hh], k_t, upper, c_l, mask)
                a_b = a.astype(BF16)
                d_e = _dot_nt(dos[hh], v_t) * a_b.astype(F32)
                d_l = totals[hh] - (_dot(d_e.astype(BF16), incl) + c_e)
                dz = d_e - jnp.exp(lb) * (d_e + d_l)
                if mask is not None:
                    dz = jnp.where(mask, dz, 0.0)
                dzb = (dz * SCALE).astype(BF16)
                dk_t += _dot_tn(dzb, qs[hh])
                dv_t += _dot_tn(a_b, dos[hh])
                out.append((c_l + _rowsum(l1), c_e + _rowsum(d_e), dq + _dot(dzb, k_t)))
            dk_ref[pl.ds(off, t), :] += dk_t
            dv_ref[pl.ds(off, t), :] += dv_t
            return tuple(out)

        init = tuple((jnp.zeros((t, 1), F32), jnp.zeros((t, 1), F32), jnp.zeros((t, LANES), F32))
                     for _ in range(2))
        carry = visit(pl.multiple_of(i * t, t), init, causal)

        def step(j, carry):
            return visit(pl.multiple_of((i - 1 - j) * t, t), carry, None)

        carry = lax.fori_loop(0, i, step, carry)
        dq_ref[...] = jnp.where(first, carry[0][2], carry[1][2]).astype(BF16)

    qblk = pl.BlockSpec((t, LANES), lambda h, i: (i, h))
    whole = pl.BlockSpec((s, LANES), lambda h, i: (0, h))
    return pl.pallas_call(
        body, name="sb_bwd", grid=(n_pairs, nq),
        in_specs=[qblk,
                  pl.BlockSpec((s, LANES), lambda h, i: (0, n_pairs + h)),
                  pl.BlockSpec((s, LANES), lambda h, i: (0, 2 * n_pairs + h)),
                  qblk, qblk],
        out_specs=[qblk, whole, whole],
        out_shape=[jax.ShapeDtypeStruct((s, width), BF16), jax.ShapeDtypeStruct((s, width), F32),
                   jax.ShapeDtypeStruct((s, width), F32)],
        compiler_params=_params(),
    )(proj, proj, proj, dout, out)


def _swa_bwd(proj, dout, bucket, rel_bias, sinks, n_heads, qcol, kcol, vcol):
    s = proj.shape[0]
    width = n_heads * HEAD_DIM
    n_groups = max(1, n_heads // 4)
    per_group = n_heads // n_groups
    nb = s // SWA_BLOCK

    def body(q_ref, kp_ref, kc_ref, vp_ref, vc_ref, bucket_ref, rb_ref, sk_ref, do_ref,
             dq_ref, dk_ref, dv_ref, dsk_ref, drb_ref, bias_ref, dbias_ref):
        i = pl.program_id(0)

        @pl.when(i == 0)
        def _():
            _swa_build_bias(bucket_ref, rb_ref, bias_ref, n_heads)
            dbias_ref[...] = jnp.zeros_like(dbias_ref)
            dk_ref[...] = jnp.zeros_like(dk_ref)
            dv_ref[...] = jnp.zeros_like(dv_ref)
            dsk_ref[...] = jnp.zeros_like(dsk_ref)

        lane = lax.broadcasted_iota(jnp.int32, (1, LANES), 1)
        first = lane < HEAD_DIM
        valid = _swa_valid(i)
        kcat = jnp.concatenate([kp_ref[...], kc_ref[...]], axis=0)
        vcat = jnp.concatenate([vp_ref[...], vc_ref[...]], axis=0)
        dkcat = jnp.zeros((2 * SWA_BLOCK, LANES), F32)
        dvcat = jnp.zeros((2 * SWA_BLOCK, LANES), F32)
        for j in range(n_heads // 2):
            halves = []
            for h in (2 * j, 2 * j + 1):
                group = h // per_group
                sel = first if group == 0 else jnp.logical_not(first)
                q_pos = _swa_place(q_ref[:, j * LANES:(j + 1) * LANES], h, group, sel)
                do_pos = _swa_place(do_ref[:, j * LANES:(j + 1) * LANES], h, group, sel)
                prob, p_sink = _swa_probs(q_pos, kcat, bias_ref[h], valid, sk_ref[0, h])
                dprob = _dot_nt(do_pos, vcat)
                delta = _rowsum(prob * dprob)
                dlog = prob * (dprob - delta)
                dsk_ref[h:h + 1, :] += jnp.broadcast_to(-_colsum(p_sink * delta), (1, LANES))
                dbias_ref[h] += dlog
                dlb = (dlog * SCALE).astype(BF16)
                dq2 = _dot(dlb, kcat)
                if (h % 2) != group:
                    dq2 = pltpu.roll(dq2, HEAD_DIM, axis=1)
                halves.append(dq2)
                dkcat += _dot_tn(dlb, q_pos)
                dvcat += _dot_tn(prob.astype(BF16), do_pos)
            dq_ref[:, j * LANES:(j + 1) * LANES] = jnp.where(first, halves[0], halves[1]).astype(BF16)

        cur = pl.multiple_of(i * SWA_BLOCK, SWA_BLOCK)
        dk_ref[pl.ds(cur, SWA_BLOCK), :] += dkcat[SWA_BLOCK:, :]
        dv_ref[pl.ds(cur, SWA_BLOCK), :] += dvcat[SWA_BLOCK:, :]

        @pl.when(i > 0)
        def _():
            prv = pl.multiple_of((i - 1) * SWA_BLOCK, SWA_BLOCK)
            dk_ref[pl.ds(prv, SWA_BLOCK), :] += dkcat[:SWA_BLOCK, :]
            dv_ref[pl.ds(prv, SWA_BLOCK), :] += dvcat[:SWA_BLOCK, :]

        @pl.when(i == nb - 1)
        def _():
            bk = bucket_ref[...]
            rowi = lax.broadcasted_iota(jnp.int32, (REL_BUCKETS, LANES), 0)
            coli = lax.broadcasted_iota(jnp.int32, (REL_BUCKETS, LANES), 1)
            res = jnp.zeros((REL_BUCKETS, LANES), F32)
            for h in range(n_heads):
                db = dbias_ref[h]
                for b in range(REL_BUCKETS):
                    tot = _colsum(_rowsum(jnp.where(bk == b, db, 0.0)))
                    res = jnp.where((rowi == b) & (coli == h), tot, res)
            drb_ref[...] = res

    in_specs = _swa_specs(n_heads, qcol, kcol, vcol) + [pl.BlockSpec((SWA_BLOCK, width), lambda i: (i, 0))]
    return pl.pallas_call(
        body, name="swa_bwd", grid=(nb,),
        in_specs=in_specs,
        out_specs=[pl.BlockSpec((SWA_BLOCK, width), lambda i: (i, 0)),
                   _full((s, LANES)), _full((s, LANES)), _full((8, LANES)), _full((REL_BUCKETS, LANES))],
        out_shape=[jax.ShapeDtypeStruct((s, width), BF16), jax.ShapeDtypeStruct((s, LANES), F32),
                   jax.ShapeDtypeStruct((s, LANES), F32), jax.ShapeDtypeStruct((8, LANES), F32),
                   jax.ShapeDtypeStruct((REL_BUCKETS, LANES), F32)],
        scratch_shapes=[pltpu.VMEM((n_heads, SWA_BLOCK, 2 * SWA_BLOCK), F32),
                        pltpu.VMEM((n_heads, SWA_BLOCK, 2 * SWA_BLOCK), F32)],
        compiler_params=_params(),
    )(proj, proj, proj, proj, proj, bucket, rel_bias, sinks, dout)


def _proj_bwd(dproj, w_in, du1, x, g_in):
    s, d = x.shape
    cols = w_in.shape[1]
    tm = min(ROW_TILE, s)

    def body(dp_ref, w_ref, du_ref, x_ref, g_ref, dx_ref, st_ref):
        i = pl.program_id(0)

        @pl.when(i == 0)
        def _():
            st_ref[...] = jnp.zeros_like(st_ref)

        dh0 = _dot_nt(dp_ref[...], w_ref[...]) + ALPHA * du_ref[...]
        xhat, r = _ln_hat(x_ref[...])
        st_ref[0:1, :] += _colsum(dh0 * xhat)
        st_ref[1:2, :] += _colsum(dh0)
        dx_ref[...] = _ln_bwd(dh0 * g_ref[...], xhat, r)

    row = lambda width: pl.BlockSpec((tm, width), lambda i: (i, 0))
    return pl.pallas_call(
        body, name="proj_bwd", grid=(s // tm,),
        in_specs=[row(cols), _full((d, cols)), row(d), row(d), _full((1, d))],
        out_specs=[row(d), _full((8, d))],
        out_shape=[jax.ShapeDtypeStruct((s, d), F32), jax.ShapeDtypeStruct((8, d), F32)],
        compiler_params=_params(),
    )(dproj, w_in, du1, x, g_in)


def _wgrad(name, a, b, tm, tn):
    s, m = a.shape
    n = b.shape[1]
    ts = min(ROW_TILE, s)

    def body(a_ref, b_ref, o_ref):
        @pl.when(pl.program_id(2) == 0)
        def _():
            o_ref[...] = jnp.zeros_like(o_ref)

        o_ref[...] += _dot_tn(a_ref[...].astype(BF16), b_ref[...].astype(BF16))

    return pl.pallas_call(
        body, name=name, grid=(m // tm, n // tn, s // ts),
        in_specs=[pl.BlockSpec((ts, tm), lambda i, j, k: (k, i)),
                  pl.BlockSpec((ts, tn), lambda i, j, k: (k, j))],
        out_specs=pl.BlockSpec((tm, tn), lambda i, j, k: (i, j)),
        out_shape=jax.ShapeDtypeStruct((m, n), F32),
        compiler_params=_params(),
    )(a, b)


def _adamw_math(w, g, m, v):
    m = ADAM_B1 * m + (1.0 - ADAM_B1) * g
    v = ADAM_B2 * v + (1.0 - ADAM_B2) * (g * g)
    m_hat = m / (1.0 - ADAM_B1 ** ADAM_STEP)
    v_hat = v / (1.0 - ADAM_B2 ** ADAM_STEP)
    delta = -ADAM_LR * (m_hat / (jnp.sqrt(v_hat) + ADAM_EPS) + ADAM_WD * w)
    return delta, m, v


def _adamw(name, landed, w, m, v, tr):
    rows, cols = w.shape

    def body(l_ref, w_ref, m_ref, v_ref, g_ref, d_ref, nm_ref, nv_ref):
        g = l_ref[0]
        for src in range(1, N_DEV):
            g = g + l_ref[src]
        delta, nm, nv = _adamw_math(w_ref[...], g, m_ref[...], v_ref[...])
        g_ref[...] = g
        d_ref[...] = delta
        nm_ref[...] = nm
        nv_ref[...] = nv

    blk = pl.BlockSpec((tr, cols), lambda i: (i, 0))
    shape = jax.ShapeDtypeStruct((rows, cols), F32)
    return pl.pallas_call(
        body, name=name, grid=(rows // tr,),
        in_specs=[pl.BlockSpec((N_DEV, tr, cols), lambda i: (0, i, 0)), blk, blk, blk],
        out_specs=[blk, blk, blk, blk],
        out_shape=[shape, shape, shape, shape],
        compiler_params=_params(),
    )(landed, w, m, v)


def _pack(d, ln_in_g, ln_in_b, ln1_g, ln1_b, ln2_g, ln2_b, sb_g, sw_g, rel_bias, sinks, extra=None):
    tail = [rel_bias.reshape(-1), sinks.reshape(-1)]
    if extra is not None:
        tail.append(extra.reshape(-1))
    tail = jnp.concatenate(tail)
    tail = jnp.concatenate([tail, jnp.zeros((d - tail.shape[0],), F32)])
    rows = [ln_in_g.reshape(-1), ln_in_b.reshape(-1), ln1_g.reshape(-1), ln1_b.reshape(-1),
            ln2_g.reshape(-1), ln2_b.reshape(-1),
            jnp.concatenate([sb_g.reshape(-1), sw_g.reshape(-1)]), tail]
    return jnp.stack(rows)


def _unpack(p, wsb, n_rb, n_sk):
    return [p[0], p[1], p[6, :wsb][None], p[6, wsb:][None], p[7, n_rb:n_rb + n_sk][None],
            p[7, :n_rb].reshape(REL_BUCKETS, -1), p[2][None], p[3][None], p[4][None], p[5][None]]


def kernel(x, ln_in_g, ln_in_b, w_in, sb_norm_g, swa_norm_g, sinks, rel_bias, w_out, ln1_g, ln1_b, w_gate_up, w_down, ln2_g, ln2_b, loss_target, m_ln_in_g, m_ln_in_b, m_w_in, m_sb_norm_g, m_swa_norm_g, m_sinks, m_rel_bias, m_w_out, m_ln1_g, m_ln1_b, m_w_gate_up, m_w_down, m_ln2_g, m_ln2_b, v_ln_in_g, v_ln_in_b, v_w_in, v_sb_norm_g, v_swa_norm_g, v_sinks, v_rel_bias, v_w_out, v_ln1_g, v_ln1_b, v_w_gate_up, v_w_down, v_ln2_g, v_ln2_b):
    x2 = x[0]
    tgt = loss_target[0]
    s, d = x2.shape
    wsb = sb_norm_g.shape[-1]
    wsw = swa_norm_g.shape[-1]
    n_sw_heads = sinks.shape[-1]
    n_pairs = wsb // LANES
    dff = w_down.shape[1] * N_DEV
    assert wsb % LANES == 0 and wsw % LANES == 0 and n_sw_heads * HEAD_DIM == wsw
    assert 3 * wsb % wsw == 0 and dff % FF_CHUNK == 0 and s % SWA_BLOCK == 0
    qcol = 3 * wsb // wsw
    kcol = (3 * wsb + wsw) // LANES
    vcol = kcol + 1
    assert w_in.shape[-1] * N_DEV == (vcol + 1) * LANES

    big_w = [w_in[0], w_out[0], w_gate_up[0], w_down[0]]
    big_m = [m_w_in[0], m_w_out[0], m_w_gate_up[0], m_w_down[0]]
    big_v = [v_w_in[0], v_w_out[0], v_w_gate_up[0], v_w_down[0]]

    gathered = _exchange("weights_allgather", [w.astype(BF16) for w in big_w], ["gather"] * 4)
    cat_cols = lambda g: jnp.transpose(g, (1, 0, 2)).reshape(g.shape[1], N_DEV * g.shape[2])
    cat_rows = lambda g: g.reshape(N_DEV * g.shape[1], g.shape[2])
    w_in_f, w_out_f, w_gu_f, w_down_f = cat_cols(gathered[0]), cat_rows(gathered[1]), cat_cols(gathered[2]), cat_rows(gathered[3])

    vec = lambda a: a.reshape(1, -1)
    g_in, b_in = vec(ln_in_g), vec(ln_in_b)
    bucket = jnp.asarray(_swa_bucket_table())

    h0b, proj = _ln_proj(x2, g_in, b_in, w_in_f)
    sb_out = _sb_fwd(proj, n_pairs)
    sw_out = _swa_fwd(proj, bucket, rel_bias, sinks, n_sw_heads, qcol, kcol, vcol)
    merged, u1 = _mix_ln1(sb_out, sw_out, x2, g_in, b_in, sb_norm_g, swa_norm_g, w_out_f)
    du2, st_ln2 = _ffn_fwd(u1, ln1_g, ln1_b, w_gu_f, w_down_f, ln2_g, ln2_b, tgt)

    du1, h1b, act, dgate, dup, st_ln1 = _ffn_bwd(u1, du2, ln1_g, ln1_b, w_gu_f, w_down_f)
    dsb, dsw, st_rms = _mix_bwd(du1, w_out_f, sb_out, sw_out, sb_norm_g, swa_norm_g)
    dq_sb, dk_sb, dv_sb = _sb_bwd(proj, dsb, sb_out, n_pairs)
    dq_sw, dk_sw, dv_sw, st_sink, st_rb = _swa_bwd(proj, dsw, bucket, rel_bias, sinks, n_sw_heads, qcol, kcol, vcol)
    dproj = jnp.concatenate([dq_sb, dk_sb.astype(BF16), dv_sb.astype(BF16), dq_sw,
                             dk_sw.astype(BF16), dv_sw.astype(BF16)], axis=1)
    grad_x, st_in = _proj_bwd(dproj, w_in_f, du1, x2, g_in)

    tile_m = min(512, d)
    gw_in = _wgrad("wgrad_in", h0b, dproj, tile_m, dproj.shape[1] // 2)
    gw_out = _wgrad("wgrad_out", merged, du1, tile_m, d)
    gw_gate = _wgrad("wgrad_gate", h1b, dgate, tile_m, dff // 2)
    gw_up = _wgrad("wgrad_up", h1b, dup, tile_m, dff // 2)
    gw_down = _wgrad("wgrad_down", act, du2, dff // 2, d)

    split_cols = lambda g: jnp.transpose(g.reshape(g.shape[0], N_DEV, g.shape[1] // N_DEV), (1, 0, 2))
    split_rows = lambda g: g.reshape(N_DEV, g.shape[0] // N_DEV, g.shape[1])
    n_rb = rel_bias.size
    small = _pack(d, st_in[0], st_in[1], st_ln1[0], st_ln1[1], st_ln2[0], st_ln2[1],
                  st_rms[0, :wsb], st_rms[0, wsb:], st_rb[:, :n_sw_heads], st_sink[:n_sw_heads, 0],
                  extra=st_ln2[2, 0:1])
    landed = _exchange(
        "grads_exchange",
        [split_cols(gw_in), split_rows(gw_out), split_cols(jnp.concatenate([gw_gate, gw_up], axis=1)),
         split_rows(gw_down), small],
        ["scatter"] * 4 + ["gather"])

    big = []
    for name, land, w, m, v in zip(["adamw_in", "adamw_out", "adamw_gate_up", "adamw_down"], landed[:4], big_w, big_m, big_v):
        rows = w.shape[0]
        tr = 128 if rows % 128 == 0 else rows // 4
        big.append(_adamw(name, land, w, m, v, tr))

    small_w = _pack(d, ln_in_g, ln_in_b, ln1_g, ln1_b, ln2_g, ln2_b, sb_norm_g, swa_norm_g, rel_bias, sinks)
    small_m = _pack(d, m_ln_in_g, m_ln_in_b, m_ln1_g, m_ln1_b, m_ln2_g, m_ln2_b, m_sb_norm_g, m_swa_norm_g, m_rel_bias, m_sinks)
    small_v = _pack(d, v_ln_in_g, v_ln_in_b, v_ln1_g, v_ln1_b, v_ln2_g, v_ln2_b, v_sb_norm_g, v_swa_norm_g, v_rel_bias, v_sinks)
    sg, sd, sm, sv = _adamw("adamw_small", landed[4], small_w, small_m, small_v, 8)
    n_sk = sinks.size
    loss = sg[7, n_rb + n_sk]

    def leaves(idx):
        sm_l = _unpack([sg, sd, sm, sv][idx], wsb, n_rb, n_sk)
        bg = [b[idx][None] for b in big]
        return [sm_l[0], sm_l[1], bg[0], sm_l[2], sm_l[3], sm_l[4], sm_l[5], bg[1], sm_l[6], sm_l[7], bg[2], bg[3], sm_l[8], sm_l[9]]

    return (loss, grad_x[None], *leaves(0), *leaves(1), *leaves(2), *leaves(3))
```

```python
import functools
import math

import numpy as np
import jax
import jax.numpy as jnp
from jax import lax
from jax.experimental import pallas as pl
from jax.experimental.pallas import tpu as pltpu

F32 = jnp.float32
BF16 = jnp.bfloat16
MESH = pl.DeviceIdType.MESH

N_DEV = 8
LANES = 128
HEAD_DIM = 64
SCALE = HEAD_DIM ** -0.5
SWA_BLOCK = 128
REL_BUCKETS = 32
REL_MAX_DIST = 128
ALPHA = 2.0 ** 0.25
LN_EPS = 1e-5
RMS_EPS = 1e-6
ADAM_LR = 0.001
ADAM_B1 = 0.9
ADAM_B2 = 0.999
ADAM_EPS = 1e-08
ADAM_WD = 0.01
ADAM_STEP = 10

ROW_TILE = 512
SB_TILE = 256
FF_CHUNK = 256
SB_UNDERFLOW = -110.0
MIB = 1024 * 1024


def _params(vmem_mib=48):
    return pltpu.CompilerParams(vmem_limit_bytes=vmem_mib * MIB)


def _dot(a, b):
    return jnp.dot(a, b, preferred_element_type=F32)


def _dot_nt(a, b):
    return lax.dot_general(a, b, (((1,), (1,)), ((), ())), preferred_element_type=F32)


def _dot_tn(a, b):
    return lax.dot_general(a, b, (((0,), (0,)), ((), ())), preferred_element_type=F32)


def _ln_hat(x):
    mu = jnp.mean(x, axis=-1, keepdims=True)
    xc = x - mu
    var = jnp.mean(xc * xc, axis=-1, keepdims=True)
    r = lax.rsqrt(var + LN_EPS)
    return xc * r, r


def _ln_bwd(dxhat, xhat, r):
    return r * (dxhat - jnp.mean(dxhat, axis=-1, keepdims=True)
                - xhat * jnp.mean(dxhat * xhat, axis=-1, keepdims=True))


def _colsum(a):
    return jnp.sum(a, axis=0, keepdims=True)


def _rowsum(a):
    return jnp.sum(a, axis=1, keepdims=True)


def _full(shape):
    return pl.BlockSpec(shape, lambda *_: (0,) * len(shape))


def _exchange(name, arrays, kinds):
    n = len(arrays)
    out_shapes = []
    for a, kind in zip(arrays, kinds):
        blk = a.shape if kind == "gather" else a.shape[1:]
        out_shapes.append(jax.ShapeDtypeStruct((N_DEV,) + tuple(blk), a.dtype))

    def body(*refs):
        ins, outs = refs[:n], refs[n:2 * n]
        send_sems, recv_sems, local_sems = refs[2 * n:]
        x, y, c = lax.axis_index("x"), lax.axis_index("y"), lax.axis_index("c")
        me = 4 * x + 2 * y + c

        def peer_of(k):
            px = 1 - x if (k >> 2) & 1 else x
            py = 1 - y if (k >> 1) & 1 else y
            pc = 1 - c if k & 1 else c
            return (px, py, pc), 4 * px + 2 * py + pc

        def src_for(t, dev_lin):
            return ins[t] if kinds[t] == "gather" else ins[t].at[dev_lin]

        local = []
        for t in range(n):
            cp = pltpu.make_async_copy(src_for(t, me), outs[t].at[me], local_sems.at[t])
            cp.start()
            local.append(cp)
        sends = []
        for k in range(1, N_DEV):
            peer, peer_lin = peer_of(k)
            for t in range(n):
                cp = pltpu.make_async_remote_copy(
                    src_ref=src_for(t, peer_lin), dst_ref=outs[t].at[me],
                    send_sem=send_sems.at[t, k - 1], recv_sem=recv_sems.at[t, k - 1],
                    device_id=peer, device_id_type=MESH)
                cp.start()
                sends.append(cp)
        for k in range(1, N_DEV):
            peer, peer_lin = peer_of(k)
            for t in range(n):
                pltpu.make_async_remote_copy(
                    src_ref=src_for(t, peer_lin), dst_ref=outs[t].at[peer_lin],
                    send_sem=send_sems.at[t, k - 1], recv_sem=recv_sems.at[t, k - 1],
                    device_id=peer, device_id_type=MESH).wait_recv()
        for cp in sends:
            cp.wait_send()
        for cp in local:
            cp.wait()

    any_spec = pl.BlockSpec(memory_space=pl.ANY)
    return pl.pallas_call(
        body, name=name, out_shape=out_shapes,
        in_specs=[any_spec] * n, out_specs=[any_spec] * n,
        scratch_shapes=[pltpu.SemaphoreType.DMA((n, N_DEV - 1)),
                        pltpu.SemaphoreType.DMA((n, N_DEV - 1)),
                        pltpu.SemaphoreType.DMA((n,))],
    )(*arrays)


def _ln_proj(x, g, b, w_in):
    s, d = x.shape
    cols = w_in.shape[1]
    tm = min(ROW_TILE, s)

    def body(x_ref, g_ref, b_ref, w_ref, h_ref, p_ref):
        xhat, _ = _ln_hat(x_ref[...])
        h = (xhat * g_ref[...] + b_ref[...]).astype(BF16)
        h_ref[...] = h
        p_ref[...] = _dot(h, w_ref[...]).astype(BF16)

    row = lambda width: pl.BlockSpec((tm, width), lambda i: (i, 0))
    return pl.pallas_call(
        body, name="ln_proj", grid=(s // tm,),
        in_specs=[row(d), _full((1, d)), _full((1, d)), _full((d, cols))],
        out_specs=[row(d), row(cols)],
        out_shape=[jax.ShapeDtypeStruct((s, d), BF16), jax.ShapeDtypeStruct((s, cols), BF16)],
        compiler_params=_params(),
    )(x, g, b, w_in)


def _sb_tile_consts(t):
    row = lax.broadcasted_iota(jnp.int32, (t, t), 0)
    col = lax.broadcasted_iota(jnp.int32, (t, t), 1)
    return row, col


def _sb_scores(qh, k_t, upper, carry_l, causal):
    z = _dot_nt(qh, k_t) * SCALE
    sp = jnp.log(1.0 + jnp.exp(-jnp.abs(z)))
    lb = jnp.minimum(z, 0.0) - sp
    l1 = -(jnp.maximum(z, 0.0) + sp)
    if causal is not None:
        l1 = jnp.where(causal, l1, 0.0)
    hi = l1.astype(BF16)
    lo = (l1 - hi.astype(F32)).astype(BF16)
    suf = _dot(hi, upper) + _dot(lo, upper) + carry_l
    a = jnp.exp(lb + suf)
    if causal is not None:
        a = jnp.where(causal, a, 0.0)
    return lb, l1, a


def _sb_walk(i, t, visit, init, causal):
    def alive(carry):
        return jnp.max(jnp.maximum(carry[0][0], carry[1][0])) > SB_UNDERFLOW

    carry = visit(pl.multiple_of(i * t, t), init, causal)

    def cond(state):
        j, go, _ = state
        return (j < i) & go

    def body(state):
        j, _, carry = state
        carry = visit(pl.multiple_of((i - 1 - j) * t, t), carry, None)
        return j + 1, alive(carry), carry

    return lax.while_loop(cond, body, (jnp.int32(0), alive(carry), carry))[2]


def _sb_fwd(proj, n_pairs):
    s = proj.shape[0]
    t = min(SB_TILE, s)
    nq = s // t

    def body(q_ref, k_ref, v_ref, o_ref):
        i = pl.program_id(1)
        lane = lax.broadcasted_iota(jnp.int32, (1, LANES), 1)
        first = lane < HEAD_DIM
        q2 = q_ref[...]
        zero = jnp.zeros_like(q2)
        qs = (jnp.where(first, q2, zero), jnp.where(first, zero, q2))
        row, col = _sb_tile_consts(t)
        upper = (row > col).astype(BF16)
        causal = col < row

        def visit(off, carry, mask):
            k_t = k_ref[pl.ds(off, t), :]
            v_t = v_ref[pl.ds(off, t), :]
            out = []
            for hh in range(2):
                c_l, acc = carry[hh]
                _, l1, a = _sb_scores(qs[hh], k_t, upper, c_l, mask)
                out.append((c_l + _rowsum(l1), acc + _dot(a.astype(BF16), v_t)))
            return tuple(out)

        init = tuple((jnp.zeros((t, 1), F32), jnp.zeros((t, LANES), F32)) for _ in range(2))
        carry = _sb_walk(i, t, visit, init, causal)
        o_ref[...] = jnp.where(first, carry[0][1], carry[1][1])

    return pl.pallas_call(
        body, name="sb_fwd", grid=(n_pairs, nq),
        in_specs=[pl.BlockSpec((t, LANES), lambda h, i: (i, h)),
                  pl.BlockSpec((s, LANES), lambda h, i: (0, n_pairs + h)),
                  pl.BlockSpec((s, LANES), lambda h, i: (0, 2 * n_pairs + h))],
        out_specs=pl.BlockSpec((t, LANES), lambda h, i: (i, h)),
        out_shape=jax.ShapeDtypeStruct((s, n_pairs * LANES), F32),
        compiler_params=_params(),
    )(proj, proj, proj)


def _swa_bucket_table():
    qi = np.arange(SWA_BLOCK)[:, None]
    cj = np.arange(2 * SWA_BLOCK)[None, :]
    dist = qi + SWA_BLOCK - cj
    exact = REL_BUCKETS // 2
    d = np.maximum(dist, 0)
    d_f = np.maximum(d, 1).astype(np.float32)
    large = exact + (np.log(d_f / np.float32(exact)) / np.float32(math.log(REL_MAX_DIST / exact))
                     * np.float32(REL_BUCKETS - exact)).astype(np.int32)
    large = np.minimum(large, REL_BUCKETS - 1)
    return np.where(d < exact, d, large).astype(np.int32)


def _swa_build_bias(bucket_ref, rb_ref, bias_ref, n_heads):
    bk = bucket_ref[...]
    for h in range(n_heads):
        acc = jnp.zeros(bk.shape, F32)
        for b in range(REL_BUCKETS):
            acc = jnp.where(bk == b, rb_ref[b, h], acc)
        bias_ref[h] = acc


def _swa_valid(i):
    row = lax.broadcasted_iota(jnp.int32, (SWA_BLOCK, 2 * SWA_BLOCK), 0)
    col = lax.broadcasted_iota(jnp.int32, (SWA_BLOCK, 2 * SWA_BLOCK), 1)
    dist = row + SWA_BLOCK - col
    return (dist >= 0) & (dist < SWA_BLOCK) & ((col >= SWA_BLOCK) | (i > 0))


def _swa_place(blk, h, group, sel):
    if (h % 2) != group:
        blk = pltpu.roll(blk.astype(F32), HEAD_DIM, axis=1).astype(BF16)
    return jnp.where(sel, blk, jnp.zeros_like(blk))


def _swa_probs(q_pos, kcat, bias_h, valid, sink):
    logits = _dot_nt(q_pos, kcat) * SCALE + bias_h
    logits = jnp.where(valid, logits, -jnp.inf)
    m = jnp.maximum(jnp.max(logits, axis=1, keepdims=True), sink)
    p = jnp.exp(logits - m)
    es = jnp.exp(sink - m)
    denom = _rowsum(p) + es
    return p / denom, es / denom


def _swa_specs(n_heads, qcol, kcol, vcol):
    width = n_heads * HEAD_DIM
    prev = lambda col: pl.BlockSpec((SWA_BLOCK, LANES), lambda i: (jnp.maximum(i - 1, 0), col))
    cur = lambda col: pl.BlockSpec((SWA_BLOCK, LANES), lambda i: (i, col))
    return [pl.BlockSpec((SWA_BLOCK, width), lambda i: (i, qcol)),
            prev(kcol), cur(kcol), prev(vcol), cur(vcol),
            _full((SWA_BLOCK, 2 * SWA_BLOCK)),
            pl.BlockSpec(memory_space=pltpu.SMEM), pl.BlockSpec(memory_space=pltpu.SMEM)]


def _swa_fwd(proj, bucket, rel_bias, sinks, n_heads, qcol, kcol, vcol):
    s = proj.shape[0]
    width = n_heads * HEAD_DIM
    n_groups = max(1, n_heads // 4)
    per_group = n_heads // n_groups

    def body(q_ref, kp_ref, kc_ref, vp_ref, vc_ref, bucket_ref, rb_ref, sk_ref, o_ref, bias_ref):
        i = pl.program_id(0)

        @pl.when(i == 0)
        def _():
            _swa_build_bias(bucket_ref, rb_ref, bias_ref, n_heads)

        lane = lax.broadcasted_iota(jnp.int32, (1, LANES), 1)
        first = lane < HEAD_DIM
        valid = _swa_valid(i)
        kcat = jnp.concatenate([kp_ref[...], kc_ref[...]], axis=0)
        vcat = jnp.concatenate([vp_ref[...], vc_ref[...]], axis=0)
        for j in range(n_heads // 2):
            halves = []
            for h in (2 * j, 2 * j + 1):
                group = h // per_group
                sel = first if group == 0 else jnp.logical_not(first)
                q_pos = _swa_place(q_ref[:, j * LANES:(j + 1) * LANES], h, group, sel)
                prob, _ = _swa_probs(q_pos, kcat, bias_ref[h], valid, sk_ref[0, h])
                o2 = _dot(prob.astype(BF16), vcat)
                if (h % 2) != group:
                    o2 = pltpu.roll(o2, HEAD_DIM, axis=1)
                halves.append(o2)
            o_ref[:, j * LANES:(j + 1) * LANES] = jnp.where(first, halves[0], halves[1])

    return pl.pallas_call(
        body, name="swa_fwd", grid=(s // SWA_BLOCK,),
        in_specs=_swa_specs(n_heads, qcol, kcol, vcol),
        out_specs=pl.BlockSpec((SWA_BLOCK, width), lambda i: (i, 0)),
        out_shape=jax.ShapeDtypeStruct((s, width), F32),
        scratch_shapes=[pltpu.VMEM((n_heads, SWA_BLOCK, 2 * SWA_BLOCK), F32)],
        compiler_params=_params(),
    )(proj, proj, proj, proj, proj, bucket, rel_bias, sinks)


def _rms_fwd(o, g):
    r = lax.rsqrt(jnp.mean(o * o, axis=-1, keepdims=True) + RMS_EPS)
    n = o * r
    return n, r, n * g


def _mix_ln1(sb_out, sw_out, x, g_in, b_in, sb_g, sw_g, w_out):
    s, d = x.shape
    wsb, wsw = sb_out.shape[1], sw_out.shape[1]
    tm = min(ROW_TILE, s)

    def body(sb_ref, sw_ref, x_ref, gi_ref, bi_ref, sbg_ref, swg_ref, w_ref, mg_ref, u_ref):
        _, _, m_sb = _rms_fwd(sb_ref[...], sbg_ref[...])
        _, _, m_sw = _rms_fwd(sw_ref[...], swg_ref[...])
        m_sb = m_sb.astype(BF16)
        m_sw = m_sw.astype(BF16)
        mg_ref[:, :wsb] = m_sb
        mg_ref[:, wsb:] = m_sw
        mix = _dot(m_sb, w_ref[:wsb, :]) + _dot(m_sw, w_ref[wsb:, :])
        xhat, _ = _ln_hat(x_ref[...])
        h0 = xhat * gi_ref[...] + bi_ref[...]
        u_ref[...] = ALPHA * h0 + mix

    row = lambda width: pl.BlockSpec((tm, width), lambda i: (i, 0))
    return pl.pallas_call(
        body, name="mix_ln1", grid=(s // tm,),
        in_specs=[row(wsb), row(wsw), row(d), _full((1, d)), _full((1, d)),
                  _full((1, wsb)), _full((1, wsw)), _full((wsb + wsw, d))],
        out_specs=[row(wsb + wsw), row(d)],
        out_shape=[jax.ShapeDtypeStruct((s, wsb + wsw), BF16), jax.ShapeDtypeStruct((s, d), F32)],
        compiler_params=_params(),
    )(sb_out, sw_out, x, g_in, b_in, sb_g, sw_g, w_out)


def _ffn_fwd(u1, g1, b1, w_gu, w_down, g2, b2, target):
    s, d = u1.shape
    dff = w_down.shape[0]
    tm = min(ROW_TILE, s)
    n_c = dff // FF_CHUNK

    def body(u_ref, g1_ref, b1_ref, wg_ref, wu_ref, wd_ref, g2_ref, b2_ref, t_ref,
             du_ref, st_ref, h1_ref, h1b_ref, acc_ref):
        i, c = pl.program_id(0), pl.program_id(1)

        @pl.when((i == 0) & (c == 0))
        def _():
            st_ref[...] = jnp.zeros_like(st_ref)

        @pl.when(c == 0)
        def _():
            xhat, _ = _ln_hat(u_ref[...])
            h1 = xhat * g1_ref[...] + b1_ref[...]
            h1_ref[...] = h1
            h1b_ref[...] = h1.astype(BF16)
            acc_ref[...] = jnp.zeros_like(acc_ref)

        h1b = h1b_ref[...]
        gate = _dot(h1b, wg_ref[...])
        up = _dot(h1b, wu_ref[...])
        act = gate * jax.nn.sigmoid(gate) * up
        acc_ref[...] += _dot(act.astype(BF16), wd_ref[...])

        @pl.when(c == n_c - 1)
        def _():
            u2 = ALPHA * h1_ref[...] + acc_ref[...]
            xhat2, r2 = _ln_hat(u2)
            diff = xhat2 * g2_ref[...] + b2_ref[...] - t_ref[...]
            dh2 = diff * (1.0 / d)
            st_ref[0:1, :] += _colsum(dh2 * xhat2)
            st_ref[1:2, :] += _colsum(dh2)
            st_ref[2:3, :] += jnp.broadcast_to(_colsum(_rowsum(diff * diff)) * (0.5 / d), (1, d))
            du_ref[...] = _ln_bwd(dh2 * g2_ref[...], xhat2, r2)

    row = pl.BlockSpec((tm, d), lambda i, c: (i, 0))
    vec = pl.BlockSpec((1, d), lambda i, c: (0, 0))
    return pl.pallas_call(
        body, name="ffn_fwd", grid=(s // tm, n_c),
        in_specs=[row, vec, vec,
                  pl.BlockSpec((d, FF_CHUNK), lambda i, c: (0, c)),
                  pl.BlockSpec((d, FF_CHUNK), lambda i, c: (0, n_c + c)),
                  pl.BlockSpec((FF_CHUNK, d), lambda i, c: (c, 0)),
                  vec, vec, row],
        out_specs=[row, pl.BlockSpec((8, d), lambda i, c: (0, 0))],
        out_shape=[jax.ShapeDtypeStruct((s, d), F32), jax.ShapeDtypeStruct((8, d), F32)],
        scratch_shapes=[pltpu.VMEM((tm, d), F32), pltpu.VMEM((tm, d), BF16), pltpu.VMEM((tm, d), F32)],
        compiler_params=_params(),
    )(u1, g1, b1, w_gu, w_gu, w_down, g2, b2, target)


def _ffn_bwd(u1, du2, g1, b1, w_gu, w_down):
    s, d = u1.shape
    dff = w_down.shape[0]
    tm = min(ROW_TILE, s)
    n_c = dff // FF_CHUNK

    def body(u_ref, du2_ref, g1_ref, b1_ref, wg_ref, wu_ref, wd_ref,
             du1_ref, h1b_ref, act_ref, dg_ref, dup_ref, st_ref, xhat_ref, rstd_ref, du2b_ref, acc_ref):
        i, c = pl.program_id(0), pl.program_id(1)

        @pl.when((i == 0) & (c == 0))
        def _():
            st_ref[...] = jnp.zeros_like(st_ref)

        @pl.when(c == 0)
        def _():
            xhat, r = _ln_hat(u_ref[...])
            xhat_ref[...] = xhat
            rstd_ref[...] = jnp.broadcast_to(r, rstd_ref.shape)
            h1b_ref[...] = (xhat * g1_ref[...] + b1_ref[...]).astype(BF16)
            du2b_ref[...] = du2_ref[...].astype(BF16)
            acc_ref[...] = jnp.zeros_like(acc_ref)

        h1b = h1b_ref[...]
        gate = _dot(h1b, wg_ref[...])
        up = _dot(h1b, wu_ref[...])
        sg = jax.nn.sigmoid(gate)
        silu = gate * sg
        dact = _dot_nt(du2b_ref[...], wd_ref[...])
        dup = (dact * silu).astype(BF16)
        dgate = (dact * up * (sg * (1.0 + gate * (1.0 - sg)))).astype(BF16)
        act_ref[...] = (silu * up).astype(BF16)
        dg_ref[...] = dgate
        dup_ref[...] = dup
        acc_ref[...] += _dot_nt(dgate, wg_ref[...]) + _dot_nt(dup, wu_ref[...])

        @pl.when(c == n_c - 1)
        def _():
            dh1 = acc_ref[...] + ALPHA * du2_ref[...]
            xhat = xhat_ref[...]
            st_ref[0:1, :] += _colsum(dh1 * xhat)
            st_ref[1:2, :] += _colsum(dh1)
            du1_ref[...] = _ln_bwd(dh1 * g1_ref[...], xhat, rstd_ref[:, 0:1])

    row = pl.BlockSpec((tm, d), lambda i, c: (i, 0))
    vec = pl.BlockSpec((1, d), lambda i, c: (0, 0))
    chunk = pl.BlockSpec((tm, FF_CHUNK), lambda i, c: (i, c))
    return pl.pallas_call(
        body, name="ffn_bwd", grid=(s // tm, n_c),
        in_specs=[row, row, vec, vec,
                  pl.BlockSpec((d, FF_CHUNK), lambda i, c: (0, c)),
                  pl.BlockSpec((d, FF_CHUNK), lambda i, c: (0, n_c + c)),
                  pl.BlockSpec((FF_CHUNK, d), lambda i, c: (c, 0))],
        out_specs=[row, row, chunk, chunk, chunk, pl.BlockSpec((8, d), lambda i, c: (0, 0))],
        out_shape=[jax.ShapeDtypeStruct((s, d), F32), jax.ShapeDtypeStruct((s, d), BF16),
                   jax.ShapeDtypeStruct((s, dff), BF16), jax.ShapeDtypeStruct((s, dff), BF16),
                   jax.ShapeDtypeStruct((s, dff), BF16), jax.ShapeDtypeStruct((8, d), F32)],
        scratch_shapes=[pltpu.VMEM((tm, d), F32), pltpu.VMEM((tm, LANES), F32),
                        pltpu.VMEM((tm, d), BF16), pltpu.VMEM((tm, d), F32)],
        compiler_params=_params(),
    )(u1, du2, g1, b1, w_gu, w_gu, w_down)


def _rms_bwd(dm, o, g):
    n, r, _ = _rms_fwd(o, g)
    dn = dm * g
    return r * (dn - n * jnp.mean(dn * n, axis=-1, keepdims=True)), _colsum(dm * n)


def _mix_bwd(du1, w_out, sb_out, sw_out, sb_g, sw_g):
    s, d = du1.shape
    wsb, wsw = sb_out.shape[1], sw_out.shape[1]
    tm = min(ROW_TILE, s)

    def body(du_ref, w_ref, sb_ref, sw_ref, sbg_ref, swg_ref, dsb_ref, dsw_ref, st_ref):
        i = pl.program_id(0)

        @pl.when(i == 0)
        def _():
            st_ref[...] = jnp.zeros_like(st_ref)

        dmerged = _dot_nt(du_ref[...].astype(BF16), w_ref[...])
        dsb, gsb = _rms_bwd(dmerged[:, :wsb], sb_ref[...], sbg_ref[...])
        dsw, gsw = _rms_bwd(dmerged[:, wsb:], sw_ref[...], swg_ref[...])
        dsb_ref[...] = dsb.astype(BF16)
        dsw_ref[...] = dsw.astype(BF16)
        st_ref[0:1, :wsb] += gsb
        st_ref[0:1, wsb:] += gsw

    row = lambda width: pl.BlockSpec((tm, width), lambda i: (i, 0))
    return pl.pallas_call(
        body, name="mix_bwd", grid=(s // tm,),
        in_specs=[row(d), _full((wsb + wsw, d)), row(wsb), row(wsw), _full((1, wsb)), _full((1, wsw))],
        out_specs=[row(wsb), row(wsw), _full((8, wsb + wsw))],
        out_shape=[jax.ShapeDtypeStruct((s, wsb), BF16), jax.ShapeDtypeStruct((s, wsw), BF16),
                   jax.ShapeDtypeStruct((8, wsb + wsw), F32)],
        compiler_params=_params(),
    )(du1, w_out, sb_out, sw_out, sb_g, sw_g)


def _sb_bwd(proj, dout, out, n_pairs):
    s = proj.shape[0]
    t = min(SB_TILE, s)
    nq = s // t
    width = n_pairs * LANES

    def body(q_ref, k_ref, v_ref, do_ref, o_ref, dq_ref, dk_ref, dv_ref):
        i = pl.program_id(1)

        @pl.when(i == 0)
        def _():
            dk_ref[...] = jnp.zeros_like(dk_ref)
            dv_ref[...] = jnp.zeros_like(dv_ref)

        lane = lax.broadcasted_iota(jnp.int32, (1, LANES), 1)
        first = lane < HEAD_DIM
        q2 = q_ref[...]
        do2 = do_ref[...]
        zero = jnp.zeros_like(q2)
        qs = (jnp.where(first, q2, zero), jnp.where(first, zero, q2))
        dos = (jnp.where(first, do2, zero), jnp.where(first, zero, do2))
        prod = do2.astype(F32) * o_ref[...]
        totals = (_rowsum(jnp.where(first, prod, 0.0)), _rowsum(jnp.where(first, 0.0, prod)))
        row, col = _sb_tile_consts(t)
        upper = (row > col).astype(BF16)
        incl = (row >= col).astype(BF16)
        causal = col < row

        def visit(off, carry, mask):
            k_t = k_ref[pl.ds(off, t), :]
            v_t = v_ref[pl.ds(off, t), :]
            out = []
            dk_t = jnp.zeros((t, LANES), F32)
            dv_t = jnp.zeros((t, LANES), F32)
            for hh in range(2):
                c_l, c_e, dq = carry[hh]
                lb, l1, a = _sb_scores(qs[hh], k_t, upper, c_l, mask)
                a_b = a.astype(BF16)
                d_e = _dot_nt(dos[hh], v_t) * a_b.astype(F32)
                d_l = totals[hh] - (_dot(d_e.astype(BF16), incl) + c_e)
                dz = d_e - jnp.exp(lb) * (d_e + d_l)
                if mask is not None:
                    dz = jnp.where(mask, dz, 0.0)
                dzb = (dz * SCALE).astype(BF16)
                dk_t += _dot_tn(dzb, qs[hh])
                dv_t += _dot_tn(a_b, dos[hh])
                out.append((c_l + _rowsum(l1), c_e + _rowsum(d_e), dq + _dot(dzb, k_t)))
            dk_ref[pl.ds(off, t), :] += dk_t
            dv_ref[pl.ds(off, t), :] += dv_t
            return tuple(out)

        init = tuple((jnp.zeros((t, 1), F32), jnp.zeros((t, 1), F32), jnp.zeros((t, LANES), F32))
                     for _ in range(2))
        carry = _sb_walk(i, t, visit, init, causal)
        dq_ref[...] = jnp.where(first, carry[0][2], carry[1][2]).astype(BF16)

    qblk = pl.BlockSpec((t, LANES), lambda h, i: (i, h))
    whole = pl.BlockSpec((s, LANES), lambda h, i: (0, h))
    return pl.pallas_call(
        body, name="sb_bwd", grid=(n_pairs, nq),
        in_specs=[qblk,
                  pl.BlockSpec((s, LANES), lambda h, i: (0, n_pairs + h)),
                  pl.BlockSpec((s, LANES), lambda h, i: (0, 2 * n_pairs + h)),
                  qblk, qblk],
        out_specs=[qblk, whole, whole],
        out_shape=[jax.ShapeDtypeStruct((s, width), BF16), jax.ShapeDtypeStruct((s, width), F32),
                   jax.ShapeDtypeStruct((s, width), F32)],
        compiler_params=_params(),
    )(proj, proj, proj, dout, out)


def _swa_bwd(proj, dout, bucket, rel_bias, sinks, n_heads, qcol, kcol, vcol):
    s = proj.shape[0]
    width = n_heads * HEAD_DIM
    n_groups = max(1, n_heads // 4)
    per_group = n_heads // n_groups
    nb = s // SWA_BLOCK

    def body(q_ref, kp_ref, kc_ref, vp_ref, vc_ref, bucket_ref, rb_ref, sk_ref, do_ref,
             dq_ref, dk_ref, dv_ref, dsk_ref, drb_ref, bias_ref, dbias_ref):
        i = pl.program_id(0)

        @pl.when(i == 0)
        def _():
            _swa_build_bias(bucket_ref, rb_ref, bias_ref, n_heads)
            dbias_ref[...] = jnp.zeros_like(dbias_ref)
            dk_ref[...] = jnp.zeros_like(dk_ref)
            dv_ref[...] = jnp.zeros_like(dv_ref)
            dsk_ref[...] = jnp.zeros_like(dsk_ref)

        lane = lax.broadcasted_iota(jnp.int32, (1, LANES), 1)
        first = lane < HEAD_DIM
        valid = _swa_valid(i)
        kcat = jnp.concatenate([kp_ref[...], kc_ref[...]], axis=0)
        vcat = jnp.concatenate([vp_ref[...], vc_ref[...]], axis=0)
        dkcat = jnp.zeros((2 * SWA_BLOCK, LANES), F32)
        dvcat = jnp.zeros((2 * SWA_BLOCK, LANES), F32)
        for j in range(n_heads // 2):
            halves = []
            for h in (2 * j, 2 * j + 1):
                group = h // per_group
                sel = first if group == 0 else jnp.logical_not(first)
                q_pos = _swa_place(q_ref[:, j * LANES:(j + 1) * LANES], h, group, sel)
                do_pos = _swa_place(do_ref[:, j * LANES:(j + 1) * LANES], h, group, sel)
                prob, p_sink = _swa_probs(q_pos, kcat, bias_ref[h], valid, sk_ref[0, h])
                dprob = _dot_nt(do_pos, vcat)
                delta = _rowsum(prob * dprob)
                dlog = prob * (dprob - delta)
                dsk_ref[h:h + 1, :] += jnp.broadcast_to(-_colsum(p_sink * delta), (1, LANES))
                dbias_ref[h] += dlog
                dlb = (dlog * SCALE).astype(BF16)
                dq2 = _dot(dlb, kcat)
                if (h % 2) != group:
                    dq2 = pltpu.roll(dq2, HEAD_DIM, axis=1)
                halves.append(dq2)
                dkcat += _dot_tn(dlb, q_pos)
                dvcat += _dot_tn(prob.astype(BF16), do_pos)
            dq_ref[:, j * LANES:(j + 1) * LANES] = jnp.where(first, halves[0], halves[1]).astype(BF16)

        cur = pl.multiple_of(i * SWA_BLOCK, SWA_BLOCK)
        dk_ref[pl.ds(cur, SWA_BLOCK), :] += dkcat[SWA_BLOCK:, :]
        dv_ref[pl.ds(cur, SWA_BLOCK), :] += dvcat[SWA_BLOCK:, :]

        @pl.when(i > 0)
        def _():
            prv = pl.multiple_of((i - 1) * SWA_BLOCK, SWA_BLOCK)
            dk_ref[pl.ds(prv, SWA_BLOCK), :] += dkcat[:SWA_BLOCK, :]
            dv_ref[pl.ds(prv, SWA_BLOCK), :] += dvcat[:SWA_BLOCK, :]

        @pl.when(i == nb - 1)
        def _():
            bk = bucket_ref[...]
            rowi = lax.broadcasted_iota(jnp.int32, (REL_BUCKETS, LANES), 0)
            coli = lax.broadcasted_iota(jnp.int32, (REL_BUCKETS, LANES), 1)
            res = jnp.zeros((REL_BUCKETS, LANES), F32)
            for h in range(n_heads):
                db = dbias_ref[h]
                for b in range(REL_BUCKETS):
                    tot = _colsum(_rowsum(jnp.where(bk == b, db, 0.0)))
                    res = jnp.where((rowi == b) & (coli == h), tot, res)
            drb_ref[...] = res

    in_specs = _swa_specs(n_heads, qcol, kcol, vcol) + [pl.BlockSpec((SWA_BLOCK, width), lambda i: (i, 0))]
    return pl.pallas_call(
        body, name="swa_bwd", grid=(nb,),
        in_specs=in_specs,
        out_specs=[pl.BlockSpec((SWA_BLOCK, width), lambda i: (i, 0)),
                   _full((s, LANES)), _full((s, LANES)), _full((8, LANES)), _full((REL_BUCKETS, LANES))],
        out_shape=[jax.ShapeDtypeStruct((s, width), BF16), jax.ShapeDtypeStruct((s, LANES), F32),
                   jax.ShapeDtypeStruct((s, LANES), F32), jax.ShapeDtypeStruct((8, LANES), F32),
                   jax.ShapeDtypeStruct((REL_BUCKETS, LANES), F32)],
        scratch_shapes=[pltpu.VMEM((n_heads, SWA_BLOCK, 2 * SWA_BLOCK), F32),
                        pltpu.VMEM((n_heads, SWA_BLOCK, 2 * SWA_BLOCK), F32)],
        compiler_params=_params(),
    )(proj, proj, proj, proj, proj, bucket, rel_bias, sinks, dout)


def _proj_bwd(dproj, w_in, du1, x, g_in):
    s, d = x.shape
    cols = w_in.shape[1]
    tm = min(ROW_TILE, s)

    def body(dp_ref, w_ref, du_ref, x_ref, g_ref, dx_ref, st_ref):
        i = pl.program_id(0)

        @pl.when(i == 0)
        def _():
            st_ref[...] = jnp.zeros_like(st_ref)

        dh0 = _dot_nt(dp_ref[...], w_ref[...]) + ALPHA * du_ref[...]
        xhat, r = _ln_hat(x_ref[...])
        st_ref[0:1, :] += _colsum(dh0 * xhat)
        st_ref[1:2, :] += _colsum(dh0)
        dx_ref[...] = _ln_bwd(dh0 * g_ref[...], xhat, r)

    row = lambda width: pl.BlockSpec((tm, width), lambda i: (i, 0))
    return pl.pallas_call(
        body, name="proj_bwd", grid=(s // tm,),
        in_specs=[row(cols), _full((d, cols)), row(d), row(d), _full((1, d))],
        out_specs=[row(d), _full((8, d))],
        out_shape=[jax.ShapeDtypeStruct((s, d), F32), jax.ShapeDtypeStruct((8, d), F32)],
        compiler_params=_params(),
    )(dproj, w_in, du1, x, g_in)


def _wgrad(name, a, b, tm, tn):
    s, m = a.shape
    n = b.shape[1]
    ts = min(ROW_TILE, s)

    def body(a_ref, b_ref, o_ref):
        @pl.when(pl.program_id(2) == 0)
        def _():
            o_ref[...] = jnp.zeros_like(o_ref)

        o_ref[...] += _dot_tn(a_ref[...].astype(BF16), b_ref[...].astype(BF16))

    return pl.pallas_call(
        body, name=name, grid=(m // tm, n // tn, s // ts),
        in_specs=[pl.BlockSpec((ts, tm), lambda i, j, k: (k, i)),
                  pl.BlockSpec((ts, tn), lambda i, j, k: (k, j))],
        out_specs=pl.BlockSpec((tm, tn), lambda i, j, k: (i, j)),
        out_shape=jax.ShapeDtypeStruct((m, n), F32),
        compiler_params=_params(),
    )(a, b)


def _adamw_math(w, g, m, v):
    m = ADAM_B1 * m + (1.0 - ADAM_B1) * g
    v = ADAM_B2 * v + (1.0 - ADAM_B2) * (g * g)
    m_hat = m / (1.0 - ADAM_B1 ** ADAM_STEP)
    v_hat = v / (1.0 - ADAM_B2 ** ADAM_STEP)
    delta = -ADAM_LR * (m_hat / (jnp.sqrt(v_hat) + ADAM_EPS) + ADAM_WD * w)
    return delta, m, v


def _adamw(name, landed, w, m, v, tr):
    rows, cols = w.shape

    def body(l_ref, w_ref, m_ref, v_ref, g_ref, d_ref, nm_ref, nv_ref):
        g = l_ref[0]
        for src in range(1, N_DEV):
            g = g + l_ref[src]
        delta, nm, nv = _adamw_math(w_ref[...], g, m_ref[...], v_ref[...])
        g_ref[...] = g
        d_ref[...] = delta
        nm_ref[...] = nm
        nv_ref[...] = nv

    blk = pl.BlockSpec((tr, cols), lambda i: (i, 0))
    shape = jax.ShapeDtypeStruct((rows, cols), F32)
    return pl.pallas_call(
        body, name=name, grid=(rows // tr,),
        in_specs=[pl.BlockSpec((N_DEV, tr, cols), lambda i: (0, i, 0)), blk, blk, blk],
        out_specs=[blk, blk, blk, blk],
        out_shape=[shape, shape, shape, shape],
        compiler_params=_params(),
    )(landed, w, m, v)


def _pack(d, ln_in_g, ln_in_b, ln1_g, ln1_b, ln2_g, ln2_b, sb_g, sw_g, rel_bias, sinks, extra=None):
    tail = [rel_bias.reshape(-1), sinks.reshape(-1)]
    if extra is not None:
        tail.append(extra.reshape(-1))
    tail = jnp.concatenate(tail)
    tail = jnp.concatenate([tail, jnp.zeros((d - tail.shape[0],), F32)])
    rows = [ln_in_g.reshape(-1), ln_in_b.reshape(-1), ln1_g.reshape(-1), ln1_b.reshape(-1),
            ln2_g.reshape(-1), ln2_b.reshape(-1),
            jnp.concatenate([sb_g.reshape(-1), sw_g.reshape(-1)]), tail]
    return jnp.stack(rows)


def _unpack(p, wsb, n_rb, n_sk):
    return [p[0], p[1], p[6, :wsb][None], p[6, wsb:][None], p[7, n_rb:n_rb + n_sk][None],
            p[7, :n_rb].reshape(REL_BUCKETS, -1), p[2][None], p[3][None], p[4][None], p[5][None]]


def kernel(x, ln_in_g, ln_in_b, w_in, sb_norm_g, swa_norm_g, sinks, rel_bias, w_out, ln1_g, ln1_b, w_gate_up, w_down, ln2_g, ln2_b, loss_target, m_ln_in_g, m_ln_in_b, m_w_in, m_sb_norm_g, m_swa_norm_g, m_sinks, m_rel_bias, m_w_out, m_ln1_g, m_ln1_b, m_w_gate_up, m_w_down, m_ln2_g, m_ln2_b, v_ln_in_g, v_ln_in_b, v_w_in, v_sb_norm_g, v_swa_norm_g, v_sinks, v_rel_bias, v_w_out, v_ln1_g, v_ln1_b, v_w_gate_up, v_w_down, v_ln2_g, v_ln2_b):
    x2 = x[0]
    tgt = loss_target[0]
    s, d = x2.shape
    wsb = sb_norm_g.shape[-1]
    wsw = swa_norm_g.shape[-1]
    n_sw_heads = sinks.shape[-1]
    n_pairs = wsb // LANES
    dff = w_down.shape[1] * N_DEV
    assert wsb % LANES == 0 and wsw % LANES == 0 and n_sw_heads * HEAD_DIM == wsw
    assert 3 * wsb % wsw == 0 and dff % FF_CHUNK == 0 and s % SWA_BLOCK == 0
    qcol = 3 * wsb // wsw
    kcol = (3 * wsb + wsw) // LANES
    vcol = kcol + 1
    assert w_in.shape[-1] * N_DEV == (vcol + 1) * LANES

    big_w = [w_in[0], w_out[0], w_gate_up[0], w_down[0]]
    big_m = [m_w_in[0], m_w_out[0], m_w_gate_up[0], m_w_down[0]]
    big_v = [v_w_in[0], v_w_out[0], v_w_gate_up[0], v_w_down[0]]

    gathered = _exchange("weights_allgather", [w.astype(BF16) for w in big_w], ["gather"] * 4)
    cat_cols = lambda g: jnp.transpose(g, (1, 0, 2)).reshape(g.shape[1], N_DEV * g.shape[2])
    cat_rows = lambda g: g.reshape(N_DEV * g.shape[1], g.shape[2])
    w_in_f, w_out_f, w_gu_f, w_down_f = cat_cols(gathered[0]), cat_rows(gathered[1]), cat_cols(gathered[2]), cat_rows(gathered[3])

    vec = lambda a: a.reshape(1, -1)
    g_in, b_in = vec(ln_in_g), vec(ln_in_b)
    bucket = jnp.asarray(_swa_bucket_table())

    h0b, proj = _ln_proj(x2, g_in, b_in, w_in_f)
    sb_out = _sb_fwd(proj, n_pairs)
    sw_out = _swa_fwd(proj, bucket, rel_bias, sinks, n_sw_heads, qcol, kcol, vcol)
    merged, u1 = _mix_ln1(sb_out, sw_out, x2, g_in, b_in, sb_norm_g, swa_norm_g, w_out_f)
    du2, st_ln2 = _ffn_fwd(u1, ln1_g, ln1_b, w_gu_f, w_down_f, ln2_g, ln2_b, tgt)

    du1, h1b, act, dgate, dup, st_ln1 = _ffn_bwd(u1, du2, ln1_g, ln1_b, w_gu_f, w_down_f)
    dsb, dsw, st_rms = _mix_bwd(du1, w_out_f, sb_out, sw_out, sb_norm_g, swa_norm_g)
    dq_sb, dk_sb, dv_sb = _sb_bwd(proj, dsb, sb_out, n_pairs)
    dq_sw, dk_sw, dv_sw, st_sink, st_rb = _swa_bwd(proj, dsw, bucket, rel_bias, sinks, n_sw_heads, qcol, kcol, vcol)
    dproj = jnp.concatenate([dq_sb, dk_sb.astype(BF16), dv_sb.astype(BF16), dq_sw,
                             dk_sw.astype(BF16), dv_sw.astype(BF16)], axis=1)
    grad_x, st_in = _proj_bwd(dproj, w_in_f, du1, x2, g_in)

    tile_m = min(512, d)
    gw_in = _wgrad("wgrad_in", h0b, dproj, tile_m, dproj.shape[1] // 2)
    gw_out = _wgrad("wgrad_out", merged, du1, tile_m, d)
    gw_gate = _wgrad("wgrad_gate", h1b, dgate, tile_m, dff // 2)
    gw_up = _wgrad("wgrad_up", h1b, dup, tile_m, dff // 2)
    gw_down = _wgrad("wgrad_down", act, du2, dff // 2, d)

    split_cols = lambda g: jnp.transpose(g.reshape(g.shape[0], N_DEV, g.shape[1] // N_DEV), (1, 0, 2))
    split_rows = lambda g: g.reshape(N_DEV, g.shape[0] // N_DEV, g.shape[1])
    n_rb = rel_bias.size
    small = _pack(d, st_in[0], st_in[1], st_ln1[0], st_ln1[1], st_ln2[0], st_ln2[1],
                  st_rms[0, :wsb], st_rms[0, wsb:], st_rb[:, :n_sw_heads], st_sink[:n_sw_heads, 0],
                  extra=st_ln2[2, 0:1])
    landed = _exchange(
        "grads_exchange",
        [split_cols(gw_in), split_rows(gw_out), split_cols(jnp.concatenate([gw_gate, gw_up], axis=1)),
         split_rows(gw_down), small],
        ["scatter"] * 4 + ["gather"])

    big = []
    for name, land, w, m, v in zip(["adamw_in", "adamw_out", "adamw_gate_up", "adamw_down"], landed[:4], big_w, big_m, big_v):
        rows = w.shape[0]
        tr = 128 if rows % 128 == 0 else rows // 4
        big.append(_adamw(name, land, w, m, v, tr))

    small_w = _pack(d, ln_in_g, ln_in_b, ln1_g, ln1_b, ln2_g, ln2_b, sb_norm_g, swa_norm_g, rel_bias, sinks)
    small_m = _pack(d, m_ln_in_g, m_ln_in_b, m_ln1_g, m_ln1_b, m_ln2_g, m_ln2_b, m_sb_norm_g, m_swa_norm_g, m_rel_bias, m_sinks)
    small_v = _pack(d, v_ln_in_g, v_ln_in_b, v_ln1_g, v_ln1_b, v_ln2_g, v_ln2_b, v_sb_norm_g, v_swa_norm_g, v_rel_bias, v_sinks)
    sg, sd, sm, sv = _adamw("adamw_small", landed[4], small_w, small_m, small_v, 8)
    n_sk = sinks.size
    loss = sg[7, n_rb + n_sk]

    def leaves(idx):
        sm_l = _unpack([sg, sd, sm, sv][idx], wsb, n_rb, n_sk)
        bg = [b[idx][None] for b in big]
        return [sm_l[0], sm_l[1], bg[0], sm_l[2], sm_l[3], sm_l[4], sm_l[5], bg[1], sm_l[6], sm_l[7], bg[2], bg[3], sm_l[8], sm_l[9]]

    return (loss, grad_x[None], *leaves(0), *leaves(1), *leaves(2), *leaves(3))
```

```python
import functools
import math

import numpy as np
import jax
import jax.numpy as jnp
from jax import lax
from jax.experimental import pallas as pl
from jax.experimental.pallas import tpu as pltpu

F32 = jnp.float32
BF16 = jnp.bfloat16
MESH = pl.DeviceIdType.MESH

N_DEV = 8
LANES = 128
HEAD_DIM = 64
SCALE = HEAD_DIM ** -0.5
SWA_BLOCK = 128
REL_BUCKETS = 32
REL_MAX_DIST = 128
ALPHA = 2.0 ** 0.25
LN_EPS = 1e-5
RMS_EPS = 1e-6
ADAM_LR = 0.001
ADAM_B1 = 0.9
ADAM_B2 = 0.999
ADAM_EPS = 1e-08
ADAM_WD = 0.01
ADAM_STEP = 10

ROW_TILE = 512
SB_TILE = 256
FF_CHUNK = 256
SB_UNDERFLOW = -110.0
MIB = 1024 * 1024


def _params(vmem_mib=48):
    return pltpu.CompilerParams(vmem_limit_bytes=vmem_mib * MIB)


def _dot(a, b):
    return jnp.dot(a, b, preferred_element_type=F32)


def _dot_nt(a, b):
    return lax.dot_general(a, b, (((1,), (1,)), ((), ())), preferred_element_type=F32)


def _dot_tn(a, b):
    return lax.dot_general(a, b, (((0,), (0,)), ((), ())), preferred_element_type=F32)


def _ln_hat(x):
    mu = jnp.mean(x, axis=-1, keepdims=True)
    xc = x - mu
    var = jnp.mean(xc * xc, axis=-1, keepdims=True)
    r = lax.rsqrt(var + LN_EPS)
    return xc * r, r


def _ln_bwd(dxhat, xhat, r):
    return r * (dxhat - jnp.mean(dxhat, axis=-1, keepdims=True)
                - xhat * jnp.mean(dxhat * xhat, axis=-1, keepdims=True))


def _colsum(a):
    return jnp.sum(a, axis=0, keepdims=True)


def _rowsum(a):
    return jnp.sum(a, axis=1, keepdims=True)


def _full(shape):
    return pl.BlockSpec(shape, lambda *_: (0,) * len(shape))


def _comm_out_shapes(arrays, kinds):
    shapes = []
    for a, kind in zip(arrays, kinds):
        blk = a.shape if kind == "gather" else a.shape[1:]
        shapes.append(jax.ShapeDtypeStruct((N_DEV,) + tuple(blk), a.dtype))
    return shapes


def _comm_sems(n):
    return [pltpu.SemaphoreType.DMA((n, N_DEV - 1)), pltpu.SemaphoreType.DMA((n, N_DEV - 1)),
            pltpu.SemaphoreType.DMA((n,))]


def _comm_copies(ins, outs, kinds, send_sems, recv_sems, local_sems):
    x, y, c = lax.axis_index("x"), lax.axis_index("y"), lax.axis_index("c")
    me = 4 * x + 2 * y + c

    def src_for(t, dev_lin):
        return ins[t] if kinds[t] == "gather" else ins[t].at[dev_lin]

    local = [pltpu.make_async_copy(src_for(t, me), outs[t].at[me], local_sems.at[t]) for t in range(len(kinds))]
    sends, arrivals = [], []
    for k in range(1, N_DEV):
        px = 1 - x if (k >> 2) & 1 else x
        py = 1 - y if (k >> 1) & 1 else y
        pc = 1 - c if k & 1 else c
        peer_lin = 4 * px + 2 * py + pc
        for t in range(len(kinds)):
            sems = dict(send_sem=send_sems.at[t, k - 1], recv_sem=recv_sems.at[t, k - 1],
                        device_id=(px, py, pc), device_id_type=MESH)
            sends.append(pltpu.make_async_remote_copy(src_ref=src_for(t, peer_lin), dst_ref=outs[t].at[me], **sems))
            arrivals.append(pltpu.make_async_remote_copy(src_ref=src_for(t, peer_lin), dst_ref=outs[t].at[peer_lin], **sems))
    return local, sends, arrivals


def _comm_start(ins, outs, kinds, sems):
    local, sends, _ = _comm_copies(ins, outs, kinds, *sems)
    for cp in local + sends:
        cp.start()


def _comm_finish(ins, outs, kinds, sems):
    local, sends, arrivals = _comm_copies(ins, outs, kinds, *sems)
    for cp in arrivals:
        cp.wait_recv()
    for cp in sends:
        cp.wait_send()
    for cp in local:
        cp.wait()


def _exchange(name, arrays, kinds):
    n = len(arrays)

    def body(*refs):
        ins, outs, sems = refs[:n], refs[n:2 * n], refs[2 * n:]
        _comm_start(ins, outs, kinds, sems)
        _comm_finish(ins, outs, kinds, sems)

    any_spec = pl.BlockSpec(memory_space=pl.ANY)
    return pl.pallas_call(
        body, name=name, out_shape=_comm_out_shapes(arrays, kinds),
        in_specs=[any_spec] * n, out_specs=[any_spec] * n,
        scratch_shapes=_comm_sems(n),
    )(*arrays)


def _call(body, name, grid, in_specs, out_specs, out_shape, args, scratch_shapes=(), comm=None):
    if comm is None:
        outs = pl.pallas_call(body, name=name, grid=grid, in_specs=in_specs, out_specs=out_specs,
                              out_shape=out_shape, scratch_shapes=list(scratch_shapes),
                              compiler_params=_params())(*args)
        return outs, []
    arrays, kinds = comm
    n, n_in, n_out, n_scr = len(arrays), len(in_specs), len(out_specs), len(scratch_shapes)

    def fused(*refs):
        c_in, x_in = refs[:n_in], refs[n_in:n_in + n]
        c_out = refs[n_in + n:n_in + n + n_out]
        x_out = refs[n_in + n + n_out:n_in + 2 * n + n_out]
        rest = refs[n_in + 2 * n + n_out:]
        c_scr, sems = rest[:n_scr], rest[n_scr:]
        ids = [pl.program_id(a) for a in range(len(grid))]
        is_first = functools.reduce(jnp.logical_and, [i == 0 for i in ids])
        is_last = functools.reduce(jnp.logical_and, [i == g - 1 for i, g in zip(ids, grid)])

        @pl.when(is_first)
        def _():
            _comm_start(x_in, x_out, kinds, sems)

        body(*c_in, *c_out, *c_scr)

        @pl.when(is_last)
        def _():
            _comm_finish(x_in, x_out, kinds, sems)

    any_spec = pl.BlockSpec(memory_space=pl.ANY)
    outs = pl.pallas_call(
        fused, name=name, grid=grid,
        in_specs=list(in_specs) + [any_spec] * n, out_specs=list(out_specs) + [any_spec] * n,
        out_shape=list(out_shape) + _comm_out_shapes(arrays, kinds),
        scratch_shapes=list(scratch_shapes) + _comm_sems(n),
        compiler_params=_params())(*args, *arrays)
    return outs[:n_out], outs[n_out:]


def _ln_proj(x, g, b, w_in):
    s, d = x.shape
    cols = w_in.shape[1]
    tm = min(ROW_TILE, s)

    def body(x_ref, g_ref, b_ref, w_ref, h_ref, p_ref):
        xhat, _ = _ln_hat(x_ref[...])
        h = (xhat * g_ref[...] + b_ref[...]).astype(BF16)
        h_ref[...] = h
        p_ref[...] = _dot(h, w_ref[...]).astype(BF16)

    row = lambda width: pl.BlockSpec((tm, width), lambda i: (i, 0))
    return pl.pallas_call(
        body, name="ln_proj", grid=(s // tm,),
        in_specs=[row(d), _full((1, d)), _full((1, d)), _full((d, cols))],
        out_specs=[row(d), row(cols)],
        out_shape=[jax.ShapeDtypeStruct((s, d), BF16), jax.ShapeDtypeStruct((s, cols), BF16)],
        compiler_params=_params(),
    )(x, g, b, w_in)


def _sb_tile_consts(t):
    row = lax.broadcasted_iota(jnp.int32, (t, t), 0)
    col = lax.broadcasted_iota(jnp.int32, (t, t), 1)
    return row, col


def _sb_scores(qh, k_t, upper, carry_l, causal):
    z = _dot_nt(qh, k_t) * SCALE
    sp = jnp.log(1.0 + jnp.exp(-jnp.abs(z)))
    lb = jnp.minimum(z, 0.0) - sp
    l1 = -(jnp.maximum(z, 0.0) + sp)
    if causal is not None:
        l1 = jnp.where(causal, l1, 0.0)
    hi = l1.astype(BF16)
    lo = (l1 - hi.astype(F32)).astype(BF16)
    suf = _dot(hi, upper) + _dot(lo, upper) + carry_l
    a = jnp.exp(lb + suf)
    if causal is not None:
        a = jnp.where(causal, a, 0.0)
    return lb, l1, a


def _sb_walk(i, t, visit, init, causal):
    def alive(carry):
        return jnp.max(jnp.maximum(carry[0][0], carry[1][0])) > SB_UNDERFLOW

    carry = visit(pl.multiple_of(i * t, t), init, causal)

    def cond(state):
        j, go, _ = state
        return (j < i) & go

    def body(state):
        j, _, carry = state
        carry = visit(pl.multiple_of((i - 1 - j) * t, t), carry, None)
        return j + 1, alive(carry), carry

    return lax.while_loop(cond, body, (jnp.int32(0), alive(carry), carry))[2]


def _sb_fwd(proj, n_pairs, comm=None):
    s = proj.shape[0]
    t = min(SB_TILE, s)
    nq = s // t

    def body(q_ref, k_ref, v_ref, o_ref):
        i = pl.program_id(1)
        lane = lax.broadcasted_iota(jnp.int32, (1, LANES), 1)
        first = lane < HEAD_DIM
        q2 = q_ref[...]
        zero = jnp.zeros_like(q2)
        qs = (jnp.where(first, q2, zero), jnp.where(first, zero, q2))
        row, col = _sb_tile_consts(t)
        upper = (row > col).astype(BF16)
        causal = col < row

        def visit(off, carry, mask):
            k_t = k_ref[pl.ds(off, t), :]
            v_t = v_ref[pl.ds(off, t), :]
            out = []
            for hh in range(2):
                c_l, acc = carry[hh]
                _, l1, a = _sb_scores(qs[hh], k_t, upper, c_l, mask)
                out.append((c_l + _rowsum(l1), acc + _dot(a.astype(BF16), v_t)))
            return tuple(out)

        init = tuple((jnp.zeros((t, 1), F32), jnp.zeros((t, LANES), F32)) for _ in range(2))
        carry = _sb_walk(i, t, visit, init, causal)
        o_ref[...] = jnp.where(first, carry[0][1], carry[1][1])

    outs, landed = _call(
        body, "sb_fwd", (n_pairs, nq),
        in_specs=[pl.BlockSpec((t, LANES), lambda h, i: (i, h)),
                  pl.BlockSpec((s, LANES), lambda h, i: (0, n_pairs + h)),
                  pl.BlockSpec((s, LANES), lambda h, i: (0, 2 * n_pairs + h))],
        out_specs=[pl.BlockSpec((t, LANES), lambda h, i: (i, h))],
        out_shape=[jax.ShapeDtypeStruct((s, n_pairs * LANES), F32)],
        args=(proj, proj, proj), comm=comm)
    return outs[0], landed


def _swa_bucket_table():
    qi = np.arange(SWA_BLOCK)[:, None]
    cj = np.arange(2 * SWA_BLOCK)[None, :]
    dist = qi + SWA_BLOCK - cj
    exact = REL_BUCKETS // 2
    d = np.maximum(dist, 0)
    d_f = np.maximum(d, 1).astype(np.float32)
    large = exact + (np.log(d_f / np.float32(exact)) / np.float32(math.log(REL_MAX_DIST / exact))
                     * np.float32(REL_BUCKETS - exact)).astype(np.int32)
    large = np.minimum(large, REL_BUCKETS - 1)
    return np.where(d < exact, d, large).astype(np.int32)


def _swa_build_bias(bucket_ref, rb_ref, bias_ref, n_heads):
    bk = bucket_ref[...]
    for h in range(n_heads):
        acc = jnp.zeros(bk.shape, F32)
        for b in range(REL_BUCKETS):
            acc = jnp.where(bk == b, rb_ref[b, h], acc)
        bias_ref[h] = acc


def _swa_valid(i):
    row = lax.broadcasted_iota(jnp.int32, (SWA_BLOCK, 2 * SWA_BLOCK), 0)
    col = lax.broadcasted_iota(jnp.int32, (SWA_BLOCK, 2 * SWA_BLOCK), 1)
    dist = row + SWA_BLOCK - col
    return (dist >= 0) & (dist < SWA_BLOCK) & ((col >= SWA_BLOCK) | (i > 0))


def _swa_place(blk, h, group, sel):
    if (h % 2) != group:
        blk = pltpu.roll(blk.astype(F32), HEAD_DIM, axis=1).astype(BF16)
    return jnp.where(sel, blk, jnp.zeros_like(blk))


def _swa_probs(q_pos, kcat, bias_h, valid, sink):
    logits = _dot_nt(q_pos, kcat) * SCALE + bias_h
    logits = jnp.where(valid, logits, -jnp.inf)
    m = jnp.maximum(jnp.max(logits, axis=1, keepdims=True), sink)
    p = jnp.exp(logits - m)
    es = jnp.exp(sink - m)
    denom = _rowsum(p) + es
    return p / denom, es / denom


def _swa_specs(n_heads, qcol, kcol, vcol):
    width = n_heads * HEAD_DIM
    prev = lambda col: pl.BlockSpec((SWA_BLOCK, LANES), lambda i: (jnp.maximum(i - 1, 0), col))
    cur = lambda col: pl.BlockSpec((SWA_BLOCK, LANES), lambda i: (i, col))
    return [pl.BlockSpec((SWA_BLOCK, width), lambda i: (i, qcol)),
            prev(kcol), cur(kcol), prev(vcol), cur(vcol),
            _full((SWA_BLOCK, 2 * SWA_BLOCK)),
            pl.BlockSpec(memory_space=pltpu.SMEM), pl.BlockSpec(memory_space=pltpu.SMEM)]


def _swa_fwd(proj, bucket, rel_bias, sinks, n_heads, qcol, kcol, vcol):
    s = proj.shape[0]
    width = n_heads * HEAD_DIM
    n_groups = max(1, n_heads // 4)
    per_group = n_heads // n_groups

    def body(q_ref, kp_ref, kc_ref, vp_ref, vc_ref, bucket_ref, rb_ref, sk_ref, o_ref, bias_ref):
        i = pl.program_id(0)

        @pl.when(i == 0)
        def _():
            _swa_build_bias(bucket_ref, rb_ref, bias_ref, n_heads)

        lane = lax.broadcasted_iota(jnp.int32, (1, LANES), 1)
        first = lane < HEAD_DIM
        valid = _swa_valid(i)
        kcat = jnp.concatenate([kp_ref[...], kc_ref[...]], axis=0)
        vcat = jnp.concatenate([vp_ref[...], vc_ref[...]], axis=0)
        for j in range(n_heads // 2):
            halves = []
            for h in (2 * j, 2 * j + 1):
                group = h // per_group
                sel = first if group == 0 else jnp.logical_not(first)
                q_pos = _swa_place(q_ref[:, j * LANES:(j + 1) * LANES], h, group, sel)
                prob, _ = _swa_probs(q_pos, kcat, bias_ref[h], valid, sk_ref[0, h])
                o2 = _dot(prob.astype(BF16), vcat)
                if (h % 2) != group:
                    o2 = pltpu.roll(o2, HEAD_DIM, axis=1)
                halves.append(o2)
            o_ref[:, j * LANES:(j + 1) * LANES] = jnp.where(first, halves[0], halves[1])

    return pl.pallas_call(
        body, name="swa_fwd", grid=(s // SWA_BLOCK,),
        in_specs=_swa_specs(n_heads, qcol, kcol, vcol),
        out_specs=pl.BlockSpec((SWA_BLOCK, width), lambda i: (i, 0)),
        out_shape=jax.ShapeDtypeStruct((s, width), F32),
        scratch_shapes=[pltpu.VMEM((n_heads, SWA_BLOCK, 2 * SWA_BLOCK), F32)],
        compiler_params=_params(),
    )(proj, proj, proj, proj, proj, bucket, rel_bias, sinks)


def _rms_fwd(o, g):
    r = lax.rsqrt(jnp.mean(o * o, axis=-1, keepdims=True) + RMS_EPS)
    n = o * r
    return n, r, n * g


def _mix_ln1(sb_out, sw_out, x, g_in, b_in, sb_g, sw_g, w_out):
    s, d = x.shape
    wsb, wsw = sb_out.shape[1], sw_out.shape[1]
    tm = min(ROW_TILE, s)

    def body(sb_ref, sw_ref, x_ref, gi_ref, bi_ref, sbg_ref, swg_ref, w_ref, mg_ref, u_ref):
        _, _, m_sb = _rms_fwd(sb_ref[...], sbg_ref[...])
        _, _, m_sw = _rms_fwd(sw_ref[...], swg_ref[...])
        m_sb = m_sb.astype(BF16)
        m_sw = m_sw.astype(BF16)
        mg_ref[:, :wsb] = m_sb
        mg_ref[:, wsb:] = m_sw
        mix = _dot(m_sb, w_ref[:wsb, :]) + _dot(m_sw, w_ref[wsb:, :])
        xhat, _ = _ln_hat(x_ref[...])
        h0 = xhat * gi_ref[...] + bi_ref[...]
        u_ref[...] = ALPHA * h0 + mix

    row = lambda width: pl.BlockSpec((tm, width), lambda i: (i, 0))
    return pl.pallas_call(
        body, name="mix_ln1", grid=(s // tm,),
        in_specs=[row(wsb), row(wsw), row(d), _full((1, d)), _full((1, d)),
                  _full((1, wsb)), _full((1, wsw)), _full((wsb + wsw, d))],
        out_specs=[row(wsb + wsw), row(d)],
        out_shape=[jax.ShapeDtypeStruct((s, wsb + wsw), BF16), jax.ShapeDtypeStruct((s, d), F32)],
        compiler_params=_params(),
    )(sb_out, sw_out, x, g_in, b_in, sb_g, sw_g, w_out)


def _ffn_fwd(u1, g1, b1, w_gu, w_down, g2, b2, target):
    s, d = u1.shape
    dff = w_down.shape[0]
    tm = min(ROW_TILE, s)
    n_c = dff // FF_CHUNK

    def body(u_ref, g1_ref, b1_ref, wg_ref, wu_ref, wd_ref, g2_ref, b2_ref, t_ref,
             du_ref, st_ref, h1_ref, h1b_ref, acc_ref):
        i, c = pl.program_id(0), pl.program_id(1)

        @pl.when((i == 0) & (c == 0))
        def _():
            st_ref[...] = jnp.zeros_like(st_ref)

        @pl.when(c == 0)
        def _():
            xhat, _ = _ln_hat(u_ref[...])
            h1 = xhat * g1_ref[...] + b1_ref[...]
            h1_ref[...] = h1
            h1b_ref[...] = h1.astype(BF16)
            acc_ref[...] = jnp.zeros_like(acc_ref)

        h1b = h1b_ref[...]
        gate = _dot(h1b, wg_ref[...])
        up = _dot(h1b, wu_ref[...])
        act = gate * jax.nn.sigmoid(gate) * up
        acc_ref[...] += _dot(act.astype(BF16), wd_ref[...])

        @pl.when(c == n_c - 1)
        def _():
            u2 = ALPHA * h1_ref[...] + acc_ref[...]
            xhat2, r2 = _ln_hat(u2)
            diff = xhat2 * g2_ref[...] + b2_ref[...] - t_ref[...]
            dh2 = diff * (1.0 / d)
            st_ref[0:1, :] += _colsum(dh2 * xhat2)
            st_ref[1:2, :] += _colsum(dh2)
            st_ref[2:3, :] += jnp.broadcast_to(_colsum(_rowsum(diff * diff)) * (0.5 / d), (1, d))
            du_ref[...] = _ln_bwd(dh2 * g2_ref[...], xhat2, r2)

    row = pl.BlockSpec((tm, d), lambda i, c: (i, 0))
    vec = pl.BlockSpec((1, d), lambda i, c: (0, 0))
    return pl.pallas_call(
        body, name="ffn_fwd", grid=(s // tm, n_c),
        in_specs=[row, vec, vec,
                  pl.BlockSpec((d, FF_CHUNK), lambda i, c: (0, c)),
                  pl.BlockSpec((d, FF_CHUNK), lambda i, c: (0, n_c + c)),
                  pl.BlockSpec((FF_CHUNK, d), lambda i, c: (c, 0)),
                  vec, vec, row],
        out_specs=[row, pl.BlockSpec((8, d), lambda i, c: (0, 0))],
        out_shape=[jax.ShapeDtypeStruct((s, d), F32), jax.ShapeDtypeStruct((8, d), F32)],
        scratch_shapes=[pltpu.VMEM((tm, d), F32), pltpu.VMEM((tm, d), BF16), pltpu.VMEM((tm, d), F32)],
        compiler_params=_params(),
    )(u1, g1, b1, w_gu, w_gu, w_down, g2, b2, target)


def _ffn_bwd(u1, du2, g1, b1, w_gu, w_down):
    s, d = u1.shape
    dff = w_down.shape[0]
    tm = min(ROW_TILE, s)
    n_c = dff // FF_CHUNK

    def body(u_ref, du2_ref, g1_ref, b1_ref, wg_ref, wu_ref, wd_ref,
             du1_ref, h1b_ref, act_ref, dg_ref, dup_ref, st_ref, xhat_ref, rstd_ref, du2b_ref, acc_ref):
        i, c = pl.program_id(0), pl.program_id(1)

        @pl.when((i == 0) & (c == 0))
        def _():
            st_ref[...] = jnp.zeros_like(st_ref)

        @pl.when(c == 0)
        def _():
            xhat, r = _ln_hat(u_ref[...])
            xhat_ref[...] = xhat
            rstd_ref[...] = jnp.broadcast_to(r, rstd_ref.shape)
            h1b_ref[...] = (xhat * g1_ref[...] + b1_ref[...]).astype(BF16)
            du2b_ref[...] = du2_ref[...].astype(BF16)
            acc_ref[...] = jnp.zeros_like(acc_ref)

        h1b = h1b_ref[...]
        gate = _dot(h1b, wg_ref[...])
        up = _dot(h1b, wu_ref[...])
        sg = jax.nn.sigmoid(gate)
        silu = gate * sg
        dact = _dot_nt(du2b_ref[...], wd_ref[...])
        dup = (dact * silu).astype(BF16)
        dgate = (dact * up * (sg * (1.0 + gate * (1.0 - sg)))).astype(BF16)
        act_ref[...] = (silu * up).astype(BF16)
        dg_ref[...] = dgate
        dup_ref[...] = dup
        acc_ref[...] += _dot_nt(dgate, wg_ref[...]) + _dot_nt(dup, wu_ref[...])

        @pl.when(c == n_c - 1)
        def _():
            dh1 = acc_ref[...] + ALPHA * du2_ref[...]
            xhat = xhat_ref[...]
            st_ref[0:1, :] += _colsum(dh1 * xhat)
            st_ref[1:2, :] += _colsum(dh1)
            du1_ref[...] = _ln_bwd(dh1 * g1_ref[...], xhat, rstd_ref[:, 0:1])

    row = pl.BlockSpec((tm, d), lambda i, c: (i, 0))
    vec = pl.BlockSpec((1, d), lambda i, c: (0, 0))
    chunk = pl.BlockSpec((tm, FF_CHUNK), lambda i, c: (i, c))
    return pl.pallas_call(
        body, name="ffn_bwd", grid=(s // tm, n_c),
        in_specs=[row, row, vec, vec,
                  pl.BlockSpec((d, FF_CHUNK), lambda i, c: (0, c)),
                  pl.BlockSpec((d, FF_CHUNK), lambda i, c: (0, n_c + c)),
                  pl.BlockSpec((FF_CHUNK, d), lambda i, c: (c, 0))],
        out_specs=[row, row, chunk, chunk, chunk, pl.BlockSpec((8, d), lambda i, c: (0, 0))],
        out_shape=[jax.ShapeDtypeStruct((s, d), F32), jax.ShapeDtypeStruct((s, d), BF16),
                   jax.ShapeDtypeStruct((s, dff), BF16), jax.ShapeDtypeStruct((s, dff), BF16),
                   jax.ShapeDtypeStruct((s, dff), BF16), jax.ShapeDtypeStruct((8, d), F32)],
        scratch_shapes=[pltpu.VMEM((tm, d), F32), pltpu.VMEM((tm, LANES), F32),
                        pltpu.VMEM((tm, d), BF16), pltpu.VMEM((tm, d), F32)],
        compiler_params=_params(),
    )(u1, du2, g1, b1, w_gu, w_gu, w_down)


def _rms_bwd(dm, o, g):
    n, r, _ = _rms_fwd(o, g)
    dn = dm * g
    return r * (dn - n * jnp.mean(dn * n, axis=-1, keepdims=True)), _colsum(dm * n)


def _mix_bwd(du1, w_out, sb_out, sw_out, sb_g, sw_g):
    s, d = du1.shape
    wsb, wsw = sb_out.shape[1], sw_out.shape[1]
    tm = min(ROW_TILE, s)

    def body(du_ref, w_ref, sb_ref, sw_ref, sbg_ref, swg_ref, dsb_ref, dsw_ref, st_ref):
        i = pl.program_id(0)

        @pl.when(i == 0)
        def _():
            st_ref[...] = jnp.zeros_like(st_ref)

        dmerged = _dot_nt(du_ref[...].astype(BF16), w_ref[...])
        dsb, gsb = _rms_bwd(dmerged[:, :wsb], sb_ref[...], sbg_ref[...])
        dsw, gsw = _rms_bwd(dmerged[:, wsb:], sw_ref[...], swg_ref[...])
        dsb_ref[...] = dsb.astype(BF16)
        dsw_ref[...] = dsw.astype(BF16)
        st_ref[0:1, :wsb] += gsb
        st_ref[0:1, wsb:] += gsw

    row = lambda width: pl.BlockSpec((tm, width), lambda i: (i, 0))
    return pl.pallas_call(
        body, name="mix_bwd", grid=(s // tm,),
        in_specs=[row(d), _full((wsb + wsw, d)), row(wsb), row(wsw), _full((1, wsb)), _full((1, wsw))],
        out_specs=[row(wsb), row(wsw), _full((8, wsb + wsw))],
        out_shape=[jax.ShapeDtypeStruct((s, wsb), BF16), jax.ShapeDtypeStruct((s, wsw), BF16),
                   jax.ShapeDtypeStruct((8, wsb + wsw), F32)],
        compiler_params=_params(),
    )(du1, w_out, sb_out, sw_out, sb_g, sw_g)


def _sb_bwd(proj, dout, out, n_pairs, comm=None):
    s = proj.shape[0]
    t = min(SB_TILE, s)
    nq = s // t
    width = n_pairs * LANES

    def body(q_ref, k_ref, v_ref, do_ref, o_ref, dq_ref, dk_ref, dv_ref):
        i = pl.program_id(1)

        @pl.when(i == 0)
        def _():
            dk_ref[...] = jnp.zeros_like(dk_ref)
            dv_ref[...] = jnp.zeros_like(dv_ref)

        lane = lax.broadcasted_iota(jnp.int32, (1, LANES), 1)
        first = lane < HEAD_DIM
        q2 = q_ref[...]
        do2 = do_ref[...]
        zero = jnp.zeros_like(q2)
        qs = (jnp.where(first, q2, zero), jnp.where(first, zero, q2))
        dos = (jnp.where(first, do2, zero), jnp.where(first, zero, do2))
        prod = do2.astype(F32) * o_ref[...]
        totals = (_rowsum(jnp.where(first, prod, 0.0)), _rowsum(jnp.where(first, 0.0, prod)))
        row, col = _sb_tile_consts(t)
        upper = (row > col).astype(BF16)
        incl = (row >= col).astype(BF16)
        causal = col < row

        def visit(off, carry, mask):
            k_t = k_ref[pl.ds(off, t), :]
            v_t = v_ref[pl.ds(off, t), :]
            out = []
            dk_t = jnp.zeros((t, LANES), F32)
            dv_t = jnp.zeros((t, LANES), F32)
            for hh in range(2):
                c_l, c_e, dq = carry[hh]
                lb, l1, a = _sb_scores(qs[hh], k_t, upper, c_l, mask)
                a_b = a.astype(BF16)
                d_e = _dot_nt(dos[hh], v_t) * a_b.astype(F32)
                d_l = totals[hh] - (_dot(d_e.astype(BF16), incl) + c_e)
                dz = d_e - jnp.exp(lb) * (d_e + d_l)
                if mask is not None:
                    dz = jnp.where(mask, dz, 0.0)
                dzb = (dz * SCALE).astype(BF16)
                dk_t += _dot_tn(dzb, qs[hh])
                dv_t += _dot_tn(a_b, dos[hh])
                out.append((c_l + _rowsum(l1), c_e + _rowsum(d_e), dq + _dot(dzb, k_t)))
            dk_ref[pl.ds(off, t), :] += dk_t
            dv_ref[pl.ds(off, t), :] += dv_t
            return tuple(out)

        init = tuple((jnp.zeros((t, 1), F32), jnp.zeros((t, 1), F32), jnp.zeros((t, LANES), F32))
                     for _ in range(2))
        carry = _sb_walk(i, t, visit, init, causal)
        dq_ref[...] = jnp.where(first, carry[0][2], carry[1][2]).astype(BF16)

    qblk = pl.BlockSpec((t, LANES), lambda h, i: (i, h))
    whole = pl.BlockSpec((s, LANES), lambda h, i: (0, h))
    return _call(
        body, "sb_bwd", (n_pairs, nq),
        in_specs=[qblk,
                  pl.BlockSpec((s, LANES), lambda h, i: (0, n_pairs + h)),
                  pl.BlockSpec((s, LANES), lambda h, i: (0, 2 * n_pairs + h)),
                  qblk, qblk],
        out_specs=[qblk, whole, whole],
        out_shape=[jax.ShapeDtypeStruct((s, width), BF16), jax.ShapeDtypeStruct((s, width), F32),
                   jax.ShapeDtypeStruct((s, width), F32)],
        args=(proj, proj, proj, dout, out), comm=comm)


def _swa_bwd(proj, dout, bucket, rel_bias, sinks, n_heads, qcol, kcol, vcol, comm=None):
    s = proj.shape[0]
    width = n_heads * HEAD_DIM
    n_groups = max(1, n_heads // 4)
    per_group = n_heads // n_groups
    nb = s // SWA_BLOCK

    def body(q_ref, kp_ref, kc_ref, vp_ref, vc_ref, bucket_ref, rb_ref, sk_ref, do_ref,
             dq_ref, dk_ref, dv_ref, dsk_ref, drb_ref, bias_ref, dbias_ref):
        i = pl.program_id(0)

        @pl.when(i == 0)
        def _():
            _swa_build_bias(bucket_ref, rb_ref, bias_ref, n_heads)
            dbias_ref[...] = jnp.zeros_like(dbias_ref)
            dk_ref[...] = jnp.zeros_like(dk_ref)
            dv_ref[...] = jnp.zeros_like(dv_ref)
            dsk_ref[...] = jnp.zeros_like(dsk_ref)

        lane = lax.broadcasted_iota(jnp.int32, (1, LANES), 1)
        first = lane < HEAD_DIM
        valid = _swa_valid(i)
        kcat = jnp.concatenate([kp_ref[...], kc_ref[...]], axis=0)
        vcat = jnp.concatenate([vp_ref[...], vc_ref[...]], axis=0)
        dkcat = jnp.zeros((2 * SWA_BLOCK, LANES), F32)
        dvcat = jnp.zeros((2 * SWA_BLOCK, LANES), F32)
        for j in range(n_heads // 2):
            halves = []
            for h in (2 * j, 2 * j + 1):
                group = h // per_group
                sel = first if group == 0 else jnp.logical_not(first)
                q_pos = _swa_place(q_ref[:, j * LANES:(j + 1) * LANES], h, group, sel)
                do_pos = _swa_place(do_ref[:, j * LANES:(j + 1) * LANES], h, group, sel)
                prob, p_sink = _swa_probs(q_pos, kcat, bias_ref[h], valid, sk_ref[0, h])
                dprob = _dot_nt(do_pos, vcat)
                delta = _rowsum(prob * dprob)
                dlog = prob * (dprob - delta)
                dsk_ref[h:h + 1, :] += jnp.broadcast_to(-_colsum(p_sink * delta), (1, LANES))
                dbias_ref[h] += dlog
                dlb = (dlog * SCALE).astype(BF16)
                dq2 = _dot(dlb, kcat)
                if (h % 2) != group:
                    dq2 = pltpu.roll(dq2, HEAD_DIM, axis=1)
                halves.append(dq2)
                dkcat += _dot_tn(dlb, q_pos)
                dvcat += _dot_tn(prob.astype(BF16), do_pos)
            dq_ref[:, j * LANES:(j + 1) * LANES] = jnp.where(first, halves[0], halves[1]).astype(BF16)

        cur = pl.multiple_of(i * SWA_BLOCK, SWA_BLOCK)
        dk_ref[pl.ds(cur, SWA_BLOCK), :] += dkcat[SWA_BLOCK:, :]
        dv_ref[pl.ds(cur, SWA_BLOCK), :] += dvcat[SWA_BLOCK:, :]

        @pl.when(i > 0)
        def _():
            prv = pl.multiple_of((i - 1) * SWA_BLOCK, SWA_BLOCK)
            dk_ref[pl.ds(prv, SWA_BLOCK), :] += dkcat[:SWA_BLOCK, :]
            dv_ref[pl.ds(prv, SWA_BLOCK), :] += dvcat[:SWA_BLOCK, :]

        @pl.when(i == nb - 1)
        def _():
            bk = bucket_ref[...]
            rowi = lax.broadcasted_iota(jnp.int32, (REL_BUCKETS, LANES), 0)
            coli = lax.broadcasted_iota(jnp.int32, (REL_BUCKETS, LANES), 1)
            res = jnp.zeros((REL_BUCKETS, LANES), F32)
            for h in range(n_heads):
                db = dbias_ref[h]
                for b in range(REL_BUCKETS):
                    tot = _colsum(_rowsum(jnp.where(bk == b, db, 0.0)))
                    res = jnp.where((rowi == b) & (coli == h), tot, res)
            drb_ref[...] = res

    in_specs = _swa_specs(n_heads, qcol, kcol, vcol) + [pl.BlockSpec((SWA_BLOCK, width), lambda i: (i, 0))]
    return _call(
        body, "swa_bwd", (nb,),
        in_specs=in_specs,
        out_specs=[pl.BlockSpec((SWA_BLOCK, width), lambda i: (i, 0)),
                   _full((s, LANES)), _full((s, LANES)), _full((8, LANES)), _full((REL_BUCKETS, LANES))],
        out_shape=[jax.ShapeDtypeStruct((s, width), BF16), jax.ShapeDtypeStruct((s, LANES), F32),
                   jax.ShapeDtypeStruct((s, LANES), F32), jax.ShapeDtypeStruct((8, LANES), F32),
                   jax.ShapeDtypeStruct((REL_BUCKETS, LANES), F32)],
        args=(proj, proj, proj, proj, proj, bucket, rel_bias, sinks, dout),
        scratch_shapes=[pltpu.VMEM((n_heads, SWA_BLOCK, 2 * SWA_BLOCK), F32),
                        pltpu.VMEM((n_heads, SWA_BLOCK, 2 * SWA_BLOCK), F32)],
        comm=comm)


def _proj_bwd(dproj, w_in, du1, x, g_in, comm=None):
    s, d = x.shape
    cols = w_in.shape[1]
    tm = min(ROW_TILE, s)

    def body(dp_ref, w_ref, du_ref, x_ref, g_ref, dx_ref, st_ref):
        i = pl.program_id(0)

        @pl.when(i == 0)
        def _():
            st_ref[...] = jnp.zeros_like(st_ref)

        dh0 = _dot_nt(dp_ref[...], w_ref[...]) + ALPHA * du_ref[...]
        xhat, r = _ln_hat(x_ref[...])
        st_ref[0:1, :] += _colsum(dh0 * xhat)
        st_ref[1:2, :] += _colsum(dh0)
        dx_ref[...] = _ln_bwd(dh0 * g_ref[...], xhat, r)

    row = lambda width: pl.BlockSpec((tm, width), lambda i: (i, 0))
    return _call(
        body, "proj_bwd", (s // tm,),
        in_specs=[row(cols), _full((d, cols)), row(d), row(d), _full((1, d))],
        out_specs=[row(d), _full((8, d))],
        out_shape=[jax.ShapeDtypeStruct((s, d), F32), jax.ShapeDtypeStruct((8, d), F32)],
        args=(dproj, w_in, du1, x, g_in), comm=comm)


def _wgrad(name, a, b, tm, tn):
    s, m = a.shape
    n = b.shape[1]
    ts = min(ROW_TILE, s)

    def body(a_ref, b_ref, o_ref):
        @pl.when(pl.program_id(2) == 0)
        def _():
            o_ref[...] = jnp.zeros_like(o_ref)

        o_ref[...] += _dot_tn(a_ref[...].astype(BF16), b_ref[...].astype(BF16))

    return pl.pallas_call(
        body, name=name, grid=(m // tm, n // tn, s // ts),
        in_specs=[pl.BlockSpec((ts, tm), lambda i, j, k: (k, i)),
                  pl.BlockSpec((ts, tn), lambda i, j, k: (k, j))],
        out_specs=pl.BlockSpec((tm, tn), lambda i, j, k: (i, j)),
        out_shape=jax.ShapeDtypeStruct((m, n), F32),
        compiler_params=_params(),
    )(a, b)


def _adamw_math(w, g, m, v):
    m = ADAM_B1 * m + (1.0 - ADAM_B1) * g
    v = ADAM_B2 * v + (1.0 - ADAM_B2) * (g * g)
    m_hat = m / (1.0 - ADAM_B1 ** ADAM_STEP)
    v_hat = v / (1.0 - ADAM_B2 ** ADAM_STEP)
    delta = -ADAM_LR * (m_hat / (jnp.sqrt(v_hat) + ADAM_EPS) + ADAM_WD * w)
    return delta, m, v


def _adamw(name, landed, w, m, v, tr):
    rows, cols = w.shape

    def body(l_ref, w_ref, m_ref, v_ref, g_ref, d_ref, nm_ref, nv_ref):
        g = l_ref[0]
        for src in range(1, N_DEV):
            g = g + l_ref[src]
        delta, nm, nv = _adamw_math(w_ref[...], g, m_ref[...], v_ref[...])
        g_ref[...] = g
        d_ref[...] = delta
        nm_ref[...] = nm
        nv_ref[...] = nv

    blk = pl.BlockSpec((tr, cols), lambda i: (i, 0))
    shape = jax.ShapeDtypeStruct((rows, cols), F32)
    return pl.pallas_call(
        body, name=name, grid=(rows // tr,),
        in_specs=[pl.BlockSpec((N_DEV, tr, cols), lambda i: (0, i, 0)), blk, blk, blk],
        out_specs=[blk, blk, blk, blk],
        out_shape=[shape, shape, shape, shape],
        compiler_params=_params(),
    )(landed, w, m, v)


def _pack(d, ln_in_g, ln_in_b, ln1_g, ln1_b, ln2_g, ln2_b, sb_g, sw_g, rel_bias, sinks, extra=None):
    tail = [rel_bias.reshape(-1), sinks.reshape(-1)]
    if extra is not None:
        tail.append(extra.reshape(-1))
    tail = jnp.concatenate(tail)
    tail = jnp.concatenate([tail, jnp.zeros((d - tail.shape[0],), F32)])
    rows = [ln_in_g.reshape(-1), ln_in_b.reshape(-1), ln1_g.reshape(-1), ln1_b.reshape(-1),
            ln2_g.reshape(-1), ln2_b.reshape(-1),
            jnp.concatenate([sb_g.reshape(-1), sw_g.reshape(-1)]), tail]
    return jnp.stack(rows)


def _unpack(p, wsb, n_rb, n_sk):
    return [p[0], p[1], p[6, :wsb][None], p[6, wsb:][None], p[7, n_rb:n_rb + n_sk][None],
            p[7, :n_rb].reshape(REL_BUCKETS, -1), p[2][None], p[3][None], p[4][None], p[5][None]]


def kernel(x, ln_in_g, ln_in_b, w_in, sb_norm_g, swa_norm_g, sinks, rel_bias, w_out, ln1_g, ln1_b, w_gate_up, w_down, ln2_g, ln2_b, loss_target, m_ln_in_g, m_ln_in_b, m_w_in, m_sb_norm_g, m_swa_norm_g, m_sinks, m_rel_bias, m_w_out, m_ln1_g, m_ln1_b, m_w_gate_up, m_w_down, m_ln2_g, m_ln2_b, v_ln_in_g, v_ln_in_b, v_w_in, v_sb_norm_g, v_swa_norm_g, v_sinks, v_rel_bias, v_w_out, v_ln1_g, v_ln1_b, v_w_gate_up, v_w_down, v_ln2_g, v_ln2_b):
    x2 = x[0]
    tgt = loss_target[0]
    s, d = x2.shape
    wsb = sb_norm_g.shape[-1]
    wsw = swa_norm_g.shape[-1]
    n_sw_heads = sinks.shape[-1]
    n_pairs = wsb // LANES
    dff = w_down.shape[1] * N_DEV
    assert wsb % LANES == 0 and wsw % LANES == 0 and n_sw_heads * HEAD_DIM == wsw
    assert 3 * wsb % wsw == 0 and dff % FF_CHUNK == 0 and s % SWA_BLOCK == 0
    qcol = 3 * wsb // wsw
    kcol = (3 * wsb + wsw) // LANES
    vcol = kcol + 1
    assert w_in.shape[-1] * N_DEV == (vcol + 1) * LANES

    big_w = [w_in[0], w_out[0], w_gate_up[0], w_down[0]]
    big_m = [m_w_in[0], m_w_out[0], m_w_gate_up[0], m_w_down[0]]
    big_v = [v_w_in[0], v_w_out[0], v_w_gate_up[0], v_w_down[0]]

    cat_cols = lambda g: jnp.transpose(g, (1, 0, 2)).reshape(g.shape[1], N_DEV * g.shape[2])
    cat_rows = lambda g: g.reshape(N_DEV * g.shape[1], g.shape[2])
    shards = [w.astype(BF16) for w in big_w]
    w_in_f = cat_cols(_exchange("w_in_allgather", shards[:1], ["gather"])[0])

    vec = lambda a: a.reshape(1, -1)
    g_in, b_in = vec(ln_in_g), vec(ln_in_b)
    bucket = jnp.asarray(_swa_bucket_table())

    h0b, proj = _ln_proj(x2, g_in, b_in, w_in_f)
    sb_out, gathered = _sb_fwd(proj, n_pairs, comm=(shards[1:], ["gather"] * 3))
    w_out_f, w_gu_f, w_down_f = cat_rows(gathered[0]), cat_cols(gathered[1]), cat_rows(gathered[2])
    sw_out = _swa_fwd(proj, bucket, rel_bias, sinks, n_sw_heads, qcol, kcol, vcol)
    merged, u1 = _mix_ln1(sb_out, sw_out, x2, g_in, b_in, sb_norm_g, swa_norm_g, w_out_f)
    du2, st_ln2 = _ffn_fwd(u1, ln1_g, ln1_b, w_gu_f, w_down_f, ln2_g, ln2_b, tgt)

    split_cols = lambda g: jnp.transpose(g.reshape(g.shape[0], N_DEV, g.shape[1] // N_DEV), (1, 0, 2))
    split_rows = lambda g: g.reshape(N_DEV, g.shape[0] // N_DEV, g.shape[1])
    tile_m = min(512, d)
    du1, h1b, act, dgate, dup, st_ln1 = _ffn_bwd(u1, du2, ln1_g, ln1_b, w_gu_f, w_down_f)
    gw_gate = _wgrad("wgrad_gate", h1b, dgate, tile_m, dff // 2)
    gw_up = _wgrad("wgrad_up", h1b, dup, tile_m, dff // 2)
    gw_down = _wgrad("wgrad_down", act, du2, dff // 2, d)
    gw_out = _wgrad("wgrad_out", merged, du1, tile_m, d)
    dsb, dsw, st_rms = _mix_bwd(du1, w_out_f, sb_out, sw_out, sb_norm_g, swa_norm_g)
    (dq_sb, dk_sb, dv_sb), (land_gu, land_out) = _sb_bwd(
        proj, dsb, sb_out, n_pairs,
        comm=([split_cols(jnp.concatenate([gw_gate, gw_up], axis=1)), split_rows(gw_out)], ["scatter"] * 2))
    (dq_sw, dk_sw, dv_sw, st_sink, st_rb), (land_down,) = _swa_bwd(
        proj, dsw, bucket, rel_bias, sinks, n_sw_heads, qcol, kcol, vcol,
        comm=([split_rows(gw_down)], ["scatter"]))
    dproj = jnp.concatenate([dq_sb, dk_sb.astype(BF16), dv_sb.astype(BF16), dq_sw,
                             dk_sw.astype(BF16), dv_sw.astype(BF16)], axis=1)
    gw_in = _wgrad("wgrad_in", h0b, dproj, tile_m, dproj.shape[1] // 2)
    (grad_x, st_in), (land_in,) = _proj_bwd(dproj, w_in_f, du1, x2, g_in, comm=([split_cols(gw_in)], ["scatter"]))

    n_rb = rel_bias.size
    small = _pack(d, st_in[0], st_in[1], st_ln1[0], st_ln1[1], st_ln2[0], st_ln2[1],
                  st_rms[0, :wsb], st_rms[0, wsb:], st_rb[:, :n_sw_heads], st_sink[:n_sw_heads, 0],
                  extra=st_ln2[2, 0:1])
    land_small = _exchange("small_grads_allgather", [small], ["gather"])[0]
    landed = [land_in, land_out, land_gu, land_down, land_small]

    big = []
    for name, land, w, m, v in zip(["adamw_in", "adamw_out", "adamw_gate_up", "adamw_down"], landed[:4], big_w, big_m, big_v):
        rows = w.shape[0]
        tr = 128 if rows % 128 == 0 else rows // 4
        big.append(_adamw(name, land, w, m, v, tr))

    small_w = _pack(d, ln_in_g, ln_in_b, ln1_g, ln1_b, ln2_g, ln2_b, sb_norm_g, swa_norm_g, rel_bias, sinks)
    small_m = _pack(d, m_ln_in_g, m_ln_in_b, m_ln1_g, m_ln1_b, m_ln2_g, m_ln2_b, m_sb_norm_g, m_swa_norm_g, m_rel_bias, m_sinks)
    small_v = _pack(d, v_ln_in_g, v_ln_in_b, v_ln1_g, v_ln1_b, v_ln2_g, v_ln2_b, v_sb_norm_g, v_swa_norm_g, v_rel_bias, v_sinks)
    sg, sd, sm, sv = _adamw("adamw_small", landed[4], small_w, small_m, small_v, 8)
    n_sk = sinks.size
    loss = sg[7, n_rb + n_sk]

    def leaves(idx):
        sm_l = _unpack([sg, sd, sm, sv][idx], wsb, n_rb, n_sk)
        bg = [b[idx][None] for b in big]
        return [sm_l[0], sm_l[1], bg[0], sm_l[2], sm_l[3], sm_l[4], sm_l[5], bg[1], sm_l[6], sm_l[7], bg[2], bg[3], sm_l[8], sm_l[9]]

    return (loss, grad_x[None], *leaves(0), *leaves(1), *leaves(2), *leaves(3))
```

```python
import functools
import math

import numpy as np
import jax
import jax.numpy as jnp
from jax import lax
from jax.experimental import pallas as pl
from jax.experimental.pallas import tpu as pltpu

F32 = jnp.float32
BF16 = jnp.bfloat16
MESH = pl.DeviceIdType.MESH

N_DEV = 8
LANES = 128
HEAD_DIM = 64
SCALE = HEAD_DIM ** -0.5
SWA_BLOCK = 128
REL_BUCKETS = 32
REL_MAX_DIST = 128
ALPHA = 2.0 ** 0.25
LN_EPS = 1e-5
RMS_EPS = 1e-6
ADAM_LR = 0.001
ADAM_B1 = 0.9
ADAM_B2 = 0.999
ADAM_EPS = 1e-08
ADAM_WD = 0.01
ADAM_STEP = 10

ROW_TILE = 512
SB_TILE = 256
FFN_TILE = 256
WGRAD_TOKENS = 2048
SB_UNDERFLOW = -110.0
MIB = 1024 * 1024


def _params(vmem_mib=48):
    return pltpu.CompilerParams(vmem_limit_bytes=vmem_mib * MIB)


def _dot(a, b):
    return jnp.dot(a, b, preferred_element_type=F32)


def _dot_nt(a, b):
    return lax.dot_general(a, b, (((1,), (1,)), ((), ())), preferred_element_type=F32)


def _dot_tn(a, b):
    return lax.dot_general(a, b, (((0,), (0,)), ((), ())), preferred_element_type=F32)


def _ln_hat(x):
    mu = jnp.mean(x, axis=-1, keepdims=True)
    xc = x - mu
    var = jnp.mean(xc * xc, axis=-1, keepdims=True)
    r = lax.rsqrt(var + LN_EPS)
    return xc * r, r


def _ln_bwd(dxhat, xhat, r):
    return r * (dxhat - jnp.mean(dxhat, axis=-1, keepdims=True)
                - xhat * jnp.mean(dxhat * xhat, axis=-1, keepdims=True))


def _colsum(a):
    return jnp.sum(a, axis=0, keepdims=True)


def _rowsum(a):
    return jnp.sum(a, axis=1, keepdims=True)


def _full(shape):
    return pl.BlockSpec(shape, lambda *_: (0,) * len(shape))


def _comm_out_shapes(arrays, kinds):
    shapes = []
    for a, kind in zip(arrays, kinds):
        blk = a.shape if kind == "gather" else a.shape[1:]
        shapes.append(jax.ShapeDtypeStruct((N_DEV,) + tuple(blk), a.dtype))
    return shapes


def _comm_sems(n):
    return [pltpu.SemaphoreType.DMA((n, N_DEV - 1)), pltpu.SemaphoreType.DMA((n, N_DEV - 1)),
            pltpu.SemaphoreType.DMA((n,))]


def _comm_copies(ins, outs, kinds, send_sems, recv_sems, local_sems):
    x, y, c = lax.axis_index("x"), lax.axis_index("y"), lax.axis_index("c")
    me = 4 * x + 2 * y + c

    def src_for(t, dev_lin):
        return ins[t] if kinds[t] == "gather" else ins[t].at[dev_lin]

    local = [pltpu.make_async_copy(src_for(t, me), outs[t].at[me], local_sems.at[t]) for t in range(len(kinds))]
    sends, arrivals = [], []
    for k in range(1, N_DEV):
        px = 1 - x if (k >> 2) & 1 else x
        py = 1 - y if (k >> 1) & 1 else y
        pc = 1 - c if k & 1 else c
        peer_lin = 4 * px + 2 * py + pc
        for t in range(len(kinds)):
            sems = dict(send_sem=send_sems.at[t, k - 1], recv_sem=recv_sems.at[t, k - 1],
                        device_id=(px, py, pc), device_id_type=MESH)
            sends.append(pltpu.make_async_remote_copy(src_ref=src_for(t, peer_lin), dst_ref=outs[t].at[me], **sems))
            arrivals.append(pltpu.make_async_remote_copy(src_ref=src_for(t, peer_lin), dst_ref=outs[t].at[peer_lin], **sems))
    return local, sends, arrivals


def _comm_start(ins, outs, kinds, sems):
    local, sends, _ = _comm_copies(ins, outs, kinds, *sems)
    for cp in local + sends:
        cp.start()


def _comm_finish(ins, outs, kinds, sems):
    local, sends, arrivals = _comm_copies(ins, outs, kinds, *sems)
    for cp in arrivals:
        cp.wait_recv()
    for cp in sends:
        cp.wait_send()
    for cp in local:
        cp.wait()


def _exchange(name, arrays, kinds):
    n = len(arrays)

    def body(*refs):
        ins, outs, sems = refs[:n], refs[n:2 * n], refs[2 * n:]
        _comm_start(ins, outs, kinds, sems)
        _comm_finish(ins, outs, kinds, sems)

    any_spec = pl.BlockSpec(memory_space=pl.ANY)
    return pl.pallas_call(
        body, name=name, out_shape=_comm_out_shapes(arrays, kinds),
        in_specs=[any_spec] * n, out_specs=[any_spec] * n,
        scratch_shapes=_comm_sems(n),
    )(*arrays)


def _call(body, name, grid, in_specs, out_specs, out_shape, args, scratch_shapes=(), comm=None):
    if comm is None:
        outs = pl.pallas_call(body, name=name, grid=grid, in_specs=in_specs, out_specs=out_specs,
                              out_shape=out_shape, scratch_shapes=list(scratch_shapes),
                              compiler_params=_params())(*args)
        return outs, []
    arrays, kinds = comm
    n, n_in, n_out, n_scr = len(arrays), len(in_specs), len(out_specs), len(scratch_shapes)

    def fused(*refs):
        c_in, x_in = refs[:n_in], refs[n_in:n_in + n]
        c_out = refs[n_in + n:n_in + n + n_out]
        x_out = refs[n_in + n + n_out:n_in + 2 * n + n_out]
        rest = refs[n_in + 2 * n + n_out:]
        c_scr, sems = rest[:n_scr], rest[n_scr:]
        ids = [pl.program_id(a) for a in range(len(grid))]
        is_first = functools.reduce(jnp.logical_and, [i == 0 for i in ids])
        is_last = functools.reduce(jnp.logical_and, [i == g - 1 for i, g in zip(ids, grid)])

        @pl.when(is_first)
        def _():
            _comm_start(x_in, x_out, kinds, sems)

        body(*c_in, *c_out, *c_scr)

        @pl.when(is_last)
        def _():
            _comm_finish(x_in, x_out, kinds, sems)

    any_spec = pl.BlockSpec(memory_space=pl.ANY)
    outs = pl.pallas_call(
        fused, name=name, grid=grid,
        in_specs=list(in_specs) + [any_spec] * n, out_specs=list(out_specs) + [any_spec] * n,
        out_shape=list(out_shape) + _comm_out_shapes(arrays, kinds),
        scratch_shapes=list(scratch_shapes) + _comm_sems(n),
        compiler_params=_params())(*args, *arrays)
    return outs[:n_out], outs[n_out:]


def _ln_proj(x, g, b, w_in):
    s, d = x.shape
    cols = w_in.shape[1]
    tm = min(ROW_TILE, s)

    def body(x_ref, g_ref, b_ref, w_ref, h_ref, p_ref):
        xhat, _ = _ln_hat(x_ref[...])
        h = (xhat * g_ref[...] + b_ref[...]).astype(BF16)
        h_ref[...] = h
        p_ref[...] = _dot(h, w_ref[...]).astype(BF16)

    row = lambda width: pl.BlockSpec((tm, width), lambda i: (i, 0))
    return pl.pallas_call(
        body, name="ln_proj", grid=(s // tm,),
        in_specs=[row(d), _full((1, d)), _full((1, d)), _full((d, cols))],
        out_specs=[row(d), row(cols)],
        out_shape=[jax.ShapeDtypeStruct((s, d), BF16), jax.ShapeDtypeStruct((s, cols), BF16)],
        compiler_params=_params(),
    )(x, g, b, w_in)


def _sb_tile_consts(t):
    row = lax.broadcasted_iota(jnp.int32, (t, t), 0)
    col = lax.broadcasted_iota(jnp.int32, (t, t), 1)
    return row, col


def _sb_scores(qh, k_t, upper, carry_l, causal):
    z = _dot_nt(qh, k_t) * SCALE
    sp = jnp.log(1.0 + jnp.exp(-jnp.abs(z)))
    lb = jnp.minimum(z, 0.0) - sp
    l1 = -(jnp.maximum(z, 0.0) + sp)
    if causal is not None:
        l1 = jnp.where(causal, l1, 0.0)
    hi = l1.astype(BF16)
    lo = (l1 - hi.astype(F32)).astype(BF16)
    suf = _dot(hi, upper) + _dot(lo, upper) + carry_l
    a = jnp.exp(lb + suf)
    if causal is not None:
        a = jnp.where(causal, a, 0.0)
    return lb, l1, a


def _sb_walk(i, t, visit, init, causal):
    def alive(carry):
        return jnp.max(jnp.maximum(carry[0][0], carry[1][0])) > SB_UNDERFLOW

    carry = visit(pl.multiple_of(i * t, t), init, causal)

    def cond(state):
        j, go, _ = state
        return (j < i) & go

    def body(state):
        j, _, carry = state
        carry = visit(pl.multiple_of((i - 1 - j) * t, t), carry, None)
        return j + 1, alive(carry), carry

    return lax.while_loop(cond, body, (jnp.int32(0), alive(carry), carry))[2]


def _sb_fwd(proj, n_pairs, comm=None):
    s = proj.shape[0]
    t = min(SB_TILE, s)
    nq = s // t

    def body(q_ref, k_ref, v_ref, o_ref):
        i = pl.program_id(1)
        lane = lax.broadcasted_iota(jnp.int32, (1, LANES), 1)
        first = lane < HEAD_DIM
        q2 = q_ref[...]
        zero = jnp.zeros_like(q2)
        qs = (jnp.where(first, q2, zero), jnp.where(first, zero, q2))
        row, col = _sb_tile_consts(t)
        upper = (row > col).astype(BF16)
        causal = col < row

        def visit(off, carry, mask):
            k_t = k_ref[pl.ds(off, t), :]
            v_t = v_ref[pl.ds(off, t), :]
            out = []
            for hh in range(2):
                c_l, acc = carry[hh]
                _, l1, a = _sb_scores(qs[hh], k_t, upper, c_l, mask)
                out.append((c_l + _rowsum(l1), acc + _dot(a.astype(BF16), v_t)))
            return tuple(out)

        init = tuple((jnp.zeros((t, 1), F32), jnp.zeros((t, LANES), F32)) for _ in range(2))
        carry = _sb_walk(i, t, visit, init, causal)
        o_ref[...] = jnp.where(first, carry[0][1], carry[1][1])

    outs, landed = _call(
        body, "sb_fwd", (n_pairs, nq),
        in_specs=[pl.BlockSpec((t, LANES), lambda h, i: (i, h)),
                  pl.BlockSpec((s, LANES), lambda h, i: (0, n_pairs + h)),
                  pl.BlockSpec((s, LANES), lambda h, i: (0, 2 * n_pairs + h))],
        out_specs=[pl.BlockSpec((t, LANES), lambda h, i: (i, h))],
        out_shape=[jax.ShapeDtypeStruct((s, n_pairs * LANES), F32)],
        args=(proj, proj, proj), comm=comm)
    return outs[0], landed


def _swa_bucket_table():
    qi = np.arange(SWA_BLOCK)[:, None]
    cj = np.arange(2 * SWA_BLOCK)[None, :]
    dist = qi + SWA_BLOCK - cj
    exact = REL_BUCKETS // 2
    d = np.maximum(dist, 0)
    d_f = np.maximum(d, 1).astype(np.float32)
    large = exact + (np.log(d_f / np.float32(exact)) / np.float32(math.log(REL_MAX_DIST / exact))
                     * np.float32(REL_BUCKETS - exact)).astype(np.int32)
    large = np.minimum(large, REL_BUCKETS - 1)
    return np.where(d < exact, d, large).astype(np.int32)


def _swa_build_bias(bucket_ref, rb_ref, bias_ref, n_heads):
    bk = bucket_ref[...]
    for h in range(n_heads):
        acc = jnp.zeros(bk.shape, F32)
        for b in range(REL_BUCKETS):
            acc = jnp.where(bk == b, rb_ref[b, h], acc)
        bias_ref[h] = acc


def _swa_valid(i):
    row = lax.broadcasted_iota(jnp.int32, (SWA_BLOCK, 2 * SWA_BLOCK), 0)
    col = lax.broadcasted_iota(jnp.int32, (SWA_BLOCK, 2 * SWA_BLOCK), 1)
    dist = row + SWA_BLOCK - col
    return (dist >= 0) & (dist < SWA_BLOCK) & ((col >= SWA_BLOCK) | (i > 0))


def _swa_place(blk, h, group, sel):
    if (h % 2) != group:
        blk = pltpu.roll(blk.astype(F32), HEAD_DIM, axis=1).astype(BF16)
    return jnp.where(sel, blk, jnp.zeros_like(blk))


def _swa_probs(q_pos, kcat, bias_h, valid, sink):
    logits = _dot_nt(q_pos, kcat) * SCALE + bias_h
    logits = jnp.where(valid, logits, -jnp.inf)
    m = jnp.maximum(jnp.max(logits, axis=1, keepdims=True), sink)
    p = jnp.exp(logits - m)
    es = jnp.exp(sink - m)
    denom = _rowsum(p) + es
    return p / denom, es / denom


def _swa_specs(n_heads, qcol, kcol, vcol):
    width = n_heads * HEAD_DIM
    prev = lambda col: pl.BlockSpec((SWA_BLOCK, LANES), lambda i: (jnp.maximum(i - 1, 0), col))
    cur = lambda col: pl.BlockSpec((SWA_BLOCK, LANES), lambda i: (i, col))
    return [pl.BlockSpec((SWA_BLOCK, width), lambda i: (i, qcol)),
            prev(kcol), cur(kcol), prev(vcol), cur(vcol),
            _full((SWA_BLOCK, 2 * SWA_BLOCK)),
            pl.BlockSpec(memory_space=pltpu.SMEM), pl.BlockSpec(memory_space=pltpu.SMEM)]


def _swa_fwd(proj, bucket, rel_bias, sinks, n_heads, qcol, kcol, vcol):
    s = proj.shape[0]
    width = n_heads * HEAD_DIM
    n_groups = max(1, n_heads // 4)
    per_group = n_heads // n_groups

    def body(q_ref, kp_ref, kc_ref, vp_ref, vc_ref, bucket_ref, rb_ref, sk_ref, o_ref, bias_ref):
        i = pl.program_id(0)

        @pl.when(i == 0)
        def _():
            _swa_build_bias(bucket_ref, rb_ref, bias_ref, n_heads)

        lane = lax.broadcasted_iota(jnp.int32, (1, LANES), 1)
        first = lane < HEAD_DIM
        valid = _swa_valid(i)
        kcat = jnp.concatenate([kp_ref[...], kc_ref[...]], axis=0)
        vcat = jnp.concatenate([vp_ref[...], vc_ref[...]], axis=0)
        for j in range(n_heads // 2):
            halves = []
            for h in (2 * j, 2 * j + 1):
                group = h // per_group
                sel = first if group == 0 else jnp.logical_not(first)
                q_pos = _swa_place(q_ref[:, j * LANES:(j + 1) * LANES], h, group, sel)
                prob, _ = _swa_probs(q_pos, kcat, bias_ref[h], valid, sk_ref[0, h])
                o2 = _dot(prob.astype(BF16), vcat)
                if (h % 2) != group:
                    o2 = pltpu.roll(o2, HEAD_DIM, axis=1)
                halves.append(o2)
            o_ref[:, j * LANES:(j + 1) * LANES] = jnp.where(first, halves[0], halves[1])

    return pl.pallas_call(
        body, name="swa_fwd", grid=(s // SWA_BLOCK,),
        in_specs=_swa_specs(n_heads, qcol, kcol, vcol),
        out_specs=pl.BlockSpec((SWA_BLOCK, width), lambda i: (i, 0)),
        out_shape=jax.ShapeDtypeStruct((s, width), F32),
        scratch_shapes=[pltpu.VMEM((n_heads, SWA_BLOCK, 2 * SWA_BLOCK), F32)],
        compiler_params=_params(),
    )(proj, proj, proj, proj, proj, bucket, rel_bias, sinks)


def _rms_fwd(o, g):
    r = lax.rsqrt(jnp.mean(o * o, axis=-1, keepdims=True) + RMS_EPS)
    n = o * r
    return n, r, n * g


def _mix_ln1(sb_out, sw_out, x, g_in, b_in, sb_g, sw_g, w_out):
    s, d = x.shape
    wsb, wsw = sb_out.shape[1], sw_out.shape[1]
    tm = min(ROW_TILE, s)

    def body(sb_ref, sw_ref, x_ref, gi_ref, bi_ref, sbg_ref, swg_ref, w_ref, mg_ref, u_ref):
        _, _, m_sb = _rms_fwd(sb_ref[...], sbg_ref[...])
        _, _, m_sw = _rms_fwd(sw_ref[...], swg_ref[...])
        m_sb = m_sb.astype(BF16)
        m_sw = m_sw.astype(BF16)
        mg_ref[:, :wsb] = m_sb
        mg_ref[:, wsb:] = m_sw
        mix = _dot(m_sb, w_ref[:wsb, :]) + _dot(m_sw, w_ref[wsb:, :])
        xhat, _ = _ln_hat(x_ref[...])
        h0 = xhat * gi_ref[...] + bi_ref[...]
        u_ref[...] = ALPHA * h0 + mix

    row = lambda width: pl.BlockSpec((tm, width), lambda i: (i, 0))
    return pl.pallas_call(
        body, name="mix_ln1", grid=(s // tm,),
        in_specs=[row(wsb), row(wsw), row(d), _full((1, d)), _full((1, d)),
                  _full((1, wsb)), _full((1, wsw)), _full((wsb + wsw, d))],
        out_specs=[row(wsb + wsw), row(d)],
        out_shape=[jax.ShapeDtypeStruct((s, wsb + wsw), BF16), jax.ShapeDtypeStruct((s, d), F32)],
        compiler_params=_params(),
    )(sb_out, sw_out, x, g_in, b_in, sb_g, sw_g, w_out)


def _ffn_fwd(u1, g1, b1, w_gu, w_down, g2, b2, target):
    s, d = u1.shape
    dff = w_down.shape[0]
    tm = min(FFN_TILE, s)

    def body(u_ref, g1_ref, b1_ref, wgu_hbm, wd_hbm, g2_ref, b2_ref, t_ref, du_ref, st_ref, wgu_ref, wd_ref):
        @pl.when(pl.program_id(0) == 0)
        def _():
            pltpu.sync_copy(wgu_hbm, wgu_ref)
            pltpu.sync_copy(wd_hbm, wd_ref)
            st_ref[...] = jnp.zeros_like(st_ref)

        xhat, _ = _ln_hat(u_ref[...])
        h1 = xhat * g1_ref[...] + b1_ref[...]
        h1b = h1.astype(BF16)
        gate = _dot(h1b, wgu_ref[:, :dff])
        up = _dot(h1b, wgu_ref[:, dff:])
        act = gate * jax.nn.sigmoid(gate) * up
        u2 = ALPHA * h1 + _dot(act.astype(BF16), wd_ref[...])
        xhat2, r2 = _ln_hat(u2)
        diff = xhat2 * g2_ref[...] + b2_ref[...] - t_ref[...]
        dh2 = diff * (1.0 / d)
        st_ref[0:1, :] += _colsum(dh2 * xhat2)
        st_ref[1:2, :] += _colsum(dh2)
        st_ref[2:3, :] += jnp.broadcast_to(_colsum(_rowsum(diff * diff)) * (0.5 / d), (1, d))
        du_ref[...] = _ln_bwd(dh2 * g2_ref[...], xhat2, r2)

    row = pl.BlockSpec((tm, d), lambda i: (i, 0))
    hbm = pl.BlockSpec(memory_space=pl.ANY)
    return pl.pallas_call(
        body, name="ffn_fwd", grid=(s // tm,),
        in_specs=[row, _full((1, d)), _full((1, d)), hbm, hbm, _full((1, d)), _full((1, d)), row],
        out_specs=[row, _full((8, d))],
        out_shape=[jax.ShapeDtypeStruct((s, d), F32), jax.ShapeDtypeStruct((8, d), F32)],
        scratch_shapes=[pltpu.VMEM(w_gu.shape, BF16), pltpu.VMEM(w_down.shape, BF16)],
        compiler_params=_params(56),
    )(u1, g1, b1, w_gu, w_down, g2, b2, target)


def _ffn_bwd_act(u1, du2, g1, b1, w_gu, w_down_t):
    s, d = u1.shape
    dff = w_down_t.shape[1]
    tm = min(FFN_TILE, s)

    def body(u_ref, du2_ref, g1_ref, b1_ref, wgu_hbm, wdt_hbm, h1b_ref, act_ref, dgu_ref, wgu_ref, wdt_ref):
        @pl.when(pl.program_id(0) == 0)
        def _():
            pltpu.sync_copy(wgu_hbm, wgu_ref)
            pltpu.sync_copy(wdt_hbm, wdt_ref)

        xhat, _ = _ln_hat(u_ref[...])
        h1b = (xhat * g1_ref[...] + b1_ref[...]).astype(BF16)
        h1b_ref[...] = h1b
        gate = _dot(h1b, wgu_ref[:, :dff])
        up = _dot(h1b, wgu_ref[:, dff:])
        dact = _dot(du2_ref[...].astype(BF16), wdt_ref[...])
        sg = jax.nn.sigmoid(gate)
        silu = gate * sg
        act_ref[...] = (silu * up).astype(BF16)
        dgu_ref[:, :dff] = (dact * up * (sg * (1.0 + gate * (1.0 - sg)))).astype(BF16)
        dgu_ref[:, dff:] = (dact * silu).astype(BF16)

    row = lambda width: pl.BlockSpec((tm, width), lambda i: (i, 0))
    hbm = pl.BlockSpec(memory_space=pl.ANY)
    return pl.pallas_call(
        body, name="ffn_bwd_act", grid=(s // tm,),
        in_specs=[row(d), row(d), _full((1, d)), _full((1, d)), hbm, hbm],
        out_specs=[row(d), row(dff), row(2 * dff)],
        out_shape=[jax.ShapeDtypeStruct((s, d), BF16), jax.ShapeDtypeStruct((s, dff), BF16),
                   jax.ShapeDtypeStruct((s, 2 * dff), BF16)],
        scratch_shapes=[pltpu.VMEM(w_gu.shape, BF16), pltpu.VMEM(w_down_t.shape, BF16)],
        compiler_params=_params(56),
    )(u1, du2, g1, b1, w_gu, w_down_t)


def _ffn_bwd_in(dgu, w_gu_t, u1, du2, g1):
    s, d = u1.shape
    tm = min(FFN_TILE, s)

    def body(dgu_ref, wt_hbm, u_ref, du2_ref, g1_ref, du1_ref, st_ref, wt_ref):
        @pl.when(pl.program_id(0) == 0)
        def _():
            pltpu.sync_copy(wt_hbm, wt_ref)
            st_ref[...] = jnp.zeros_like(st_ref)

        dh1 = _dot(dgu_ref[...], wt_ref[...]) + ALPHA * du2_ref[...]
        xhat, r = _ln_hat(u_ref[...])
        st_ref[0:1, :] += _colsum(dh1 * xhat)
        st_ref[1:2, :] += _colsum(dh1)
        du1_ref[...] = _ln_bwd(dh1 * g1_ref[...], xhat, r)

    row = lambda width: pl.BlockSpec((tm, width), lambda i: (i, 0))
    return pl.pallas_call(
        body, name="ffn_bwd_in", grid=(s // tm,),
        in_specs=[row(dgu.shape[1]), pl.BlockSpec(memory_space=pl.ANY), row(d), row(d), _full((1, d))],
        out_specs=[row(d), _full((8, d))],
        out_shape=[jax.ShapeDtypeStruct((s, d), F32), jax.ShapeDtypeStruct((8, d), F32)],
        scratch_shapes=[pltpu.VMEM(w_gu_t.shape, BF16)],
        compiler_params=_params(56),
    )(dgu, w_gu_t, u1, du2, g1)


def _rms_bwd(dm, o, g):
    n, r, _ = _rms_fwd(o, g)
    dn = dm * g
    return r * (dn - n * jnp.mean(dn * n, axis=-1, keepdims=True)), _colsum(dm * n)


def _mix_bwd(du1, w_out, sb_out, sw_out, sb_g, sw_g):
    s, d = du1.shape
    wsb, wsw = sb_out.shape[1], sw_out.shape[1]
    tm = min(ROW_TILE, s)

    def body(du_ref, w_ref, sb_ref, sw_ref, sbg_ref, swg_ref, dsb_ref, dsw_ref, st_ref):
        i = pl.program_id(0)

        @pl.when(i == 0)
        def _():
            st_ref[...] = jnp.zeros_like(st_ref)

        dmerged = _dot_nt(du_ref[...].astype(BF16), w_ref[...])
        dsb, gsb = _rms_bwd(dmerged[:, :wsb], sb_ref[...], sbg_ref[...])
        dsw, gsw = _rms_bwd(dmerged[:, wsb:], sw_ref[...], swg_ref[...])
        dsb_ref[...] = dsb.astype(BF16)
        dsw_ref[...] = dsw.astype(BF16)
        st_ref[0:1, :wsb] += gsb
        st_ref[0:1, wsb:] += gsw

    row = lambda width: pl.BlockSpec((tm, width), lambda i: (i, 0))
    return pl.pallas_call(
        body, name="mix_bwd", grid=(s // tm,),
        in_specs=[row(d), _full((wsb + wsw, d)), row(wsb), row(wsw), _full((1, wsb)), _full((1, wsw))],
        out_specs=[row(wsb), row(wsw), _full((8, wsb + wsw))],
        out_shape=[jax.ShapeDtypeStruct((s, wsb), BF16), jax.ShapeDtypeStruct((s, wsw), BF16),
                   jax.ShapeDtypeStruct((8, wsb + wsw), F32)],
        compiler_params=_params(),
    )(du1, w_out, sb_out, sw_out, sb_g, sw_g)


def _sb_bwd(proj, dout, out, n_pairs, comm=None):
    s = proj.shape[0]
    t = min(SB_TILE, s)
    nq = s // t
    width = n_pairs * LANES

    def body(q_ref, k_ref, v_ref, do_ref, o_ref, dq_ref, dk_ref, dv_ref):
        i = pl.program_id(1)

        @pl.when(i == 0)
        def _():
            dk_ref[...] = jnp.zeros_like(dk_ref)
            dv_ref[...] = jnp.zeros_like(dv_ref)

        lane = lax.broadcasted_iota(jnp.int32, (1, LANES), 1)
        first = lane < HEAD_DIM
        q2 = q_ref[...]
        do2 = do_ref[...]
        zero = jnp.zeros_like(q2)
        qs = (jnp.where(first, q2, zero), jnp.where(first, zero, q2))
        dos = (jnp.where(first, do2, zero), jnp.where(first, zero, do2))
        prod = do2.astype(F32) * o_ref[...]
        totals = (_rowsum(jnp.where(first, prod, 0.0)), _rowsum(jnp.where(first, 0.0, prod)))
        row, col = _sb_tile_consts(t)
        upper = (row > col).astype(BF16)
        incl = (row >= col).astype(BF16)
        causal = col < row

        def visit(off, carry, mask):
            k_t = k_ref[pl.ds(off, t), :]
            v_t = v_ref[pl.ds(off, t), :]
            out = []
            dk_t = jnp.zeros((t, LANES), F32)
            dv_t = jnp.zeros((t, LANES), F32)
            for hh in range(2):
                c_l, c_e, dq = carry[hh]
                lb, l1, a = _sb_scores(qs[hh], k_t, upper, c_l, mask)
                a_b = a.astype(BF16)
                d_e = _dot_nt(dos[hh], v_t) * a_b.astype(F32)
                d_l = totals[hh] - (_dot(d_e.astype(BF16), incl) + c_e)
                dz = d_e - jnp.exp(lb) * (d_e + d_l)
                if mask is not None:
                    dz = jnp.where(mask, dz, 0.0)
                dzb = (dz * SCALE).astype(BF16)
                dk_t += _dot_tn(dzb, qs[hh])
                dv_t += _dot_tn(a_b, dos[hh])
                out.append((c_l + _rowsum(l1), c_e + _rowsum(d_e), dq + _dot(dzb, k_t)))
            dk_ref[pl.ds(off, t), :] += dk_t
            dv_ref[pl.ds(off, t), :] += dv_t
            return tuple(out)

        init = tuple((jnp.zeros((t, 1), F32), jnp.zeros((t, 1), F32), jnp.zeros((t, LANES), F32))
                     for _ in range(2))
        carry = _sb_walk(i, t, visit, init, causal)
        dq_ref[...] = jnp.where(first, carry[0][2], carry[1][2]).astype(BF16)

    qblk = pl.BlockSpec((t, LANES), lambda h, i: (i, h))
    whole = pl.BlockSpec((s, LANES), lambda h, i: (0, h))
    return _call(
        body, "sb_bwd", (n_pairs, nq),
        in_specs=[qblk,
                  pl.BlockSpec((s, LANES), lambda h, i: (0, n_pairs + h)),
                  pl.BlockSpec((s, LANES), lambda h, i: (0, 2 * n_pairs + h)),
                  qblk, qblk],
        out_specs=[qblk, whole, whole],
        out_shape=[jax.ShapeDtypeStruct((s, width), BF16), jax.ShapeDtypeStruct((s, width), F32),
                   jax.ShapeDtypeStruct((s, width), F32)],
        args=(proj, proj, proj, dout, out), comm=comm)


def _swa_bwd(proj, dout, bucket, rel_bias, sinks, n_heads, qcol, kcol, vcol, comm=None):
    s = proj.shape[0]
    width = n_heads * HEAD_DIM
    n_groups = max(1, n_heads // 4)
    per_group = n_heads // n_groups
    nb = s // SWA_BLOCK

    def body(q_ref, kp_ref, kc_ref, vp_ref, vc_ref, bucket_ref, rb_ref, sk_ref, do_ref,
             dq_ref, dk_ref, dv_ref, dsk_ref, drb_ref, bias_ref, dbias_ref):
        i = pl.program_id(0)

        @pl.when(i == 0)
        def _():
            _swa_build_bias(bucket_ref, rb_ref, bias_ref, n_heads)
            dbias_ref[...] = jnp.zeros_like(dbias_ref)
            dk_ref[...] = jnp.zeros_like(dk_ref)
            dv_ref[...] = jnp.zeros_like(dv_ref)
            dsk_ref[...] = jnp.zeros_like(dsk_ref)

        lane = lax.broadcasted_iota(jnp.int32, (1, LANES), 1)
        first = lane < HEAD_DIM
        valid = _swa_valid(i)
        kcat = jnp.concatenate([kp_ref[...], kc_ref[...]], axis=0)
        vcat = jnp.concatenate([vp_ref[...], vc_ref[...]], axis=0)
        dkcat = jnp.zeros((2 * SWA_BLOCK, LANES), F32)
        dvcat = jnp.zeros((2 * SWA_BLOCK, LANES), F32)
        for j in range(n_heads // 2):
            halves = []
            for h in (2 * j, 2 * j + 1):
                group = h // per_group
                sel = first if group == 0 else jnp.logical_not(first)
                q_pos = _swa_place(q_ref[:, j * LANES:(j + 1) * LANES], h, group, sel)
                do_pos = _swa_place(do_ref[:, j * LANES:(j + 1) * LANES], h, group, sel)
                prob, p_sink = _swa_probs(q_pos, kcat, bias_ref[h], valid, sk_ref[0, h])
                dprob = _dot_nt(do_pos, vcat)
                delta = _rowsum(prob * dprob)
                dlog = prob * (dprob - delta)
                dsk_ref[h:h + 1, :] += jnp.broadcast_to(-_colsum(p_sink * delta), (1, LANES))
                dbias_ref[h] += dlog
                dlb = (dlog * SCALE).astype(BF16)
                dq2 = _dot(dlb, kcat)
                if (h % 2) != group:
                    dq2 = pltpu.roll(dq2, HEAD_DIM, axis=1)
                halves.append(dq2)
                dkcat += _dot_tn(dlb, q_pos)
                dvcat += _dot_tn(prob.astype(BF16), do_pos)
            dq_ref[:, j * LANES:(j + 1) * LANES] = jnp.where(first, halves[0], halves[1]).astype(BF16)

        cur = pl.multiple_of(i * SWA_BLOCK, SWA_BLOCK)
        dk_ref[pl.ds(cur, SWA_BLOCK), :] += dkcat[SWA_BLOCK:, :]
        dv_ref[pl.ds(cur, SWA_BLOCK), :] += dvcat[SWA_BLOCK:, :]

        @pl.when(i > 0)
        def _():
            prv = pl.multiple_of((i - 1) * SWA_BLOCK, SWA_BLOCK)
            dk_ref[pl.ds(prv, SWA_BLOCK), :] += dkcat[:SWA_BLOCK, :]
            dv_ref[pl.ds(prv, SWA_BLOCK), :] += dvcat[:SWA_BLOCK, :]

        @pl.when(i == nb - 1)
        def _():
            bk = bucket_ref[...]
            rowi = lax.broadcasted_iota(jnp.int32, (REL_BUCKETS, LANES), 0)
            coli = lax.broadcasted_iota(jnp.int32, (REL_BUCKETS, LANES), 1)
            res = jnp.zeros((REL_BUCKETS, LANES), F32)
            for h in range(n_heads):
                db = dbias_ref[h]
                for b in range(REL_BUCKETS):
                    tot = _colsum(_rowsum(jnp.where(bk == b, db, 0.0)))
                    res = jnp.where((rowi == b) & (coli == h), tot, res)
            drb_ref[...] = res

    in_specs = _swa_specs(n_heads, qcol, kcol, vcol) + [pl.BlockSpec((SWA_BLOCK, width), lambda i: (i, 0))]
    return _call(
        body, "swa_bwd", (nb,),
        in_specs=in_specs,
        out_specs=[pl.BlockSpec((SWA_BLOCK, width), lambda i: (i, 0)),
                   _full((s, LANES)), _full((s, LANES)), _full((8, LANES)), _full((REL_BUCKETS, LANES))],
        out_shape=[jax.ShapeDtypeStruct((s, width), BF16), jax.ShapeDtypeStruct((s, LANES), F32),
                   jax.ShapeDtypeStruct((s, LANES), F32), jax.ShapeDtypeStruct((8, LANES), F32),
                   jax.ShapeDtypeStruct((REL_BUCKETS, LANES), F32)],
        args=(proj, proj, proj, proj, proj, bucket, rel_bias, sinks, dout),
        scratch_shapes=[pltpu.VMEM((n_heads, SWA_BLOCK, 2 * SWA_BLOCK), F32),
                        pltpu.VMEM((n_heads, SWA_BLOCK, 2 * SWA_BLOCK), F32)],
        comm=comm)


def _proj_bwd(dproj, w_in, du1, x, g_in, comm=None):
    s, d = x.shape
    cols = w_in.shape[1]
    tm = min(ROW_TILE, s)

    def body(dp_ref, w_ref, du_ref, x_ref, g_ref, dx_ref, st_ref):
        i = pl.program_id(0)

        @pl.when(i == 0)
        def _():
            st_ref[...] = jnp.zeros_like(st_ref)

        dh0 = _dot_nt(dp_ref[...], w_ref[...]) + ALPHA * du_ref[...]
        xhat, r = _ln_hat(x_ref[...])
        st_ref[0:1, :] += _colsum(dh0 * xhat)
        st_ref[1:2, :] += _colsum(dh0)
        dx_ref[...] = _ln_bwd(dh0 * g_ref[...], xhat, r)

    row = lambda width: pl.BlockSpec((tm, width), lambda i: (i, 0))
    return _call(
        body, "proj_bwd", (s // tm,),
        in_specs=[row(cols), _full((d, cols)), row(d), row(d), _full((1, d))],
        out_specs=[row(d), _full((8, d))],
        out_shape=[jax.ShapeDtypeStruct((s, d), F32), jax.ShapeDtypeStruct((8, d), F32)],
        args=(dproj, w_in, du1, x, g_in), comm=comm)


def _wgrad(name, a, b, tm, tn):
    s, m = a.shape
    n = b.shape[1]
    ts = min(WGRAD_TOKENS if b.dtype == BF16 else WGRAD_TOKENS // 2, s)

    def body(a_ref, b_ref, o_ref):
        @pl.when(pl.program_id(2) == 0)
        def _():
            o_ref[...] = jnp.zeros_like(o_ref)

        o_ref[...] += _dot_tn(a_ref[...].astype(BF16), b_ref[...].astype(BF16))

    return pl.pallas_call(
        body, name=name, grid=(m // tm, n // tn, s // ts),
        in_specs=[pl.BlockSpec((ts, tm), lambda i, j, k: (k, i)),
                  pl.BlockSpec((ts, tn), lambda i, j, k: (k, j))],
        out_specs=pl.BlockSpec((tm, tn), lambda i, j, k: (i, j)),
        out_shape=jax.ShapeDtypeStruct((m, n), F32),
        compiler_params=_params(),
    )(a, b)


def _adamw_math(w, g, m, v):
    m = ADAM_B1 * m + (1.0 - ADAM_B1) * g
    v = ADAM_B2 * v + (1.0 - ADAM_B2) * (g * g)
    m_hat = m / (1.0 - ADAM_B1 ** ADAM_STEP)
    v_hat = v / (1.0 - ADAM_B2 ** ADAM_STEP)
    delta = -ADAM_LR * (m_hat / (jnp.sqrt(v_hat) + ADAM_EPS) + ADAM_WD * w)
    return delta, m, v


def _adamw(name, landed, w, m, v, tr):
    rows, cols = w.shape

    def body(l_ref, w_ref, m_ref, v_ref, g_ref, d_ref, nm_ref, nv_ref):
        g = l_ref[0]
        for src in range(1, N_DEV):
            g = g + l_ref[src]
        delta, nm, nv = _adamw_math(w_ref[...], g, m_ref[...], v_ref[...])
        g_ref[...] = g
        d_ref[...] = delta
        nm_ref[...] = nm
        nv_ref[...] = nv

    blk = pl.BlockSpec((tr, cols), lambda i: (i, 0))
    shape = jax.ShapeDtypeStruct((rows, cols), F32)
    return pl.pallas_call(
        body, name=name, grid=(rows // tr,),
        in_specs=[pl.BlockSpec((N_DEV, tr, cols), lambda i: (0, i, 0)), blk, blk, blk],
        out_specs=[blk, blk, blk, blk],
        out_shape=[shape, shape, shape, shape],
        compiler_params=_params(),
    )(landed, w, m, v)


def _pack(d, ln_in_g, ln_in_b, ln1_g, ln1_b, ln2_g, ln2_b, sb_g, sw_g, rel_bias, sinks, extra=None):
    tail = [rel_bias.reshape(-1), sinks.reshape(-1)]
    if extra is not None:
        tail.append(extra.reshape(-1))
    tail = jnp.concatenate(tail)
    tail = jnp.concatenate([tail, jnp.zeros((d - tail.shape[0],), F32)])
    rows = [ln_in_g.reshape(-1), ln_in_b.reshape(-1), ln1_g.reshape(-1), ln1_b.reshape(-1),
            ln2_g.reshape(-1), ln2_b.reshape(-1),
            jnp.concatenate([sb_g.reshape(-1), sw_g.reshape(-1)]), tail]
    return jnp.stack(rows)


def _unpack(p, wsb, n_rb, n_sk):
    return [p[0], p[1], p[6, :wsb][None], p[6, wsb:][None], p[7, n_rb:n_rb + n_sk][None],
            p[7, :n_rb].reshape(REL_BUCKETS, -1), p[2][None], p[3][None], p[4][None], p[5][None]]


def kernel(x, ln_in_g, ln_in_b, w_in, sb_norm_g, swa_norm_g, sinks, rel_bias, w_out, ln1_g, ln1_b, w_gate_up, w_down, ln2_g, ln2_b, loss_target, m_ln_in_g, m_ln_in_b, m_w_in, m_sb_norm_g, m_swa_norm_g, m_sinks, m_rel_bias, m_w_out, m_ln1_g, m_ln1_b, m_w_gate_up, m_w_down, m_ln2_g, m_ln2_b, v_ln_in_g, v_ln_in_b, v_w_in, v_sb_norm_g, v_swa_norm_g, v_sinks, v_rel_bias, v_w_out, v_ln1_g, v_ln1_b, v_w_gate_up, v_w_down, v_ln2_g, v_ln2_b):
    x2 = x[0]
    tgt = loss_target[0]
    s, d = x2.shape
    wsb = sb_norm_g.shape[-1]
    wsw = swa_norm_g.shape[-1]
    n_sw_heads = sinks.shape[-1]
    n_pairs = wsb // LANES
    dff = w_down.shape[1] * N_DEV
    assert wsb % LANES == 0 and wsw % LANES == 0 and n_sw_heads * HEAD_DIM == wsw
    assert 3 * wsb % wsw == 0 and dff % LANES == 0 and s % SWA_BLOCK == 0
    qcol = 3 * wsb // wsw
    kcol = (3 * wsb + wsw) // LANES
    vcol = kcol + 1
    assert w_in.shape[-1] * N_DEV == (vcol + 1) * LANES

    big_w = [w_in[0], w_out[0], w_gate_up[0], w_down[0]]
    big_m = [m_w_in[0], m_w_out[0], m_w_gate_up[0], m_w_down[0]]
    big_v = [v_w_in[0], v_w_out[0], v_w_gate_up[0], v_w_down[0]]

    cat_cols = lambda g: jnp.transpose(g, (1, 0, 2)).reshape(g.shape[1], N_DEV * g.shape[2])
    cat_rows = lambda g: g.reshape(N_DEV * g.shape[1], g.shape[2])
    shards = [w.astype(BF16) for w in big_w]
    w_in_f = cat_cols(_exchange("w_in_allgather", shards[:1], ["gather"])[0])

    vec = lambda a: a.reshape(1, -1)
    g_in, b_in = vec(ln_in_g), vec(ln_in_b)
    bucket = jnp.asarray(_swa_bucket_table())

    h0b, proj = _ln_proj(x2, g_in, b_in, w_in_f)
    sb_out, gathered = _sb_fwd(proj, n_pairs, comm=(shards[1:], ["gather"] * 3))
    w_out_f, w_gu_f, w_down_f = cat_rows(gathered[0]), cat_cols(gathered[1]), cat_rows(gathered[2])
    sw_out = _swa_fwd(proj, bucket, rel_bias, sinks, n_sw_heads, qcol, kcol, vcol)
    merged, u1 = _mix_ln1(sb_out, sw_out, x2, g_in, b_in, sb_norm_g, swa_norm_g, w_out_f)
    du2, st_ln2 = _ffn_fwd(u1, ln1_g, ln1_b, w_gu_f, w_down_f, ln2_g, ln2_b, tgt)

    split_cols = lambda g: jnp.transpose(g.reshape(g.shape[0], N_DEV, g.shape[1] // N_DEV), (1, 0, 2))
    split_rows = lambda g: g.reshape(N_DEV, g.shape[0] // N_DEV, g.shape[1])
    tile_m = min(512, d)
    h1b, act, dgu = _ffn_bwd_act(u1, du2, ln1_g, ln1_b, w_gu_f, jnp.transpose(w_down_f))
    du1, st_ln1 = _ffn_bwd_in(dgu, jnp.transpose(w_gu_f), u1, du2, ln1_g)
    gw_gu = _wgrad("wgrad_gate_up", h1b, dgu, tile_m, dff // 2)
    gw_down = _wgrad("wgrad_down", act, du2, dff // 2, d)
    gw_out = _wgrad("wgrad_out", merged, du1, tile_m, d)
    dsb, dsw, st_rms = _mix_bwd(du1, w_out_f, sb_out, sw_out, sb_norm_g, swa_norm_g)
    (dq_sb, dk_sb, dv_sb), (land_gu, land_out) = _sb_bwd(
        proj, dsb, sb_out, n_pairs, comm=([split_cols(gw_gu), split_rows(gw_out)], ["scatter"] * 2))
    (dq_sw, dk_sw, dv_sw, st_sink, st_rb), (land_down,) = _swa_bwd(
        proj, dsw, bucket, rel_bias, sinks, n_sw_heads, qcol, kcol, vcol,
        comm=([split_rows(gw_down)], ["scatter"]))
    dproj = jnp.concatenate([dq_sb, dk_sb.astype(BF16), dv_sb.astype(BF16), dq_sw,
                             dk_sw.astype(BF16), dv_sw.astype(BF16)], axis=1)
    gw_in = _wgrad("wgrad_in", h0b, dproj, tile_m, dproj.shape[1] // 2)
    (grad_x, st_in), (land_in,) = _proj_bwd(dproj, w_in_f, du1, x2, g_in, comm=([split_cols(gw_in)], ["scatter"]))

    n_rb = rel_bias.size
    small = _pack(d, st_in[0], st_in[1], st_ln1[0], st_ln1[1], st_ln2[0], st_ln2[1],
                  st_rms[0, :wsb], st_rms[0, wsb:], st_rb[:, :n_sw_heads], st_sink[:n_sw_heads, 0],
                  extra=st_ln2[2, 0:1])
    land_small = _exchange("small_grads_allgather", [small], ["gather"])[0]
    landed = [land_in, land_out, land_gu, land_down, land_small]

    big = []
    for name, land, w, m, v in zip(["adamw_in", "adamw_out", "adamw_gate_up", "adamw_down"], landed[:4], big_w, big_m, big_v):
        rows = w.shape[0]
        tr = 128 if rows % 128 == 0 else rows // 4
        big.append(_adamw(name, land, w, m, v, tr))

    small_w = _pack(d, ln_in_g, ln_in_b, ln1_g, ln1_b, ln2_g, ln2_b, sb_norm_g, swa_norm_g, rel_bias, sinks)
    small_m = _pack(d, m_ln_in_g, m_ln_in_b, m_ln1_g, m_ln1_b, m_ln2_g, m_ln2_b, m_sb_norm_g, m_swa_norm_g, m_rel_bias, m_sinks)
    small_v = _pack(d, v_ln_in_g, v_ln_in_b, v_ln1_g, v_ln1_b, v_ln2_g, v_ln2_b, v_sb_norm_g, v_swa_norm_g, v_rel_bias, v_sinks)
    sg, sd, sm, sv = _adamw("adamw_small", landed[4], small_w, small_m, small_v, 8)
    n_sk = sinks.size
    loss = sg[7, n_rb + n_sk]

    def leaves(idx):
        sm_l = _unpack([sg, sd, sm, sv][idx], wsb, n_rb, n_sk)
        bg = [b[idx][None] for b in big]
        return [sm_l[0], sm_l[1], bg[0], sm_l[2], sm_l[3], sm_l[4], sm_l[5], bg[1], sm_l[6], sm_l[7], bg[2], bg[3], sm_l[8], sm_l[9]]

    return (loss, grad_x[None], *leaves(0), *leaves(1), *leaves(2), *leaves(3))
```

```python
import functools
import math

import numpy as np
import jax
import jax.numpy as jnp
from jax import lax
from jax.experimental import pallas as pl
from jax.experimental.pallas import tpu as pltpu

F32 = jnp.float32
BF16 = jnp.bfloat16
MESH = pl.DeviceIdType.MESH

N_DEV = 8
LANES = 128
HEAD_DIM = 64
SCALE = HEAD_DIM ** -0.5
SWA_BLOCK = 128
REL_BUCKETS = 32
REL_MAX_DIST = 128
ALPHA = 2.0 ** 0.25
LN_EPS = 1e-5
RMS_EPS = 1e-6
ADAM_LR = 0.001
ADAM_B1 = 0.9
ADAM_B2 = 0.999
ADAM_EPS = 1e-08
ADAM_WD = 0.01
ADAM_STEP = 10

ROW_TILE = 512
SB_TILE = 256
FFN_TILE = 256
WGRAD_TOKENS = 2048
SB_UNDERFLOW = -110.0
MIB = 1024 * 1024


def _params(vmem_mib=48):
    return pltpu.CompilerParams(vmem_limit_bytes=vmem_mib * MIB)


def _dot(a, b):
    return jnp.dot(a, b, preferred_element_type=F32)


def _dot_nt(a, b):
    return lax.dot_general(a, b, (((1,), (1,)), ((), ())), preferred_element_type=F32)


def _dot_tn(a, b):
    return lax.dot_general(a, b, (((0,), (0,)), ((), ())), preferred_element_type=F32)


def _ln_hat(x):
    mu = jnp.mean(x, axis=-1, keepdims=True)
    xc = x - mu
    var = jnp.mean(xc * xc, axis=-1, keepdims=True)
    r = lax.rsqrt(var + LN_EPS)
    return xc * r, r


def _ln_bwd(dxhat, xhat, r):
    return r * (dxhat - jnp.mean(dxhat, axis=-1, keepdims=True)
                - xhat * jnp.mean(dxhat * xhat, axis=-1, keepdims=True))


def _colsum(a):
    return jnp.sum(a, axis=0, keepdims=True)


def _rowsum(a):
    return jnp.sum(a, axis=1, keepdims=True)


def _full(shape):
    return pl.BlockSpec(shape, lambda *_: (0,) * len(shape))


def _comm_out_shapes(arrays, kinds):
    shapes = []
    for a, kind in zip(arrays, kinds):
        blk = a.shape if kind == "gather" else a.shape[1:]
        shapes.append(jax.ShapeDtypeStruct((N_DEV,) + tuple(blk), a.dtype))
    return shapes


def _comm_sems(n):
    return [pltpu.SemaphoreType.DMA((n, N_DEV - 1)), pltpu.SemaphoreType.DMA((n, N_DEV - 1)),
            pltpu.SemaphoreType.DMA((n,))]


def _comm_copies(ins, outs, kinds, send_sems, recv_sems, local_sems):
    x, y, c = lax.axis_index("x"), lax.axis_index("y"), lax.axis_index("c")
    me = 4 * x + 2 * y + c

    def src_for(t, dev_lin):
        return ins[t] if kinds[t] == "gather" else ins[t].at[dev_lin]

    local = [pltpu.make_async_copy(src_for(t, me), outs[t].at[me], local_sems.at[t]) for t in range(len(kinds))]
    sends, arrivals = [], []
    for k in range(1, N_DEV):
        px = 1 - x if (k >> 2) & 1 else x
        py = 1 - y if (k >> 1) & 1 else y
        pc = 1 - c if k & 1 else c
        peer_lin = 4 * px + 2 * py + pc
        for t in range(len(kinds)):
            sems = dict(send_sem=send_sems.at[t, k - 1], recv_sem=recv_sems.at[t, k - 1],
                        device_id=(px, py, pc), device_id_type=MESH)
            sends.append(pltpu.make_async_remote_copy(src_ref=src_for(t, peer_lin), dst_ref=outs[t].at[me], **sems))
            arrivals.append(pltpu.make_async_remote_copy(src_ref=src_for(t, peer_lin), dst_ref=outs[t].at[peer_lin], **sems))
    return local, sends, arrivals


def _comm_start(ins, outs, kinds, sems):
    local, sends, _ = _comm_copies(ins, outs, kinds, *sems)
    for cp in local + sends:
        cp.start()


def _comm_finish(ins, outs, kinds, sems):
    local, sends, arrivals = _comm_copies(ins, outs, kinds, *sems)
    for cp in arrivals:
        cp.wait_recv()
    for cp in sends:
        cp.wait_send()
    for cp in local:
        cp.wait()


def _exchange(name, arrays, kinds):
    n = len(arrays)

    def body(*refs):
        ins, outs, sems = refs[:n], refs[n:2 * n], refs[2 * n:]
        _comm_start(ins, outs, kinds, sems)
        _comm_finish(ins, outs, kinds, sems)

    any_spec = pl.BlockSpec(memory_space=pl.ANY)
    return pl.pallas_call(
        body, name=name, out_shape=_comm_out_shapes(arrays, kinds),
        in_specs=[any_spec] * n, out_specs=[any_spec] * n,
        scratch_shapes=_comm_sems(n),
    )(*arrays)


def _call(body, name, grid, in_specs, out_specs, out_shape, args, scratch_shapes=(), comm=None):
    if comm is None:
        outs = pl.pallas_call(body, name=name, grid=grid, in_specs=in_specs, out_specs=out_specs,
                              out_shape=out_shape, scratch_shapes=list(scratch_shapes),
                              compiler_params=_params())(*args)
        return outs, []
    arrays, kinds = comm
    n, n_in, n_out, n_scr = len(arrays), len(in_specs), len(out_specs), len(scratch_shapes)

    def fused(*refs):
        c_in, x_in = refs[:n_in], refs[n_in:n_in + n]
        c_out = refs[n_in + n:n_in + n + n_out]
        x_out = refs[n_in + n + n_out:n_in + 2 * n + n_out]
        rest = refs[n_in + 2 * n + n_out:]
        c_scr, sems = rest[:n_scr], rest[n_scr:]
        ids = [pl.program_id(a) for a in range(len(grid))]
        is_first = functools.reduce(jnp.logical_and, [i == 0 for i in ids])
        is_last = functools.reduce(jnp.logical_and, [i == g - 1 for i, g in zip(ids, grid)])

        @pl.when(is_first)
        def _():
            _comm_start(x_in, x_out, kinds, sems)

        body(*c_in, *c_out, *c_scr)

        @pl.when(is_last)
        def _():
            _comm_finish(x_in, x_out, kinds, sems)

    any_spec = pl.BlockSpec(memory_space=pl.ANY)
    outs = pl.pallas_call(
        fused, name=name, grid=grid,
        in_specs=list(in_specs) + [any_spec] * n, out_specs=list(out_specs) + [any_spec] * n,
        out_shape=list(out_shape) + _comm_out_shapes(arrays, kinds),
        scratch_shapes=list(scratch_shapes) + _comm_sems(n),
        compiler_params=_params())(*args, *arrays)
    return outs[:n_out], outs[n_out:]


def _ln_proj(x, g, b, w_in):
    s, d = x.shape
    cols = w_in.shape[1]
    tm = min(ROW_TILE, s)

    def body(x_ref, g_ref, b_ref, w_ref, h_ref, p_ref):
        xhat, _ = _ln_hat(x_ref[...])
        h = (xhat * g_ref[...] + b_ref[...]).astype(BF16)
        h_ref[...] = h
        p_ref[...] = _dot(h, w_ref[...]).astype(BF16)

    row = lambda width: pl.BlockSpec((tm, width), lambda i: (i, 0))
    return pl.pallas_call(
        body, name="ln_proj", grid=(s // tm,),
        in_specs=[row(d), _full((1, d)), _full((1, d)), _full((d, cols))],
        out_specs=[row(d), row(cols)],
        out_shape=[jax.ShapeDtypeStruct((s, d), BF16), jax.ShapeDtypeStruct((s, cols), BF16)],
        compiler_params=_params(),
    )(x, g, b, w_in)


def _sb_tile_consts(t):
    row = lax.broadcasted_iota(jnp.int32, (t, t), 0)
    col = lax.broadcasted_iota(jnp.int32, (t, t), 1)
    return row, col


def _sb_scores(qh, k_t, upper, carry_l, causal):
    z = _dot_nt(qh, k_t) * SCALE
    sp = jnp.log(1.0 + jnp.exp(-jnp.abs(z)))
    lb = jnp.minimum(z, 0.0) - sp
    l1 = -(jnp.maximum(z, 0.0) + sp)
    if causal is not None:
        l1 = jnp.where(causal, l1, 0.0)
    hi = l1.astype(BF16)
    lo = (l1 - hi.astype(F32)).astype(BF16)
    suf = _dot(hi, upper) + _dot(lo, upper) + carry_l
    a = jnp.exp(lb + suf)
    if causal is not None:
        a = jnp.where(causal, a, 0.0)
    return lb, l1, a


def _sb_walk(i, t, visit, init, causal):
    def alive(carry):
        return jnp.max(jnp.maximum(carry[0][0], carry[1][0])) > SB_UNDERFLOW

    carry = visit(pl.multiple_of(i * t, t), init, causal)

    def cond(state):
        j, go, _ = state
        return (j < i) & go

    def body(state):
        j, _, carry = state
        carry = visit(pl.multiple_of((i - 1 - j) * t, t), carry, None)
        return j + 1, alive(carry), carry

    return lax.while_loop(cond, body, (jnp.int32(0), alive(carry), carry))[2]


def _sb_fwd(proj, n_pairs, comm=None):
    s = proj.shape[0]
    t = min(SB_TILE, s)
    nq = s // t

    def body(q_ref, k_ref, v_ref, o_ref):
        i = pl.program_id(1)
        lane = lax.broadcasted_iota(jnp.int32, (1, LANES), 1)
        first = lane < HEAD_DIM
        q2 = q_ref[...]
        zero = jnp.zeros_like(q2)
        qs = (jnp.where(first, q2, zero), jnp.where(first, zero, q2))
        row, col = _sb_tile_consts(t)
        upper = (row > col).astype(BF16)
        causal = col < row

        def visit(off, carry, mask):
            k_t = k_ref[pl.ds(off, t), :]
            v_t = v_ref[pl.ds(off, t), :]
            out = []
            for hh in range(2):
                c_l, acc = carry[hh]
                _, l1, a = _sb_scores(qs[hh], k_t, upper, c_l, mask)
                out.append((c_l + _rowsum(l1), acc + _dot(a.astype(BF16), v_t)))
            return tuple(out)

        init = tuple((jnp.zeros((t, 1), F32), jnp.zeros((t, LANES), F32)) for _ in range(2))
        carry = _sb_walk(i, t, visit, init, causal)
        o_ref[...] = jnp.where(first, carry[0][1], carry[1][1])

    outs, landed = _call(
        body, "sb_fwd", (n_pairs, nq),
        in_specs=[pl.BlockSpec((t, LANES), lambda h, i: (i, h)),
                  pl.BlockSpec((s, LANES), lambda h, i: (0, n_pairs + h)),
                  pl.BlockSpec((s, LANES), lambda h, i: (0, 2 * n_pairs + h))],
        out_specs=[pl.BlockSpec((t, LANES), lambda h, i: (i, h))],
        out_shape=[jax.ShapeDtypeStruct((s, n_pairs * LANES), F32)],
        args=(proj, proj, proj), comm=comm)
    return outs[0], landed


def _swa_bucket_table():
    qi = np.arange(SWA_BLOCK)[:, None]
    cj = np.arange(2 * SWA_BLOCK)[None, :]
    dist = qi + SWA_BLOCK - cj
    exact = REL_BUCKETS // 2
    d = np.maximum(dist, 0)
    d_f = np.maximum(d, 1).astype(np.float32)
    large = exact + (np.log(d_f / np.float32(exact)) / np.float32(math.log(REL_MAX_DIST / exact))
                     * np.float32(REL_BUCKETS - exact)).astype(np.int32)
    large = np.minimum(large, REL_BUCKETS - 1)
    return np.where(d < exact, d, large).astype(np.int32)


def _swa_build_bias(bucket_ref, rb_ref, bias_ref, n_groups, per_group):
    bk = bucket_ref[...]
    for g in range(n_groups):
        for hh in range(per_group):
            acc = jnp.zeros(bk.shape, F32)
            for b in range(REL_BUCKETS):
                acc = jnp.where(bk == b, rb_ref[b, g * per_group + hh], acc)
            bias_ref[g, hh * SWA_BLOCK:(hh + 1) * SWA_BLOCK, :] = acc


def _swa_valid(i, reps):
    shape = (reps * SWA_BLOCK, 2 * SWA_BLOCK)
    row = lax.broadcasted_iota(jnp.int32, shape, 0) & (SWA_BLOCK - 1)
    col = lax.broadcasted_iota(jnp.int32, shape, 1)
    dist = row + SWA_BLOCK - col
    return (dist >= 0) & (dist < SWA_BLOCK) & ((col >= SWA_BLOCK) | (i > 0))


def _swa_place(blk, h, group, sel):
    if (h % 2) != group:
        blk = pltpu.roll(blk.astype(F32), HEAD_DIM, axis=1).astype(BF16)
    return jnp.where(sel, blk, jnp.zeros_like(blk))


def _swa_stack(ref, group, per_group, sel):
    parts = []
    for hh in range(per_group):
        h = group * per_group + hh
        parts.append(_swa_place(ref[:, (h // 2) * LANES:(h // 2 + 1) * LANES], h, group, sel))
    return jnp.concatenate(parts, axis=0)


def _swa_unstack(stacked, group, per_group, pieces):
    for hh in range(per_group):
        h = group * per_group + hh
        piece = stacked[hh * SWA_BLOCK:(hh + 1) * SWA_BLOCK, :]
        pieces[h] = pltpu.roll(piece, HEAD_DIM, axis=1) if (h % 2) != group else piece


def _swa_sink_rows(sk_ref, group, per_group):
    rowh = lax.broadcasted_iota(jnp.int32, (per_group * SWA_BLOCK, 1), 0) // SWA_BLOCK
    sink = jnp.zeros((per_group * SWA_BLOCK, 1), F32) + sk_ref[0, group * per_group]
    for hh in range(1, per_group):
        sink = jnp.where(rowh == hh, sk_ref[0, group * per_group + hh], sink)
    return sink


def _swa_probs(q_pos, kcat, bias_h, valid, sink):
    logits = _dot_nt(q_pos, kcat) * SCALE + bias_h
    logits = jnp.where(valid, logits, -jnp.inf)
    m = jnp.maximum(jnp.max(logits, axis=1, keepdims=True), sink)
    p = jnp.exp(logits - m)
    es = jnp.exp(sink - m)
    denom = _rowsum(p) + es
    return p / denom, es / denom


def _swa_specs(n_heads, qcol, kcol, vcol):
    width = n_heads * HEAD_DIM
    prev = lambda col: pl.BlockSpec((SWA_BLOCK, LANES), lambda i: (jnp.maximum(i - 1, 0), col))
    cur = lambda col: pl.BlockSpec((SWA_BLOCK, LANES), lambda i: (i, col))
    return [pl.BlockSpec((SWA_BLOCK, width), lambda i: (i, qcol)),
            prev(kcol), cur(kcol), prev(vcol), cur(vcol),
            _full((SWA_BLOCK, 2 * SWA_BLOCK)),
            pl.BlockSpec(memory_space=pltpu.SMEM), pl.BlockSpec(memory_space=pltpu.SMEM)]


def _swa_fwd(proj, bucket, rel_bias, sinks, n_heads, qcol, kcol, vcol):
    s = proj.shape[0]
    width = n_heads * HEAD_DIM
    n_groups = LANES // HEAD_DIM
    per_group = n_heads // n_groups

    def body(q_ref, kp_ref, kc_ref, vp_ref, vc_ref, bucket_ref, rb_ref, sk_ref, o_ref, bias_ref):
        i = pl.program_id(0)

        @pl.when(i == 0)
        def _():
            _swa_build_bias(bucket_ref, rb_ref, bias_ref, n_groups, per_group)

        lane = lax.broadcasted_iota(jnp.int32, (1, LANES), 1)
        first = lane < HEAD_DIM
        valid = _swa_valid(i, per_group)
        kcat = jnp.concatenate([kp_ref[...], kc_ref[...]], axis=0)
        vcat = jnp.concatenate([vp_ref[...], vc_ref[...]], axis=0)
        pieces = {}
        for g in range(n_groups):
            sel = first if g == 0 else jnp.logical_not(first)
            prob, _ = _swa_probs(_swa_stack(q_ref, g, per_group, sel), kcat, bias_ref[g], valid,
                                 _swa_sink_rows(sk_ref, g, per_group))
            _swa_unstack(_dot(prob.astype(BF16), vcat), g, per_group, pieces)
        for j in range(n_heads // 2):
            o_ref[:, j * LANES:(j + 1) * LANES] = jnp.where(first, pieces[2 * j], pieces[2 * j + 1])

    return pl.pallas_call(
        body, name="swa_fwd", grid=(s // SWA_BLOCK,),
        in_specs=_swa_specs(n_heads, qcol, kcol, vcol),
        out_specs=pl.BlockSpec((SWA_BLOCK, width), lambda i: (i, 0)),
        out_shape=jax.ShapeDtypeStruct((s, width), F32),
        scratch_shapes=[pltpu.VMEM((n_groups, per_group * SWA_BLOCK, 2 * SWA_BLOCK), F32)],
        compiler_params=_params(),
    )(proj, proj, proj, proj, proj, bucket, rel_bias, sinks)


def _rms_fwd(o, g):
    r = lax.rsqrt(jnp.mean(o * o, axis=-1, keepdims=True) + RMS_EPS)
    n = o * r
    return n, r, n * g


def _mix_ln1(sb_out, sw_out, x, g_in, b_in, sb_g, sw_g, w_out):
    s, d = x.shape
    wsb, wsw = sb_out.shape[1], sw_out.shape[1]
    tm = min(ROW_TILE, s)

    def body(sb_ref, sw_ref, x_ref, gi_ref, bi_ref, sbg_ref, swg_ref, w_ref, mg_ref, u_ref):
        _, _, m_sb = _rms_fwd(sb_ref[...], sbg_ref[...])
        _, _, m_sw = _rms_fwd(sw_ref[...], swg_ref[...])
        m_sb = m_sb.astype(BF16)
        m_sw = m_sw.astype(BF16)
        mg_ref[:, :wsb] = m_sb
        mg_ref[:, wsb:] = m_sw
        mix = _dot(m_sb, w_ref[:wsb, :]) + _dot(m_sw, w_ref[wsb:, :])
        xhat, _ = _ln_hat(x_ref[...])
        h0 = xhat * gi_ref[...] + bi_ref[...]
        u_ref[...] = ALPHA * h0 + mix

    row = lambda width: pl.BlockSpec((tm, width), lambda i: (i, 0))
    return pl.pallas_call(
        body, name="mix_ln1", grid=(s // tm,),
        in_specs=[row(wsb), row(wsw), row(d), _full((1, d)), _full((1, d)),
                  _full((1, wsb)), _full((1, wsw)), _full((wsb + wsw, d))],
        out_specs=[row(wsb + wsw), row(d)],
        out_shape=[jax.ShapeDtypeStruct((s, wsb + wsw), BF16), jax.ShapeDtypeStruct((s, d), F32)],
        compiler_params=_params(),
    )(sb_out, sw_out, x, g_in, b_in, sb_g, sw_g, w_out)


def _ffn_fwd(u1, g1, b1, w_gu, w_down, g2, b2, target):
    s, d = u1.shape
    dff = w_down.shape[0]
    tm = min(FFN_TILE, s)

    def body(u_ref, g1_ref, b1_ref, wgu_hbm, wd_hbm, g2_ref, b2_ref, t_ref, du_ref, st_ref, wgu_ref, wd_ref):
        @pl.when(pl.program_id(0) == 0)
        def _():
            pltpu.sync_copy(wgu_hbm, wgu_ref)
            pltpu.sync_copy(wd_hbm, wd_ref)
            st_ref[...] = jnp.zeros_like(st_ref)

        xhat, _ = _ln_hat(u_ref[...])
        h1 = xhat * g1_ref[...] + b1_ref[...]
        h1b = h1.astype(BF16)
        gate = _dot(h1b, wgu_ref[:, :dff])
        up = _dot(h1b, wgu_ref[:, dff:])
        act = gate * jax.nn.sigmoid(gate) * up
        u2 = ALPHA * h1 + _dot(act.astype(BF16), wd_ref[...])
        xhat2, r2 = _ln_hat(u2)
        diff = xhat2 * g2_ref[...] + b2_ref[...] - t_ref[...]
        dh2 = diff * (1.0 / d)
        st_ref[0:1, :] += _colsum(dh2 * xhat2)
        st_ref[1:2, :] += _colsum(dh2)
        st_ref[2:3, :] += jnp.broadcast_to(_colsum(_rowsum(diff * diff)) * (0.5 / d), (1, d))
        du_ref[...] = _ln_bwd(dh2 * g2_ref[...], xhat2, r2)

    row = pl.BlockSpec((tm, d), lambda i: (i, 0))
    hbm = pl.BlockSpec(memory_space=pl.ANY)
    return pl.pallas_call(
        body, name="ffn_fwd", grid=(s // tm,),
        in_specs=[row, _full((1, d)), _full((1, d)), hbm, hbm, _full((1, d)), _full((1, d)), row],
        out_specs=[row, _full((8, d))],
        out_shape=[jax.ShapeDtypeStruct((s, d), F32), jax.ShapeDtypeStruct((8, d), F32)],
        scratch_shapes=[pltpu.VMEM(w_gu.shape, BF16), pltpu.VMEM(w_down.shape, BF16)],
        compiler_params=_params(56),
    )(u1, g1, b1, w_gu, w_down, g2, b2, target)


def _ffn_bwd_act(u1, du2, g1, b1, w_gu, w_down_t):
    s, d = u1.shape
    dff = w_down_t.shape[1]
    tm = min(FFN_TILE, s)

    def body(u_ref, du2_ref, g1_ref, b1_ref, wgu_hbm, wdt_hbm, h1b_ref, act_ref, dgu_ref, wgu_ref, wdt_ref):
        @pl.when(pl.program_id(0) == 0)
        def _():
            pltpu.sync_copy(wgu_hbm, wgu_ref)
            pltpu.sync_copy(wdt_hbm, wdt_ref)

        xhat, _ = _ln_hat(u_ref[...])
        h1b = (xhat * g1_ref[...] + b1_ref[...]).astype(BF16)
        h1b_ref[...] = h1b
        gate = _dot(h1b, wgu_ref[:, :dff])
        up = _dot(h1b, wgu_ref[:, dff:])
        dact = _dot(du2_ref[...].astype(BF16), wdt_ref[...])
        sg = jax.nn.sigmoid(gate)
        silu = gate * sg
        act_ref[...] = (silu * up).astype(BF16)
        dgu_ref[:, :dff] = (dact * up * (sg * (1.0 + gate * (1.0 - sg)))).astype(BF16)
        dgu_ref[:, dff:] = (dact * silu).astype(BF16)

    row = lambda width: pl.BlockSpec((tm, width), lambda i: (i, 0))
    hbm = pl.BlockSpec(memory_space=pl.ANY)
    return pl.pallas_call(
        body, name="ffn_bwd_act", grid=(s // tm,),
        in_specs=[row(d), row(d), _full((1, d)), _full((1, d)), hbm, hbm],
        out_specs=[row(d), row(dff), row(2 * dff)],
        out_shape=[jax.ShapeDtypeStruct((s, d), BF16), jax.ShapeDtypeStruct((s, dff), BF16),
                   jax.ShapeDtypeStruct((s, 2 * dff), BF16)],
        scratch_shapes=[pltpu.VMEM(w_gu.shape, BF16), pltpu.VMEM(w_down_t.shape, BF16)],
        compiler_params=_params(56),
    )(u1, du2, g1, b1, w_gu, w_down_t)


def _ffn_bwd_in(dgu, w_gu_t, u1, du2, g1):
    s, d = u1.shape
    tm = min(FFN_TILE, s)

    def body(dgu_ref, wt_hbm, u_ref, du2_ref, g1_ref, du1_ref, st_ref, wt_ref):
        @pl.when(pl.program_id(0) == 0)
        def _():
            pltpu.sync_copy(wt_hbm, wt_ref)
            st_ref[...] = jnp.zeros_like(st_ref)

        dh1 = _dot(dgu_ref[...], wt_ref[...]) + ALPHA * du2_ref[...]
        xhat, r = _ln_hat(u_ref[...])
        st_ref[0:1, :] += _colsum(dh1 * xhat)
        st_ref[1:2, :] += _colsum(dh1)
        du1_ref[...] = _ln_bwd(dh1 * g1_ref[...], xhat, r)

    row = lambda width: pl.BlockSpec((tm, width), lambda i: (i, 0))
    return pl.pallas_call(
        body, name="ffn_bwd_in", grid=(s // tm,),
        in_specs=[row(dgu.shape[1]), pl.BlockSpec(memory_space=pl.ANY), row(d), row(d), _full((1, d))],
        out_specs=[row(d), _full((8, d))],
        out_shape=[jax.ShapeDtypeStruct((s, d), F32), jax.ShapeDtypeStruct((8, d), F32)],
        scratch_shapes=[pltpu.VMEM(w_gu_t.shape, BF16)],
        compiler_params=_params(56),
    )(dgu, w_gu_t, u1, du2, g1)


def _rms_bwd(dm, o, g):
    n, r, _ = _rms_fwd(o, g)
    dn = dm * g
    return r * (dn - n * jnp.mean(dn * n, axis=-1, keepdims=True)), _colsum(dm * n)


def _mix_bwd(du1, w_out, sb_out, sw_out, sb_g, sw_g):
    s, d = du1.shape
    wsb, wsw = sb_out.shape[1], sw_out.shape[1]
    tm = min(ROW_TILE, s)

    def body(du_ref, w_ref, sb_ref, sw_ref, sbg_ref, swg_ref, dsb_ref, dsw_ref, st_ref):
        i = pl.program_id(0)

        @pl.when(i == 0)
        def _():
            st_ref[...] = jnp.zeros_like(st_ref)

        dmerged = _dot_nt(du_ref[...].astype(BF16), w_ref[...])
        dsb, gsb = _rms_bwd(dmerged[:, :wsb], sb_ref[...], sbg_ref[...])
        dsw, gsw = _rms_bwd(dmerged[:, wsb:], sw_ref[...], swg_ref[...])
        dsb_ref[...] = dsb.astype(BF16)
        dsw_ref[...] = dsw.astype(BF16)
        st_ref[0:1, :wsb] += gsb
        st_ref[0:1, wsb:] += gsw

    row = lambda width: pl.BlockSpec((tm, width), lambda i: (i, 0))
    return pl.pallas_call(
        body, name="mix_bwd", grid=(s // tm,),
        in_specs=[row(d), _full((wsb + wsw, d)), row(wsb), row(wsw), _full((1, wsb)), _full((1, wsw))],
        out_specs=[row(wsb), row(wsw), _full((8, wsb + wsw))],
        out_shape=[jax.ShapeDtypeStruct((s, wsb), BF16), jax.ShapeDtypeStruct((s, wsw), BF16),
                   jax.ShapeDtypeStruct((8, wsb + wsw), F32)],
        compiler_params=_params(),
    )(du1, w_out, sb_out, sw_out, sb_g, sw_g)


def _sb_bwd(proj, dout, out, n_pairs, comm=None):
    s = proj.shape[0]
    t = min(SB_TILE, s)
    nq = s // t
    width = n_pairs * LANES

    def body(q_ref, k_ref, v_ref, do_ref, o_ref, dq_ref, dk_ref, dv_ref):
        i = pl.program_id(1)

        @pl.when(i == 0)
        def _():
            dk_ref[...] = jnp.zeros_like(dk_ref)
            dv_ref[...] = jnp.zeros_like(dv_ref)

        lane = lax.broadcasted_iota(jnp.int32, (1, LANES), 1)
        first = lane < HEAD_DIM
        q2 = q_ref[...]
        do2 = do_ref[...]
        zero = jnp.zeros_like(q2)
        qs = (jnp.where(first, q2, zero), jnp.where(first, zero, q2))
        dos = (jnp.where(first, do2, zero), jnp.where(first, zero, do2))
        prod = do2.astype(F32) * o_ref[...]
        totals = (_rowsum(jnp.where(first, prod, 0.0)), _rowsum(jnp.where(first, 0.0, prod)))
        row, col = _sb_tile_consts(t)
        upper = (row > col).astype(BF16)
        incl = (row >= col).astype(BF16)
        causal = col < row

        def visit(off, carry, mask):
            k_t = k_ref[pl.ds(off, t), :]
            v_t = v_ref[pl.ds(off, t), :]
            out = []
            dk_t = jnp.zeros((t, LANES), F32)
            dv_t = jnp.zeros((t, LANES), F32)
            for hh in range(2):
                c_l, c_e, dq = carry[hh]
                lb, l1, a = _sb_scores(qs[hh], k_t, upper, c_l, mask)
                a_b = a.astype(BF16)
                d_e = _dot_nt(dos[hh], v_t) * a_b.astype(F32)
                d_l = totals[hh] - (_dot(d_e.astype(BF16), incl) + c_e)
                dz = d_e - jnp.exp(lb) * (d_e + d_l)
                if mask is not None:
                    dz = jnp.where(mask, dz, 0.0)
                dzb = (dz * SCALE).astype(BF16)
                dk_t += _dot_tn(dzb, qs[hh])
                dv_t += _dot_tn(a_b, dos[hh])
                out.append((c_l + _rowsum(l1), c_e + _rowsum(d_e), dq + _dot(dzb, k_t)))
            dk_ref[pl.ds(off, t), :] += dk_t
            dv_ref[pl.ds(off, t), :] += dv_t
            return tuple(out)

        init = tuple((jnp.zeros((t, 1), F32), jnp.zeros((t, 1), F32), jnp.zeros((t, LANES), F32))
                     for _ in range(2))
        carry = _sb_walk(i, t, visit, init, causal)
        dq_ref[...] = jnp.where(first, carry[0][2], carry[1][2]).astype(BF16)

    qblk = pl.BlockSpec((t, LANES), lambda h, i: (i, h))
    whole = pl.BlockSpec((s, LANES), lambda h, i: (0, h))
    return _call(
        body, "sb_bwd", (n_pairs, nq),
        in_specs=[qblk,
                  pl.BlockSpec((s, LANES), lambda h, i: (0, n_pairs + h)),
                  pl.BlockSpec((s, LANES), lambda h, i: (0, 2 * n_pairs + h)),
                  qblk, qblk],
        out_specs=[qblk, whole, whole],
        out_shape=[jax.ShapeDtypeStruct((s, width), BF16), jax.ShapeDtypeStruct((s, width), F32),
                   jax.ShapeDtypeStruct((s, width), F32)],
        args=(proj, proj, proj, dout, out), comm=comm)


def _swa_bwd(proj, dout, bucket, rel_bias, sinks, n_heads, qcol, kcol, vcol, comm=None):
    s = proj.shape[0]
    width = n_heads * HEAD_DIM
    n_groups = LANES // HEAD_DIM
    per_group = n_heads // n_groups
    nb = s // SWA_BLOCK

    def body(q_ref, kp_ref, kc_ref, vp_ref, vc_ref, bucket_ref, rb_ref, sk_ref, do_ref,
             dq_ref, dk_ref, dv_ref, dsk_ref, drb_ref, bias_ref, dbias_ref):
        i = pl.program_id(0)

        @pl.when(i == 0)
        def _():
            _swa_build_bias(bucket_ref, rb_ref, bias_ref, n_groups, per_group)
            dbias_ref[...] = jnp.zeros_like(dbias_ref)
            dk_ref[...] = jnp.zeros_like(dk_ref)
            dv_ref[...] = jnp.zeros_like(dv_ref)
            dsk_ref[...] = jnp.zeros_like(dsk_ref)

        lane = lax.broadcasted_iota(jnp.int32, (1, LANES), 1)
        first = lane < HEAD_DIM
        valid = _swa_valid(i, per_group)
        kcat = jnp.concatenate([kp_ref[...], kc_ref[...]], axis=0)
        vcat = jnp.concatenate([vp_ref[...], vc_ref[...]], axis=0)
        dkcat = jnp.zeros((2 * SWA_BLOCK, LANES), F32)
        dvcat = jnp.zeros((2 * SWA_BLOCK, LANES), F32)
        pieces = {}
        for g in range(n_groups):
            sel = first if g == 0 else jnp.logical_not(first)
            q_g = _swa_stack(q_ref, g, per_group, sel)
            do_g = _swa_stack(do_ref, g, per_group, sel)
            prob, p_sink = _swa_probs(q_g, kcat, bias_ref[g], valid, _swa_sink_rows(sk_ref, g, per_group))
            dprob = _dot_nt(do_g, vcat)
            delta = _rowsum(prob * dprob)
            dlog = prob * (dprob - delta)
            sink_term = p_sink * delta
            for hh in range(per_group):
                h = g * per_group + hh
                tot = _colsum(sink_term[hh * SWA_BLOCK:(hh + 1) * SWA_BLOCK, :])
                dsk_ref[h:h + 1, :] += jnp.broadcast_to(-tot, (1, LANES))
            dbias_ref[g] += dlog
            dlb = (dlog * SCALE).astype(BF16)
            _swa_unstack(_dot(dlb, kcat), g, per_group, pieces)
            dkcat += _dot_tn(dlb, q_g)
            dvcat += _dot_tn(prob.astype(BF16), do_g)
        for j in range(n_heads // 2):
            dq_ref[:, j * LANES:(j + 1) * LANES] = jnp.where(first, pieces[2 * j], pieces[2 * j + 1]).astype(BF16)

        cur = pl.multiple_of(i * SWA_BLOCK, SWA_BLOCK)
        dk_ref[pl.ds(cur, SWA_BLOCK), :] += dkcat[SWA_BLOCK:, :]
        dv_ref[pl.ds(cur, SWA_BLOCK), :] += dvcat[SWA_BLOCK:, :]

        @pl.when(i > 0)
        def _():
            prv = pl.multiple_of((i - 1) * SWA_BLOCK, SWA_BLOCK)
            dk_ref[pl.ds(prv, SWA_BLOCK), :] += dkcat[:SWA_BLOCK, :]
            dv_ref[pl.ds(prv, SWA_BLOCK), :] += dvcat[:SWA_BLOCK, :]

        @pl.when(i == nb - 1)
        def _():
            bk = bucket_ref[...]
            rowi = lax.broadcasted_iota(jnp.int32, (REL_BUCKETS, LANES), 0)
            coli = lax.broadcasted_iota(jnp.int32, (REL_BUCKETS, LANES), 1)
            res = jnp.zeros((REL_BUCKETS, LANES), F32)
            for h in range(n_heads):
                g, hh = divmod(h, per_group)
                db = dbias_ref[g, hh * SWA_BLOCK:(hh + 1) * SWA_BLOCK, :]
                for b in range(REL_BUCKETS):
                    tot = _colsum(_rowsum(jnp.where(bk == b, db, 0.0)))
                    res = jnp.where((rowi == b) & (coli == h), tot, res)
            drb_ref[...] = res

    in_specs = _swa_specs(n_heads, qcol, kcol, vcol) + [pl.BlockSpec((SWA_BLOCK, width), lambda i: (i, 0))]
    return _call(
        body, "swa_bwd", (nb,),
        in_specs=in_specs,
        out_specs=[pl.BlockSpec((SWA_BLOCK, width), lambda i: (i, 0)),
                   _full((s, LANES)), _full((s, LANES)), _full((8, LANES)), _full((REL_BUCKETS, LANES))],
        out_shape=[jax.ShapeDtypeStruct((s, width), BF16), jax.ShapeDtypeStruct((s, LANES), F32),
                   jax.ShapeDtypeStruct((s, LANES), F32), jax.ShapeDtypeStruct((8, LANES), F32),
                   jax.ShapeDtypeStruct((REL_BUCKETS, LANES), F32)],
        args=(proj, proj, proj, proj, proj, bucket, rel_bias, sinks, dout),
        scratch_shapes=[pltpu.VMEM((n_groups, per_group * SWA_BLOCK, 2 * SWA_BLOCK), F32),
                        pltpu.VMEM((n_groups, per_group * SWA_BLOCK, 2 * SWA_BLOCK), F32)],
        comm=comm)


def _proj_bwd(dproj, w_in, du1, x, g_in, comm=None):
    s, d = x.shape
    cols = w_in.shape[1]
    tm = min(ROW_TILE, s)

    def body(dp_ref, w_ref, du_ref, x_ref, g_ref, dx_ref, st_ref):
        i = pl.program_id(0)

        @pl.when(i == 0)
        def _():
            st_ref[...] = jnp.zeros_like(st_ref)

        dh0 = _dot_nt(dp_ref[...], w_ref[...]) + ALPHA * du_ref[...]
        xhat, r = _ln_hat(x_ref[...])
        st_ref[0:1, :] += _colsum(dh0 * xhat)
        st_ref[1:2, :] += _colsum(dh0)
        dx_ref[...] = _ln_bwd(dh0 * g_ref[...], xhat, r)

    row = lambda width: pl.BlockSpec((tm, width), lambda i: (i, 0))
    return _call(
        body, "proj_bwd", (s // tm,),
        in_specs=[row(cols), _full((d, cols)), row(d), row(d), _full((1, d))],
        out_specs=[row(d), _full((8, d))],
        out_shape=[jax.ShapeDtypeStruct((s, d), F32), jax.ShapeDtypeStruct((8, d), F32)],
        args=(dproj, w_in, du1, x, g_in), comm=comm)


def _wgrad(name, a, b, tm, tn):
    s, m = a.shape
    n = b.shape[1]
    ts = min(WGRAD_TOKENS if b.dtype == BF16 else WGRAD_TOKENS // 2, s)

    def body(a_ref, b_ref, o_ref):
        @pl.when(pl.program_id(2) == 0)
        def _():
            o_ref[...] = jnp.zeros_like(o_ref)

        o_ref[...] += _dot_tn(a_ref[...].astype(BF16), b_ref[...].astype(BF16))

    return pl.pallas_call(
        body, name=name, grid=(m // tm, n // tn, s // ts),
        in_specs=[pl.BlockSpec((ts, tm), lambda i, j, k: (k, i)),
                  pl.BlockSpec((ts, tn), lambda i, j, k: (k, j))],
        out_specs=pl.BlockSpec((tm, tn), lambda i, j, k: (i, j)),
        out_shape=jax.ShapeDtypeStruct((m, n), F32),
        compiler_params=_params(),
    )(a, b)


def _adamw_math(w, g, m, v):
    m = ADAM_B1 * m + (1.0 - ADAM_B1) * g
    v = ADAM_B2 * v + (1.0 - ADAM_B2) * (g * g)
    m_hat = m / (1.0 - ADAM_B1 ** ADAM_STEP)
    v_hat = v / (1.0 - ADAM_B2 ** ADAM_STEP)
    delta = -ADAM_LR * (m_hat / (jnp.sqrt(v_hat) + ADAM_EPS) + ADAM_WD * w)
    return delta, m, v


def _adamw(name, landed, w, m, v, tr):
    rows, cols = w.shape

    def body(l_ref, w_ref, m_ref, v_ref, g_ref, d_ref, nm_ref, nv_ref):
        g = l_ref[0].astype(F32)
        for src in range(1, N_DEV):
            g = g + l_ref[src].astype(F32)
        delta, nm, nv = _adamw_math(w_ref[...], g, m_ref[...], v_ref[...])
        g_ref[...] = g
        d_ref[...] = delta
        nm_ref[...] = nm
        nv_ref[...] = nv

    blk = pl.BlockSpec((tr, cols), lambda i: (i, 0))
    shape = jax.ShapeDtypeStruct((rows, cols), F32)
    return pl.pallas_call(
        body, name=name, grid=(rows // tr,),
        in_specs=[pl.BlockSpec((N_DEV, tr, cols), lambda i: (0, i, 0)), blk, blk, blk],
        out_specs=[blk, blk, blk, blk],
        out_shape=[shape, shape, shape, shape],
        compiler_params=_params(),
    )(landed, w, m, v)


def _pack(d, ln_in_g, ln_in_b, ln1_g, ln1_b, ln2_g, ln2_b, sb_g, sw_g, rel_bias, sinks, extra=None):
    tail = [rel_bias.reshape(-1), sinks.reshape(-1)]
    if extra is not None:
        tail.append(extra.reshape(-1))
    tail = jnp.concatenate(tail)
    tail = jnp.concatenate([tail, jnp.zeros((d - tail.shape[0],), F32)])
    rows = [ln_in_g.reshape(-1), ln_in_b.reshape(-1), ln1_g.reshape(-1), ln1_b.reshape(-1),
            ln2_g.reshape(-1), ln2_b.reshape(-1),
            jnp.concatenate([sb_g.reshape(-1), sw_g.reshape(-1)]), tail]
    return jnp.stack(rows)


def _unpack(p, wsb, n_rb, n_sk):
    return [p[0], p[1], p[6, :wsb][None], p[6, wsb:][None], p[7, n_rb:n_rb + n_sk][None],
            p[7, :n_rb].reshape(REL_BUCKETS, -1), p[2][None], p[3][None], p[4][None], p[5][None]]


def kernel(x, ln_in_g, ln_in_b, w_in, sb_norm_g, swa_norm_g, sinks, rel_bias, w_out, ln1_g, ln1_b, w_gate_up, w_down, ln2_g, ln2_b, loss_target, m_ln_in_g, m_ln_in_b, m_w_in, m_sb_norm_g, m_swa_norm_g, m_sinks, m_rel_bias, m_w_out, m_ln1_g, m_ln1_b, m_w_gate_up, m_w_down, m_ln2_g, m_ln2_b, v_ln_in_g, v_ln_in_b, v_w_in, v_sb_norm_g, v_swa_norm_g, v_sinks, v_rel_bias, v_w_out, v_ln1_g, v_ln1_b, v_w_gate_up, v_w_down, v_ln2_g, v_ln2_b):
    x2 = x[0]
    tgt = loss_target[0]
    s, d = x2.shape
    wsb = sb_norm_g.shape[-1]
    wsw = swa_norm_g.shape[-1]
    n_sw_heads = sinks.shape[-1]
    n_pairs = wsb // LANES
    dff = w_down.shape[1] * N_DEV
    assert wsb % LANES == 0 and wsw % LANES == 0 and n_sw_heads * HEAD_DIM == wsw
    assert 3 * wsb % wsw == 0 and dff % LANES == 0 and s % SWA_BLOCK == 0
    qcol = 3 * wsb // wsw
    kcol = (3 * wsb + wsw) // LANES
    vcol = kcol + 1
    assert w_in.shape[-1] * N_DEV == (vcol + 1) * LANES

    big_w = [w_in[0], w_out[0], w_gate_up[0], w_down[0]]
    big_m = [m_w_in[0], m_w_out[0], m_w_gate_up[0], m_w_down[0]]
    big_v = [v_w_in[0], v_w_out[0], v_w_gate_up[0], v_w_down[0]]

    cat_cols = lambda g: jnp.transpose(g, (1, 0, 2)).reshape(g.shape[1], N_DEV * g.shape[2])
    cat_rows = lambda g: g.reshape(N_DEV * g.shape[1], g.shape[2])
    shards = [w.astype(BF16) for w in big_w]
    w_in_f = cat_cols(_exchange("w_in_allgather", shards[:1], ["gather"])[0])

    vec = lambda a: a.reshape(1, -1)
    g_in, b_in = vec(ln_in_g), vec(ln_in_b)
    bucket = jnp.asarray(_swa_bucket_table())

    h0b, proj = _ln_proj(x2, g_in, b_in, w_in_f)
    sb_out, gathered = _sb_fwd(proj, n_pairs, comm=(shards[1:], ["gather"] * 3))
    w_out_f, w_gu_f, w_down_f = cat_rows(gathered[0]), cat_cols(gathered[1]), cat_rows(gathered[2])
    sw_out = _swa_fwd(proj, bucket, rel_bias, sinks, n_sw_heads, qcol, kcol, vcol)
    merged, u1 = _mix_ln1(sb_out, sw_out, x2, g_in, b_in, sb_norm_g, swa_norm_g, w_out_f)
    du2, st_ln2 = _ffn_fwd(u1, ln1_g, ln1_b, w_gu_f, w_down_f, ln2_g, ln2_b, tgt)

    split_cols = lambda g: jnp.transpose(g.astype(BF16).reshape(g.shape[0], N_DEV, g.shape[1] // N_DEV), (1, 0, 2))
    split_rows = lambda g: g.astype(BF16).reshape(N_DEV, g.shape[0] // N_DEV, g.shape[1])
    tile_m = min(512, d)
    h1b, act, dgu = _ffn_bwd_act(u1, du2, ln1_g, ln1_b, w_gu_f, jnp.transpose(w_down_f))
    du1, st_ln1 = _ffn_bwd_in(dgu, jnp.transpose(w_gu_f), u1, du2, ln1_g)
    gw_gu = _wgrad("wgrad_gate_up", h1b, dgu, tile_m, dff // 2)
    gw_down = _wgrad("wgrad_down", act, du2, dff // 2, d)
    gw_out = _wgrad("wgrad_out", merged, du1, tile_m, d)
    dsb, dsw, st_rms = _mix_bwd(du1, w_out_f, sb_out, sw_out, sb_norm_g, swa_norm_g)
    (dq_sb, dk_sb, dv_sb), (land_gu, land_out) = _sb_bwd(
        proj, dsb, sb_out, n_pairs, comm=([split_cols(gw_gu), split_rows(gw_out)], ["scatter"] * 2))
    (dq_sw, dk_sw, dv_sw, st_sink, st_rb), (land_down,) = _swa_bwd(
        proj, dsw, bucket, rel_bias, sinks, n_sw_heads, qcol, kcol, vcol,
        comm=([split_rows(gw_down)], ["scatter"]))
    dproj = jnp.concatenate([dq_sb, dk_sb.astype(BF16), dv_sb.astype(BF16), dq_sw,
                             dk_sw.astype(BF16), dv_sw.astype(BF16)], axis=1)
    gw_in = _wgrad("wgrad_in", h0b, dproj, tile_m, dproj.shape[1] // 2)
    (grad_x, st_in), (land_in,) = _proj_bwd(dproj, w_in_f, du1, x2, g_in, comm=([split_cols(gw_in)], ["scatter"]))

    n_rb = rel_bias.size
    small = _pack(d, st_in[0], st_in[1], st_ln1[0], st_ln1[1], st_ln2[0], st_ln2[1],
                  st_rms[0, :wsb], st_rms[0, wsb:], st_rb[:, :n_sw_heads], st_sink[:n_sw_heads, 0],
                  extra=st_ln2[2, 0:1])
    land_small = _exchange("small_grads_allgather", [small], ["gather"])[0]
    landed = [land_in, land_out, land_gu, land_down, land_small]

    big = []
    for name, land, w, m, v in zip(["adamw_in", "adamw_out", "adamw_gate_up", "adamw_down"], landed[:4], big_w, big_m, big_v):
        rows = w.shape[0]
        tr = 128 if rows % 128 == 0 else rows // 2
        big.append(_adamw(name, land, w, m, v, tr))

    small_w = _pack(d, ln_in_g, ln_in_b, ln1_g, ln1_b, ln2_g, ln2_b, sb_norm_g, swa_norm_g, rel_bias, sinks)
    small_m = _pack(d, m_ln_in_g, m_ln_in_b, m_ln1_g, m_ln1_b, m_ln2_g, m_ln2_b, m_sb_norm_g, m_swa_norm_g, m_rel_bias, m_sinks)
    small_v = _pack(d, v_ln_in_g, v_ln_in_b, v_ln1_g, v_ln1_b, v_ln2_g, v_ln2_b, v_sb_norm_g, v_swa_norm_g, v_rel_bias, v_sinks)
    sg, sd, sm, sv = _adamw("adamw_small", landed[4], small_w, small_m, small_v, 8)
    n_sk = sinks.size
    loss = sg[7, n_rb + n_sk]

    def leaves(idx):
        sm_l = _unpack([sg, sd, sm, sv][idx], wsb, n_rb, n_sk)
        bg = [b[idx][None] for b in big]
        return [sm_l[0], sm_l[1], bg[0], sm_l[2], sm_l[3], sm_l[4], sm_l[5], bg[1], sm_l[6], sm_l[7], bg[2], bg[3], sm_l[8], sm_l[9]]

    return (loss, grad_x[None], *leaves(0), *leaves(1), *leaves(2), *leaves(3))
```

```python
import functools
import math

import numpy as np
import jax
import jax.numpy as jnp
from jax import lax
from jax.experimental import pallas as pl
from jax.experimental.pallas import tpu as pltpu

F32 = jnp.float32
BF16 = jnp.bfloat16
MESH = pl.DeviceIdType.MESH

N_DEV = 8
LANES = 128
HEAD_DIM = 64
SCALE = HEAD_DIM ** -0.5
SWA_BLOCK = 128
REL_BUCKETS = 32
REL_MAX_DIST = 128
ALPHA = 2.0 ** 0.25
LN_EPS = 1e-5
RMS_EPS = 1e-6
ADAM_LR = 0.001
ADAM_B1 = 0.9
ADAM_B2 = 0.999
ADAM_EPS = 1e-08
ADAM_WD = 0.01
ADAM_STEP = 10

ROW_TILE = 512
SB_TILE = 256
FFN_TILE = 256
WGRAD_TOKENS = 2048
SB_UNDERFLOW = -110.0
MIB = 1024 * 1024


def _params(vmem_mib=48):
    return pltpu.CompilerParams(vmem_limit_bytes=vmem_mib * MIB)


def _dot(a, b):
    return jnp.dot(a, b, preferred_element_type=F32)


def _dot_nt(a, b):
    return lax.dot_general(a, b, (((1,), (1,)), ((), ())), preferred_element_type=F32)


def _dot_tn(a, b):
    return lax.dot_general(a, b, (((0,), (0,)), ((), ())), preferred_element_type=F32)


def _ln_hat(x):
    mu = jnp.mean(x, axis=-1, keepdims=True)
    xc = x - mu
    var = jnp.mean(xc * xc, axis=-1, keepdims=True)
    r = lax.rsqrt(var + LN_EPS)
    return xc * r, r


def _ln_bwd(dxhat, xhat, r):
    return r * (dxhat - jnp.mean(dxhat, axis=-1, keepdims=True)
                - xhat * jnp.mean(dxhat * xhat, axis=-1, keepdims=True))


def _colsum(a):
    return jnp.sum(a, axis=0, keepdims=True)


def _rowsum(a):
    return jnp.sum(a, axis=1, keepdims=True)


def _full(shape):
    return pl.BlockSpec(shape, lambda *_: (0,) * len(shape))


def _comm_out_shapes(arrays, kinds):
    shapes = []
    for a, kind in zip(arrays, kinds):
        blk = a.shape if kind == "gather" else a.shape[1:]
        shapes.append(jax.ShapeDtypeStruct((N_DEV,) + tuple(blk), a.dtype))
    return shapes


def _comm_sems(n):
    return [pltpu.SemaphoreType.DMA((n, N_DEV - 1)), pltpu.SemaphoreType.DMA((n, N_DEV - 1)),
            pltpu.SemaphoreType.DMA((n,))]


def _comm_copies(ins, outs, kinds, send_sems, recv_sems, local_sems):
    x, y, c = lax.axis_index("x"), lax.axis_index("y"), lax.axis_index("c")
    me = 4 * x + 2 * y + c

    def src_for(t, dev_lin):
        return ins[t] if kinds[t] == "gather" else ins[t].at[dev_lin]

    local = [pltpu.make_async_copy(src_for(t, me), outs[t].at[me], local_sems.at[t]) for t in range(len(kinds))]
    sends, arrivals = [], []
    for k in range(1, N_DEV):
        px = 1 - x if (k >> 2) & 1 else x
        py = 1 - y if (k >> 1) & 1 else y
        pc = 1 - c if k & 1 else c
        peer_lin = 4 * px + 2 * py + pc
        for t in range(len(kinds)):
            sems = dict(send_sem=send_sems.at[t, k - 1], recv_sem=recv_sems.at[t, k - 1],
                        device_id=(px, py, pc), device_id_type=MESH)
            sends.append(pltpu.make_async_remote_copy(src_ref=src_for(t, peer_lin), dst_ref=outs[t].at[me], **sems))
            arrivals.append(pltpu.make_async_remote_copy(src_ref=src_for(t, peer_lin), dst_ref=outs[t].at[peer_lin], **sems))
    return local, sends, arrivals


def _comm_start(ins, outs, kinds, sems):
    local, sends, _ = _comm_copies(ins, outs, kinds, *sems)
    for cp in local + sends:
        cp.start()


def _comm_finish(ins, outs, kinds, sems):
    local, sends, arrivals = _comm_copies(ins, outs, kinds, *sems)
    for cp in arrivals:
        cp.wait_recv()
    for cp in sends:
        cp.wait_send()
    for cp in local:
        cp.wait()


def _exchange(name, arrays, kinds):
    n = len(arrays)

    def body(*refs):
        ins, outs, sems = refs[:n], refs[n:2 * n], refs[2 * n:]
        _comm_start(ins, outs, kinds, sems)
        _comm_finish(ins, outs, kinds, sems)

    any_spec = pl.BlockSpec(memory_space=pl.ANY)
    return pl.pallas_call(
        body, name=name, out_shape=_comm_out_shapes(arrays, kinds),
        in_specs=[any_spec] * n, out_specs=[any_spec] * n,
        scratch_shapes=_comm_sems(n),
    )(*arrays)


def _call(body, name, grid, in_specs, out_specs, out_shape, args, scratch_shapes=(), comm=None):
    if comm is None:
        outs = pl.pallas_call(body, name=name, grid=grid, in_specs=in_specs, out_specs=out_specs,
                              out_shape=out_shape, scratch_shapes=list(scratch_shapes),
                              compiler_params=_params())(*args)
        return outs, []
    arrays, kinds = comm
    n, n_in, n_out, n_scr = len(arrays), len(in_specs), len(out_specs), len(scratch_shapes)

    def fused(*refs):
        c_in, x_in = refs[:n_in], refs[n_in:n_in + n]
        c_out = refs[n_in + n:n_in + n + n_out]
        x_out = refs[n_in + n + n_out:n_in + 2 * n + n_out]
        rest = refs[n_in + 2 * n + n_out:]
        c_scr, sems = rest[:n_scr], rest[n_scr:]
        ids = [pl.program_id(a) for a in range(len(grid))]
        is_first = functools.reduce(jnp.logical_and, [i == 0 for i in ids])
        is_last = functools.reduce(jnp.logical_and, [i == g - 1 for i, g in zip(ids, grid)])

        @pl.when(is_first)
        def _():
            _comm_start(x_in, x_out, kinds, sems)

        body(*c_in, *c_out, *c_scr)

        @pl.when(is_last)
        def _():
            _comm_finish(x_in, x_out, kinds, sems)

    any_spec = pl.BlockSpec(memory_space=pl.ANY)
    outs = pl.pallas_call(
        fused, name=name, grid=grid,
        in_specs=list(in_specs) + [any_spec] * n, out_specs=list(out_specs) + [any_spec] * n,
        out_shape=list(out_shape) + _comm_out_shapes(arrays, kinds),
        scratch_shapes=list(scratch_shapes) + _comm_sems(n),
        compiler_params=_params())(*args, *arrays)
    return outs[:n_out], outs[n_out:]


def _ln_proj(x, g, b, w_in):
    s, d = x.shape
    cols = w_in.shape[1]
    tm = min(ROW_TILE, s)

    def body(x_ref, g_ref, b_ref, w_ref, h_ref, p_ref):
        xhat, _ = _ln_hat(x_ref[...])
        h = (xhat * g_ref[...] + b_ref[...]).astype(BF16)
        h_ref[...] = h
        p_ref[...] = _dot(h, w_ref[...]).astype(BF16)

    row = lambda width: pl.BlockSpec((tm, width), lambda i: (i, 0))
    return pl.pallas_call(
        body, name="ln_proj", grid=(s // tm,),
        in_specs=[row(d), _full((1, d)), _full((1, d)), _full((d, cols))],
        out_specs=[row(d), row(cols)],
        out_shape=[jax.ShapeDtypeStruct((s, d), BF16), jax.ShapeDtypeStruct((s, cols), BF16)],
        compiler_params=_params(),
    )(x, g, b, w_in)


def _sb_tile_consts(t):
    row = lax.broadcasted_iota(jnp.int32, (t, t), 0)
    col = lax.broadcasted_iota(jnp.int32, (t, t), 1)
    return row, col


def _sb_scores(qh, k_t, upper, carry_l, causal):
    z = _dot_nt(qh, k_t)
    sp = jnp.log(1.0 + jnp.exp(-jnp.abs(z)))
    neg = jnp.minimum(z, 0.0)
    lb = neg - sp
    l1 = (neg - z) - sp
    if causal is not None:
        l1 = jnp.where(causal, l1, 0.0)
    hi = l1.astype(BF16)
    lo = (l1 - hi.astype(F32)).astype(BF16)
    suf = _dot(hi, upper) + _dot(lo, upper) + carry_l
    a = jnp.exp(lb + suf)
    if causal is not None:
        a = jnp.where(causal, a, 0.0)
    return lb, l1, a


def _sb_walk(i, t, visit, init, causal):
    def alive(carry):
        return jnp.max(jnp.maximum(carry[0][0], carry[1][0])) > SB_UNDERFLOW

    carry = visit(pl.multiple_of(i * t, t), init, causal)

    def cond(state):
        j, go, _ = state
        return (j < i) & go

    def body(state):
        j, _, carry = state
        carry = visit(pl.multiple_of((i - 1 - j) * t, t), carry, None)
        return j + 1, alive(carry), carry

    return lax.while_loop(cond, body, (jnp.int32(0), alive(carry), carry))[2]


def _sb_fwd(proj, n_pairs, comm=None):
    s = proj.shape[0]
    t = min(SB_TILE, s)
    nq = s // t

    def body(q_ref, k_ref, v_ref, o_ref):
        i = pl.program_id(1)
        lane = lax.broadcasted_iota(jnp.int32, (1, LANES), 1)
        first = lane < HEAD_DIM
        q2 = q_ref[...] * SCALE
        zero = jnp.zeros_like(q2)
        qs = (jnp.where(first, q2, zero), jnp.where(first, zero, q2))
        row, col = _sb_tile_consts(t)
        upper = (row > col).astype(BF16)
        causal = col < row

        def visit(off, carry, mask):
            k_t = k_ref[pl.ds(off, t), :]
            v_t = v_ref[pl.ds(off, t), :]
            out = []
            for hh in range(2):
                c_l, acc = carry[hh]
                _, l1, a = _sb_scores(qs[hh], k_t, upper, c_l, mask)
                out.append((c_l + _rowsum(l1), acc + _dot(a.astype(BF16), v_t)))
            return tuple(out)

        init = tuple((jnp.zeros((t, 1), F32), jnp.zeros((t, LANES), F32)) for _ in range(2))
        carry = _sb_walk(i, t, visit, init, causal)
        o_ref[...] = jnp.where(first, carry[0][1], carry[1][1])

    outs, landed = _call(
        body, "sb_fwd", (n_pairs, nq),
        in_specs=[pl.BlockSpec((t, LANES), lambda h, i: (i, h)),
                  pl.BlockSpec((s, LANES), lambda h, i: (0, n_pairs + h)),
                  pl.BlockSpec((s, LANES), lambda h, i: (0, 2 * n_pairs + h))],
        out_specs=[pl.BlockSpec((t, LANES), lambda h, i: (i, h))],
        out_shape=[jax.ShapeDtypeStruct((s, n_pairs * LANES), F32)],
        args=(proj, proj, proj), comm=comm)
    return outs[0], landed


def _swa_bucket_table():
    qi = np.arange(SWA_BLOCK)[:, None]
    cj = np.arange(2 * SWA_BLOCK)[None, :]
    dist = qi + SWA_BLOCK - cj
    exact = REL_BUCKETS // 2
    d = np.maximum(dist, 0)
    d_f = np.maximum(d, 1).astype(np.float32)
    large = exact + (np.log(d_f / np.float32(exact)) / np.float32(math.log(REL_MAX_DIST / exact))
                     * np.float32(REL_BUCKETS - exact)).astype(np.int32)
    large = np.minimum(large, REL_BUCKETS - 1)
    return np.where(d < exact, d, large).astype(np.int32)


def _swa_build_bias(bucket_ref, rb_ref, bias_ref, n_groups, per_group):
    bk = bucket_ref[...]
    for g in range(n_groups):
        for hh in range(per_group):
            acc = jnp.zeros(bk.shape, F32)
            for b in range(REL_BUCKETS):
                acc = jnp.where(bk == b, rb_ref[b, g * per_group + hh], acc)
            bias_ref[g, hh * SWA_BLOCK:(hh + 1) * SWA_BLOCK, :] = acc


def _swa_valid(i, reps):
    shape = (reps * SWA_BLOCK, 2 * SWA_BLOCK)
    row = lax.broadcasted_iota(jnp.int32, shape, 0) & (SWA_BLOCK - 1)
    col = lax.broadcasted_iota(jnp.int32, shape, 1)
    dist = row + SWA_BLOCK - col
    return (dist >= 0) & (dist < SWA_BLOCK) & ((col >= SWA_BLOCK) | (i > 0))


def _swa_place(blk, h, group, sel):
    if (h % 2) != group:
        blk = pltpu.roll(blk.astype(F32), HEAD_DIM, axis=1).astype(BF16)
    return jnp.where(sel, blk, jnp.zeros_like(blk))


def _swa_stack(ref, group, per_group, sel):
    parts = []
    for hh in range(per_group):
        h = group * per_group + hh
        parts.append(_swa_place(ref[:, (h // 2) * LANES:(h // 2 + 1) * LANES], h, group, sel))
    return jnp.concatenate(parts, axis=0)


def _swa_unstack(stacked, group, per_group, pieces):
    for hh in range(per_group):
        h = group * per_group + hh
        piece = stacked[hh * SWA_BLOCK:(hh + 1) * SWA_BLOCK, :]
        pieces[h] = pltpu.roll(piece, HEAD_DIM, axis=1) if (h % 2) != group else piece


def _swa_sink_rows(sk_ref, group, per_group):
    rowh = lax.broadcasted_iota(jnp.int32, (per_group * SWA_BLOCK, 1), 0) // SWA_BLOCK
    sink = jnp.zeros((per_group * SWA_BLOCK, 1), F32) + sk_ref[0, group * per_group]
    for hh in range(1, per_group):
        sink = jnp.where(rowh == hh, sk_ref[0, group * per_group + hh], sink)
    return sink


def _swa_probs(q_pos, kcat, bias_h, valid, sink):
    logits = _dot_nt(q_pos, kcat) * SCALE + bias_h
    logits = jnp.where(valid, logits, -jnp.inf)
    m = jnp.maximum(jnp.max(logits, axis=1, keepdims=True), sink)
    p = jnp.exp(logits - m)
    es = jnp.exp(sink - m)
    denom = _rowsum(p) + es
    return p / denom, es / denom


def _swa_specs(n_heads, qcol, kcol, vcol):
    width = n_heads * HEAD_DIM
    prev = lambda col: pl.BlockSpec((SWA_BLOCK, LANES), lambda i: (jnp.maximum(i - 1, 0), col))
    cur = lambda col: pl.BlockSpec((SWA_BLOCK, LANES), lambda i: (i, col))
    return [pl.BlockSpec((SWA_BLOCK, width), lambda i: (i, qcol)),
            prev(kcol), cur(kcol), prev(vcol), cur(vcol),
            _full((SWA_BLOCK, 2 * SWA_BLOCK)),
            pl.BlockSpec(memory_space=pltpu.SMEM), pl.BlockSpec(memory_space=pltpu.SMEM)]


def _swa_fwd(proj, bucket, rel_bias, sinks, n_heads, qcol, kcol, vcol):
    s = proj.shape[0]
    width = n_heads * HEAD_DIM
    n_groups = LANES // HEAD_DIM
    per_group = n_heads // n_groups

    def body(q_ref, kp_ref, kc_ref, vp_ref, vc_ref, bucket_ref, rb_ref, sk_ref, o_ref, bias_ref):
        i = pl.program_id(0)

        @pl.when(i == 0)
        def _():
            _swa_build_bias(bucket_ref, rb_ref, bias_ref, n_groups, per_group)

        lane = lax.broadcasted_iota(jnp.int32, (1, LANES), 1)
        first = lane < HEAD_DIM
        valid = _swa_valid(i, per_group)
        kcat = jnp.concatenate([kp_ref[...], kc_ref[...]], axis=0)
        vcat = jnp.concatenate([vp_ref[...], vc_ref[...]], axis=0)
        pieces = {}
        for g in range(n_groups):
            sel = first if g == 0 else jnp.logical_not(first)
            prob, _ = _swa_probs(_swa_stack(q_ref, g, per_group, sel), kcat, bias_ref[g], valid,
                                 _swa_sink_rows(sk_ref, g, per_group))
            _swa_unstack(_dot(prob.astype(BF16), vcat), g, per_group, pieces)
        for j in range(n_heads // 2):
            o_ref[:, j * LANES:(j + 1) * LANES] = jnp.where(first, pieces[2 * j], pieces[2 * j + 1])

    return pl.pallas_call(
        body, name="swa_fwd", grid=(s // SWA_BLOCK,),
        in_specs=_swa_specs(n_heads, qcol, kcol, vcol),
        out_specs=pl.BlockSpec((SWA_BLOCK, width), lambda i: (i, 0)),
        out_shape=jax.ShapeDtypeStruct((s, width), F32),
        scratch_shapes=[pltpu.VMEM((n_groups, per_group * SWA_BLOCK, 2 * SWA_BLOCK), F32)],
        compiler_params=_params(),
    )(proj, proj, proj, proj, proj, bucket, rel_bias, sinks)


def _rms_fwd(o, g):
    r = lax.rsqrt(jnp.mean(o * o, axis=-1, keepdims=True) + RMS_EPS)
    n = o * r
    return n, r, n * g


def _mix_ln1(sb_out, sw_out, x, g_in, b_in, sb_g, sw_g, w_out):
    s, d = x.shape
    wsb, wsw = sb_out.shape[1], sw_out.shape[1]
    tm = min(ROW_TILE, s)

    def body(sb_ref, sw_ref, x_ref, gi_ref, bi_ref, sbg_ref, swg_ref, w_ref, mg_ref, u_ref):
        _, _, m_sb = _rms_fwd(sb_ref[...], sbg_ref[...])
        _, _, m_sw = _rms_fwd(sw_ref[...], swg_ref[...])
        m_sb = m_sb.astype(BF16)
        m_sw = m_sw.astype(BF16)
        mg_ref[:, :wsb] = m_sb
        mg_ref[:, wsb:] = m_sw
        mix = _dot(m_sb, w_ref[:wsb, :]) + _dot(m_sw, w_ref[wsb:, :])
        xhat, _ = _ln_hat(x_ref[...])
        h0 = xhat * gi_ref[...] + bi_ref[...]
        u_ref[...] = ALPHA * h0 + mix

    row = lambda width: pl.BlockSpec((tm, width), lambda i: (i, 0))
    return pl.pallas_call(
        body, name="mix_ln1", grid=(s // tm,),
        in_specs=[row(wsb), row(wsw), row(d), _full((1, d)), _full((1, d)),
                  _full((1, wsb)), _full((1, wsw)), _full((wsb + wsw, d))],
        out_specs=[row(wsb + wsw), row(d)],
        out_shape=[jax.ShapeDtypeStruct((s, wsb + wsw), BF16), jax.ShapeDtypeStruct((s, d), F32)],
        compiler_params=_params(),
    )(sb_out, sw_out, x, g_in, b_in, sb_g, sw_g, w_out)


def _ffn_fwd(u1, g1, b1, w_gu, w_down, g2, b2, target):
    s, d = u1.shape
    dff = w_down.shape[0]
    tm = min(FFN_TILE, s)

    def body(u_ref, g1_ref, b1_ref, wgu_hbm, wd_hbm, g2_ref, b2_ref, t_ref, du_ref, st_ref, wgu_ref, wd_ref):
        @pl.when(pl.program_id(0) == 0)
        def _():
            pltpu.sync_copy(wgu_hbm, wgu_ref)
            pltpu.sync_copy(wd_hbm, wd_ref)
            st_ref[...] = jnp.zeros_like(st_ref)

        xhat, _ = _ln_hat(u_ref[...])
        h1 = xhat * g1_ref[...] + b1_ref[...]
        h1b = h1.astype(BF16)
        gate = _dot(h1b, wgu_ref[:, :dff])
        up = _dot(h1b, wgu_ref[:, dff:])
        act = gate * jax.nn.sigmoid(gate) * up
        u2 = ALPHA * h1 + _dot(act.astype(BF16), wd_ref[...])
        xhat2, r2 = _ln_hat(u2)
        diff = xhat2 * g2_ref[...] + b2_ref[...] - t_ref[...]
        dh2 = diff * (1.0 / d)
        st_ref[0:1, :] += _colsum(dh2 * xhat2)
        st_ref[1:2, :] += _colsum(dh2)
        st_ref[2:3, :] += jnp.broadcast_to(_colsum(_rowsum(diff * diff)) * (0.5 / d), (1, d))
        du_ref[...] = _ln_bwd(dh2 * g2_ref[...], xhat2, r2)

    row = pl.BlockSpec((tm, d), lambda i: (i, 0))
    hbm = pl.BlockSpec(memory_space=pl.ANY)
    return pl.pallas_call(
        body, name="ffn_fwd", grid=(s // tm,),
        in_specs=[row, _full((1, d)), _full((1, d)), hbm, hbm, _full((1, d)), _full((1, d)), row],
        out_specs=[row, _full((8, d))],
        out_shape=[jax.ShapeDtypeStruct((s, d), F32), jax.ShapeDtypeStruct((8, d), F32)],
        scratch_shapes=[pltpu.VMEM(w_gu.shape, BF16), pltpu.VMEM(w_down.shape, BF16)],
        compiler_params=_params(56),
    )(u1, g1, b1, w_gu, w_down, g2, b2, target)


def _ffn_bwd_act(u1, du2, g1, b1, w_gu, w_down_t):
    s, d = u1.shape
    dff = w_down_t.shape[1]
    tm = min(FFN_TILE, s)

    def body(u_ref, du2_ref, g1_ref, b1_ref, wgu_hbm, wdt_hbm, h1b_ref, act_ref, dgu_ref, wgu_ref, wdt_ref):
        @pl.when(pl.program_id(0) == 0)
        def _():
            pltpu.sync_copy(wgu_hbm, wgu_ref)
            pltpu.sync_copy(wdt_hbm, wdt_ref)

        xhat, _ = _ln_hat(u_ref[...])
        h1b = (xhat * g1_ref[...] + b1_ref[...]).astype(BF16)
        h1b_ref[...] = h1b
        gate = _dot(h1b, wgu_ref[:, :dff])
        up = _dot(h1b, wgu_ref[:, dff:])
        dact = _dot(du2_ref[...].astype(BF16), wdt_ref[...])
        sg = jax.nn.sigmoid(gate)
        silu = gate * sg
        act_ref[...] = (silu * up).astype(BF16)
        dgu_ref[:, :dff] = (dact * up * (sg * (1.0 + gate * (1.0 - sg)))).astype(BF16)
        dgu_ref[:, dff:] = (dact * silu).astype(BF16)

    row = lambda width: pl.BlockSpec((tm, width), lambda i: (i, 0))
    hbm = pl.BlockSpec(memory_space=pl.ANY)
    return pl.pallas_call(
        body, name="ffn_bwd_act", grid=(s // tm,),
        in_specs=[row(d), row(d), _full((1, d)), _full((1, d)), hbm, hbm],
        out_specs=[row(d), row(dff), row(2 * dff)],
        out_shape=[jax.ShapeDtypeStruct((s, d), BF16), jax.ShapeDtypeStruct((s, dff), BF16),
                   jax.ShapeDtypeStruct((s, 2 * dff), BF16)],
        scratch_shapes=[pltpu.VMEM(w_gu.shape, BF16), pltpu.VMEM(w_down_t.shape, BF16)],
        compiler_params=_params(56),
    )(u1, du2, g1, b1, w_gu, w_down_t)


def _ffn_bwd_in(dgu, w_gu_t, u1, du2, g1):
    s, d = u1.shape
    tm = min(FFN_TILE, s)

    def body(dgu_ref, wt_hbm, u_ref, du2_ref, g1_ref, du1_ref, st_ref, wt_ref):
        @pl.when(pl.program_id(0) == 0)
        def _():
            pltpu.sync_copy(wt_hbm, wt_ref)
            st_ref[...] = jnp.zeros_like(st_ref)

        dh1 = _dot(dgu_ref[...], wt_ref[...]) + ALPHA * du2_ref[...]
        xhat, r = _ln_hat(u_ref[...])
        st_ref[0:1, :] += _colsum(dh1 * xhat)
        st_ref[1:2, :] += _colsum(dh1)
        du1_ref[...] = _ln_bwd(dh1 * g1_ref[...], xhat, r)

    row = lambda width: pl.BlockSpec((tm, width), lambda i: (i, 0))
    return pl.pallas_call(
        body, name="ffn_bwd_in", grid=(s // tm,),
        in_specs=[row(dgu.shape[1]), pl.BlockSpec(memory_space=pl.ANY), row(d), row(d), _full((1, d))],
        out_specs=[row(d), _full((8, d))],
        out_shape=[jax.ShapeDtypeStruct((s, d), F32), jax.ShapeDtypeStruct((8, d), F32)],
        scratch_shapes=[pltpu.VMEM(w_gu_t.shape, BF16)],
        compiler_params=_params(56),
    )(dgu, w_gu_t, u1, du2, g1)


def _rms_bwd(dm, o, g):
    n, r, _ = _rms_fwd(o, g)
    dn = dm * g
    return r * (dn - n * jnp.mean(dn * n, axis=-1, keepdims=True)), _colsum(dm * n)


def _mix_bwd(du1, w_out, sb_out, sw_out, sb_g, sw_g):
    s, d = du1.shape
    wsb, wsw = sb_out.shape[1], sw_out.shape[1]
    tm = min(ROW_TILE, s)

    def body(du_ref, w_ref, sb_ref, sw_ref, sbg_ref, swg_ref, dsb_ref, dsw_ref, st_ref):
        i = pl.program_id(0)

        @pl.when(i == 0)
        def _():
            st_ref[...] = jnp.zeros_like(st_ref)

        dmerged = _dot_nt(du_ref[...].astype(BF16), w_ref[...])
        dsb, gsb = _rms_bwd(dmerged[:, :wsb], sb_ref[...], sbg_ref[...])
        dsw, gsw = _rms_bwd(dmerged[:, wsb:], sw_ref[...], swg_ref[...])
        dsb_ref[...] = dsb.astype(BF16)
        dsw_ref[...] = dsw.astype(BF16)
        st_ref[0:1, :wsb] += gsb
        st_ref[0:1, wsb:] += gsw

    row = lambda width: pl.BlockSpec((tm, width), lambda i: (i, 0))
    return pl.pallas_call(
        body, name="mix_bwd", grid=(s // tm,),
        in_specs=[row(d), _full((wsb + wsw, d)), row(wsb), row(wsw), _full((1, wsb)), _full((1, wsw))],
        out_specs=[row(wsb), row(wsw), _full((8, wsb + wsw))],
        out_shape=[jax.ShapeDtypeStruct((s, wsb), BF16), jax.ShapeDtypeStruct((s, wsw), BF16),
                   jax.ShapeDtypeStruct((8, wsb + wsw), F32)],
        compiler_params=_params(),
    )(du1, w_out, sb_out, sw_out, sb_g, sw_g)


def _sb_bwd(proj, dout, out, n_pairs, comm=None):
    s = proj.shape[0]
    t = min(SB_TILE, s)
    nq = s // t
    width = n_pairs * LANES

    def body(q_ref, k_ref, v_ref, do_ref, o_ref, dq_ref, dk_ref, dv_ref):
        i = pl.program_id(1)

        @pl.when(i == 0)
        def _():
            dk_ref[...] = jnp.zeros_like(dk_ref)
            dv_ref[...] = jnp.zeros_like(dv_ref)

        lane = lax.broadcasted_iota(jnp.int32, (1, LANES), 1)
        first = lane < HEAD_DIM
        q2 = q_ref[...] * SCALE
        do2 = do_ref[...]
        zero = jnp.zeros_like(q2)
        qs = (jnp.where(first, q2, zero), jnp.where(first, zero, q2))
        dos = (jnp.where(first, do2, zero), jnp.where(first, zero, do2))
        prod = do2.astype(F32) * o_ref[...]
        totals = (_rowsum(jnp.where(first, prod, 0.0)), _rowsum(jnp.where(first, 0.0, prod)))
        row, col = _sb_tile_consts(t)
        upper = (row > col).astype(BF16)
        incl = (row >= col).astype(BF16)
        causal = col < row

        def visit(off, carry, mask):
            k_t = k_ref[pl.ds(off, t), :]
            v_t = v_ref[pl.ds(off, t), :]
            out = []
            dk_t = jnp.zeros((t, LANES), F32)
            dv_t = jnp.zeros((t, LANES), F32)
            for hh in range(2):
                c_l, c_e, dq = carry[hh]
                lb, l1, a = _sb_scores(qs[hh], k_t, upper, c_l, mask)
                a_b = a.astype(BF16)
                d_e = _dot_nt(dos[hh], v_t) * a_b.astype(F32)
                d_l = totals[hh] - (_dot(d_e.astype(BF16), incl) + c_e)
                dz = d_e - jnp.exp(lb) * (d_e + d_l)
                if mask is not None:
                    dz = jnp.where(mask, dz, 0.0)
                dzb = dz.astype(BF16)
                dk_t += _dot_tn(dzb, qs[hh])
                dv_t += _dot_tn(a_b, dos[hh])
                out.append((c_l + _rowsum(l1), c_e + _rowsum(d_e), dq + _dot(dzb, k_t)))
            dk_ref[pl.ds(off, t), :] += dk_t
            dv_ref[pl.ds(off, t), :] += dv_t
            return tuple(out)

        init = tuple((jnp.zeros((t, 1), F32), jnp.zeros((t, 1), F32), jnp.zeros((t, LANES), F32))
                     for _ in range(2))
        carry = _sb_walk(i, t, visit, init, causal)
        dq_ref[...] = (jnp.where(first, carry[0][2], carry[1][2]) * SCALE).astype(BF16)

    qblk = pl.BlockSpec((t, LANES), lambda h, i: (i, h))
    whole = pl.BlockSpec((s, LANES), lambda h, i: (0, h))
    return _call(
        body, "sb_bwd", (n_pairs, nq),
        in_specs=[qblk,
                  pl.BlockSpec((s, LANES), lambda h, i: (0, n_pairs + h)),
                  pl.BlockSpec((s, LANES), lambda h, i: (0, 2 * n_pairs + h)),
                  qblk, qblk],
        out_specs=[qblk, whole, whole],
        out_shape=[jax.ShapeDtypeStruct((s, width), BF16), jax.ShapeDtypeStruct((s, width), F32),
                   jax.ShapeDtypeStruct((s, width), F32)],
        args=(proj, proj, proj, dout, out), comm=comm)


def _swa_bwd(proj, dout, bucket, rel_bias, sinks, n_heads, qcol, kcol, vcol, comm=None):
    s = proj.shape[0]
    width = n_heads * HEAD_DIM
    n_groups = LANES // HEAD_DIM
    per_group = n_heads // n_groups
    nb = s // SWA_BLOCK

    def body(q_ref, kp_ref, kc_ref, vp_ref, vc_ref, bucket_ref, rb_ref, sk_ref, do_ref,
             dq_ref, dk_ref, dv_ref, dsk_ref, drb_ref, bias_ref, dbias_ref):
        i = pl.program_id(0)

        @pl.when(i == 0)
        def _():
            _swa_build_bias(bucket_ref, rb_ref, bias_ref, n_groups, per_group)
            dbias_ref[...] = jnp.zeros_like(dbias_ref)
            dk_ref[...] = jnp.zeros_like(dk_ref)
            dv_ref[...] = jnp.zeros_like(dv_ref)
            dsk_ref[...] = jnp.zeros_like(dsk_ref)

        lane = lax.broadcasted_iota(jnp.int32, (1, LANES), 1)
        first = lane < HEAD_DIM
        valid = _swa_valid(i, per_group)
        kcat = jnp.concatenate([kp_ref[...], kc_ref[...]], axis=0)
        vcat = jnp.concatenate([vp_ref[...], vc_ref[...]], axis=0)
        dkcat = jnp.zeros((2 * SWA_BLOCK, LANES), F32)
        dvcat = jnp.zeros((2 * SWA_BLOCK, LANES), F32)
        pieces = {}
        for g in range(n_groups):
            sel = first if g == 0 else jnp.logical_not(first)
            q_g = _swa_stack(q_ref, g, per_group, sel)
            do_g = _swa_stack(do_ref, g, per_group, sel)
            prob, p_sink = _swa_probs(q_g, kcat, bias_ref[g], valid, _swa_sink_rows(sk_ref, g, per_group))
            dprob = _dot_nt(do_g, vcat)
            delta = _rowsum(prob * dprob)
            dlog = prob * (dprob - delta)
            sink_term = p_sink * delta
            for hh in range(per_group):
                h = g * per_group + hh
                tot = _colsum(sink_term[hh * SWA_BLOCK:(hh + 1) * SWA_BLOCK, :])
                dsk_ref[h:h + 1, :] += jnp.broadcast_to(-tot, (1, LANES))
            dbias_ref[g] += dlog
            dlb = (dlog * SCALE).astype(BF16)
            _swa_unstack(_dot(dlb, kcat), g, per_group, pieces)
            dkcat += _dot_tn(dlb, q_g)
            dvcat += _dot_tn(prob.astype(BF16), do_g)
        for j in range(n_heads // 2):
            dq_ref[:, j * LANES:(j + 1) * LANES] = jnp.where(first, pieces[2 * j], pieces[2 * j + 1]).astype(BF16)

        cur = pl.multiple_of(i * SWA_BLOCK, SWA_BLOCK)
        dk_ref[pl.ds(cur, SWA_BLOCK), :] += dkcat[SWA_BLOCK:, :]
        dv_ref[pl.ds(cur, SWA_BLOCK), :] += dvcat[SWA_BLOCK:, :]

        @pl.when(i > 0)
        def _():
            prv = pl.multiple_of((i - 1) * SWA_BLOCK, SWA_BLOCK)
            dk_ref[pl.ds(prv, SWA_BLOCK), :] += dkcat[:SWA_BLOCK, :]
            dv_ref[pl.ds(prv, SWA_BLOCK), :] += dvcat[:SWA_BLOCK, :]

        @pl.when(i == nb - 1)
        def _():
            bk = bucket_ref[...]
            rowi = lax.broadcasted_iota(jnp.int32, (REL_BUCKETS, LANES), 0)
            coli = lax.broadcasted_iota(jnp.int32, (REL_BUCKETS, LANES), 1)
            res = jnp.zeros((REL_BUCKETS, LANES), F32)
            for h in range(n_heads):
                g, hh = divmod(h, per_group)
                db = dbias_ref[g, hh * SWA_BLOCK:(hh + 1) * SWA_BLOCK, :]
                for b in range(REL_BUCKETS):
                    tot = _colsum(_rowsum(jnp.where(bk == b, db, 0.0)))
                    res = jnp.where((rowi == b) & (coli == h), tot, res)
            drb_ref[...] = res

    in_specs = _swa_specs(n_heads, qcol, kcol, vcol) + [pl.BlockSpec((SWA_BLOCK, width), lambda i: (i, 0))]
    return _call(
        body, "swa_bwd", (nb,),
        in_specs=in_specs,
        out_specs=[pl.BlockSpec((SWA_BLOCK, width), lambda i: (i, 0)),
                   _full((s, LANES)), _full((s, LANES)), _full((8, LANES)), _full((REL_BUCKETS, LANES))],
        out_shape=[jax.ShapeDtypeStruct((s, width), BF16), jax.ShapeDtypeStruct((s, LANES), F32),
                   jax.ShapeDtypeStruct((s, LANES), F32), jax.ShapeDtypeStruct((8, LANES), F32),
                   jax.ShapeDtypeStruct((REL_BUCKETS, LANES), F32)],
        args=(proj, proj, proj, proj, proj, bucket, rel_bias, sinks, dout),
        scratch_shapes=[pltpu.VMEM((n_groups, per_group * SWA_BLOCK, 2 * SWA_BLOCK), F32),
                        pltpu.VMEM((n_groups, per_group * SWA_BLOCK, 2 * SWA_BLOCK), F32)],
        comm=comm)


def _proj_bwd(dproj, w_in, du1, x, g_in, comm=None):
    s, d = x.shape
    cols = w_in.shape[1]
    tm = min(ROW_TILE, s)

    def body(dp_ref, w_ref, du_ref, x_ref, g_ref, dx_ref, st_ref):
        i = pl.program_id(0)

        @pl.when(i == 0)
        def _():
            st_ref[...] = jnp.zeros_like(st_ref)

        dh0 = _dot_nt(dp_ref[...], w_ref[...]) + ALPHA * du_ref[...]
        xhat, r = _ln_hat(x_ref[...])
        st_ref[0:1, :] += _colsum(dh0 * xhat)
        st_ref[1:2, :] += _colsum(dh0)
        dx_ref[...] = _ln_bwd(dh0 * g_ref[...], xhat, r)

    row = lambda width: pl.BlockSpec((tm, width), lambda i: (i, 0))
    return _call(
        body, "proj_bwd", (s // tm,),
        in_specs=[row(cols), _full((d, cols)), row(d), row(d), _full((1, d))],
        out_specs=[row(d), _full((8, d))],
        out_shape=[jax.ShapeDtypeStruct((s, d), F32), jax.ShapeDtypeStruct((8, d), F32)],
        args=(dproj, w_in, du1, x, g_in), comm=comm)


def _wgrad(name, a, b, tm, tn):
    s, m = a.shape
    n = b.shape[1]
    ts = min(WGRAD_TOKENS if b.dtype == BF16 else WGRAD_TOKENS // 2, s)

    n_k = s // ts

    def body(a_ref, b_ref, o_ref, acc_ref):
        k = pl.program_id(2)

        @pl.when(k == 0)
        def _():
            acc_ref[...] = jnp.zeros_like(acc_ref)

        acc_ref[...] += _dot_tn(a_ref[...].astype(BF16), b_ref[...].astype(BF16))

        @pl.when(k == n_k - 1)
        def _():
            o_ref[...] = acc_ref[...].astype(BF16)

    return pl.pallas_call(
        body, name=name, grid=(m // tm, n // tn, n_k),
        in_specs=[pl.BlockSpec((ts, tm), lambda i, j, k: (k, i)),
                  pl.BlockSpec((ts, tn), lambda i, j, k: (k, j))],
        out_specs=pl.BlockSpec((tm, tn), lambda i, j, k: (i, j)),
        out_shape=jax.ShapeDtypeStruct((m, n), BF16),
        scratch_shapes=[pltpu.VMEM((tm, tn), F32)],
        compiler_params=_params(),
    )(a, b)


def _adamw_math(w, g, m, v):
    m = ADAM_B1 * m + (1.0 - ADAM_B1) * g
    v = ADAM_B2 * v + (1.0 - ADAM_B2) * (g * g)
    m_hat = m / (1.0 - ADAM_B1 ** ADAM_STEP)
    v_hat = v / (1.0 - ADAM_B2 ** ADAM_STEP)
    delta = -ADAM_LR * (m_hat / (jnp.sqrt(v_hat) + ADAM_EPS) + ADAM_WD * w)
    return delta, m, v


def _adamw(name, landed, w, m, v, tr):
    rows, cols = w.shape

    def body(l_ref, w_ref, m_ref, v_ref, g_ref, d_ref, nm_ref, nv_ref):
        g = l_ref[0].astype(F32)
        for src in range(1, N_DEV):
            g = g + l_ref[src].astype(F32)
        delta, nm, nv = _adamw_math(w_ref[...], g, m_ref[...], v_ref[...])
        g_ref[...] = g
        d_ref[...] = delta
        nm_ref[...] = nm
        nv_ref[...] = nv

    blk = pl.BlockSpec((tr, cols), lambda i: (i, 0))
    shape = jax.ShapeDtypeStruct((rows, cols), F32)
    return pl.pallas_call(
        body, name=name, grid=(rows // tr,),
        in_specs=[pl.BlockSpec((N_DEV, tr, cols), lambda i: (0, i, 0)), blk, blk, blk],
        out_specs=[blk, blk, blk, blk],
        out_shape=[shape, shape, shape, shape],
        compiler_params=_params(),
    )(landed, w, m, v)


def _pack(d, ln_in_g, ln_in_b, ln1_g, ln1_b, ln2_g, ln2_b, sb_g, sw_g, rel_bias, sinks, extra=None):
    tail = [rel_bias.reshape(-1), sinks.reshape(-1)]
    if extra is not None:
        tail.append(extra.reshape(-1))
    tail = jnp.concatenate(tail)
    tail = jnp.concatenate([tail, jnp.zeros((d - tail.shape[0],), F32)])
    rows = [ln_in_g.reshape(-1), ln_in_b.reshape(-1), ln1_g.reshape(-1), ln1_b.reshape(-1),
            ln2_g.reshape(-1), ln2_b.reshape(-1),
            jnp.concatenate([sb_g.reshape(-1), sw_g.reshape(-1)]), tail]
    return jnp.stack(rows)


def _unpack(p, wsb, n_rb, n_sk):
    return [p[0], p[1], p[6, :wsb][None], p[6, wsb:][None], p[7, n_rb:n_rb + n_sk][None],
            p[7, :n_rb].reshape(REL_BUCKETS, -1), p[2][None], p[3][None], p[4][None], p[5][None]]


def kernel(x, ln_in_g, ln_in_b, w_in, sb_norm_g, swa_norm_g, sinks, rel_bias, w_out, ln1_g, ln1_b, w_gate_up, w_down, ln2_g, ln2_b, loss_target, m_ln_in_g, m_ln_in_b, m_w_in, m_sb_norm_g, m_swa_norm_g, m_sinks, m_rel_bias, m_w_out, m_ln1_g, m_ln1_b, m_w_gate_up, m_w_down, m_ln2_g, m_ln2_b, v_ln_in_g, v_ln_in_b, v_w_in, v_sb_norm_g, v_swa_norm_g, v_sinks, v_rel_bias, v_w_out, v_ln1_g, v_ln1_b, v_w_gate_up, v_w_down, v_ln2_g, v_ln2_b):
    x2 = x[0]
    tgt = loss_target[0]
    s, d = x2.shape
    wsb = sb_norm_g.shape[-1]
    wsw = swa_norm_g.shape[-1]
    n_sw_heads = sinks.shape[-1]
    n_pairs = wsb // LANES
    dff = w_down.shape[1] * N_DEV
    assert wsb % LANES == 0 and wsw % LANES == 0 and n_sw_heads * HEAD_DIM == wsw
    assert 3 * wsb % wsw == 0 and dff % LANES == 0 and s % SWA_BLOCK == 0
    qcol = 3 * wsb // wsw
    kcol = (3 * wsb + wsw) // LANES
    vcol = kcol + 1
    assert w_in.shape[-1] * N_DEV == (vcol + 1) * LANES

    big_w = [w_in[0], w_out[0], w_gate_up[0], w_down[0]]
    big_m = [m_w_in[0], m_w_out[0], m_w_gate_up[0], m_w_down[0]]
    big_v = [v_w_in[0], v_w_out[0], v_w_gate_up[0], v_w_down[0]]

    cat_cols = lambda g: jnp.transpose(g, (1, 0, 2)).reshape(g.shape[1], N_DEV * g.shape[2])
    cat_rows = lambda g: g.reshape(N_DEV * g.shape[1], g.shape[2])
    shards = [w.astype(BF16) for w in big_w]
    w_in_f = cat_cols(_exchange("w_in_allgather", shards[:1], ["gather"])[0])

    vec = lambda a: a.reshape(1, -1)
    g_in, b_in = vec(ln_in_g), vec(ln_in_b)
    bucket = jnp.asarray(_swa_bucket_table())

    h0b, proj = _ln_proj(x2, g_in, b_in, w_in_f)
    sb_out, gathered = _sb_fwd(proj, n_pairs, comm=(shards[1:], ["gather"] * 3))
    w_out_f, w_gu_f, w_down_f = cat_rows(gathered[0]), cat_cols(gathered[1]), cat_rows(gathered[2])
    sw_out = _swa_fwd(proj, bucket, rel_bias, sinks, n_sw_heads, qcol, kcol, vcol)
    merged, u1 = _mix_ln1(sb_out, sw_out, x2, g_in, b_in, sb_norm_g, swa_norm_g, w_out_f)
    du2, st_ln2 = _ffn_fwd(u1, ln1_g, ln1_b, w_gu_f, w_down_f, ln2_g, ln2_b, tgt)

    split_cols = lambda g: jnp.transpose(g.astype(BF16).reshape(g.shape[0], N_DEV, g.shape[1] // N_DEV), (1, 0, 2))
    split_rows = lambda g: g.astype(BF16).reshape(N_DEV, g.shape[0] // N_DEV, g.shape[1])
    tile_m = min(512, d)
    h1b, act, dgu = _ffn_bwd_act(u1, du2, ln1_g, ln1_b, w_gu_f, jnp.transpose(w_down_f))
    du1, st_ln1 = _ffn_bwd_in(dgu, jnp.transpose(w_gu_f), u1, du2, ln1_g)
    gw_gu = _wgrad("wgrad_gate_up", h1b, dgu, tile_m, dff // 2)
    gw_down = _wgrad("wgrad_down", act, du2, dff // 2, d)
    gw_out = _wgrad("wgrad_out", merged, du1, tile_m, d)
    dsb, dsw, st_rms = _mix_bwd(du1, w_out_f, sb_out, sw_out, sb_norm_g, swa_norm_g)
    (dq_sb, dk_sb, dv_sb), (land_gu, land_out) = _sb_bwd(
        proj, dsb, sb_out, n_pairs, comm=([split_cols(gw_gu), split_rows(gw_out)], ["scatter"] * 2))
    (dq_sw, dk_sw, dv_sw, st_sink, st_rb), (land_down,) = _swa_bwd(
        proj, dsw, bucket, rel_bias, sinks, n_sw_heads, qcol, kcol, vcol,
        comm=([split_rows(gw_down)], ["scatter"]))
    dproj = jnp.concatenate([dq_sb, dk_sb.astype(BF16), dv_sb.astype(BF16), dq_sw,
                             dk_sw.astype(BF16), dv_sw.astype(BF16)], axis=1)
    gw_in = _wgrad("wgrad_in", h0b, dproj, tile_m, dproj.shape[1] // 2)
    (grad_x, st_in), (land_in,) = _proj_bwd(dproj, w_in_f, du1, x2, g_in, comm=([split_cols(gw_in)], ["scatter"]))

    n_rb = rel_bias.size
    small = _pack(d, st_in[0], st_in[1], st_ln1[0], st_ln1[1], st_ln2[0], st_ln2[1],
                  st_rms[0, :wsb], st_rms[0, wsb:], st_rb[:, :n_sw_heads], st_sink[:n_sw_heads, 0],
                  extra=st_ln2[2, 0:1])
    land_small = _exchange("small_grads_allgather", [small], ["gather"])[0]
    landed = [land_in, land_out, land_gu, land_down, land_small]

    big = []
    for name, land, w, m, v in zip(["adamw_in", "adamw_out", "adamw_gate_up", "adamw_down"], landed[:4], big_w, big_m, big_v):
        rows = w.shape[0]
        tr = 128 if rows % 128 == 0 else rows // 2
        big.append(_adamw(name, land, w, m, v, tr))

    small_w = _pack(d, ln_in_g, ln_in_b, ln1_g, ln1_b, ln2_g, ln2_b, sb_norm_g, swa_norm_g, rel_bias, sinks)
    small_m = _pack(d, m_ln_in_g, m_ln_in_b, m_ln1_g, m_ln1_b, m_ln2_g, m_ln2_b, m_sb_norm_g, m_swa_norm_g, m_rel_bias, m_sinks)
    small_v = _pack(d, v_ln_in_g, v_ln_in_b, v_ln1_g, v_ln1_b, v_ln2_g, v_ln2_b, v_sb_norm_g, v_swa_norm_g, v_rel_bias, v_sinks)
    sg, sd, sm, sv = _adamw("adamw_small", landed[4], small_w, small_m, small_v, 8)
    n_sk = sinks.size
    loss = sg[7, n_rb + n_sk]

    def leaves(idx):
        sm_l = _unpack([sg, sd, sm, sv][idx], wsb, n_rb, n_sk)
        bg = [b[idx][None] for b in big]
        return [sm_l[0], sm_l[1], bg[0], sm_l[2], sm_l[3], sm_l[4], sm_l[5], bg[1], sm_l[6], sm_l[7], bg[2], bg[3], sm_l[8], sm_l[9]]

    return (loss, grad_x[None], *leaves(0), *leaves(1), *leaves(2), *leaves(3))
```

```python
import functools
import math

import numpy as np
import jax
import jax.numpy as jnp
from jax import lax
from jax.experimental import pallas as pl
from jax.experimental.pallas import tpu as pltpu

F32 = jnp.float32
BF16 = jnp.bfloat16
MESH = pl.DeviceIdType.MESH

N_DEV = 8
LANES = 128
HEAD_DIM = 64
SCALE = HEAD_DIM ** -0.5
SWA_BLOCK = 128
REL_BUCKETS = 32
REL_MAX_DIST = 128
ALPHA = 2.0 ** 0.25
LN_EPS = 1e-5
RMS_EPS = 1e-6
ADAM_LR = 0.001
ADAM_B1 = 0.9
ADAM_B2 = 0.999
ADAM_EPS = 1e-08
ADAM_WD = 0.01
ADAM_STEP = 10

ROW_TILE = 512
SB_TILE = 256
FFN_TILE = 256
WGRAD_TOKENS = 2048
SB_UNDERFLOW = -110.0
MIB = 1024 * 1024


def _params(vmem_mib=48):
    return pltpu.CompilerParams(vmem_limit_bytes=vmem_mib * MIB)


def _dot(a, b):
    return jnp.dot(a, b, preferred_element_type=F32)


def _dot_nt(a, b):
    return lax.dot_general(a, b, (((1,), (1,)), ((), ())), preferred_element_type=F32)


def _dot_tn(a, b):
    return lax.dot_general(a, b, (((0,), (0,)), ((), ())), preferred_element_type=F32)


def _ln_hat(x):
    mu = jnp.mean(x, axis=-1, keepdims=True)
    xc = x - mu
    var = jnp.mean(xc * xc, axis=-1, keepdims=True)
    r = lax.rsqrt(var + LN_EPS)
    return xc * r, r


def _ln_bwd(dxhat, xhat, r):
    return r * (dxhat - jnp.mean(dxhat, axis=-1, keepdims=True)
                - xhat * jnp.mean(dxhat * xhat, axis=-1, keepdims=True))


def _colsum(a):
    return jnp.sum(a, axis=0, keepdims=True)


def _rowsum(a):
    return jnp.sum(a, axis=1, keepdims=True)


def _full(shape):
    return pl.BlockSpec(shape, lambda *_: (0,) * len(shape))


def _comm_out_shapes(arrays, kinds):
    shapes = []
    for a, kind in zip(arrays, kinds):
        blk = a.shape if kind == "gather" else a.shape[1:]
        shapes.append(jax.ShapeDtypeStruct((N_DEV,) + tuple(blk), a.dtype))
    return shapes


def _comm_sems(n):
    return [pltpu.SemaphoreType.DMA((n, N_DEV - 1)), pltpu.SemaphoreType.DMA((n, N_DEV - 1)),
            pltpu.SemaphoreType.DMA((n,))]


def _comm_copies(ins, outs, kinds, send_sems, recv_sems, local_sems):
    x, y, c = lax.axis_index("x"), lax.axis_index("y"), lax.axis_index("c")
    me = 4 * x + 2 * y + c

    def src_for(t, dev_lin):
        return ins[t] if kinds[t] == "gather" else ins[t].at[dev_lin]

    local = [pltpu.make_async_copy(src_for(t, me), outs[t].at[me], local_sems.at[t]) for t in range(len(kinds))]
    sends, arrivals = [], []
    for k in range(1, N_DEV):
        px = 1 - x if (k >> 2) & 1 else x
        py = 1 - y if (k >> 1) & 1 else y
        pc = 1 - c if k & 1 else c
        peer_lin = 4 * px + 2 * py + pc
        for t in range(len(kinds)):
            sems = dict(send_sem=send_sems.at[t, k - 1], recv_sem=recv_sems.at[t, k - 1],
                        device_id=(px, py, pc), device_id_type=MESH)
            sends.append(pltpu.make_async_remote_copy(src_ref=src_for(t, peer_lin), dst_ref=outs[t].at[me], **sems))
            arrivals.append(pltpu.make_async_remote_copy(src_ref=src_for(t, peer_lin), dst_ref=outs[t].at[peer_lin], **sems))
    return local, sends, arrivals


def _comm_start(ins, outs, kinds, sems):
    local, sends, _ = _comm_copies(ins, outs, kinds, *sems)
    for cp in local + sends:
        cp.start()


def _comm_finish(ins, outs, kinds, sems):
    local, sends, arrivals = _comm_copies(ins, outs, kinds, *sems)
    for cp in arrivals:
        cp.wait_recv()
    for cp in sends:
        cp.wait_send()
    for cp in local:
        cp.wait()


def _exchange(name, arrays, kinds):
    n = len(arrays)

    def body(*refs):
        ins, outs, sems = refs[:n], refs[n:2 * n], refs[2 * n:]
        _comm_start(ins, outs, kinds, sems)
        _comm_finish(ins, outs, kinds, sems)

    any_spec = pl.BlockSpec(memory_space=pl.ANY)
    return pl.pallas_call(
        body, name=name, out_shape=_comm_out_shapes(arrays, kinds),
        in_specs=[any_spec] * n, out_specs=[any_spec] * n,
        scratch_shapes=_comm_sems(n),
    )(*arrays)


def _call(body, name, grid, in_specs, out_specs, out_shape, args, scratch_shapes=(), comm=None):
    if comm is None:
        outs = pl.pallas_call(body, name=name, grid=grid, in_specs=in_specs, out_specs=out_specs,
                              out_shape=out_shape, scratch_shapes=list(scratch_shapes),
                              compiler_params=_params())(*args)
        return outs, []
    arrays, kinds = comm
    n, n_in, n_out, n_scr = len(arrays), len(in_specs), len(out_specs), len(scratch_shapes)

    def fused(*refs):
        c_in, x_in = refs[:n_in], refs[n_in:n_in + n]
        c_out = refs[n_in + n:n_in + n + n_out]
        x_out = refs[n_in + n + n_out:n_in + 2 * n + n_out]
        rest = refs[n_in + 2 * n + n_out:]
        c_scr, sems = rest[:n_scr], rest[n_scr:]
        ids = [pl.program_id(a) for a in range(len(grid))]
        is_first = functools.reduce(jnp.logical_and, [i == 0 for i in ids])
        is_last = functools.reduce(jnp.logical_and, [i == g - 1 for i, g in zip(ids, grid)])

        @pl.when(is_first)
        def _():
            _comm_start(x_in, x_out, kinds, sems)

        body(*c_in, *c_out, *c_scr)

        @pl.when(is_last)
        def _():
            _comm_finish(x_in, x_out, kinds, sems)

    any_spec = pl.BlockSpec(memory_space=pl.ANY)
    outs = pl.pallas_call(
        fused, name=name, grid=grid,
        in_specs=list(in_specs) + [any_spec] * n, out_specs=list(out_specs) + [any_spec] * n,
        out_shape=list(out_shape) + _comm_out_shapes(arrays, kinds),
        scratch_shapes=list(scratch_shapes) + _comm_sems(n),
        compiler_params=_params())(*args, *arrays)
    return outs[:n_out], outs[n_out:]


def _ln_proj(x, g, b, w_in):
    s, d = x.shape
    cols = w_in.shape[1]
    tm = min(ROW_TILE, s)

    def body(x_ref, g_ref, b_ref, w_ref, h_ref, p_ref):
        xhat, _ = _ln_hat(x_ref[...])
        h = (xhat * g_ref[...] + b_ref[...]).astype(BF16)
        h_ref[...] = h
        p_ref[...] = _dot(h, w_ref[...]).astype(BF16)

    row = lambda width: pl.BlockSpec((tm, width), lambda i: (i, 0))
    return pl.pallas_call(
        body, name="ln_proj", grid=(s // tm,),
        in_specs=[row(d), _full((1, d)), _full((1, d)), _full((d, cols))],
        out_specs=[row(d), row(cols)],
        out_shape=[jax.ShapeDtypeStruct((s, d), BF16), jax.ShapeDtypeStruct((s, cols), BF16)],
        compiler_params=_params(),
    )(x, g, b, w_in)


def _sb_tile_consts(t):
    row = lax.broadcasted_iota(jnp.int32, (t, t), 0)
    col = lax.broadcasted_iota(jnp.int32, (t, t), 1)
    qrow = lax.broadcasted_iota(jnp.int32, (2 * t, t), 0) & (t - 1)
    qcol = lax.broadcasted_iota(jnp.int32, (2 * t, t), 1)
    return (row > col).astype(BF16), (row >= col).astype(BF16), qcol < qrow


def _sb_stack_heads(x2, first):
    zero = jnp.zeros_like(x2)
    return jnp.concatenate([jnp.where(first, x2, zero), jnp.where(first, zero, x2)], axis=0)


def _sb_scores(qh, k_t, upper, carry_l, causal):
    z = _dot_nt(qh, k_t)
    sp = jnp.log(1.0 + jnp.exp(-jnp.abs(z)))
    neg = jnp.minimum(z, 0.0)
    lb = neg - sp
    l1 = (neg - z) - sp
    if causal is not None:
        l1 = jnp.where(causal, l1, 0.0)
    hi = l1.astype(BF16)
    lo = (l1 - hi.astype(F32)).astype(BF16)
    suf = _dot(hi, upper) + _dot(lo, upper) + carry_l
    a = jnp.exp(lb + suf)
    if causal is not None:
        a = jnp.where(causal, a, 0.0)
    return lb, l1, a


def _sb_walk(i, t, visit, init, causal):
    def alive(carry):
        return jnp.max(carry[0]) > SB_UNDERFLOW

    carry = visit(pl.multiple_of(i * t, t), init, causal)

    def cond(state):
        j, go, _ = state
        return (j < i) & go

    def body(state):
        j, _, carry = state
        carry = visit(pl.multiple_of((i - 1 - j) * t, t), carry, None)
        return j + 1, alive(carry), carry

    return lax.while_loop(cond, body, (jnp.int32(0), alive(carry), carry))[2]


def _sb_fwd(proj, n_pairs, comm=None):
    s = proj.shape[0]
    t = min(SB_TILE, s)
    nq = s // t

    def body(q_ref, k_ref, v_ref, o_ref):
        i = pl.program_id(1)
        lane = lax.broadcasted_iota(jnp.int32, (1, LANES), 1)
        first = lane < HEAD_DIM
        qs = _sb_stack_heads(q_ref[...] * SCALE, first)
        upper, _, causal = _sb_tile_consts(t)

        def visit(off, carry, mask):
            k_t = k_ref[pl.ds(off, t), :]
            v_t = v_ref[pl.ds(off, t), :]
            c_l, acc = carry
            _, l1, a = _sb_scores(qs, k_t, upper, c_l, mask)
            return c_l + _rowsum(l1), acc + _dot(a.astype(BF16), v_t)

        init = (jnp.zeros((2 * t, 1), F32), jnp.zeros((2 * t, LANES), F32))
        _, acc = _sb_walk(i, t, visit, init, causal)
        o_ref[...] = jnp.where(first, acc[:t], acc[t:])

    outs, landed = _call(
        body, "sb_fwd", (n_pairs, nq),
        in_specs=[pl.BlockSpec((t, LANES), lambda h, i: (i, h)),
                  pl.BlockSpec((s, LANES), lambda h, i: (0, n_pairs + h)),
                  pl.BlockSpec((s, LANES), lambda h, i: (0, 2 * n_pairs + h))],
        out_specs=[pl.BlockSpec((t, LANES), lambda h, i: (i, h))],
        out_shape=[jax.ShapeDtypeStruct((s, n_pairs * LANES), F32)],
        args=(proj, proj, proj), comm=comm)
    return outs[0], landed


def _swa_bucket_table():
    qi = np.arange(SWA_BLOCK)[:, None]
    cj = np.arange(2 * SWA_BLOCK)[None, :]
    dist = qi + SWA_BLOCK - cj
    exact = REL_BUCKETS // 2
    d = np.maximum(dist, 0)
    d_f = np.maximum(d, 1).astype(np.float32)
    large = exact + (np.log(d_f / np.float32(exact)) / np.float32(math.log(REL_MAX_DIST / exact))
                     * np.float32(REL_BUCKETS - exact)).astype(np.int32)
    large = np.minimum(large, REL_BUCKETS - 1)
    return np.where(d < exact, d, large).astype(np.int32)


def _swa_build_bias(bucket_ref, rb_ref, bias_ref, n_groups, per_group):
    bk = bucket_ref[...]
    for g in range(n_groups):
        for hh in range(per_group):
            acc = jnp.zeros(bk.shape, F32)
            for b in range(REL_BUCKETS):
                acc = jnp.where(bk == b, rb_ref[b, g * per_group + hh], acc)
            bias_ref[g, hh * SWA_BLOCK:(hh + 1) * SWA_BLOCK, :] = acc


def _swa_valid(i, reps):
    shape = (reps * SWA_BLOCK, 2 * SWA_BLOCK)
    row = lax.broadcasted_iota(jnp.int32, shape, 0) & (SWA_BLOCK - 1)
    col = lax.broadcasted_iota(jnp.int32, shape, 1)
    dist = row + SWA_BLOCK - col
    return (dist >= 0) & (dist < SWA_BLOCK) & ((col >= SWA_BLOCK) | (i > 0))


def _swa_place(blk, h, group, sel):
    if (h % 2) != group:
        blk = pltpu.roll(blk.astype(F32), HEAD_DIM, axis=1).astype(BF16)
    return jnp.where(sel, blk, jnp.zeros_like(blk))


def _swa_stack(ref, group, per_group, sel):
    parts = []
    for hh in range(per_group):
        h = group * per_group + hh
        parts.append(_swa_place(ref[:, (h // 2) * LANES:(h // 2 + 1) * LANES], h, group, sel))
    return jnp.concatenate(parts, axis=0)


def _swa_unstack(stacked, group, per_group, pieces):
    for hh in range(per_group):
        h = group * per_group + hh
        piece = stacked[hh * SWA_BLOCK:(hh + 1) * SWA_BLOCK, :]
        pieces[h] = pltpu.roll(piece, HEAD_DIM, axis=1) if (h % 2) != group else piece


def _swa_sink_rows(sk_ref, group, per_group):
    rowh = lax.broadcasted_iota(jnp.int32, (per_group * SWA_BLOCK, 1), 0) // SWA_BLOCK
    sink = jnp.zeros((per_group * SWA_BLOCK, 1), F32) + sk_ref[0, group * per_group]
    for hh in range(1, per_group):
        sink = jnp.where(rowh == hh, sk_ref[0, group * per_group + hh], sink)
    return sink


def _swa_probs(q_pos, kcat, bias_h, valid, sink):
    logits = _dot_nt(q_pos, kcat) * SCALE + bias_h
    logits = jnp.where(valid, logits, -jnp.inf)
    m = jnp.maximum(jnp.max(logits, axis=1, keepdims=True), sink)
    p = jnp.exp(logits - m)
    es = jnp.exp(sink - m)
    denom = _rowsum(p) + es
    return p / denom, es / denom


def _swa_specs(n_heads, qcol, kcol, vcol):
    width = n_heads * HEAD_DIM
    prev = lambda col: pl.BlockSpec((SWA_BLOCK, LANES), lambda i: (jnp.maximum(i - 1, 0), col))
    cur = lambda col: pl.BlockSpec((SWA_BLOCK, LANES), lambda i: (i, col))
    return [pl.BlockSpec((SWA_BLOCK, width), lambda i: (i, qcol)),
            prev(kcol), cur(kcol), prev(vcol), cur(vcol),
            _full((SWA_BLOCK, 2 * SWA_BLOCK)),
            pl.BlockSpec(memory_space=pltpu.SMEM), pl.BlockSpec(memory_space=pltpu.SMEM)]


def _swa_fwd(proj, bucket, rel_bias, sinks, n_heads, qcol, kcol, vcol):
    s = proj.shape[0]
    width = n_heads * HEAD_DIM
    n_groups = LANES // HEAD_DIM
    per_group = n_heads // n_groups

    def body(q_ref, kp_ref, kc_ref, vp_ref, vc_ref, bucket_ref, rb_ref, sk_ref, o_ref, bias_ref):
        i = pl.program_id(0)

        @pl.when(i == 0)
        def _():
            _swa_build_bias(bucket_ref, rb_ref, bias_ref, n_groups, per_group)

        lane = lax.broadcasted_iota(jnp.int32, (1, LANES), 1)
        first = lane < HEAD_DIM
        valid = _swa_valid(i, per_group)
        kcat = jnp.concatenate([kp_ref[...], kc_ref[...]], axis=0)
        vcat = jnp.concatenate([vp_ref[...], vc_ref[...]], axis=0)
        pieces = {}
        for g in range(n_groups):
            sel = first if g == 0 else jnp.logical_not(first)
            prob, _ = _swa_probs(_swa_stack(q_ref, g, per_group, sel), kcat, bias_ref[g], valid,
                                 _swa_sink_rows(sk_ref, g, per_group))
            _swa_unstack(_dot(prob.astype(BF16), vcat), g, per_group, pieces)
        for j in range(n_heads // 2):
            o_ref[:, j * LANES:(j + 1) * LANES] = jnp.where(first, pieces[2 * j], pieces[2 * j + 1])

    return pl.pallas_call(
        body, name="swa_fwd", grid=(s // SWA_BLOCK,),
        in_specs=_swa_specs(n_heads, qcol, kcol, vcol),
        out_specs=pl.BlockSpec((SWA_BLOCK, width), lambda i: (i, 0)),
        out_shape=jax.ShapeDtypeStruct((s, width), F32),
        scratch_shapes=[pltpu.VMEM((n_groups, per_group * SWA_BLOCK, 2 * SWA_BLOCK), F32)],
        compiler_params=_params(),
    )(proj, proj, proj, proj, proj, bucket, rel_bias, sinks)


def _rms_fwd(o, g):
    r = lax.rsqrt(jnp.mean(o * o, axis=-1, keepdims=True) + RMS_EPS)
    n = o * r
    return n, r, n * g


def _mix_ln1(sb_out, sw_out, x, g_in, b_in, sb_g, sw_g, w_out):
    s, d = x.shape
    wsb, wsw = sb_out.shape[1], sw_out.shape[1]
    tm = min(ROW_TILE, s)

    def body(sb_ref, sw_ref, x_ref, gi_ref, bi_ref, sbg_ref, swg_ref, w_ref, mg_ref, u_ref):
        _, _, m_sb = _rms_fwd(sb_ref[...], sbg_ref[...])
        _, _, m_sw = _rms_fwd(sw_ref[...], swg_ref[...])
        m_sb = m_sb.astype(BF16)
        m_sw = m_sw.astype(BF16)
        mg_ref[:, :wsb] = m_sb
        mg_ref[:, wsb:] = m_sw
        mix = _dot(m_sb, w_ref[:wsb, :]) + _dot(m_sw, w_ref[wsb:, :])
        xhat, _ = _ln_hat(x_ref[...])
        h0 = xhat * gi_ref[...] + bi_ref[...]
        u_ref[...] = ALPHA * h0 + mix

    row = lambda width: pl.BlockSpec((tm, width), lambda i: (i, 0))
    return pl.pallas_call(
        body, name="mix_ln1", grid=(s // tm,),
        in_specs=[row(wsb), row(wsw), row(d), _full((1, d)), _full((1, d)),
                  _full((1, wsb)), _full((1, wsw)), _full((wsb + wsw, d))],
        out_specs=[row(wsb + wsw), row(d)],
        out_shape=[jax.ShapeDtypeStruct((s, wsb + wsw), BF16), jax.ShapeDtypeStruct((s, d), F32)],
        compiler_params=_params(),
    )(sb_out, sw_out, x, g_in, b_in, sb_g, sw_g, w_out)


def _ffn_fwd(u1, g1, b1, w_gu, w_down, g2, b2, target):
    s, d = u1.shape
    dff = w_down.shape[0]
    tm = min(FFN_TILE, s)

    def body(u_ref, g1_ref, b1_ref, wgu_hbm, wd_hbm, g2_ref, b2_ref, t_ref, du_ref, st_ref, wgu_ref, wd_ref):
        @pl.when(pl.program_id(0) == 0)
        def _():
            pltpu.sync_copy(wgu_hbm, wgu_ref)
            pltpu.sync_copy(wd_hbm, wd_ref)
            st_ref[...] = jnp.zeros_like(st_ref)

        xhat, _ = _ln_hat(u_ref[...])
        h1 = xhat * g1_ref[...] + b1_ref[...]
        h1b = h1.astype(BF16)
        gate = _dot(h1b, wgu_ref[:, :dff])
        up = _dot(h1b, wgu_ref[:, dff:])
        act = gate * jax.nn.sigmoid(gate) * up
        u2 = ALPHA * h1 + _dot(act.astype(BF16), wd_ref[...])
        xhat2, r2 = _ln_hat(u2)
        diff = xhat2 * g2_ref[...] + b2_ref[...] - t_ref[...]
        dh2 = diff * (1.0 / d)
        st_ref[0:1, :] += _colsum(dh2 * xhat2)
        st_ref[1:2, :] += _colsum(dh2)
        st_ref[2:3, :] += jnp.broadcast_to(_colsum(_rowsum(diff * diff)) * (0.5 / d), (1, d))
        du_ref[...] = _ln_bwd(dh2 * g2_ref[...], xhat2, r2)

    row = pl.BlockSpec((tm, d), lambda i: (i, 0))
    hbm = pl.BlockSpec(memory_space=pl.ANY)
    return pl.pallas_call(
        body, name="ffn_fwd", grid=(s // tm,),
        in_specs=[row, _full((1, d)), _full((1, d)), hbm, hbm, _full((1, d)), _full((1, d)), row],
        out_specs=[row, _full((8, d))],
        out_shape=[jax.ShapeDtypeStruct((s, d), F32), jax.ShapeDtypeStruct((8, d), F32)],
        scratch_shapes=[pltpu.VMEM(w_gu.shape, BF16), pltpu.VMEM(w_down.shape, BF16)],
        compiler_params=_params(56),
    )(u1, g1, b1, w_gu, w_down, g2, b2, target)


def _ffn_bwd_act(u1, du2, g1, b1, w_gu, w_down_t):
    s, d = u1.shape
    dff = w_down_t.shape[1]
    tm = min(FFN_TILE, s)

    def body(u_ref, du2_ref, g1_ref, b1_ref, wgu_hbm, wdt_hbm, h1b_ref, act_ref, dgu_ref, wgu_ref, wdt_ref):
        @pl.when(pl.program_id(0) == 0)
        def _():
            pltpu.sync_copy(wgu_hbm, wgu_ref)
            pltpu.sync_copy(wdt_hbm, wdt_ref)

        xhat, _ = _ln_hat(u_ref[...])
        h1b = (xhat * g1_ref[...] + b1_ref[...]).astype(BF16)
        h1b_ref[...] = h1b
        gate = _dot(h1b, wgu_ref[:, :dff])
        up = _dot(h1b, wgu_ref[:, dff:])
        dact = _dot(du2_ref[...].astype(BF16), wdt_ref[...])
        sg = jax.nn.sigmoid(gate)
        silu = gate * sg
        act_ref[...] = (silu * up).astype(BF16)
        dgu_ref[:, :dff] = (dact * up * (sg * (1.0 + gate * (1.0 - sg)))).astype(BF16)
        dgu_ref[:, dff:] = (dact * silu).astype(BF16)

    row = lambda width: pl.BlockSpec((tm, width), lambda i: (i, 0))
    hbm = pl.BlockSpec(memory_space=pl.ANY)
    return pl.pallas_call(
        body, name="ffn_bwd_act", grid=(s // tm,),
        in_specs=[row(d), row(d), _full((1, d)), _full((1, d)), hbm, hbm],
        out_specs=[row(d), row(dff), row(2 * dff)],
        out_shape=[jax.ShapeDtypeStruct((s, d), BF16), jax.ShapeDtypeStruct((s, dff), BF16),
                   jax.ShapeDtypeStruct((s, 2 * dff), BF16)],
        scratch_shapes=[pltpu.VMEM(w_gu.shape, BF16), pltpu.VMEM(w_down_t.shape, BF16)],
        compiler_params=_params(56),
    )(u1, du2, g1, b1, w_gu, w_down_t)


def _ffn_bwd_in(dgu, w_gu_t, u1, du2, g1):
    s, d = u1.shape
    tm = min(FFN_TILE, s)

    def body(dgu_ref, wt_hbm, u_ref, du2_ref, g1_ref, du1_ref, st_ref, wt_ref):
        @pl.when(pl.program_id(0) == 0)
        def _():
            pltpu.sync_copy(wt_hbm, wt_ref)
            st_ref[...] = jnp.zeros_like(st_ref)

        dh1 = _dot(dgu_ref[...], wt_ref[...]) + ALPHA * du2_ref[...]
        xhat, r = _ln_hat(u_ref[...])
        st_ref[0:1, :] += _colsum(dh1 * xhat)
        st_ref[1:2, :] += _colsum(dh1)
        du1_ref[...] = _ln_bwd(dh1 * g1_ref[...], xhat, r)

    row = lambda width: pl.BlockSpec((tm, width), lambda i: (i, 0))
    return pl.pallas_call(
        body, name="ffn_bwd_in", grid=(s // tm,),
        in_specs=[row(dgu.shape[1]), pl.BlockSpec(memory_space=pl.ANY), row(d), row(d), _full((1, d))],
        out_specs=[row(d), _full((8, d))],
        out_shape=[jax.ShapeDtypeStruct((s, d), F32), jax.ShapeDtypeStruct((8, d), F32)],
        scratch_shapes=[pltpu.VMEM(w_gu_t.shape, BF16)],
        compiler_params=_params(56),
    )(dgu, w_gu_t, u1, du2, g1)


def _rms_bwd(dm, o, g):
    n, r, _ = _rms_fwd(o, g)
    dn = dm * g
    return r * (dn - n * jnp.mean(dn * n, axis=-1, keepdims=True)), _colsum(dm * n)


def _mix_bwd(du1, w_out, sb_out, sw_out, sb_g, sw_g):
    s, d = du1.shape
    wsb, wsw = sb_out.shape[1], sw_out.shape[1]
    tm = min(ROW_TILE, s)

    def body(du_ref, w_ref, sb_ref, sw_ref, sbg_ref, swg_ref, dsb_ref, dsw_ref, st_ref):
        i = pl.program_id(0)

        @pl.when(i == 0)
        def _():
            st_ref[...] = jnp.zeros_like(st_ref)

        dmerged = _dot_nt(du_ref[...].astype(BF16), w_ref[...])
        dsb, gsb = _rms_bwd(dmerged[:, :wsb], sb_ref[...], sbg_ref[...])
        dsw, gsw = _rms_bwd(dmerged[:, wsb:], sw_ref[...], swg_ref[...])
        dsb_ref[...] = dsb.astype(BF16)
        dsw_ref[...] = dsw.astype(BF16)
        st_ref[0:1, :wsb] += gsb
        st_ref[0:1, wsb:] += gsw

    row = lambda width: pl.BlockSpec((tm, width), lambda i: (i, 0))
    return pl.pallas_call(
        body, name="mix_bwd", grid=(s // tm,),
        in_specs=[row(d), _full((wsb + wsw, d)), row(wsb), row(wsw), _full((1, wsb)), _full((1, wsw))],
        out_specs=[row(wsb), row(wsw), _full((8, wsb + wsw))],
        out_shape=[jax.ShapeDtypeStruct((s, wsb), BF16), jax.ShapeDtypeStruct((s, wsw), BF16),
                   jax.ShapeDtypeStruct((8, wsb + wsw), F32)],
        compiler_params=_params(),
    )(du1, w_out, sb_out, sw_out, sb_g, sw_g)


def _sb_bwd(proj, dout, out, n_pairs, comm=None):
    s = proj.shape[0]
    t = min(SB_TILE, s)
    nq = s // t
    width = n_pairs * LANES

    def body(q_ref, k_ref, v_ref, do_ref, o_ref, dq_ref, dk_ref, dv_ref):
        i = pl.program_id(1)

        @pl.when(i == 0)
        def _():
            dk_ref[...] = jnp.zeros_like(dk_ref)
            dv_ref[...] = jnp.zeros_like(dv_ref)

        lane = lax.broadcasted_iota(jnp.int32, (1, LANES), 1)
        first = lane < HEAD_DIM
        do2 = do_ref[...]
        qs = _sb_stack_heads(q_ref[...] * SCALE, first)
        dos = _sb_stack_heads(do2, first)
        prod = do2.astype(F32) * o_ref[...]
        totals = jnp.concatenate([_rowsum(jnp.where(first, prod, 0.0)), _rowsum(jnp.where(first, 0.0, prod))], axis=0)
        upper, incl, causal = _sb_tile_consts(t)

        def visit(off, carry, mask):
            k_t = k_ref[pl.ds(off, t), :]
            v_t = v_ref[pl.ds(off, t), :]
            c_l, c_e, dq = carry
            lb, l1, a = _sb_scores(qs, k_t, upper, c_l, mask)
            a_b = a.astype(BF16)
            d_e = _dot_nt(dos, v_t) * a_b.astype(F32)
            d_l = totals - (_dot(d_e.astype(BF16), incl) + c_e)
            dz = d_e - jnp.exp(lb) * (d_e + d_l)
            if mask is not None:
                dz = jnp.where(mask, dz, 0.0)
            dzb = dz.astype(BF16)
            dk_ref[pl.ds(off, t), :] += _dot_tn(dzb, qs)
            dv_ref[pl.ds(off, t), :] += _dot_tn(a_b, dos)
            return c_l + _rowsum(l1), c_e + _rowsum(d_e), dq + _dot(dzb, k_t)

        init = (jnp.zeros((2 * t, 1), F32), jnp.zeros((2 * t, 1), F32), jnp.zeros((2 * t, LANES), F32))
        _, _, dq = _sb_walk(i, t, visit, init, causal)
        dq_ref[...] = (jnp.where(first, dq[:t], dq[t:]) * SCALE).astype(BF16)

    qblk = pl.BlockSpec((t, LANES), lambda h, i: (i, h))
    whole = pl.BlockSpec((s, LANES), lambda h, i: (0, h))
    return _call(
        body, "sb_bwd", (n_pairs, nq),
        in_specs=[qblk,
                  pl.BlockSpec((s, LANES), lambda h, i: (0, n_pairs + h)),
                  pl.BlockSpec((s, LANES), lambda h, i: (0, 2 * n_pairs + h)),
                  qblk, qblk],
        out_specs=[qblk, whole, whole],
        out_shape=[jax.ShapeDtypeStruct((s, width), BF16), jax.ShapeDtypeStruct((s, width), F32),
                   jax.ShapeDtypeStruct((s, width), F32)],
        args=(proj, proj, proj, dout, out), comm=comm)


def _swa_bwd(proj, dout, bucket, rel_bias, sinks, n_heads, qcol, kcol, vcol, comm=None):
    s = proj.shape[0]
    width = n_heads * HEAD_DIM
    n_groups = LANES // HEAD_DIM
    per_group = n_heads // n_groups
    nb = s // SWA_BLOCK

    def body(q_ref, kp_ref, kc_ref, vp_ref, vc_ref, bucket_ref, rb_ref, sk_ref, do_ref,
             dq_ref, dk_ref, dv_ref, dsk_ref, drb_ref, bias_ref, dbias_ref):
        i = pl.program_id(0)

        @pl.when(i == 0)
        def _():
            _swa_build_bias(bucket_ref, rb_ref, bias_ref, n_groups, per_group)
            dbias_ref[...] = jnp.zeros_like(dbias_ref)
            dk_ref[...] = jnp.zeros_like(dk_ref)
            dv_ref[...] = jnp.zeros_like(dv_ref)
            dsk_ref[...] = jnp.zeros_like(dsk_ref)

        lane = lax.broadcasted_iota(jnp.int32, (1, LANES), 1)
        first = lane < HEAD_DIM
        valid = _swa_valid(i, per_group)
        kcat = jnp.concatenate([kp_ref[...], kc_ref[...]], axis=0)
        vcat = jnp.concatenate([vp_ref[...], vc_ref[...]], axis=0)
        dkcat = jnp.zeros((2 * SWA_BLOCK, LANES), F32)
        dvcat = jnp.zeros((2 * SWA_BLOCK, LANES), F32)
        pieces = {}
        for g in range(n_groups):
            sel = first if g == 0 else jnp.logical_not(first)
            q_g = _swa_stack(q_ref, g, per_group, sel)
            do_g = _swa_stack(do_ref, g, per_group, sel)
            prob, p_sink = _swa_probs(q_g, kcat, bias_ref[g], valid, _swa_sink_rows(sk_ref, g, per_group))
            dprob = _dot_nt(do_g, vcat)
            delta = _rowsum(prob * dprob)
            dlog = prob * (dprob - delta)
            sink_term = p_sink * delta
            for hh in range(per_group):
                h = g * per_group + hh
                tot = _colsum(sink_term[hh * SWA_BLOCK:(hh + 1) * SWA_BLOCK, :])
                dsk_ref[h:h + 1, :] += jnp.broadcast_to(-tot, (1, LANES))
            dbias_ref[g] += dlog
            dlb = (dlog * SCALE).astype(BF16)
            _swa_unstack(_dot(dlb, kcat), g, per_group, pieces)
            dkcat += _dot_tn(dlb, q_g)
            dvcat += _dot_tn(prob.astype(BF16), do_g)
        for j in range(n_heads // 2):
            dq_ref[:, j * LANES:(j + 1) * LANES] = jnp.where(first, pieces[2 * j], pieces[2 * j + 1]).astype(BF16)

        cur = pl.multiple_of(i * SWA_BLOCK, SWA_BLOCK)
        dk_ref[pl.ds(cur, SWA_BLOCK), :] += dkcat[SWA_BLOCK:, :]
        dv_ref[pl.ds(cur, SWA_BLOCK), :] += dvcat[SWA_BLOCK:, :]

        @pl.when(i > 0)
        def _():
            prv = pl.multiple_of((i - 1) * SWA_BLOCK, SWA_BLOCK)
            dk_ref[pl.ds(prv, SWA_BLOCK), :] += dkcat[:SWA_BLOCK, :]
            dv_ref[pl.ds(prv, SWA_BLOCK), :] += dvcat[:SWA_BLOCK, :]

        @pl.when(i == nb - 1)
        def _():
            bk = bucket_ref[...]
            rowi = lax.broadcasted_iota(jnp.int32, (REL_BUCKETS, LANES), 0)
            coli = lax.broadcasted_iota(jnp.int32, (REL_BUCKETS, LANES), 1)
            res = jnp.zeros((REL_BUCKETS, LANES), F32)
            for h in range(n_heads):
                g, hh = divmod(h, per_group)
                db = dbias_ref[g, hh * SWA_BLOCK:(hh + 1) * SWA_BLOCK, :]
                for b in range(REL_BUCKETS):
                    tot = _colsum(_rowsum(jnp.where(bk == b, db, 0.0)))
                    res = jnp.where((rowi == b) & (coli == h), tot, res)
            drb_ref[...] = res

    in_specs = _swa_specs(n_heads, qcol, kcol, vcol) + [pl.BlockSpec((SWA_BLOCK, width), lambda i: (i, 0))]
    return _call(
        body, "swa_bwd", (nb,),
        in_specs=in_specs,
        out_specs=[pl.BlockSpec((SWA_BLOCK, width), lambda i: (i, 0)),
                   _full((s, LANES)), _full((s, LANES)), _full((8, LANES)), _full((REL_BUCKETS, LANES))],
        out_shape=[jax.ShapeDtypeStruct((s, width), BF16), jax.ShapeDtypeStruct((s, LANES), F32),
                   jax.ShapeDtypeStruct((s, LANES), F32), jax.ShapeDtypeStruct((8, LANES), F32),
                   jax.ShapeDtypeStruct((REL_BUCKETS, LANES), F32)],
        args=(proj, proj, proj, proj, proj, bucket, rel_bias, sinks, dout),
        scratch_shapes=[pltpu.VMEM((n_groups, per_group * SWA_BLOCK, 2 * SWA_BLOCK), F32),
                        pltpu.VMEM((n_groups, per_group * SWA_BLOCK, 2 * SWA_BLOCK), F32)],
        comm=comm)


def _proj_bwd(dproj, w_in, du1, x, g_in, comm=None):
    s, d = x.shape
    cols = w_in.shape[1]
    tm = min(ROW_TILE, s)

    def body(dp_ref, w_ref, du_ref, x_ref, g_ref, dx_ref, st_ref):
        i = pl.program_id(0)

        @pl.when(i == 0)
        def _():
            st_ref[...] = jnp.zeros_like(st_ref)

        dh0 = _dot_nt(dp_ref[...], w_ref[...]) + ALPHA * du_ref[...]
        xhat, r = _ln_hat(x_ref[...])
        st_ref[0:1, :] += _colsum(dh0 * xhat)
        st_ref[1:2, :] += _colsum(dh0)
        dx_ref[...] = _ln_bwd(dh0 * g_ref[...], xhat, r)

    row = lambda width: pl.BlockSpec((tm, width), lambda i: (i, 0))
    return _call(
        body, "proj_bwd", (s // tm,),
        in_specs=[row(cols), _full((d, cols)), row(d), row(d), _full((1, d))],
        out_specs=[row(d), _full((8, d))],
        out_shape=[jax.ShapeDtypeStruct((s, d), F32), jax.ShapeDtypeStruct((8, d), F32)],
        args=(dproj, w_in, du1, x, g_in), comm=comm)


def _wgrad(name, a, b, tm, tn):
    s, m = a.shape
    n = b.shape[1]
    ts = min(WGRAD_TOKENS if b.dtype == BF16 else WGRAD_TOKENS // 2, s)

    n_k = s // ts

    def body(a_ref, b_ref, o_ref, acc_ref):
        k = pl.program_id(2)

        @pl.when(k == 0)
        def _():
            acc_ref[...] = jnp.zeros_like(acc_ref)

        acc_ref[...] += _dot_tn(a_ref[...].astype(BF16), b_ref[...].astype(BF16))

        @pl.when(k == n_k - 1)
        def _():
            o_ref[...] = acc_ref[...].astype(BF16)

    return pl.pallas_call(
        body, name=name, grid=(m // tm, n // tn, n_k),
        in_specs=[pl.BlockSpec((ts, tm), lambda i, j, k: (k, i)),
                  pl.BlockSpec((ts, tn), lambda i, j, k: (k, j))],
        out_specs=pl.BlockSpec((tm, tn), lambda i, j, k: (i, j)),
        out_shape=jax.ShapeDtypeStruct((m, n), BF16),
        scratch_shapes=[pltpu.VMEM((tm, tn), F32)],
        compiler_params=_params(),
    )(a, b)


def _adamw_math(w, g, m, v):
    m = ADAM_B1 * m + (1.0 - ADAM_B1) * g
    v = ADAM_B2 * v + (1.0 - ADAM_B2) * (g * g)
    m_hat = m / (1.0 - ADAM_B1 ** ADAM_STEP)
    v_hat = v / (1.0 - ADAM_B2 ** ADAM_STEP)
    delta = -ADAM_LR * (m_hat / (jnp.sqrt(v_hat) + ADAM_EPS) + ADAM_WD * w)
    return delta, m, v


def _adamw(name, landed, w, m, v, tr):
    rows, cols = w.shape

    def body(l_ref, w_ref, m_ref, v_ref, g_ref, d_ref, nm_ref, nv_ref):
        g = l_ref[0].astype(F32)
        for src in range(1, N_DEV):
            g = g + l_ref[src].astype(F32)
        delta, nm, nv = _adamw_math(w_ref[...], g, m_ref[...], v_ref[...])
        g_ref[...] = g
        d_ref[...] = delta
        nm_ref[...] = nm
        nv_ref[...] = nv

    blk = pl.BlockSpec((tr, cols), lambda i: (i, 0))
    shape = jax.ShapeDtypeStruct((rows, cols), F32)
    return pl.pallas_call(
        body, name=name, grid=(rows // tr,),
        in_specs=[pl.BlockSpec((N_DEV, tr, cols), lambda i: (0, i, 0)), blk, blk, blk],
        out_specs=[blk, blk, blk, blk],
        out_shape=[shape, shape, shape, shape],
        compiler_params=_params(),
    )(landed, w, m, v)


def _pack(d, ln_in_g, ln_in_b, ln1_g, ln1_b, ln2_g, ln2_b, sb_g, sw_g, rel_bias, sinks, extra=None):
    tail = [rel_bias.reshape(-1), sinks.reshape(-1)]
    if extra is not None:
        tail.append(extra.reshape(-1))
    tail = jnp.concatenate(tail)
    tail = jnp.concatenate([tail, jnp.zeros((d - tail.shape[0],), F32)])
    rows = [ln_in_g.reshape(-1), ln_in_b.reshape(-1), ln1_g.reshape(-1), ln1_b.reshape(-1),
            ln2_g.reshape(-1), ln2_b.reshape(-1),
            jnp.concatenate([sb_g.reshape(-1), sw_g.reshape(-1)]), tail]
    return jnp.stack(rows)


def _unpack(p, wsb, n_rb, n_sk):
    return [p[0], p[1], p[6, :wsb][None], p[6, wsb:][None], p[7, n_rb:n_rb + n_sk][None],
            p[7, :n_rb].reshape(REL_BUCKETS, -1), p[2][None], p[3][None], p[4][None], p[5][None]]


def kernel(x, ln_in_g, ln_in_b, w_in, sb_norm_g, swa_norm_g, sinks, rel_bias, w_out, ln1_g, ln1_b, w_gate_up, w_down, ln2_g, ln2_b, loss_target, m_ln_in_g, m_ln_in_b, m_w_in, m_sb_norm_g, m_swa_norm_g, m_sinks, m_rel_bias, m_w_out, m_ln1_g, m_ln1_b, m_w_gate_up, m_w_down, m_ln2_g, m_ln2_b, v_ln_in_g, v_ln_in_b, v_w_in, v_sb_norm_g, v_swa_norm_g, v_sinks, v_rel_bias, v_w_out, v_ln1_g, v_ln1_b, v_w_gate_up, v_w_down, v_ln2_g, v_ln2_b):
    x2 = x[0]
    tgt = loss_target[0]
    s, d = x2.shape
    wsb = sb_norm_g.shape[-1]
    wsw = swa_norm_g.shape[-1]
    n_sw_heads = sinks.shape[-1]
    n_pairs = wsb // LANES
    dff = w_down.shape[1] * N_DEV
    assert wsb % LANES == 0 and wsw % LANES == 0 and n_sw_heads * HEAD_DIM == wsw
    assert 3 * wsb % wsw == 0 and dff % LANES == 0 and s % SWA_BLOCK == 0
    qcol = 3 * wsb // wsw
    kcol = (3 * wsb + wsw) // LANES
    vcol = kcol + 1
    assert w_in.shape[-1] * N_DEV == (vcol + 1) * LANES

    big_w = [w_in[0], w_out[0], w_gate_up[0], w_down[0]]
    big_m = [m_w_in[0], m_w_out[0], m_w_gate_up[0], m_w_down[0]]
    big_v = [v_w_in[0], v_w_out[0], v_w_gate_up[0], v_w_down[0]]

    cat_cols = lambda g: jnp.transpose(g, (1, 0, 2)).reshape(g.shape[1], N_DEV * g.shape[2])
    cat_rows = lambda g: g.reshape(N_DEV * g.shape[1], g.shape[2])
    shards = [w.astype(BF16) for w in big_w]
    w_in_f = cat_cols(_exchange("w_in_allgather", shards[:1], ["gather"])[0])

    vec = lambda a: a.reshape(1, -1)
    g_in, b_in = vec(ln_in_g), vec(ln_in_b)
    bucket = jnp.asarray(_swa_bucket_table())

    h0b, proj = _ln_proj(x2, g_in, b_in, w_in_f)
    sb_out, gathered = _sb_fwd(proj, n_pairs, comm=(shards[1:], ["gather"] * 3))
    w_out_f, w_gu_f, w_down_f = cat_rows(gathered[0]), cat_cols(gathered[1]), cat_rows(gathered[2])
    sw_out = _swa_fwd(proj, bucket, rel_bias, sinks, n_sw_heads, qcol, kcol, vcol)
    merged, u1 = _mix_ln1(sb_out, sw_out, x2, g_in, b_in, sb_norm_g, swa_norm_g, w_out_f)
    du2, st_ln2 = _ffn_fwd(u1, ln1_g, ln1_b, w_gu_f, w_down_f, ln2_g, ln2_b, tgt)

    split_cols = lambda g: jnp.transpose(g.astype(BF16).reshape(g.shape[0], N_DEV, g.shape[1] // N_DEV), (1, 0, 2))
    split_rows = lambda g: g.astype(BF16).reshape(N_DEV, g.shape[0] // N_DEV, g.shape[1])
    tile_m = min(512, d)
    h1b, act, dgu = _ffn_bwd_act(u1, du2, ln1_g, ln1_b, w_gu_f, jnp.transpose(w_down_f))
    du1, st_ln1 = _ffn_bwd_in(dgu, jnp.transpose(w_gu_f), u1, du2, ln1_g)
    gw_gu = _wgrad("wgrad_gate_up", h1b, dgu, tile_m, dff // 2)
    gw_down = _wgrad("wgrad_down", act, du2, dff // 2, d)
    gw_out = _wgrad("wgrad_out", merged, du1, tile_m, d)
    dsb, dsw, st_rms = _mix_bwd(du1, w_out_f, sb_out, sw_out, sb_norm_g, swa_norm_g)
    (dq_sb, dk_sb, dv_sb), (land_gu, land_out) = _sb_bwd(
        proj, dsb, sb_out, n_pairs, comm=([split_cols(gw_gu), split_rows(gw_out)], ["scatter"] * 2))
    (dq_sw, dk_sw, dv_sw, st_sink, st_rb), (land_down,) = _swa_bwd(
        proj, dsw, bucket, rel_bias, sinks, n_sw_heads, qcol, kcol, vcol,
        comm=([split_rows(gw_down)], ["scatter"]))
    dproj = jnp.concatenate([dq_sb, dk_sb.astype(BF16), dv_sb.astype(BF16), dq_sw,
                             dk_sw.astype(BF16), dv_sw.astype(BF16)], axis=1)
    gw_in = _wgrad("wgrad_in", h0b, dproj, tile_m, dproj.shape[1] // 2)
    (grad_x, st_in), (land_in,) = _proj_bwd(dproj, w_in_f, du1, x2, g_in, comm=([split_cols(gw_in)], ["scatter"]))

    n_rb = rel_bias.size
    small = _pack(d, st_in[0], st_in[1], st_ln1[0], st_ln1[1], st_ln2[0], st_ln2[1],
                  st_rms[0, :wsb], st_rms[0, wsb:], st_rb[:, :n_sw_heads], st_sink[:n_sw_heads, 0],
                  extra=st_ln2[2, 0:1])
    land_small = _exchange("small_grads_allgather", [small], ["gather"])[0]
    landed = [land_in, land_out, land_gu, land_down, land_small]

    big = []
    for name, land, w, m, v in zip(["adamw_in", "adamw_out", "adamw_gate_up", "adamw_down"], landed[:4], big_w, big_m, big_v):
        rows = w.shape[0]
        tr = 128 if rows % 128 == 0 else rows // 2
        big.append(_adamw(name, land, w, m, v, tr))

    small_w = _pack(d, ln_in_g, ln_in_b, ln1_g, ln1_b, ln2_g, ln2_b, sb_norm_g, swa_norm_g, rel_bias, sinks)
    small_m = _pack(d, m_ln_in_g, m_ln_in_b, m_ln1_g, m_ln1_b, m_ln2_g, m_ln2_b, m_sb_norm_g, m_swa_norm_g, m_rel_bias, m_sinks)
    small_v = _pack(d, v_ln_in_g, v_ln_in_b, v_ln1_g, v_ln1_b, v_ln2_g, v_ln2_b, v_sb_norm_g, v_swa_norm_g, v_rel_bias, v_sinks)
    sg, sd, sm, sv = _adamw("adamw_small", landed[4], small_w, small_m, small_v, 8)
    n_sk = sinks.size
    loss = sg[7, n_rb + n_sk]

    def leaves(idx):
        sm_l = _unpack([sg, sd, sm, sv][idx], wsb, n_rb, n_sk)
        bg = [b[idx][None] for b in big]
        return [sm_l[0], sm_l[1], bg[0], sm_l[2], sm_l[3], sm_l[4], sm_l[5], bg[1], sm_l[6], sm_l[7], bg[2], bg[3], sm_l[8], sm_l[9]]

    return (loss, grad_x[None], *leaves(0), *leaves(1), *leaves(2), *leaves(3))
```

```python
import functools
import math

import numpy as np
import jax
import jax.numpy as jnp
from jax import lax
from jax.experimental import pallas as pl
from jax.experimental.pallas import tpu as pltpu

F32 = jnp.float32
BF16 = jnp.bfloat16
MESH = pl.DeviceIdType.MESH

N_DEV = 8
LANES = 128
HEAD_DIM = 64
SCALE = HEAD_DIM ** -0.5
SWA_BLOCK = 128
REL_BUCKETS = 32
REL_MAX_DIST = 128
ALPHA = 2.0 ** 0.25
LN_EPS = 1e-5
RMS_EPS = 1e-6
ADAM_LR = 0.001
ADAM_B1 = 0.9
ADAM_B2 = 0.999
ADAM_EPS = 1e-08
ADAM_WD = 0.01
ADAM_STEP = 10

ROW_TILE = 512
SB_TILE = 256
FFN_TILE = 256
WGRAD_TOKENS = 2048
SB_UNDERFLOW = -110.0
MIB = 1024 * 1024


def _params(vmem_mib=48):
    return pltpu.CompilerParams(vmem_limit_bytes=vmem_mib * MIB)


def _dot(a, b):
    return jnp.dot(a, b, preferred_element_type=F32)


def _dot_nt(a, b):
    return lax.dot_general(a, b, (((1,), (1,)), ((), ())), preferred_element_type=F32)


def _dot_tn(a, b):
    return lax.dot_general(a, b, (((0,), (0,)), ((), ())), preferred_element_type=F32)


def _ln_hat(x):
    mu = jnp.mean(x, axis=-1, keepdims=True)
    xc = x - mu
    var = jnp.mean(xc * xc, axis=-1, keepdims=True)
    r = lax.rsqrt(var + LN_EPS)
    return xc * r, r


def _ln_bwd(dxhat, xhat, r):
    return r * (dxhat - jnp.mean(dxhat, axis=-1, keepdims=True)
                - xhat * jnp.mean(dxhat * xhat, axis=-1, keepdims=True))


def _colsum(a):
    return jnp.sum(a, axis=0, keepdims=True)


def _rowsum(a):
    return jnp.sum(a, axis=1, keepdims=True)


def _full(shape):
    return pl.BlockSpec(shape, lambda *_: (0,) * len(shape))


def _comm_out_shapes(arrays, kinds):
    shapes = []
    for a, kind in zip(arrays, kinds):
        blk = a.shape if kind == "gather" else a.shape[1:]
        shapes.append(jax.ShapeDtypeStruct((N_DEV,) + tuple(blk), a.dtype))
    return shapes


def _comm_sems(n):
    return [pltpu.SemaphoreType.DMA((n, N_DEV - 1)), pltpu.SemaphoreType.DMA((n, N_DEV - 1)),
            pltpu.SemaphoreType.DMA((n,))]


def _comm_copies(ins, outs, kinds, send_sems, recv_sems, local_sems):
    x, y, c = lax.axis_index("x"), lax.axis_index("y"), lax.axis_index("c")
    me = 4 * x + 2 * y + c

    def src_for(t, dev_lin):
        return ins[t] if kinds[t] == "gather" else ins[t].at[dev_lin]

    local = [pltpu.make_async_copy(src_for(t, me), outs[t].at[me], local_sems.at[t]) for t in range(len(kinds))]
    sends, arrivals = [], []
    for k in range(1, N_DEV):
        px = 1 - x if (k >> 2) & 1 else x
        py = 1 - y if (k >> 1) & 1 else y
        pc = 1 - c if k & 1 else c
        peer_lin = 4 * px + 2 * py + pc
        for t in range(len(kinds)):
            sems = dict(send_sem=send_sems.at[t, k - 1], recv_sem=recv_sems.at[t, k - 1],
                        device_id=(px, py, pc), device_id_type=MESH)
            sends.append(pltpu.make_async_remote_copy(src_ref=src_for(t, peer_lin), dst_ref=outs[t].at[me], **sems))
            arrivals.append(pltpu.make_async_remote_copy(src_ref=src_for(t, peer_lin), dst_ref=outs[t].at[peer_lin], **sems))
    return local, sends, arrivals


def _comm_start(ins, outs, kinds, sems):
    local, sends, _ = _comm_copies(ins, outs, kinds, *sems)
    for cp in local + sends:
        cp.start()


def _comm_finish(ins, outs, kinds, sems):
    local, sends, arrivals = _comm_copies(ins, outs, kinds, *sems)
    for cp in arrivals:
        cp.wait_recv()
    for cp in sends:
        cp.wait_send()
    for cp in local:
        cp.wait()


def _exchange(name, arrays, kinds):
    n = len(arrays)

    def body(*refs):
        ins, outs, sems = refs[:n], refs[n:2 * n], refs[2 * n:]
        _comm_start(ins, outs, kinds, sems)
        _comm_finish(ins, outs, kinds, sems)

    any_spec = pl.BlockSpec(memory_space=pl.ANY)
    return pl.pallas_call(
        body, name=name, out_shape=_comm_out_shapes(arrays, kinds),
        in_specs=[any_spec] * n, out_specs=[any_spec] * n,
        scratch_shapes=_comm_sems(n),
    )(*arrays)


def _call(body, name, grid, in_specs, out_specs, out_shape, args, scratch_shapes=(), comm=None):
    if comm is None:
        outs = pl.pallas_call(body, name=name, grid=grid, in_specs=in_specs, out_specs=out_specs,
                              out_shape=out_shape, scratch_shapes=list(scratch_shapes),
                              compiler_params=_params())(*args)
        return outs, []
    arrays, kinds = comm
    n, n_in, n_out, n_scr = len(arrays), len(in_specs), len(out_specs), len(scratch_shapes)

    def fused(*refs):
        c_in, x_in = refs[:n_in], refs[n_in:n_in + n]
        c_out = refs[n_in + n:n_in + n + n_out]
        x_out = refs[n_in + n + n_out:n_in + 2 * n + n_out]
        rest = refs[n_in + 2 * n + n_out:]
        c_scr, sems = rest[:n_scr], rest[n_scr:]
        ids = [pl.program_id(a) for a in range(len(grid))]
        is_first = functools.reduce(jnp.logical_and, [i == 0 for i in ids])
        is_last = functools.reduce(jnp.logical_and, [i == g - 1 for i, g in zip(ids, grid)])

        @pl.when(is_first)
        def _():
            _comm_start(x_in, x_out, kinds, sems)

        body(*c_in, *c_out, *c_scr)

        @pl.when(is_last)
        def _():
            _comm_finish(x_in, x_out, kinds, sems)

    any_spec = pl.BlockSpec(memory_space=pl.ANY)
    outs = pl.pallas_call(
        fused, name=name, grid=grid,
        in_specs=list(in_specs) + [any_spec] * n, out_specs=list(out_specs) + [any_spec] * n,
        out_shape=list(out_shape) + _comm_out_shapes(arrays, kinds),
        scratch_shapes=list(scratch_shapes) + _comm_sems(n),
        compiler_params=_params())(*args, *arrays)
    return outs[:n_out], outs[n_out:]


def _ln_proj(x, g, b, w_in):
    s, d = x.shape
    cols = w_in.shape[1]
    tm = min(ROW_TILE, s)

    def body(x_ref, g_ref, b_ref, w_ref, h_ref, p_ref):
        xhat, _ = _ln_hat(x_ref[...])
        h = (xhat * g_ref[...] + b_ref[...]).astype(BF16)
        h_ref[...] = h
        p_ref[...] = _dot(h, w_ref[...]).astype(BF16)

    row = lambda width: pl.BlockSpec((tm, width), lambda i: (i, 0))
    return pl.pallas_call(
        body, name="ln_proj", grid=(s // tm,),
        in_specs=[row(d), _full((1, d)), _full((1, d)), _full((d, cols))],
        out_specs=[row(d), row(cols)],
        out_shape=[jax.ShapeDtypeStruct((s, d), BF16), jax.ShapeDtypeStruct((s, cols), BF16)],
        compiler_params=_params(),
    )(x, g, b, w_in)


def _sb_tile_consts(t):
    row = lax.broadcasted_iota(jnp.int32, (t, t), 0)
    col = lax.broadcasted_iota(jnp.int32, (t, t), 1)
    qrow = lax.broadcasted_iota(jnp.int32, (2 * t, t), 0) & (t - 1)
    qcol = lax.broadcasted_iota(jnp.int32, (2 * t, t), 1)
    return (row > col).astype(BF16), (row >= col).astype(BF16), qcol < qrow


def _sb_stack_heads(x2, first):
    zero = jnp.zeros_like(x2)
    return jnp.concatenate([jnp.where(first, x2, zero), jnp.where(first, zero, x2)], axis=0)


def _sb_scores(qh, k_t, upper, carry_l, causal):
    z = _dot_nt(qh, k_t)
    sp = jnp.log(1.0 + jnp.exp(-jnp.abs(z)))
    neg = jnp.minimum(z, 0.0)
    lb = neg - sp
    l1 = (neg - z) - sp
    if causal is not None:
        l1 = jnp.where(causal, l1, 0.0)
    hi = l1.astype(BF16)
    lo = (l1 - hi.astype(F32)).astype(BF16)
    suf = _dot(hi, upper) + _dot(lo, upper) + carry_l
    a = jnp.exp(lb + suf)
    if causal is not None:
        a = jnp.where(causal, a, 0.0)
    return lb, l1, a


def _sb_walk(i, t, visit, init, causal):
    def alive(carry):
        return jnp.max(carry[0]) > SB_UNDERFLOW

    carry = visit(pl.multiple_of(i * t, t), init, causal)

    def cond(state):
        j, go, _ = state
        return (j < i) & go

    def body(state):
        j, _, carry = state
        carry = visit(pl.multiple_of((i - 1 - j) * t, t), carry, None)
        return j + 1, alive(carry), carry

    return lax.while_loop(cond, body, (jnp.int32(0), alive(carry), carry))[2]


def _sb_fwd(proj, n_pairs, comm=None):
    s = proj.shape[0]
    t = min(SB_TILE, s)
    nq = s // t

    def body(q_ref, k_ref, v_ref, o_ref):
        i = pl.program_id(1)
        lane = lax.broadcasted_iota(jnp.int32, (1, LANES), 1)
        first = lane < HEAD_DIM
        qs = _sb_stack_heads(q_ref[...] * SCALE, first)
        upper, _, causal = _sb_tile_consts(t)

        def visit(off, carry, mask):
            k_t = k_ref[pl.ds(off, t), :]
            v_t = v_ref[pl.ds(off, t), :]
            c_l, acc = carry
            _, l1, a = _sb_scores(qs, k_t, upper, c_l, mask)
            return c_l + _rowsum(l1), acc + _dot(a.astype(BF16), v_t)

        init = (jnp.zeros((2 * t, 1), F32), jnp.zeros((2 * t, LANES), F32))
        _, acc = _sb_walk(i, t, visit, init, causal)
        o_ref[...] = jnp.where(first, acc[:t], acc[t:])

    outs, landed = _call(
        body, "sb_fwd", (n_pairs, nq),
        in_specs=[pl.BlockSpec((t, LANES), lambda h, i: (i, h)),
                  pl.BlockSpec((s, LANES), lambda h, i: (0, n_pairs + h)),
                  pl.BlockSpec((s, LANES), lambda h, i: (0, 2 * n_pairs + h))],
        out_specs=[pl.BlockSpec((t, LANES), lambda h, i: (i, h))],
        out_shape=[jax.ShapeDtypeStruct((s, n_pairs * LANES), F32)],
        args=(proj, proj, proj), comm=comm)
    return outs[0], landed


def _swa_bucket_table():
    qi = np.arange(SWA_BLOCK)[:, None]
    cj = np.arange(2 * SWA_BLOCK)[None, :]
    dist = qi + SWA_BLOCK - cj
    exact = REL_BUCKETS // 2
    d = np.maximum(dist, 0)
    d_f = np.maximum(d, 1).astype(np.float32)
    large = exact + (np.log(d_f / np.float32(exact)) / np.float32(math.log(REL_MAX_DIST / exact))
                     * np.float32(REL_BUCKETS - exact)).astype(np.int32)
    large = np.minimum(large, REL_BUCKETS - 1)
    return np.where(d < exact, d, large).astype(np.int32)


def _swa_build_bias(bucket_ref, rb_ref, bias_ref, n_groups, per_group):
    bk = bucket_ref[...]
    for g in range(n_groups):
        for hh in range(per_group):
            acc = jnp.zeros(bk.shape, F32)
            for b in range(REL_BUCKETS):
                acc = jnp.where(bk == b, rb_ref[b, g * per_group + hh], acc)
            bias_ref[g, hh * SWA_BLOCK:(hh + 1) * SWA_BLOCK, :] = acc


def _swa_valid(i, reps):
    shape = (reps * SWA_BLOCK, 2 * SWA_BLOCK)
    row = lax.broadcasted_iota(jnp.int32, shape, 0) & (SWA_BLOCK - 1)
    col = lax.broadcasted_iota(jnp.int32, shape, 1)
    dist = row + SWA_BLOCK - col
    return (dist >= 0) & (dist < SWA_BLOCK) & ((col >= SWA_BLOCK) | (i > 0))


def _swa_place(blk, h, group, sel):
    if (h % 2) != group:
        blk = pltpu.roll(blk.astype(F32), HEAD_DIM, axis=1).astype(BF16)
    return jnp.where(sel, blk, jnp.zeros_like(blk))


def _swa_stack(ref, group, per_group, sel):
    parts = []
    for hh in range(per_group):
        h = group * per_group + hh
        parts.append(_swa_place(ref[:, (h // 2) * LANES:(h // 2 + 1) * LANES], h, group, sel))
    return jnp.concatenate(parts, axis=0)


def _swa_unstack(stacked, group, per_group, pieces):
    for hh in range(per_group):
        h = group * per_group + hh
        piece = stacked[hh * SWA_BLOCK:(hh + 1) * SWA_BLOCK, :]
        pieces[h] = pltpu.roll(piece, HEAD_DIM, axis=1) if (h % 2) != group else piece


def _swa_sink_rows(sk_ref, group, per_group):
    rowh = lax.broadcasted_iota(jnp.int32, (per_group * SWA_BLOCK, 1), 0) // SWA_BLOCK
    sink = jnp.zeros((per_group * SWA_BLOCK, 1), F32) + sk_ref[0, group * per_group]
    for hh in range(1, per_group):
        sink = jnp.where(rowh == hh, sk_ref[0, group * per_group + hh], sink)
    return sink


def _swa_probs(q_pos, kcat, bias_h, valid, sink):
    logits = _dot_nt(q_pos, kcat) * SCALE + bias_h
    logits = jnp.where(valid, logits, -jnp.inf)
    m = jnp.maximum(jnp.max(logits, axis=1, keepdims=True), sink)
    p = jnp.exp(logits - m)
    es = jnp.exp(sink - m)
    denom = _rowsum(p) + es
    return p / denom, es / denom


def _swa_specs(n_heads, qcol, kcol, vcol):
    width = n_heads * HEAD_DIM
    prev = lambda col: pl.BlockSpec((SWA_BLOCK, LANES), lambda i: (jnp.maximum(i - 1, 0), col))
    cur = lambda col: pl.BlockSpec((SWA_BLOCK, LANES), lambda i: (i, col))
    return [pl.BlockSpec((SWA_BLOCK, width), lambda i: (i, qcol)),
            prev(kcol), cur(kcol), prev(vcol), cur(vcol),
            _full((SWA_BLOCK, 2 * SWA_BLOCK)),
            pl.BlockSpec(memory_space=pltpu.SMEM), pl.BlockSpec(memory_space=pltpu.SMEM)]


def _swa_fwd(proj, bucket, rel_bias, sinks, n_heads, qcol, kcol, vcol):
    s = proj.shape[0]
    width = n_heads * HEAD_DIM
    n_groups = LANES // HEAD_DIM
    per_group = n_heads // n_groups

    def body(q_ref, kp_ref, kc_ref, vp_ref, vc_ref, bucket_ref, rb_ref, sk_ref, o_ref, bias_ref):
        i = pl.program_id(0)

        @pl.when(i == 0)
        def _():
            _swa_build_bias(bucket_ref, rb_ref, bias_ref, n_groups, per_group)

        lane = lax.broadcasted_iota(jnp.int32, (1, LANES), 1)
        first = lane < HEAD_DIM
        valid = _swa_valid(i, per_group)
        kcat = jnp.concatenate([kp_ref[...], kc_ref[...]], axis=0)
        vcat = jnp.concatenate([vp_ref[...], vc_ref[...]], axis=0)
        pieces = {}
        for g in range(n_groups):
            sel = first if g == 0 else jnp.logical_not(first)
            prob, _ = _swa_probs(_swa_stack(q_ref, g, per_group, sel), kcat, bias_ref[g], valid,
                                 _swa_sink_rows(sk_ref, g, per_group))
            _swa_unstack(_dot(prob.astype(BF16), vcat), g, per_group, pieces)
        for j in range(n_heads // 2):
            o_ref[:, j * LANES:(j + 1) * LANES] = jnp.where(first, pieces[2 * j], pieces[2 * j + 1])

    return pl.pallas_call(
        body, name="swa_fwd", grid=(s // SWA_BLOCK,),
        in_specs=_swa_specs(n_heads, qcol, kcol, vcol),
        out_specs=pl.BlockSpec((SWA_BLOCK, width), lambda i: (i, 0)),
        out_shape=jax.ShapeDtypeStruct((s, width), F32),
        scratch_shapes=[pltpu.VMEM((n_groups, per_group * SWA_BLOCK, 2 * SWA_BLOCK), F32)],
        compiler_params=_params(),
    )(proj, proj, proj, proj, proj, bucket, rel_bias, sinks)


def _rms_fwd(o, g):
    r = lax.rsqrt(jnp.mean(o * o, axis=-1, keepdims=True) + RMS_EPS)
    n = o * r
    return n, r, n * g


def _mix_ln1(sb_out, sw_out, x, g_in, b_in, sb_g, sw_g, w_out):
    s, d = x.shape
    wsb, wsw = sb_out.shape[1], sw_out.shape[1]
    tm = min(ROW_TILE, s)

    def body(sb_ref, sw_ref, x_ref, gi_ref, bi_ref, sbg_ref, swg_ref, w_ref, mg_ref, u_ref):
        _, _, m_sb = _rms_fwd(sb_ref[...], sbg_ref[...])
        _, _, m_sw = _rms_fwd(sw_ref[...], swg_ref[...])
        m_sb = m_sb.astype(BF16)
        m_sw = m_sw.astype(BF16)
        mg_ref[:, :wsb] = m_sb
        mg_ref[:, wsb:] = m_sw
        mix = _dot(m_sb, w_ref[:wsb, :]) + _dot(m_sw, w_ref[wsb:, :])
        xhat, _ = _ln_hat(x_ref[...])
        h0 = xhat * gi_ref[...] + bi_ref[...]
        u_ref[...] = ALPHA * h0 + mix

    row = lambda width: pl.BlockSpec((tm, width), lambda i: (i, 0))
    return pl.pallas_call(
        body, name="mix_ln1", grid=(s // tm,),
        in_specs=[row(wsb), row(wsw), row(d), _full((1, d)), _full((1, d)),
                  _full((1, wsb)), _full((1, wsw)), _full((wsb + wsw, d))],
        out_specs=[row(wsb + wsw), row(d)],
        out_shape=[jax.ShapeDtypeStruct((s, wsb + wsw), BF16), jax.ShapeDtypeStruct((s, d), F32)],
        compiler_params=_params(),
    )(sb_out, sw_out, x, g_in, b_in, sb_g, sw_g, w_out)


def _ffn_fwd(u1, g1, b1, w_gu, w_down, g2, b2, target):
    s, d = u1.shape
    dff = w_down.shape[0]
    tm = min(FFN_TILE, s)

    def body(u_ref, g1_ref, b1_ref, wgu_hbm, wd_hbm, g2_ref, b2_ref, t_ref, du_ref, st_ref, wgu_ref, wd_ref):
        @pl.when(pl.program_id(0) == 0)
        def _():
            pltpu.sync_copy(wgu_hbm, wgu_ref)
            pltpu.sync_copy(wd_hbm, wd_ref)
            st_ref[...] = jnp.zeros_like(st_ref)

        xhat, _ = _ln_hat(u_ref[...])
        h1 = xhat * g1_ref[...] + b1_ref[...]
        h1b = h1.astype(BF16)
        gate = _dot(h1b, wgu_ref[:, :dff])
        up = _dot(h1b, wgu_ref[:, dff:])
        act = gate * jax.nn.sigmoid(gate) * up
        u2 = ALPHA * h1 + _dot(act.astype(BF16), wd_ref[...])
        xhat2, r2 = _ln_hat(u2)
        diff = xhat2 * g2_ref[...] + b2_ref[...] - t_ref[...]
        dh2 = diff * (1.0 / d)
        st_ref[0:1, :] += _colsum(dh2 * xhat2)
        st_ref[1:2, :] += _colsum(dh2)
        st_ref[2:3, :] += jnp.broadcast_to(_colsum(_rowsum(diff * diff)) * (0.5 / d), (1, d))
        du_ref[...] = _ln_bwd(dh2 * g2_ref[...], xhat2, r2)

    row = pl.BlockSpec((tm, d), lambda i: (i, 0))
    hbm = pl.BlockSpec(memory_space=pl.ANY)
    return pl.pallas_call(
        body, name="ffn_fwd", grid=(s // tm,),
        in_specs=[row, _full((1, d)), _full((1, d)), hbm, hbm, _full((1, d)), _full((1, d)), row],
        out_specs=[row, _full((8, d))],
        out_shape=[jax.ShapeDtypeStruct((s, d), F32), jax.ShapeDtypeStruct((8, d), F32)],
        scratch_shapes=[pltpu.VMEM(w_gu.shape, BF16), pltpu.VMEM(w_down.shape, BF16)],
        compiler_params=_params(56),
    )(u1, g1, b1, w_gu, w_down, g2, b2, target)


def _ffn_bwd_act(u1, du2, g1, b1, w_gu, w_down_t):
    s, d = u1.shape
    dff = w_down_t.shape[1]
    tm = min(FFN_TILE, s)

    def body(u_ref, du2_ref, g1_ref, b1_ref, wgu_hbm, wdt_hbm, h1b_ref, act_ref, dgu_ref, wgu_ref, wdt_ref):
        @pl.when(pl.program_id(0) == 0)
        def _():
            pltpu.sync_copy(wgu_hbm, wgu_ref)
            pltpu.sync_copy(wdt_hbm, wdt_ref)

        xhat, _ = _ln_hat(u_ref[...])
        h1b = (xhat * g1_ref[...] + b1_ref[...]).astype(BF16)
        h1b_ref[...] = h1b
        gate = _dot(h1b, wgu_ref[:, :dff])
        up = _dot(h1b, wgu_ref[:, dff:])
        dact = _dot(du2_ref[...].astype(BF16), wdt_ref[...])
        sg = jax.nn.sigmoid(gate)
        silu = gate * sg
        act_ref[...] = (silu * up).astype(BF16)
        dgu_ref[:, :dff] = (dact * up * (sg * (1.0 + gate * (1.0 - sg)))).astype(BF16)
        dgu_ref[:, dff:] = (dact * silu).astype(BF16)

    row = lambda width: pl.BlockSpec((tm, width), lambda i: (i, 0))
    hbm = pl.BlockSpec(memory_space=pl.ANY)
    return pl.pallas_call(
        body, name="ffn_bwd_act", grid=(s // tm,),
        in_specs=[row(d), row(d), _full((1, d)), _full((1, d)), hbm, hbm],
        out_specs=[row(d), row(dff), row(2 * dff)],
        out_shape=[jax.ShapeDtypeStruct((s, d), BF16), jax.ShapeDtypeStruct((s, dff), BF16),
                   jax.ShapeDtypeStruct((s, 2 * dff), BF16)],
        scratch_shapes=[pltpu.VMEM(w_gu.shape, BF16), pltpu.VMEM(w_down_t.shape, BF16)],
        compiler_params=_params(56),
    )(u1, du2, g1, b1, w_gu, w_down_t)


def _ffn_bwd_in(dgu, w_gu_t, u1, du2, g1):
    s, d = u1.shape
    tm = min(FFN_TILE, s)

    def body(dgu_ref, wt_hbm, u_ref, du2_ref, g1_ref, du1_ref, st_ref, wt_ref):
        @pl.when(pl.program_id(0) == 0)
        def _():
            pltpu.sync_copy(wt_hbm, wt_ref)
            st_ref[...] = jnp.zeros_like(st_ref)

        dh1 = _dot(dgu_ref[...], wt_ref[...]) + ALPHA * du2_ref[...]
        xhat, r = _ln_hat(u_ref[...])
        st_ref[0:1, :] += _colsum(dh1 * xhat)
        st_ref[1:2, :] += _colsum(dh1)
        du1_ref[...] = _ln_bwd(dh1 * g1_ref[...], xhat, r)

    row = lambda width: pl.BlockSpec((tm, width), lambda i: (i, 0))
    return pl.pallas_call(
        body, name="ffn_bwd_in", grid=(s // tm,),
        in_specs=[row(dgu.shape[1]), pl.BlockSpec(memory_space=pl.ANY), row(d), row(d), _full((1, d))],
        out_specs=[row(d), _full((8, d))],
        out_shape=[jax.ShapeDtypeStruct((s, d), F32), jax.ShapeDtypeStruct((8, d), F32)],
        scratch_shapes=[pltpu.VMEM(w_gu_t.shape, BF16)],
        compiler_params=_params(56),
    )(dgu, w_gu_t, u1, du2, g1)


def _rms_bwd(dm, o, g):
    n, r, _ = _rms_fwd(o, g)
    dn = dm * g
    return r * (dn - n * jnp.mean(dn * n, axis=-1, keepdims=True)), _colsum(dm * n)


def _mix_bwd(du1, w_out, sb_out, sw_out, sb_g, sw_g):
    s, d = du1.shape
    wsb, wsw = sb_out.shape[1], sw_out.shape[1]
    tm = min(ROW_TILE, s)

    def body(du_ref, w_ref, sb_ref, sw_ref, sbg_ref, swg_ref, dsb_ref, dsw_ref, st_ref):
        i = pl.program_id(0)

        @pl.when(i == 0)
        def _():
            st_ref[...] = jnp.zeros_like(st_ref)

        dmerged = _dot_nt(du_ref[...].astype(BF16), w_ref[...])
        dsb, gsb = _rms_bwd(dmerged[:, :wsb], sb_ref[...], sbg_ref[...])
        dsw, gsw = _rms_bwd(dmerged[:, wsb:], sw_ref[...], swg_ref[...])
        dsb_ref[...] = dsb.astype(BF16)
        dsw_ref[...] = dsw.astype(BF16)
        st_ref[0:1, :wsb] += gsb
        st_ref[0:1, wsb:] += gsw

    row = lambda width: pl.BlockSpec((tm, width), lambda i: (i, 0))
    return pl.pallas_call(
        body, name="mix_bwd", grid=(s // tm,),
        in_specs=[row(d), _full((wsb + wsw, d)), row(wsb), row(wsw), _full((1, wsb)), _full((1, wsw))],
        out_specs=[row(wsb), row(wsw), _full((8, wsb + wsw))],
        out_shape=[jax.ShapeDtypeStruct((s, wsb), BF16), jax.ShapeDtypeStruct((s, wsw), BF16),
                   jax.ShapeDtypeStruct((8, wsb + wsw), F32)],
        compiler_params=_params(),
    )(du1, w_out, sb_out, sw_out, sb_g, sw_g)


def _sb_bwd(proj, dout, out, n_pairs, comm=None):
    s = proj.shape[0]
    t = min(SB_TILE, s)
    nq = s // t
    width = n_pairs * LANES

    def body(q_ref, k_ref, v_ref, do_ref, o_ref, dq_ref, dk_out, dv_out, dk_ref, dv_ref):
        i = pl.program_id(1)

        @pl.when(i == 0)
        def _():
            dk_ref[...] = jnp.zeros_like(dk_ref)
            dv_ref[...] = jnp.zeros_like(dv_ref)

        lane = lax.broadcasted_iota(jnp.int32, (1, LANES), 1)
        first = lane < HEAD_DIM
        do2 = do_ref[...]
        qs = _sb_stack_heads(q_ref[...] * SCALE, first)
        dos = _sb_stack_heads(do2, first)
        prod = do2.astype(F32) * o_ref[...]
        totals = jnp.concatenate([_rowsum(jnp.where(first, prod, 0.0)), _rowsum(jnp.where(first, 0.0, prod))], axis=0)
        upper, incl, causal = _sb_tile_consts(t)

        def visit(off, carry, mask):
            k_t = k_ref[pl.ds(off, t), :]
            v_t = v_ref[pl.ds(off, t), :]
            c_l, c_e, dq = carry
            lb, l1, a = _sb_scores(qs, k_t, upper, c_l, mask)
            a_b = a.astype(BF16)
            d_e = _dot_nt(dos, v_t) * a_b.astype(F32)
            d_l = totals - (_dot(d_e.astype(BF16), incl) + c_e)
            dz = d_e - jnp.exp(lb) * (d_e + d_l)
            if mask is not None:
                dz = jnp.where(mask, dz, 0.0)
            dzb = dz.astype(BF16)
            dk_ref[pl.ds(off, t), :] += _dot_tn(dzb, qs)
            dv_ref[pl.ds(off, t), :] += _dot_tn(a_b, dos)
            return c_l + _rowsum(l1), c_e + _rowsum(d_e), dq + _dot(dzb, k_t)

        init = (jnp.zeros((2 * t, 1), F32), jnp.zeros((2 * t, 1), F32), jnp.zeros((2 * t, LANES), F32))
        _, _, dq = _sb_walk(i, t, visit, init, causal)
        dq_ref[...] = (jnp.where(first, dq[:t], dq[t:]) * SCALE).astype(BF16)

        @pl.when(i == nq - 1)
        def _():
            dk_out[...] = dk_ref[...].astype(BF16)
            dv_out[...] = dv_ref[...].astype(BF16)

    qblk = pl.BlockSpec((t, LANES), lambda h, i: (i, h))
    whole = pl.BlockSpec((s, LANES), lambda h, i: (0, h))
    return _call(
        body, "sb_bwd", (n_pairs, nq),
        in_specs=[qblk,
                  pl.BlockSpec((s, LANES), lambda h, i: (0, n_pairs + h)),
                  pl.BlockSpec((s, LANES), lambda h, i: (0, 2 * n_pairs + h)),
                  qblk, qblk],
        out_specs=[qblk, whole, whole],
        out_shape=[jax.ShapeDtypeStruct((s, width), BF16)] * 3,
        args=(proj, proj, proj, dout, out),
        scratch_shapes=[pltpu.VMEM((s, LANES), F32), pltpu.VMEM((s, LANES), F32)], comm=comm)


def _swa_bwd(proj, dout, bucket, rel_bias, sinks, n_heads, qcol, kcol, vcol, comm=None):
    s = proj.shape[0]
    width = n_heads * HEAD_DIM
    n_groups = LANES // HEAD_DIM
    per_group = n_heads // n_groups
    nb = s // SWA_BLOCK

    def body(q_ref, kp_ref, kc_ref, vp_ref, vc_ref, bucket_ref, rb_ref, sk_ref, do_ref,
             dq_ref, dk_out, dv_out, dsk_ref, drb_ref, bias_ref, dbias_ref, dk_ref, dv_ref):
        i = pl.program_id(0)

        @pl.when(i == 0)
        def _():
            _swa_build_bias(bucket_ref, rb_ref, bias_ref, n_groups, per_group)
            dbias_ref[...] = jnp.zeros_like(dbias_ref)
            dk_ref[...] = jnp.zeros_like(dk_ref)
            dv_ref[...] = jnp.zeros_like(dv_ref)
            dsk_ref[...] = jnp.zeros_like(dsk_ref)

        lane = lax.broadcasted_iota(jnp.int32, (1, LANES), 1)
        first = lane < HEAD_DIM
        valid = _swa_valid(i, per_group)
        kcat = jnp.concatenate([kp_ref[...], kc_ref[...]], axis=0)
        vcat = jnp.concatenate([vp_ref[...], vc_ref[...]], axis=0)
        dkcat = jnp.zeros((2 * SWA_BLOCK, LANES), F32)
        dvcat = jnp.zeros((2 * SWA_BLOCK, LANES), F32)
        pieces = {}
        for g in range(n_groups):
            sel = first if g == 0 else jnp.logical_not(first)
            q_g = _swa_stack(q_ref, g, per_group, sel)
            do_g = _swa_stack(do_ref, g, per_group, sel)
            prob, p_sink = _swa_probs(q_g, kcat, bias_ref[g], valid, _swa_sink_rows(sk_ref, g, per_group))
            dprob = _dot_nt(do_g, vcat)
            delta = _rowsum(prob * dprob)
            dlog = prob * (dprob - delta)
            sink_term = p_sink * delta
            for hh in range(per_group):
                h = g * per_group + hh
                tot = _colsum(sink_term[hh * SWA_BLOCK:(hh + 1) * SWA_BLOCK, :])
                dsk_ref[h:h + 1, :] += jnp.broadcast_to(-tot, (1, LANES))
            dbias_ref[g] += dlog
            dlb = (dlog * SCALE).astype(BF16)
            _swa_unstack(_dot(dlb, kcat), g, per_group, pieces)
            dkcat += _dot_tn(dlb, q_g)
            dvcat += _dot_tn(prob.astype(BF16), do_g)
        for j in range(n_heads // 2):
            dq_ref[:, j * LANES:(j + 1) * LANES] = jnp.where(first, pieces[2 * j], pieces[2 * j + 1]).astype(BF16)

        cur = pl.multiple_of(i * SWA_BLOCK, SWA_BLOCK)
        dk_ref[pl.ds(cur, SWA_BLOCK), :] += dkcat[SWA_BLOCK:, :]
        dv_ref[pl.ds(cur, SWA_BLOCK), :] += dvcat[SWA_BLOCK:, :]

        @pl.when(i > 0)
        def _():
            prv = pl.multiple_of((i - 1) * SWA_BLOCK, SWA_BLOCK)
            dk_ref[pl.ds(prv, SWA_BLOCK), :] += dkcat[:SWA_BLOCK, :]
            dv_ref[pl.ds(prv, SWA_BLOCK), :] += dvcat[:SWA_BLOCK, :]

        @pl.when(i == nb - 1)
        def _():
            bk = bucket_ref[...]
            rowi = lax.broadcasted_iota(jnp.int32, (REL_BUCKETS, LANES), 0)
            coli = lax.broadcasted_iota(jnp.int32, (REL_BUCKETS, LANES), 1)
            res = jnp.zeros((REL_BUCKETS, LANES), F32)
            for h in range(n_heads):
                g, hh = divmod(h, per_group)
                db = dbias_ref[g, hh * SWA_BLOCK:(hh + 1) * SWA_BLOCK, :]
                for b in range(REL_BUCKETS):
                    tot = _colsum(_rowsum(jnp.where(bk == b, db, 0.0)))
                    res = jnp.where((rowi == b) & (coli == h), tot, res)
            drb_ref[...] = res
            dk_out[...] = dk_ref[...].astype(BF16)
            dv_out[...] = dv_ref[...].astype(BF16)

    in_specs = _swa_specs(n_heads, qcol, kcol, vcol) + [pl.BlockSpec((SWA_BLOCK, width), lambda i: (i, 0))]
    return _call(
        body, "swa_bwd", (nb,),
        in_specs=in_specs,
        out_specs=[pl.BlockSpec((SWA_BLOCK, width), lambda i: (i, 0)),
                   _full((s, LANES)), _full((s, LANES)), _full((8, LANES)), _full((REL_BUCKETS, LANES))],
        out_shape=[jax.ShapeDtypeStruct((s, width), BF16), jax.ShapeDtypeStruct((s, LANES), BF16),
                   jax.ShapeDtypeStruct((s, LANES), BF16), jax.ShapeDtypeStruct((8, LANES), F32),
                   jax.ShapeDtypeStruct((REL_BUCKETS, LANES), F32)],
        args=(proj, proj, proj, proj, proj, bucket, rel_bias, sinks, dout),
        scratch_shapes=[pltpu.VMEM((n_groups, per_group * SWA_BLOCK, 2 * SWA_BLOCK), F32),
                        pltpu.VMEM((n_groups, per_group * SWA_BLOCK, 2 * SWA_BLOCK), F32),
                        pltpu.VMEM((s, LANES), F32), pltpu.VMEM((s, LANES), F32)],
        comm=comm)


def _proj_bwd(pieces, w_in, du1, x, g_in, comm=None):
    s, d = x.shape
    cols = w_in.shape[1]
    tm = min(ROW_TILE, s)
    n_p = len(pieces)

    def body(*refs):
        p_refs = refs[:n_p]
        w_ref, du_ref, x_ref, g_ref, dx_ref, st_ref = refs[n_p:]
        i = pl.program_id(0)

        @pl.when(i == 0)
        def _():
            st_ref[...] = jnp.zeros_like(st_ref)

        dproj = jnp.concatenate([p[...] for p in p_refs], axis=1)
        dh0 = _dot_nt(dproj, w_ref[...]) + ALPHA * du_ref[...]
        xhat, r = _ln_hat(x_ref[...])
        st_ref[0:1, :] += _colsum(dh0 * xhat)
        st_ref[1:2, :] += _colsum(dh0)
        dx_ref[...] = _ln_bwd(dh0 * g_ref[...], xhat, r)

    row = lambda width: pl.BlockSpec((tm, width), lambda i: (i, 0))
    return _call(
        body, "proj_bwd", (s // tm,),
        in_specs=[row(p.shape[1]) for p in pieces] + [_full((d, cols)), row(d), row(d), _full((1, d))],
        out_specs=[row(d), _full((8, d))],
        out_shape=[jax.ShapeDtypeStruct((s, d), F32), jax.ShapeDtypeStruct((8, d), F32)],
        args=(*pieces, w_in, du1, x, g_in), comm=comm)


def _wgrad(name, a, pieces, tm, tn, dev_cols=None):
    s, m = a.shape
    n = sum(p.shape[1] for p in pieces)
    n_p = len(pieces)
    assert n_p == 1 or tn == n
    ts = min(WGRAD_TOKENS if pieces[0].dtype == BF16 and n_p == 1 else WGRAD_TOKENS // 2, s)
    n_k = s // ts

    def body(*refs):
        a_ref, p_refs, o_ref, acc_ref = refs[0], refs[1:1 + n_p], refs[1 + n_p], refs[2 + n_p]
        k = pl.program_id(2)

        @pl.when(k == 0)
        def _():
            acc_ref[...] = jnp.zeros_like(acc_ref)

        b = p_refs[0][...] if n_p == 1 else jnp.concatenate([p[...] for p in p_refs], axis=1)
        acc_ref[...] += _dot_tn(a_ref[...].astype(BF16), b.astype(BF16))

        @pl.when(k == n_k - 1)
        def _():
            if dev_cols is None:
                o_ref[...] = acc_ref[...].astype(BF16)
            else:
                for dev in range(tn // dev_cols):
                    o_ref[dev] = acc_ref[:, dev * dev_cols:(dev + 1) * dev_cols].astype(BF16)

    if dev_cols is None:
        out_spec = pl.BlockSpec((tm, tn), lambda i, j, k: (i, j))
        out_shape = jax.ShapeDtypeStruct((m, n), BF16)
    else:
        out_spec = pl.BlockSpec((tn // dev_cols, tm, dev_cols), lambda i, j, k: (j, i, 0))
        out_shape = jax.ShapeDtypeStruct((n // dev_cols, m, dev_cols), BF16)
    piece_spec = lambda p: pl.BlockSpec((ts, tn if n_p == 1 else p.shape[1]), lambda i, j, k: (k, j))
    return pl.pallas_call(
        body, name=name, grid=(m // tm, n // tn, n_k),
        in_specs=[pl.BlockSpec((ts, tm), lambda i, j, k: (k, i))] + [piece_spec(p) for p in pieces],
        out_specs=out_spec, out_shape=out_shape,
        scratch_shapes=[pltpu.VMEM((tm, tn), F32)],
        compiler_params=_params(),
    )(a, *pieces)


def _adamw_math(w, g, m, v):
    m = ADAM_B1 * m + (1.0 - ADAM_B1) * g
    v = ADAM_B2 * v + (1.0 - ADAM_B2) * (g * g)
    m_hat = m / (1.0 - ADAM_B1 ** ADAM_STEP)
    v_hat = v / (1.0 - ADAM_B2 ** ADAM_STEP)
    delta = -ADAM_LR * (m_hat / (jnp.sqrt(v_hat) + ADAM_EPS) + ADAM_WD * w)
    return delta, m, v


def _adamw(name, landed, w, m, v, tr):
    rows, cols = w.shape

    def body(l_ref, w_ref, m_ref, v_ref, g_ref, d_ref, nm_ref, nv_ref):
        g = l_ref[0].astype(F32)
        for src in range(1, N_DEV):
            g = g + l_ref[src].astype(F32)
        delta, nm, nv = _adamw_math(w_ref[...], g, m_ref[...], v_ref[...])
        g_ref[...] = g
        d_ref[...] = delta
        nm_ref[...] = nm
        nv_ref[...] = nv

    blk = pl.BlockSpec((tr, cols), lambda i: (i, 0))
    shape = jax.ShapeDtypeStruct((rows, cols), F32)
    return pl.pallas_call(
        body, name=name, grid=(rows // tr,),
        in_specs=[pl.BlockSpec((N_DEV, tr, cols), lambda i: (0, i, 0)), blk, blk, blk],
        out_specs=[blk, blk, blk, blk],
        out_shape=[shape, shape, shape, shape],
        compiler_params=_params(),
    )(landed, w, m, v)


def _pack(d, ln_in_g, ln_in_b, ln1_g, ln1_b, ln2_g, ln2_b, sb_g, sw_g, rel_bias, sinks, extra=None):
    tail = [rel_bias.reshape(-1), sinks.reshape(-1)]
    if extra is not None:
        tail.append(extra.reshape(-1))
    tail = jnp.concatenate(tail)
    tail = jnp.concatenate([tail, jnp.zeros((d - tail.shape[0],), F32)])
    rows = [ln_in_g.reshape(-1), ln_in_b.reshape(-1), ln1_g.reshape(-1), ln1_b.reshape(-1),
            ln2_g.reshape(-1), ln2_b.reshape(-1),
            jnp.concatenate([sb_g.reshape(-1), sw_g.reshape(-1)]), tail]
    return jnp.stack(rows)


def _unpack(p, wsb, n_rb, n_sk):
    return [p[0], p[1], p[6, :wsb][None], p[6, wsb:][None], p[7, n_rb:n_rb + n_sk][None],
            p[7, :n_rb].reshape(REL_BUCKETS, -1), p[2][None], p[3][None], p[4][None], p[5][None]]


def kernel(x, ln_in_g, ln_in_b, w_in, sb_norm_g, swa_norm_g, sinks, rel_bias, w_out, ln1_g, ln1_b, w_gate_up, w_down, ln2_g, ln2_b, loss_target, m_ln_in_g, m_ln_in_b, m_w_in, m_sb_norm_g, m_swa_norm_g, m_sinks, m_rel_bias, m_w_out, m_ln1_g, m_ln1_b, m_w_gate_up, m_w_down, m_ln2_g, m_ln2_b, v_ln_in_g, v_ln_in_b, v_w_in, v_sb_norm_g, v_swa_norm_g, v_sinks, v_rel_bias, v_w_out, v_ln1_g, v_ln1_b, v_w_gate_up, v_w_down, v_ln2_g, v_ln2_b):
    x2 = x[0]
    tgt = loss_target[0]
    s, d = x2.shape
    wsb = sb_norm_g.shape[-1]
    wsw = swa_norm_g.shape[-1]
    n_sw_heads = sinks.shape[-1]
    n_pairs = wsb // LANES
    dff = w_down.shape[1] * N_DEV
    assert wsb % LANES == 0 and wsw % LANES == 0 and n_sw_heads * HEAD_DIM == wsw
    assert 3 * wsb % wsw == 0 and dff % LANES == 0 and s % SWA_BLOCK == 0
    qcol = 3 * wsb // wsw
    kcol = (3 * wsb + wsw) // LANES
    vcol = kcol + 1
    assert w_in.shape[-1] * N_DEV == (vcol + 1) * LANES

    big_w = [w_in[0], w_out[0], w_gate_up[0], w_down[0]]
    big_m = [m_w_in[0], m_w_out[0], m_w_gate_up[0], m_w_down[0]]
    big_v = [v_w_in[0], v_w_out[0], v_w_gate_up[0], v_w_down[0]]

    cat_cols = lambda g: jnp.transpose(g, (1, 0, 2)).reshape(g.shape[1], N_DEV * g.shape[2])
    cat_rows = lambda g: g.reshape(N_DEV * g.shape[1], g.shape[2])
    shards = [w.astype(BF16) for w in big_w]
    w_in_f = cat_cols(_exchange("w_in_allgather", shards[:1], ["gather"])[0])

    vec = lambda a: a.reshape(1, -1)
    g_in, b_in = vec(ln_in_g), vec(ln_in_b)
    bucket = jnp.asarray(_swa_bucket_table())

    h0b, proj = _ln_proj(x2, g_in, b_in, w_in_f)
    sb_out, gathered = _sb_fwd(proj, n_pairs, comm=(shards[1:], ["gather"] * 3))
    w_out_f, w_gu_f, w_down_f = cat_rows(gathered[0]), cat_cols(gathered[1]), cat_rows(gathered[2])
    sw_out = _swa_fwd(proj, bucket, rel_bias, sinks, n_sw_heads, qcol, kcol, vcol)
    merged, u1 = _mix_ln1(sb_out, sw_out, x2, g_in, b_in, sb_norm_g, swa_norm_g, w_out_f)
    du2, st_ln2 = _ffn_fwd(u1, ln1_g, ln1_b, w_gu_f, w_down_f, ln2_g, ln2_b, tgt)

    split_rows = lambda g: g.reshape(N_DEV, g.shape[0] // N_DEV, g.shape[1])
    tile_m = min(512, d)
    h1b, act, dgu = _ffn_bwd_act(u1, du2, ln1_g, ln1_b, w_gu_f, jnp.transpose(w_down_f))
    du1, st_ln1 = _ffn_bwd_in(dgu, jnp.transpose(w_gu_f), u1, du2, ln1_g)
    gw_gu = _wgrad("wgrad_gate_up", h1b, [dgu], tile_m, dff // 2, dev_cols=w_gate_up.shape[-1])
    gw_down = _wgrad("wgrad_down", act, [du2], dff // 2, d)
    gw_out = _wgrad("wgrad_out", merged, [du1], tile_m, d)
    dsb, dsw, st_rms = _mix_bwd(du1, w_out_f, sb_out, sw_out, sb_norm_g, swa_norm_g)
    (dq_sb, dk_sb, dv_sb), (land_gu, land_out) = _sb_bwd(
        proj, dsb, sb_out, n_pairs, comm=([gw_gu, split_rows(gw_out)], ["scatter"] * 2))
    (dq_sw, dk_sw, dv_sw, st_sink, st_rb), (land_down,) = _swa_bwd(
        proj, dsw, bucket, rel_bias, sinks, n_sw_heads, qcol, kcol, vcol,
        comm=([split_rows(gw_down)], ["scatter"]))
    pieces = [dq_sb, dk_sb, dv_sb, dq_sw, dk_sw, dv_sw]
    gw_in = _wgrad("wgrad_in", h0b, pieces, tile_m, proj.shape[1], dev_cols=w_in.shape[-1])
    (grad_x, st_in), (land_in,) = _proj_bwd(pieces, w_in_f, du1, x2, g_in, comm=([gw_in], ["scatter"]))

    n_rb = rel_bias.size
    small = _pack(d, st_in[0], st_in[1], st_ln1[0], st_ln1[1], st_ln2[0], st_ln2[1],
                  st_rms[0, :wsb], st_rms[0, wsb:], st_rb[:, :n_sw_heads], st_sink[:n_sw_heads, 0],
                  extra=st_ln2[2, 0:1])
    land_small = _exchange("small_grads_allgather", [small], ["gather"])[0]
    landed = [land_in, land_out, land_gu, land_down, land_small]

    big = []
    for name, land, w, m, v in zip(["adamw_in", "adamw_out", "adamw_gate_up", "adamw_down"], landed[:4], big_w, big_m, big_v):
        rows = w.shape[0]
        tr = 128 if rows % 128 == 0 else rows // 2
        big.append(_adamw(name, land, w, m, v, tr))

    small_w = _pack(d, ln_in_g, ln_in_b, ln1_g, ln1_b, ln2_g, ln2_b, sb_norm_g, swa_norm_g, rel_bias, sinks)
    small_m = _pack(d, m_ln_in_g, m_ln_in_b, m_ln1_g, m_ln1_b, m_ln2_g, m_ln2_b, m_sb_norm_g, m_swa_norm_g, m_rel_bias, m_sinks)
    small_v = _pack(d, v_ln_in_g, v_ln_in_b, v_ln1_g, v_ln1_b, v_ln2_g, v_ln2_b, v_sb_norm_g, v_swa_norm_g, v_rel_bias, v_sinks)
    sg, sd, sm, sv = _adamw("adamw_small", landed[4], small_w, small_m, small_v, 8)
    n_sk = sinks.size
    loss = sg[7, n_rb + n_sk]

    def leaves(idx):
        sm_l = _unpack([sg, sd, sm, sv][idx], wsb, n_rb, n_sk)
        bg = [b[idx][None] for b in big]
        return [sm_l[0], sm_l[1], bg[0], sm_l[2], sm_l[3], sm_l[4], sm_l[5], bg[1], sm_l[6], sm_l[7], bg[2], bg[3], sm_l[8], sm_l[9]]

    return (loss, grad_x[None], *leaves(0), *leaves(1), *leaves(2), *leaves(3))
```

```python
import functools
import math

import numpy as np
import jax
import jax.numpy as jnp
from jax import lax
from jax.experimental import pallas as pl
from jax.experimental.pallas import tpu as pltpu

F32 = jnp.float32
BF16 = jnp.bfloat16
MESH = pl.DeviceIdType.MESH

N_DEV = 8
LANES = 128
HEAD_DIM = 64
SCALE = HEAD_DIM ** -0.5
SWA_BLOCK = 128
REL_BUCKETS = 32
REL_MAX_DIST = 128
ALPHA = 2.0 ** 0.25
LN_EPS = 1e-5
RMS_EPS = 1e-6
ADAM_LR = 0.001
ADAM_B1 = 0.9
ADAM_B2 = 0.999
ADAM_EPS = 1e-08
ADAM_WD = 0.01
ADAM_STEP = 10

ROW_TILE = 512
SB_TILE = 256
FFN_TILE = 256
WGRAD_TOKENS = 2048
SB_UNDERFLOW = -110.0
MIB = 1024 * 1024


def _params(vmem_mib=48):
    return pltpu.CompilerParams(vmem_limit_bytes=vmem_mib * MIB)


def _dot(a, b):
    return jnp.dot(a, b, preferred_element_type=F32)


def _dot_nt(a, b):
    return lax.dot_general(a, b, (((1,), (1,)), ((), ())), preferred_element_type=F32)


def _dot_tn(a, b):
    return lax.dot_general(a, b, (((0,), (0,)), ((), ())), preferred_element_type=F32)


def _ln_hat(x):
    mu = jnp.mean(x, axis=-1, keepdims=True)
    xc = x - mu
    var = jnp.mean(xc * xc, axis=-1, keepdims=True)
    r = lax.rsqrt(var + LN_EPS)
    return xc * r, r


def _ln_bwd(dxhat, xhat, r):
    return r * (dxhat - jnp.mean(dxhat, axis=-1, keepdims=True)
                - xhat * jnp.mean(dxhat * xhat, axis=-1, keepdims=True))


def _colsum(a):
    return jnp.sum(a, axis=0, keepdims=True)


def _rowsum(a):
    return jnp.sum(a, axis=1, keepdims=True)


def _full(shape):
    return pl.BlockSpec(shape, lambda *_: (0,) * len(shape))


def _comm_out_shapes(arrays, kinds):
    shapes = []
    for a, kind in zip(arrays, kinds):
        blk = a.shape if kind == "gather" else a.shape[1:]
        shapes.append(jax.ShapeDtypeStruct((N_DEV,) + tuple(blk), a.dtype))
    return shapes


def _comm_sems(n):
    return [pltpu.SemaphoreType.DMA((n, N_DEV - 1)), pltpu.SemaphoreType.DMA((n, N_DEV - 1)),
            pltpu.SemaphoreType.DMA((n,))]


def _comm_copies(ins, outs, kinds, send_sems, recv_sems, local_sems):
    x, y, c = lax.axis_index("x"), lax.axis_index("y"), lax.axis_index("c")
    me = 4 * x + 2 * y + c

    def src_for(t, dev_lin):
        return ins[t] if kinds[t] == "gather" else ins[t].at[dev_lin]

    local = [pltpu.make_async_copy(src_for(t, me), outs[t].at[me], local_sems.at[t]) for t in range(len(kinds))]
    sends, arrivals = [], []
    for k in range(1, N_DEV):
        px = 1 - x if (k >> 2) & 1 else x
        py = 1 - y if (k >> 1) & 1 else y
        pc = 1 - c if k & 1 else c
        peer_lin = 4 * px + 2 * py + pc
        for t in range(len(kinds)):
            sems = dict(send_sem=send_sems.at[t, k - 1], recv_sem=recv_sems.at[t, k - 1],
                        device_id=(px, py, pc), device_id_type=MESH)
            sends.append(pltpu.make_async_remote_copy(src_ref=src_for(t, peer_lin), dst_ref=outs[t].at[me], **sems))
            arrivals.append(pltpu.make_async_remote_copy(src_ref=src_for(t, peer_lin), dst_ref=outs[t].at[peer_lin], **sems))
    return local, sends, arrivals


def _comm_start(ins, outs, kinds, sems):
    local, sends, _ = _comm_copies(ins, outs, kinds, *sems)
    for cp in local + sends:
        cp.start()


def _comm_finish(ins, outs, kinds, sems):
    local, sends, arrivals = _comm_copies(ins, outs, kinds, *sems)
    for cp in arrivals:
        cp.wait_recv()
    for cp in sends:
        cp.wait_send()
    for cp in local:
        cp.wait()


def _exchange(name, arrays, kinds):
    n = len(arrays)

    def body(*refs):
        ins, outs, sems = refs[:n], refs[n:2 * n], refs[2 * n:]
        _comm_start(ins, outs, kinds, sems)
        _comm_finish(ins, outs, kinds, sems)

    any_spec = pl.BlockSpec(memory_space=pl.ANY)
    return pl.pallas_call(
        body, name=name, out_shape=_comm_out_shapes(arrays, kinds),
        in_specs=[any_spec] * n, out_specs=[any_spec] * n,
        scratch_shapes=_comm_sems(n),
    )(*arrays)


def _call(body, name, grid, in_specs, out_specs, out_shape, args, scratch_shapes=(), comm=None):
    if comm is None:
        outs = pl.pallas_call(body, name=name, grid=grid, in_specs=in_specs, out_specs=out_specs,
                              out_shape=out_shape, scratch_shapes=list(scratch_shapes),
                              compiler_params=_params())(*args)
        return outs, []
    arrays, kinds = comm
    n, n_in, n_out, n_scr = len(arrays), len(in_specs), len(out_specs), len(scratch_shapes)

    def fused(*refs):
        c_in, x_in = refs[:n_in], refs[n_in:n_in + n]
        c_out = refs[n_in + n:n_in + n + n_out]
        x_out = refs[n_in + n + n_out:n_in + 2 * n + n_out]
        rest = refs[n_in + 2 * n + n_out:]
        c_scr, sems = rest[:n_scr], rest[n_scr:]
        ids = [pl.program_id(a) for a in range(len(grid))]
        is_first = functools.reduce(jnp.logical_and, [i == 0 for i in ids])
        is_last = functools.reduce(jnp.logical_and, [i == g - 1 for i, g in zip(ids, grid)])

        @pl.when(is_first)
        def _():
            _comm_start(x_in, x_out, kinds, sems)

        body(*c_in, *c_out, *c_scr)

        @pl.when(is_last)
        def _():
            _comm_finish(x_in, x_out, kinds, sems)

    any_spec = pl.BlockSpec(memory_space=pl.ANY)
    outs = pl.pallas_call(
        fused, name=name, grid=grid,
        in_specs=list(in_specs) + [any_spec] * n, out_specs=list(out_specs) + [any_spec] * n,
        out_shape=list(out_shape) + _comm_out_shapes(arrays, kinds),
        scratch_shapes=list(scratch_shapes) + _comm_sems(n),
        compiler_params=_params())(*args, *arrays)
    return outs[:n_out], outs[n_out:]


def _ln_proj(x, g, b, w_in):
    s, d = x.shape
    cols = w_in.shape[1]
    tm = min(ROW_TILE, s)

    def body(x_ref, g_ref, b_ref, w_ref, h_ref, p_ref):
        xhat, _ = _ln_hat(x_ref[...])
        h = (xhat * g_ref[...] + b_ref[...]).astype(BF16)
        h_ref[...] = h
        p_ref[...] = _dot(h, w_ref[...]).astype(BF16)

    row = lambda width: pl.BlockSpec((tm, width), lambda i: (i, 0))
    return pl.pallas_call(
        body, name="ln_proj", grid=(s // tm,),
        in_specs=[row(d), _full((1, d)), _full((1, d)), _full((d, cols))],
        out_specs=[row(d), row(cols)],
        out_shape=[jax.ShapeDtypeStruct((s, d), BF16), jax.ShapeDtypeStruct((s, cols), BF16)],
        compiler_params=_params(),
    )(x, g, b, w_in)


def _sb_triangles(t):
    row = lax.broadcasted_iota(jnp.int32, (t, t), 0)
    col = lax.broadcasted_iota(jnp.int32, (t, t), 1)
    return (row > col).astype(BF16), (row >= col).astype(BF16)


def _sb_first_mask(t, has_prev):
    qrow = lax.broadcasted_iota(jnp.int32, (2 * t, 2 * t), 0) & (t - 1)
    col = lax.broadcasted_iota(jnp.int32, (2 * t, 2 * t), 1)
    return ((col < t) & has_prev) | ((col >= t) & (col - t < qrow))


def _sb_stack_heads(x2, first):
    zero = jnp.zeros_like(x2)
    return jnp.concatenate([jnp.where(first, x2, zero), jnp.where(first, zero, x2)], axis=0)


def _sb_key_tiles(ref, offs, t):
    tiles = [ref[pl.ds(off, t), :] for off in offs]
    return tiles[0] if len(tiles) == 1 else jnp.concatenate(tiles, axis=0)


def _sb_suffix(terms, row_sums, tri, carry):
    out = [None] * len(terms)
    for j in reversed(range(len(terms))):
        suf = carry
        for op in terms[j]:
            suf = suf + _dot(op, tri)
        out[j] = suf
        carry = carry + row_sums[j]
    return (out[0] if len(out) == 1 else jnp.concatenate(out, axis=1)), carry


def _sb_scores(qh, k_t, upper, carry_l, mask, t):
    z = _dot_nt(qh, k_t)
    sp = jnp.log(1.0 + jnp.exp(-jnp.abs(z)))
    neg = jnp.minimum(z, 0.0)
    lb = neg - sp
    l1 = (neg - z) - sp
    if mask is not None:
        l1 = jnp.where(mask, l1, 0.0)
    hi = l1.astype(BF16)
    lo = (l1 - hi.astype(F32)).astype(BF16)
    cols = [slice(j * t, (j + 1) * t) for j in range(z.shape[1] // t)]
    suf, carry_l = _sb_suffix([[hi[:, c], lo[:, c]] for c in cols], [_rowsum(l1[:, c]) for c in cols], upper, carry_l)
    a = jnp.exp(lb + suf)
    if mask is not None:
        a = jnp.where(mask, a, 0.0)
    return lb, a, carry_l


def _sb_walk(i, t, visit, init):
    def alive(carry):
        return jnp.max(carry[0]) > SB_UNDERFLOW

    prev = pl.multiple_of(jnp.maximum(i - 1, 0) * t, t)
    carry = visit((prev, pl.multiple_of(i * t, t)), init, _sb_first_mask(t, i > 0))

    def cond(state):
        j, go, _ = state
        return (j < i - 1) & go

    def body(state):
        j, _, carry = state
        carry = visit((pl.multiple_of((i - 2 - j) * t, t),), carry, None)
        return j + 1, alive(carry), carry

    return lax.while_loop(cond, body, (jnp.int32(0), alive(carry), carry))[2]


def _sb_fwd(proj, n_pairs, comm=None):
    s = proj.shape[0]
    t = min(SB_TILE, s)
    nq = s // t

    def body(q_ref, k_ref, v_ref, o_ref):
        i = pl.program_id(1)
        lane = lax.broadcasted_iota(jnp.int32, (1, LANES), 1)
        first = lane < HEAD_DIM
        qs = _sb_stack_heads(q_ref[...] * SCALE, first)
        upper, _ = _sb_triangles(t)

        def visit(offs, carry, mask):
            c_l, acc = carry
            _, a, c_l = _sb_scores(qs, _sb_key_tiles(k_ref, offs, t), upper, c_l, mask, t)
            return c_l, acc + _dot(a.astype(BF16), _sb_key_tiles(v_ref, offs, t))

        init = (jnp.zeros((2 * t, 1), F32), jnp.zeros((2 * t, LANES), F32))
        _, acc = _sb_walk(i, t, visit, init)
        o_ref[...] = jnp.where(first, acc[:t], acc[t:])

    outs, landed = _call(
        body, "sb_fwd", (n_pairs, nq),
        in_specs=[pl.BlockSpec((t, LANES), lambda h, i: (i, h)),
                  pl.BlockSpec((s, LANES), lambda h, i: (0, n_pairs + h)),
                  pl.BlockSpec((s, LANES), lambda h, i: (0, 2 * n_pairs + h))],
        out_specs=[pl.BlockSpec((t, LANES), lambda h, i: (i, h))],
        out_shape=[jax.ShapeDtypeStruct((s, n_pairs * LANES), F32)],
        args=(proj, proj, proj), comm=comm)
    return outs[0], landed


def _swa_bucket_table():
    qi = np.arange(SWA_BLOCK)[:, None]
    cj = np.arange(2 * SWA_BLOCK)[None, :]
    dist = qi + SWA_BLOCK - cj
    exact = REL_BUCKETS // 2
    d = np.maximum(dist, 0)
    d_f = np.maximum(d, 1).astype(np.float32)
    large = exact + (np.log(d_f / np.float32(exact)) / np.float32(math.log(REL_MAX_DIST / exact))
                     * np.float32(REL_BUCKETS - exact)).astype(np.int32)
    large = np.minimum(large, REL_BUCKETS - 1)
    return np.where(d < exact, d, large).astype(np.int32)


def _swa_build_bias(bucket_ref, rb_ref, bias_ref, n_groups, per_group):
    bk = bucket_ref[...]
    for g in range(n_groups):
        for hh in range(per_group):
            acc = jnp.zeros(bk.shape, F32)
            for b in range(REL_BUCKETS):
                acc = jnp.where(bk == b, rb_ref[b, g * per_group + hh], acc)
            bias_ref[g, hh * SWA_BLOCK:(hh + 1) * SWA_BLOCK, :] = acc


def _swa_valid(i, reps):
    shape = (reps * SWA_BLOCK, 2 * SWA_BLOCK)
    row = lax.broadcasted_iota(jnp.int32, shape, 0) & (SWA_BLOCK - 1)
    col = lax.broadcasted_iota(jnp.int32, shape, 1)
    dist = row + SWA_BLOCK - col
    return (dist >= 0) & (dist < SWA_BLOCK) & ((col >= SWA_BLOCK) | (i > 0))


def _swa_place(blk, h, group, sel):
    if (h % 2) != group:
        blk = pltpu.roll(blk.astype(F32), HEAD_DIM, axis=1).astype(BF16)
    return jnp.where(sel, blk, jnp.zeros_like(blk))


def _swa_stack(ref, group, per_group, sel):
    parts = []
    for hh in range(per_group):
        h = group * per_group + hh
        parts.append(_swa_place(ref[:, (h // 2) * LANES:(h // 2 + 1) * LANES], h, group, sel))
    return jnp.concatenate(parts, axis=0)


def _swa_unstack(stacked, group, per_group, pieces):
    for hh in range(per_group):
        h = group * per_group + hh
        piece = stacked[hh * SWA_BLOCK:(hh + 1) * SWA_BLOCK, :]
        pieces[h] = pltpu.roll(piece, HEAD_DIM, axis=1) if (h % 2) != group else piece


def _swa_sink_rows(sk_ref, group, per_group):
    rowh = lax.broadcasted_iota(jnp.int32, (per_group * SWA_BLOCK, 1), 0) // SWA_BLOCK
    sink = jnp.zeros((per_group * SWA_BLOCK, 1), F32) + sk_ref[0, group * per_group]
    for hh in range(1, per_group):
        sink = jnp.where(rowh == hh, sk_ref[0, group * per_group + hh], sink)
    return sink


def _swa_probs(q_pos, kcat, bias_h, valid, sink):
    logits = _dot_nt(q_pos, kcat) * SCALE + bias_h
    logits = jnp.where(valid, logits, -jnp.inf)
    m = jnp.maximum(jnp.max(logits, axis=1, keepdims=True), sink)
    p = jnp.exp(logits - m)
    es = jnp.exp(sink - m)
    denom = _rowsum(p) + es
    return p / denom, es / denom


def _swa_specs(n_heads, qcol, kcol, vcol):
    width = n_heads * HEAD_DIM
    prev = lambda col: pl.BlockSpec((SWA_BLOCK, LANES), lambda i: (jnp.maximum(i - 1, 0), col))
    cur = lambda col: pl.BlockSpec((SWA_BLOCK, LANES), lambda i: (i, col))
    return [pl.BlockSpec((SWA_BLOCK, width), lambda i: (i, qcol)),
            prev(kcol), cur(kcol), prev(vcol), cur(vcol),
            _full((SWA_BLOCK, 2 * SWA_BLOCK)),
            pl.BlockSpec(memory_space=pltpu.SMEM), pl.BlockSpec(memory_space=pltpu.SMEM)]


def _swa_fwd(proj, bucket, rel_bias, sinks, n_heads, qcol, kcol, vcol):
    s = proj.shape[0]
    width = n_heads * HEAD_DIM
    n_groups = LANES // HEAD_DIM
    per_group = n_heads // n_groups

    def body(q_ref, kp_ref, kc_ref, vp_ref, vc_ref, bucket_ref, rb_ref, sk_ref, o_ref, bias_ref):
        i = pl.program_id(0)

        @pl.when(i == 0)
        def _():
            _swa_build_bias(bucket_ref, rb_ref, bias_ref, n_groups, per_group)

        lane = lax.broadcasted_iota(jnp.int32, (1, LANES), 1)
        first = lane < HEAD_DIM
        valid = _swa_valid(i, per_group)
        kcat = jnp.concatenate([kp_ref[...], kc_ref[...]], axis=0)
        vcat = jnp.concatenate([vp_ref[...], vc_ref[...]], axis=0)
        pieces = {}
        for g in range(n_groups):
            sel = first if g == 0 else jnp.logical_not(first)
            prob, _ = _swa_probs(_swa_stack(q_ref, g, per_group, sel), kcat, bias_ref[g], valid,
                                 _swa_sink_rows(sk_ref, g, per_group))
            _swa_unstack(_dot(prob.astype(BF16), vcat), g, per_group, pieces)
        for j in range(n_heads // 2):
            o_ref[:, j * LANES:(j + 1) * LANES] = jnp.where(first, pieces[2 * j], pieces[2 * j + 1])

    return pl.pallas_call(
        body, name="swa_fwd", grid=(s // SWA_BLOCK,),
        in_specs=_swa_specs(n_heads, qcol, kcol, vcol),
        out_specs=pl.BlockSpec((SWA_BLOCK, width), lambda i: (i, 0)),
        out_shape=jax.ShapeDtypeStruct((s, width), F32),
        scratch_shapes=[pltpu.VMEM((n_groups, per_group * SWA_BLOCK, 2 * SWA_BLOCK), F32)],
        compiler_params=_params(),
    )(proj, proj, proj, proj, proj, bucket, rel_bias, sinks)


def _rms_fwd(o, g):
    r = lax.rsqrt(jnp.mean(o * o, axis=-1, keepdims=True) + RMS_EPS)
    n = o * r
    return n, r, n * g


def _mix_ln1(sb_out, sw_out, x, g_in, b_in, sb_g, sw_g, w_out):
    s, d = x.shape
    wsb, wsw = sb_out.shape[1], sw_out.shape[1]
    tm = min(ROW_TILE, s)

    def body(sb_ref, sw_ref, x_ref, gi_ref, bi_ref, sbg_ref, swg_ref, w_ref, mg_ref, u_ref):
        _, _, m_sb = _rms_fwd(sb_ref[...], sbg_ref[...])
        _, _, m_sw = _rms_fwd(sw_ref[...], swg_ref[...])
        m_sb = m_sb.astype(BF16)
        m_sw = m_sw.astype(BF16)
        mg_ref[:, :wsb] = m_sb
        mg_ref[:, wsb:] = m_sw
        mix = _dot(m_sb, w_ref[:wsb, :]) + _dot(m_sw, w_ref[wsb:, :])
        xhat, _ = _ln_hat(x_ref[...])
        h0 = xhat * gi_ref[...] + bi_ref[...]
        u_ref[...] = ALPHA * h0 + mix

    row = lambda width: pl.BlockSpec((tm, width), lambda i: (i, 0))
    return pl.pallas_call(
        body, name="mix_ln1", grid=(s // tm,),
        in_specs=[row(wsb), row(wsw), row(d), _full((1, d)), _full((1, d)),
                  _full((1, wsb)), _full((1, wsw)), _full((wsb + wsw, d))],
        out_specs=[row(wsb + wsw), row(d)],
        out_shape=[jax.ShapeDtypeStruct((s, wsb + wsw), BF16), jax.ShapeDtypeStruct((s, d), F32)],
        compiler_params=_params(),
    )(sb_out, sw_out, x, g_in, b_in, sb_g, sw_g, w_out)


def _ffn_fwd(u1, g1, b1, w_gu, w_down, g2, b2, target):
    s, d = u1.shape
    dff = w_down.shape[0]
    tm = min(FFN_TILE, s)

    def body(u_ref, g1_ref, b1_ref, wgu_hbm, wd_hbm, g2_ref, b2_ref, t_ref, du_ref, st_ref, wgu_ref, wd_ref):
        @pl.when(pl.program_id(0) == 0)
        def _():
            pltpu.sync_copy(wgu_hbm, wgu_ref)
            pltpu.sync_copy(wd_hbm, wd_ref)
            st_ref[...] = jnp.zeros_like(st_ref)

        xhat, _ = _ln_hat(u_ref[...])
        h1 = xhat * g1_ref[...] + b1_ref[...]
        h1b = h1.astype(BF16)
        gate = _dot(h1b, wgu_ref[:, :dff])
        up = _dot(h1b, wgu_ref[:, dff:])
        act = gate * jax.nn.sigmoid(gate) * up
        u2 = ALPHA * h1 + _dot(act.astype(BF16), wd_ref[...])
        xhat2, r2 = _ln_hat(u2)
        diff = xhat2 * g2_ref[...] + b2_ref[...] - t_ref[...]
        dh2 = diff * (1.0 / d)
        st_ref[0:1, :] += _colsum(dh2 * xhat2)
        st_ref[1:2, :] += _colsum(dh2)
        st_ref[2:3, :] += jnp.broadcast_to(_colsum(_rowsum(diff * diff)) * (0.5 / d), (1, d))
        du_ref[...] = _ln_bwd(dh2 * g2_ref[...], xhat2, r2)

    row = pl.BlockSpec((tm, d), lambda i: (i, 0))
    hbm = pl.BlockSpec(memory_space=pl.ANY)
    return pl.pallas_call(
        body, name="ffn_fwd", grid=(s // tm,),
        in_specs=[row, _full((1, d)), _full((1, d)), hbm, hbm, _full((1, d)), _full((1, d)), row],
        out_specs=[row, _full((8, d))],
        out_shape=[jax.ShapeDtypeStruct((s, d), F32), jax.ShapeDtypeStruct((8, d), F32)],
        scratch_shapes=[pltpu.VMEM(w_gu.shape, BF16), pltpu.VMEM(w_down.shape, BF16)],
        compiler_params=_params(56),
    )(u1, g1, b1, w_gu, w_down, g2, b2, target)


def _ffn_bwd_act(u1, du2, g1, b1, w_gu, w_down_t):
    s, d = u1.shape
    dff = w_down_t.shape[1]
    tm = min(FFN_TILE, s)

    def body(u_ref, du2_ref, g1_ref, b1_ref, wgu_hbm, wdt_hbm, h1b_ref, act_ref, dgu_ref, wgu_ref, wdt_ref):
        @pl.when(pl.program_id(0) == 0)
        def _():
            pltpu.sync_copy(wgu_hbm, wgu_ref)
            pltpu.sync_copy(wdt_hbm, wdt_ref)

        xhat, _ = _ln_hat(u_ref[...])
        h1b = (xhat * g1_ref[...] + b1_ref[...]).astype(BF16)
        h1b_ref[...] = h1b
        gate = _dot(h1b, wgu_ref[:, :dff])
        up = _dot(h1b, wgu_ref[:, dff:])
        dact = _dot(du2_ref[...].astype(BF16), wdt_ref[...])
        sg = jax.nn.sigmoid(gate)
        silu = gate * sg
        act_ref[...] = (silu * up).astype(BF16)
        dgu_ref[:, :dff] = (dact * up * (sg * (1.0 + gate * (1.0 - sg)))).astype(BF16)
        dgu_ref[:, dff:] = (dact * silu).astype(BF16)

    row = lambda width: pl.BlockSpec((tm, width), lambda i: (i, 0))
    hbm = pl.BlockSpec(memory_space=pl.ANY)
    return pl.pallas_call(
        body, name="ffn_bwd_act", grid=(s // tm,),
        in_specs=[row(d), row(d), _full((1, d)), _full((1, d)), hbm, hbm],
        out_specs=[row(d), row(dff), row(2 * dff)],
        out_shape=[jax.ShapeDtypeStruct((s, d), BF16), jax.ShapeDtypeStruct((s, dff), BF16),
                   jax.ShapeDtypeStruct((s, 2 * dff), BF16)],
        scratch_shapes=[pltpu.VMEM(w_gu.shape, BF16), pltpu.VMEM(w_down_t.shape, BF16)],
        compiler_params=_params(56),
    )(u1, du2, g1, b1, w_gu, w_down_t)


def _ffn_bwd_in(dgu, w_gu_t, u1, du2, g1):
    s, d = u1.shape
    tm = min(FFN_TILE, s)

    def body(dgu_ref, wt_hbm, u_ref, du2_ref, g1_ref, du1_ref, st_ref, wt_ref):
        @pl.when(pl.program_id(0) == 0)
        def _():
            pltpu.sync_copy(wt_hbm, wt_ref)
            st_ref[...] = jnp.zeros_like(st_ref)

        dh1 = _dot(dgu_ref[...], wt_ref[...]) + ALPHA * du2_ref[...]
        xhat, r = _ln_hat(u_ref[...])
        st_ref[0:1, :] += _colsum(dh1 * xhat)
        st_ref[1:2, :] += _colsum(dh1)
        du1_ref[...] = _ln_bwd(dh1 * g1_ref[...], xhat, r)

    row = lambda width: pl.BlockSpec((tm, width), lambda i: (i, 0))
    return pl.pallas_call(
        body, name="ffn_bwd_in", grid=(s // tm,),
        in_specs=[row(dgu.shape[1]), pl.BlockSpec(memory_space=pl.ANY), row(d), row(d), _full((1, d))],
        out_specs=[row(d), _full((8, d))],
        out_shape=[jax.ShapeDtypeStruct((s, d), F32), jax.ShapeDtypeStruct((8, d), F32)],
        scratch_shapes=[pltpu.VMEM(w_gu_t.shape, BF16)],
        compiler_params=_params(56),
    )(dgu, w_gu_t, u1, du2, g1)


def _rms_bwd(dm, o, g):
    n, r, _ = _rms_fwd(o, g)
    dn = dm * g
    return r * (dn - n * jnp.mean(dn * n, axis=-1, keepdims=True)), _colsum(dm * n)


def _mix_bwd(du1, w_out, sb_out, sw_out, sb_g, sw_g):
    s, d = du1.shape
    wsb, wsw = sb_out.shape[1], sw_out.shape[1]
    tm = min(ROW_TILE, s)

    def body(du_ref, w_ref, sb_ref, sw_ref, sbg_ref, swg_ref, dsb_ref, dsw_ref, st_ref):
        i = pl.program_id(0)

        @pl.when(i == 0)
        def _():
            st_ref[...] = jnp.zeros_like(st_ref)

        dmerged = _dot_nt(du_ref[...].astype(BF16), w_ref[...])
        dsb, gsb = _rms_bwd(dmerged[:, :wsb], sb_ref[...], sbg_ref[...])
        dsw, gsw = _rms_bwd(dmerged[:, wsb:], sw_ref[...], swg_ref[...])
        dsb_ref[...] = dsb.astype(BF16)
        dsw_ref[...] = dsw.astype(BF16)
        st_ref[0:1, :wsb] += gsb
        st_ref[0:1, wsb:] += gsw

    row = lambda width: pl.BlockSpec((tm, width), lambda i: (i, 0))
    return pl.pallas_call(
        body, name="mix_bwd", grid=(s // tm,),
        in_specs=[row(d), _full((wsb + wsw, d)), row(wsb), row(wsw), _full((1, wsb)), _full((1, wsw))],
        out_specs=[row(wsb), row(wsw), _full((8, wsb + wsw))],
        out_shape=[jax.ShapeDtypeStruct((s, wsb), BF16), jax.ShapeDtypeStruct((s, wsw), BF16),
                   jax.ShapeDtypeStruct((8, wsb + wsw), F32)],
        compiler_params=_params(),
    )(du1, w_out, sb_out, sw_out, sb_g, sw_g)


def _sb_bwd(proj, dout, out, n_pairs, comm=None):
    s = proj.shape[0]
    t = min(SB_TILE, s)
    nq = s // t
    width = n_pairs * LANES

    def body(q_ref, k_ref, v_ref, do_ref, o_ref, dq_ref, dk_out, dv_out, dk_ref, dv_ref):
        i = pl.program_id(1)

        @pl.when(i == 0)
        def _():
            dk_ref[...] = jnp.zeros_like(dk_ref)
            dv_ref[...] = jnp.zeros_like(dv_ref)

        lane = lax.broadcasted_iota(jnp.int32, (1, LANES), 1)
        first = lane < HEAD_DIM
        do2 = do_ref[...]
        qs = _sb_stack_heads(q_ref[...] * SCALE, first)
        dos = _sb_stack_heads(do2, first)
        prod = do2.astype(F32) * o_ref[...]
        totals = jnp.concatenate([_rowsum(jnp.where(first, prod, 0.0)), _rowsum(jnp.where(first, 0.0, prod))], axis=0)
        upper, incl = _sb_triangles(t)

        def visit(offs, carry, mask):
            k_t = _sb_key_tiles(k_ref, offs, t)
            v_t = _sb_key_tiles(v_ref, offs, t)
            c_l, c_e, dq = carry
            lb, a, c_l = _sb_scores(qs, k_t, upper, c_l, mask, t)
            a_b = a.astype(BF16)
            d_e = _dot_nt(dos, v_t) * a_b.astype(F32)
            d_eb = d_e.astype(BF16)
            cols = [slice(j * t, (j + 1) * t) for j in range(len(offs))]
            suf_e, c_e = _sb_suffix([[d_eb[:, c]] for c in cols], [_rowsum(d_e[:, c]) for c in cols], incl, c_e)
            dz = d_e - jnp.exp(lb) * (d_e + (totals - suf_e))
            if mask is not None:
                dz = jnp.where(mask, dz, 0.0)
            dzb = dz.astype(BF16)
            dk_t = _dot_tn(dzb, qs)
            dv_t = _dot_tn(a_b, dos)
            for off, c in zip(offs, cols):
                dk_ref[pl.ds(off, t), :] += dk_t[c, :]
                dv_ref[pl.ds(off, t), :] += dv_t[c, :]
            return c_l, c_e, dq + _dot(dzb, k_t)

        init = (jnp.zeros((2 * t, 1), F32), jnp.zeros((2 * t, 1), F32), jnp.zeros((2 * t, LANES), F32))
        _, _, dq = _sb_walk(i, t, visit, init)
        dq_ref[...] = (jnp.where(first, dq[:t], dq[t:]) * SCALE).astype(BF16)

        @pl.when(i == nq - 1)
        def _():
            dk_out[...] = dk_ref[...].astype(BF16)
            dv_out[...] = dv_ref[...].astype(BF16)

    qblk = pl.BlockSpec((t, LANES), lambda h, i: (i, h))
    whole = pl.BlockSpec((s, LANES), lambda h, i: (0, h))
    return _call(
        body, "sb_bwd", (n_pairs, nq),
        in_specs=[qblk,
                  pl.BlockSpec((s, LANES), lambda h, i: (0, n_pairs + h)),
                  pl.BlockSpec((s, LANES), lambda h, i: (0, 2 * n_pairs + h)),
                  qblk, qblk],
        out_specs=[qblk, whole, whole],
        out_shape=[jax.ShapeDtypeStruct((s, width), BF16)] * 3,
        args=(proj, proj, proj, dout, out),
        scratch_shapes=[pltpu.VMEM((s, LANES), F32), pltpu.VMEM((s, LANES), F32)], comm=comm)


def _swa_bwd(proj, dout, bucket, rel_bias, sinks, n_heads, qcol, kcol, vcol, comm=None):
    s = proj.shape[0]
    width = n_heads * HEAD_DIM
    n_groups = LANES // HEAD_DIM
    per_group = n_heads // n_groups
    nb = s // SWA_BLOCK

    def body(q_ref, kp_ref, kc_ref, vp_ref, vc_ref, bucket_ref, rb_ref, sk_ref, do_ref,
             dq_ref, dk_out, dv_out, dsk_ref, drb_ref, bias_ref, dbias_ref, dk_ref, dv_ref):
        i = pl.program_id(0)

        @pl.when(i == 0)
        def _():
            _swa_build_bias(bucket_ref, rb_ref, bias_ref, n_groups, per_group)
            dbias_ref[...] = jnp.zeros_like(dbias_ref)
            dk_ref[...] = jnp.zeros_like(dk_ref)
            dv_ref[...] = jnp.zeros_like(dv_ref)
            dsk_ref[...] = jnp.zeros_like(dsk_ref)

        lane = lax.broadcasted_iota(jnp.int32, (1, LANES), 1)
        first = lane < HEAD_DIM
        valid = _swa_valid(i, per_group)
        kcat = jnp.concatenate([kp_ref[...], kc_ref[...]], axis=0)
        vcat = jnp.concatenate([vp_ref[...], vc_ref[...]], axis=0)
        dkcat = jnp.zeros((2 * SWA_BLOCK, LANES), F32)
        dvcat = jnp.zeros((2 * SWA_BLOCK, LANES), F32)
        pieces = {}
        for g in range(n_groups):
            sel = first if g == 0 else jnp.logical_not(first)
            q_g = _swa_stack(q_ref, g, per_group, sel)
            do_g = _swa_stack(do_ref, g, per_group, sel)
            prob, p_sink = _swa_probs(q_g, kcat, bias_ref[g], valid, _swa_sink_rows(sk_ref, g, per_group))
            dprob = _dot_nt(do_g, vcat)
            delta = _rowsum(prob * dprob)
            dlog = prob * (dprob - delta)
            sink_term = p_sink * delta
            for hh in range(per_group):
                h = g * per_group + hh
                tot = _colsum(sink_term[hh * SWA_BLOCK:(hh + 1) * SWA_BLOCK, :])
                dsk_ref[h:h + 1, :] += jnp.broadcast_to(-tot, (1, LANES))
            dbias_ref[g] += dlog
            dlb = (dlog * SCALE).astype(BF16)
            _swa_unstack(_dot(dlb, kcat), g, per_group, pieces)
            dkcat += _dot_tn(dlb, q_g)
            dvcat += _dot_tn(prob.astype(BF16), do_g)
        for j in range(n_heads // 2):
            dq_ref[:, j * LANES:(j + 1) * LANES] = jnp.where(first, pieces[2 * j], pieces[2 * j + 1]).astype(BF16)

        cur = pl.multiple_of(i * SWA_BLOCK, SWA_BLOCK)
        dk_ref[pl.ds(cur, SWA_BLOCK), :] += dkcat[SWA_BLOCK:, :]
        dv_ref[pl.ds(cur, SWA_BLOCK), :] += dvcat[SWA_BLOCK:, :]

        @pl.when(i > 0)
        def _():
            prv = pl.multiple_of((i - 1) * SWA_BLOCK, SWA_BLOCK)
            dk_ref[pl.ds(prv, SWA_BLOCK), :] += dkcat[:SWA_BLOCK, :]
            dv_ref[pl.ds(prv, SWA_BLOCK), :] += dvcat[:SWA_BLOCK, :]

        @pl.when(i == nb - 1)
        def _():
            bk = bucket_ref[...]
            rowi = lax.broadcasted_iota(jnp.int32, (REL_BUCKETS, LANES), 0)
            coli = lax.broadcasted_iota(jnp.int32, (REL_BUCKETS, LANES), 1)
            res = jnp.zeros((REL_BUCKETS, LANES), F32)
            for h in range(n_heads):
                g, hh = divmod(h, per_group)
                db = dbias_ref[g, hh * SWA_BLOCK:(hh + 1) * SWA_BLOCK, :]
                for b in range(REL_BUCKETS):
                    tot = _colsum(_rowsum(jnp.where(bk == b, db, 0.0)))
                    res = jnp.where((rowi == b) & (coli == h), tot, res)
            drb_ref[...] = res
            dk_out[...] = dk_ref[...].astype(BF16)
            dv_out[...] = dv_ref[...].astype(BF16)

    in_specs = _swa_specs(n_heads, qcol, kcol, vcol) + [pl.BlockSpec((SWA_BLOCK, width), lambda i: (i, 0))]
    return _call(
        body, "swa_bwd", (nb,),
        in_specs=in_specs,
        out_specs=[pl.BlockSpec((SWA_BLOCK, width), lambda i: (i, 0)),
                   _full((s, LANES)), _full((s, LANES)), _full((8, LANES)), _full((REL_BUCKETS, LANES))],
        out_shape=[jax.ShapeDtypeStruct((s, width), BF16), jax.ShapeDtypeStruct((s, LANES), BF16),
                   jax.ShapeDtypeStruct((s, LANES), BF16), jax.ShapeDtypeStruct((8, LANES), F32),
                   jax.ShapeDtypeStruct((REL_BUCKETS, LANES), F32)],
        args=(proj, proj, proj, proj, proj, bucket, rel_bias, sinks, dout),
        scratch_shapes=[pltpu.VMEM((n_groups, per_group * SWA_BLOCK, 2 * SWA_BLOCK), F32),
                        pltpu.VMEM((n_groups, per_group * SWA_BLOCK, 2 * SWA_BLOCK), F32),
                        pltpu.VMEM((s, LANES), F32), pltpu.VMEM((s, LANES), F32)],
        comm=comm)


def _proj_bwd(pieces, w_in, du1, x, g_in, comm=None):
    s, d = x.shape
    cols = w_in.shape[1]
    tm = min(ROW_TILE, s)
    n_p = len(pieces)

    def body(*refs):
        p_refs = refs[:n_p]
        w_ref, du_ref, x_ref, g_ref, dx_ref, st_ref = refs[n_p:]
        i = pl.program_id(0)

        @pl.when(i == 0)
        def _():
            st_ref[...] = jnp.zeros_like(st_ref)

        dproj = jnp.concatenate([p[...] for p in p_refs], axis=1)
        dh0 = _dot_nt(dproj, w_ref[...]) + ALPHA * du_ref[...]
        xhat, r = _ln_hat(x_ref[...])
        st_ref[0:1, :] += _colsum(dh0 * xhat)
        st_ref[1:2, :] += _colsum(dh0)
        dx_ref[...] = _ln_bwd(dh0 * g_ref[...], xhat, r)

    row = lambda width: pl.BlockSpec((tm, width), lambda i: (i, 0))
    return _call(
        body, "proj_bwd", (s // tm,),
        in_specs=[row(p.shape[1]) for p in pieces] + [_full((d, cols)), row(d), row(d), _full((1, d))],
        out_specs=[row(d), _full((8, d))],
        out_shape=[jax.ShapeDtypeStruct((s, d), F32), jax.ShapeDtypeStruct((8, d), F32)],
        args=(*pieces, w_in, du1, x, g_in), comm=comm)


def _wgrad(name, a, pieces, tm, tn, dev_cols=None):
    s, m = a.shape
    n = sum(p.shape[1] for p in pieces)
    n_p = len(pieces)
    assert n_p == 1 or tn == n
    ts = min(WGRAD_TOKENS if pieces[0].dtype == BF16 and n_p == 1 else WGRAD_TOKENS // 2, s)
    n_k = s // ts

    def body(*refs):
        a_ref, p_refs, o_ref, acc_ref = refs[0], refs[1:1 + n_p], refs[1 + n_p], refs[2 + n_p]
        k = pl.program_id(2)

        @pl.when(k == 0)
        def _():
            acc_ref[...] = jnp.zeros_like(acc_ref)

        b = p_refs[0][...] if n_p == 1 else jnp.concatenate([p[...] for p in p_refs], axis=1)
        acc_ref[...] += _dot_tn(a_ref[...].astype(BF16), b.astype(BF16))

        @pl.when(k == n_k - 1)
        def _():
            if dev_cols is None:
                o_ref[...] = acc_ref[...].astype(BF16)
            else:
                for dev in range(tn // dev_cols):
                    o_ref[dev] = acc_ref[:, dev * dev_cols:(dev + 1) * dev_cols].astype(BF16)

    if dev_cols is None:
        out_spec = pl.BlockSpec((tm, tn), lambda i, j, k: (i, j))
        out_shape = jax.ShapeDtypeStruct((m, n), BF16)
    else:
        out_spec = pl.BlockSpec((tn // dev_cols, tm, dev_cols), lambda i, j, k: (j, i, 0))
        out_shape = jax.ShapeDtypeStruct((n // dev_cols, m, dev_cols), BF16)
    piece_spec = lambda p: pl.BlockSpec((ts, tn if n_p == 1 else p.shape[1]), lambda i, j, k: (k, j))
    return pl.pallas_call(
        body, name=name, grid=(m // tm, n // tn, n_k),
        in_specs=[pl.BlockSpec((ts, tm), lambda i, j, k: (k, i))] + [piece_spec(p) for p in pieces],
        out_specs=out_spec, out_shape=out_shape,
        scratch_shapes=[pltpu.VMEM((tm, tn), F32)],
        compiler_params=_params(),
    )(a, *pieces)


def _adamw_math(w, g, m, v):
    m = ADAM_B1 * m + (1.0 - ADAM_B1) * g
    v = ADAM_B2 * v + (1.0 - ADAM_B2) * (g * g)
    m_hat = m / (1.0 - ADAM_B1 ** ADAM_STEP)
    v_hat = v / (1.0 - ADAM_B2 ** ADAM_STEP)
    delta = -ADAM_LR * (m_hat / (jnp.sqrt(v_hat) + ADAM_EPS) + ADAM_WD * w)
    return delta, m, v


def _adamw(name, landed, w, m, v, tr):
    rows, cols = w.shape

    def body(l_ref, w_ref, m_ref, v_ref, g_ref, d_ref, nm_ref, nv_ref):
        g = l_ref[0].astype(F32)
        for src in range(1, N_DEV):
            g = g + l_ref[src].astype(F32)
        delta, nm, nv = _adamw_math(w_ref[...], g, m_ref[...], v_ref[...])
        g_ref[...] = g
        d_ref[...] = delta
        nm_ref[...] = nm
        nv_ref[...] = nv

    blk = pl.BlockSpec((tr, cols), lambda i: (i, 0))
    shape = jax.ShapeDtypeStruct((rows, cols), F32)
    return pl.pallas_call(
        body, name=name, grid=(rows // tr,),
        in_specs=[pl.BlockSpec((N_DEV, tr, cols), lambda i: (0, i, 0)), blk, blk, blk],
        out_specs=[blk, blk, blk, blk],
        out_shape=[shape, shape, shape, shape],
        compiler_params=_params(),
    )(landed, w, m, v)


def _pack(d, ln_in_g, ln_in_b, ln1_g, ln1_b, ln2_g, ln2_b, sb_g, sw_g, rel_bias, sinks, extra=None):
    tail = [rel_bias.reshape(-1), sinks.reshape(-1)]
    if extra is not None:
        tail.append(extra.reshape(-1))
    tail = jnp.concatenate(tail)
    tail = jnp.concatenate([tail, jnp.zeros((d - tail.shape[0],), F32)])
    rows = [ln_in_g.reshape(-1), ln_in_b.reshape(-1), ln1_g.reshape(-1), ln1_b.reshape(-1),
            ln2_g.reshape(-1), ln2_b.reshape(-1),
            jnp.concatenate([sb_g.reshape(-1), sw_g.reshape(-1)]), tail]
    return jnp.stack(rows)


def _unpack(p, wsb, n_rb, n_sk):
    return [p[0], p[1], p[6, :wsb][None], p[6, wsb:][None], p[7, n_rb:n_rb + n_sk][None],
            p[7, :n_rb].reshape(REL_BUCKETS, -1), p[2][None], p[3][None], p[4][None], p[5][None]]


def kernel(x, ln_in_g, ln_in_b, w_in, sb_norm_g, swa_norm_g, sinks, rel_bias, w_out, ln1_g, ln1_b, w_gate_up, w_down, ln2_g, ln2_b, loss_target, m_ln_in_g, m_ln_in_b, m_w_in, m_sb_norm_g, m_swa_norm_g, m_sinks, m_rel_bias, m_w_out, m_ln1_g, m_ln1_b, m_w_gate_up, m_w_down, m_ln2_g, m_ln2_b, v_ln_in_g, v_ln_in_b, v_w_in, v_sb_norm_g, v_swa_norm_g, v_sinks, v_rel_bias, v_w_out, v_ln1_g, v_ln1_b, v_w_gate_up, v_w_down, v_ln2_g, v_ln2_b):
    x2 = x[0]
    tgt = loss_target[0]
    s, d = x2.shape
    wsb = sb_norm_g.shape[-1]
    wsw = swa_norm_g.shape[-1]
    n_sw_heads = sinks.shape[-1]
    n_pairs = wsb // LANES
    dff = w_down.shape[1] * N_DEV
    assert wsb % LANES == 0 and wsw % LANES == 0 and n_sw_heads * HEAD_DIM == wsw
    assert 3 * wsb % wsw == 0 and dff % LANES == 0 and s % SWA_BLOCK == 0
    qcol = 3 * wsb // wsw
    kcol = (3 * wsb + wsw) // LANES
    vcol = kcol + 1
    assert w_in.shape[-1] * N_DEV == (vcol + 1) * LANES

    big_w = [w_in[0], w_out[0], w_gate_up[0], w_down[0]]
    big_m = [m_w_in[0], m_w_out[0], m_w_gate_up[0], m_w_down[0]]
    big_v = [v_w_in[0], v_w_out[0], v_w_gate_up[0], v_w_down[0]]

    cat_cols = lambda g: jnp.transpose(g, (1, 0, 2)).reshape(g.shape[1], N_DEV * g.shape[2])
    cat_rows = lambda g: g.reshape(N_DEV * g.shape[1], g.shape[2])
    shards = [w.astype(BF16) for w in big_w]
    w_in_f = cat_cols(_exchange("w_in_allgather", shards[:1], ["gather"])[0])

    vec = lambda a: a.reshape(1, -1)
    g_in, b_in = vec(ln_in_g), vec(ln_in_b)
    bucket = jnp.asarray(_swa_bucket_table())

    h0b, proj = _ln_proj(x2, g_in, b_in, w_in_f)
    sb_out, gathered = _sb_fwd(proj, n_pairs, comm=(shards[1:], ["gather"] * 3))
    w_out_f, w_gu_f, w_down_f = cat_rows(gathered[0]), cat_cols(gathered[1]), cat_rows(gathered[2])
    sw_out = _swa_fwd(proj, bucket, rel_bias, sinks, n_sw_heads, qcol, kcol, vcol)
    merged, u1 = _mix_ln1(sb_out, sw_out, x2, g_in, b_in, sb_norm_g, swa_norm_g, w_out_f)
    du2, st_ln2 = _ffn_fwd(u1, ln1_g, ln1_b, w_gu_f, w_down_f, ln2_g, ln2_b, tgt)

    split_rows = lambda g: g.reshape(N_DEV, g.shape[0] // N_DEV, g.shape[1])
    tile_m = min(512, d)
    h1b, act, dgu = _ffn_bwd_act(u1, du2, ln1_g, ln1_b, w_gu_f, jnp.transpose(w_down_f))
    du1, st_ln1 = _ffn_bwd_in(dgu, jnp.transpose(w_gu_f), u1, du2, ln1_g)
    gw_gu = _wgrad("wgrad_gate_up", h1b, [dgu], tile_m, dff // 2, dev_cols=w_gate_up.shape[-1])
    gw_down = _wgrad("wgrad_down", act, [du2], dff // 2, d)
    gw_out = _wgrad("wgrad_out", merged, [du1], tile_m, d)
    dsb, dsw, st_rms = _mix_bwd(du1, w_out_f, sb_out, sw_out, sb_norm_g, swa_norm_g)
    (dq_sb, dk_sb, dv_sb), (land_gu, land_out) = _sb_bwd(
        proj, dsb, sb_out, n_pairs, comm=([gw_gu, split_rows(gw_out)], ["scatter"] * 2))
    (dq_sw, dk_sw, dv_sw, st_sink, st_rb), (land_down,) = _swa_bwd(
        proj, dsw, bucket, rel_bias, sinks, n_sw_heads, qcol, kcol, vcol,
        comm=([split_rows(gw_down)], ["scatter"]))
    pieces = [dq_sb, dk_sb, dv_sb, dq_sw, dk_sw, dv_sw]
    gw_in = _wgrad("wgrad_in", h0b, pieces, tile_m, proj.shape[1], dev_cols=w_in.shape[-1])
    (grad_x, st_in), (land_in,) = _proj_bwd(pieces, w_in_f, du1, x2, g_in, comm=([gw_in], ["scatter"]))

    n_rb = rel_bias.size
    small = _pack(d, st_in[0], st_in[1], st_ln1[0], st_ln1[1], st_ln2[0], st_ln2[1],
                  st_rms[0, :wsb], st_rms[0, wsb:], st_rb[:, :n_sw_heads], st_sink[:n_sw_heads, 0],
                  extra=st_ln2[2, 0:1])
    land_small = _exchange("small_grads_allgather", [small], ["gather"])[0]
    landed = [land_in, land_out, land_gu, land_down, land_small]

    big = []
    for name, land, w, m, v in zip(["adamw_in", "adamw_out", "adamw_gate_up", "adamw_down"], landed[:4], big_w, big_m, big_v):
        rows = w.shape[0]
        tr = 128 if rows % 128 == 0 else rows // 2
        big.append(_adamw(name, land, w, m, v, tr))

    small_w = _pack(d, ln_in_g, ln_in_b, ln1_g, ln1_b, ln2_g, ln2_b, sb_norm_g, swa_norm_g, rel_bias, sinks)
    small_m = _pack(d, m_ln_in_g, m_ln_in_b, m_ln1_g, m_ln1_b, m_ln2_g, m_ln2_b, m_sb_norm_g, m_swa_norm_g, m_rel_bias, m_sinks)
    small_v = _pack(d, v_ln_in_g, v_ln_in_b, v_ln1_g, v_ln1_b, v_ln2_g, v_ln2_b, v_sb_norm_g, v_swa_norm_g, v_rel_bias, v_sinks)
    sg, sd, sm, sv = _adamw("adamw_small", landed[4], small_w, small_m, small_v, 8)
    n_sk = sinks.size
    loss = sg[7, n_rb + n_sk]

    def leaves(idx):
        sm_l = _unpack([sg, sd, sm, sv][idx], wsb, n_rb, n_sk)
        bg = [b[idx][None] for b in big]
        return [sm_l[0], sm_l[1], bg[0], sm_l[2], sm_l[3], sm_l[4], sm_l[5], bg[1], sm_l[6], sm_l[7], bg[2], bg[3], sm_l[8], sm_l[9]]

    return (loss, grad_x[None], *leaves(0), *leaves(1), *leaves(2), *leaves(3))
```

```python
import functools
import math

import numpy as np
import jax
import jax.numpy as jnp
from jax import lax
from jax.experimental import pallas as pl
from jax.experimental.pallas import tpu as pltpu

F32 = jnp.float32
BF16 = jnp.bfloat16
MESH = pl.DeviceIdType.MESH

N_DEV = 8
LANES = 128
HEAD_DIM = 64
SCALE = HEAD_DIM ** -0.5
SWA_BLOCK = 128
REL_BUCKETS = 32
REL_MAX_DIST = 128
ALPHA = 2.0 ** 0.25
LN_EPS = 1e-5
RMS_EPS = 1e-6
ADAM_LR = 0.001
ADAM_B1 = 0.9
ADAM_B2 = 0.999
ADAM_EPS = 1e-08
ADAM_WD = 0.01
ADAM_STEP = 10

ROW_TILE = 512
SB_TILE = 256
FFN_TILE = 256
WGRAD_TOKENS = 2048
SB_UNDERFLOW = -110.0
MIB = 1024 * 1024


def _params(vmem_mib=48):
    return pltpu.CompilerParams(vmem_limit_bytes=vmem_mib * MIB)


def _dot(a, b):
    return jnp.dot(a, b, preferred_element_type=F32)


def _dot_nt(a, b):
    return lax.dot_general(a, b, (((1,), (1,)), ((), ())), preferred_element_type=F32)


def _dot_tn(a, b):
    return lax.dot_general(a, b, (((0,), (0,)), ((), ())), preferred_element_type=F32)


def _ln_hat(x):
    mu = jnp.mean(x, axis=-1, keepdims=True)
    xc = x - mu
    var = jnp.mean(xc * xc, axis=-1, keepdims=True)
    r = lax.rsqrt(var + LN_EPS)
    return xc * r, r


def _ln_bwd(dxhat, xhat, r):
    return r * (dxhat - jnp.mean(dxhat, axis=-1, keepdims=True)
                - xhat * jnp.mean(dxhat * xhat, axis=-1, keepdims=True))


def _colsum(a):
    return jnp.sum(a, axis=0, keepdims=True)


def _rowsum(a):
    return jnp.sum(a, axis=1, keepdims=True)


def _full(shape):
    return pl.BlockSpec(shape, lambda *_: (0,) * len(shape))


def _comm_out_shapes(arrays, kinds):
    shapes = []
    for a, kind in zip(arrays, kinds):
        blk = a.shape if kind == "gather" else a.shape[1:]
        shapes.append(jax.ShapeDtypeStruct((N_DEV,) + tuple(blk), a.dtype))
    return shapes


def _comm_sems(n):
    return [pltpu.SemaphoreType.DMA((n, N_DEV - 1)), pltpu.SemaphoreType.DMA((n, N_DEV - 1)),
            pltpu.SemaphoreType.DMA((n,))]


def _comm_copies(ins, outs, kinds, send_sems, recv_sems, local_sems):
    x, y, c = lax.axis_index("x"), lax.axis_index("y"), lax.axis_index("c")
    me = 4 * x + 2 * y + c

    def src_for(t, dev_lin):
        return ins[t] if kinds[t] == "gather" else ins[t].at[dev_lin]

    local = [pltpu.make_async_copy(src_for(t, me), outs[t].at[me], local_sems.at[t]) for t in range(len(kinds))]
    sends, arrivals = [], []
    for k in range(1, N_DEV):
        px = 1 - x if (k >> 2) & 1 else x
        py = 1 - y if (k >> 1) & 1 else y
        pc = 1 - c if k & 1 else c
        peer_lin = 4 * px + 2 * py + pc
        for t in range(len(kinds)):
            sems = dict(send_sem=send_sems.at[t, k - 1], recv_sem=recv_sems.at[t, k - 1],
                        device_id=(px, py, pc), device_id_type=MESH)
            sends.append(pltpu.make_async_remote_copy(src_ref=src_for(t, peer_lin), dst_ref=outs[t].at[me], **sems))
            arrivals.append(pltpu.make_async_remote_copy(src_ref=src_for(t, peer_lin), dst_ref=outs[t].at[peer_lin], **sems))
    return local, sends, arrivals


def _comm_start(ins, outs, kinds, sems):
    local, sends, _ = _comm_copies(ins, outs, kinds, *sems)
    for cp in local + sends:
        cp.start()


def _comm_finish(ins, outs, kinds, sems):
    local, sends, arrivals = _comm_copies(ins, outs, kinds, *sems)
    for cp in arrivals:
        cp.wait_recv()
    for cp in sends:
        cp.wait_send()
    for cp in local:
        cp.wait()


def _exchange(name, arrays, kinds):
    n = len(arrays)

    def body(*refs):
        ins, outs, sems = refs[:n], refs[n:2 * n], refs[2 * n:]
        _comm_start(ins, outs, kinds, sems)
        _comm_finish(ins, outs, kinds, sems)

    any_spec = pl.BlockSpec(memory_space=pl.ANY)
    return pl.pallas_call(
        body, name=name, out_shape=_comm_out_shapes(arrays, kinds),
        in_specs=[any_spec] * n, out_specs=[any_spec] * n,
        scratch_shapes=_comm_sems(n),
    )(*arrays)


def _call(body, name, grid, in_specs, out_specs, out_shape, args, scratch_shapes=(), comm=None):
    if comm is None:
        outs = pl.pallas_call(body, name=name, grid=grid, in_specs=in_specs, out_specs=out_specs,
                              out_shape=out_shape, scratch_shapes=list(scratch_shapes),
                              compiler_params=_params())(*args)
        return outs, []
    arrays, kinds = comm
    n, n_in, n_out, n_scr = len(arrays), len(in_specs), len(out_specs), len(scratch_shapes)

    def fused(*refs):
        c_in, x_in = refs[:n_in], refs[n_in:n_in + n]
        c_out = refs[n_in + n:n_in + n + n_out]
        x_out = refs[n_in + n + n_out:n_in + 2 * n + n_out]
        rest = refs[n_in + 2 * n + n_out:]
        c_scr, sems = rest[:n_scr], rest[n_scr:]
        ids = [pl.program_id(a) for a in range(len(grid))]
        is_first = functools.reduce(jnp.logical_and, [i == 0 for i in ids])
        is_last = functools.reduce(jnp.logical_and, [i == g - 1 for i, g in zip(ids, grid)])

        @pl.when(is_first)
        def _():
            _comm_start(x_in, x_out, kinds, sems)

        body(*c_in, *c_out, *c_scr)

        @pl.when(is_last)
        def _():
            _comm_finish(x_in, x_out, kinds, sems)

    any_spec = pl.BlockSpec(memory_space=pl.ANY)
    outs = pl.pallas_call(
        fused, name=name, grid=grid,
        in_specs=list(in_specs) + [any_spec] * n, out_specs=list(out_specs) + [any_spec] * n,
        out_shape=list(out_shape) + _comm_out_shapes(arrays, kinds),
        scratch_shapes=list(scratch_shapes) + _comm_sems(n),
        compiler_params=_params())(*args, *arrays)
    return outs[:n_out], outs[n_out:]


def _ln_proj(x, g, b, w_in):
    s, d = x.shape
    cols = w_in.shape[1]
    tm = min(ROW_TILE, s)

    def body(x_ref, g_ref, b_ref, w_ref, h_ref, p_ref):
        xhat, _ = _ln_hat(x_ref[...])
        h = (xhat * g_ref[...] + b_ref[...]).astype(BF16)
        h_ref[...] = h
        p_ref[...] = _dot(h, w_ref[...]).astype(BF16)

    row = lambda width: pl.BlockSpec((tm, width), lambda i: (i, 0))
    return pl.pallas_call(
        body, name="ln_proj", grid=(s // tm,),
        in_specs=[row(d), _full((1, d)), _full((1, d)), _full((d, cols))],
        out_specs=[row(d), row(cols)],
        out_shape=[jax.ShapeDtypeStruct((s, d), BF16), jax.ShapeDtypeStruct((s, cols), BF16)],
        compiler_params=_params(),
    )(x, g, b, w_in)


def _sb_triangles(t):
    row = lax.broadcasted_iota(jnp.int32, (t, t), 0)
    col = lax.broadcasted_iota(jnp.int32, (t, t), 1)
    return (row > col).astype(BF16), (row >= col).astype(BF16)


def _sb_first_mask(t, has_prev):
    qrow = lax.broadcasted_iota(jnp.int32, (2 * t, 2 * t), 0) & (t - 1)
    col = lax.broadcasted_iota(jnp.int32, (2 * t, 2 * t), 1)
    return ((col < t) & has_prev) | ((col >= t) & (col - t < qrow))


def _sb_stack_heads(x2, first):
    zero = jnp.zeros_like(x2)
    return jnp.concatenate([jnp.where(first, x2, zero), jnp.where(first, zero, x2)], axis=0)


def _sb_key_tiles(ref, offs, t):
    tiles = [ref[pl.ds(off, t), :] for off in offs]
    return tiles[0] if len(tiles) == 1 else jnp.concatenate(tiles, axis=0)


def _sb_suffix(terms, row_sums, tri, carry):
    out = [None] * len(terms)
    for j in reversed(range(len(terms))):
        suf = carry
        for op in terms[j]:
            suf = suf + _dot(op, tri)
        out[j] = suf
        carry = carry + row_sums[j]
    return (out[0] if len(out) == 1 else jnp.concatenate(out, axis=1)), carry


def _sb_scores(qh, k_t, upper, carry_l, mask, t):
    z = _dot_nt(qh, k_t)
    sp = jnp.log(1.0 + jnp.exp(-jnp.abs(z)))
    neg = jnp.minimum(z, 0.0)
    lb = neg - sp
    l1 = (neg - z) - sp
    if mask is not None:
        l1 = jnp.where(mask, l1, 0.0)
    hi = l1.astype(BF16)
    lo = (l1 - hi.astype(F32)).astype(BF16)
    cols = [slice(j * t, (j + 1) * t) for j in range(z.shape[1] // t)]
    suf, carry_l = _sb_suffix([[hi[:, c], lo[:, c]] for c in cols], [_rowsum(l1[:, c]) for c in cols], upper, carry_l)
    a = jnp.exp(lb + suf)
    if mask is not None:
        a = jnp.where(mask, a, 0.0)
    return lb, a, carry_l


def _sb_walk(i, t, visit, init):
    def alive(carry):
        return jnp.max(carry[0]) > SB_UNDERFLOW

    prev = pl.multiple_of(jnp.maximum(i - 1, 0) * t, t)
    carry = visit((prev, pl.multiple_of(i * t, t)), init, _sb_first_mask(t, i > 0))

    def cond(state):
        j, go, _ = state
        return (j < i - 1) & go

    def body(state):
        j, _, carry = state
        carry = visit((pl.multiple_of((i - 2 - j) * t, t),), carry, None)
        return j + 1, alive(carry), carry

    return lax.while_loop(cond, body, (jnp.int32(0), alive(carry), carry))[2]


def _sb_fwd(proj, n_pairs, comm=None):
    s = proj.shape[0]
    t = min(SB_TILE, s)
    nq = s // t

    def body(q_ref, k_ref, v_ref, o_ref):
        i = pl.program_id(1)
        lane = lax.broadcasted_iota(jnp.int32, (1, LANES), 1)
        first = lane < HEAD_DIM
        qs = _sb_stack_heads(q_ref[...] * SCALE, first)
        upper, _ = _sb_triangles(t)

        def visit(offs, carry, mask):
            c_l, acc = carry
            _, a, c_l = _sb_scores(qs, _sb_key_tiles(k_ref, offs, t), upper, c_l, mask, t)
            return c_l, acc + _dot(a.astype(BF16), _sb_key_tiles(v_ref, offs, t))

        init = (jnp.zeros((2 * t, 1), F32), jnp.zeros((2 * t, LANES), F32))
        _, acc = _sb_walk(i, t, visit, init)
        o_ref[...] = jnp.where(first, acc[:t], acc[t:])

    outs, landed = _call(
        body, "sb_fwd", (n_pairs, nq),
        in_specs=[pl.BlockSpec((t, LANES), lambda h, i: (i, h)),
                  pl.BlockSpec((s, LANES), lambda h, i: (0, n_pairs + h)),
                  pl.BlockSpec((s, LANES), lambda h, i: (0, 2 * n_pairs + h))],
        out_specs=[pl.BlockSpec((t, LANES), lambda h, i: (i, h))],
        out_shape=[jax.ShapeDtypeStruct((s, n_pairs * LANES), F32)],
        args=(proj, proj, proj), comm=comm)
    return outs[0], landed


def _swa_bucket_table():
    qi = np.arange(SWA_BLOCK)[:, None]
    cj = np.arange(2 * SWA_BLOCK)[None, :]
    dist = qi + SWA_BLOCK - cj
    exact = REL_BUCKETS // 2
    d = np.maximum(dist, 0)
    d_f = np.maximum(d, 1).astype(np.float32)
    large = exact + (np.log(d_f / np.float32(exact)) / np.float32(math.log(REL_MAX_DIST / exact))
                     * np.float32(REL_BUCKETS - exact)).astype(np.int32)
    large = np.minimum(large, REL_BUCKETS - 1)
    return np.where(d < exact, d, large).astype(np.int32)


def _swa_build_bias(bucket_ref, rb_ref, bias_ref, n_groups, per_group):
    bk = bucket_ref[...]
    for g in range(n_groups):
        for hh in range(per_group):
            acc = jnp.zeros(bk.shape, F32)
            for b in range(REL_BUCKETS):
                acc = jnp.where(bk == b, rb_ref[b, g * per_group + hh], acc)
            bias_ref[g, hh * SWA_BLOCK:(hh + 1) * SWA_BLOCK, :] = acc


def _swa_valid(i, reps):
    shape = (reps * SWA_BLOCK, 2 * SWA_BLOCK)
    row = lax.broadcasted_iota(jnp.int32, shape, 0) & (SWA_BLOCK - 1)
    col = lax.broadcasted_iota(jnp.int32, shape, 1)
    dist = row + SWA_BLOCK - col
    return (dist >= 0) & (dist < SWA_BLOCK) & ((col >= SWA_BLOCK) | (i > 0))


def _swa_place(blk, h, group, sel):
    if (h % 2) != group:
        blk = pltpu.roll(blk.astype(F32), HEAD_DIM, axis=1).astype(BF16)
    return jnp.where(sel, blk, jnp.zeros_like(blk))


def _swa_stack(ref, group, per_group, sel):
    parts = []
    for hh in range(per_group):
        h = group * per_group + hh
        parts.append(_swa_place(ref[:, (h // 2) * LANES:(h // 2 + 1) * LANES], h, group, sel))
    return jnp.concatenate(parts, axis=0)


def _swa_unstack(stacked, group, per_group, pieces):
    for hh in range(per_group):
        h = group * per_group + hh
        piece = stacked[hh * SWA_BLOCK:(hh + 1) * SWA_BLOCK, :]
        pieces[h] = pltpu.roll(piece, HEAD_DIM, axis=1) if (h % 2) != group else piece


def _swa_sink_rows(sk_ref, group, per_group):
    rowh = lax.broadcasted_iota(jnp.int32, (per_group * SWA_BLOCK, 1), 0) // SWA_BLOCK
    sink = jnp.zeros((per_group * SWA_BLOCK, 1), F32) + sk_ref[0, group * per_group]
    for hh in range(1, per_group):
        sink = jnp.where(rowh == hh, sk_ref[0, group * per_group + hh], sink)
    return sink


def _swa_probs(q_pos, kcat, bias_h, valid, sink):
    logits = _dot_nt(q_pos, kcat) * SCALE + bias_h
    logits = jnp.where(valid, logits, -jnp.inf)
    m = jnp.maximum(jnp.max(logits, axis=1, keepdims=True), sink)
    p = jnp.exp(logits - m)
    es = jnp.exp(sink - m)
    denom = _rowsum(p) + es
    return p / denom, es / denom


def _swa_specs(n_heads, qcol, kcol, vcol):
    width = n_heads * HEAD_DIM
    prev = lambda col: pl.BlockSpec((SWA_BLOCK, LANES), lambda i: (jnp.maximum(i - 1, 0), col))
    cur = lambda col: pl.BlockSpec((SWA_BLOCK, LANES), lambda i: (i, col))
    return [pl.BlockSpec((SWA_BLOCK, width), lambda i: (i, qcol)),
            prev(kcol), cur(kcol), prev(vcol), cur(vcol),
            _full((SWA_BLOCK, 2 * SWA_BLOCK)),
            pl.BlockSpec(memory_space=pltpu.SMEM), pl.BlockSpec(memory_space=pltpu.SMEM)]


def _swa_fwd(proj, bucket, rel_bias, sinks, n_heads, qcol, kcol, vcol):
    s = proj.shape[0]
    width = n_heads * HEAD_DIM
    n_groups = LANES // HEAD_DIM
    per_group = n_heads // n_groups

    def body(q_ref, kp_ref, kc_ref, vp_ref, vc_ref, bucket_ref, rb_ref, sk_ref, o_ref, bias_ref):
        i = pl.program_id(0)

        @pl.when(i == 0)
        def _():
            _swa_build_bias(bucket_ref, rb_ref, bias_ref, n_groups, per_group)

        lane = lax.broadcasted_iota(jnp.int32, (1, LANES), 1)
        first = lane < HEAD_DIM
        valid = _swa_valid(i, per_group)
        kcat = jnp.concatenate([kp_ref[...], kc_ref[...]], axis=0)
        vcat = jnp.concatenate([vp_ref[...], vc_ref[...]], axis=0)
        pieces = {}
        for g in range(n_groups):
            sel = first if g == 0 else jnp.logical_not(first)
            prob, _ = _swa_probs(_swa_stack(q_ref, g, per_group, sel), kcat, bias_ref[g], valid,
                                 _swa_sink_rows(sk_ref, g, per_group))
            _swa_unstack(_dot(prob.astype(BF16), vcat), g, per_group, pieces)
        for j in range(n_heads // 2):
            o_ref[:, j * LANES:(j + 1) * LANES] = jnp.where(first, pieces[2 * j], pieces[2 * j + 1])

    return pl.pallas_call(
        body, name="swa_fwd", grid=(s // SWA_BLOCK,),
        in_specs=_swa_specs(n_heads, qcol, kcol, vcol),
        out_specs=pl.BlockSpec((SWA_BLOCK, width), lambda i: (i, 0)),
        out_shape=jax.ShapeDtypeStruct((s, width), F32),
        scratch_shapes=[pltpu.VMEM((n_groups, per_group * SWA_BLOCK, 2 * SWA_BLOCK), F32)],
        compiler_params=_params(),
    )(proj, proj, proj, proj, proj, bucket, rel_bias, sinks)


def _rms_fwd(o, g):
    r = lax.rsqrt(jnp.mean(o * o, axis=-1, keepdims=True) + RMS_EPS)
    n = o * r
    return n, r, n * g


def _mix_ln1(sb_out, sw_out, x, g_in, b_in, sb_g, sw_g, w_out):
    s, d = x.shape
    wsb, wsw = sb_out.shape[1], sw_out.shape[1]
    tm = min(ROW_TILE, s)

    def body(sb_ref, sw_ref, x_ref, gi_ref, bi_ref, sbg_ref, swg_ref, w_ref, mg_ref, u_ref):
        _, _, m_sb = _rms_fwd(sb_ref[...], sbg_ref[...])
        _, _, m_sw = _rms_fwd(sw_ref[...], swg_ref[...])
        m_sb = m_sb.astype(BF16)
        m_sw = m_sw.astype(BF16)
        mg_ref[:, :wsb] = m_sb
        mg_ref[:, wsb:] = m_sw
        mix = _dot(m_sb, w_ref[:wsb, :]) + _dot(m_sw, w_ref[wsb:, :])
        xhat, _ = _ln_hat(x_ref[...])
        h0 = xhat * gi_ref[...] + bi_ref[...]
        u_ref[...] = ALPHA * h0 + mix

    row = lambda width: pl.BlockSpec((tm, width), lambda i: (i, 0))
    return pl.pallas_call(
        body, name="mix_ln1", grid=(s // tm,),
        in_specs=[row(wsb), row(wsw), row(d), _full((1, d)), _full((1, d)),
                  _full((1, wsb)), _full((1, wsw)), _full((wsb + wsw, d))],
        out_specs=[row(wsb + wsw), row(d)],
        out_shape=[jax.ShapeDtypeStruct((s, wsb + wsw), BF16), jax.ShapeDtypeStruct((s, d), F32)],
        compiler_params=_params(),
    )(sb_out, sw_out, x, g_in, b_in, sb_g, sw_g, w_out)


def _ffn_fwd(u1, g1, b1, w_gu, w_down, g2, b2, target):
    s, d = u1.shape
    dff = w_down.shape[0]
    tm = min(FFN_TILE, s)

    def body(u_ref, g1_ref, b1_ref, wgu_hbm, wd_hbm, g2_ref, b2_ref, t_ref, du_ref, st_ref, wgu_ref, wd_ref):
        @pl.when(pl.program_id(0) == 0)
        def _():
            pltpu.sync_copy(wgu_hbm, wgu_ref)
            pltpu.sync_copy(wd_hbm, wd_ref)
            st_ref[...] = jnp.zeros_like(st_ref)

        xhat, _ = _ln_hat(u_ref[...])
        h1 = xhat * g1_ref[...] + b1_ref[...]
        h1b = h1.astype(BF16)
        gate = _dot(h1b, wgu_ref[:, :dff])
        up = _dot(h1b, wgu_ref[:, dff:])
        act = gate * jax.nn.sigmoid(gate) * up
        u2 = ALPHA * h1 + _dot(act.astype(BF16), wd_ref[...])
        xhat2, r2 = _ln_hat(u2)
        diff = xhat2 * g2_ref[...] + b2_ref[...] - t_ref[...]
        dh2 = diff * (1.0 / d)
        st_ref[0:1, :] += _colsum(dh2 * xhat2)
        st_ref[1:2, :] += _colsum(dh2)
        st_ref[2:3, :] += jnp.broadcast_to(_colsum(_rowsum(diff * diff)) * (0.5 / d), (1, d))
        du_ref[...] = _ln_bwd(dh2 * g2_ref[...], xhat2, r2)

    row = pl.BlockSpec((tm, d), lambda i: (i, 0))
    hbm = pl.BlockSpec(memory_space=pl.ANY)
    return pl.pallas_call(
        body, name="ffn_fwd", grid=(s // tm,),
        in_specs=[row, _full((1, d)), _full((1, d)), hbm, hbm, _full((1, d)), _full((1, d)), row],
        out_specs=[row, _full((8, d))],
        out_shape=[jax.ShapeDtypeStruct((s, d), F32), jax.ShapeDtypeStruct((8, d), F32)],
        scratch_shapes=[pltpu.VMEM(w_gu.shape, BF16), pltpu.VMEM(w_down.shape, BF16)],
        compiler_params=_params(56),
    )(u1, g1, b1, w_gu, w_down, g2, b2, target)


def _ffn_bwd_act(u1, du2, g1, b1, w_gu, w_down_t):
    s, d = u1.shape
    dff = w_down_t.shape[1]
    tm = min(FFN_TILE, s)

    def body(u_ref, du2_ref, g1_ref, b1_ref, wgu_hbm, wdt_hbm, h1b_ref, act_ref, dgu_ref, wgu_ref, wdt_ref):
        @pl.when(pl.program_id(0) == 0)
        def _():
            pltpu.sync_copy(wgu_hbm, wgu_ref)
            pltpu.sync_copy(wdt_hbm, wdt_ref)

        xhat, _ = _ln_hat(u_ref[...])
        h1b = (xhat * g1_ref[...] + b1_ref[...]).astype(BF16)
        h1b_ref[...] = h1b
        gate = _dot(h1b, wgu_ref[:, :dff])
        up = _dot(h1b, wgu_ref[:, dff:])
        dact = _dot(du2_ref[...].astype(BF16), wdt_ref[...])
        sg = jax.nn.sigmoid(gate)
        silu = gate * sg
        act_ref[...] = (silu * up).astype(BF16)
        dgu_ref[:, :dff] = (dact * up * (sg * (1.0 + gate * (1.0 - sg)))).astype(BF16)
        dgu_ref[:, dff:] = (dact * silu).astype(BF16)

    row = lambda width: pl.BlockSpec((tm, width), lambda i: (i, 0))
    hbm = pl.BlockSpec(memory_space=pl.ANY)
    return pl.pallas_call(
        body, name="ffn_bwd_act", grid=(s // tm,),
        in_specs=[row(d), row(d), _full((1, d)), _full((1, d)), hbm, hbm],
        out_specs=[row(d), row(dff), row(2 * dff)],
        out_shape=[jax.ShapeDtypeStruct((s, d), BF16), jax.ShapeDtypeStruct((s, dff), BF16),
                   jax.ShapeDtypeStruct((s, 2 * dff), BF16)],
        scratch_shapes=[pltpu.VMEM(w_gu.shape, BF16), pltpu.VMEM(w_down_t.shape, BF16)],
        compiler_params=_params(56),
    )(u1, du2, g1, b1, w_gu, w_down_t)


def _ffn_bwd_in(dgu, w_gu_t, u1, du2, g1):
    s, d = u1.shape
    tm = min(FFN_TILE, s)

    def body(dgu_ref, wt_hbm, u_ref, du2_ref, g1_ref, du1_ref, st_ref, wt_ref):
        @pl.when(pl.program_id(0) == 0)
        def _():
            pltpu.sync_copy(wt_hbm, wt_ref)
            st_ref[...] = jnp.zeros_like(st_ref)

        dh1 = _dot(dgu_ref[...], wt_ref[...]) + ALPHA * du2_ref[...]
        xhat, r = _ln_hat(u_ref[...])
        st_ref[0:1, :] += _colsum(dh1 * xhat)
        st_ref[1:2, :] += _colsum(dh1)
        du1_ref[...] = _ln_bwd(dh1 * g1_ref[...], xhat, r)

    row = lambda width: pl.BlockSpec((tm, width), lambda i: (i, 0))
    return pl.pallas_call(
        body, name="ffn_bwd_in", grid=(s // tm,),
        in_specs=[row(dgu.shape[1]), pl.BlockSpec(memory_space=pl.ANY), row(d), row(d), _full((1, d))],
        out_specs=[row(d), _full((8, d))],
        out_shape=[jax.ShapeDtypeStruct((s, d), F32), jax.ShapeDtypeStruct((8, d), F32)],
        scratch_shapes=[pltpu.VMEM(w_gu_t.shape, BF16)],
        compiler_params=_params(56),
    )(dgu, w_gu_t, u1, du2, g1)


def _rms_bwd(dm, o, g):
    n, r, _ = _rms_fwd(o, g)
    dn = dm * g
    return r * (dn - n * jnp.mean(dn * n, axis=-1, keepdims=True)), _colsum(dm * n)


def _mix_bwd(du1, w_out, sb_out, sw_out, sb_g, sw_g):
    s, d = du1.shape
    wsb, wsw = sb_out.shape[1], sw_out.shape[1]
    tm = min(ROW_TILE, s)

    def body(du_ref, w_ref, sb_ref, sw_ref, sbg_ref, swg_ref, dsb_ref, dsw_ref, st_ref):
        i = pl.program_id(0)

        @pl.when(i == 0)
        def _():
            st_ref[...] = jnp.zeros_like(st_ref)

        dmerged = _dot_nt(du_ref[...].astype(BF16), w_ref[...])
        dsb, gsb = _rms_bwd(dmerged[:, :wsb], sb_ref[...], sbg_ref[...])
        dsw, gsw = _rms_bwd(dmerged[:, wsb:], sw_ref[...], swg_ref[...])
        dsb_ref[...] = dsb.astype(BF16)
        dsw_ref[...] = dsw.astype(BF16)
        st_ref[0:1, :wsb] += gsb
        st_ref[0:1, wsb:] += gsw

    row = lambda width: pl.BlockSpec((tm, width), lambda i: (i, 0))
    return pl.pallas_call(
        body, name="mix_bwd", grid=(s // tm,),
        in_specs=[row(d), _full((wsb + wsw, d)), row(wsb), row(wsw), _full((1, wsb)), _full((1, wsw))],
        out_specs=[row(wsb), row(wsw), _full((8, wsb + wsw))],
        out_shape=[jax.ShapeDtypeStruct((s, wsb), BF16), jax.ShapeDtypeStruct((s, wsw), BF16),
                   jax.ShapeDtypeStruct((8, wsb + wsw), F32)],
        compiler_params=_params(),
    )(du1, w_out, sb_out, sw_out, sb_g, sw_g)


def _sb_bwd(proj, dout, out, n_pairs, comm=None):
    s = proj.shape[0]
    t = min(SB_TILE, s)
    nq = s // t
    width = n_pairs * LANES

    def body(q_ref, k_ref, v_ref, do_ref, o_ref, dq_ref, dk_out, dv_out, dk_ref, dv_ref):
        i = pl.program_id(1)

        @pl.when(i == 0)
        def _():
            dk_ref[...] = jnp.zeros_like(dk_ref)
            dv_ref[...] = jnp.zeros_like(dv_ref)

        lane = lax.broadcasted_iota(jnp.int32, (1, LANES), 1)
        first = lane < HEAD_DIM
        do2 = do_ref[...]
        qs = _sb_stack_heads(q_ref[...] * SCALE, first)
        dos = _sb_stack_heads(do2, first)
        prod = do2.astype(F32) * o_ref[...]
        totals = jnp.concatenate([_rowsum(jnp.where(first, prod, 0.0)), _rowsum(jnp.where(first, 0.0, prod))], axis=0)
        upper, incl = _sb_triangles(t)

        def visit(offs, carry, mask):
            k_t = _sb_key_tiles(k_ref, offs, t)
            v_t = _sb_key_tiles(v_ref, offs, t)
            c_l, c_e, dq = carry
            lb, a, c_l = _sb_scores(qs, k_t, upper, c_l, mask, t)
            a_b = a.astype(BF16)
            d_e = _dot_nt(dos, v_t) * a_b.astype(F32)
            d_eb = d_e.astype(BF16)
            cols = [slice(j * t, (j + 1) * t) for j in range(len(offs))]
            suf_e, c_e = _sb_suffix([[d_eb[:, c]] for c in cols], [_rowsum(d_e[:, c]) for c in cols], incl, c_e)
            dz = d_e - jnp.exp(lb) * (d_e + (totals - suf_e))
            if mask is not None:
                dz = jnp.where(mask, dz, 0.0)
            dzb = dz.astype(BF16)
            dk_t = _dot_tn(dzb, qs)
            dv_t = _dot_tn(a_b, dos)
            for off, c in zip(offs, cols):
                dk_ref[pl.ds(off, t), :] += dk_t[c, :]
                dv_ref[pl.ds(off, t), :] += dv_t[c, :]
            return c_l, c_e, dq + _dot(dzb, k_t)

        init = (jnp.zeros((2 * t, 1), F32), jnp.zeros((2 * t, 1), F32), jnp.zeros((2 * t, LANES), F32))
        _, _, dq = _sb_walk(i, t, visit, init)
        dq_ref[...] = (jnp.where(first, dq[:t], dq[t:]) * SCALE).astype(BF16)

        @pl.when(i == nq - 1)
        def _():
            dk_out[...] = dk_ref[...].astype(BF16)
            dv_out[...] = dv_ref[...].astype(BF16)

    qblk = pl.BlockSpec((t, LANES), lambda h, i: (i, h))
    whole = pl.BlockSpec((s, LANES), lambda h, i: (0, h))
    return _call(
        body, "sb_bwd", (n_pairs, nq),
        in_specs=[qblk,
                  pl.BlockSpec((s, LANES), lambda h, i: (0, n_pairs + h)),
                  pl.BlockSpec((s, LANES), lambda h, i: (0, 2 * n_pairs + h)),
                  qblk, qblk],
        out_specs=[qblk, whole, whole],
        out_shape=[jax.ShapeDtypeStruct((s, width), BF16)] * 3,
        args=(proj, proj, proj, dout, out),
        scratch_shapes=[pltpu.VMEM((s, LANES), F32), pltpu.VMEM((s, LANES), F32)], comm=comm)


def _swa_bwd(proj, dout, bucket, rel_bias, sinks, n_heads, qcol, kcol, vcol, comm=None):
    s = proj.shape[0]
    width = n_heads * HEAD_DIM
    n_groups = LANES // HEAD_DIM
    per_group = n_heads // n_groups
    nb = s // SWA_BLOCK

    def body(q_ref, kp_ref, kc_ref, vp_ref, vc_ref, bucket_ref, rb_ref, sk_ref, do_ref,
             dq_ref, dk_out, dv_out, dsk_ref, drb_ref, bias_ref, dbias_ref, dk_ref, dv_ref):
        i = pl.program_id(0)

        @pl.when(i == 0)
        def _():
            _swa_build_bias(bucket_ref, rb_ref, bias_ref, n_groups, per_group)
            dbias_ref[...] = jnp.zeros_like(dbias_ref)
            dk_ref[...] = jnp.zeros_like(dk_ref)
            dv_ref[...] = jnp.zeros_like(dv_ref)
            dsk_ref[...] = jnp.zeros_like(dsk_ref)

        lane = lax.broadcasted_iota(jnp.int32, (1, LANES), 1)
        first = lane < HEAD_DIM
        valid = _swa_valid(i, per_group)
        kcat = jnp.concatenate([kp_ref[...], kc_ref[...]], axis=0)
        vcat = jnp.concatenate([vp_ref[...], vc_ref[...]], axis=0)
        dkcat = jnp.zeros((2 * SWA_BLOCK, LANES), F32)
        dvcat = jnp.zeros((2 * SWA_BLOCK, LANES), F32)
        pieces = {}
        for g in range(n_groups):
            sel = first if g == 0 else jnp.logical_not(first)
            q_g = _swa_stack(q_ref, g, per_group, sel)
            do_g = _swa_stack(do_ref, g, per_group, sel)
            prob, p_sink = _swa_probs(q_g, kcat, bias_ref[g], valid, _swa_sink_rows(sk_ref, g, per_group))
            dprob = _dot_nt(do_g, vcat)
            delta = _rowsum(prob * dprob)
            dlog = prob * (dprob - delta)
            sink_term = p_sink * delta
            for hh in range(per_group):
                h = g * per_group + hh
                tot = _colsum(sink_term[hh * SWA_BLOCK:(hh + 1) * SWA_BLOCK, :])
                dsk_ref[h:h + 1, :] += jnp.broadcast_to(-tot, (1, LANES))
            dbias_ref[g] += dlog
            dlb = (dlog * SCALE).astype(BF16)
            _swa_unstack(_dot(dlb, kcat), g, per_group, pieces)
            dkcat += _dot_tn(dlb, q_g)
            dvcat += _dot_tn(prob.astype(BF16), do_g)
        for j in range(n_heads // 2):
            dq_ref[:, j * LANES:(j + 1) * LANES] = jnp.where(first, pieces[2 * j], pieces[2 * j + 1]).astype(BF16)

        cur = pl.multiple_of(i * SWA_BLOCK, SWA_BLOCK)
        dk_ref[pl.ds(cur, SWA_BLOCK), :] += dkcat[SWA_BLOCK:, :]
        dv_ref[pl.ds(cur, SWA_BLOCK), :] += dvcat[SWA_BLOCK:, :]

        @pl.when(i > 0)
        def _():
            prv = pl.multiple_of((i - 1) * SWA_BLOCK, SWA_BLOCK)
            dk_ref[pl.ds(prv, SWA_BLOCK), :] += dkcat[:SWA_BLOCK, :]
            dv_ref[pl.ds(prv, SWA_BLOCK), :] += dvcat[:SWA_BLOCK, :]

        @pl.when(i == nb - 1)
        def _():
            bk = bucket_ref[...]
            rowi = lax.broadcasted_iota(jnp.int32, (REL_BUCKETS, LANES), 0)
            coli = lax.broadcasted_iota(jnp.int32, (REL_BUCKETS, LANES), 1)
            res = jnp.zeros((REL_BUCKETS, LANES), F32)
            for h in range(n_heads):
                g, hh = divmod(h, per_group)
                db = dbias_ref[g, hh * SWA_BLOCK:(hh + 1) * SWA_BLOCK, :]
                for b in range(REL_BUCKETS):
                    tot = _colsum(_rowsum(jnp.where(bk == b, db, 0.0)))
                    res = jnp.where((rowi == b) & (coli == h), tot, res)
            drb_ref[...] = res
            dk_out[...] = dk_ref[...].astype(BF16)
            dv_out[...] = dv_ref[...].astype(BF16)

    in_specs = _swa_specs(n_heads, qcol, kcol, vcol) + [pl.BlockSpec((SWA_BLOCK, width), lambda i: (i, 0))]
    return _call(
        body, "swa_bwd", (nb,),
        in_specs=in_specs,
        out_specs=[pl.BlockSpec((SWA_BLOCK, width), lambda i: (i, 0)),
                   _full((s, LANES)), _full((s, LANES)), _full((8, LANES)), _full((REL_BUCKETS, LANES))],
        out_shape=[jax.ShapeDtypeStruct((s, width), BF16), jax.ShapeDtypeStruct((s, LANES), BF16),
                   jax.ShapeDtypeStruct((s, LANES), BF16), jax.ShapeDtypeStruct((8, LANES), F32),
                   jax.ShapeDtypeStruct((REL_BUCKETS, LANES), F32)],
        args=(proj, proj, proj, proj, proj, bucket, rel_bias, sinks, dout),
        scratch_shapes=[pltpu.VMEM((n_groups, per_group * SWA_BLOCK, 2 * SWA_BLOCK), F32),
                        pltpu.VMEM((n_groups, per_group * SWA_BLOCK, 2 * SWA_BLOCK), F32),
                        pltpu.VMEM((s, LANES), F32), pltpu.VMEM((s, LANES), F32)],
        comm=comm)


def _proj_bwd(pieces, w_in_t, du1, x, g_in, comm=None):
    s, d = x.shape
    cols = w_in_t.shape[0]
    tm = min(ROW_TILE, s)
    n_p = len(pieces)

    def body(*refs):
        p_refs = refs[:n_p]
        w_ref, du_ref, x_ref, g_ref, dx_ref, st_ref = refs[n_p:]
        i = pl.program_id(0)

        @pl.when(i == 0)
        def _():
            st_ref[...] = jnp.zeros_like(st_ref)

        dproj = jnp.concatenate([p[...] for p in p_refs], axis=1)
        dh0 = _dot(dproj, w_ref[...]) + ALPHA * du_ref[...]
        xhat, r = _ln_hat(x_ref[...])
        st_ref[0:1, :] += _colsum(dh0 * xhat)
        st_ref[1:2, :] += _colsum(dh0)
        dx_ref[...] = _ln_bwd(dh0 * g_ref[...], xhat, r)

    row = lambda width: pl.BlockSpec((tm, width), lambda i: (i, 0))
    return _call(
        body, "proj_bwd", (s // tm,),
        in_specs=[row(p.shape[1]) for p in pieces] + [_full((cols, d)), row(d), row(d), _full((1, d))],
        out_specs=[row(d), _full((8, d))],
        out_shape=[jax.ShapeDtypeStruct((s, d), F32), jax.ShapeDtypeStruct((8, d), F32)],
        args=(*pieces, w_in_t, du1, x, g_in), comm=comm)


def _wgrad(name, pieces, b, tm, tn):
    s, n = b.shape
    m = sum(p.shape[1] for p in pieces)
    n_p = len(pieces)
    assert n_p == 1 or tm == m
    ts = min(WGRAD_TOKENS if b.dtype == BF16 and n_p == 1 else WGRAD_TOKENS // 2, s)
    n_k = s // ts

    def body(*refs):
        p_refs, b_ref, o_ref, acc_ref = refs[:n_p], refs[n_p], refs[n_p + 1], refs[n_p + 2]
        k = pl.program_id(2)

        @pl.when(k == 0)
        def _():
            acc_ref[...] = jnp.zeros_like(acc_ref)

        a = p_refs[0][...] if n_p == 1 else jnp.concatenate([p[...] for p in p_refs], axis=1)
        acc_ref[...] += _dot_tn(a, b_ref[...].astype(BF16))

        @pl.when(k == n_k - 1)
        def _():
            o_ref[...] = acc_ref[...].astype(BF16)

    piece_spec = lambda p: pl.BlockSpec((ts, tm if n_p == 1 else p.shape[1]), lambda i, j, k: (k, i))
    return pl.pallas_call(
        body, name=name, grid=(m // tm, n // tn, n_k),
        in_specs=[piece_spec(p) for p in pieces] + [pl.BlockSpec((ts, tn), lambda i, j, k: (k, j))],
        out_specs=pl.BlockSpec((tm, tn), lambda i, j, k: (i, j)),
        out_shape=jax.ShapeDtypeStruct((m, n), BF16),
        scratch_shapes=[pltpu.VMEM((tm, tn), F32)],
        compiler_params=_params(),
    )(*pieces, b)


def _adamw_math(w, g, m, v):
    m = ADAM_B1 * m + (1.0 - ADAM_B1) * g
    v = ADAM_B2 * v + (1.0 - ADAM_B2) * (g * g)
    m_hat = m / (1.0 - ADAM_B1 ** ADAM_STEP)
    v_hat = v / (1.0 - ADAM_B2 ** ADAM_STEP)
    delta = -ADAM_LR * (m_hat / (jnp.sqrt(v_hat) + ADAM_EPS) + ADAM_WD * w)
    return delta, m, v


def _adamw_rows(rows):
    return max(r for r in range(16, 257, 16) if rows % r == 0)


def _adamw(name, landed, w, m, v, tr):
    rows, cols = w.shape

    def body(l_ref, w_ref, m_ref, v_ref, g_ref, d_ref, nm_ref, nv_ref):
        g = l_ref[0].astype(F32)
        for src in range(1, N_DEV):
            g = g + l_ref[src].astype(F32)
        delta, nm, nv = _adamw_math(w_ref[...], g, m_ref[...], v_ref[...])
        g_ref[...] = g
        d_ref[...] = delta
        nm_ref[...] = nm
        nv_ref[...] = nv

    blk = pl.BlockSpec((tr, cols), lambda i: (i, 0))
    shape = jax.ShapeDtypeStruct((rows, cols), F32)
    return pl.pallas_call(
        body, name=name, grid=(rows // tr,),
        in_specs=[pl.BlockSpec((N_DEV, tr, cols), lambda i: (0, i, 0)), blk, blk, blk],
        out_specs=[blk, blk, blk, blk],
        out_shape=[shape, shape, shape, shape],
        compiler_params=_params(),
    )(landed, w, m, v)


def _pack(d, ln_in_g, ln_in_b, ln1_g, ln1_b, ln2_g, ln2_b, sb_g, sw_g, rel_bias, sinks, extra=None):
    tail = [rel_bias.reshape(-1), sinks.reshape(-1)]
    if extra is not None:
        tail.append(extra.reshape(-1))
    tail = jnp.concatenate(tail)
    tail = jnp.concatenate([tail, jnp.zeros((d - tail.shape[0],), F32)])
    rows = [ln_in_g.reshape(-1), ln_in_b.reshape(-1), ln1_g.reshape(-1), ln1_b.reshape(-1),
            ln2_g.reshape(-1), ln2_b.reshape(-1),
            jnp.concatenate([sb_g.reshape(-1), sw_g.reshape(-1)]), tail]
    return jnp.stack(rows)


def _unpack(p, wsb, n_rb, n_sk):
    return [p[0], p[1], p[6, :wsb][None], p[6, wsb:][None], p[7, n_rb:n_rb + n_sk][None],
            p[7, :n_rb].reshape(REL_BUCKETS, -1), p[2][None], p[3][None], p[4][None], p[5][None]]


def kernel(x, ln_in_g, ln_in_b, w_in, sb_norm_g, swa_norm_g, sinks, rel_bias, w_out, ln1_g, ln1_b, w_gate_up, w_down, ln2_g, ln2_b, loss_target, m_ln_in_g, m_ln_in_b, m_w_in, m_sb_norm_g, m_swa_norm_g, m_sinks, m_rel_bias, m_w_out, m_ln1_g, m_ln1_b, m_w_gate_up, m_w_down, m_ln2_g, m_ln2_b, v_ln_in_g, v_ln_in_b, v_w_in, v_sb_norm_g, v_swa_norm_g, v_sinks, v_rel_bias, v_w_out, v_ln1_g, v_ln1_b, v_w_gate_up, v_w_down, v_ln2_g, v_ln2_b):
    x2 = x[0]
    tgt = loss_target[0]
    s, d = x2.shape
    wsb = sb_norm_g.shape[-1]
    wsw = swa_norm_g.shape[-1]
    n_sw_heads = sinks.shape[-1]
    n_pairs = wsb // LANES
    dff = w_down.shape[1] * N_DEV
    assert wsb % LANES == 0 and wsw % LANES == 0 and n_sw_heads * HEAD_DIM == wsw
    assert 3 * wsb % wsw == 0 and dff % LANES == 0 and s % SWA_BLOCK == 0
    qcol = 3 * wsb // wsw
    kcol = (3 * wsb + wsw) // LANES
    vcol = kcol + 1
    assert w_in.shape[-1] * N_DEV == (vcol + 1) * LANES

    t2 = lambda a: jnp.transpose(a[0])
    big_w = [t2(w_in), w_out[0], t2(w_gate_up), w_down[0]]
    big_m = [t2(m_w_in), m_w_out[0], t2(m_w_gate_up), m_w_down[0]]
    big_v = [t2(v_w_in), v_w_out[0], t2(v_w_gate_up), v_w_down[0]]

    cat_rows = lambda g: g.reshape(N_DEV * g.shape[1], g.shape[2])
    shards = [w.astype(BF16) for w in big_w]
    w_in_t = cat_rows(_exchange("w_in_allgather", shards[:1], ["gather"])[0])
    w_in_f = jnp.transpose(w_in_t)

    vec = lambda a: a.reshape(1, -1)
    g_in, b_in = vec(ln_in_g), vec(ln_in_b)
    bucket = jnp.asarray(_swa_bucket_table())

    h0b, proj = _ln_proj(x2, g_in, b_in, w_in_f)
    sb_out, gathered = _sb_fwd(proj, n_pairs, comm=(shards[1:], ["gather"] * 3))
    w_out_f, w_gu_t, w_down_f = cat_rows(gathered[0]), cat_rows(gathered[1]), cat_rows(gathered[2])
    w_gu_f = jnp.transpose(w_gu_t)
    sw_out = _swa_fwd(proj, bucket, rel_bias, sinks, n_sw_heads, qcol, kcol, vcol)
    merged, u1 = _mix_ln1(sb_out, sw_out, x2, g_in, b_in, sb_norm_g, swa_norm_g, w_out_f)
    du2, st_ln2 = _ffn_fwd(u1, ln1_g, ln1_b, w_gu_f, w_down_f, ln2_g, ln2_b, tgt)

    split_rows = lambda g: g.reshape(N_DEV, g.shape[0] // N_DEV, g.shape[1])
    h1b, act, dgu = _ffn_bwd_act(u1, du2, ln1_g, ln1_b, w_gu_f, jnp.transpose(w_down_f))
    du1, st_ln1 = _ffn_bwd_in(dgu, w_gu_t, u1, du2, ln1_g)
    gw_gu = _wgrad("wgrad_gate_up", [dgu], h1b, dff // 2, d)
    gw_down = _wgrad("wgrad_down", [act], du2, dff // 2, d)
    gw_out = _wgrad("wgrad_out", [merged], du1, min(512, d), d)
    dsb, dsw, st_rms = _mix_bwd(du1, w_out_f, sb_out, sw_out, sb_norm_g, swa_norm_g)
    (dq_sb, dk_sb, dv_sb), (land_gu, land_out) = _sb_bwd(
        proj, dsb, sb_out, n_pairs, comm=([split_rows(gw_gu), split_rows(gw_out)], ["scatter"] * 2))
    (dq_sw, dk_sw, dv_sw, st_sink, st_rb), (land_down,) = _swa_bwd(
        proj, dsw, bucket, rel_bias, sinks, n_sw_heads, qcol, kcol, vcol,
        comm=([split_rows(gw_down)], ["scatter"]))
    pieces = [dq_sb, dk_sb, dv_sb, dq_sw, dk_sw, dv_sw]
    gw_in = _wgrad("wgrad_in", pieces, h0b, proj.shape[1], d)
    (grad_x, st_in), (land_in,) = _proj_bwd(pieces, w_in_t, du1, x2, g_in, comm=([split_rows(gw_in)], ["scatter"]))

    n_rb = rel_bias.size
    small = _pack(d, st_in[0], st_in[1], st_ln1[0], st_ln1[1], st_ln2[0], st_ln2[1],
                  st_rms[0, :wsb], st_rms[0, wsb:], st_rb[:, :n_sw_heads], st_sink[:n_sw_heads, 0],
                  extra=st_ln2[2, 0:1])
    land_small = _exchange("small_grads_allgather", [small], ["gather"])[0]
    landed = [land_in, land_out, land_gu, land_down, land_small]

    big = []
    for name, land, w, m, v in zip(["adamw_in", "adamw_out", "adamw_gate_up", "adamw_down"], landed[:4], big_w, big_m, big_v):
        big.append(_adamw(name, land, w, m, v, _adamw_rows(w.shape[0])))

    small_w = _pack(d, ln_in_g, ln_in_b, ln1_g, ln1_b, ln2_g, ln2_b, sb_norm_g, swa_norm_g, rel_bias, sinks)
    small_m = _pack(d, m_ln_in_g, m_ln_in_b, m_ln1_g, m_ln1_b, m_ln2_g, m_ln2_b, m_sb_norm_g, m_swa_norm_g, m_rel_bias, m_sinks)
    small_v = _pack(d, v_ln_in_g, v_ln_in_b, v_ln1_g, v_ln1_b, v_ln2_g, v_ln2_b, v_sb_norm_g, v_swa_norm_g, v_rel_bias, v_sinks)
    sg, sd, sm, sv = _adamw("adamw_small", landed[4], small_w, small_m, small_v, 8)
    n_sk = sinks.size
    loss = sg[7, n_rb + n_sk]

    def leaves(idx):
        sm_l = _unpack([sg, sd, sm, sv][idx], wsb, n_rb, n_sk)
        bg = [jnp.transpose(big[0][idx])[None], big[1][idx][None], jnp.transpose(big[2][idx])[None], big[3][idx][None]]
        return [sm_l[0], sm_l[1], bg[0], sm_l[2], sm_l[3], sm_l[4], sm_l[5], bg[1], sm_l[6], sm_l[7], bg[2], bg[3], sm_l[8], sm_l[9]]

    return (loss, grad_x[None], *leaves(0), *leaves(1), *leaves(2), *leaves(3))
```

```python
import functools
import math

import numpy as np
import jax
import jax.numpy as jnp
from jax import lax
from jax.experimental import pallas as pl
from jax.experimental.pallas import tpu as pltpu

F32 = jnp.float32
BF16 = jnp.bfloat16
MESH = pl.DeviceIdType.MESH

N_DEV = 8
LANES = 128
HEAD_DIM = 64
SCALE = HEAD_DIM ** -0.5
SWA_BLOCK = 128
REL_BUCKETS = 32
REL_MAX_DIST = 128
ALPHA = 2.0 ** 0.25
LN_EPS = 1e-5
RMS_EPS = 1e-6
ADAM_LR = 0.001
ADAM_B1 = 0.9
ADAM_B2 = 0.999
ADAM_EPS = 1e-08
ADAM_WD = 0.01
ADAM_STEP = 10

ROW_TILE = 512
SB_TILE = 256
FFN_TILE = 256
WGRAD_TOKENS = 2048
SB_UNDERFLOW = -110.0
MIB = 1024 * 1024


def _params(vmem_mib=48):
    return pltpu.CompilerParams(vmem_limit_bytes=vmem_mib * MIB)


def _dot(a, b):
    return jnp.dot(a, b, preferred_element_type=F32)


def _dot_nt(a, b):
    return lax.dot_general(a, b, (((1,), (1,)), ((), ())), preferred_element_type=F32)


def _dot_tn(a, b):
    return lax.dot_general(a, b, (((0,), (0,)), ((), ())), preferred_element_type=F32)


def _ln_hat(x):
    mu = jnp.mean(x, axis=-1, keepdims=True)
    xc = x - mu
    var = jnp.mean(xc * xc, axis=-1, keepdims=True)
    r = lax.rsqrt(var + LN_EPS)
    return xc * r, r


def _ln_bwd(dxhat, xhat, r):
    return r * (dxhat - jnp.mean(dxhat, axis=-1, keepdims=True)
                - xhat * jnp.mean(dxhat * xhat, axis=-1, keepdims=True))


def _colsum(a):
    return jnp.sum(a, axis=0, keepdims=True)


def _rowsum(a):
    return jnp.sum(a, axis=1, keepdims=True)


def _full(shape):
    return pl.BlockSpec(shape, lambda *_: (0,) * len(shape))


def _comm_out_shapes(arrays, kinds):
    shapes = []
    for a, kind in zip(arrays, kinds):
        blk = a.shape if kind == "gather" else a.shape[1:]
        shapes.append(jax.ShapeDtypeStruct((N_DEV,) + tuple(blk), a.dtype))
    return shapes


def _comm_sems(n):
    return [pltpu.SemaphoreType.DMA((n, N_DEV - 1)), pltpu.SemaphoreType.DMA((n, N_DEV - 1)),
            pltpu.SemaphoreType.DMA((n,))]


def _comm_copies(ins, outs, kinds, send_sems, recv_sems, local_sems):
    x, y, c = lax.axis_index("x"), lax.axis_index("y"), lax.axis_index("c")
    me = 4 * x + 2 * y + c

    def src_for(t, dev_lin):
        return ins[t] if kinds[t] == "gather" else ins[t].at[dev_lin]

    local = [pltpu.make_async_copy(src_for(t, me), outs[t].at[me], local_sems.at[t]) for t in range(len(kinds))]
    sends, arrivals = [], []
    for k in range(1, N_DEV):
        px = 1 - x if (k >> 2) & 1 else x
        py = 1 - y if (k >> 1) & 1 else y
        pc = 1 - c if k & 1 else c
        peer_lin = 4 * px + 2 * py + pc
        for t in range(len(kinds)):
            sems = dict(send_sem=send_sems.at[t, k - 1], recv_sem=recv_sems.at[t, k - 1],
                        device_id=(px, py, pc), device_id_type=MESH)
            sends.append(pltpu.make_async_remote_copy(src_ref=src_for(t, peer_lin), dst_ref=outs[t].at[me], **sems))
            arrivals.append(pltpu.make_async_remote_copy(src_ref=src_for(t, peer_lin), dst_ref=outs[t].at[peer_lin], **sems))
    return local, sends, arrivals


def _comm_start(ins, outs, kinds, sems):
    local, sends, _ = _comm_copies(ins, outs, kinds, *sems)
    for cp in local + sends:
        cp.start()


def _comm_finish(ins, outs, kinds, sems):
    local, sends, arrivals = _comm_copies(ins, outs, kinds, *sems)
    for cp in arrivals:
        cp.wait_recv()
    for cp in sends:
        cp.wait_send()
    for cp in local:
        cp.wait()


def _exchange(name, arrays, kinds):
    n = len(arrays)

    def body(*refs):
        ins, outs, sems = refs[:n], refs[n:2 * n], refs[2 * n:]
        _comm_start(ins, outs, kinds, sems)
        _comm_finish(ins, outs, kinds, sems)

    any_spec = pl.BlockSpec(memory_space=pl.ANY)
    return pl.pallas_call(
        body, name=name, out_shape=_comm_out_shapes(arrays, kinds),
        in_specs=[any_spec] * n, out_specs=[any_spec] * n,
        scratch_shapes=_comm_sems(n),
    )(*arrays)


def _call(body, name, grid, in_specs, out_specs, out_shape, args, scratch_shapes=(), comm=None):
    if comm is None:
        outs = pl.pallas_call(body, name=name, grid=grid, in_specs=in_specs, out_specs=out_specs,
                              out_shape=out_shape, scratch_shapes=list(scratch_shapes),
                              compiler_params=_params())(*args)
        return outs, []
    arrays, kinds = comm
    n, n_in, n_out, n_scr = len(arrays), len(in_specs), len(out_specs), len(scratch_shapes)

    def fused(*refs):
        c_in, x_in = refs[:n_in], refs[n_in:n_in + n]
        c_out = refs[n_in + n:n_in + n + n_out]
        x_out = refs[n_in + n + n_out:n_in + 2 * n + n_out]
        rest = refs[n_in + 2 * n + n_out:]
        c_scr, sems = rest[:n_scr], rest[n_scr:]
        ids = [pl.program_id(a) for a in range(len(grid))]
        is_first = functools.reduce(jnp.logical_and, [i == 0 for i in ids])
        is_last = functools.reduce(jnp.logical_and, [i == g - 1 for i, g in zip(ids, grid)])

        @pl.when(is_first)
        def _():
            _comm_start(x_in, x_out, kinds, sems)

        body(*c_in, *c_out, *c_scr)

        @pl.when(is_last)
        def _():
            _comm_finish(x_in, x_out, kinds, sems)

    any_spec = pl.BlockSpec(memory_space=pl.ANY)
    outs = pl.pallas_call(
        fused, name=name, grid=grid,
        in_specs=list(in_specs) + [any_spec] * n, out_specs=list(out_specs) + [any_spec] * n,
        out_shape=list(out_shape) + _comm_out_shapes(arrays, kinds),
        scratch_shapes=list(scratch_shapes) + _comm_sems(n),
        compiler_params=_params())(*args, *arrays)
    return outs[:n_out], outs[n_out:]


def _ln_proj(x, g, b, w_in_t):
    s, d = x.shape
    cols = w_in_t.shape[0]
    tm = min(ROW_TILE, s)

    def body(x_ref, g_ref, b_ref, w_ref, h_ref, p_ref):
        xhat, _ = _ln_hat(x_ref[...])
        h = (xhat * g_ref[...] + b_ref[...]).astype(BF16)
        h_ref[...] = h
        p_ref[...] = _dot_nt(h, w_ref[...]).astype(BF16)

    row = lambda width: pl.BlockSpec((tm, width), lambda i: (i, 0))
    return pl.pallas_call(
        body, name="ln_proj", grid=(s // tm,),
        in_specs=[row(d), _full((1, d)), _full((1, d)), _full((cols, d))],
        out_specs=[row(d), row(cols)],
        out_shape=[jax.ShapeDtypeStruct((s, d), BF16), jax.ShapeDtypeStruct((s, cols), BF16)],
        compiler_params=_params(),
    )(x, g, b, w_in_t)


def _sb_triangles(t):
    row = lax.broadcasted_iota(jnp.int32, (t, t), 0)
    col = lax.broadcasted_iota(jnp.int32, (t, t), 1)
    return (row > col).astype(BF16), (row >= col).astype(BF16)


def _sb_first_mask(t, has_prev):
    qrow = lax.broadcasted_iota(jnp.int32, (2 * t, 2 * t), 0) & (t - 1)
    col = lax.broadcasted_iota(jnp.int32, (2 * t, 2 * t), 1)
    return ((col < t) & has_prev) | ((col >= t) & (col - t < qrow))


def _sb_stack_heads(x2, first):
    zero = jnp.zeros_like(x2)
    return jnp.concatenate([jnp.where(first, x2, zero), jnp.where(first, zero, x2)], axis=0)


def _sb_key_tiles(ref, offs, t):
    tiles = [ref[pl.ds(off, t), :] for off in offs]
    return tiles[0] if len(tiles) == 1 else jnp.concatenate(tiles, axis=0)


def _sb_suffix(terms, row_sums, tri, carry):
    out = [None] * len(terms)
    for j in reversed(range(len(terms))):
        suf = carry
        for op in terms[j]:
            suf = suf + _dot(op, tri)
        out[j] = suf
        carry = carry + row_sums[j]
    return (out[0] if len(out) == 1 else jnp.concatenate(out, axis=1)), carry


def _sb_scores(qh, k_t, upper, carry_l, mask, t):
    z = _dot_nt(qh, k_t)
    sp = jnp.log(1.0 + jnp.exp(-jnp.abs(z)))
    neg = jnp.minimum(z, 0.0)
    lb = neg - sp
    l1 = (neg - z) - sp
    if mask is not None:
        l1 = jnp.where(mask, l1, 0.0)
    hi = l1.astype(BF16)
    lo = (l1 - hi.astype(F32)).astype(BF16)
    cols = [slice(j * t, (j + 1) * t) for j in range(z.shape[1] // t)]
    suf, carry_l = _sb_suffix([[hi[:, c], lo[:, c]] for c in cols], [_rowsum(l1[:, c]) for c in cols], upper, carry_l)
    a = jnp.exp(lb + suf)
    if mask is not None:
        a = jnp.where(mask, a, 0.0)
    return lb, a, carry_l


def _sb_walk(i, t, visit, init):
    def alive(carry):
        return jnp.max(carry[0]) > SB_UNDERFLOW

    prev = pl.multiple_of(jnp.maximum(i - 1, 0) * t, t)
    carry = visit((prev, pl.multiple_of(i * t, t)), init, _sb_first_mask(t, i > 0))

    def cond(state):
        j, go, _ = state
        return (j < i - 1) & go

    def body(state):
        j, _, carry = state
        carry = visit((pl.multiple_of((i - 2 - j) * t, t),), carry, None)
        return j + 1, alive(carry), carry

    return lax.while_loop(cond, body, (jnp.int32(0), alive(carry), carry))[2]


def _sb_fwd(proj, n_pairs, comm=None):
    s = proj.shape[0]
    t = min(SB_TILE, s)
    nq = s // t

    def body(q_ref, k_ref, v_ref, o_ref):
        i = pl.program_id(1)
        lane = lax.broadcasted_iota(jnp.int32, (1, LANES), 1)
        first = lane < HEAD_DIM
        qs = _sb_stack_heads(q_ref[...] * SCALE, first)
        upper, _ = _sb_triangles(t)

        def visit(offs, carry, mask):
            c_l, acc = carry
            _, a, c_l = _sb_scores(qs, _sb_key_tiles(k_ref, offs, t), upper, c_l, mask, t)
            return c_l, acc + _dot(a.astype(BF16), _sb_key_tiles(v_ref, offs, t))

        init = (jnp.zeros((2 * t, 1), F32), jnp.zeros((2 * t, LANES), F32))
        _, acc = _sb_walk(i, t, visit, init)
        o_ref[...] = jnp.where(first, acc[:t], acc[t:])

    outs, landed = _call(
        body, "sb_fwd", (n_pairs, nq),
        in_specs=[pl.BlockSpec((t, LANES), lambda h, i: (i, h)),
                  pl.BlockSpec((s, LANES), lambda h, i: (0, n_pairs + h)),
                  pl.BlockSpec((s, LANES), lambda h, i: (0, 2 * n_pairs + h))],
        out_specs=[pl.BlockSpec((t, LANES), lambda h, i: (i, h))],
        out_shape=[jax.ShapeDtypeStruct((s, n_pairs * LANES), F32)],
        args=(proj, proj, proj), comm=comm)
    return outs[0], landed


def _swa_bucket_table():
    qi = np.arange(SWA_BLOCK)[:, None]
    cj = np.arange(2 * SWA_BLOCK)[None, :]
    dist = qi + SWA_BLOCK - cj
    exact = REL_BUCKETS // 2
    d = np.maximum(dist, 0)
    d_f = np.maximum(d, 1).astype(np.float32)
    large = exact + (np.log(d_f / np.float32(exact)) / np.float32(math.log(REL_MAX_DIST / exact))
                     * np.float32(REL_BUCKETS - exact)).astype(np.int32)
    large = np.minimum(large, REL_BUCKETS - 1)
    return np.where(d < exact, d, large).astype(np.int32)


def _swa_build_bias(bucket_ref, rb_ref, bias_ref, n_groups, per_group):
    bk = bucket_ref[...]
    dist = (lax.broadcasted_iota(jnp.int32, bk.shape, 0) + SWA_BLOCK) - lax.broadcasted_iota(jnp.int32, bk.shape, 1)
    window = (dist >= 0) & (dist < SWA_BLOCK)
    for g in range(n_groups):
        for hh in range(per_group):
            acc = jnp.zeros(bk.shape, F32)
            for b in range(REL_BUCKETS):
                acc = jnp.where(bk == b, rb_ref[b, g * per_group + hh], acc)
            bias_ref[g, hh * SWA_BLOCK:(hh + 1) * SWA_BLOCK, :] = jnp.where(window, acc, -jnp.inf)


def _swa_first_block_mask(i):
    col = lax.broadcasted_iota(jnp.int32, (1, 2 * SWA_BLOCK), 1)
    return jnp.where((col < SWA_BLOCK) & (i == 0), -jnp.inf, 0.0)


def _swa_place(blk, h, group, sel):
    if (h % 2) != group:
        blk = pltpu.roll(blk.astype(F32), HEAD_DIM, axis=1).astype(BF16)
    return jnp.where(sel, blk, jnp.zeros_like(blk))


def _swa_stack(ref, group, per_group, sel, scale=1.0):
    parts = []
    for hh in range(per_group):
        h = group * per_group + hh
        parts.append(_swa_place(ref[:, (h // 2) * LANES:(h // 2 + 1) * LANES], h, group, sel))
    stacked = jnp.concatenate(parts, axis=0)
    return stacked if scale == 1.0 else stacked * scale


def _swa_unstack(stacked, group, per_group, pieces):
    for hh in range(per_group):
        h = group * per_group + hh
        piece = stacked[hh * SWA_BLOCK:(hh + 1) * SWA_BLOCK, :]
        pieces[h] = pltpu.roll(piece, HEAD_DIM, axis=1) if (h % 2) != group else piece


def _swa_sink_rows(sk_ref, group, per_group):
    rowh = lax.broadcasted_iota(jnp.int32, (per_group * SWA_BLOCK, 1), 0) // SWA_BLOCK
    sink = jnp.zeros((per_group * SWA_BLOCK, 1), F32) + sk_ref[0, group * per_group]
    for hh in range(1, per_group):
        sink = jnp.where(rowh == hh, sk_ref[0, group * per_group + hh], sink)
    return sink


def _swa_probs(q_pos, kcat, bias_h, first_mask, sink):
    logits = _dot_nt(q_pos, kcat) + (bias_h + first_mask)
    m = jnp.maximum(jnp.max(logits, axis=1, keepdims=True), sink)
    p = jnp.exp(logits - m)
    es = jnp.exp(sink - m)
    inv = 1.0 / (_rowsum(p) + es)
    return p * inv, es * inv


def _swa_specs(n_heads, qcol, kcol, vcol):
    width = n_heads * HEAD_DIM
    prev = lambda col: pl.BlockSpec((SWA_BLOCK, LANES), lambda i: (jnp.maximum(i - 1, 0), col))
    cur = lambda col: pl.BlockSpec((SWA_BLOCK, LANES), lambda i: (i, col))
    return [pl.BlockSpec((SWA_BLOCK, width), lambda i: (i, qcol)),
            prev(kcol), cur(kcol), prev(vcol), cur(vcol),
            _full((SWA_BLOCK, 2 * SWA_BLOCK)),
            pl.BlockSpec(memory_space=pltpu.SMEM), pl.BlockSpec(memory_space=pltpu.SMEM)]


def _swa_fwd(proj, bucket, rel_bias, sinks, n_heads, qcol, kcol, vcol):
    s = proj.shape[0]
    width = n_heads * HEAD_DIM
    n_groups = LANES // HEAD_DIM
    per_group = n_heads // n_groups

    def body(q_ref, kp_ref, kc_ref, vp_ref, vc_ref, bucket_ref, rb_ref, sk_ref, o_ref, bias_ref):
        i = pl.program_id(0)

        @pl.when(i == 0)
        def _():
            _swa_build_bias(bucket_ref, rb_ref, bias_ref, n_groups, per_group)

        lane = lax.broadcasted_iota(jnp.int32, (1, LANES), 1)
        first = lane < HEAD_DIM
        first_mask = _swa_first_block_mask(i)
        kcat = jnp.concatenate([kp_ref[...], kc_ref[...]], axis=0)
        vcat = jnp.concatenate([vp_ref[...], vc_ref[...]], axis=0)
        pieces = {}
        for g in range(n_groups):
            sel = first if g == 0 else jnp.logical_not(first)
            prob, _ = _swa_probs(_swa_stack(q_ref, g, per_group, sel, SCALE), kcat, bias_ref[g], first_mask,
                                 _swa_sink_rows(sk_ref, g, per_group))
            _swa_unstack(_dot(prob.astype(BF16), vcat), g, per_group, pieces)
        for j in range(n_heads // 2):
            o_ref[:, j * LANES:(j + 1) * LANES] = jnp.where(first, pieces[2 * j], pieces[2 * j + 1])

    return pl.pallas_call(
        body, name="swa_fwd", grid=(s // SWA_BLOCK,),
        in_specs=_swa_specs(n_heads, qcol, kcol, vcol),
        out_specs=pl.BlockSpec((SWA_BLOCK, width), lambda i: (i, 0)),
        out_shape=jax.ShapeDtypeStruct((s, width), F32),
        scratch_shapes=[pltpu.VMEM((n_groups, per_group * SWA_BLOCK, 2 * SWA_BLOCK), F32)],
        compiler_params=_params(),
    )(proj, proj, proj, proj, proj, bucket, rel_bias, sinks)


def _rms_fwd(o, g):
    r = lax.rsqrt(jnp.mean(o * o, axis=-1, keepdims=True) + RMS_EPS)
    n = o * r
    return n, r, n * g


def _mix_ln1(sb_out, sw_out, x, g_in, b_in, sb_g, sw_g, w_out):
    s, d = x.shape
    wsb, wsw = sb_out.shape[1], sw_out.shape[1]
    tm = min(ROW_TILE, s)

    def body(sb_ref, sw_ref, x_ref, gi_ref, bi_ref, sbg_ref, swg_ref, w_ref, mg_ref, u_ref):
        _, _, m_sb = _rms_fwd(sb_ref[...], sbg_ref[...])
        _, _, m_sw = _rms_fwd(sw_ref[...], swg_ref[...])
        m_sb = m_sb.astype(BF16)
        m_sw = m_sw.astype(BF16)
        mg_ref[:, :wsb] = m_sb
        mg_ref[:, wsb:] = m_sw
        mix = _dot(m_sb, w_ref[:wsb, :]) + _dot(m_sw, w_ref[wsb:, :])
        xhat, _ = _ln_hat(x_ref[...])
        h0 = xhat * gi_ref[...] + bi_ref[...]
        u_ref[...] = ALPHA * h0 + mix

    row = lambda width: pl.BlockSpec((tm, width), lambda i: (i, 0))
    return pl.pallas_call(
        body, name="mix_ln1", grid=(s // tm,),
        in_specs=[row(wsb), row(wsw), row(d), _full((1, d)), _full((1, d)),
                  _full((1, wsb)), _full((1, wsw)), _full((wsb + wsw, d))],
        out_specs=[row(wsb + wsw), row(d)],
        out_shape=[jax.ShapeDtypeStruct((s, wsb + wsw), BF16), jax.ShapeDtypeStruct((s, d), F32)],
        compiler_params=_params(),
    )(sb_out, sw_out, x, g_in, b_in, sb_g, sw_g, w_out)


def _ffn_fwd(u1, g1, b1, w_gu, w_down, g2, b2, target):
    s, d = u1.shape
    dff = w_down.shape[0]
    tm = min(FFN_TILE, s)

    def body(u_ref, g1_ref, b1_ref, wgu_hbm, wd_hbm, g2_ref, b2_ref, t_ref, du_ref, st_ref, wgu_ref, wd_ref):
        @pl.when(pl.program_id(0) == 0)
        def _():
            pltpu.sync_copy(wgu_hbm, wgu_ref)
            pltpu.sync_copy(wd_hbm, wd_ref)
            st_ref[...] = jnp.zeros_like(st_ref)

        xhat, _ = _ln_hat(u_ref[...])
        h1 = xhat * g1_ref[...] + b1_ref[...]
        h1b = h1.astype(BF16)
        gate = _dot_nt(h1b, wgu_ref[:dff, :])
        up = _dot_nt(h1b, wgu_ref[dff:, :])
        act = gate * jax.nn.sigmoid(gate) * up
        u2 = ALPHA * h1 + _dot(act.astype(BF16), wd_ref[...])
        xhat2, r2 = _ln_hat(u2)
        diff = xhat2 * g2_ref[...] + b2_ref[...] - t_ref[...]
        dh2 = diff * (1.0 / d)
        st_ref[0:1, :] += _colsum(dh2 * xhat2)
        st_ref[1:2, :] += _colsum(dh2)
        st_ref[2:3, :] += jnp.broadcast_to(_colsum(_rowsum(diff * diff)) * (0.5 / d), (1, d))
        du_ref[...] = _ln_bwd(dh2 * g2_ref[...], xhat2, r2)

    row = pl.BlockSpec((tm, d), lambda i: (i, 0))
    hbm = pl.BlockSpec(memory_space=pl.ANY)
    return pl.pallas_call(
        body, name="ffn_fwd", grid=(s // tm,),
        in_specs=[row, _full((1, d)), _full((1, d)), hbm, hbm, _full((1, d)), _full((1, d)), row],
        out_specs=[row, _full((8, d))],
        out_shape=[jax.ShapeDtypeStruct((s, d), F32), jax.ShapeDtypeStruct((8, d), F32)],
        scratch_shapes=[pltpu.VMEM(w_gu.shape, BF16), pltpu.VMEM(w_down.shape, BF16)],
        compiler_params=_params(56),
    )(u1, g1, b1, w_gu, w_down, g2, b2, target)


def _ffn_bwd_act(u1, du2, g1, b1, w_gu, w_down_t):
    s, d = u1.shape
    dff = w_down_t.shape[0]
    tm = min(FFN_TILE, s)

    def body(u_ref, du2_ref, g1_ref, b1_ref, wgu_hbm, wdt_hbm, h1b_ref, act_ref, dgu_ref, wgu_ref, wdt_ref):
        @pl.when(pl.program_id(0) == 0)
        def _():
            pltpu.sync_copy(wgu_hbm, wgu_ref)
            pltpu.sync_copy(wdt_hbm, wdt_ref)

        xhat, _ = _ln_hat(u_ref[...])
        h1b = (xhat * g1_ref[...] + b1_ref[...]).astype(BF16)
        h1b_ref[...] = h1b
        gate = _dot_nt(h1b, wgu_ref[:dff, :])
        up = _dot_nt(h1b, wgu_ref[dff:, :])
        dact = _dot_nt(du2_ref[...].astype(BF16), wdt_ref[...])
        sg = jax.nn.sigmoid(gate)
        silu = gate * sg
        act_ref[...] = (silu * up).astype(BF16)
        dgu_ref[:, :dff] = (dact * up * (sg * (1.0 + gate * (1.0 - sg)))).astype(BF16)
        dgu_ref[:, dff:] = (dact * silu).astype(BF16)

    row = lambda width: pl.BlockSpec((tm, width), lambda i: (i, 0))
    hbm = pl.BlockSpec(memory_space=pl.ANY)
    return pl.pallas_call(
        body, name="ffn_bwd_act", grid=(s // tm,),
        in_specs=[row(d), row(d), _full((1, d)), _full((1, d)), hbm, hbm],
        out_specs=[row(d), row(dff), row(2 * dff)],
        out_shape=[jax.ShapeDtypeStruct((s, d), BF16), jax.ShapeDtypeStruct((s, dff), BF16),
                   jax.ShapeDtypeStruct((s, 2 * dff), BF16)],
        scratch_shapes=[pltpu.VMEM(w_gu.shape, BF16), pltpu.VMEM(w_down_t.shape, BF16)],
        compiler_params=_params(56),
    )(u1, du2, g1, b1, w_gu, w_down_t)


def _ffn_bwd_in(dgu, w_gu_t, u1, du2, g1):
    s, d = u1.shape
    tm = min(FFN_TILE, s)

    def body(dgu_ref, wt_hbm, u_ref, du2_ref, g1_ref, du1_ref, st_ref, wt_ref):
        @pl.when(pl.program_id(0) == 0)
        def _():
            pltpu.sync_copy(wt_hbm, wt_ref)
            st_ref[...] = jnp.zeros_like(st_ref)

        dh1 = _dot(dgu_ref[...], wt_ref[...]) + ALPHA * du2_ref[...]
        xhat, r = _ln_hat(u_ref[...])
        st_ref[0:1, :] += _colsum(dh1 * xhat)
        st_ref[1:2, :] += _colsum(dh1)
        du1_ref[...] = _ln_bwd(dh1 * g1_ref[...], xhat, r)

    row = lambda width: pl.BlockSpec((tm, width), lambda i: (i, 0))
    return pl.pallas_call(
        body, name="ffn_bwd_in", grid=(s // tm,),
        in_specs=[row(dgu.shape[1]), pl.BlockSpec(memory_space=pl.ANY), row(d), row(d), _full((1, d))],
        out_specs=[row(d), _full((8, d))],
        out_shape=[jax.ShapeDtypeStruct((s, d), F32), jax.ShapeDtypeStruct((8, d), F32)],
        scratch_shapes=[pltpu.VMEM(w_gu_t.shape, BF16)],
        compiler_params=_params(56),
    )(dgu, w_gu_t, u1, du2, g1)


def _rms_bwd(dm, o, g):
    n, r, _ = _rms_fwd(o, g)
    dn = dm * g
    return r * (dn - n * jnp.mean(dn * n, axis=-1, keepdims=True)), _colsum(dm * n)


def _mix_bwd(du1, w_out, sb_out, sw_out, sb_g, sw_g):
    s, d = du1.shape
    wsb, wsw = sb_out.shape[1], sw_out.shape[1]
    tm = min(ROW_TILE, s)

    def body(du_ref, w_ref, sb_ref, sw_ref, sbg_ref, swg_ref, dsb_ref, dsw_ref, st_ref):
        i = pl.program_id(0)

        @pl.when(i == 0)
        def _():
            st_ref[...] = jnp.zeros_like(st_ref)

        dmerged = _dot_nt(du_ref[...].astype(BF16), w_ref[...])
        dsb, gsb = _rms_bwd(dmerged[:, :wsb], sb_ref[...], sbg_ref[...])
        dsw, gsw = _rms_bwd(dmerged[:, wsb:], sw_ref[...], swg_ref[...])
        dsb_ref[...] = dsb.astype(BF16)
        dsw_ref[...] = dsw.astype(BF16)
        st_ref[0:1, :wsb] += gsb
        st_ref[0:1, wsb:] += gsw

    row = lambda width: pl.BlockSpec((tm, width), lambda i: (i, 0))
    return pl.pallas_call(
        body, name="mix_bwd", grid=(s // tm,),
        in_specs=[row(d), _full((wsb + wsw, d)), row(wsb), row(wsw), _full((1, wsb)), _full((1, wsw))],
        out_specs=[row(wsb), row(wsw), _full((8, wsb + wsw))],
        out_shape=[jax.ShapeDtypeStruct((s, wsb), BF16), jax.ShapeDtypeStruct((s, wsw), BF16),
                   jax.ShapeDtypeStruct((8, wsb + wsw), F32)],
        compiler_params=_params(),
    )(du1, w_out, sb_out, sw_out, sb_g, sw_g)


def _sb_bwd(proj, dout, out, n_pairs, comm=None):
    s = proj.shape[0]
    t = min(SB_TILE, s)
    nq = s // t
    width = n_pairs * LANES

    def body(q_ref, k_ref, v_ref, do_ref, o_ref, dq_ref, dk_out, dv_out, dk_ref, dv_ref):
        i = pl.program_id(1)

        @pl.when(i == 0)
        def _():
            dk_ref[...] = jnp.zeros_like(dk_ref)
            dv_ref[...] = jnp.zeros_like(dv_ref)

        lane = lax.broadcasted_iota(jnp.int32, (1, LANES), 1)
        first = lane < HEAD_DIM
        do2 = do_ref[...]
        qs = _sb_stack_heads(q_ref[...] * SCALE, first)
        dos = _sb_stack_heads(do2, first)
        prod = do2.astype(F32) * o_ref[...]
        totals = jnp.concatenate([_rowsum(jnp.where(first, prod, 0.0)), _rowsum(jnp.where(first, 0.0, prod))], axis=0)
        upper, incl = _sb_triangles(t)

        def visit(offs, carry, mask):
            k_t = _sb_key_tiles(k_ref, offs, t)
            v_t = _sb_key_tiles(v_ref, offs, t)
            c_l, c_e, dq = carry
            lb, a, c_l = _sb_scores(qs, k_t, upper, c_l, mask, t)
            a_b = a.astype(BF16)
            d_e = _dot_nt(dos, v_t) * a_b.astype(F32)
            d_eb = d_e.astype(BF16)
            cols = [slice(j * t, (j + 1) * t) for j in range(len(offs))]
            suf_e, c_e = _sb_suffix([[d_eb[:, c]] for c in cols], [_rowsum(d_e[:, c]) for c in cols], incl, c_e)
            dz = d_e - jnp.exp(lb) * (d_e + (totals - suf_e))
            if mask is not None:
                dz = jnp.where(mask, dz, 0.0)
            dzb = dz.astype(BF16)
            dk_t = _dot_tn(dzb, qs)
            dv_t = _dot_tn(a_b, dos)
            for off, c in zip(offs, cols):
                dk_ref[pl.ds(off, t), :] += dk_t[c, :]
                dv_ref[pl.ds(off, t), :] += dv_t[c, :]
            return c_l, c_e, dq + _dot(dzb, k_t)

        init = (jnp.zeros((2 * t, 1), F32), jnp.zeros((2 * t, 1), F32), jnp.zeros((2 * t, LANES), F32))
        _, _, dq = _sb_walk(i, t, visit, init)
        dq_ref[...] = (jnp.where(first, dq[:t], dq[t:]) * SCALE).astype(BF16)

        @pl.when(i == nq - 1)
        def _():
            dk_out[...] = dk_ref[...].astype(BF16)
            dv_out[...] = dv_ref[...].astype(BF16)

    qblk = pl.BlockSpec((t, LANES), lambda h, i: (i, h))
    whole = pl.BlockSpec((s, LANES), lambda h, i: (0, h))
    return _call(
        body, "sb_bwd", (n_pairs, nq),
        in_specs=[qblk,
                  pl.BlockSpec((s, LANES), lambda h, i: (0, n_pairs + h)),
                  pl.BlockSpec((s, LANES), lambda h, i: (0, 2 * n_pairs + h)),
                  qblk, qblk],
        out_specs=[qblk, whole, whole],
        out_shape=[jax.ShapeDtypeStruct((s, width), BF16)] * 3,
        args=(proj, proj, proj, dout, out),
        scratch_shapes=[pltpu.VMEM((s, LANES), F32), pltpu.VMEM((s, LANES), F32)], comm=comm)


def _swa_bwd(proj, dout, bucket, rel_bias, sinks, n_heads, qcol, kcol, vcol, comm=None):
    s = proj.shape[0]
    width = n_heads * HEAD_DIM
    n_groups = LANES // HEAD_DIM
    per_group = n_heads // n_groups
    nb = s // SWA_BLOCK

    def body(q_ref, kp_ref, kc_ref, vp_ref, vc_ref, bucket_ref, rb_ref, sk_ref, do_ref,
             dq_ref, dk_out, dv_out, dsk_ref, drb_ref, bias_ref, dbias_ref, dk_ref, dv_ref):
        i = pl.program_id(0)

        @pl.when(i == 0)
        def _():
            _swa_build_bias(bucket_ref, rb_ref, bias_ref, n_groups, per_group)
            dbias_ref[...] = jnp.zeros_like(dbias_ref)
            dk_ref[...] = jnp.zeros_like(dk_ref)
            dv_ref[...] = jnp.zeros_like(dv_ref)
            dsk_ref[...] = jnp.zeros_like(dsk_ref)

        lane = lax.broadcasted_iota(jnp.int32, (1, LANES), 1)
        first = lane < HEAD_DIM
        first_mask = _swa_first_block_mask(i)
        kcat = jnp.concatenate([kp_ref[...], kc_ref[...]], axis=0)
        vcat = jnp.concatenate([vp_ref[...], vc_ref[...]], axis=0)
        dkcat = jnp.zeros((2 * SWA_BLOCK, LANES), F32)
        dvcat = jnp.zeros((2 * SWA_BLOCK, LANES), F32)
        pieces = {}
        for g in range(n_groups):
            sel = first if g == 0 else jnp.logical_not(first)
            q_g = _swa_stack(q_ref, g, per_group, sel, SCALE)
            do_g = _swa_stack(do_ref, g, per_group, sel)
            prob, p_sink = _swa_probs(q_g, kcat, bias_ref[g], first_mask, _swa_sink_rows(sk_ref, g, per_group))
            dprob = _dot_nt(do_g, vcat)
            delta = _rowsum(prob * dprob)
            dlog = prob * (dprob - delta)
            sink_term = p_sink * delta
            for hh in range(per_group):
                h = g * per_group + hh
                tot = _colsum(sink_term[hh * SWA_BLOCK:(hh + 1) * SWA_BLOCK, :])
                dsk_ref[h:h + 1, :] += jnp.broadcast_to(-tot, (1, LANES))
            dbias_ref[g] += dlog
            dlb = dlog.astype(BF16)
            _swa_unstack(_dot(dlb, kcat) * SCALE, g, per_group, pieces)
            dkcat += _dot_tn(dlb, q_g)
            dvcat += _dot_tn(prob.astype(BF16), do_g)
        for j in range(n_heads // 2):
            dq_ref[:, j * LANES:(j + 1) * LANES] = jnp.where(first, pieces[2 * j], pieces[2 * j + 1]).astype(BF16)

        cur = pl.multiple_of(i * SWA_BLOCK, SWA_BLOCK)
        dk_ref[pl.ds(cur, SWA_BLOCK), :] += dkcat[SWA_BLOCK:, :]
        dv_ref[pl.ds(cur, SWA_BLOCK), :] += dvcat[SWA_BLOCK:, :]

        @pl.when(i > 0)
        def _():
            prv = pl.multiple_of((i - 1) * SWA_BLOCK, SWA_BLOCK)
            dk_ref[pl.ds(prv, SWA_BLOCK), :] += dkcat[:SWA_BLOCK, :]
            dv_ref[pl.ds(prv, SWA_BLOCK), :] += dvcat[:SWA_BLOCK, :]

        @pl.when(i == nb - 1)
        def _():
            bk = bucket_ref[...]
            rowi = lax.broadcasted_iota(jnp.int32, (REL_BUCKETS, LANES), 0)
            coli = lax.broadcasted_iota(jnp.int32, (REL_BUCKETS, LANES), 1)
            res = jnp.zeros((REL_BUCKETS, LANES), F32)
            for h in range(n_heads):
                g, hh = divmod(h, per_group)
                db = dbias_ref[g, hh * SWA_BLOCK:(hh + 1) * SWA_BLOCK, :]
                for b in range(REL_BUCKETS):
                    tot = _colsum(_rowsum(jnp.where(bk == b, db, 0.0)))
                    res = jnp.where((rowi == b) & (coli == h), tot, res)
            drb_ref[...] = res
            dk_out[...] = dk_ref[...].astype(BF16)
            dv_out[...] = dv_ref[...].astype(BF16)

    in_specs = _swa_specs(n_heads, qcol, kcol, vcol) + [pl.BlockSpec((SWA_BLOCK, width), lambda i: (i, 0))]
    return _call(
        body, "swa_bwd", (nb,),
        in_specs=in_specs,
        out_specs=[pl.BlockSpec((SWA_BLOCK, width), lambda i: (i, 0)),
                   _full((s, LANES)), _full((s, LANES)), _full((8, LANES)), _full((REL_BUCKETS, LANES))],
        out_shape=[jax.ShapeDtypeStruct((s, width), BF16), jax.ShapeDtypeStruct((s, LANES), BF16),
                   jax.ShapeDtypeStruct((s, LANES), BF16), jax.ShapeDtypeStruct((8, LANES), F32),
                   jax.ShapeDtypeStruct((REL_BUCKETS, LANES), F32)],
        args=(proj, proj, proj, proj, proj, bucket, rel_bias, sinks, dout),
        scratch_shapes=[pltpu.VMEM((n_groups, per_group * SWA_BLOCK, 2 * SWA_BLOCK), F32),
                        pltpu.VMEM((n_groups, per_group * SWA_BLOCK, 2 * SWA_BLOCK), F32),
                        pltpu.VMEM((s, LANES), F32), pltpu.VMEM((s, LANES), F32)],
        comm=comm)


def _proj_bwd(pieces, w_in_t, du1, x, g_in, comm=None):
    s, d = x.shape
    cols = w_in_t.shape[0]
    tm = min(ROW_TILE, s)
    n_p = len(pieces)

    def body(*refs):
        p_refs = refs[:n_p]
        w_ref, du_ref, x_ref, g_ref, dx_ref, st_ref = refs[n_p:]
        i = pl.program_id(0)

        @pl.when(i == 0)
        def _():
            st_ref[...] = jnp.zeros_like(st_ref)

        dproj = jnp.concatenate([p[...] for p in p_refs], axis=1)
        dh0 = _dot(dproj, w_ref[...]) + ALPHA * du_ref[...]
        xhat, r = _ln_hat(x_ref[...])
        st_ref[0:1, :] += _colsum(dh0 * xhat)
        st_ref[1:2, :] += _colsum(dh0)
        dx_ref[...] = _ln_bwd(dh0 * g_ref[...], xhat, r)

    row = lambda width: pl.BlockSpec((tm, width), lambda i: (i, 0))
    return _call(
        body, "proj_bwd", (s // tm,),
        in_specs=[row(p.shape[1]) for p in pieces] + [_full((cols, d)), row(d), row(d), _full((1, d))],
        out_specs=[row(d), _full((8, d))],
        out_shape=[jax.ShapeDtypeStruct((s, d), F32), jax.ShapeDtypeStruct((8, d), F32)],
        args=(*pieces, w_in_t, du1, x, g_in), comm=comm)


def _wgrad(name, pieces, b, tm, tn):
    s, n = b.shape
    m = sum(p.shape[1] for p in pieces)
    n_p = len(pieces)
    assert n_p == 1 or tm == m
    ts = min(WGRAD_TOKENS if b.dtype == BF16 and n_p == 1 else WGRAD_TOKENS // 2, s)
    n_k = s // ts

    def body(*refs):
        p_refs, b_ref, o_ref, acc_ref = refs[:n_p], refs[n_p], refs[n_p + 1], refs[n_p + 2]
        k = pl.program_id(2)

        @pl.when(k == 0)
        def _():
            acc_ref[...] = jnp.zeros_like(acc_ref)

        a = p_refs[0][...] if n_p == 1 else jnp.concatenate([p[...] for p in p_refs], axis=1)
        acc_ref[...] += _dot_tn(a, b_ref[...].astype(BF16))

        @pl.when(k == n_k - 1)
        def _():
            o_ref[...] = acc_ref[...].astype(BF16)

    piece_spec = lambda p: pl.BlockSpec((ts, tm if n_p == 1 else p.shape[1]), lambda i, j, k: (k, i))
    return pl.pallas_call(
        body, name=name, grid=(m // tm, n // tn, n_k),
        in_specs=[piece_spec(p) for p in pieces] + [pl.BlockSpec((ts, tn), lambda i, j, k: (k, j))],
        out_specs=pl.BlockSpec((tm, tn), lambda i, j, k: (i, j)),
        out_shape=jax.ShapeDtypeStruct((m, n), BF16),
        scratch_shapes=[pltpu.VMEM((tm, tn), F32)],
        compiler_params=_params(),
    )(*pieces, b)


def _adamw_math(w, g, m, v):
    m = ADAM_B1 * m + (1.0 - ADAM_B1) * g
    v = ADAM_B2 * v + (1.0 - ADAM_B2) * (g * g)
    m_hat = m / (1.0 - ADAM_B1 ** ADAM_STEP)
    v_hat = v / (1.0 - ADAM_B2 ** ADAM_STEP)
    delta = -ADAM_LR * (m_hat / (jnp.sqrt(v_hat) + ADAM_EPS) + ADAM_WD * w)
    return delta, m, v


def _adamw_rows(rows):
    return max(r for r in range(16, 257, 16) if rows % r == 0)


def _adamw(name, landed, w, m, v, tr):
    rows, cols = w.shape

    def body(l_ref, w_ref, m_ref, v_ref, g_ref, d_ref, nm_ref, nv_ref):
        g = l_ref[0].astype(F32)
        for src in range(1, N_DEV):
            g = g + l_ref[src].astype(F32)
        delta, nm, nv = _adamw_math(w_ref[...], g, m_ref[...], v_ref[...])
        g_ref[...] = g
        d_ref[...] = delta
        nm_ref[...] = nm
        nv_ref[...] = nv

    blk = pl.BlockSpec((tr, cols), lambda i: (i, 0))
    shape = jax.ShapeDtypeStruct((rows, cols), F32)
    return pl.pallas_call(
        body, name=name, grid=(rows // tr,),
        in_specs=[pl.BlockSpec((N_DEV, tr, cols), lambda i: (0, i, 0)), blk, blk, blk],
        out_specs=[blk, blk, blk, blk],
        out_shape=[shape, shape, shape, shape],
        compiler_params=_params(),
    )(landed, w, m, v)


def _pack(d, ln_in_g, ln_in_b, ln1_g, ln1_b, ln2_g, ln2_b, sb_g, sw_g, rel_bias, sinks, extra=None):
    tail = [rel_bias.reshape(-1), sinks.reshape(-1)]
    if extra is not None:
        tail.append(extra.reshape(-1))
    tail = jnp.concatenate(tail)
    tail = jnp.concatenate([tail, jnp.zeros((d - tail.shape[0],), F32)])
    rows = [ln_in_g.reshape(-1), ln_in_b.reshape(-1), ln1_g.reshape(-1), ln1_b.reshape(-1),
            ln2_g.reshape(-1), ln2_b.reshape(-1),
            jnp.concatenate([sb_g.reshape(-1), sw_g.reshape(-1)]), tail]
    return jnp.stack(rows)


def _unpack(p, wsb, n_rb, n_sk):
    return [p[0], p[1], p[6, :wsb][None], p[6, wsb:][None], p[7, n_rb:n_rb + n_sk][None],
            p[7, :n_rb].reshape(REL_BUCKETS, -1), p[2][None], p[3][None], p[4][None], p[5][None]]


def kernel(x, ln_in_g, ln_in_b, w_in, sb_norm_g, swa_norm_g, sinks, rel_bias, w_out, ln1_g, ln1_b, w_gate_up, w_down, ln2_g, ln2_b, loss_target, m_ln_in_g, m_ln_in_b, m_w_in, m_sb_norm_g, m_swa_norm_g, m_sinks, m_rel_bias, m_w_out, m_ln1_g, m_ln1_b, m_w_gate_up, m_w_down, m_ln2_g, m_ln2_b, v_ln_in_g, v_ln_in_b, v_w_in, v_sb_norm_g, v_swa_norm_g, v_sinks, v_rel_bias, v_w_out, v_ln1_g, v_ln1_b, v_w_gate_up, v_w_down, v_ln2_g, v_ln2_b):
    x2 = x[0]
    tgt = loss_target[0]
    s, d = x2.shape
    wsb = sb_norm_g.shape[-1]
    wsw = swa_norm_g.shape[-1]
    n_sw_heads = sinks.shape[-1]
    n_pairs = wsb // LANES
    dff = w_down.shape[1] * N_DEV
    assert wsb % LANES == 0 and wsw % LANES == 0 and n_sw_heads * HEAD_DIM == wsw
    assert 3 * wsb % wsw == 0 and dff % LANES == 0 and s % SWA_BLOCK == 0
    qcol = 3 * wsb // wsw
    kcol = (3 * wsb + wsw) // LANES
    vcol = kcol + 1
    assert w_in.shape[-1] * N_DEV == (vcol + 1) * LANES

    t2 = lambda a: jnp.transpose(a[0])
    big_w = [t2(w_in), w_out[0], t2(w_gate_up), w_down[0]]
    big_m = [t2(m_w_in), m_w_out[0], t2(m_w_gate_up), m_w_down[0]]
    big_v = [t2(v_w_in), v_w_out[0], t2(v_w_gate_up), v_w_down[0]]

    cat_rows = lambda g: g.reshape(N_DEV * g.shape[1], g.shape[2])
    shards = [w.astype(BF16) for w in big_w]
    w_in_t = cat_rows(_exchange("w_in_allgather", shards[:1], ["gather"])[0])

    vec = lambda a: a.reshape(1, -1)
    g_in, b_in = vec(ln_in_g), vec(ln_in_b)
    bucket = jnp.asarray(_swa_bucket_table())

    h0b, proj = _ln_proj(x2, g_in, b_in, w_in_t)
    sb_out, gathered = _sb_fwd(proj, n_pairs, comm=(shards[1:], ["gather"] * 3))
    w_out_f, w_gu_t, w_down_f = cat_rows(gathered[0]), cat_rows(gathered[1]), cat_rows(gathered[2])
    sw_out = _swa_fwd(proj, bucket, rel_bias, sinks, n_sw_heads, qcol, kcol, vcol)
    merged, u1 = _mix_ln1(sb_out, sw_out, x2, g_in, b_in, sb_norm_g, swa_norm_g, w_out_f)
    du2, st_ln2 = _ffn_fwd(u1, ln1_g, ln1_b, w_gu_t, w_down_f, ln2_g, ln2_b, tgt)

    split_rows = lambda g: g.reshape(N_DEV, g.shape[0] // N_DEV, g.shape[1])
    h1b, act, dgu = _ffn_bwd_act(u1, du2, ln1_g, ln1_b, w_gu_t, w_down_f)
    du1, st_ln1 = _ffn_bwd_in(dgu, w_gu_t, u1, du2, ln1_g)
    gw_gu = _wgrad("wgrad_gate_up", [dgu], h1b, dff // 2, d)
    gw_down = _wgrad("wgrad_down", [act], du2, dff // 2, d)
    gw_out = _wgrad("wgrad_out", [merged], du1, min(512, d), d)
    dsb, dsw, st_rms = _mix_bwd(du1, w_out_f, sb_out, sw_out, sb_norm_g, swa_norm_g)
    (dq_sb, dk_sb, dv_sb), (land_gu, land_out) = _sb_bwd(
        proj, dsb, sb_out, n_pairs, comm=([split_rows(gw_gu), split_rows(gw_out)], ["scatter"] * 2))
    (dq_sw, dk_sw, dv_sw, st_sink, st_rb), (land_down,) = _swa_bwd(
        proj, dsw, bucket, rel_bias, sinks, n_sw_heads, qcol, kcol, vcol,
        comm=([split_rows(gw_down)], ["scatter"]))
    pieces = [dq_sb, dk_sb, dv_sb, dq_sw, dk_sw, dv_sw]
    gw_in = _wgrad("wgrad_in", pieces, h0b, proj.shape[1], d)
    (grad_x, st_in), (land_in,) = _proj_bwd(pieces, w_in_t, du1, x2, g_in, comm=([split_rows(gw_in)], ["scatter"]))

    n_rb = rel_bias.size
    small = _pack(d, st_in[0], st_in[1], st_ln1[0], st_ln1[1], st_ln2[0], st_ln2[1],
                  st_rms[0, :wsb], st_rms[0, wsb:], st_rb[:, :n_sw_heads], st_sink[:n_sw_heads, 0],
                  extra=st_ln2[2, 0:1])
    land_small = _exchange("small_grads_allgather", [small], ["gather"])[0]
    landed = [land_in, land_out, land_gu, land_down, land_small]

    big = []
    for name, land, w, m, v in zip(["adamw_in", "adamw_out", "adamw_gate_up", "adamw_down"], landed[:4], big_w, big_m, big_v):
        big.append(_adamw(name, land, w, m, v, _adamw_rows(w.shape[0])))

    small_w = _pack(d, ln_in_g, ln_in_b, ln1_g, ln1_b, ln2_g, ln2_b, sb_norm_g, swa_norm_g, rel_bias, sinks)
    small_m = _pack(d, m_ln_in_g, m_ln_in_b, m_ln1_g, m_ln1_b, m_ln2_g, m_ln2_b, m_sb_norm_g, m_swa_norm_g, m_rel_bias, m_sinks)
    small_v = _pack(d, v_ln_in_g, v_ln_in_b, v_ln1_g, v_ln1_b, v_ln2_g, v_ln2_b, v_sb_norm_g, v_swa_norm_g, v_rel_bias, v_sinks)
    sg, sd, sm, sv = _adamw("adamw_small", landed[4], small_w, small_m, small_v, 8)
    n_sk = sinks.size
    loss = sg[7, n_rb + n_sk]

    def leaves(idx):
        sm_l = _unpack([sg, sd, sm, sv][idx], wsb, n_rb, n_sk)
        bg = [jnp.transpose(big[0][idx])[None], big[1][idx][None], jnp.transpose(big[2][idx])[None], big[3][idx][None]]
        return [sm_l[0], sm_l[1], bg[0], sm_l[2], sm_l[3], sm_l[4], sm_l[5], bg[1], sm_l[6], sm_l[7], bg[2], bg[3], sm_l[8], sm_l[9]]

    return (loss, grad_x[None], *leaves(0), *leaves(1), *leaves(2), *leaves(3))
```

```python
import functools
import math

import numpy as np
import jax
import jax.numpy as jnp
from jax import lax
from jax.experimental import pallas as pl
from jax.experimental.pallas import tpu as pltpu

F32 = jnp.float32
BF16 = jnp.bfloat16
MESH = pl.DeviceIdType.MESH

N_DEV = 8
LANES = 128
HEAD_DIM = 64
SCALE = HEAD_DIM ** -0.5
SWA_BLOCK = 128
REL_BUCKETS = 32
REL_MAX_DIST = 128
ALPHA = 2.0 ** 0.25
LN_EPS = 1e-5
RMS_EPS = 1e-6
ADAM_LR = 0.001
ADAM_B1 = 0.9
ADAM_B2 = 0.999
ADAM_EPS = 1e-08
ADAM_WD = 0.01
ADAM_STEP = 10

ROW_TILE = 512
SB_TILE = 256
FFN_TILE = 256
WGRAD_TOKENS = 2048
SB_UNDERFLOW = -110.0
MIB = 1024 * 1024


def _params(vmem_mib=48):
    return pltpu.CompilerParams(vmem_limit_bytes=vmem_mib * MIB)


def _dot(a, b):
    return jnp.dot(a, b, preferred_element_type=F32)


def _dot_nt(a, b):
    return lax.dot_general(a, b, (((1,), (1,)), ((), ())), preferred_element_type=F32)


def _dot_tn(a, b):
    return lax.dot_general(a, b, (((0,), (0,)), ((), ())), preferred_element_type=F32)


def _ln_hat(x):
    mu = jnp.mean(x, axis=-1, keepdims=True)
    xc = x - mu
    var = jnp.mean(xc * xc, axis=-1, keepdims=True)
    r = lax.rsqrt(var + LN_EPS)
    return xc * r, r


def _ln_bwd(dxhat, xhat, r):
    return r * (dxhat - jnp.mean(dxhat, axis=-1, keepdims=True)
                - xhat * jnp.mean(dxhat * xhat, axis=-1, keepdims=True))


def _colsum(a):
    return jnp.sum(a, axis=0, keepdims=True)


def _rowsum(a):
    return jnp.sum(a, axis=1, keepdims=True)


def _full(shape):
    return pl.BlockSpec(shape, lambda *_: (0,) * len(shape))


def _comm_out_shapes(arrays, kinds):
    shapes = []
    for a, kind in zip(arrays, kinds):
        blk = a.shape if kind == "gather" else a.shape[1:]
        shapes.append(jax.ShapeDtypeStruct((N_DEV,) + tuple(blk), a.dtype))
    return shapes


def _comm_sems(n):
    return [pltpu.SemaphoreType.DMA((n, N_DEV - 1)), pltpu.SemaphoreType.DMA((n, N_DEV - 1)),
            pltpu.SemaphoreType.DMA((n,))]


def _comm_copies(ins, outs, kinds, send_sems, recv_sems, local_sems):
    x, y, c = lax.axis_index("x"), lax.axis_index("y"), lax.axis_index("c")
    me = 4 * x + 2 * y + c

    def src_for(t, dev_lin):
        return ins[t] if kinds[t] == "gather" else ins[t].at[dev_lin]

    local = [pltpu.make_async_copy(src_for(t, me), outs[t].at[me], local_sems.at[t]) for t in range(len(kinds))]
    sends, arrivals = [], []
    for k in range(1, N_DEV):
        px = 1 - x if (k >> 2) & 1 else x
        py = 1 - y if (k >> 1) & 1 else y
        pc = 1 - c if k & 1 else c
        peer_lin = 4 * px + 2 * py + pc
        for t in range(len(kinds)):
            sems = dict(send_sem=send_sems.at[t, k - 1], recv_sem=recv_sems.at[t, k - 1],
                        device_id=(px, py, pc), device_id_type=MESH)
            sends.append(pltpu.make_async_remote_copy(src_ref=src_for(t, peer_lin), dst_ref=outs[t].at[me], **sems))
            arrivals.append(pltpu.make_async_remote_copy(src_ref=src_for(t, peer_lin), dst_ref=outs[t].at[peer_lin], **sems))
    return local, sends, arrivals


def _comm_start(ins, outs, kinds, sems):
    local, sends, _ = _comm_copies(ins, outs, kinds, *sems)
    for cp in local + sends:
        cp.start()


def _comm_finish(ins, outs, kinds, sems):
    local, sends, arrivals = _comm_copies(ins, outs, kinds, *sems)
    for cp in arrivals:
        cp.wait_recv()
    for cp in sends:
        cp.wait_send()
    for cp in local:
        cp.wait()


def _exchange(name, arrays, kinds):
    n = len(arrays)

    def body(*refs):
        ins, outs, sems = refs[:n], refs[n:2 * n], refs[2 * n:]
        _comm_start(ins, outs, kinds, sems)
        _comm_finish(ins, outs, kinds, sems)

    any_spec = pl.BlockSpec(memory_space=pl.ANY)
    return pl.pallas_call(
        body, name=name, out_shape=_comm_out_shapes(arrays, kinds),
        in_specs=[any_spec] * n, out_specs=[any_spec] * n,
        scratch_shapes=_comm_sems(n),
    )(*arrays)


def _call(body, name, grid, in_specs, out_specs, out_shape, args, scratch_shapes=(), comm=None):
    if comm is None:
        outs = pl.pallas_call(body, name=name, grid=grid, in_specs=in_specs, out_specs=out_specs,
                              out_shape=out_shape, scratch_shapes=list(scratch_shapes),
                              compiler_params=_params())(*args)
        return outs, []
    arrays, kinds = comm
    n, n_in, n_out, n_scr = len(arrays), len(in_specs), len(out_specs), len(scratch_shapes)

    def fused(*refs):
        c_in, x_in = refs[:n_in], refs[n_in:n_in + n]
        c_out = refs[n_in + n:n_in + n + n_out]
        x_out = refs[n_in + n + n_out:n_in + 2 * n + n_out]
        rest = refs[n_in + 2 * n + n_out:]
        c_scr, sems = rest[:n_scr], rest[n_scr:]
        ids = [pl.program_id(a) for a in range(len(grid))]
        is_first = functools.reduce(jnp.logical_and, [i == 0 for i in ids])
        is_last = functools.reduce(jnp.logical_and, [i == g - 1 for i, g in zip(ids, grid)])

        @pl.when(is_first)
        def _():
            _comm_start(x_in, x_out, kinds, sems)

        body(*c_in, *c_out, *c_scr)

        @pl.when(is_last)
        def _():
            _comm_finish(x_in, x_out, kinds, sems)

    any_spec = pl.BlockSpec(memory_space=pl.ANY)
    outs = pl.pallas_call(
        fused, name=name, grid=grid,
        in_specs=list(in_specs) + [any_spec] * n, out_specs=list(out_specs) + [any_spec] * n,
        out_shape=list(out_shape) + _comm_out_shapes(arrays, kinds),
        scratch_shapes=list(scratch_shapes) + _comm_sems(n),
        compiler_params=_params())(*args, *arrays)
    return outs[:n_out], outs[n_out:]


def _ln_proj(x, g, b, w_in_t):
    s, d = x.shape
    cols = w_in_t.shape[0]
    tm = min(ROW_TILE, s)

    def body(x_ref, g_ref, b_ref, w_ref, h_ref, p_ref):
        xhat, _ = _ln_hat(x_ref[...])
        h = (xhat * g_ref[...] + b_ref[...]).astype(BF16)
        h_ref[...] = h
        p_ref[...] = _dot_nt(h, w_ref[...]).astype(BF16)

    row = lambda width: pl.BlockSpec((tm, width), lambda i: (i, 0))
    return pl.pallas_call(
        body, name="ln_proj", grid=(s // tm,),
        in_specs=[row(d), _full((1, d)), _full((1, d)), _full((cols, d))],
        out_specs=[row(d), row(cols)],
        out_shape=[jax.ShapeDtypeStruct((s, d), BF16), jax.ShapeDtypeStruct((s, cols), BF16)],
        compiler_params=_params(),
    )(x, g, b, w_in_t)


def _sb_triangles(t):
    row = lax.broadcasted_iota(jnp.int32, (t, t), 0)
    col = lax.broadcasted_iota(jnp.int32, (t, t), 1)
    return (row > col).astype(BF16), (row >= col).astype(BF16)


def _sb_first_mask(t, has_prev):
    qrow = lax.broadcasted_iota(jnp.int32, (2 * t, 2 * t), 0) & (t - 1)
    col = lax.broadcasted_iota(jnp.int32, (2 * t, 2 * t), 1)
    return ((col < t) & has_prev) | ((col >= t) & (col - t < qrow))


def _sb_stack_heads(x2, first):
    zero = jnp.zeros_like(x2)
    return jnp.concatenate([jnp.where(first, x2, zero), jnp.where(first, zero, x2)], axis=0)


def _sb_key_tiles(ref, offs, t):
    tiles = [ref[pl.ds(off, t), :] for off in offs]
    return tiles[0] if len(tiles) == 1 else jnp.concatenate(tiles, axis=0)


def _sb_suffix(terms, row_sums, tri, carry):
    out = [None] * len(terms)
    for j in reversed(range(len(terms))):
        suf = carry
        for op in terms[j]:
            suf = suf + _dot(op, tri)
        out[j] = suf
        carry = carry + row_sums[j]
    return (out[0] if len(out) == 1 else jnp.concatenate(out, axis=1)), carry


def _sb_scores(qh, k_t, upper, carry_l, mask, t):
    z = _dot_nt(qh, k_t)
    sp = jnp.log(1.0 + jnp.exp(-jnp.abs(z)))
    neg = jnp.minimum(z, 0.0)
    lb = neg - sp
    l1 = (neg - z) - sp
    if mask is not None:
        l1 = jnp.where(mask, l1, 0.0)
    hi = l1.astype(BF16)
    lo = (l1 - hi.astype(F32)).astype(BF16)
    cols = [slice(j * t, (j + 1) * t) for j in range(z.shape[1] // t)]
    suf, carry_l = _sb_suffix([[hi[:, c], lo[:, c]] for c in cols], [_rowsum(l1[:, c]) for c in cols], upper, carry_l)
    a = jnp.exp(lb + suf)
    if mask is not None:
        a = jnp.where(mask, a, 0.0)
    return lb, a, carry_l


def _sb_walk(i, t, visit, init):
    def alive(carry):
        return jnp.max(carry[0]) > SB_UNDERFLOW

    prev = pl.multiple_of(jnp.maximum(i - 1, 0) * t, t)
    carry = visit((prev, pl.multiple_of(i * t, t)), init, _sb_first_mask(t, i > 0))

    def cond(state):
        j, go, _ = state
        return (j < i - 1) & go

    def body(state):
        j, _, carry = state
        carry = visit((pl.multiple_of((i - 2 - j) * t, t),), carry, None)
        return j + 1, alive(carry), carry

    return lax.while_loop(cond, body, (jnp.int32(0), alive(carry), carry))[2]


def _sb_fwd(proj, n_pairs, comm=None):
    s = proj.shape[0]
    t = min(SB_TILE, s)
    nq = s // t

    def body(q_ref, k_ref, v_ref, o_ref):
        i = pl.program_id(1)
        lane = lax.broadcasted_iota(jnp.int32, (1, LANES), 1)
        first = lane < HEAD_DIM
        qs = _sb_stack_heads(q_ref[...] * SCALE, first)
        upper, _ = _sb_triangles(t)

        def visit(offs, carry, mask):
            c_l, acc = carry
            _, a, c_l = _sb_scores(qs, _sb_key_tiles(k_ref, offs, t), upper, c_l, mask, t)
            return c_l, acc + _dot(a.astype(BF16), _sb_key_tiles(v_ref, offs, t))

        init = (jnp.zeros((2 * t, 1), F32), jnp.zeros((2 * t, LANES), F32))
        _, acc = _sb_walk(i, t, visit, init)
        o_ref[...] = jnp.where(first, acc[:t], acc[t:])

    outs, landed = _call(
        body, "sb_fwd", (n_pairs, nq),
        in_specs=[pl.BlockSpec((t, LANES), lambda h, i: (i, h)),
                  pl.BlockSpec((s, LANES), lambda h, i: (0, n_pairs + h)),
                  pl.BlockSpec((s, LANES), lambda h, i: (0, 2 * n_pairs + h))],
        out_specs=[pl.BlockSpec((t, LANES), lambda h, i: (i, h))],
        out_shape=[jax.ShapeDtypeStruct((s, n_pairs * LANES), F32)],
        args=(proj, proj, proj), comm=comm)
    return outs[0], landed


def _swa_bucket_table():
    qi = np.arange(SWA_BLOCK)[:, None]
    cj = np.arange(2 * SWA_BLOCK)[None, :]
    dist = qi + SWA_BLOCK - cj
    exact = REL_BUCKETS // 2
    d = np.maximum(dist, 0)
    d_f = np.maximum(d, 1).astype(np.float32)
    large = exact + (np.log(d_f / np.float32(exact)) / np.float32(math.log(REL_MAX_DIST / exact))
                     * np.float32(REL_BUCKETS - exact)).astype(np.int32)
    large = np.minimum(large, REL_BUCKETS - 1)
    return np.where(d < exact, d, large).astype(np.int32)


def _swa_build_bias(bucket_ref, rb_ref, bias_ref, n_groups, per_group):
    bk = bucket_ref[...]
    dist = (lax.broadcasted_iota(jnp.int32, bk.shape, 0) + SWA_BLOCK) - lax.broadcasted_iota(jnp.int32, bk.shape, 1)
    window = (dist >= 0) & (dist < SWA_BLOCK)
    for g in range(n_groups):
        for hh in range(per_group):
            acc = jnp.zeros(bk.shape, F32)
            for b in range(REL_BUCKETS):
                acc = jnp.where(bk == b, rb_ref[b, g * per_group + hh], acc)
            bias_ref[g, hh * SWA_BLOCK:(hh + 1) * SWA_BLOCK, :] = jnp.where(window, acc, -jnp.inf)


def _swa_first_block_mask(i):
    col = lax.broadcasted_iota(jnp.int32, (1, 2 * SWA_BLOCK), 1)
    return jnp.where((col < SWA_BLOCK) & (i == 0), -jnp.inf, 0.0)


def _swa_place(blk, h, group, sel):
    if (h % 2) != group:
        blk = pltpu.roll(blk.astype(F32), HEAD_DIM, axis=1).astype(BF16)
    return jnp.where(sel, blk, jnp.zeros_like(blk))


def _swa_stack(ref, group, per_group, sel, scale=1.0):
    parts = []
    for hh in range(per_group):
        h = group * per_group + hh
        parts.append(_swa_place(ref[:, (h // 2) * LANES:(h // 2 + 1) * LANES], h, group, sel))
    stacked = jnp.concatenate(parts, axis=0)
    return stacked if scale == 1.0 else stacked * scale


def _swa_unstack(stacked, group, per_group, pieces):
    for hh in range(per_group):
        h = group * per_group + hh
        piece = stacked[hh * SWA_BLOCK:(hh + 1) * SWA_BLOCK, :]
        pieces[h] = pltpu.roll(piece, HEAD_DIM, axis=1) if (h % 2) != group else piece


def _swa_sink_rows(sk_ref, group, per_group):
    rowh = lax.broadcasted_iota(jnp.int32, (per_group * SWA_BLOCK, 1), 0) // SWA_BLOCK
    sink = jnp.zeros((per_group * SWA_BLOCK, 1), F32) + sk_ref[0, group * per_group]
    for hh in range(1, per_group):
        sink = jnp.where(rowh == hh, sk_ref[0, group * per_group + hh], sink)
    return sink


def _swa_probs(q_pos, kcat, bias_h, first_mask, sink):
    logits = _dot_nt(q_pos, kcat) + (bias_h + first_mask)
    m = jnp.maximum(jnp.max(logits, axis=1, keepdims=True), sink)
    p = jnp.exp(logits - m)
    es = jnp.exp(sink - m)
    inv = 1.0 / (_rowsum(p) + es)
    return p * inv, es * inv


def _swa_specs(n_heads, qcol, kcol, vcol):
    width = n_heads * HEAD_DIM
    prev = lambda col: pl.BlockSpec((SWA_BLOCK, LANES), lambda i: (jnp.maximum(i - 1, 0), col))
    cur = lambda col: pl.BlockSpec((SWA_BLOCK, LANES), lambda i: (i, col))
    return [pl.BlockSpec((SWA_BLOCK, width), lambda i: (i, qcol)),
            prev(kcol), cur(kcol), prev(vcol), cur(vcol),
            _full((SWA_BLOCK, 2 * SWA_BLOCK)),
            pl.BlockSpec(memory_space=pltpu.SMEM), pl.BlockSpec(memory_space=pltpu.SMEM)]


def _swa_fwd(proj, bucket, rel_bias, sinks, n_heads, qcol, kcol, vcol):
    s = proj.shape[0]
    width = n_heads * HEAD_DIM
    n_groups = LANES // HEAD_DIM
    per_group = n_heads // n_groups

    def body(q_ref, kp_ref, kc_ref, vp_ref, vc_ref, bucket_ref, rb_ref, sk_ref, o_ref, bias_ref):
        i = pl.program_id(0)

        @pl.when(i == 0)
        def _():
            _swa_build_bias(bucket_ref, rb_ref, bias_ref, n_groups, per_group)

        lane = lax.broadcasted_iota(jnp.int32, (1, LANES), 1)
        first = lane < HEAD_DIM
        first_mask = _swa_first_block_mask(i)
        kcat = jnp.concatenate([kp_ref[...], kc_ref[...]], axis=0)
        vcat = jnp.concatenate([vp_ref[...], vc_ref[...]], axis=0)
        pieces = {}
        for g in range(n_groups):
            sel = first if g == 0 else jnp.logical_not(first)
            prob, _ = _swa_probs(_swa_stack(q_ref, g, per_group, sel, SCALE), kcat, bias_ref[g], first_mask,
                                 _swa_sink_rows(sk_ref, g, per_group))
            _swa_unstack(_dot(prob.astype(BF16), vcat), g, per_group, pieces)
        for j in range(n_heads // 2):
            o_ref[:, j * LANES:(j + 1) * LANES] = jnp.where(first, pieces[2 * j], pieces[2 * j + 1])

    return pl.pallas_call(
        body, name="swa_fwd", grid=(s // SWA_BLOCK,),
        in_specs=_swa_specs(n_heads, qcol, kcol, vcol),
        out_specs=pl.BlockSpec((SWA_BLOCK, width), lambda i: (i, 0)),
        out_shape=jax.ShapeDtypeStruct((s, width), F32),
        scratch_shapes=[pltpu.VMEM((n_groups, per_group * SWA_BLOCK, 2 * SWA_BLOCK), F32)],
        compiler_params=_params(),
    )(proj, proj, proj, proj, proj, bucket, rel_bias, sinks)


def _rms_fwd(o, g):
    r = lax.rsqrt(jnp.mean(o * o, axis=-1, keepdims=True) + RMS_EPS)
    n = o * r
    return n, r, n * g


def _mix_ln1(sb_out, sw_out, x, g_in, b_in, sb_g, sw_g, w_out):
    s, d = x.shape
    wsb, wsw = sb_out.shape[1], sw_out.shape[1]
    tm = min(ROW_TILE, s)

    def body(sb_ref, sw_ref, x_ref, gi_ref, bi_ref, sbg_ref, swg_ref, w_ref, mg_ref, u_ref):
        _, _, m_sb = _rms_fwd(sb_ref[...], sbg_ref[...])
        _, _, m_sw = _rms_fwd(sw_ref[...], swg_ref[...])
        m_sb = m_sb.astype(BF16)
        m_sw = m_sw.astype(BF16)
        mg_ref[:, :wsb] = m_sb
        mg_ref[:, wsb:] = m_sw
        mix = _dot(m_sb, w_ref[:wsb, :]) + _dot(m_sw, w_ref[wsb:, :])
        xhat, _ = _ln_hat(x_ref[...])
        h0 = xhat * gi_ref[...] + bi_ref[...]
        u_ref[...] = ALPHA * h0 + mix

    row = lambda width: pl.BlockSpec((tm, width), lambda i: (i, 0))
    return pl.pallas_call(
        body, name="mix_ln1", grid=(s // tm,),
        in_specs=[row(wsb), row(wsw), row(d), _full((1, d)), _full((1, d)),
                  _full((1, wsb)), _full((1, wsw)), _full((wsb + wsw, d))],
        out_specs=[row(wsb + wsw), row(d)],
        out_shape=[jax.ShapeDtypeStruct((s, wsb + wsw), BF16), jax.ShapeDtypeStruct((s, d), F32)],
        compiler_params=_params(),
    )(sb_out, sw_out, x, g_in, b_in, sb_g, sw_g, w_out)


def _ffn(u1, g1, b1, w_gu_t, w_down, g2, b2, target):
    s, d = u1.shape
    dff = w_down.shape[0]
    tm = min(FFN_TILE, s)

    def body(u_ref, g1_ref, b1_ref, wgu_hbm, wd_hbm, g2_ref, b2_ref, t_ref,
             du1_ref, h1b_ref, act_ref, dgu_ref, du2b_ref, st_ref, wgu_ref, wd_ref):
        @pl.when(pl.program_id(0) == 0)
        def _():
            pltpu.sync_copy(wgu_hbm, wgu_ref)
            pltpu.sync_copy(wd_hbm, wd_ref)
            st_ref[...] = jnp.zeros_like(st_ref)

        xhat1, r1 = _ln_hat(u_ref[...])
        h1 = xhat1 * g1_ref[...] + b1_ref[...]
        h1b = h1.astype(BF16)
        h1b_ref[...] = h1b
        gate = _dot_nt(h1b, wgu_ref[:dff, :])
        up = _dot_nt(h1b, wgu_ref[dff:, :])
        sg = jax.nn.sigmoid(gate)
        silu = gate * sg
        act = (silu * up).astype(BF16)
        act_ref[...] = act
        u2 = ALPHA * h1 + _dot(act, wd_ref[...])
        xhat2, r2 = _ln_hat(u2)
        diff = xhat2 * g2_ref[...] + b2_ref[...] - t_ref[...]
        dh2 = diff * (1.0 / d)
        st_ref[0:1, :] += _colsum(dh2 * xhat2)
        st_ref[1:2, :] += _colsum(dh2)
        st_ref[2:3, :] += jnp.broadcast_to(_colsum(_rowsum(diff * diff)) * (0.5 / d), (1, d))
        du2 = _ln_bwd(dh2 * g2_ref[...], xhat2, r2)
        du2b = du2.astype(BF16)
        du2b_ref[...] = du2b
        dact = _dot_nt(du2b, wd_ref[...])
        dgate = (dact * up * (sg * (1.0 + gate * (1.0 - sg)))).astype(BF16)
        dup = (dact * silu).astype(BF16)
        dgu_ref[:, :dff] = dgate
        dgu_ref[:, dff:] = dup
        dh1 = _dot(dgate, wgu_ref[:dff, :]) + _dot(dup, wgu_ref[dff:, :]) + ALPHA * du2
        st_ref[3:4, :] += _colsum(dh1 * xhat1)
        st_ref[4:5, :] += _colsum(dh1)
        du1_ref[...] = _ln_bwd(dh1 * g1_ref[...], xhat1, r1)

    row = lambda width: pl.BlockSpec((tm, width), lambda i: (i, 0))
    hbm = pl.BlockSpec(memory_space=pl.ANY)
    return pl.pallas_call(
        body, name="ffn", grid=(s // tm,),
        in_specs=[row(d), _full((1, d)), _full((1, d)), hbm, hbm, _full((1, d)), _full((1, d)), row(d)],
        out_specs=[row(d), row(d), row(dff), row(2 * dff), row(d), _full((8, d))],
        out_shape=[jax.ShapeDtypeStruct((s, d), F32), jax.ShapeDtypeStruct((s, d), BF16),
                   jax.ShapeDtypeStruct((s, dff), BF16), jax.ShapeDtypeStruct((s, 2 * dff), BF16),
                   jax.ShapeDtypeStruct((s, d), BF16), jax.ShapeDtypeStruct((8, d), F32)],
        scratch_shapes=[pltpu.VMEM(w_gu_t.shape, BF16), pltpu.VMEM(w_down.shape, BF16)],
        compiler_params=_params(60),
    )(u1, g1, b1, w_gu_t, w_down, g2, b2, target)


def _rms_bwd(dm, o, g):
    n, r, _ = _rms_fwd(o, g)
    dn = dm * g
    return r * (dn - n * jnp.mean(dn * n, axis=-1, keepdims=True)), _colsum(dm * n)


def _mix_bwd(du1, w_out, sb_out, sw_out, sb_g, sw_g):
    s, d = du1.shape
    wsb, wsw = sb_out.shape[1], sw_out.shape[1]
    tm = min(ROW_TILE, s)

    def body(du_ref, w_ref, sb_ref, sw_ref, sbg_ref, swg_ref, dsb_ref, dsw_ref, st_ref):
        i = pl.program_id(0)

        @pl.when(i == 0)
        def _():
            st_ref[...] = jnp.zeros_like(st_ref)

        dmerged = _dot_nt(du_ref[...].astype(BF16), w_ref[...])
        dsb, gsb = _rms_bwd(dmerged[:, :wsb], sb_ref[...], sbg_ref[...])
        dsw, gsw = _rms_bwd(dmerged[:, wsb:], sw_ref[...], swg_ref[...])
        dsb_ref[...] = dsb.astype(BF16)
        dsw_ref[...] = dsw.astype(BF16)
        st_ref[0:1, :wsb] += gsb
        st_ref[0:1, wsb:] += gsw

    row = lambda width: pl.BlockSpec((tm, width), lambda i: (i, 0))
    return pl.pallas_call(
        body, name="mix_bwd", grid=(s // tm,),
        in_specs=[row(d), _full((wsb + wsw, d)), row(wsb), row(wsw), _full((1, wsb)), _full((1, wsw))],
        out_specs=[row(wsb), row(wsw), _full((8, wsb + wsw))],
        out_shape=[jax.ShapeDtypeStruct((s, wsb), BF16), jax.ShapeDtypeStruct((s, wsw), BF16),
                   jax.ShapeDtypeStruct((8, wsb + wsw), F32)],
        compiler_params=_params(),
    )(du1, w_out, sb_out, sw_out, sb_g, sw_g)


def _sb_bwd(proj, dout, out, n_pairs, comm=None):
    s = proj.shape[0]
    t = min(SB_TILE, s)
    nq = s // t
    width = n_pairs * LANES

    def body(q_ref, k_ref, v_ref, do_ref, o_ref, dq_ref, dk_out, dv_out, dk_ref, dv_ref):
        i = pl.program_id(1)

        @pl.when(i == 0)
        def _():
            dk_ref[...] = jnp.zeros_like(dk_ref)
            dv_ref[...] = jnp.zeros_like(dv_ref)

        lane = lax.broadcasted_iota(jnp.int32, (1, LANES), 1)
        first = lane < HEAD_DIM
        do2 = do_ref[...]
        qs = _sb_stack_heads(q_ref[...] * SCALE, first)
        dos = _sb_stack_heads(do2, first)
        prod = do2.astype(F32) * o_ref[...]
        totals = jnp.concatenate([_rowsum(jnp.where(first, prod, 0.0)), _rowsum(jnp.where(first, 0.0, prod))], axis=0)
        upper, incl = _sb_triangles(t)

        def visit(offs, carry, mask):
            k_t = _sb_key_tiles(k_ref, offs, t)
            v_t = _sb_key_tiles(v_ref, offs, t)
            c_l, c_e, dq = carry
            lb, a, c_l = _sb_scores(qs, k_t, upper, c_l, mask, t)
            a_b = a.astype(BF16)
            d_e = _dot_nt(dos, v_t) * a_b.astype(F32)
            d_eb = d_e.astype(BF16)
            cols = [slice(j * t, (j + 1) * t) for j in range(len(offs))]
            suf_e, c_e = _sb_suffix([[d_eb[:, c]] for c in cols], [_rowsum(d_e[:, c]) for c in cols], incl, c_e)
            dz = d_e - jnp.exp(lb) * (d_e + (totals - suf_e))
            if mask is not None:
                dz = jnp.where(mask, dz, 0.0)
            dzb = dz.astype(BF16)
            dk_t = _dot_tn(dzb, qs)
            dv_t = _dot_tn(a_b, dos)
            for off, c in zip(offs, cols):
                dk_ref[pl.ds(off, t), :] += dk_t[c, :]
                dv_ref[pl.ds(off, t), :] += dv_t[c, :]
            return c_l, c_e, dq + _dot(dzb, k_t)

        init = (jnp.zeros((2 * t, 1), F32), jnp.zeros((2 * t, 1), F32), jnp.zeros((2 * t, LANES), F32))
        _, _, dq = _sb_walk(i, t, visit, init)
        dq_ref[...] = (jnp.where(first, dq[:t], dq[t:]) * SCALE).astype(BF16)

        @pl.when(i == nq - 1)
        def _():
            dk_out[...] = dk_ref[...].astype(BF16)
            dv_out[...] = dv_ref[...].astype(BF16)

    qblk = pl.BlockSpec((t, LANES), lambda h, i: (i, h))
    whole = pl.BlockSpec((s, LANES), lambda h, i: (0, h))
    return _call(
        body, "sb_bwd", (n_pairs, nq),
        in_specs=[qblk,
                  pl.BlockSpec((s, LANES), lambda h, i: (0, n_pairs + h)),
                  pl.BlockSpec((s, LANES), lambda h, i: (0, 2 * n_pairs + h)),
                  qblk, qblk],
        out_specs=[qblk, whole, whole],
        out_shape=[jax.ShapeDtypeStruct((s, width), BF16)] * 3,
        args=(proj, proj, proj, dout, out),
        scratch_shapes=[pltpu.VMEM((s, LANES), F32), pltpu.VMEM((s, LANES), F32)], comm=comm)


def _swa_bwd(proj, dout, bucket, rel_bias, sinks, n_heads, qcol, kcol, vcol, comm=None):
    s = proj.shape[0]
    width = n_heads * HEAD_DIM
    n_groups = LANES // HEAD_DIM
    per_group = n_heads // n_groups
    nb = s // SWA_BLOCK

    def body(q_ref, kp_ref, kc_ref, vp_ref, vc_ref, bucket_ref, rb_ref, sk_ref, do_ref,
             dq_ref, dk_out, dv_out, dsk_ref, drb_ref, bias_ref, dbias_ref, dk_ref, dv_ref):
        i = pl.program_id(0)

        @pl.when(i == 0)
        def _():
            _swa_build_bias(bucket_ref, rb_ref, bias_ref, n_groups, per_group)
            dbias_ref[...] = jnp.zeros_like(dbias_ref)
            dk_ref[...] = jnp.zeros_like(dk_ref)
            dv_ref[...] = jnp.zeros_like(dv_ref)
            dsk_ref[...] = jnp.zeros_like(dsk_ref)

        lane = lax.broadcasted_iota(jnp.int32, (1, LANES), 1)
        first = lane < HEAD_DIM
        first_mask = _swa_first_block_mask(i)
        kcat = jnp.concatenate([kp_ref[...], kc_ref[...]], axis=0)
        vcat = jnp.concatenate([vp_ref[...], vc_ref[...]], axis=0)
        dkcat = jnp.zeros((2 * SWA_BLOCK, LANES), F32)
        dvcat = jnp.zeros((2 * SWA_BLOCK, LANES), F32)
        pieces = {}
        for g in range(n_groups):
            sel = first if g == 0 else jnp.logical_not(first)
            q_g = _swa_stack(q_ref, g, per_group, sel, SCALE)
            do_g = _swa_stack(do_ref, g, per_group, sel)
            prob, p_sink = _swa_probs(q_g, kcat, bias_ref[g], first_mask, _swa_sink_rows(sk_ref, g, per_group))
            dprob = _dot_nt(do_g, vcat)
            delta = _rowsum(prob * dprob)
            dlog = prob * (dprob - delta)
            sink_term = p_sink * delta
            for hh in range(per_group):
                h = g * per_group + hh
                tot = _colsum(sink_term[hh * SWA_BLOCK:(hh + 1) * SWA_BLOCK, :])
                dsk_ref[h:h + 1, :] += jnp.broadcast_to(-tot, (1, LANES))
            dbias_ref[g] += dlog
            dlb = dlog.astype(BF16)
            _swa_unstack(_dot(dlb, kcat) * SCALE, g, per_group, pieces)
            dkcat += _dot_tn(dlb, q_g)
            dvcat += _dot_tn(prob.astype(BF16), do_g)
        for j in range(n_heads // 2):
            dq_ref[:, j * LANES:(j + 1) * LANES] = jnp.where(first, pieces[2 * j], pieces[2 * j + 1]).astype(BF16)

        cur = pl.multiple_of(i * SWA_BLOCK, SWA_BLOCK)
        dk_ref[pl.ds(cur, SWA_BLOCK), :] += dkcat[SWA_BLOCK:, :]
        dv_ref[pl.ds(cur, SWA_BLOCK), :] += dvcat[SWA_BLOCK:, :]

        @pl.when(i > 0)
        def _():
            prv = pl.multiple_of((i - 1) * SWA_BLOCK, SWA_BLOCK)
            dk_ref[pl.ds(prv, SWA_BLOCK), :] += dkcat[:SWA_BLOCK, :]
            dv_ref[pl.ds(prv, SWA_BLOCK), :] += dvcat[:SWA_BLOCK, :]

        @pl.when(i == nb - 1)
        def _():
            bk = bucket_ref[...]
            rowi = lax.broadcasted_iota(jnp.int32, (REL_BUCKETS, LANES), 0)
            coli = lax.broadcasted_iota(jnp.int32, (REL_BUCKETS, LANES), 1)
            res = jnp.zeros((REL_BUCKETS, LANES), F32)
            for h in range(n_heads):
                g, hh = divmod(h, per_group)
                db = dbias_ref[g, hh * SWA_BLOCK:(hh + 1) * SWA_BLOCK, :]
                for b in range(REL_BUCKETS):
                    tot = _colsum(_rowsum(jnp.where(bk == b, db, 0.0)))
                    res = jnp.where((rowi == b) & (coli == h), tot, res)
            drb_ref[...] = res
            dk_out[...] = dk_ref[...].astype(BF16)
            dv_out[...] = dv_ref[...].astype(BF16)

    in_specs = _swa_specs(n_heads, qcol, kcol, vcol) + [pl.BlockSpec((SWA_BLOCK, width), lambda i: (i, 0))]
    return _call(
        body, "swa_bwd", (nb,),
        in_specs=in_specs,
        out_specs=[pl.BlockSpec((SWA_BLOCK, width), lambda i: (i, 0)),
                   _full((s, LANES)), _full((s, LANES)), _full((8, LANES)), _full((REL_BUCKETS, LANES))],
        out_shape=[jax.ShapeDtypeStruct((s, width), BF16), jax.ShapeDtypeStruct((s, LANES), BF16),
                   jax.ShapeDtypeStruct((s, LANES), BF16), jax.ShapeDtypeStruct((8, LANES), F32),
                   jax.ShapeDtypeStruct((REL_BUCKETS, LANES), F32)],
        args=(proj, proj, proj, proj, proj, bucket, rel_bias, sinks, dout),
        scratch_shapes=[pltpu.VMEM((n_groups, per_group * SWA_BLOCK, 2 * SWA_BLOCK), F32),
                        pltpu.VMEM((n_groups, per_group * SWA_BLOCK, 2 * SWA_BLOCK), F32),
                        pltpu.VMEM((s, LANES), F32), pltpu.VMEM((s, LANES), F32)],
        comm=comm)


def _proj_bwd(pieces, w_in_t, du1, x, g_in, comm=None):
    s, d = x.shape
    cols = w_in_t.shape[0]
    tm = min(ROW_TILE, s)
    n_p = len(pieces)

    def body(*refs):
        p_refs = refs[:n_p]
        w_ref, du_ref, x_ref, g_ref, dx_ref, st_ref = refs[n_p:]
        i = pl.program_id(0)

        @pl.when(i == 0)
        def _():
            st_ref[...] = jnp.zeros_like(st_ref)

        dproj = jnp.concatenate([p[...] for p in p_refs], axis=1)
        dh0 = _dot(dproj, w_ref[...]) + ALPHA * du_ref[...]
        xhat, r = _ln_hat(x_ref[...])
        st_ref[0:1, :] += _colsum(dh0 * xhat)
        st_ref[1:2, :] += _colsum(dh0)
        dx_ref[...] = _ln_bwd(dh0 * g_ref[...], xhat, r)

    row = lambda width: pl.BlockSpec((tm, width), lambda i: (i, 0))
    return _call(
        body, "proj_bwd", (s // tm,),
        in_specs=[row(p.shape[1]) for p in pieces] + [_full((cols, d)), row(d), row(d), _full((1, d))],
        out_specs=[row(d), _full((8, d))],
        out_shape=[jax.ShapeDtypeStruct((s, d), F32), jax.ShapeDtypeStruct((8, d), F32)],
        args=(*pieces, w_in_t, du1, x, g_in), comm=comm)


def _wgrad(name, pieces, b, tm, tn):
    s, n = b.shape
    m = sum(p.shape[1] for p in pieces)
    n_p = len(pieces)
    assert n_p == 1 or tm == m
    ts = min(WGRAD_TOKENS if b.dtype == BF16 and n_p == 1 else WGRAD_TOKENS // 2, s)
    n_k = s // ts

    def body(*refs):
        p_refs, b_ref, o_ref, acc_ref = refs[:n_p], refs[n_p], refs[n_p + 1], refs[n_p + 2]
        k = pl.program_id(2)

        @pl.when(k == 0)
        def _():
            acc_ref[...] = jnp.zeros_like(acc_ref)

        a = p_refs[0][...] if n_p == 1 else jnp.concatenate([p[...] for p in p_refs], axis=1)
        acc_ref[...] += _dot_tn(a, b_ref[...].astype(BF16))

        @pl.when(k == n_k - 1)
        def _():
            o_ref[...] = acc_ref[...].astype(BF16)

    piece_spec = lambda p: pl.BlockSpec((ts, tm if n_p == 1 else p.shape[1]), lambda i, j, k: (k, i))
    return pl.pallas_call(
        body, name=name, grid=(m // tm, n // tn, n_k),
        in_specs=[piece_spec(p) for p in pieces] + [pl.BlockSpec((ts, tn), lambda i, j, k: (k, j))],
        out_specs=pl.BlockSpec((tm, tn), lambda i, j, k: (i, j)),
        out_shape=jax.ShapeDtypeStruct((m, n), BF16),
        scratch_shapes=[pltpu.VMEM((tm, tn), F32)],
        compiler_params=_params(),
    )(*pieces, b)


def _adamw_math(w, g, m, v):
    m = ADAM_B1 * m + (1.0 - ADAM_B1) * g
    v = ADAM_B2 * v + (1.0 - ADAM_B2) * (g * g)
    m_hat = m / (1.0 - ADAM_B1 ** ADAM_STEP)
    v_hat = v / (1.0 - ADAM_B2 ** ADAM_STEP)
    delta = -ADAM_LR * (m_hat / (jnp.sqrt(v_hat) + ADAM_EPS) + ADAM_WD * w)
    return delta, m, v


def _adamw_rows(rows):
    return max(r for r in range(16, 257, 16) if rows % r == 0)


def _adamw(name, landed, w, m, v, tr):
    rows, cols = w.shape

    def body(l_ref, w_ref, m_ref, v_ref, g_ref, d_ref, nm_ref, nv_ref):
        g = l_ref[0].astype(F32)
        for src in range(1, N_DEV):
            g = g + l_ref[src].astype(F32)
        delta, nm, nv = _adamw_math(w_ref[...], g, m_ref[...], v_ref[...])
        g_ref[...] = g
        d_ref[...] = delta
        nm_ref[...] = nm
        nv_ref[...] = nv

    blk = pl.BlockSpec((tr, cols), lambda i: (i, 0))
    shape = jax.ShapeDtypeStruct((rows, cols), F32)
    return pl.pallas_call(
        body, name=name, grid=(rows // tr,),
        in_specs=[pl.BlockSpec((N_DEV, tr, cols), lambda i: (0, i, 0)), blk, blk, blk],
        out_specs=[blk, blk, blk, blk],
        out_shape=[shape, shape, shape, shape],
        compiler_params=_params(),
    )(landed, w, m, v)


def _pack(d, ln_in_g, ln_in_b, ln1_g, ln1_b, ln2_g, ln2_b, sb_g, sw_g, rel_bias, sinks, extra=None):
    tail = [rel_bias.reshape(-1), sinks.reshape(-1)]
    if extra is not None:
        tail.append(extra.reshape(-1))
    tail = jnp.concatenate(tail)
    tail = jnp.concatenate([tail, jnp.zeros((d - tail.shape[0],), F32)])
    rows = [ln_in_g.reshape(-1), ln_in_b.reshape(-1), ln1_g.reshape(-1), ln1_b.reshape(-1),
            ln2_g.reshape(-1), ln2_b.reshape(-1),
            jnp.concatenate([sb_g.reshape(-1), sw_g.reshape(-1)]), tail]
    return jnp.stack(rows)


def _unpack(p, wsb, n_rb, n_sk):
    return [p[0], p[1], p[6, :wsb][None], p[6, wsb:][None], p[7, n_rb:n_rb + n_sk][None],
            p[7, :n_rb].reshape(REL_BUCKETS, -1), p[2][None], p[3][None], p[4][None], p[5][None]]


def kernel(x, ln_in_g, ln_in_b, w_in, sb_norm_g, swa_norm_g, sinks, rel_bias, w_out, ln1_g, ln1_b, w_gate_up, w_down, ln2_g, ln2_b, loss_target, m_ln_in_g, m_ln_in_b, m_w_in, m_sb_norm_g, m_swa_norm_g, m_sinks, m_rel_bias, m_w_out, m_ln1_g, m_ln1_b, m_w_gate_up, m_w_down, m_ln2_g, m_ln2_b, v_ln_in_g, v_ln_in_b, v_w_in, v_sb_norm_g, v_swa_norm_g, v_sinks, v_rel_bias, v_w_out, v_ln1_g, v_ln1_b, v_w_gate_up, v_w_down, v_ln2_g, v_ln2_b):
    x2 = x[0]
    tgt = loss_target[0]
    s, d = x2.shape
    wsb = sb_norm_g.shape[-1]
    wsw = swa_norm_g.shape[-1]
    n_sw_heads = sinks.shape[-1]
    n_pairs = wsb // LANES
    dff = w_down.shape[1] * N_DEV
    assert wsb % LANES == 0 and wsw % LANES == 0 and n_sw_heads * HEAD_DIM == wsw
    assert 3 * wsb % wsw == 0 and dff % LANES == 0 and s % SWA_BLOCK == 0
    qcol = 3 * wsb // wsw
    kcol = (3 * wsb + wsw) // LANES
    vcol = kcol + 1
    assert w_in.shape[-1] * N_DEV == (vcol + 1) * LANES

    t2 = lambda a: jnp.transpose(a[0])
    big_w = [t2(w_in), w_out[0], t2(w_gate_up), w_down[0]]
    big_m = [t2(m_w_in), m_w_out[0], t2(m_w_gate_up), m_w_down[0]]
    big_v = [t2(v_w_in), v_w_out[0], t2(v_w_gate_up), v_w_down[0]]

    cat_rows = lambda g: g.reshape(N_DEV * g.shape[1], g.shape[2])
    shards = [w.astype(BF16) for w in big_w]
    w_in_t = cat_rows(_exchange("w_in_allgather", shards[:1], ["gather"])[0])

    vec = lambda a: a.reshape(1, -1)
    g_in, b_in = vec(ln_in_g), vec(ln_in_b)
    bucket = jnp.asarray(_swa_bucket_table())

    h0b, proj = _ln_proj(x2, g_in, b_in, w_in_t)
    sb_out, gathered = _sb_fwd(proj, n_pairs, comm=(shards[1:], ["gather"] * 3))
    w_out_f, w_gu_t, w_down_f = cat_rows(gathered[0]), cat_rows(gathered[1]), cat_rows(gathered[2])
    sw_out = _swa_fwd(proj, bucket, rel_bias, sinks, n_sw_heads, qcol, kcol, vcol)
    merged, u1 = _mix_ln1(sb_out, sw_out, x2, g_in, b_in, sb_norm_g, swa_norm_g, w_out_f)
    du1, h1b, act, dgu, du2b, st_ffn = _ffn(u1, ln1_g, ln1_b, w_gu_t, w_down_f, ln2_g, ln2_b, tgt)

    split_rows = lambda g: g.reshape(N_DEV, g.shape[0] // N_DEV, g.shape[1])
    gw_gu = _wgrad("wgrad_gate_up", [dgu], h1b, dff // 2, d)
    gw_down = _wgrad("wgrad_down", [act], du2b, dff // 2, d)
    gw_out = _wgrad("wgrad_out", [merged], du1, min(512, d), d)
    dsb, dsw, st_rms = _mix_bwd(du1, w_out_f, sb_out, sw_out, sb_norm_g, swa_norm_g)
    (dq_sb, dk_sb, dv_sb), (land_gu, land_out) = _sb_bwd(
        proj, dsb, sb_out, n_pairs, comm=([split_rows(gw_gu), split_rows(gw_out)], ["scatter"] * 2))
    (dq_sw, dk_sw, dv_sw, st_sink, st_rb), (land_down,) = _swa_bwd(
        proj, dsw, bucket, rel_bias, sinks, n_sw_heads, qcol, kcol, vcol,
        comm=([split_rows(gw_down)], ["scatter"]))
    pieces = [dq_sb, dk_sb, dv_sb, dq_sw, dk_sw, dv_sw]
    gw_in = _wgrad("wgrad_in", pieces, h0b, proj.shape[1], d)
    (grad_x, st_in), (land_in,) = _proj_bwd(pieces, w_in_t, du1, x2, g_in, comm=([split_rows(gw_in)], ["scatter"]))

    n_rb = rel_bias.size
    small = _pack(d, st_in[0], st_in[1], st_ffn[3], st_ffn[4], st_ffn[0], st_ffn[1],
                  st_rms[0, :wsb], st_rms[0, wsb:], st_rb[:, :n_sw_heads], st_sink[:n_sw_heads, 0],
                  extra=st_ffn[2, 0:1])
    land_small = _exchange("small_grads_allgather", [small], ["gather"])[0]
    landed = [land_in, land_out, land_gu, land_down, land_small]

    big = []
    for name, land, w, m, v in zip(["adamw_in", "adamw_out", "adamw_gate_up", "adamw_down"], landed[:4], big_w, big_m, big_v):
        big.append(_adamw(name, land, w, m, v, _adamw_rows(w.shape[0])))

    small_w = _pack(d, ln_in_g, ln_in_b, ln1_g, ln1_b, ln2_g, ln2_b, sb_norm_g, swa_norm_g, rel_bias, sinks)
    small_m = _pack(d, m_ln_in_g, m_ln_in_b, m_ln1_g, m_ln1_b, m_ln2_g, m_ln2_b, m_sb_norm_g, m_swa_norm_g, m_rel_bias, m_sinks)
    small_v = _pack(d, v_ln_in_g, v_ln_in_b, v_ln1_g, v_ln1_b, v_ln2_g, v_ln2_b, v_sb_norm_g, v_swa_norm_g, v_rel_bias, v_sinks)
    sg, sd, sm, sv = _adamw("adamw_small", landed[4], small_w, small_m, small_v, 8)
    n_sk = sinks.size
    loss = sg[7, n_rb + n_sk]

    def leaves(idx):
        sm_l = _unpack([sg, sd, sm, sv][idx], wsb, n_rb, n_sk)
        bg = [jnp.transpose(big[0][idx])[None], big[1][idx][None], jnp.transpose(big[2][idx])[None], big[3][idx][None]]
        return [sm_l[0], sm_l[1], bg[0], sm_l[2], sm_l[3], sm_l[4], sm_l[5], bg[1], sm_l[6], sm_l[7], bg[2], bg[3], sm_l[8], sm_l[9]]

    return (loss, grad_x[None], *leaves(0), *leaves(1), *leaves(2), *leaves(3))
```

```python
import functools
import math

import numpy as np
import jax
import jax.numpy as jnp
from jax import lax
from jax.experimental import pallas as pl
from jax.experimental.pallas import tpu as pltpu

F32 = jnp.float32
BF16 = jnp.bfloat16
MESH = pl.DeviceIdType.MESH

N_DEV = 8
LANES = 128
HEAD_DIM = 64
SCALE = HEAD_DIM ** -0.5
SWA_BLOCK = 128
REL_BUCKETS = 32
REL_MAX_DIST = 128
ALPHA = 2.0 ** 0.25
LN_EPS = 1e-5
RMS_EPS = 1e-6
ADAM_LR = 0.001
ADAM_B1 = 0.9
ADAM_B2 = 0.999
ADAM_EPS = 1e-08
ADAM_WD = 0.01
ADAM_STEP = 10

ROW_TILE = 512
SB_TILE = 256
FFN_TILE = 256
WGRAD_TOKENS = 2048
SB_UNDERFLOW = -110.0
MIB = 1024 * 1024


def _params(vmem_mib=48):
    return pltpu.CompilerParams(vmem_limit_bytes=vmem_mib * MIB)


def _dot(a, b):
    return jnp.dot(a, b, preferred_element_type=F32)


def _dot_nt(a, b):
    return lax.dot_general(a, b, (((1,), (1,)), ((), ())), preferred_element_type=F32)


def _dot_tn(a, b):
    return lax.dot_general(a, b, (((0,), (0,)), ((), ())), preferred_element_type=F32)


def _ln_hat(x):
    mu = jnp.mean(x, axis=-1, keepdims=True)
    xc = x - mu
    var = jnp.mean(xc * xc, axis=-1, keepdims=True)
    r = lax.rsqrt(var + LN_EPS)
    return xc * r, r


def _ln_bwd(dxhat, xhat, r):
    return r * (dxhat - jnp.mean(dxhat, axis=-1, keepdims=True)
                - xhat * jnp.mean(dxhat * xhat, axis=-1, keepdims=True))


def _colsum(a):
    return jnp.sum(a, axis=0, keepdims=True)


def _rowsum(a):
    return jnp.sum(a, axis=1, keepdims=True)


def _full(shape):
    return pl.BlockSpec(shape, lambda *_: (0,) * len(shape))


def _comm_out_shapes(arrays, kinds):
    shapes = []
    for a, kind in zip(arrays, kinds):
        blk = a.shape if kind == "gather" else a.shape[1:]
        shapes.append(jax.ShapeDtypeStruct((N_DEV,) + tuple(blk), a.dtype))
    return shapes


def _comm_sems(n):
    return [pltpu.SemaphoreType.DMA((n, N_DEV - 1)), pltpu.SemaphoreType.DMA((n, N_DEV - 1)),
            pltpu.SemaphoreType.DMA((n,))]


def _comm_copies(ins, outs, kinds, send_sems, recv_sems, local_sems):
    x, y, c = lax.axis_index("x"), lax.axis_index("y"), lax.axis_index("c")
    me = 4 * x + 2 * y + c

    def src_for(t, dev_lin):
        return ins[t] if kinds[t] == "gather" else ins[t].at[dev_lin]

    local = [pltpu.make_async_copy(src_for(t, me), outs[t].at[me], local_sems.at[t]) for t in range(len(kinds))]
    sends, arrivals = [], []
    for k in range(1, N_DEV):
        px = 1 - x if (k >> 2) & 1 else x
        py = 1 - y if (k >> 1) & 1 else y
        pc = 1 - c if k & 1 else c
        peer_lin = 4 * px + 2 * py + pc
        for t in range(len(kinds)):
            sems = dict(send_sem=send_sems.at[t, k - 1], recv_sem=recv_sems.at[t, k - 1],
                        device_id=(px, py, pc), device_id_type=MESH)
            sends.append(pltpu.make_async_remote_copy(src_ref=src_for(t, peer_lin), dst_ref=outs[t].at[me], **sems))
            arrivals.append(pltpu.make_async_remote_copy(src_ref=src_for(t, peer_lin), dst_ref=outs[t].at[peer_lin], **sems))
    return local, sends, arrivals


def _comm_start(ins, outs, kinds, sems):
    local, sends, _ = _comm_copies(ins, outs, kinds, *sems)
    for cp in local + sends:
        cp.start()


def _comm_finish(ins, outs, kinds, sems):
    local, sends, arrivals = _comm_copies(ins, outs, kinds, *sems)
    for cp in arrivals:
        cp.wait_recv()
    for cp in sends:
        cp.wait_send()
    for cp in local:
        cp.wait()


def _exchange(name, arrays, kinds):
    n = len(arrays)

    def body(*refs):
        ins, outs, sems = refs[:n], refs[n:2 * n], refs[2 * n:]
        _comm_start(ins, outs, kinds, sems)
        _comm_finish(ins, outs, kinds, sems)

    any_spec = pl.BlockSpec(memory_space=pl.ANY)
    return pl.pallas_call(
        body, name=name, out_shape=_comm_out_shapes(arrays, kinds),
        in_specs=[any_spec] * n, out_specs=[any_spec] * n,
        scratch_shapes=_comm_sems(n),
    )(*arrays)


def _call(body, name, grid, in_specs, out_specs, out_shape, args, scratch_shapes=(), comm=None):
    if comm is None:
        outs = pl.pallas_call(body, name=name, grid=grid, in_specs=in_specs, out_specs=out_specs,
                              out_shape=out_shape, scratch_shapes=list(scratch_shapes),
                              compiler_params=_params())(*args)
        return outs, []
    arrays, kinds = comm
    n, n_in, n_out, n_scr = len(arrays), len(in_specs), len(out_specs), len(scratch_shapes)

    def fused(*refs):
        c_in, x_in = refs[:n_in], refs[n_in:n_in + n]
        c_out = refs[n_in + n:n_in + n + n_out]
        x_out = refs[n_in + n + n_out:n_in + 2 * n + n_out]
        rest = refs[n_in + 2 * n + n_out:]
        c_scr, sems = rest[:n_scr], rest[n_scr:]
        ids = [pl.program_id(a) for a in range(len(grid))]
        is_first = functools.reduce(jnp.logical_and, [i == 0 for i in ids])
        is_last = functools.reduce(jnp.logical_and, [i == g - 1 for i, g in zip(ids, grid)])

        @pl.when(is_first)
        def _():
            _comm_start(x_in, x_out, kinds, sems)

        body(*c_in, *c_out, *c_scr)

        @pl.when(is_last)
        def _():
            _comm_finish(x_in, x_out, kinds, sems)

    any_spec = pl.BlockSpec(memory_space=pl.ANY)
    outs = pl.pallas_call(
        fused, name=name, grid=grid,
        in_specs=list(in_specs) + [any_spec] * n, out_specs=list(out_specs) + [any_spec] * n,
        out_shape=list(out_shape) + _comm_out_shapes(arrays, kinds),
        scratch_shapes=list(scratch_shapes) + _comm_sems(n),
        compiler_params=_params())(*args, *arrays)
    return outs[:n_out], outs[n_out:]


def _ln_proj(x, g, b, w_in_t):
    s, d = x.shape
    cols = w_in_t.shape[0]
    tm = min(ROW_TILE, s)

    def body(x_ref, g_ref, b_ref, w_ref, h_ref, p_ref):
        xhat, _ = _ln_hat(x_ref[...])
        h = (xhat * g_ref[...] + b_ref[...]).astype(BF16)
        h_ref[...] = h
        p_ref[...] = _dot_nt(h, w_ref[...]).astype(BF16)

    row = lambda width: pl.BlockSpec((tm, width), lambda i: (i, 0))
    return pl.pallas_call(
        body, name="ln_proj", grid=(s // tm,),
        in_specs=[row(d), _full((1, d)), _full((1, d)), _full((cols, d))],
        out_specs=[row(d), row(cols)],
        out_shape=[jax.ShapeDtypeStruct((s, d), BF16), jax.ShapeDtypeStruct((s, cols), BF16)],
        compiler_params=_params(),
    )(x, g, b, w_in_t)


def _sb_triangles(t):
    row = lax.broadcasted_iota(jnp.int32, (t, t), 0)
    col = lax.broadcasted_iota(jnp.int32, (t, t), 1)
    return (row > col).astype(BF16), (row >= col).astype(BF16)


def _sb_first_mask(t, has_prev):
    qrow = lax.broadcasted_iota(jnp.int32, (2 * t, 2 * t), 0) & (t - 1)
    col = lax.broadcasted_iota(jnp.int32, (2 * t, 2 * t), 1)
    return ((col < t) & has_prev) | ((col >= t) & (col - t < qrow))


def _sb_stack_heads(x2, first):
    zero = jnp.zeros_like(x2)
    return jnp.concatenate([jnp.where(first, x2, zero), jnp.where(first, zero, x2)], axis=0)


def _sb_key_tiles(ref, offs, t):
    tiles = [ref[pl.ds(off, t), :] for off in offs]
    return tiles[0] if len(tiles) == 1 else jnp.concatenate(tiles, axis=0)


def _sb_suffix(terms, row_sums, tri, carry):
    out = [None] * len(terms)
    for j in reversed(range(len(terms))):
        suf = carry
        for op in terms[j]:
            suf = suf + _dot(op, tri)
        out[j] = suf
        carry = carry + row_sums[j]
    return (out[0] if len(out) == 1 else jnp.concatenate(out, axis=1)), carry


def _sb_scores(qh, k_t, upper, carry_l, mask, t):
    z = _dot_nt(qh, k_t)
    sp = jnp.log(1.0 + jnp.exp(-jnp.abs(z)))
    neg = jnp.minimum(z, 0.0)
    lb = neg - sp
    l1 = (neg - z) - sp
    if mask is not None:
        l1 = jnp.where(mask, l1, 0.0)
    hi = l1.astype(BF16)
    lo = (l1 - hi.astype(F32)).astype(BF16)
    cols = [slice(j * t, (j + 1) * t) for j in range(z.shape[1] // t)]
    suf, carry_l = _sb_suffix([[hi[:, c], lo[:, c]] for c in cols], [_rowsum(l1[:, c]) for c in cols], upper, carry_l)
    a = jnp.exp(lb + suf)
    if mask is not None:
        a = jnp.where(mask, a, 0.0)
    return lb, a, carry_l


def _sb_walk(i, t, visit, init):
    def alive(carry):
        return jnp.max(carry[0]) > SB_UNDERFLOW

    prev = pl.multiple_of(jnp.maximum(i - 1, 0) * t, t)
    carry = visit((prev, pl.multiple_of(i * t, t)), init, _sb_first_mask(t, i > 0))

    def cond(state):
        j, go, _ = state
        return (j < i - 1) & go

    def body(state):
        j, _, carry = state
        carry = visit((pl.multiple_of((i - 2 - j) * t, t),), carry, None)
        return j + 1, alive(carry), carry

    return lax.while_loop(cond, body, (jnp.int32(0), alive(carry), carry))[2]


def _sb_fwd(proj, n_pairs, comm=None):
    s = proj.shape[0]
    t = min(SB_TILE, s)
    nq = s // t

    def body(q_ref, k_ref, v_ref, o_ref):
        i = pl.program_id(1)
        lane = lax.broadcasted_iota(jnp.int32, (1, LANES), 1)
        first = lane < HEAD_DIM
        qs = _sb_stack_heads(q_ref[...] * SCALE, first)
        upper, _ = _sb_triangles(t)

        def visit(offs, carry, mask):
            c_l, acc = carry
            _, a, c_l = _sb_scores(qs, _sb_key_tiles(k_ref, offs, t), upper, c_l, mask, t)
            return c_l, acc + _dot(a.astype(BF16), _sb_key_tiles(v_ref, offs, t))

        init = (jnp.zeros((2 * t, 1), F32), jnp.zeros((2 * t, LANES), F32))
        _, acc = _sb_walk(i, t, visit, init)
        o_ref[...] = jnp.where(first, acc[:t], acc[t:])

    outs, landed = _call(
        body, "sb_fwd", (n_pairs, nq),
        in_specs=[pl.BlockSpec((t, LANES), lambda h, i: (i, h)),
                  pl.BlockSpec((s, LANES), lambda h, i: (0, n_pairs + h)),
                  pl.BlockSpec((s, LANES), lambda h, i: (0, 2 * n_pairs + h))],
        out_specs=[pl.BlockSpec((t, LANES), lambda h, i: (i, h))],
        out_shape=[jax.ShapeDtypeStruct((s, n_pairs * LANES), F32)],
        args=(proj, proj, proj), comm=comm)
    return outs[0], landed


def _swa_bucket_table():
    qi = np.arange(SWA_BLOCK)[:, None]
    cj = np.arange(2 * SWA_BLOCK)[None, :]
    dist = qi + SWA_BLOCK - cj
    exact = REL_BUCKETS // 2
    d = np.maximum(dist, 0)
    d_f = np.maximum(d, 1).astype(np.float32)
    large = exact + (np.log(d_f / np.float32(exact)) / np.float32(math.log(REL_MAX_DIST / exact))
                     * np.float32(REL_BUCKETS - exact)).astype(np.int32)
    large = np.minimum(large, REL_BUCKETS - 1)
    return np.where(d < exact, d, large).astype(np.int32)


def _swa_build_bias(bucket_ref, rb_ref, bias_ref, n_groups, per_group):
    bk = bucket_ref[...]
    dist = (lax.broadcasted_iota(jnp.int32, bk.shape, 0) + SWA_BLOCK) - lax.broadcasted_iota(jnp.int32, bk.shape, 1)
    window = (dist >= 0) & (dist < SWA_BLOCK)
    for g in range(n_groups):
        for hh in range(per_group):
            acc = jnp.zeros(bk.shape, F32)
            for b in range(REL_BUCKETS):
                acc = jnp.where(bk == b, rb_ref[b, g * per_group + hh], acc)
            bias_ref[g, hh * SWA_BLOCK:(hh + 1) * SWA_BLOCK, :] = jnp.where(window, acc, -jnp.inf)


def _swa_first_block_mask(i):
    col = lax.broadcasted_iota(jnp.int32, (1, 2 * SWA_BLOCK), 1)
    return jnp.where((col < SWA_BLOCK) & (i == 0), -jnp.inf, 0.0)


def _swa_place(blk, h, group, sel):
    if (h % 2) != group:
        blk = pltpu.roll(blk.astype(F32), HEAD_DIM, axis=1).astype(BF16)
    return jnp.where(sel, blk, jnp.zeros_like(blk))


def _swa_stack(ref, group, per_group, sel, scale=1.0):
    parts = []
    for hh in range(per_group):
        h = group * per_group + hh
        parts.append(_swa_place(ref[:, (h // 2) * LANES:(h // 2 + 1) * LANES], h, group, sel))
    stacked = jnp.concatenate(parts, axis=0)
    return stacked if scale == 1.0 else stacked * scale


def _swa_unstack(stacked, group, per_group, pieces):
    for hh in range(per_group):
        h = group * per_group + hh
        piece = stacked[hh * SWA_BLOCK:(hh + 1) * SWA_BLOCK, :]
        pieces[h] = pltpu.roll(piece, HEAD_DIM, axis=1) if (h % 2) != group else piece


def _swa_sink_rows(sk_ref, group, per_group):
    rowh = lax.broadcasted_iota(jnp.int32, (per_group * SWA_BLOCK, 1), 0) // SWA_BLOCK
    sink = jnp.zeros((per_group * SWA_BLOCK, 1), F32) + sk_ref[0, group * per_group]
    for hh in range(1, per_group):
        sink = jnp.where(rowh == hh, sk_ref[0, group * per_group + hh], sink)
    return sink


def _swa_probs(q_pos, kcat, bias_h, first_mask, sink):
    logits = _dot_nt(q_pos, kcat) + (bias_h + first_mask)
    m = jnp.maximum(jnp.max(logits, axis=1, keepdims=True), sink)
    p = jnp.exp(logits - m)
    es = jnp.exp(sink - m)
    inv = 1.0 / (_rowsum(p) + es)
    return p * inv, es * inv


def _swa_specs(n_heads, qcol, kcol, vcol):
    width = n_heads * HEAD_DIM
    prev = lambda col: pl.BlockSpec((SWA_BLOCK, LANES), lambda i: (jnp.maximum(i - 1, 0), col))
    cur = lambda col: pl.BlockSpec((SWA_BLOCK, LANES), lambda i: (i, col))
    return [pl.BlockSpec((SWA_BLOCK, width), lambda i: (i, qcol)),
            prev(kcol), cur(kcol), prev(vcol), cur(vcol),
            _full((SWA_BLOCK, 2 * SWA_BLOCK)),
            pl.BlockSpec(memory_space=pltpu.SMEM), pl.BlockSpec(memory_space=pltpu.SMEM)]


def _swa_fwd(proj, bucket, rel_bias, sinks, n_heads, qcol, kcol, vcol, comm=None):
    s = proj.shape[0]
    width = n_heads * HEAD_DIM
    n_groups = LANES // HEAD_DIM
    per_group = n_heads // n_groups

    def body(q_ref, kp_ref, kc_ref, vp_ref, vc_ref, bucket_ref, rb_ref, sk_ref, o_ref, bias_ref):
        i = pl.program_id(0)

        @pl.when(i == 0)
        def _():
            _swa_build_bias(bucket_ref, rb_ref, bias_ref, n_groups, per_group)

        lane = lax.broadcasted_iota(jnp.int32, (1, LANES), 1)
        first = lane < HEAD_DIM
        first_mask = _swa_first_block_mask(i)
        kcat = jnp.concatenate([kp_ref[...], kc_ref[...]], axis=0)
        vcat = jnp.concatenate([vp_ref[...], vc_ref[...]], axis=0)
        pieces = {}
        for g in range(n_groups):
            sel = first if g == 0 else jnp.logical_not(first)
            prob, _ = _swa_probs(_swa_stack(q_ref, g, per_group, sel, SCALE), kcat, bias_ref[g], first_mask,
                                 _swa_sink_rows(sk_ref, g, per_group))
            _swa_unstack(_dot(prob.astype(BF16), vcat), g, per_group, pieces)
        for j in range(n_heads // 2):
            o_ref[:, j * LANES:(j + 1) * LANES] = jnp.where(first, pieces[2 * j], pieces[2 * j + 1])

    outs, landed = _call(
        body, "swa_fwd", (s // SWA_BLOCK,),
        in_specs=_swa_specs(n_heads, qcol, kcol, vcol),
        out_specs=[pl.BlockSpec((SWA_BLOCK, width), lambda i: (i, 0))],
        out_shape=[jax.ShapeDtypeStruct((s, width), F32)],
        args=(proj, proj, proj, proj, proj, bucket, rel_bias, sinks),
        scratch_shapes=[pltpu.VMEM((n_groups, per_group * SWA_BLOCK, 2 * SWA_BLOCK), F32)], comm=comm)
    return outs[0], landed


def _rms_fwd(o, g):
    r = lax.rsqrt(jnp.mean(o * o, axis=-1, keepdims=True) + RMS_EPS)
    n = o * r
    return n, r, n * g


def _mix_ln1(sb_out, sw_out, x, g_in, b_in, sb_g, sw_g, w_out):
    s, d = x.shape
    wsb, wsw = sb_out.shape[1], sw_out.shape[1]
    tm = min(ROW_TILE, s)

    def body(sb_ref, sw_ref, x_ref, gi_ref, bi_ref, sbg_ref, swg_ref, w_ref, mg_ref, u_ref):
        _, _, m_sb = _rms_fwd(sb_ref[...], sbg_ref[...])
        _, _, m_sw = _rms_fwd(sw_ref[...], swg_ref[...])
        m_sb = m_sb.astype(BF16)
        m_sw = m_sw.astype(BF16)
        mg_ref[:, :wsb] = m_sb
        mg_ref[:, wsb:] = m_sw
        mix = _dot(m_sb, w_ref[:wsb, :]) + _dot(m_sw, w_ref[wsb:, :])
        xhat, _ = _ln_hat(x_ref[...])
        h0 = xhat * gi_ref[...] + bi_ref[...]
        u_ref[...] = ALPHA * h0 + mix

    row = lambda width: pl.BlockSpec((tm, width), lambda i: (i, 0))
    return pl.pallas_call(
        body, name="mix_ln1", grid=(s // tm,),
        in_specs=[row(wsb), row(wsw), row(d), _full((1, d)), _full((1, d)),
                  _full((1, wsb)), _full((1, wsw)), _full((wsb + wsw, d))],
        out_specs=[row(wsb + wsw), row(d)],
        out_shape=[jax.ShapeDtypeStruct((s, wsb + wsw), BF16), jax.ShapeDtypeStruct((s, d), F32)],
        compiler_params=_params(),
    )(sb_out, sw_out, x, g_in, b_in, sb_g, sw_g, w_out)


def _ffn(u1, g1, b1, w_gu_t, w_down, g2, b2, target):
    s, d = u1.shape
    dff = w_down.shape[0]
    tm = min(FFN_TILE, s)

    def body(u_ref, g1_ref, b1_ref, wgu_hbm, wd_hbm, g2_ref, b2_ref, t_ref,
             du1_ref, h1b_ref, act_ref, dgu_ref, du2b_ref, st_ref, wgu_ref, wd_ref):
        @pl.when(pl.program_id(0) == 0)
        def _():
            pltpu.sync_copy(wgu_hbm, wgu_ref)
            pltpu.sync_copy(wd_hbm, wd_ref)
            st_ref[...] = jnp.zeros_like(st_ref)

        xhat1, r1 = _ln_hat(u_ref[...])
        h1 = xhat1 * g1_ref[...] + b1_ref[...]
        h1b = h1.astype(BF16)
        h1b_ref[...] = h1b
        gate = _dot_nt(h1b, wgu_ref[:dff, :])
        up = _dot_nt(h1b, wgu_ref[dff:, :])
        sg = jax.nn.sigmoid(gate)
        silu = gate * sg
        act = (silu * up).astype(BF16)
        act_ref[...] = act
        u2 = ALPHA * h1 + _dot(act, wd_ref[...])
        xhat2, r2 = _ln_hat(u2)
        diff = xhat2 * g2_ref[...] + b2_ref[...] - t_ref[...]
        dh2 = diff * (1.0 / d)
        st_ref[0:1, :] += _colsum(dh2 * xhat2)
        st_ref[1:2, :] += _colsum(dh2)
        st_ref[2:3, :] += jnp.broadcast_to(_colsum(_rowsum(diff * diff)) * (0.5 / d), (1, d))
        du2 = _ln_bwd(dh2 * g2_ref[...], xhat2, r2)
        du2b = du2.astype(BF16)
        du2b_ref[...] = du2b
        dact = _dot_nt(du2b, wd_ref[...])
        dgate = (dact * up * (sg * (1.0 + gate * (1.0 - sg)))).astype(BF16)
        dup = (dact * silu).astype(BF16)
        dgu_ref[:, :dff] = dgate
        dgu_ref[:, dff:] = dup
        dh1 = _dot(dgate, wgu_ref[:dff, :]) + _dot(dup, wgu_ref[dff:, :]) + ALPHA * du2
        st_ref[3:4, :] += _colsum(dh1 * xhat1)
        st_ref[4:5, :] += _colsum(dh1)
        du1_ref[...] = _ln_bwd(dh1 * g1_ref[...], xhat1, r1)

    row = lambda width: pl.BlockSpec((tm, width), lambda i: (i, 0))
    hbm = pl.BlockSpec(memory_space=pl.ANY)
    return pl.pallas_call(
        body, name="ffn", grid=(s // tm,),
        in_specs=[row(d), _full((1, d)), _full((1, d)), hbm, hbm, _full((1, d)), _full((1, d)), row(d)],
        out_specs=[row(d), row(d), row(dff), row(2 * dff), row(d), _full((8, d))],
        out_shape=[jax.ShapeDtypeStruct((s, d), F32), jax.ShapeDtypeStruct((s, d), BF16),
                   jax.ShapeDtypeStruct((s, dff), BF16), jax.ShapeDtypeStruct((s, 2 * dff), BF16),
                   jax.ShapeDtypeStruct((s, d), BF16), jax.ShapeDtypeStruct((8, d), F32)],
        scratch_shapes=[pltpu.VMEM(w_gu_t.shape, BF16), pltpu.VMEM(w_down.shape, BF16)],
        compiler_params=_params(60),
    )(u1, g1, b1, w_gu_t, w_down, g2, b2, target)


def _rms_bwd(dm, o, g):
    n, r, _ = _rms_fwd(o, g)
    dn = dm * g
    return r * (dn - n * jnp.mean(dn * n, axis=-1, keepdims=True)), _colsum(dm * n)


def _mix_bwd(du1, w_out, sb_out, sw_out, sb_g, sw_g):
    s, d = du1.shape
    wsb, wsw = sb_out.shape[1], sw_out.shape[1]
    tm = min(ROW_TILE, s)

    def body(du_ref, w_ref, sb_ref, sw_ref, sbg_ref, swg_ref, dsb_ref, dsw_ref, st_ref):
        i = pl.program_id(0)

        @pl.when(i == 0)
        def _():
            st_ref[...] = jnp.zeros_like(st_ref)

        dmerged = _dot_nt(du_ref[...].astype(BF16), w_ref[...])
        dsb, gsb = _rms_bwd(dmerged[:, :wsb], sb_ref[...], sbg_ref[...])
        dsw, gsw = _rms_bwd(dmerged[:, wsb:], sw_ref[...], swg_ref[...])
        dsb_ref[...] = dsb.astype(BF16)
        dsw_ref[...] = dsw.astype(BF16)
        st_ref[0:1, :wsb] += gsb
        st_ref[0:1, wsb:] += gsw

    row = lambda width: pl.BlockSpec((tm, width), lambda i: (i, 0))
    return pl.pallas_call(
        body, name="mix_bwd", grid=(s // tm,),
        in_specs=[row(d), _full((wsb + wsw, d)), row(wsb), row(wsw), _full((1, wsb)), _full((1, wsw))],
        out_specs=[row(wsb), row(wsw), _full((8, wsb + wsw))],
        out_shape=[jax.ShapeDtypeStruct((s, wsb), BF16), jax.ShapeDtypeStruct((s, wsw), BF16),
                   jax.ShapeDtypeStruct((8, wsb + wsw), F32)],
        compiler_params=_params(),
    )(du1, w_out, sb_out, sw_out, sb_g, sw_g)


def _sb_bwd(proj, dout, out, n_pairs, comm=None):
    s = proj.shape[0]
    t = min(SB_TILE, s)
    nq = s // t
    width = n_pairs * LANES

    def body(q_ref, k_ref, v_ref, do_ref, o_ref, dq_ref, dk_out, dv_out, dk_ref, dv_ref):
        i = pl.program_id(1)

        @pl.when(i == 0)
        def _():
            dk_ref[...] = jnp.zeros_like(dk_ref)
            dv_ref[...] = jnp.zeros_like(dv_ref)

        lane = lax.broadcasted_iota(jnp.int32, (1, LANES), 1)
        first = lane < HEAD_DIM
        do2 = do_ref[...]
        qs = _sb_stack_heads(q_ref[...] * SCALE, first)
        dos = _sb_stack_heads(do2, first)
        prod = do2.astype(F32) * o_ref[...]
        totals = jnp.concatenate([_rowsum(jnp.where(first, prod, 0.0)), _rowsum(jnp.where(first, 0.0, prod))], axis=0)
        upper, incl = _sb_triangles(t)

        def visit(offs, carry, mask):
            k_t = _sb_key_tiles(k_ref, offs, t)
            v_t = _sb_key_tiles(v_ref, offs, t)
            c_l, c_e, dq = carry
            lb, a, c_l = _sb_scores(qs, k_t, upper, c_l, mask, t)
            a_b = a.astype(BF16)
            d_e = _dot_nt(dos, v_t) * a_b.astype(F32)
            d_eb = d_e.astype(BF16)
            cols = [slice(j * t, (j + 1) * t) for j in range(len(offs))]
            suf_e, c_e = _sb_suffix([[d_eb[:, c]] for c in cols], [_rowsum(d_e[:, c]) for c in cols], incl, c_e)
            dz = d_e - jnp.exp(lb) * (d_e + (totals - suf_e))
            if mask is not None:
                dz = jnp.where(mask, dz, 0.0)
            dzb = dz.astype(BF16)
            dk_t = _dot_tn(dzb, qs)
            dv_t = _dot_tn(a_b, dos)
            for off, c in zip(offs, cols):
                dk_ref[pl.ds(off, t), :] += dk_t[c, :]
                dv_ref[pl.ds(off, t), :] += dv_t[c, :]
            return c_l, c_e, dq + _dot(dzb, k_t)

        init = (jnp.zeros((2 * t, 1), F32), jnp.zeros((2 * t, 1), F32), jnp.zeros((2 * t, LANES), F32))
        _, _, dq = _sb_walk(i, t, visit, init)
        dq_ref[...] = (jnp.where(first, dq[:t], dq[t:]) * SCALE).astype(BF16)

        @pl.when(i == nq - 1)
        def _():
            dk_out[...] = dk_ref[...].astype(BF16)
            dv_out[...] = dv_ref[...].astype(BF16)

    qblk = pl.BlockSpec((t, LANES), lambda h, i: (i, h))
    whole = pl.BlockSpec((s, LANES), lambda h, i: (0, h))
    return _call(
        body, "sb_bwd", (n_pairs, nq),
        in_specs=[qblk,
                  pl.BlockSpec((s, LANES), lambda h, i: (0, n_pairs + h)),
                  pl.BlockSpec((s, LANES), lambda h, i: (0, 2 * n_pairs + h)),
                  qblk, qblk],
        out_specs=[qblk, whole, whole],
        out_shape=[jax.ShapeDtypeStruct((s, width), BF16)] * 3,
        args=(proj, proj, proj, dout, out),
        scratch_shapes=[pltpu.VMEM((s, LANES), F32), pltpu.VMEM((s, LANES), F32)], comm=comm)


def _swa_bwd(proj, dout, bucket, rel_bias, sinks, n_heads, qcol, kcol, vcol, comm=None):
    s = proj.shape[0]
    width = n_heads * HEAD_DIM
    n_groups = LANES // HEAD_DIM
    per_group = n_heads // n_groups
    nb = s // SWA_BLOCK

    def body(q_ref, kp_ref, kc_ref, vp_ref, vc_ref, bucket_ref, rb_ref, sk_ref, do_ref,
             dq_ref, dk_out, dv_out, dsk_ref, drb_ref, bias_ref, dbias_ref, dk_ref, dv_ref):
        i = pl.program_id(0)

        @pl.when(i == 0)
        def _():
            _swa_build_bias(bucket_ref, rb_ref, bias_ref, n_groups, per_group)
            dbias_ref[...] = jnp.zeros_like(dbias_ref)
            dk_ref[...] = jnp.zeros_like(dk_ref)
            dv_ref[...] = jnp.zeros_like(dv_ref)
            dsk_ref[...] = jnp.zeros_like(dsk_ref)

        lane = lax.broadcasted_iota(jnp.int32, (1, LANES), 1)
        first = lane < HEAD_DIM
        first_mask = _swa_first_block_mask(i)
        kcat = jnp.concatenate([kp_ref[...], kc_ref[...]], axis=0)
        vcat = jnp.concatenate([vp_ref[...], vc_ref[...]], axis=0)
        dkcat = jnp.zeros((2 * SWA_BLOCK, LANES), F32)
        dvcat = jnp.zeros((2 * SWA_BLOCK, LANES), F32)
        pieces = {}
        for g in range(n_groups):
            sel = first if g == 0 else jnp.logical_not(first)
            q_g = _swa_stack(q_ref, g, per_group, sel, SCALE)
            do_g = _swa_stack(do_ref, g, per_group, sel)
            prob, p_sink = _swa_probs(q_g, kcat, bias_ref[g], first_mask, _swa_sink_rows(sk_ref, g, per_group))
            dprob = _dot_nt(do_g, vcat)
            delta = _rowsum(prob * dprob)
            dlog = prob * (dprob - delta)
            sink_term = p_sink * delta
            for hh in range(per_group):
                h = g * per_group + hh
                tot = _colsum(sink_term[hh * SWA_BLOCK:(hh + 1) * SWA_BLOCK, :])
                dsk_ref[h:h + 1, :] += jnp.broadcast_to(-tot, (1, LANES))
            dbias_ref[g] += dlog
            dlb = dlog.astype(BF16)
            _swa_unstack(_dot(dlb, kcat) * SCALE, g, per_group, pieces)
            dkcat += _dot_tn(dlb, q_g)
            dvcat += _dot_tn(prob.astype(BF16), do_g)
        for j in range(n_heads // 2):
            dq_ref[:, j * LANES:(j + 1) * LANES] = jnp.where(first, pieces[2 * j], pieces[2 * j + 1]).astype(BF16)

        cur = pl.multiple_of(i * SWA_BLOCK, SWA_BLOCK)
        dk_ref[pl.ds(cur, SWA_BLOCK), :] += dkcat[SWA_BLOCK:, :]
        dv_ref[pl.ds(cur, SWA_BLOCK), :] += dvcat[SWA_BLOCK:, :]

        @pl.when(i > 0)
        def _():
            prv = pl.multiple_of((i - 1) * SWA_BLOCK, SWA_BLOCK)
            dk_ref[pl.ds(prv, SWA_BLOCK), :] += dkcat[:SWA_BLOCK, :]
            dv_ref[pl.ds(prv, SWA_BLOCK), :] += dvcat[:SWA_BLOCK, :]

        @pl.when(i == nb - 1)
        def _():
            bk = bucket_ref[...]
            rowi = lax.broadcasted_iota(jnp.int32, (REL_BUCKETS, LANES), 0)
            coli = lax.broadcasted_iota(jnp.int32, (REL_BUCKETS, LANES), 1)
            res = jnp.zeros((REL_BUCKETS, LANES), F32)
            for h in range(n_heads):
                g, hh = divmod(h, per_group)
                db = dbias_ref[g, hh * SWA_BLOCK:(hh + 1) * SWA_BLOCK, :]
                for b in range(REL_BUCKETS):
                    tot = _colsum(_rowsum(jnp.where(bk == b, db, 0.0)))
                    res = jnp.where((rowi == b) & (coli == h), tot, res)
            drb_ref[...] = res
            dk_out[...] = dk_ref[...].astype(BF16)
            dv_out[...] = dv_ref[...].astype(BF16)

    in_specs = _swa_specs(n_heads, qcol, kcol, vcol) + [pl.BlockSpec((SWA_BLOCK, width), lambda i: (i, 0))]
    return _call(
        body, "swa_bwd", (nb,),
        in_specs=in_specs,
        out_specs=[pl.BlockSpec((SWA_BLOCK, width), lambda i: (i, 0)),
                   _full((s, LANES)), _full((s, LANES)), _full((8, LANES)), _full((REL_BUCKETS, LANES))],
        out_shape=[jax.ShapeDtypeStruct((s, width), BF16), jax.ShapeDtypeStruct((s, LANES), BF16),
                   jax.ShapeDtypeStruct((s, LANES), BF16), jax.ShapeDtypeStruct((8, LANES), F32),
                   jax.ShapeDtypeStruct((REL_BUCKETS, LANES), F32)],
        args=(proj, proj, proj, proj, proj, bucket, rel_bias, sinks, dout),
        scratch_shapes=[pltpu.VMEM((n_groups, per_group * SWA_BLOCK, 2 * SWA_BLOCK), F32),
                        pltpu.VMEM((n_groups, per_group * SWA_BLOCK, 2 * SWA_BLOCK), F32),
                        pltpu.VMEM((s, LANES), F32), pltpu.VMEM((s, LANES), F32)],
        comm=comm)


def _proj_bwd(pieces, w_in_t, du1, x, g_in, comm=None):
    s, d = x.shape
    cols = w_in_t.shape[0]
    tm = min(ROW_TILE, s)
    n_p = len(pieces)

    def body(*refs):
        p_refs = refs[:n_p]
        w_ref, du_ref, x_ref, g_ref, dx_ref, st_ref = refs[n_p:]
        i = pl.program_id(0)

        @pl.when(i == 0)
        def _():
            st_ref[...] = jnp.zeros_like(st_ref)

        dproj = jnp.concatenate([p[...] for p in p_refs], axis=1)
        dh0 = _dot(dproj, w_ref[...]) + ALPHA * du_ref[...]
        xhat, r = _ln_hat(x_ref[...])
        st_ref[0:1, :] += _colsum(dh0 * xhat)
        st_ref[1:2, :] += _colsum(dh0)
        dx_ref[...] = _ln_bwd(dh0 * g_ref[...], xhat, r)

    row = lambda width: pl.BlockSpec((tm, width), lambda i: (i, 0))
    return _call(
        body, "proj_bwd", (s // tm,),
        in_specs=[row(p.shape[1]) for p in pieces] + [_full((cols, d)), row(d), row(d), _full((1, d))],
        out_specs=[row(d), _full((8, d))],
        out_shape=[jax.ShapeDtypeStruct((s, d), F32), jax.ShapeDtypeStruct((8, d), F32)],
        args=(*pieces, w_in_t, du1, x, g_in), comm=comm)


def _wgrad(name, pieces, b, tm, tn):
    s, n = b.shape
    m = sum(p.shape[1] for p in pieces)
    n_p = len(pieces)
    assert n_p == 1 or tm == m
    ts = min(WGRAD_TOKENS if b.dtype == BF16 and n_p == 1 else WGRAD_TOKENS // 2, s)
    n_k = s // ts

    def body(*refs):
        p_refs, b_ref, o_ref, acc_ref = refs[:n_p], refs[n_p], refs[n_p + 1], refs[n_p + 2]
        k = pl.program_id(2)

        @pl.when(k == 0)
        def _():
            acc_ref[...] = jnp.zeros_like(acc_ref)

        a = p_refs[0][...] if n_p == 1 else jnp.concatenate([p[...] for p in p_refs], axis=1)
        acc_ref[...] += _dot_tn(a, b_ref[...].astype(BF16))

        @pl.when(k == n_k - 1)
        def _():
            o_ref[...] = acc_ref[...].astype(BF16)

    piece_spec = lambda p: pl.BlockSpec((ts, tm if n_p == 1 else p.shape[1]), lambda i, j, k: (k, i))
    return pl.pallas_call(
        body, name=name, grid=(m // tm, n // tn, n_k),
        in_specs=[piece_spec(p) for p in pieces] + [pl.BlockSpec((ts, tn), lambda i, j, k: (k, j))],
        out_specs=pl.BlockSpec((tm, tn), lambda i, j, k: (i, j)),
        out_shape=jax.ShapeDtypeStruct((m, n), BF16),
        scratch_shapes=[pltpu.VMEM((tm, tn), F32)],
        compiler_params=_params(),
    )(*pieces, b)


def _adamw_math(w, g, m, v):
    m = ADAM_B1 * m + (1.0 - ADAM_B1) * g
    v = ADAM_B2 * v + (1.0 - ADAM_B2) * (g * g)
    m_hat = m / (1.0 - ADAM_B1 ** ADAM_STEP)
    v_hat = v / (1.0 - ADAM_B2 ** ADAM_STEP)
    delta = -ADAM_LR * (m_hat / (jnp.sqrt(v_hat) + ADAM_EPS) + ADAM_WD * w)
    return delta, m, v


def _adamw_rows(rows):
    return max(r for r in range(16, 257, 16) if rows % r == 0)


def _adamw(name, landed, w, m, v, tr):
    rows, cols = w.shape

    def body(l_ref, w_ref, m_ref, v_ref, g_ref, d_ref, nm_ref, nv_ref):
        g = l_ref[0].astype(F32)
        for src in range(1, N_DEV):
            g = g + l_ref[src].astype(F32)
        delta, nm, nv = _adamw_math(w_ref[...], g, m_ref[...], v_ref[...])
        g_ref[...] = g
        d_ref[...] = delta
        nm_ref[...] = nm
        nv_ref[...] = nv

    blk = pl.BlockSpec((tr, cols), lambda i: (i, 0))
    shape = jax.ShapeDtypeStruct((rows, cols), F32)
    return pl.pallas_call(
        body, name=name, grid=(rows // tr,),
        in_specs=[pl.BlockSpec((N_DEV, tr, cols), lambda i: (0, i, 0)), blk, blk, blk],
        out_specs=[blk, blk, blk, blk],
        out_shape=[shape, shape, shape, shape],
        compiler_params=_params(),
    )(landed, w, m, v)


def _pack(d, ln_in_g, ln_in_b, ln1_g, ln1_b, ln2_g, ln2_b, sb_g, sw_g, rel_bias, sinks, extra=None):
    tail = [rel_bias.reshape(-1), sinks.reshape(-1)]
    if extra is not None:
        tail.append(extra.reshape(-1))
    tail = jnp.concatenate(tail)
    tail = jnp.concatenate([tail, jnp.zeros((d - tail.shape[0],), F32)])
    rows = [ln_in_g.reshape(-1), ln_in_b.reshape(-1), ln1_g.reshape(-1), ln1_b.reshape(-1),
            ln2_g.reshape(-1), ln2_b.reshape(-1),
            jnp.concatenate([sb_g.reshape(-1), sw_g.reshape(-1)]), tail]
    return jnp.stack(rows)


def _unpack(p, wsb, n_rb, n_sk):
    return [p[0], p[1], p[6, :wsb][None], p[6, wsb:][None], p[7, n_rb:n_rb + n_sk][None],
            p[7, :n_rb].reshape(REL_BUCKETS, -1), p[2][None], p[3][None], p[4][None], p[5][None]]


def kernel(x, ln_in_g, ln_in_b, w_in, sb_norm_g, swa_norm_g, sinks, rel_bias, w_out, ln1_g, ln1_b, w_gate_up, w_down, ln2_g, ln2_b, loss_target, m_ln_in_g, m_ln_in_b, m_w_in, m_sb_norm_g, m_swa_norm_g, m_sinks, m_rel_bias, m_w_out, m_ln1_g, m_ln1_b, m_w_gate_up, m_w_down, m_ln2_g, m_ln2_b, v_ln_in_g, v_ln_in_b, v_w_in, v_sb_norm_g, v_swa_norm_g, v_sinks, v_rel_bias, v_w_out, v_ln1_g, v_ln1_b, v_w_gate_up, v_w_down, v_ln2_g, v_ln2_b):
    x2 = x[0]
    tgt = loss_target[0]
    s, d = x2.shape
    wsb = sb_norm_g.shape[-1]
    wsw = swa_norm_g.shape[-1]
    n_sw_heads = sinks.shape[-1]
    n_pairs = wsb // LANES
    dff = w_down.shape[1] * N_DEV
    assert wsb % LANES == 0 and wsw % LANES == 0 and n_sw_heads * HEAD_DIM == wsw
    assert 3 * wsb % wsw == 0 and dff % LANES == 0 and s % SWA_BLOCK == 0
    qcol = 3 * wsb // wsw
    kcol = (3 * wsb + wsw) // LANES
    vcol = kcol + 1
    assert w_in.shape[-1] * N_DEV == (vcol + 1) * LANES

    t2 = lambda a: jnp.transpose(a[0])
    big_w = [t2(w_in), w_out[0], t2(w_gate_up), w_down[0]]
    big_m = [t2(m_w_in), m_w_out[0], t2(m_w_gate_up), m_w_down[0]]
    big_v = [t2(v_w_in), v_w_out[0], t2(v_w_gate_up), v_w_down[0]]

    cat_rows = lambda g: g.reshape(N_DEV * g.shape[1], g.shape[2])
    shards = [w.astype(BF16) for w in big_w]
    w_in_t = cat_rows(_exchange("w_in_allgather", shards[:1], ["gather"])[0])

    vec = lambda a: a.reshape(1, -1)
    g_in, b_in = vec(ln_in_g), vec(ln_in_b)
    bucket = jnp.asarray(_swa_bucket_table())

    h0b, proj = _ln_proj(x2, g_in, b_in, w_in_t)
    sb_out, gathered = _sb_fwd(proj, n_pairs, comm=(shards[1:3], ["gather"] * 2))
    w_out_f, w_gu_t = cat_rows(gathered[0]), cat_rows(gathered[1])
    sw_out, gathered = _swa_fwd(proj, bucket, rel_bias, sinks, n_sw_heads, qcol, kcol, vcol,
                                comm=(shards[3:], ["gather"]))
    w_down_f = cat_rows(gathered[0])
    merged, u1 = _mix_ln1(sb_out, sw_out, x2, g_in, b_in, sb_norm_g, swa_norm_g, w_out_f)
    du1, h1b, act, dgu, du2b, st_ffn = _ffn(u1, ln1_g, ln1_b, w_gu_t, w_down_f, ln2_g, ln2_b, tgt)

    split_rows = lambda g: g.reshape(N_DEV, g.shape[0] // N_DEV, g.shape[1])
    gw_gu = _wgrad("wgrad_gate_up", [dgu], h1b, dff // 2, d)
    gw_down = _wgrad("wgrad_down", [act], du2b, dff // 2, d)
    gw_out = _wgrad("wgrad_out", [merged], du1, min(512, d), d)
    dsb, dsw, st_rms = _mix_bwd(du1, w_out_f, sb_out, sw_out, sb_norm_g, swa_norm_g)
    (dq_sb, dk_sb, dv_sb), (land_gu, land_out) = _sb_bwd(
        proj, dsb, sb_out, n_pairs, comm=([split_rows(gw_gu), split_rows(gw_out)], ["scatter"] * 2))
    (dq_sw, dk_sw, dv_sw, st_sink, st_rb), (land_down,) = _swa_bwd(
        proj, dsw, bucket, rel_bias, sinks, n_sw_heads, qcol, kcol, vcol,
        comm=([split_rows(gw_down)], ["scatter"]))
    pieces = [dq_sb, dk_sb, dv_sb, dq_sw, dk_sw, dv_sw]
    gw_in = _wgrad("wgrad_in", pieces, h0b, proj.shape[1], d)
    (grad_x, st_in), (land_in,) = _proj_bwd(pieces, w_in_t, du1, x2, g_in, comm=([split_rows(gw_in)], ["scatter"]))

    n_rb = rel_bias.size
    small = _pack(d, st_in[0], st_in[1], st_ffn[3], st_ffn[4], st_ffn[0], st_ffn[1],
                  st_rms[0, :wsb], st_rms[0, wsb:], st_rb[:, :n_sw_heads], st_sink[:n_sw_heads, 0],
                  extra=st_ffn[2, 0:1])
    land_small = _exchange("small_grads_allgather", [small], ["gather"])[0]
    landed = [land_in, land_out, land_gu, land_down, land_small]

    big = []
    for name, land, w, m, v in zip(["adamw_in", "adamw_out", "adamw_gate_up", "adamw_down"], landed[:4], big_w, big_m, big_v):
        big.append(_adamw(name, land, w, m, v, _adamw_rows(w.shape[0])))

    small_w = _pack(d, ln_in_g, ln_in_b, ln1_g, ln1_b, ln2_g, ln2_b, sb_norm_g, swa_norm_g, rel_bias, sinks)
    small_m = _pack(d, m_ln_in_g, m_ln_in_b, m_ln1_g, m_ln1_b, m_ln2_g, m_ln2_b, m_sb_norm_g, m_swa_norm_g, m_rel_bias, m_sinks)
    small_v = _pack(d, v_ln_in_g, v_ln_in_b, v_ln1_g, v_ln1_b, v_ln2_g, v_ln2_b, v_sb_norm_g, v_swa_norm_g, v_rel_bias, v_sinks)
    sg, sd, sm, sv = _adamw("adamw_small", landed[4], small_w, small_m, small_v, 8)
    n_sk = sinks.size
    loss = sg[7, n_rb + n_sk]

    def leaves(idx):
        sm_l = _unpack([sg, sd, sm, sv][idx], wsb, n_rb, n_sk)
        bg = [jnp.transpose(big[0][idx])[None], big[1][idx][None], jnp.transpose(big[2][idx])[None], big[3][idx][None]]
        return [sm_l[0], sm_l[1], bg[0], sm_l[2], sm_l[3], sm_l[4], sm_l[5], bg[1], sm_l[6], sm_l[7], bg[2], bg[3], sm_l[8], sm_l[9]]

    return (loss, grad_x[None], *leaves(0), *leaves(1), *leaves(2), *leaves(3))
```

```python
import functools
import math

import numpy as np
import jax
import jax.numpy as jnp
from jax import lax
from jax.experimental import pallas as pl
from jax.experimental.pallas import tpu as pltpu

F32 = jnp.float32
BF16 = jnp.bfloat16
MESH = pl.DeviceIdType.MESH

N_DEV = 8
LANES = 128
HEAD_DIM = 64
SCALE = HEAD_DIM ** -0.5
SWA_BLOCK = 128
REL_BUCKETS = 32
REL_MAX_DIST = 128
ALPHA = 2.0 ** 0.25
LN_EPS = 1e-5
RMS_EPS = 1e-6
ADAM_LR = 0.001
ADAM_B1 = 0.9
ADAM_B2 = 0.999
ADAM_EPS = 1e-08
ADAM_WD = 0.01
ADAM_STEP = 10

ROW_TILE = 512
SB_TILE = 256
FFN_TILE = 256
WGRAD_TOKENS = 2048
SB_UNDERFLOW = -110.0
MIB = 1024 * 1024


def _params(vmem_mib=48):
    return pltpu.CompilerParams(vmem_limit_bytes=vmem_mib * MIB)


def _dot(a, b):
    return jnp.dot(a, b, preferred_element_type=F32)


def _dot_nt(a, b):
    return lax.dot_general(a, b, (((1,), (1,)), ((), ())), preferred_element_type=F32)


def _dot_tn(a, b):
    return lax.dot_general(a, b, (((0,), (0,)), ((), ())), preferred_element_type=F32)


def _ln_hat(x):
    mu = jnp.mean(x, axis=-1, keepdims=True)
    xc = x - mu
    var = jnp.mean(xc * xc, axis=-1, keepdims=True)
    r = lax.rsqrt(var + LN_EPS)
    return xc * r, r


def _ln_bwd(dxhat, xhat, r):
    return r * (dxhat - jnp.mean(dxhat, axis=-1, keepdims=True)
                - xhat * jnp.mean(dxhat * xhat, axis=-1, keepdims=True))


def _colsum(a):
    return jnp.sum(a, axis=0, keepdims=True)


def _rowsum(a):
    return jnp.sum(a, axis=1, keepdims=True)


def _full(shape):
    return pl.BlockSpec(shape, lambda *_: (0,) * len(shape))


def _comm_out_shapes(arrays, kinds):
    shapes = []
    for a, kind in zip(arrays, kinds):
        blk = a.shape if kind == "gather" else a.shape[1:]
        shapes.append(jax.ShapeDtypeStruct((N_DEV,) + tuple(blk), a.dtype))
    return shapes


def _comm_sems(n):
    return [pltpu.SemaphoreType.DMA((n, N_DEV - 1)), pltpu.SemaphoreType.DMA((n, N_DEV - 1)),
            pltpu.SemaphoreType.DMA((n,))]


def _comm_copies(ins, outs, kinds, send_sems, recv_sems, local_sems):
    x, y, c = lax.axis_index("x"), lax.axis_index("y"), lax.axis_index("c")
    me = 4 * x + 2 * y + c

    def src_for(t, dev_lin):
        return ins[t] if kinds[t] == "gather" else ins[t].at[dev_lin]

    local = [pltpu.make_async_copy(src_for(t, me), outs[t].at[me], local_sems.at[t]) for t in range(len(kinds))]
    sends, arrivals = [], []
    for k in range(1, N_DEV):
        px = 1 - x if (k >> 2) & 1 else x
        py = 1 - y if (k >> 1) & 1 else y
        pc = 1 - c if k & 1 else c
        peer_lin = 4 * px + 2 * py + pc
        for t in range(len(kinds)):
            sems = dict(send_sem=send_sems.at[t, k - 1], recv_sem=recv_sems.at[t, k - 1],
                        device_id=(px, py, pc), device_id_type=MESH)
            sends.append(pltpu.make_async_remote_copy(src_ref=src_for(t, peer_lin), dst_ref=outs[t].at[me], **sems))
            arrivals.append(pltpu.make_async_remote_copy(src_ref=src_for(t, peer_lin), dst_ref=outs[t].at[peer_lin], **sems))
    return local, sends, arrivals


def _comm_start(ins, outs, kinds, sems):
    local, sends, _ = _comm_copies(ins, outs, kinds, *sems)
    for cp in local + sends:
        cp.start()


def _comm_finish(ins, outs, kinds, sems):
    local, sends, arrivals = _comm_copies(ins, outs, kinds, *sems)
    for cp in arrivals:
        cp.wait_recv()
    for cp in sends:
        cp.wait_send()
    for cp in local:
        cp.wait()


def _exchange(name, arrays, kinds):
    n = len(arrays)

    def body(*refs):
        ins, outs, sems = refs[:n], refs[n:2 * n], refs[2 * n:]
        _comm_start(ins, outs, kinds, sems)
        _comm_finish(ins, outs, kinds, sems)

    any_spec = pl.BlockSpec(memory_space=pl.ANY)
    return pl.pallas_call(
        body, name=name, out_shape=_comm_out_shapes(arrays, kinds),
        in_specs=[any_spec] * n, out_specs=[any_spec] * n,
        scratch_shapes=_comm_sems(n),
    )(*arrays)


def _call(body, name, grid, in_specs, out_specs, out_shape, args, scratch_shapes=(), comm=None):
    if comm is None:
        outs = pl.pallas_call(body, name=name, grid=grid, in_specs=in_specs, out_specs=out_specs,
                              out_shape=out_shape, scratch_shapes=list(scratch_shapes),
                              compiler_params=_params())(*args)
        return outs, []
    arrays, kinds = comm
    n, n_in, n_out, n_scr = len(arrays), len(in_specs), len(out_specs), len(scratch_shapes)

    def fused(*refs):
        c_in, x_in = refs[:n_in], refs[n_in:n_in + n]
        c_out = refs[n_in + n:n_in + n + n_out]
        x_out = refs[n_in + n + n_out:n_in + 2 * n + n_out]
        rest = refs[n_in + 2 * n + n_out:]
        c_scr, sems = rest[:n_scr], rest[n_scr:]
        ids = [pl.program_id(a) for a in range(len(grid))]
        is_first = functools.reduce(jnp.logical_and, [i == 0 for i in ids])
        is_last = functools.reduce(jnp.logical_and, [i == g - 1 for i, g in zip(ids, grid)])

        @pl.when(is_first)
        def _():
            _comm_start(x_in, x_out, kinds, sems)

        body(*c_in, *c_out, *c_scr)

        @pl.when(is_last)
        def _():
            _comm_finish(x_in, x_out, kinds, sems)

    any_spec = pl.BlockSpec(memory_space=pl.ANY)
    outs = pl.pallas_call(
        fused, name=name, grid=grid,
        in_specs=list(in_specs) + [any_spec] * n, out_specs=list(out_specs) + [any_spec] * n,
        out_shape=list(out_shape) + _comm_out_shapes(arrays, kinds),
        scratch_shapes=list(scratch_shapes) + _comm_sems(n),
        compiler_params=_params())(*args, *arrays)
    return outs[:n_out], outs[n_out:]


def _ln_proj(x, g, b, w_in_t):
    s, d = x.shape
    cols = w_in_t.shape[0]
    tm = min(ROW_TILE, s)

    def body(x_ref, g_ref, b_ref, w_ref, h_ref, p_ref):
        xhat, _ = _ln_hat(x_ref[...])
        h = (xhat * g_ref[...] + b_ref[...]).astype(BF16)
        h_ref[...] = h
        p_ref[...] = _dot_nt(h, w_ref[...]).astype(BF16)

    row = lambda width: pl.BlockSpec((tm, width), lambda i: (i, 0))
    return pl.pallas_call(
        body, name="ln_proj", grid=(s // tm,),
        in_specs=[row(d), _full((1, d)), _full((1, d)), _full((cols, d))],
        out_specs=[row(d), row(cols)],
        out_shape=[jax.ShapeDtypeStruct((s, d), BF16), jax.ShapeDtypeStruct((s, cols), BF16)],
        compiler_params=_params(),
    )(x, g, b, w_in_t)


def _sb_triangles(t):
    row = lax.broadcasted_iota(jnp.int32, (t, t), 0)
    col = lax.broadcasted_iota(jnp.int32, (t, t), 1)
    return (row > col).astype(BF16), (row >= col).astype(BF16)


def _sb_first_mask(t, has_prev):
    qrow = lax.broadcasted_iota(jnp.int32, (2 * t, 2 * t), 0) & (t - 1)
    col = lax.broadcasted_iota(jnp.int32, (2 * t, 2 * t), 1)
    return ((col < t) & has_prev) | ((col >= t) & (col - t < qrow))


def _sb_stack_heads(x2, first):
    zero = jnp.zeros_like(x2)
    return jnp.concatenate([jnp.where(first, x2, zero), jnp.where(first, zero, x2)], axis=0)


def _sb_key_tiles(ref, offs, t):
    tiles = [ref[pl.ds(off, t), :] for off in offs]
    return tiles[0] if len(tiles) == 1 else jnp.concatenate(tiles, axis=0)


def _sb_suffix(terms, row_sums, tri, carry):
    out = [None] * len(terms)
    for j in reversed(range(len(terms))):
        suf = carry
        for op in terms[j]:
            suf = suf + _dot(op, tri)
        out[j] = suf
        carry = carry + row_sums[j]
    return (out[0] if len(out) == 1 else jnp.concatenate(out, axis=1)), carry


def _sb_scores(qh, k_t, upper, carry_l, mask, t):
    z = _dot_nt(qh, k_t)
    sp = jnp.log(1.0 + jnp.exp(-jnp.abs(z)))
    neg = jnp.minimum(z, 0.0)
    lb = neg - sp
    l1 = (neg - z) - sp
    if mask is not None:
        l1 = jnp.where(mask, l1, 0.0)
    hi = l1.astype(BF16)
    lo = (l1 - hi.astype(F32)).astype(BF16)
    cols = [slice(j * t, (j + 1) * t) for j in range(z.shape[1] // t)]
    suf, carry_l = _sb_suffix([[hi[:, c], lo[:, c]] for c in cols], [_rowsum(l1[:, c]) for c in cols], upper, carry_l)
    a = jnp.exp(lb + suf)
    if mask is not None:
        a = jnp.where(mask, a, 0.0)
    return lb, a, carry_l


def _sb_walk(i, t, visit, init):
    def alive(carry):
        return jnp.max(carry[0]) > SB_UNDERFLOW

    prev = pl.multiple_of(jnp.maximum(i - 1, 0) * t, t)
    carry = visit((prev, pl.multiple_of(i * t, t)), init, _sb_first_mask(t, i > 0))

    def cond(state):
        j, go, _ = state
        return (j < i - 1) & go

    def body(state):
        j, _, carry = state
        carry = visit((pl.multiple_of((i - 2 - j) * t, t),), carry, None)
        return j + 1, alive(carry), carry

    return lax.while_loop(cond, body, (jnp.int32(0), alive(carry), carry))[2]


def _sb_fwd(proj, n_pairs, comm=None):
    s = proj.shape[0]
    t = min(SB_TILE, s)
    nq = s // t

    def body(q_ref, k_ref, v_ref, o_ref):
        i = pl.program_id(1)
        lane = lax.broadcasted_iota(jnp.int32, (1, LANES), 1)
        first = lane < HEAD_DIM
        qs = _sb_stack_heads(q_ref[...] * SCALE, first)
        upper, _ = _sb_triangles(t)

        def visit(offs, carry, mask):
            c_l, acc = carry
            _, a, c_l = _sb_scores(qs, _sb_key_tiles(k_ref, offs, t), upper, c_l, mask, t)
            return c_l, acc + _dot(a.astype(BF16), _sb_key_tiles(v_ref, offs, t))

        init = (jnp.zeros((2 * t, 1), F32), jnp.zeros((2 * t, LANES), F32))
        _, acc = _sb_walk(i, t, visit, init)
        o_ref[...] = jnp.where(first, acc[:t], acc[t:])

    outs, landed = _call(
        body, "sb_fwd", (n_pairs, nq),
        in_specs=[pl.BlockSpec((t, LANES), lambda h, i: (i, h)),
                  pl.BlockSpec((s, LANES), lambda h, i: (0, n_pairs + h)),
                  pl.BlockSpec((s, LANES), lambda h, i: (0, 2 * n_pairs + h))],
        out_specs=[pl.BlockSpec((t, LANES), lambda h, i: (i, h))],
        out_shape=[jax.ShapeDtypeStruct((s, n_pairs * LANES), F32)],
        args=(proj, proj, proj), comm=comm)
    return outs[0], landed


def _swa_bucket_table():
    qi = np.arange(SWA_BLOCK)[:, None]
    cj = np.arange(2 * SWA_BLOCK)[None, :]
    dist = qi + SWA_BLOCK - cj
    exact = REL_BUCKETS // 2
    d = np.maximum(dist, 0)
    d_f = np.maximum(d, 1).astype(np.float32)
    large = exact + (np.log(d_f / np.float32(exact)) / np.float32(math.log(REL_MAX_DIST / exact))
                     * np.float32(REL_BUCKETS - exact)).astype(np.int32)
    large = np.minimum(large, REL_BUCKETS - 1)
    return np.where(d < exact, d, large).astype(np.int32)


def _swa_build_bias(bucket_ref, rb_ref, bias_ref, n_groups, per_group):
    bk = bucket_ref[...]
    dist = (lax.broadcasted_iota(jnp.int32, bk.shape, 0) + SWA_BLOCK) - lax.broadcasted_iota(jnp.int32, bk.shape, 1)
    window = (dist >= 0) & (dist < SWA_BLOCK)
    for g in range(n_groups):
        for hh in range(per_group):
            acc = jnp.zeros(bk.shape, F32)
            for b in range(REL_BUCKETS):
                acc = jnp.where(bk == b, rb_ref[b, g * per_group + hh], acc)
            bias_ref[g, hh * SWA_BLOCK:(hh + 1) * SWA_BLOCK, :] = jnp.where(window, acc, -jnp.inf)


def _swa_first_block_mask(i):
    col = lax.broadcasted_iota(jnp.int32, (1, 2 * SWA_BLOCK), 1)
    return jnp.where((col < SWA_BLOCK) & (i == 0), -jnp.inf, 0.0)


def _swa_place(blk, h, group, sel):
    if (h % 2) != group:
        blk = pltpu.roll(blk.astype(F32), HEAD_DIM, axis=1).astype(BF16)
    return jnp.where(sel, blk, jnp.zeros_like(blk))


def _swa_stack(ref, group, per_group, sel, scale=1.0):
    parts = []
    for hh in range(per_group):
        h = group * per_group + hh
        parts.append(_swa_place(ref[:, (h // 2) * LANES:(h // 2 + 1) * LANES], h, group, sel))
    stacked = jnp.concatenate(parts, axis=0)
    return stacked if scale == 1.0 else stacked * scale


def _swa_unstack(stacked, group, per_group, pieces):
    for hh in range(per_group):
        h = group * per_group + hh
        piece = stacked[hh * SWA_BLOCK:(hh + 1) * SWA_BLOCK, :]
        pieces[h] = pltpu.roll(piece, HEAD_DIM, axis=1) if (h % 2) != group else piece


def _swa_sink_rows(sk_ref, group, per_group):
    rowh = lax.broadcasted_iota(jnp.int32, (per_group * SWA_BLOCK, 1), 0) // SWA_BLOCK
    sink = jnp.zeros((per_group * SWA_BLOCK, 1), F32) + sk_ref[0, group * per_group]
    for hh in range(1, per_group):
        sink = jnp.where(rowh == hh, sk_ref[0, group * per_group + hh], sink)
    return sink


def _swa_probs(q_pos, kcat, bias_h, first_mask, sink):
    logits = _dot_nt(q_pos, kcat) + (bias_h + first_mask)
    m = jnp.maximum(jnp.max(logits, axis=1, keepdims=True), sink)
    p = jnp.exp(logits - m)
    es = jnp.exp(sink - m)
    inv = 1.0 / (_rowsum(p) + es)
    return p * inv, es * inv


def _swa_specs(n_heads, qcol, kcol, vcol):
    width = n_heads * HEAD_DIM
    prev = lambda col: pl.BlockSpec((SWA_BLOCK, LANES), lambda i: (jnp.maximum(i - 1, 0), col))
    cur = lambda col: pl.BlockSpec((SWA_BLOCK, LANES), lambda i: (i, col))
    return [pl.BlockSpec((SWA_BLOCK, width), lambda i: (i, qcol)),
            prev(kcol), cur(kcol), prev(vcol), cur(vcol),
            _full((SWA_BLOCK, 2 * SWA_BLOCK)),
            pl.BlockSpec(memory_space=pltpu.SMEM), pl.BlockSpec(memory_space=pltpu.SMEM)]


def _swa_fwd(proj, bucket, rel_bias, sinks, n_heads, qcol, kcol, vcol, comm=None):
    s = proj.shape[0]
    width = n_heads * HEAD_DIM
    n_groups = LANES // HEAD_DIM
    per_group = n_heads // n_groups

    def body(q_ref, kp_ref, kc_ref, vp_ref, vc_ref, bucket_ref, rb_ref, sk_ref, o_ref, bias_ref):
        i = pl.program_id(0)

        @pl.when(i == 0)
        def _():
            _swa_build_bias(bucket_ref, rb_ref, bias_ref, n_groups, per_group)

        lane = lax.broadcasted_iota(jnp.int32, (1, LANES), 1)
        first = lane < HEAD_DIM
        first_mask = _swa_first_block_mask(i)
        kcat = jnp.concatenate([kp_ref[...], kc_ref[...]], axis=0)
        vcat = jnp.concatenate([vp_ref[...], vc_ref[...]], axis=0)
        pieces = {}
        for g in range(n_groups):
            sel = first if g == 0 else jnp.logical_not(first)
            prob, _ = _swa_probs(_swa_stack(q_ref, g, per_group, sel, SCALE), kcat, bias_ref[g], first_mask,
                                 _swa_sink_rows(sk_ref, g, per_group))
            _swa_unstack(_dot(prob.astype(BF16), vcat), g, per_group, pieces)
        for j in range(n_heads // 2):
            o_ref[:, j * LANES:(j + 1) * LANES] = jnp.where(first, pieces[2 * j], pieces[2 * j + 1])

    outs, landed = _call(
        body, "swa_fwd", (s // SWA_BLOCK,),
        in_specs=_swa_specs(n_heads, qcol, kcol, vcol),
        out_specs=[pl.BlockSpec((SWA_BLOCK, width), lambda i: (i, 0))],
        out_shape=[jax.ShapeDtypeStruct((s, width), F32)],
        args=(proj, proj, proj, proj, proj, bucket, rel_bias, sinks),
        scratch_shapes=[pltpu.VMEM((n_groups, per_group * SWA_BLOCK, 2 * SWA_BLOCK), F32)], comm=comm)
    return outs[0], landed


def _rms_fwd(o, g):
    r = lax.rsqrt(jnp.mean(o * o, axis=-1, keepdims=True) + RMS_EPS)
    n = o * r
    return n, r, n * g


def _mix_ffn(sb_out, sw_out, x, g_in, b_in, sb_g, sw_g, w_out, g1, b1, w_gu_t, w_down, g2, b2, target):
    s, d = x.shape
    wsb, wsw = sb_out.shape[1], sw_out.shape[1]
    dff = w_down.shape[0]
    assert wsb + wsw == d
    tm = min(FFN_TILE, s)

    def body(sb_ref, sw_ref, x_ref, gi_ref, bi_ref, sbg_ref, swg_ref, wo_hbm, g1_ref, b1_ref, wgu_hbm, wd_hbm,
             g2_ref, b2_ref, t_ref,
             du1_ref, mg_ref, h1b_ref, act_ref, dgu_ref, du2b_ref, dsb_ref, dsw_ref, st_ref,
             wo_ref, wgu_ref, wd_ref):
        @pl.when(pl.program_id(0) == 0)
        def _():
            pltpu.sync_copy(wo_hbm, wo_ref)
            pltpu.sync_copy(wgu_hbm, wgu_ref)
            pltpu.sync_copy(wd_hbm, wd_ref)
            st_ref[...] = jnp.zeros_like(st_ref)

        sb, sw = sb_ref[...], sw_ref[...]
        _, _, m_sb = _rms_fwd(sb, sbg_ref[...])
        _, _, m_sw = _rms_fwd(sw, swg_ref[...])
        m_sb = m_sb.astype(BF16)
        m_sw = m_sw.astype(BF16)
        mg_ref[:, :wsb] = m_sb
        mg_ref[:, wsb:] = m_sw
        xhat0, _ = _ln_hat(x_ref[...])
        u1 = ALPHA * (xhat0 * gi_ref[...] + bi_ref[...]) + _dot(m_sb, wo_ref[:wsb, :]) + _dot(m_sw, wo_ref[wsb:, :])

        xhat1, r1 = _ln_hat(u1)
        h1 = xhat1 * g1_ref[...] + b1_ref[...]
        h1b = h1.astype(BF16)
        h1b_ref[...] = h1b
        gate = _dot_nt(h1b, wgu_ref[:dff, :])
        up = _dot_nt(h1b, wgu_ref[dff:, :])
        sg = jax.nn.sigmoid(gate)
        silu = gate * sg
        act = (silu * up).astype(BF16)
        act_ref[...] = act
        u2 = ALPHA * h1 + _dot(act, wd_ref[...])
        xhat2, r2 = _ln_hat(u2)
        diff = xhat2 * g2_ref[...] + b2_ref[...] - t_ref[...]
        dh2 = diff * (1.0 / d)
        st_ref[0:1, :] += _colsum(dh2 * xhat2)
        st_ref[1:2, :] += _colsum(dh2)
        st_ref[2:3, :] += jnp.broadcast_to(_colsum(_rowsum(diff * diff)) * (0.5 / d), (1, d))
        du2 = _ln_bwd(dh2 * g2_ref[...], xhat2, r2)
        du2b = du2.astype(BF16)
        du2b_ref[...] = du2b
        dact = _dot_nt(du2b, wd_ref[...])
        dgate = (dact * up * (sg * (1.0 + gate * (1.0 - sg)))).astype(BF16)
        dup = (dact * silu).astype(BF16)
        dgu_ref[:, :dff] = dgate
        dgu_ref[:, dff:] = dup
        dh1 = _dot(dgate, wgu_ref[:dff, :]) + _dot(dup, wgu_ref[dff:, :]) + ALPHA * du2
        st_ref[3:4, :] += _colsum(dh1 * xhat1)
        st_ref[4:5, :] += _colsum(dh1)
        du1 = _ln_bwd(dh1 * g1_ref[...], xhat1, r1)
        du1_ref[...] = du1

        dmerged = _dot_nt(du1.astype(BF16), wo_ref[...])
        dsb, gsb = _rms_bwd(dmerged[:, :wsb], sb, sbg_ref[...])
        dsw, gsw = _rms_bwd(dmerged[:, wsb:], sw, swg_ref[...])
        dsb_ref[...] = dsb.astype(BF16)
        dsw_ref[...] = dsw.astype(BF16)
        st_ref[5:6, :wsb] += gsb
        st_ref[5:6, wsb:] += gsw

    row = lambda width: pl.BlockSpec((tm, width), lambda i: (i, 0))
    vec = lambda width: _full((1, width))
    hbm = pl.BlockSpec(memory_space=pl.ANY)
    bf = lambda width: jax.ShapeDtypeStruct((s, width), BF16)
    return pl.pallas_call(
        body, name="mix_ffn", grid=(s // tm,),
        in_specs=[row(wsb), row(wsw), row(d), vec(d), vec(d), vec(wsb), vec(wsw), hbm, vec(d), vec(d), hbm, hbm,
                  vec(d), vec(d), row(d)],
        out_specs=[row(d), row(d), row(d), row(dff), row(2 * dff), row(d), row(wsb), row(wsw), _full((8, d))],
        out_shape=[jax.ShapeDtypeStruct((s, d), F32), bf(d), bf(d), bf(dff), bf(2 * dff), bf(d), bf(wsb), bf(wsw),
                   jax.ShapeDtypeStruct((8, d), F32)],
        scratch_shapes=[pltpu.VMEM(w_out.shape, BF16), pltpu.VMEM(w_gu_t.shape, BF16), pltpu.VMEM(w_down.shape, BF16)],
        compiler_params=_params(60),
    )(sb_out, sw_out, x, g_in, b_in, sb_g, sw_g, w_out, g1, b1, w_gu_t, w_down, g2, b2, target)


def _rms_bwd(dm, o, g):
    n, r, _ = _rms_fwd(o, g)
    dn = dm * g
    return r * (dn - n * jnp.mean(dn * n, axis=-1, keepdims=True)), _colsum(dm * n)


def _sb_bwd(proj, dout, out, n_pairs, comm=None):
    s = proj.shape[0]
    t = min(SB_TILE, s)
    nq = s // t
    width = n_pairs * LANES

    def body(q_ref, k_ref, v_ref, do_ref, o_ref, dq_ref, dk_out, dv_out, dk_ref, dv_ref):
        i = pl.program_id(1)

        @pl.when(i == 0)
        def _():
            dk_ref[...] = jnp.zeros_like(dk_ref)
            dv_ref[...] = jnp.zeros_like(dv_ref)

        lane = lax.broadcasted_iota(jnp.int32, (1, LANES), 1)
        first = lane < HEAD_DIM
        do2 = do_ref[...]
        qs = _sb_stack_heads(q_ref[...] * SCALE, first)
        dos = _sb_stack_heads(do2, first)
        prod = do2.astype(F32) * o_ref[...]
        totals = jnp.concatenate([_rowsum(jnp.where(first, prod, 0.0)), _rowsum(jnp.where(first, 0.0, prod))], axis=0)
        upper, incl = _sb_triangles(t)

        def visit(offs, carry, mask):
            k_t = _sb_key_tiles(k_ref, offs, t)
            v_t = _sb_key_tiles(v_ref, offs, t)
            c_l, c_e, dq = carry
            lb, a, c_l = _sb_scores(qs, k_t, upper, c_l, mask, t)
            a_b = a.astype(BF16)
            d_e = _dot_nt(dos, v_t) * a_b.astype(F32)
            d_eb = d_e.astype(BF16)
            cols = [slice(j * t, (j + 1) * t) for j in range(len(offs))]
            suf_e, c_e = _sb_suffix([[d_eb[:, c]] for c in cols], [_rowsum(d_e[:, c]) for c in cols], incl, c_e)
            dz = d_e - jnp.exp(lb) * (d_e + (totals - suf_e))
            if mask is not None:
                dz = jnp.where(mask, dz, 0.0)
            dzb = dz.astype(BF16)
            dk_t = _dot_tn(dzb, qs)
            dv_t = _dot_tn(a_b, dos)
            for off, c in zip(offs, cols):
                dk_ref[pl.ds(off, t), :] += dk_t[c, :]
                dv_ref[pl.ds(off, t), :] += dv_t[c, :]
            return c_l, c_e, dq + _dot(dzb, k_t)

        init = (jnp.zeros((2 * t, 1), F32), jnp.zeros((2 * t, 1), F32), jnp.zeros((2 * t, LANES), F32))
        _, _, dq = _sb_walk(i, t, visit, init)
        dq_ref[...] = (jnp.where(first, dq[:t], dq[t:]) * SCALE).astype(BF16)

        @pl.when(i == nq - 1)
        def _():
            dk_out[...] = dk_ref[...].astype(BF16)
            dv_out[...] = dv_ref[...].astype(BF16)

    qblk = pl.BlockSpec((t, LANES), lambda h, i: (i, h))
    whole = pl.BlockSpec((s, LANES), lambda h, i: (0, h))
    return _call(
        body, "sb_bwd", (n_pairs, nq),
        in_specs=[qblk,
                  pl.BlockSpec((s, LANES), lambda h, i: (0, n_pairs + h)),
                  pl.BlockSpec((s, LANES), lambda h, i: (0, 2 * n_pairs + h)),
                  qblk, qblk],
        out_specs=[qblk, whole, whole],
        out_shape=[jax.ShapeDtypeStruct((s, width), BF16)] * 3,
        args=(proj, proj, proj, dout, out),
        scratch_shapes=[pltpu.VMEM((s, LANES), F32), pltpu.VMEM((s, LANES), F32)], comm=comm)


def _swa_bwd(proj, dout, bucket, rel_bias, sinks, n_heads, qcol, kcol, vcol, comm=None):
    s = proj.shape[0]
    width = n_heads * HEAD_DIM
    n_groups = LANES // HEAD_DIM
    per_group = n_heads // n_groups
    nb = s // SWA_BLOCK

    def body(q_ref, kp_ref, kc_ref, vp_ref, vc_ref, bucket_ref, rb_ref, sk_ref, do_ref,
             dq_ref, dk_out, dv_out, dsk_ref, drb_ref, bias_ref, dbias_ref, dk_ref, dv_ref):
        i = pl.program_id(0)

        @pl.when(i == 0)
        def _():
            _swa_build_bias(bucket_ref, rb_ref, bias_ref, n_groups, per_group)
            dbias_ref[...] = jnp.zeros_like(dbias_ref)
            dk_ref[...] = jnp.zeros_like(dk_ref)
            dv_ref[...] = jnp.zeros_like(dv_ref)
            dsk_ref[...] = jnp.zeros_like(dsk_ref)

        lane = lax.broadcasted_iota(jnp.int32, (1, LANES), 1)
        first = lane < HEAD_DIM
        first_mask = _swa_first_block_mask(i)
        kcat = jnp.concatenate([kp_ref[...], kc_ref[...]], axis=0)
        vcat = jnp.concatenate([vp_ref[...], vc_ref[...]], axis=0)
        dkcat = jnp.zeros((2 * SWA_BLOCK, LANES), F32)
        dvcat = jnp.zeros((2 * SWA_BLOCK, LANES), F32)
        pieces = {}
        for g in range(n_groups):
            sel = first if g == 0 else jnp.logical_not(first)
            q_g = _swa_stack(q_ref, g, per_group, sel, SCALE)
            do_g = _swa_stack(do_ref, g, per_group, sel)
            prob, p_sink = _swa_probs(q_g, kcat, bias_ref[g], first_mask, _swa_sink_rows(sk_ref, g, per_group))
            dprob = _dot_nt(do_g, vcat)
            delta = _rowsum(prob * dprob)
            dlog = prob * (dprob - delta)
            sink_term = p_sink * delta
            for hh in range(per_group):
                h = g * per_group + hh
                tot = _colsum(sink_term[hh * SWA_BLOCK:(hh + 1) * SWA_BLOCK, :])
                dsk_ref[h:h + 1, :] += jnp.broadcast_to(-tot, (1, LANES))
            dbias_ref[g] += dlog
            dlb = dlog.astype(BF16)
            _swa_unstack(_dot(dlb, kcat) * SCALE, g, per_group, pieces)
            dkcat += _dot_tn(dlb, q_g)
            dvcat += _dot_tn(prob.astype(BF16), do_g)
        for j in range(n_heads // 2):
            dq_ref[:, j * LANES:(j + 1) * LANES] = jnp.where(first, pieces[2 * j], pieces[2 * j + 1]).astype(BF16)

        cur = pl.multiple_of(i * SWA_BLOCK, SWA_BLOCK)
        dk_ref[pl.ds(cur, SWA_BLOCK), :] += dkcat[SWA_BLOCK:, :]
        dv_ref[pl.ds(cur, SWA_BLOCK), :] += dvcat[SWA_BLOCK:, :]

        @pl.when(i > 0)
        def _():
            prv = pl.multiple_of((i - 1) * SWA_BLOCK, SWA_BLOCK)
            dk_ref[pl.ds(prv, SWA_BLOCK), :] += dkcat[:SWA_BLOCK, :]
            dv_ref[pl.ds(prv, SWA_BLOCK), :] += dvcat[:SWA_BLOCK, :]

        @pl.when(i == nb - 1)
        def _():
            bk = bucket_ref[...]
            rowi = lax.broadcasted_iota(jnp.int32, (REL_BUCKETS, LANES), 0)
            coli = lax.broadcasted_iota(jnp.int32, (REL_BUCKETS, LANES), 1)
            res = jnp.zeros((REL_BUCKETS, LANES), F32)
            for h in range(n_heads):
                g, hh = divmod(h, per_group)
                db = dbias_ref[g, hh * SWA_BLOCK:(hh + 1) * SWA_BLOCK, :]
                for b in range(REL_BUCKETS):
                    tot = _colsum(_rowsum(jnp.where(bk == b, db, 0.0)))
                    res = jnp.where((rowi == b) & (coli == h), tot, res)
            drb_ref[...] = res
            dk_out[...] = dk_ref[...].astype(BF16)
            dv_out[...] = dv_ref[...].astype(BF16)

    in_specs = _swa_specs(n_heads, qcol, kcol, vcol) + [pl.BlockSpec((SWA_BLOCK, width), lambda i: (i, 0))]
    return _call(
        body, "swa_bwd", (nb,),
        in_specs=in_specs,
        out_specs=[pl.BlockSpec((SWA_BLOCK, width), lambda i: (i, 0)),
                   _full((s, LANES)), _full((s, LANES)), _full((8, LANES)), _full((REL_BUCKETS, LANES))],
        out_shape=[jax.ShapeDtypeStruct((s, width), BF16), jax.ShapeDtypeStruct((s, LANES), BF16),
                   jax.ShapeDtypeStruct((s, LANES), BF16), jax.ShapeDtypeStruct((8, LANES), F32),
                   jax.ShapeDtypeStruct((REL_BUCKETS, LANES), F32)],
        args=(proj, proj, proj, proj, proj, bucket, rel_bias, sinks, dout),
        scratch_shapes=[pltpu.VMEM((n_groups, per_group * SWA_BLOCK, 2 * SWA_BLOCK), F32),
                        pltpu.VMEM((n_groups, per_group * SWA_BLOCK, 2 * SWA_BLOCK), F32),
                        pltpu.VMEM((s, LANES), F32), pltpu.VMEM((s, LANES), F32)],
        comm=comm)


def _proj_bwd(pieces, w_in_t, du1, x, g_in, comm=None):
    s, d = x.shape
    cols = w_in_t.shape[0]
    tm = min(ROW_TILE, s)
    n_p = len(pieces)

    def body(*refs):
        p_refs = refs[:n_p]
        w_ref, du_ref, x_ref, g_ref, dx_ref, st_ref = refs[n_p:]
        i = pl.program_id(0)

        @pl.when(i == 0)
        def _():
            st_ref[...] = jnp.zeros_like(st_ref)

        dproj = jnp.concatenate([p[...] for p in p_refs], axis=1)
        dh0 = _dot(dproj, w_ref[...]) + ALPHA * du_ref[...]
        xhat, r = _ln_hat(x_ref[...])
        st_ref[0:1, :] += _colsum(dh0 * xhat)
        st_ref[1:2, :] += _colsum(dh0)
        dx_ref[...] = _ln_bwd(dh0 * g_ref[...], xhat, r)

    row = lambda width: pl.BlockSpec((tm, width), lambda i: (i, 0))
    return _call(
        body, "proj_bwd", (s // tm,),
        in_specs=[row(p.shape[1]) for p in pieces] + [_full((cols, d)), row(d), row(d), _full((1, d))],
        out_specs=[row(d), _full((8, d))],
        out_shape=[jax.ShapeDtypeStruct((s, d), F32), jax.ShapeDtypeStruct((8, d), F32)],
        args=(*pieces, w_in_t, du1, x, g_in), comm=comm)


def _wgrad(name, pieces, b, tm, tn):
    s, n = b.shape
    m = sum(p.shape[1] for p in pieces)
    n_p = len(pieces)
    assert n_p == 1 or tm == m
    ts = min(WGRAD_TOKENS if b.dtype == BF16 and n_p == 1 else WGRAD_TOKENS // 2, s)
    n_k = s // ts

    def body(*refs):
        p_refs, b_ref, o_ref, acc_ref = refs[:n_p], refs[n_p], refs[n_p + 1], refs[n_p + 2]
        k = pl.program_id(2)

        @pl.when(k == 0)
        def _():
            acc_ref[...] = jnp.zeros_like(acc_ref)

        a = p_refs[0][...] if n_p == 1 else jnp.concatenate([p[...] for p in p_refs], axis=1)
        acc_ref[...] += _dot_tn(a, b_ref[...].astype(BF16))

        @pl.when(k == n_k - 1)
        def _():
            o_ref[...] = acc_ref[...].astype(BF16)

    piece_spec = lambda p: pl.BlockSpec((ts, tm if n_p == 1 else p.shape[1]), lambda i, j, k: (k, i))
    return pl.pallas_call(
        body, name=name, grid=(m // tm, n // tn, n_k),
        in_specs=[piece_spec(p) for p in pieces] + [pl.BlockSpec((ts, tn), lambda i, j, k: (k, j))],
        out_specs=pl.BlockSpec((tm, tn), lambda i, j, k: (i, j)),
        out_shape=jax.ShapeDtypeStruct((m, n), BF16),
        scratch_shapes=[pltpu.VMEM((tm, tn), F32)],
        compiler_params=_params(),
    )(*pieces, b)


def _adamw_math(w, g, m, v):
    m = ADAM_B1 * m + (1.0 - ADAM_B1) * g
    v = ADAM_B2 * v + (1.0 - ADAM_B2) * (g * g)
    m_hat = m / (1.0 - ADAM_B1 ** ADAM_STEP)
    v_hat = v / (1.0 - ADAM_B2 ** ADAM_STEP)
    delta = -ADAM_LR * (m_hat / (jnp.sqrt(v_hat) + ADAM_EPS) + ADAM_WD * w)
    return delta, m, v


def _adamw_rows(rows):
    return max(r for r in range(16, 257, 16) if rows % r == 0)


def _adamw(name, landed, w, m, v, tr):
    rows, cols = w.shape

    def body(l_ref, w_ref, m_ref, v_ref, g_ref, d_ref, nm_ref, nv_ref):
        g = l_ref[0].astype(F32)
        for src in range(1, N_DEV):
            g = g + l_ref[src].astype(F32)
        delta, nm, nv = _adamw_math(w_ref[...], g, m_ref[...], v_ref[...])
        g_ref[...] = g
        d_ref[...] = delta
        nm_ref[...] = nm
        nv_ref[...] = nv

    blk = pl.BlockSpec((tr, cols), lambda i: (i, 0))
    shape = jax.ShapeDtypeStruct((rows, cols), F32)
    return pl.pallas_call(
        body, name=name, grid=(rows // tr,),
        in_specs=[pl.BlockSpec((N_DEV, tr, cols), lambda i: (0, i, 0)), blk, blk, blk],
        out_specs=[blk, blk, blk, blk],
        out_shape=[shape, shape, shape, shape],
        compiler_params=_params(),
    )(landed, w, m, v)


def _pack(d, ln_in_g, ln_in_b, ln1_g, ln1_b, ln2_g, ln2_b, sb_g, sw_g, rel_bias, sinks, extra=None):
    tail = [rel_bias.reshape(-1), sinks.reshape(-1)]
    if extra is not None:
        tail.append(extra.reshape(-1))
    tail = jnp.concatenate(tail)
    tail = jnp.concatenate([tail, jnp.zeros((d - tail.shape[0],), F32)])
    rows = [ln_in_g.reshape(-1), ln_in_b.reshape(-1), ln1_g.reshape(-1), ln1_b.reshape(-1),
            ln2_g.reshape(-1), ln2_b.reshape(-1),
            jnp.concatenate([sb_g.reshape(-1), sw_g.reshape(-1)]), tail]
    return jnp.stack(rows)


def _unpack(p, wsb, n_rb, n_sk):
    return [p[0], p[1], p[6, :wsb][None], p[6, wsb:][None], p[7, n_rb:n_rb + n_sk][None],
            p[7, :n_rb].reshape(REL_BUCKETS, -1), p[2][None], p[3][None], p[4][None], p[5][None]]


def kernel(x, ln_in_g, ln_in_b, w_in, sb_norm_g, swa_norm_g, sinks, rel_bias, w_out, ln1_g, ln1_b, w_gate_up, w_down, ln2_g, ln2_b, loss_target, m_ln_in_g, m_ln_in_b, m_w_in, m_sb_norm_g, m_swa_norm_g, m_sinks, m_rel_bias, m_w_out, m_ln1_g, m_ln1_b, m_w_gate_up, m_w_down, m_ln2_g, m_ln2_b, v_ln_in_g, v_ln_in_b, v_w_in, v_sb_norm_g, v_swa_norm_g, v_sinks, v_rel_bias, v_w_out, v_ln1_g, v_ln1_b, v_w_gate_up, v_w_down, v_ln2_g, v_ln2_b):
    x2 = x[0]
    tgt = loss_target[0]
    s, d = x2.shape
    wsb = sb_norm_g.shape[-1]
    wsw = swa_norm_g.shape[-1]
    n_sw_heads = sinks.shape[-1]
    n_pairs = wsb // LANES
    dff = w_down.shape[1] * N_DEV
    assert wsb % LANES == 0 and wsw % LANES == 0 and n_sw_heads * HEAD_DIM == wsw
    assert 3 * wsb % wsw == 0 and dff % LANES == 0 and s % SWA_BLOCK == 0
    qcol = 3 * wsb // wsw
    kcol = (3 * wsb + wsw) // LANES
    vcol = kcol + 1
    assert w_in.shape[-1] * N_DEV == (vcol + 1) * LANES

    t2 = lambda a: jnp.transpose(a[0])
    big_w = [t2(w_in), w_out[0], t2(w_gate_up), w_down[0]]
    big_m = [t2(m_w_in), m_w_out[0], t2(m_w_gate_up), m_w_down[0]]
    big_v = [t2(v_w_in), v_w_out[0], t2(v_w_gate_up), v_w_down[0]]

    cat_rows = lambda g: g.reshape(N_DEV * g.shape[1], g.shape[2])
    shards = [w.astype(BF16) for w in big_w]
    w_in_t = cat_rows(_exchange("w_in_allgather", shards[:1], ["gather"])[0])

    vec = lambda a: a.reshape(1, -1)
    g_in, b_in = vec(ln_in_g), vec(ln_in_b)
    bucket = jnp.asarray(_swa_bucket_table())

    h0b, proj = _ln_proj(x2, g_in, b_in, w_in_t)
    sb_out, gathered = _sb_fwd(proj, n_pairs, comm=(shards[1:3], ["gather"] * 2))
    w_out_f, w_gu_t = cat_rows(gathered[0]), cat_rows(gathered[1])
    sw_out, gathered = _swa_fwd(proj, bucket, rel_bias, sinks, n_sw_heads, qcol, kcol, vcol,
                                comm=(shards[3:], ["gather"]))
    w_down_f = cat_rows(gathered[0])
    du1, merged, h1b, act, dgu, du2b, dsb, dsw, st_ffn = _mix_ffn(
        sb_out, sw_out, x2, g_in, b_in, sb_norm_g, swa_norm_g, w_out_f, ln1_g, ln1_b, w_gu_t, w_down_f, ln2_g, ln2_b, tgt)

    split_rows = lambda g: g.reshape(N_DEV, g.shape[0] // N_DEV, g.shape[1])
    gw_gu = _wgrad("wgrad_gate_up", [dgu], h1b, dff // 2, d)
    gw_down = _wgrad("wgrad_down", [act], du2b, dff // 2, d)
    gw_out = _wgrad("wgrad_out", [merged], du1, min(512, d), d)
    (dq_sb, dk_sb, dv_sb), (land_gu, land_out) = _sb_bwd(
        proj, dsb, sb_out, n_pairs, comm=([split_rows(gw_gu), split_rows(gw_out)], ["scatter"] * 2))
    (dq_sw, dk_sw, dv_sw, st_sink, st_rb), (land_down,) = _swa_bwd(
        proj, dsw, bucket, rel_bias, sinks, n_sw_heads, qcol, kcol, vcol,
        comm=([split_rows(gw_down)], ["scatter"]))
    pieces = [dq_sb, dk_sb, dv_sb, dq_sw, dk_sw, dv_sw]
    gw_in = _wgrad("wgrad_in", pieces, h0b, proj.shape[1], d)
    (grad_x, st_in), (land_in,) = _proj_bwd(pieces, w_in_t, du1, x2, g_in, comm=([split_rows(gw_in)], ["scatter"]))

    n_rb = rel_bias.size
    small = _pack(d, st_in[0], st_in[1], st_ffn[3], st_ffn[4], st_ffn[0], st_ffn[1],
                  st_ffn[5, :wsb], st_ffn[5, wsb:], st_rb[:, :n_sw_heads], st_sink[:n_sw_heads, 0],
                  extra=st_ffn[2, 0:1])
    land_small = _exchange("small_grads_allgather", [small], ["gather"])[0]
    landed = [land_in, land_out, land_gu, land_down, land_small]

    big = []
    for name, land, w, m, v in zip(["adamw_in", "adamw_out", "adamw_gate_up", "adamw_down"], landed[:4], big_w, big_m, big_v):
        big.append(_adamw(name, land, w, m, v, _adamw_rows(w.shape[0])))

    small_w = _pack(d, ln_in_g, ln_in_b, ln1_g, ln1_b, ln2_g, ln2_b, sb_norm_g, swa_norm_g, rel_bias, sinks)
    small_m = _pack(d, m_ln_in_g, m_ln_in_b, m_ln1_g, m_ln1_b, m_ln2_g, m_ln2_b, m_sb_norm_g, m_swa_norm_g, m_rel_bias, m_sinks)
    small_v = _pack(d, v_ln_in_g, v_ln_in_b, v_ln1_g, v_ln1_b, v_ln2_g, v_ln2_b, v_sb_norm_g, v_swa_norm_g, v_rel_bias, v_sinks)
    sg, sd, sm, sv = _adamw("adamw_small", landed[4], small_w, small_m, small_v, 8)
    n_sk = sinks.size
    loss = sg[7, n_rb + n_sk]

    def leaves(idx):
        sm_l = _unpack([sg, sd, sm, sv][idx], wsb, n_rb, n_sk)
        bg = [jnp.transpose(big[0][idx])[None], big[1][idx][None], jnp.transpose(big[2][idx])[None], big[3][idx][None]]
        return [sm_l[0], sm_l[1], bg[0], sm_l[2], sm_l[3], sm_l[4], sm_l[5], bg[1], sm_l[6], sm_l[7], bg[2], bg[3], sm_l[8], sm_l[9]]

    return (loss, grad_x[None], *leaves(0), *leaves(1), *leaves(2), *leaves(3))
```

```python
import functools
import math

import numpy as np
import jax
import jax.numpy as jnp
from jax import lax
from jax.experimental import pallas as pl
from jax.experimental.pallas import tpu as pltpu

F32 = jnp.float32
BF16 = jnp.bfloat16
MESH = pl.DeviceIdType.MESH

N_DEV = 8
LANES = 128
HEAD_DIM = 64
SCALE = HEAD_DIM ** -0.5
SWA_BLOCK = 128
REL_BUCKETS = 32
REL_MAX_DIST = 128
ALPHA = 2.0 ** 0.25
LN_EPS = 1e-5
RMS_EPS = 1e-6
ADAM_LR = 0.001
ADAM_B1 = 0.9
ADAM_B2 = 0.999
ADAM_EPS = 1e-08
ADAM_WD = 0.01
ADAM_STEP = 10

ROW_TILE = 512
SB_TILE = 256
FFN_TILE = 256
WGRAD_TOKENS = 2048
SB_UNDERFLOW = -110.0
MIB = 1024 * 1024


def _params(vmem_mib=48):
    return pltpu.CompilerParams(vmem_limit_bytes=vmem_mib * MIB)


def _dot(a, b):
    return jnp.dot(a, b, preferred_element_type=F32)


def _dot_nt(a, b):
    return lax.dot_general(a, b, (((1,), (1,)), ((), ())), preferred_element_type=F32)


def _dot_tn(a, b):
    return lax.dot_general(a, b, (((0,), (0,)), ((), ())), preferred_element_type=F32)


def _ln_hat(x):
    mu = jnp.mean(x, axis=-1, keepdims=True)
    xc = x - mu
    var = jnp.mean(xc * xc, axis=-1, keepdims=True)
    r = lax.rsqrt(var + LN_EPS)
    return xc * r, r


def _ln_bwd(dxhat, xhat, r):
    return r * (dxhat - jnp.mean(dxhat, axis=-1, keepdims=True)
                - xhat * jnp.mean(dxhat * xhat, axis=-1, keepdims=True))


def _colsum(a):
    return jnp.sum(a, axis=0, keepdims=True)


def _rowsum(a):
    return jnp.sum(a, axis=1, keepdims=True)


def _full(shape):
    return pl.BlockSpec(shape, lambda *_: (0,) * len(shape))


def _comm_out_shapes(arrays, kinds):
    shapes = []
    for a, kind in zip(arrays, kinds):
        blk = a.shape if kind == "gather" else a.shape[1:]
        shapes.append(jax.ShapeDtypeStruct((N_DEV,) + tuple(blk), a.dtype))
    return shapes


def _comm_sems(n):
    return [pltpu.SemaphoreType.DMA((n, N_DEV - 1)), pltpu.SemaphoreType.DMA((n, N_DEV - 1)),
            pltpu.SemaphoreType.DMA((n,))]


def _comm_copies(ins, outs, kinds, send_sems, recv_sems, local_sems):
    x, y, c = lax.axis_index("x"), lax.axis_index("y"), lax.axis_index("c")
    me = 4 * x + 2 * y + c

    def src_for(t, dev_lin):
        return ins[t] if kinds[t] == "gather" else ins[t].at[dev_lin]

    local = [pltpu.make_async_copy(src_for(t, me), outs[t].at[me], local_sems.at[t]) for t in range(len(kinds))]
    sends, arrivals = [], []
    for k in range(1, N_DEV):
        px = 1 - x if (k >> 2) & 1 else x
        py = 1 - y if (k >> 1) & 1 else y
        pc = 1 - c if k & 1 else c
        peer_lin = 4 * px + 2 * py + pc
        for t in range(len(kinds)):
            sems = dict(send_sem=send_sems.at[t, k - 1], recv_sem=recv_sems.at[t, k - 1],
                        device_id=(px, py, pc), device_id_type=MESH)
            sends.append(pltpu.make_async_remote_copy(src_ref=src_for(t, peer_lin), dst_ref=outs[t].at[me], **sems))
            arrivals.append(pltpu.make_async_remote_copy(src_ref=src_for(t, peer_lin), dst_ref=outs[t].at[peer_lin], **sems))
    return local, sends, arrivals


def _comm_start(ins, outs, kinds, sems):
    local, sends, _ = _comm_copies(ins, outs, kinds, *sems)
    for cp in local + sends:
        cp.start()


def _comm_finish(ins, outs, kinds, sems):
    local, sends, arrivals = _comm_copies(ins, outs, kinds, *sems)
    for cp in arrivals:
        cp.wait_recv()
    for cp in sends:
        cp.wait_send()
    for cp in local:
        cp.wait()


def _exchange(name, arrays, kinds):
    n = len(arrays)

    def body(*refs):
        ins, outs, sems = refs[:n], refs[n:2 * n], refs[2 * n:]
        _comm_start(ins, outs, kinds, sems)
        _comm_finish(ins, outs, kinds, sems)

    any_spec = pl.BlockSpec(memory_space=pl.ANY)
    return pl.pallas_call(
        body, name=name, out_shape=_comm_out_shapes(arrays, kinds),
        in_specs=[any_spec] * n, out_specs=[any_spec] * n,
        scratch_shapes=_comm_sems(n),
    )(*arrays)


def _call(body, name, grid, in_specs, out_specs, out_shape, args, scratch_shapes=(), comm=None):
    if comm is None:
        outs = pl.pallas_call(body, name=name, grid=grid, in_specs=in_specs, out_specs=out_specs,
                              out_shape=out_shape, scratch_shapes=list(scratch_shapes),
                              compiler_params=_params())(*args)
        return outs, []
    arrays, kinds = comm
    n, n_in, n_out, n_scr = len(arrays), len(in_specs), len(out_specs), len(scratch_shapes)

    def fused(*refs):
        c_in, x_in = refs[:n_in], refs[n_in:n_in + n]
        c_out = refs[n_in + n:n_in + n + n_out]
        x_out = refs[n_in + n + n_out:n_in + 2 * n + n_out]
        rest = refs[n_in + 2 * n + n_out:]
        c_scr, sems = rest[:n_scr], rest[n_scr:]
        ids = [pl.program_id(a) for a in range(len(grid))]
        is_first = functools.reduce(jnp.logical_and, [i == 0 for i in ids])
        is_last = functools.reduce(jnp.logical_and, [i == g - 1 for i, g in zip(ids, grid)])

        @pl.when(is_first)
        def _():
            _comm_start(x_in, x_out, kinds, sems)

        body(*c_in, *c_out, *c_scr)

        @pl.when(is_last)
        def _():
            _comm_finish(x_in, x_out, kinds, sems)

    any_spec = pl.BlockSpec(memory_space=pl.ANY)
    outs = pl.pallas_call(
        fused, name=name, grid=grid,
        in_specs=list(in_specs) + [any_spec] * n, out_specs=list(out_specs) + [any_spec] * n,
        out_shape=list(out_shape) + _comm_out_shapes(arrays, kinds),
        scratch_shapes=list(scratch_shapes) + _comm_sems(n),
        compiler_params=_params())(*args, *arrays)
    return outs[:n_out], outs[n_out:]


def _ln_proj(x, g, b, w_in_t):
    s, d = x.shape
    cols = w_in_t.shape[0]
    tm = min(ROW_TILE, s)

    def body(x_ref, g_ref, b_ref, w_ref, h_ref, p_ref):
        xhat, _ = _ln_hat(x_ref[...])
        h = (xhat * g_ref[...] + b_ref[...]).astype(BF16)
        h_ref[...] = h
        p_ref[...] = _dot_nt(h, w_ref[...]).astype(BF16)

    row = lambda width: pl.BlockSpec((tm, width), lambda i: (i, 0))
    return pl.pallas_call(
        body, name="ln_proj", grid=(s // tm,),
        in_specs=[row(d), _full((1, d)), _full((1, d)), _full((cols, d))],
        out_specs=[row(d), row(cols)],
        out_shape=[jax.ShapeDtypeStruct((s, d), BF16), jax.ShapeDtypeStruct((s, cols), BF16)],
        compiler_params=_params(),
    )(x, g, b, w_in_t)


def _sb_triangles(t):
    row = lax.broadcasted_iota(jnp.int32, (t, t), 0)
    col = lax.broadcasted_iota(jnp.int32, (t, t), 1)
    return (row > col).astype(BF16), (row >= col).astype(BF16)


def _sb_first_mask(t, has_prev):
    qrow = lax.broadcasted_iota(jnp.int32, (2 * t, 2 * t), 0) & (t - 1)
    col = lax.broadcasted_iota(jnp.int32, (2 * t, 2 * t), 1)
    return ((col < t) & has_prev) | ((col >= t) & (col - t < qrow))


def _sb_stack_heads(x2, first):
    zero = jnp.zeros_like(x2)
    return jnp.concatenate([jnp.where(first, x2, zero), jnp.where(first, zero, x2)], axis=0)


def _sb_key_tiles(ref, offs, t):
    tiles = [ref[pl.ds(off, t), :] for off in offs]
    return tiles[0] if len(tiles) == 1 else jnp.concatenate(tiles, axis=0)


def _sb_suffix(terms, row_sums, tri, carry):
    out = [None] * len(terms)
    for j in reversed(range(len(terms))):
        suf = carry
        for op in terms[j]:
            suf = suf + _dot(op, tri)
        out[j] = suf
        carry = carry + row_sums[j]
    return (out[0] if len(out) == 1 else jnp.concatenate(out, axis=1)), carry


def _sb_scores(qh, k_t, upper, carry_l, mask, t):
    z = _dot_nt(qh, k_t)
    sp = jnp.log(1.0 + jnp.exp(-jnp.abs(z)))
    neg = jnp.minimum(z, 0.0)
    lb = neg - sp
    l1 = (neg - z) - sp
    if mask is not None:
        l1 = jnp.where(mask, l1, 0.0)
    hi = l1.astype(BF16)
    lo = (l1 - hi.astype(F32)).astype(BF16)
    cols = [slice(j * t, (j + 1) * t) for j in range(z.shape[1] // t)]
    suf, carry_l = _sb_suffix([[hi[:, c], lo[:, c]] for c in cols], [_rowsum(l1[:, c]) for c in cols], upper, carry_l)
    a = jnp.exp(lb + suf)
    if mask is not None:
        a = jnp.where(mask, a, 0.0)
    return lb, a, carry_l


def _sb_walk(i, t, first_visit, visit, init):
    def alive(carry):
        return jnp.max(carry[0]) > SB_UNDERFLOW

    prev = pl.multiple_of(jnp.maximum(i - 1, 0) * t, t)
    carry = first_visit((prev, pl.multiple_of(i * t, t)), init)

    def cond(state):
        j, go, _ = state
        return (j < i - 1) & go

    def body(state):
        j, _, carry = state
        carry = visit((pl.multiple_of((i - 2 - j) * t, t),), carry)
        return j + 1, alive(carry), carry

    return lax.while_loop(cond, body, (jnp.int32(0), alive(carry), carry))[2]


def _sb_first_specs(n_pairs, nq, t):
    at = lambda h, i: (h * nq + i, 0, 0)
    specs = [pl.BlockSpec((None, 2 * t, 2 * t), at), pl.BlockSpec((None, 2 * t, 2 * t), at), pl.BlockSpec((None, 2 * t, 1), at)]
    shapes = [jax.ShapeDtypeStruct((n_pairs * nq, 2 * t, 2 * t), BF16), jax.ShapeDtypeStruct((n_pairs * nq, 2 * t, 2 * t), F32),
              jax.ShapeDtypeStruct((n_pairs * nq, 2 * t, 1), F32)]
    return specs, shapes


def _sb_fwd(proj, n_pairs, comm=None):
    s = proj.shape[0]
    t = min(SB_TILE, s)
    nq = s // t

    def body(q_ref, k_ref, v_ref, o_ref, ab_ref, beta_ref, cl_ref):
        i = pl.program_id(1)
        lane = lax.broadcasted_iota(jnp.int32, (1, LANES), 1)
        first = lane < HEAD_DIM
        qs = _sb_stack_heads(q_ref[...] * SCALE, first)
        upper, _ = _sb_triangles(t)

        def first_visit(offs, carry):
            c_l, acc = carry
            mask = _sb_first_mask(t, i > 0)
            lb, a, c_l = _sb_scores(qs, _sb_key_tiles(k_ref, offs, t), upper, c_l, mask, t)
            a_b = a.astype(BF16)
            ab_ref[...] = a_b
            beta_ref[...] = jnp.where(mask, jnp.exp(lb), 0.0)
            cl_ref[...] = c_l
            return c_l, acc + _dot(a_b, _sb_key_tiles(v_ref, offs, t))

        def visit(offs, carry):
            c_l, acc = carry
            _, a, c_l = _sb_scores(qs, _sb_key_tiles(k_ref, offs, t), upper, c_l, None, t)
            return c_l, acc + _dot(a.astype(BF16), _sb_key_tiles(v_ref, offs, t))

        init = (jnp.zeros((2 * t, 1), F32), jnp.zeros((2 * t, LANES), F32))
        _, acc = _sb_walk(i, t, first_visit, visit, init)
        o_ref[...] = jnp.where(first, acc[:t], acc[t:])

    first_specs, first_shapes = _sb_first_specs(n_pairs, nq, t)
    outs, landed = _call(
        body, "sb_fwd", (n_pairs, nq),
        in_specs=[pl.BlockSpec((t, LANES), lambda h, i: (i, h)),
                  pl.BlockSpec((s, LANES), lambda h, i: (0, n_pairs + h)),
                  pl.BlockSpec((s, LANES), lambda h, i: (0, 2 * n_pairs + h))],
        out_specs=[pl.BlockSpec((t, LANES), lambda h, i: (i, h))] + first_specs,
        out_shape=[jax.ShapeDtypeStruct((s, n_pairs * LANES), F32)] + first_shapes,
        args=(proj, proj, proj), comm=comm)
    return outs[0], outs[1:], landed


def _swa_bucket_table():
    qi = np.arange(SWA_BLOCK)[:, None]
    cj = np.arange(2 * SWA_BLOCK)[None, :]
    dist = qi + SWA_BLOCK - cj
    exact = REL_BUCKETS // 2
    d = np.maximum(dist, 0)
    d_f = np.maximum(d, 1).astype(np.float32)
    large = exact + (np.log(d_f / np.float32(exact)) / np.float32(math.log(REL_MAX_DIST / exact))
                     * np.float32(REL_BUCKETS - exact)).astype(np.int32)
    large = np.minimum(large, REL_BUCKETS - 1)
    return np.where(d < exact, d, large).astype(np.int32)


def _swa_build_bias(bucket_ref, rb_ref, bias_ref, n_groups, per_group):
    bk = bucket_ref[...]
    dist = (lax.broadcasted_iota(jnp.int32, bk.shape, 0) + SWA_BLOCK) - lax.broadcasted_iota(jnp.int32, bk.shape, 1)
    window = (dist >= 0) & (dist < SWA_BLOCK)
    for g in range(n_groups):
        for hh in range(per_group):
            acc = jnp.zeros(bk.shape, F32)
            for b in range(REL_BUCKETS):
                acc = jnp.where(bk == b, rb_ref[b, g * per_group + hh], acc)
            bias_ref[g, hh * SWA_BLOCK:(hh + 1) * SWA_BLOCK, :] = jnp.where(window, acc, -jnp.inf)


def _swa_first_block_mask(i):
    col = lax.broadcasted_iota(jnp.int32, (1, 2 * SWA_BLOCK), 1)
    return jnp.where((col < SWA_BLOCK) & (i == 0), -jnp.inf, 0.0)


def _swa_place(blk, h, group, sel):
    if (h % 2) != group:
        blk = pltpu.roll(blk.astype(F32), HEAD_DIM, axis=1).astype(BF16)
    return jnp.where(sel, blk, jnp.zeros_like(blk))


def _swa_stack(ref, group, per_group, sel, scale=1.0):
    parts = []
    for hh in range(per_group):
        h = group * per_group + hh
        parts.append(_swa_place(ref[:, (h // 2) * LANES:(h // 2 + 1) * LANES], h, group, sel))
    stacked = jnp.concatenate(parts, axis=0)
    return stacked if scale == 1.0 else stacked * scale


def _swa_unstack(stacked, group, per_group, pieces):
    for hh in range(per_group):
        h = group * per_group + hh
        piece = stacked[hh * SWA_BLOCK:(hh + 1) * SWA_BLOCK, :]
        pieces[h] = pltpu.roll(piece, HEAD_DIM, axis=1) if (h % 2) != group else piece


def _swa_sink_rows(sk_ref, group, per_group):
    rowh = lax.broadcasted_iota(jnp.int32, (per_group * SWA_BLOCK, 1), 0) // SWA_BLOCK
    sink = jnp.zeros((per_group * SWA_BLOCK, 1), F32) + sk_ref[0, group * per_group]
    for hh in range(1, per_group):
        sink = jnp.where(rowh == hh, sk_ref[0, group * per_group + hh], sink)
    return sink


def _swa_probs(q_pos, kcat, bias_h, first_mask, sink):
    logits = _dot_nt(q_pos, kcat) + (bias_h + first_mask)
    m = jnp.maximum(jnp.max(logits, axis=1, keepdims=True), sink)
    p = jnp.exp(logits - m)
    es = jnp.exp(sink - m)
    inv = 1.0 / (_rowsum(p) + es)
    return p * inv, es * inv


def _swa_specs(n_heads, qcol, kcol, vcol):
    width = n_heads * HEAD_DIM
    prev = lambda col: pl.BlockSpec((SWA_BLOCK, LANES), lambda i: (jnp.maximum(i - 1, 0), col))
    cur = lambda col: pl.BlockSpec((SWA_BLOCK, LANES), lambda i: (i, col))
    return [pl.BlockSpec((SWA_BLOCK, width), lambda i: (i, qcol)),
            prev(kcol), cur(kcol), prev(vcol), cur(vcol),
            _full((SWA_BLOCK, 2 * SWA_BLOCK)),
            pl.BlockSpec(memory_space=pltpu.SMEM), pl.BlockSpec(memory_space=pltpu.SMEM)]


def _swa_fwd(proj, bucket, rel_bias, sinks, n_heads, qcol, kcol, vcol, comm=None):
    s = proj.shape[0]
    width = n_heads * HEAD_DIM
    n_groups = LANES // HEAD_DIM
    per_group = n_heads // n_groups

    def body(q_ref, kp_ref, kc_ref, vp_ref, vc_ref, bucket_ref, rb_ref, sk_ref, o_ref, bias_ref):
        i = pl.program_id(0)

        @pl.when(i == 0)
        def _():
            _swa_build_bias(bucket_ref, rb_ref, bias_ref, n_groups, per_group)

        lane = lax.broadcasted_iota(jnp.int32, (1, LANES), 1)
        first = lane < HEAD_DIM
        first_mask = _swa_first_block_mask(i)
        kcat = jnp.concatenate([kp_ref[...], kc_ref[...]], axis=0)
        vcat = jnp.concatenate([vp_ref[...], vc_ref[...]], axis=0)
        pieces = {}
        for g in range(n_groups):
            sel = first if g == 0 else jnp.logical_not(first)
            prob, _ = _swa_probs(_swa_stack(q_ref, g, per_group, sel, SCALE), kcat, bias_ref[g], first_mask,
                                 _swa_sink_rows(sk_ref, g, per_group))
            _swa_unstack(_dot(prob.astype(BF16), vcat), g, per_group, pieces)
        for j in range(n_heads // 2):
            o_ref[:, j * LANES:(j + 1) * LANES] = jnp.where(first, pieces[2 * j], pieces[2 * j + 1])

    outs, landed = _call(
        body, "swa_fwd", (s // SWA_BLOCK,),
        in_specs=_swa_specs(n_heads, qcol, kcol, vcol),
        out_specs=[pl.BlockSpec((SWA_BLOCK, width), lambda i: (i, 0))],
        out_shape=[jax.ShapeDtypeStruct((s, width), F32)],
        args=(proj, proj, proj, proj, proj, bucket, rel_bias, sinks),
        scratch_shapes=[pltpu.VMEM((n_groups, per_group * SWA_BLOCK, 2 * SWA_BLOCK), F32)], comm=comm)
    return outs[0], landed


def _rms_fwd(o, g):
    r = lax.rsqrt(jnp.mean(o * o, axis=-1, keepdims=True) + RMS_EPS)
    n = o * r
    return n, r, n * g


def _mix_ffn(sb_out, sw_out, x, g_in, b_in, sb_g, sw_g, w_out, g1, b1, w_gu_t, w_down, g2, b2, target):
    s, d = x.shape
    wsb, wsw = sb_out.shape[1], sw_out.shape[1]
    dff = w_down.shape[0]
    assert wsb + wsw == d
    tm = min(FFN_TILE, s)

    def body(sb_ref, sw_ref, x_ref, gi_ref, bi_ref, sbg_ref, swg_ref, wo_hbm, g1_ref, b1_ref, wgu_hbm, wd_hbm,
             g2_ref, b2_ref, t_ref,
             du1_ref, mg_ref, h1b_ref, act_ref, dgu_ref, du2b_ref, dsb_ref, dsw_ref, st_ref,
             wo_ref, wgu_ref, wd_ref):
        @pl.when(pl.program_id(0) == 0)
        def _():
            pltpu.sync_copy(wo_hbm, wo_ref)
            pltpu.sync_copy(wgu_hbm, wgu_ref)
            pltpu.sync_copy(wd_hbm, wd_ref)
            st_ref[...] = jnp.zeros_like(st_ref)

        sb, sw = sb_ref[...], sw_ref[...]
        _, _, m_sb = _rms_fwd(sb, sbg_ref[...])
        _, _, m_sw = _rms_fwd(sw, swg_ref[...])
        m_sb = m_sb.astype(BF16)
        m_sw = m_sw.astype(BF16)
        mg_ref[:, :wsb] = m_sb
        mg_ref[:, wsb:] = m_sw
        xhat0, _ = _ln_hat(x_ref[...])
        u1 = ALPHA * (xhat0 * gi_ref[...] + bi_ref[...]) + _dot(m_sb, wo_ref[:wsb, :]) + _dot(m_sw, wo_ref[wsb:, :])

        xhat1, r1 = _ln_hat(u1)
        h1 = xhat1 * g1_ref[...] + b1_ref[...]
        h1b = h1.astype(BF16)
        h1b_ref[...] = h1b
        gate = _dot_nt(h1b, wgu_ref[:dff, :])
        up = _dot_nt(h1b, wgu_ref[dff:, :])
        sg = jax.nn.sigmoid(gate)
        silu = gate * sg
        act = (silu * up).astype(BF16)
        act_ref[...] = act
        u2 = ALPHA * h1 + _dot(act, wd_ref[...])
        xhat2, r2 = _ln_hat(u2)
        diff = xhat2 * g2_ref[...] + b2_ref[...] - t_ref[...]
        dh2 = diff * (1.0 / d)
        st_ref[0:1, :] += _colsum(dh2 * xhat2)
        st_ref[1:2, :] += _colsum(dh2)
        st_ref[2:3, :] += jnp.broadcast_to(_colsum(_rowsum(diff * diff)) * (0.5 / d), (1, d))
        du2 = _ln_bwd(dh2 * g2_ref[...], xhat2, r2)
        du2b = du2.astype(BF16)
        du2b_ref[...] = du2b
        dact = _dot_nt(du2b, wd_ref[...])
        dgate = (dact * up * (sg * (1.0 + gate * (1.0 - sg)))).astype(BF16)
        dup = (dact * silu).astype(BF16)
        dgu_ref[:, :dff] = dgate
        dgu_ref[:, dff:] = dup
        dh1 = _dot(dgate, wgu_ref[:dff, :]) + _dot(dup, wgu_ref[dff:, :]) + ALPHA * du2
        st_ref[3:4, :] += _colsum(dh1 * xhat1)
        st_ref[4:5, :] += _colsum(dh1)
        du1 = _ln_bwd(dh1 * g1_ref[...], xhat1, r1)
        du1_ref[...] = du1

        dmerged = _dot_nt(du1.astype(BF16), wo_ref[...])
        dsb, gsb = _rms_bwd(dmerged[:, :wsb], sb, sbg_ref[...])
        dsw, gsw = _rms_bwd(dmerged[:, wsb:], sw, swg_ref[...])
        dsb_ref[...] = dsb.astype(BF16)
        dsw_ref[...] = dsw.astype(BF16)
        st_ref[5:6, :wsb] += gsb
        st_ref[5:6, wsb:] += gsw

    row = lambda width: pl.BlockSpec((tm, width), lambda i: (i, 0))
    vec = lambda width: _full((1, width))
    hbm = pl.BlockSpec(memory_space=pl.ANY)
    bf = lambda width: jax.ShapeDtypeStruct((s, width), BF16)
    return pl.pallas_call(
        body, name="mix_ffn", grid=(s // tm,),
        in_specs=[row(wsb), row(wsw), row(d), vec(d), vec(d), vec(wsb), vec(wsw), hbm, vec(d), vec(d), hbm, hbm,
                  vec(d), vec(d), row(d)],
        out_specs=[row(d), row(d), row(d), row(dff), row(2 * dff), row(d), row(wsb), row(wsw), _full((8, d))],
        out_shape=[jax.ShapeDtypeStruct((s, d), F32), bf(d), bf(d), bf(dff), bf(2 * dff), bf(d), bf(wsb), bf(wsw),
                   jax.ShapeDtypeStruct((8, d), F32)],
        scratch_shapes=[pltpu.VMEM(w_out.shape, BF16), pltpu.VMEM(w_gu_t.shape, BF16), pltpu.VMEM(w_down.shape, BF16)],
        compiler_params=_params(60),
    )(sb_out, sw_out, x, g_in, b_in, sb_g, sw_g, w_out, g1, b1, w_gu_t, w_down, g2, b2, target)


def _rms_bwd(dm, o, g):
    n, r, _ = _rms_fwd(o, g)
    dn = dm * g
    return r * (dn - n * jnp.mean(dn * n, axis=-1, keepdims=True)), _colsum(dm * n)


def _sb_bwd(proj, dout, out, first, n_pairs, comm=None):
    s = proj.shape[0]
    t = min(SB_TILE, s)
    nq = s // t
    width = n_pairs * LANES

    def body(q_ref, k_ref, v_ref, do_ref, o_ref, ab_ref, beta_ref, cl_ref, dq_ref, dk_out, dv_out, dk_ref, dv_ref):
        i = pl.program_id(1)

        @pl.when(i == 0)
        def _():
            dk_ref[...] = jnp.zeros_like(dk_ref)
            dv_ref[...] = jnp.zeros_like(dv_ref)

        lane = lax.broadcasted_iota(jnp.int32, (1, LANES), 1)
        first_lanes = lane < HEAD_DIM
        do2 = do_ref[...]
        qs = _sb_stack_heads(q_ref[...] * SCALE, first_lanes)
        dos = _sb_stack_heads(do2, first_lanes)
        prod = do2.astype(F32) * o_ref[...]
        totals = jnp.concatenate([_rowsum(jnp.where(first_lanes, prod, 0.0)), _rowsum(jnp.where(first_lanes, 0.0, prod))], axis=0)
        upper, incl = _sb_triangles(t)

        def grads(offs, k_t, v_t, a_b, beta, c_e, dq):
            d_e = _dot_nt(dos, v_t) * a_b.astype(F32)
            d_eb = d_e.astype(BF16)
            cols = [slice(j * t, (j + 1) * t) for j in range(len(offs))]
            suf_e, c_e = _sb_suffix([[d_eb[:, c]] for c in cols], [_rowsum(d_e[:, c]) for c in cols], incl, c_e)
            dzb = (d_e - beta * (d_e + (totals - suf_e))).astype(BF16)
            dk_t = _dot_tn(dzb, qs)
            dv_t = _dot_tn(a_b, dos)
            for off, c in zip(offs, cols):
                dk_ref[pl.ds(off, t), :] += dk_t[c, :]
                dv_ref[pl.ds(off, t), :] += dv_t[c, :]
            return c_e, dq + _dot(dzb, k_t)

        def first_visit(offs, carry):
            _, c_e, dq = carry
            k_t = _sb_key_tiles(k_ref, offs, t)
            v_t = _sb_key_tiles(v_ref, offs, t)
            c_e, dq = grads(offs, k_t, v_t, ab_ref[...], beta_ref[...], c_e, dq)
            return cl_ref[...], c_e, dq

        def visit(offs, carry):
            c_l, c_e, dq = carry
            k_t = _sb_key_tiles(k_ref, offs, t)
            v_t = _sb_key_tiles(v_ref, offs, t)
            lb, a, c_l = _sb_scores(qs, k_t, upper, c_l, None, t)
            c_e, dq = grads(offs, k_t, v_t, a.astype(BF16), jnp.exp(lb), c_e, dq)
            return c_l, c_e, dq

        init = (jnp.zeros((2 * t, 1), F32), jnp.zeros((2 * t, 1), F32), jnp.zeros((2 * t, LANES), F32))
        _, _, dq = _sb_walk(i, t, first_visit, visit, init)
        dq_ref[...] = (jnp.where(first_lanes, dq[:t], dq[t:]) * SCALE).astype(BF16)

        @pl.when(i == nq - 1)
        def _():
            dk_out[...] = dk_ref[...].astype(BF16)
            dv_out[...] = dv_ref[...].astype(BF16)

    qblk = pl.BlockSpec((t, LANES), lambda h, i: (i, h))
    whole = pl.BlockSpec((s, LANES), lambda h, i: (0, h))
    first_specs, _ = _sb_first_specs(n_pairs, nq, t)
    return _call(
        body, "sb_bwd", (n_pairs, nq),
        in_specs=[qblk,
                  pl.BlockSpec((s, LANES), lambda h, i: (0, n_pairs + h)),
                  pl.BlockSpec((s, LANES), lambda h, i: (0, 2 * n_pairs + h)),
                  qblk, qblk] + first_specs,
        out_specs=[qblk, whole, whole],
        out_shape=[jax.ShapeDtypeStruct((s, width), BF16)] * 3,
        args=(proj, proj, proj, dout, out, *first),
        scratch_shapes=[pltpu.VMEM((s, LANES), F32), pltpu.VMEM((s, LANES), F32)], comm=comm)


def _swa_bwd(proj, dout, bucket, rel_bias, sinks, n_heads, qcol, kcol, vcol, comm=None):
    s = proj.shape[0]
    width = n_heads * HEAD_DIM
    n_groups = LANES // HEAD_DIM
    per_group = n_heads // n_groups
    nb = s // SWA_BLOCK

    def body(q_ref, kp_ref, kc_ref, vp_ref, vc_ref, bucket_ref, rb_ref, sk_ref, do_ref,
             dq_ref, dk_out, dv_out, dsk_ref, drb_ref, bias_ref, dbias_ref, dk_ref, dv_ref):
        i = pl.program_id(0)

        @pl.when(i == 0)
        def _():
            _swa_build_bias(bucket_ref, rb_ref, bias_ref, n_groups, per_group)
            dbias_ref[...] = jnp.zeros_like(dbias_ref)
            dk_ref[...] = jnp.zeros_like(dk_ref)
            dv_ref[...] = jnp.zeros_like(dv_ref)
            dsk_ref[...] = jnp.zeros_like(dsk_ref)

        lane = lax.broadcasted_iota(jnp.int32, (1, LANES), 1)
        first = lane < HEAD_DIM
        first_mask = _swa_first_block_mask(i)
        kcat = jnp.concatenate([kp_ref[...], kc_ref[...]], axis=0)
        vcat = jnp.concatenate([vp_ref[...], vc_ref[...]], axis=0)
        dkcat = jnp.zeros((2 * SWA_BLOCK, LANES), F32)
        dvcat = jnp.zeros((2 * SWA_BLOCK, LANES), F32)
        pieces = {}
        for g in range(n_groups):
            sel = first if g == 0 else jnp.logical_not(first)
            q_g = _swa_stack(q_ref, g, per_group, sel, SCALE)
            do_g = _swa_stack(do_ref, g, per_group, sel)
            prob, p_sink = _swa_probs(q_g, kcat, bias_ref[g], first_mask, _swa_sink_rows(sk_ref, g, per_group))
            dprob = _dot_nt(do_g, vcat)
            delta = _rowsum(prob * dprob)
            dlog = prob * (dprob - delta)
            sink_term = p_sink * delta
            for hh in range(per_group):
                h = g * per_group + hh
                tot = _colsum(sink_term[hh * SWA_BLOCK:(hh + 1) * SWA_BLOCK, :])
                dsk_ref[h:h + 1, :] += jnp.broadcast_to(-tot, (1, LANES))
            dbias_ref[g] += dlog
            dlb = dlog.astype(BF16)
            _swa_unstack(_dot(dlb, kcat) * SCALE, g, per_group, pieces)
            dkcat += _dot_tn(dlb, q_g)
            dvcat += _dot_tn(prob.astype(BF16), do_g)
        for j in range(n_heads // 2):
            dq_ref[:, j * LANES:(j + 1) * LANES] = jnp.where(first, pieces[2 * j], pieces[2 * j + 1]).astype(BF16)

        cur = pl.multiple_of(i * SWA_BLOCK, SWA_BLOCK)
        dk_ref[pl.ds(cur, SWA_BLOCK), :] += dkcat[SWA_BLOCK:, :]
        dv_ref[pl.ds(cur, SWA_BLOCK), :] += dvcat[SWA_BLOCK:, :]

        @pl.when(i > 0)
        def _():
            prv = pl.multiple_of((i - 1) * SWA_BLOCK, SWA_BLOCK)
            dk_ref[pl.ds(prv, SWA_BLOCK), :] += dkcat[:SWA_BLOCK, :]
            dv_ref[pl.ds(prv, SWA_BLOCK), :] += dvcat[:SWA_BLOCK, :]

        @pl.when(i == nb - 1)
        def _():
            bk = bucket_ref[...]
            rowi = lax.broadcasted_iota(jnp.int32, (REL_BUCKETS, LANES), 0)
            coli = lax.broadcasted_iota(jnp.int32, (REL_BUCKETS, LANES), 1)
            res = jnp.zeros((REL_BUCKETS, LANES), F32)
            for h in range(n_heads):
                g, hh = divmod(h, per_group)
                db = dbias_ref[g, hh * SWA_BLOCK:(hh + 1) * SWA_BLOCK, :]
                for b in range(REL_BUCKETS):
                    tot = _colsum(_rowsum(jnp.where(bk == b, db, 0.0)))
                    res = jnp.where((rowi == b) & (coli == h), tot, res)
            drb_ref[...] = res
            dk_out[...] = dk_ref[...].astype(BF16)
            dv_out[...] = dv_ref[...].astype(BF16)

    in_specs = _swa_specs(n_heads, qcol, kcol, vcol) + [pl.BlockSpec((SWA_BLOCK, width), lambda i: (i, 0))]
    return _call(
        body, "swa_bwd", (nb,),
        in_specs=in_specs,
        out_specs=[pl.BlockSpec((SWA_BLOCK, width), lambda i: (i, 0)),
                   _full((s, LANES)), _full((s, LANES)), _full((8, LANES)), _full((REL_BUCKETS, LANES))],
        out_shape=[jax.ShapeDtypeStruct((s, width), BF16), jax.ShapeDtypeStruct((s, LANES), BF16),
                   jax.ShapeDtypeStruct((s, LANES), BF16), jax.ShapeDtypeStruct((8, LANES), F32),
                   jax.ShapeDtypeStruct((REL_BUCKETS, LANES), F32)],
        args=(proj, proj, proj, proj, proj, bucket, rel_bias, sinks, dout),
        scratch_shapes=[pltpu.VMEM((n_groups, per_group * SWA_BLOCK, 2 * SWA_BLOCK), F32),
                        pltpu.VMEM((n_groups, per_group * SWA_BLOCK, 2 * SWA_BLOCK), F32),
                        pltpu.VMEM((s, LANES), F32), pltpu.VMEM((s, LANES), F32)],
        comm=comm)


def _proj_bwd(pieces, w_in_t, du1, x, g_in, comm=None):
    s, d = x.shape
    cols = w_in_t.shape[0]
    tm = min(ROW_TILE, s)
    n_p = len(pieces)

    def body(*refs):
        p_refs = refs[:n_p]
        w_ref, du_ref, x_ref, g_ref, dx_ref, st_ref = refs[n_p:]
        i = pl.program_id(0)

        @pl.when(i == 0)
        def _():
            st_ref[...] = jnp.zeros_like(st_ref)

        dproj = jnp.concatenate([p[...] for p in p_refs], axis=1)
        dh0 = _dot(dproj, w_ref[...]) + ALPHA * du_ref[...]
        xhat, r = _ln_hat(x_ref[...])
        st_ref[0:1, :] += _colsum(dh0 * xhat)
        st_ref[1:2, :] += _colsum(dh0)
        dx_ref[...] = _ln_bwd(dh0 * g_ref[...], xhat, r)

    row = lambda width: pl.BlockSpec((tm, width), lambda i: (i, 0))
    return _call(
        body, "proj_bwd", (s // tm,),
        in_specs=[row(p.shape[1]) for p in pieces] + [_full((cols, d)), row(d), row(d), _full((1, d))],
        out_specs=[row(d), _full((8, d))],
        out_shape=[jax.ShapeDtypeStruct((s, d), F32), jax.ShapeDtypeStruct((8, d), F32)],
        args=(*pieces, w_in_t, du1, x, g_in), comm=comm)


def _wgrad(name, pieces, b, tm, tn):
    s, n = b.shape
    m = sum(p.shape[1] for p in pieces)
    n_p = len(pieces)
    assert n_p == 1 or tm == m
    ts = min(WGRAD_TOKENS if b.dtype == BF16 and n_p == 1 else WGRAD_TOKENS // 2, s)
    n_k = s // ts

    def body(*refs):
        p_refs, b_ref, o_ref, acc_ref = refs[:n_p], refs[n_p], refs[n_p + 1], refs[n_p + 2]
        k = pl.program_id(2)

        @pl.when(k == 0)
        def _():
            acc_ref[...] = jnp.zeros_like(acc_ref)

        a = p_refs[0][...] if n_p == 1 else jnp.concatenate([p[...] for p in p_refs], axis=1)
        acc_ref[...] += _dot_tn(a, b_ref[...].astype(BF16))

        @pl.when(k == n_k - 1)
        def _():
            o_ref[...] = acc_ref[...].astype(BF16)

    piece_spec = lambda p: pl.BlockSpec((ts, tm if n_p == 1 else p.shape[1]), lambda i, j, k: (k, i))
    return pl.pallas_call(
        body, name=name, grid=(m // tm, n // tn, n_k),
        in_specs=[piece_spec(p) for p in pieces] + [pl.BlockSpec((ts, tn), lambda i, j, k: (k, j))],
        out_specs=pl.BlockSpec((tm, tn), lambda i, j, k: (i, j)),
        out_shape=jax.ShapeDtypeStruct((m, n), BF16),
        scratch_shapes=[pltpu.VMEM((tm, tn), F32)],
        compiler_params=_params(),
    )(*pieces, b)


def _adamw_math(w, g, m, v):
    m = ADAM_B1 * m + (1.0 - ADAM_B1) * g
    v = ADAM_B2 * v + (1.0 - ADAM_B2) * (g * g)
    m_hat = m / (1.0 - ADAM_B1 ** ADAM_STEP)
    v_hat = v / (1.0 - ADAM_B2 ** ADAM_STEP)
    delta = -ADAM_LR * (m_hat / (jnp.sqrt(v_hat) + ADAM_EPS) + ADAM_WD * w)
    return delta, m, v


def _adamw_rows(rows):
    return max(r for r in range(16, 257, 16) if rows % r == 0)


def _adamw(name, landed, w, m, v, tr):
    rows, cols = w.shape

    def body(l_ref, w_ref, m_ref, v_ref, g_ref, d_ref, nm_ref, nv_ref):
        g = l_ref[0].astype(F32)
        for src in range(1, N_DEV):
            g = g + l_ref[src].astype(F32)
        delta, nm, nv = _adamw_math(w_ref[...], g, m_ref[...], v_ref[...])
        g_ref[...] = g
        d_ref[...] = delta
        nm_ref[...] = nm
        nv_ref[...] = nv

    blk = pl.BlockSpec((tr, cols), lambda i: (i, 0))
    shape = jax.ShapeDtypeStruct((rows, cols), F32)
    return pl.pallas_call(
        body, name=name, grid=(rows // tr,),
        in_specs=[pl.BlockSpec((N_DEV, tr, cols), lambda i: (0, i, 0)), blk, blk, blk],
        out_specs=[blk, blk, blk, blk],
        out_shape=[shape, shape, shape, shape],
        compiler_params=_params(),
    )(landed, w, m, v)


def _pack(d, ln_in_g, ln_in_b, ln1_g, ln1_b, ln2_g, ln2_b, sb_g, sw_g, rel_bias, sinks, extra=None):
    tail = [rel_bias.reshape(-1), sinks.reshape(-1)]
    if extra is not None:
        tail.append(extra.reshape(-1))
    tail = jnp.concatenate(tail)
    tail = jnp.concatenate([tail, jnp.zeros((d - tail.shape[0],), F32)])
    rows = [ln_in_g.reshape(-1), ln_in_b.reshape(-1), ln1_g.reshape(-1), ln1_b.reshape(-1),
            ln2_g.reshape(-1), ln2_b.reshape(-1),
            jnp.concatenate([sb_g.reshape(-1), sw_g.reshape(-1)]), tail]
    return jnp.stack(rows)


def _unpack(p, wsb, n_rb, n_sk):
    return [p[0], p[1], p[6, :wsb][None], p[6, wsb:][None], p[7, n_rb:n_rb + n_sk][None],
            p[7, :n_rb].reshape(REL_BUCKETS, -1), p[2][None], p[3][None], p[4][None], p[5][None]]


def kernel(x, ln_in_g, ln_in_b, w_in, sb_norm_g, swa_norm_g, sinks, rel_bias, w_out, ln1_g, ln1_b, w_gate_up, w_down, ln2_g, ln2_b, loss_target, m_ln_in_g, m_ln_in_b, m_w_in, m_sb_norm_g, m_swa_norm_g, m_sinks, m_rel_bias, m_w_out, m_ln1_g, m_ln1_b, m_w_gate_up, m_w_down, m_ln2_g, m_ln2_b, v_ln_in_g, v_ln_in_b, v_w_in, v_sb_norm_g, v_swa_norm_g, v_sinks, v_rel_bias, v_w_out, v_ln1_g, v_ln1_b, v_w_gate_up, v_w_down, v_ln2_g, v_ln2_b):
    x2 = x[0]
    tgt = loss_target[0]
    s, d = x2.shape
    wsb = sb_norm_g.shape[-1]
    wsw = swa_norm_g.shape[-1]
    n_sw_heads = sinks.shape[-1]
    n_pairs = wsb // LANES
    dff = w_down.shape[1] * N_DEV
    assert wsb % LANES == 0 and wsw % LANES == 0 and n_sw_heads * HEAD_DIM == wsw
    assert 3 * wsb % wsw == 0 and dff % LANES == 0 and s % SWA_BLOCK == 0
    qcol = 3 * wsb // wsw
    kcol = (3 * wsb + wsw) // LANES
    vcol = kcol + 1
    assert w_in.shape[-1] * N_DEV == (vcol + 1) * LANES

    t2 = lambda a: jnp.transpose(a[0])
    big_w = [t2(w_in), w_out[0], t2(w_gate_up), w_down[0]]
    big_m = [t2(m_w_in), m_w_out[0], t2(m_w_gate_up), m_w_down[0]]
    big_v = [t2(v_w_in), v_w_out[0], t2(v_w_gate_up), v_w_down[0]]

    cat_rows = lambda g: g.reshape(N_DEV * g.shape[1], g.shape[2])
    shards = [w.astype(BF16) for w in big_w]
    w_in_t = cat_rows(_exchange("w_in_allgather", shards[:1], ["gather"])[0])

    vec = lambda a: a.reshape(1, -1)
    g_in, b_in = vec(ln_in_g), vec(ln_in_b)
    bucket = jnp.asarray(_swa_bucket_table())

    h0b, proj = _ln_proj(x2, g_in, b_in, w_in_t)
    sb_out, sb_first, gathered = _sb_fwd(proj, n_pairs, comm=(shards[1:3], ["gather"] * 2))
    w_out_f, w_gu_t = cat_rows(gathered[0]), cat_rows(gathered[1])
    sw_out, gathered = _swa_fwd(proj, bucket, rel_bias, sinks, n_sw_heads, qcol, kcol, vcol,
                                comm=(shards[3:], ["gather"]))
    w_down_f = cat_rows(gathered[0])
    du1, merged, h1b, act, dgu, du2b, dsb, dsw, st_ffn = _mix_ffn(
        sb_out, sw_out, x2, g_in, b_in, sb_norm_g, swa_norm_g, w_out_f, ln1_g, ln1_b, w_gu_t, w_down_f, ln2_g, ln2_b, tgt)

    split_rows = lambda g: g.reshape(N_DEV, g.shape[0] // N_DEV, g.shape[1])
    gw_gu = _wgrad("wgrad_gate_up", [dgu], h1b, dff // 2, d)
    gw_down = _wgrad("wgrad_down", [act], du2b, dff // 2, d)
    gw_out = _wgrad("wgrad_out", [merged], du1, min(512, d), d)
    (dq_sb, dk_sb, dv_sb), (land_gu, land_out) = _sb_bwd(
        proj, dsb, sb_out, sb_first, n_pairs, comm=([split_rows(gw_gu), split_rows(gw_out)], ["scatter"] * 2))
    (dq_sw, dk_sw, dv_sw, st_sink, st_rb), (land_down,) = _swa_bwd(
        proj, dsw, bucket, rel_bias, sinks, n_sw_heads, qcol, kcol, vcol,
        comm=([split_rows(gw_down)], ["scatter"]))
    pieces = [dq_sb, dk_sb, dv_sb, dq_sw, dk_sw, dv_sw]
    gw_in = _wgrad("wgrad_in", pieces, h0b, proj.shape[1], d)
    (grad_x, st_in), (land_in,) = _proj_bwd(pieces, w_in_t, du1, x2, g_in, comm=([split_rows(gw_in)], ["scatter"]))

    n_rb = rel_bias.size
    small = _pack(d, st_in[0], st_in[1], st_ffn[3], st_ffn[4], st_ffn[0], st_ffn[1],
                  st_ffn[5, :wsb], st_ffn[5, wsb:], st_rb[:, :n_sw_heads], st_sink[:n_sw_heads, 0],
                  extra=st_ffn[2, 0:1])
    land_small = _exchange("small_grads_allgather", [small], ["gather"])[0]
    landed = [land_in, land_out, land_gu, land_down, land_small]

    big = []
    for name, land, w, m, v in zip(["adamw_in", "adamw_out", "adamw_gate_up", "adamw_down"], landed[:4], big_w, big_m, big_v):
        big.append(_adamw(name, land, w, m, v, _adamw_rows(w.shape[0])))

    small_w = _pack(d, ln_in_g, ln_in_b, ln1_g, ln1_b, ln2_g, ln2_b, sb_norm_g, swa_norm_g, rel_bias, sinks)
    small_m = _pack(d, m_ln_in_g, m_ln_in_b, m_ln1_g, m_ln1_b, m_ln2_g, m_ln2_b, m_sb_norm_g, m_swa_norm_g, m_rel_bias, m_sinks)
    small_v = _pack(d, v_ln_in_g, v_ln_in_b, v_ln1_g, v_ln1_b, v_ln2_g, v_ln2_b, v_sb_norm_g, v_swa_norm_g, v_rel_bias, v_sinks)
    sg, sd, sm, sv = _adamw("adamw_small", landed[4], small_w, small_m, small_v, 8)
    n_sk = sinks.size
    loss = sg[7, n_rb + n_sk]

    def leaves(idx):
        sm_l = _unpack([sg, sd, sm, sv][idx], wsb, n_rb, n_sk)
        bg = [jnp.transpose(big[0][idx])[None], big[1][idx][None], jnp.transpose(big[2][idx])[None], big[3][idx][None]]
        return [sm_l[0], sm_l[1], bg[0], sm_l[2], sm_l[3], sm_l[4], sm_l[5], bg[1], sm_l[6], sm_l[7], bg[2], bg[3], sm_l[8], sm_l[9]]

    return (loss, grad_x[None], *leaves(0), *leaves(1), *leaves(2), *leaves(3))
```

```python
import functools
import math

import numpy as np
import jax
import jax.numpy as jnp
from jax import lax
from jax.experimental import pallas as pl
from jax.experimental.pallas import tpu as pltpu

F32 = jnp.float32
BF16 = jnp.bfloat16
MESH = pl.DeviceIdType.MESH

N_DEV = 8
LANES = 128
HEAD_DIM = 64
SCALE = HEAD_DIM ** -0.5
SWA_BLOCK = 128
REL_BUCKETS = 32
REL_MAX_DIST = 128
ALPHA = 2.0 ** 0.25
LN_EPS = 1e-5
RMS_EPS = 1e-6
ADAM_LR = 0.001
ADAM_B1 = 0.9
ADAM_B2 = 0.999
ADAM_EPS = 1e-08
ADAM_WD = 0.01
ADAM_STEP = 10

ROW_TILE = 512
SB_TILE = 256
FFN_TILE = 256
WGRAD_TOKENS = 2048
SB_UNDERFLOW = -110.0
MIB = 1024 * 1024


def _params(vmem_mib=48):
    return pltpu.CompilerParams(vmem_limit_bytes=vmem_mib * MIB)


def _dot(a, b):
    return jnp.dot(a, b, preferred_element_type=F32)


def _dot_nt(a, b):
    return lax.dot_general(a, b, (((1,), (1,)), ((), ())), preferred_element_type=F32)


def _dot_tn(a, b):
    return lax.dot_general(a, b, (((0,), (0,)), ((), ())), preferred_element_type=F32)


def _ln_hat(x):
    mu = jnp.mean(x, axis=-1, keepdims=True)
    xc = x - mu
    var = jnp.mean(xc * xc, axis=-1, keepdims=True)
    r = lax.rsqrt(var + LN_EPS)
    return xc * r, r


def _ln_bwd(dxhat, xhat, r):
    return r * (dxhat - jnp.mean(dxhat, axis=-1, keepdims=True)
                - xhat * jnp.mean(dxhat * xhat, axis=-1, keepdims=True))


def _colsum(a):
    return jnp.sum(a, axis=0, keepdims=True)


def _rowsum(a):
    return jnp.sum(a, axis=1, keepdims=True)


def _full(shape):
    return pl.BlockSpec(shape, lambda *_: (0,) * len(shape))


def _comm_out_shapes(arrays, kinds):
    shapes = []
    for a, kind in zip(arrays, kinds):
        blk = a.shape if kind == "gather" else a.shape[1:]
        shapes.append(jax.ShapeDtypeStruct((N_DEV,) + tuple(blk), a.dtype))
    return shapes


def _comm_sems(n):
    return [pltpu.SemaphoreType.DMA((n, N_DEV - 1)), pltpu.SemaphoreType.DMA((n, N_DEV - 1)),
            pltpu.SemaphoreType.DMA((n,))]


def _comm_copies(ins, outs, kinds, send_sems, recv_sems, local_sems):
    x, y, c = lax.axis_index("x"), lax.axis_index("y"), lax.axis_index("c")
    me = 4 * x + 2 * y + c

    def src_for(t, dev_lin):
        return ins[t] if kinds[t] == "gather" else ins[t].at[dev_lin]

    local = [pltpu.make_async_copy(src_for(t, me), outs[t].at[me], local_sems.at[t]) for t in range(len(kinds))]
    sends, arrivals = [], []
    for k in range(1, N_DEV):
        px = 1 - x if (k >> 2) & 1 else x
        py = 1 - y if (k >> 1) & 1 else y
        pc = 1 - c if k & 1 else c
        peer_lin = 4 * px + 2 * py + pc
        for t in range(len(kinds)):
            sems = dict(send_sem=send_sems.at[t, k - 1], recv_sem=recv_sems.at[t, k - 1],
                        device_id=(px, py, pc), device_id_type=MESH)
            sends.append(pltpu.make_async_remote_copy(src_ref=src_for(t, peer_lin), dst_ref=outs[t].at[me], **sems))
            arrivals.append(pltpu.make_async_remote_copy(src_ref=src_for(t, peer_lin), dst_ref=outs[t].at[peer_lin], **sems))
    return local, sends, arrivals


def _comm_start(ins, outs, kinds, sems):
    local, sends, _ = _comm_copies(ins, outs, kinds, *sems)
    for cp in local + sends:
        cp.start()


def _comm_finish(ins, outs, kinds, sems):
    local, sends, arrivals = _comm_copies(ins, outs, kinds, *sems)
    for cp in arrivals:
        cp.wait_recv()
    for cp in sends:
        cp.wait_send()
    for cp in local:
        cp.wait()


def _exchange(name, arrays, kinds):
    n = len(arrays)

    def body(*refs):
        ins, outs, sems = refs[:n], refs[n:2 * n], refs[2 * n:]
        _comm_start(ins, outs, kinds, sems)
        _comm_finish(ins, outs, kinds, sems)

    any_spec = pl.BlockSpec(memory_space=pl.ANY)
    return pl.pallas_call(
        body, name=name, out_shape=_comm_out_shapes(arrays, kinds),
        in_specs=[any_spec] * n, out_specs=[any_spec] * n,
        scratch_shapes=_comm_sems(n),
    )(*arrays)


def _call(body, name, grid, in_specs, out_specs, out_shape, args, scratch_shapes=(), comm=None):
    if comm is None:
        outs = pl.pallas_call(body, name=name, grid=grid, in_specs=in_specs, out_specs=out_specs,
                              out_shape=out_shape, scratch_shapes=list(scratch_shapes),
                              compiler_params=_params())(*args)
        return outs, []
    arrays, kinds = comm
    n, n_in, n_out, n_scr = len(arrays), len(in_specs), len(out_specs), len(scratch_shapes)

    def fused(*refs):
        c_in, x_in = refs[:n_in], refs[n_in:n_in + n]
        c_out = refs[n_in + n:n_in + n + n_out]
        x_out = refs[n_in + n + n_out:n_in + 2 * n + n_out]
        rest = refs[n_in + 2 * n + n_out:]
        c_scr, sems = rest[:n_scr], rest[n_scr:]
        ids = [pl.program_id(a) for a in range(len(grid))]
        is_first = functools.reduce(jnp.logical_and, [i == 0 for i in ids])
        is_last = functools.reduce(jnp.logical_and, [i == g - 1 for i, g in zip(ids, grid)])

        @pl.when(is_first)
        def _():
            _comm_start(x_in, x_out, kinds, sems)

        body(*c_in, *c_out, *c_scr)

        @pl.when(is_last)
        def _():
            _comm_finish(x_in, x_out, kinds, sems)

    any_spec = pl.BlockSpec(memory_space=pl.ANY)
    outs = pl.pallas_call(
        fused, name=name, grid=grid,
        in_specs=list(in_specs) + [any_spec] * n, out_specs=list(out_specs) + [any_spec] * n,
        out_shape=list(out_shape) + _comm_out_shapes(arrays, kinds),
        scratch_shapes=list(scratch_shapes) + _comm_sems(n),
        compiler_params=_params())(*args, *arrays)
    return outs[:n_out], outs[n_out:]


def _ln_proj(x, g, b, w_in_t):
    s, d = x.shape
    cols = w_in_t.shape[0]
    tm = min(ROW_TILE, s)

    def body(x_ref, g_ref, b_ref, w_ref, h_ref, p_ref):
        xhat, _ = _ln_hat(x_ref[...])
        h = (xhat * g_ref[...] + b_ref[...]).astype(BF16)
        h_ref[...] = h
        p_ref[...] = _dot_nt(h, w_ref[...]).astype(BF16)

    row = lambda width: pl.BlockSpec((tm, width), lambda i: (i, 0))
    return pl.pallas_call(
        body, name="ln_proj", grid=(s // tm,),
        in_specs=[row(d), _full((1, d)), _full((1, d)), _full((cols, d))],
        out_specs=[row(d), row(cols)],
        out_shape=[jax.ShapeDtypeStruct((s, d), BF16), jax.ShapeDtypeStruct((s, cols), BF16)],
        compiler_params=_params(),
    )(x, g, b, w_in_t)


def _sb_triangles(t):
    row = lax.broadcasted_iota(jnp.int32, (t, t), 0)
    col = lax.broadcasted_iota(jnp.int32, (t, t), 1)
    return (row > col).astype(BF16), (row >= col).astype(BF16)


def _sb_first_mask(t, has_prev):
    qrow = lax.broadcasted_iota(jnp.int32, (2 * t, 2 * t), 0) & (t - 1)
    col = lax.broadcasted_iota(jnp.int32, (2 * t, 2 * t), 1)
    return ((col < t) & has_prev) | ((col >= t) & (col - t < qrow))


def _sb_stack_heads(x2, first):
    zero = jnp.zeros_like(x2)
    return jnp.concatenate([jnp.where(first, x2, zero), jnp.where(first, zero, x2)], axis=0)


def _sb_key_tiles(ref, offs, t):
    tiles = [ref[pl.ds(off, t), :] for off in offs]
    return tiles[0] if len(tiles) == 1 else jnp.concatenate(tiles, axis=0)


def _sb_suffix(terms, row_sums, tri, carry):
    out = [None] * len(terms)
    for j in reversed(range(len(terms))):
        suf = carry
        for op in terms[j]:
            suf = suf + _dot(op, tri)
        out[j] = suf
        carry = carry + row_sums[j]
    return (out[0] if len(out) == 1 else jnp.concatenate(out, axis=1)), carry


def _sb_scores(qh, k_t, upper, carry_l, mask, t):
    z = _dot_nt(qh, k_t)
    sp = jnp.log(1.0 + jnp.exp(-jnp.abs(z)))
    neg = jnp.minimum(z, 0.0)
    lb = neg - sp
    l1 = (neg - z) - sp
    if mask is not None:
        l1 = jnp.where(mask, l1, 0.0)
    hi = l1.astype(BF16)
    lo = (l1 - hi.astype(F32)).astype(BF16)
    cols = [slice(j * t, (j + 1) * t) for j in range(z.shape[1] // t)]
    suf, carry_l = _sb_suffix([[hi[:, c], lo[:, c]] for c in cols], [_rowsum(l1[:, c]) for c in cols], upper, carry_l)
    a = jnp.exp(lb + suf)
    if mask is not None:
        a = jnp.where(mask, a, 0.0)
    return lb, a, carry_l


def _sb_walk(i, t, first_visit, visit, init):
    def alive(carry):
        return jnp.max(carry[0]) > SB_UNDERFLOW

    prev = pl.multiple_of(jnp.maximum(i - 1, 0) * t, t)
    carry = first_visit((prev, pl.multiple_of(i * t, t)), init)

    def cond(state):
        j, go, _ = state
        return (j < i - 1) & go

    def body(state):
        j, _, carry = state
        carry = visit((pl.multiple_of((i - 2 - j) * t, t),), carry)
        return j + 1, alive(carry), carry

    return lax.while_loop(cond, body, (jnp.int32(0), alive(carry), carry))[2]


def _sb_first_specs(n_pairs, nq, t):
    at = lambda h, i: (h * nq + i, 0, 0)
    specs = [pl.BlockSpec((None, 2 * t, 2 * t), at), pl.BlockSpec((None, 2 * t, 2 * t), at), pl.BlockSpec((None, 2 * t, 1), at)]
    shapes = [jax.ShapeDtypeStruct((n_pairs * nq, 2 * t, 2 * t), BF16), jax.ShapeDtypeStruct((n_pairs * nq, 2 * t, 2 * t), F32),
              jax.ShapeDtypeStruct((n_pairs * nq, 2 * t, 1), F32)]
    return specs, shapes


def _sb_fwd(proj, n_pairs, comm=None):
    s = proj.shape[0]
    t = min(SB_TILE, s)
    nq = s // t

    def body(q_ref, k_ref, v_ref, o_ref, ab_ref, beta_ref, cl_ref):
        i = pl.program_id(1)
        lane = lax.broadcasted_iota(jnp.int32, (1, LANES), 1)
        first = lane < HEAD_DIM
        qs = _sb_stack_heads(q_ref[...] * SCALE, first)
        upper, _ = _sb_triangles(t)

        def first_visit(offs, carry):
            c_l, acc = carry
            mask = _sb_first_mask(t, i > 0)
            lb, a, c_l = _sb_scores(qs, _sb_key_tiles(k_ref, offs, t), upper, c_l, mask, t)
            a_b = a.astype(BF16)
            ab_ref[...] = a_b
            beta_ref[...] = jnp.where(mask, jnp.exp(lb), 0.0)
            cl_ref[...] = c_l
            return c_l, acc + _dot(a_b, _sb_key_tiles(v_ref, offs, t))

        def visit(offs, carry):
            c_l, acc = carry
            _, a, c_l = _sb_scores(qs, _sb_key_tiles(k_ref, offs, t), upper, c_l, None, t)
            return c_l, acc + _dot(a.astype(BF16), _sb_key_tiles(v_ref, offs, t))

        init = (jnp.zeros((2 * t, 1), F32), jnp.zeros((2 * t, LANES), F32))
        _, acc = _sb_walk(i, t, first_visit, visit, init)
        o_ref[...] = jnp.where(first, acc[:t], acc[t:])

    first_specs, first_shapes = _sb_first_specs(n_pairs, nq, t)
    outs, landed = _call(
        body, "sb_fwd", (n_pairs, nq),
        in_specs=[pl.BlockSpec((t, LANES), lambda h, i: (i, h)),
                  pl.BlockSpec((s, LANES), lambda h, i: (0, n_pairs + h)),
                  pl.BlockSpec((s, LANES), lambda h, i: (0, 2 * n_pairs + h))],
        out_specs=[pl.BlockSpec((t, LANES), lambda h, i: (i, h))] + first_specs,
        out_shape=[jax.ShapeDtypeStruct((s, n_pairs * LANES), F32)] + first_shapes,
        args=(proj, proj, proj), comm=comm)
    return outs[0], outs[1:], landed


def _swa_bucket_table():
    qi = np.arange(SWA_BLOCK)[:, None]
    cj = np.arange(2 * SWA_BLOCK)[None, :]
    dist = qi + SWA_BLOCK - cj
    exact = REL_BUCKETS // 2
    d = np.maximum(dist, 0)
    d_f = np.maximum(d, 1).astype(np.float32)
    large = exact + (np.log(d_f / np.float32(exact)) / np.float32(math.log(REL_MAX_DIST / exact))
                     * np.float32(REL_BUCKETS - exact)).astype(np.int32)
    large = np.minimum(large, REL_BUCKETS - 1)
    return np.where(d < exact, d, large).astype(np.int32)


def _swa_build_bias(bucket_ref, rb_ref, bias_ref, n_groups, per_group):
    bk = bucket_ref[...]
    dist = (lax.broadcasted_iota(jnp.int32, bk.shape, 0) + SWA_BLOCK) - lax.broadcasted_iota(jnp.int32, bk.shape, 1)
    window = (dist >= 0) & (dist < SWA_BLOCK)
    for g in range(n_groups):
        for hh in range(per_group):
            acc = jnp.zeros(bk.shape, F32)
            for b in range(REL_BUCKETS):
                acc = jnp.where(bk == b, rb_ref[b, g * per_group + hh], acc)
            bias_ref[g, hh * SWA_BLOCK:(hh + 1) * SWA_BLOCK, :] = jnp.where(window, acc, -jnp.inf)


def _swa_first_block_mask(i):
    col = lax.broadcasted_iota(jnp.int32, (1, 2 * SWA_BLOCK), 1)
    return jnp.where((col < SWA_BLOCK) & (i == 0), -jnp.inf, 0.0)


def _swa_place(blk, h, group, sel):
    if (h % 2) != group:
        blk = pltpu.roll(blk.astype(F32), HEAD_DIM, axis=1).astype(BF16)
    return jnp.where(sel, blk, jnp.zeros_like(blk))


def _swa_stack(ref, group, per_group, sel, scale=1.0):
    parts = []
    for hh in range(per_group):
        h = group * per_group + hh
        parts.append(_swa_place(ref[:, (h // 2) * LANES:(h // 2 + 1) * LANES], h, group, sel))
    stacked = jnp.concatenate(parts, axis=0)
    return stacked if scale == 1.0 else stacked * scale


def _swa_unstack(stacked, group, per_group, pieces):
    for hh in range(per_group):
        h = group * per_group + hh
        piece = stacked[hh * SWA_BLOCK:(hh + 1) * SWA_BLOCK, :]
        pieces[h] = pltpu.roll(piece, HEAD_DIM, axis=1) if (h % 2) != group else piece


def _swa_sink_rows(sk_ref, group, per_group):
    rowh = lax.broadcasted_iota(jnp.int32, (per_group * SWA_BLOCK, 1), 0) // SWA_BLOCK
    sink = jnp.zeros((per_group * SWA_BLOCK, 1), F32) + sk_ref[0, group * per_group]
    for hh in range(1, per_group):
        sink = jnp.where(rowh == hh, sk_ref[0, group * per_group + hh], sink)
    return sink


def _swa_probs(q_pos, kcat, bias_h, first_mask, sink):
    logits = _dot_nt(q_pos, kcat) + (bias_h + first_mask)
    m = jnp.maximum(jnp.max(logits, axis=1, keepdims=True), sink)
    p = jnp.exp(logits - m)
    es = jnp.exp(sink - m)
    inv = 1.0 / (_rowsum(p) + es)
    return p * inv, es * inv


def _swa_specs(n_heads, qcol, kcol, vcol):
    width = n_heads * HEAD_DIM
    prev = lambda col: pl.BlockSpec((SWA_BLOCK, LANES), lambda i: (jnp.maximum(i - 1, 0), col))
    cur = lambda col: pl.BlockSpec((SWA_BLOCK, LANES), lambda i: (i, col))
    return [pl.BlockSpec((SWA_BLOCK, width), lambda i: (i, qcol)),
            prev(kcol), cur(kcol), prev(vcol), cur(vcol),
            _full((SWA_BLOCK, 2 * SWA_BLOCK)),
            pl.BlockSpec(memory_space=pltpu.SMEM), pl.BlockSpec(memory_space=pltpu.SMEM)]


def _swa_kept_specs(nb, n_groups, per_group):
    rows = per_group * SWA_BLOCK
    at = lambda i: (i, 0, 0, 0)
    specs = [pl.BlockSpec((None, n_groups, rows, 2 * SWA_BLOCK), at), pl.BlockSpec((None, n_groups, rows, 1), at)]
    shapes = [jax.ShapeDtypeStruct((nb, n_groups, rows, 2 * SWA_BLOCK), F32), jax.ShapeDtypeStruct((nb, n_groups, rows, 1), F32)]
    return specs, shapes


def _swa_fwd(proj, bucket, rel_bias, sinks, n_heads, qcol, kcol, vcol, comm=None):
    s = proj.shape[0]
    width = n_heads * HEAD_DIM
    n_groups = LANES // HEAD_DIM
    per_group = n_heads // n_groups

    def body(q_ref, kp_ref, kc_ref, vp_ref, vc_ref, bucket_ref, rb_ref, sk_ref, o_ref, prob_ref, psink_ref, bias_ref):
        i = pl.program_id(0)

        @pl.when(i == 0)
        def _():
            _swa_build_bias(bucket_ref, rb_ref, bias_ref, n_groups, per_group)

        lane = lax.broadcasted_iota(jnp.int32, (1, LANES), 1)
        first = lane < HEAD_DIM
        first_mask = _swa_first_block_mask(i)
        kcat = jnp.concatenate([kp_ref[...], kc_ref[...]], axis=0)
        vcat = jnp.concatenate([vp_ref[...], vc_ref[...]], axis=0)
        pieces = {}
        for g in range(n_groups):
            sel = first if g == 0 else jnp.logical_not(first)
            prob, p_sink = _swa_probs(_swa_stack(q_ref, g, per_group, sel, SCALE), kcat, bias_ref[g], first_mask,
                                      _swa_sink_rows(sk_ref, g, per_group))
            prob_ref[g] = prob
            psink_ref[g] = p_sink
            _swa_unstack(_dot(prob.astype(BF16), vcat), g, per_group, pieces)
        for j in range(n_heads // 2):
            o_ref[:, j * LANES:(j + 1) * LANES] = jnp.where(first, pieces[2 * j], pieces[2 * j + 1])

    kept_specs, kept_shapes = _swa_kept_specs(s // SWA_BLOCK, n_groups, per_group)
    outs, landed = _call(
        body, "swa_fwd", (s // SWA_BLOCK,),
        in_specs=_swa_specs(n_heads, qcol, kcol, vcol),
        out_specs=[pl.BlockSpec((SWA_BLOCK, width), lambda i: (i, 0))] + kept_specs,
        out_shape=[jax.ShapeDtypeStruct((s, width), F32)] + kept_shapes,
        args=(proj, proj, proj, proj, proj, bucket, rel_bias, sinks),
        scratch_shapes=[pltpu.VMEM((n_groups, per_group * SWA_BLOCK, 2 * SWA_BLOCK), F32)], comm=comm)
    return outs[0], outs[1:], landed


def _rms_fwd(o, g):
    r = lax.rsqrt(jnp.mean(o * o, axis=-1, keepdims=True) + RMS_EPS)
    n = o * r
    return n, r, n * g


def _mix_ffn(sb_out, sw_out, x, g_in, b_in, sb_g, sw_g, w_out, g1, b1, w_gu_t, w_down, g2, b2, target):
    s, d = x.shape
    wsb, wsw = sb_out.shape[1], sw_out.shape[1]
    dff = w_down.shape[0]
    assert wsb + wsw == d
    tm = min(FFN_TILE, s)

    def body(sb_ref, sw_ref, x_ref, gi_ref, bi_ref, sbg_ref, swg_ref, wo_hbm, g1_ref, b1_ref, wgu_hbm, wd_hbm,
             g2_ref, b2_ref, t_ref,
             du1_ref, mg_ref, h1b_ref, act_ref, dgu_ref, du2b_ref, dsb_ref, dsw_ref, st_ref,
             wo_ref, wgu_ref, wd_ref):
        @pl.when(pl.program_id(0) == 0)
        def _():
            pltpu.sync_copy(wo_hbm, wo_ref)
            pltpu.sync_copy(wgu_hbm, wgu_ref)
            pltpu.sync_copy(wd_hbm, wd_ref)
            st_ref[...] = jnp.zeros_like(st_ref)

        sb, sw = sb_ref[...], sw_ref[...]
        _, _, m_sb = _rms_fwd(sb, sbg_ref[...])
        _, _, m_sw = _rms_fwd(sw, swg_ref[...])
        m_sb = m_sb.astype(BF16)
        m_sw = m_sw.astype(BF16)
        mg_ref[:, :wsb] = m_sb
        mg_ref[:, wsb:] = m_sw
        xhat0, _ = _ln_hat(x_ref[...])
        u1 = ALPHA * (xhat0 * gi_ref[...] + bi_ref[...]) + _dot(m_sb, wo_ref[:wsb, :]) + _dot(m_sw, wo_ref[wsb:, :])

        xhat1, r1 = _ln_hat(u1)
        h1 = xhat1 * g1_ref[...] + b1_ref[...]
        h1b = h1.astype(BF16)
        h1b_ref[...] = h1b
        gate = _dot_nt(h1b, wgu_ref[:dff, :])
        up = _dot_nt(h1b, wgu_ref[dff:, :])
        sg = jax.nn.sigmoid(gate)
        silu = gate * sg
        act = (silu * up).astype(BF16)
        act_ref[...] = act
        u2 = ALPHA * h1 + _dot(act, wd_ref[...])
        xhat2, r2 = _ln_hat(u2)
        diff = xhat2 * g2_ref[...] + b2_ref[...] - t_ref[...]
        dh2 = diff * (1.0 / d)
        st_ref[0:1, :] += _colsum(dh2 * xhat2)
        st_ref[1:2, :] += _colsum(dh2)
        st_ref[2:3, :] += jnp.broadcast_to(_colsum(_rowsum(diff * diff)) * (0.5 / d), (1, d))
        du2 = _ln_bwd(dh2 * g2_ref[...], xhat2, r2)
        du2b = du2.astype(BF16)
        du2b_ref[...] = du2b
        dact = _dot_nt(du2b, wd_ref[...])
        dgate = (dact * up * (sg * (1.0 + gate * (1.0 - sg)))).astype(BF16)
        dup = (dact * silu).astype(BF16)
        dgu_ref[:, :dff] = dgate
        dgu_ref[:, dff:] = dup
        dh1 = _dot(dgate, wgu_ref[:dff, :]) + _dot(dup, wgu_ref[dff:, :]) + ALPHA * du2
        st_ref[3:4, :] += _colsum(dh1 * xhat1)
        st_ref[4:5, :] += _colsum(dh1)
        du1 = _ln_bwd(dh1 * g1_ref[...], xhat1, r1)
        du1_ref[...] = du1

        dmerged = _dot_nt(du1.astype(BF16), wo_ref[...])
        dsb, gsb = _rms_bwd(dmerged[:, :wsb], sb, sbg_ref[...])
        dsw, gsw = _rms_bwd(dmerged[:, wsb:], sw, swg_ref[...])
        dsb_ref[...] = dsb.astype(BF16)
        dsw_ref[...] = dsw.astype(BF16)
        st_ref[5:6, :wsb] += gsb
        st_ref[5:6, wsb:] += gsw

    row = lambda width: pl.BlockSpec((tm, width), lambda i: (i, 0))
    vec = lambda width: _full((1, width))
    hbm = pl.BlockSpec(memory_space=pl.ANY)
    bf = lambda width: jax.ShapeDtypeStruct((s, width), BF16)
    return pl.pallas_call(
        body, name="mix_ffn", grid=(s // tm,),
        in_specs=[row(wsb), row(wsw), row(d), vec(d), vec(d), vec(wsb), vec(wsw), hbm, vec(d), vec(d), hbm, hbm,
                  vec(d), vec(d), row(d)],
        out_specs=[row(d), row(d), row(d), row(dff), row(2 * dff), row(d), row(wsb), row(wsw), _full((8, d))],
        out_shape=[jax.ShapeDtypeStruct((s, d), F32), bf(d), bf(d), bf(dff), bf(2 * dff), bf(d), bf(wsb), bf(wsw),
                   jax.ShapeDtypeStruct((8, d), F32)],
        scratch_shapes=[pltpu.VMEM(w_out.shape, BF16), pltpu.VMEM(w_gu_t.shape, BF16), pltpu.VMEM(w_down.shape, BF16)],
        compiler_params=_params(60),
    )(sb_out, sw_out, x, g_in, b_in, sb_g, sw_g, w_out, g1, b1, w_gu_t, w_down, g2, b2, target)


def _rms_bwd(dm, o, g):
    n, r, _ = _rms_fwd(o, g)
    dn = dm * g
    return r * (dn - n * jnp.mean(dn * n, axis=-1, keepdims=True)), _colsum(dm * n)


def _sb_bwd(proj, dout, out, first, n_pairs, comm=None):
    s = proj.shape[0]
    t = min(SB_TILE, s)
    nq = s // t
    width = n_pairs * LANES

    def body(q_ref, k_ref, v_ref, do_ref, o_ref, ab_ref, beta_ref, cl_ref, dq_ref, dk_out, dv_out, dk_ref, dv_ref):
        i = pl.program_id(1)

        @pl.when(i == 0)
        def _():
            dk_ref[...] = jnp.zeros_like(dk_ref)
            dv_ref[...] = jnp.zeros_like(dv_ref)

        lane = lax.broadcasted_iota(jnp.int32, (1, LANES), 1)
        first_lanes = lane < HEAD_DIM
        do2 = do_ref[...]
        qs = _sb_stack_heads(q_ref[...] * SCALE, first_lanes)
        dos = _sb_stack_heads(do2, first_lanes)
        prod = do2.astype(F32) * o_ref[...]
        totals = jnp.concatenate([_rowsum(jnp.where(first_lanes, prod, 0.0)), _rowsum(jnp.where(first_lanes, 0.0, prod))], axis=0)
        upper, incl = _sb_triangles(t)

        def grads(offs, k_t, v_t, a_b, beta, c_e, dq):
            d_e = _dot_nt(dos, v_t) * a_b.astype(F32)
            d_eb = d_e.astype(BF16)
            cols = [slice(j * t, (j + 1) * t) for j in range(len(offs))]
            suf_e, c_e = _sb_suffix([[d_eb[:, c]] for c in cols], [_rowsum(d_e[:, c]) for c in cols], incl, c_e)
            dzb = (d_e - beta * (d_e + (totals - suf_e))).astype(BF16)
            dk_t = _dot_tn(dzb, qs)
            dv_t = _dot_tn(a_b, dos)
            for off, c in zip(offs, cols):
                dk_ref[pl.ds(off, t), :] += dk_t[c, :]
                dv_ref[pl.ds(off, t), :] += dv_t[c, :]
            return c_e, dq + _dot(dzb, k_t)

        def first_visit(offs, carry):
            _, c_e, dq = carry
            k_t = _sb_key_tiles(k_ref, offs, t)
            v_t = _sb_key_tiles(v_ref, offs, t)
            c_e, dq = grads(offs, k_t, v_t, ab_ref[...], beta_ref[...], c_e, dq)
            return cl_ref[...], c_e, dq

        def visit(offs, carry):
            c_l, c_e, dq = carry
            k_t = _sb_key_tiles(k_ref, offs, t)
            v_t = _sb_key_tiles(v_ref, offs, t)
            lb, a, c_l = _sb_scores(qs, k_t, upper, c_l, None, t)
            c_e, dq = grads(offs, k_t, v_t, a.astype(BF16), jnp.exp(lb), c_e, dq)
            return c_l, c_e, dq

        init = (jnp.zeros((2 * t, 1), F32), jnp.zeros((2 * t, 1), F32), jnp.zeros((2 * t, LANES), F32))
        _, _, dq = _sb_walk(i, t, first_visit, visit, init)
        dq_ref[...] = (jnp.where(first_lanes, dq[:t], dq[t:]) * SCALE).astype(BF16)

        @pl.when(i == nq - 1)
        def _():
            dk_out[...] = dk_ref[...].astype(BF16)
            dv_out[...] = dv_ref[...].astype(BF16)

    qblk = pl.BlockSpec((t, LANES), lambda h, i: (i, h))
    whole = pl.BlockSpec((s, LANES), lambda h, i: (0, h))
    first_specs, _ = _sb_first_specs(n_pairs, nq, t)
    return _call(
        body, "sb_bwd", (n_pairs, nq),
        in_specs=[qblk,
                  pl.BlockSpec((s, LANES), lambda h, i: (0, n_pairs + h)),
                  pl.BlockSpec((s, LANES), lambda h, i: (0, 2 * n_pairs + h)),
                  qblk, qblk] + first_specs,
        out_specs=[qblk, whole, whole],
        out_shape=[jax.ShapeDtypeStruct((s, width), BF16)] * 3,
        args=(proj, proj, proj, dout, out, *first),
        scratch_shapes=[pltpu.VMEM((s, LANES), F32), pltpu.VMEM((s, LANES), F32)], comm=comm)


def _swa_bwd(proj, dout, kept, bucket, n_heads, qcol, kcol, vcol, comm=None):
    s = proj.shape[0]
    width = n_heads * HEAD_DIM
    n_groups = LANES // HEAD_DIM
    per_group = n_heads // n_groups
    nb = s // SWA_BLOCK

    def body(q_ref, kp_ref, kc_ref, vp_ref, vc_ref, bucket_ref, do_ref, prob_ref, psink_ref,
             dq_ref, dk_out, dv_out, dsk_ref, drb_ref, dbias_ref, dk_ref, dv_ref):
        i = pl.program_id(0)

        @pl.when(i == 0)
        def _():
            dbias_ref[...] = jnp.zeros_like(dbias_ref)
            dk_ref[...] = jnp.zeros_like(dk_ref)
            dv_ref[...] = jnp.zeros_like(dv_ref)
            dsk_ref[...] = jnp.zeros_like(dsk_ref)

        lane = lax.broadcasted_iota(jnp.int32, (1, LANES), 1)
        first = lane < HEAD_DIM
        kcat = jnp.concatenate([kp_ref[...], kc_ref[...]], axis=0)
        vcat = jnp.concatenate([vp_ref[...], vc_ref[...]], axis=0)
        dkcat = jnp.zeros((2 * SWA_BLOCK, LANES), F32)
        dvcat = jnp.zeros((2 * SWA_BLOCK, LANES), F32)
        pieces = {}
        for g in range(n_groups):
            sel = first if g == 0 else jnp.logical_not(first)
            q_g = _swa_stack(q_ref, g, per_group, sel, SCALE)
            do_g = _swa_stack(do_ref, g, per_group, sel)
            prob, p_sink = prob_ref[g], psink_ref[g]
            dprob = _dot_nt(do_g, vcat)
            delta = _rowsum(prob * dprob)
            dlog = prob * (dprob - delta)
            sink_term = p_sink * delta
            for hh in range(per_group):
                h = g * per_group + hh
                tot = _colsum(sink_term[hh * SWA_BLOCK:(hh + 1) * SWA_BLOCK, :])
                dsk_ref[h:h + 1, :] += jnp.broadcast_to(-tot, (1, LANES))
            dbias_ref[g] += dlog
            dlb = dlog.astype(BF16)
            _swa_unstack(_dot(dlb, kcat) * SCALE, g, per_group, pieces)
            dkcat += _dot_tn(dlb, q_g)
            dvcat += _dot_tn(prob.astype(BF16), do_g)
        for j in range(n_heads // 2):
            dq_ref[:, j * LANES:(j + 1) * LANES] = jnp.where(first, pieces[2 * j], pieces[2 * j + 1]).astype(BF16)

        cur = pl.multiple_of(i * SWA_BLOCK, SWA_BLOCK)
        dk_ref[pl.ds(cur, SWA_BLOCK), :] += dkcat[SWA_BLOCK:, :]
        dv_ref[pl.ds(cur, SWA_BLOCK), :] += dvcat[SWA_BLOCK:, :]

        @pl.when(i > 0)
        def _():
            prv = pl.multiple_of((i - 1) * SWA_BLOCK, SWA_BLOCK)
            dk_ref[pl.ds(prv, SWA_BLOCK), :] += dkcat[:SWA_BLOCK, :]
            dv_ref[pl.ds(prv, SWA_BLOCK), :] += dvcat[:SWA_BLOCK, :]

        @pl.when(i == nb - 1)
        def _():
            bk = bucket_ref[...]
            rowi = lax.broadcasted_iota(jnp.int32, (REL_BUCKETS, LANES), 0)
            coli = lax.broadcasted_iota(jnp.int32, (REL_BUCKETS, LANES), 1)
            res = jnp.zeros((REL_BUCKETS, LANES), F32)
            for h in range(n_heads):
                g, hh = divmod(h, per_group)
                db = dbias_ref[g, hh * SWA_BLOCK:(hh + 1) * SWA_BLOCK, :]
                for b in range(REL_BUCKETS):
                    tot = _colsum(_rowsum(jnp.where(bk == b, db, 0.0)))
                    res = jnp.where((rowi == b) & (coli == h), tot, res)
            drb_ref[...] = res
            dk_out[...] = dk_ref[...].astype(BF16)
            dv_out[...] = dv_ref[...].astype(BF16)

    kept_specs, _ = _swa_kept_specs(nb, n_groups, per_group)
    in_specs = _swa_specs(n_heads, qcol, kcol, vcol)[:6] + [pl.BlockSpec((SWA_BLOCK, width), lambda i: (i, 0))] + kept_specs
    return _call(
        body, "swa_bwd", (nb,),
        in_specs=in_specs,
        out_specs=[pl.BlockSpec((SWA_BLOCK, width), lambda i: (i, 0)),
                   _full((s, LANES)), _full((s, LANES)), _full((8, LANES)), _full((REL_BUCKETS, LANES))],
        out_shape=[jax.ShapeDtypeStruct((s, width), BF16), jax.ShapeDtypeStruct((s, LANES), BF16),
                   jax.ShapeDtypeStruct((s, LANES), BF16), jax.ShapeDtypeStruct((8, LANES), F32),
                   jax.ShapeDtypeStruct((REL_BUCKETS, LANES), F32)],
        args=(proj, proj, proj, proj, proj, bucket, dout, *kept),
        scratch_shapes=[pltpu.VMEM((n_groups, per_group * SWA_BLOCK, 2 * SWA_BLOCK), F32),
                        pltpu.VMEM((s, LANES), F32), pltpu.VMEM((s, LANES), F32)],
        comm=comm)


def _proj_bwd(pieces, w_in_t, du1, x, g_in, comm=None):
    s, d = x.shape
    cols = w_in_t.shape[0]
    tm = min(ROW_TILE, s)
    n_p = len(pieces)

    def body(*refs):
        p_refs = refs[:n_p]
        w_ref, du_ref, x_ref, g_ref, dx_ref, st_ref = refs[n_p:]
        i = pl.program_id(0)

        @pl.when(i == 0)
        def _():
            st_ref[...] = jnp.zeros_like(st_ref)

        dproj = jnp.concatenate([p[...] for p in p_refs], axis=1)
        dh0 = _dot(dproj, w_ref[...]) + ALPHA * du_ref[...]
        xhat, r = _ln_hat(x_ref[...])
        st_ref[0:1, :] += _colsum(dh0 * xhat)
        st_ref[1:2, :] += _colsum(dh0)
        dx_ref[...] = _ln_bwd(dh0 * g_ref[...], xhat, r)

    row = lambda width: pl.BlockSpec((tm, width), lambda i: (i, 0))
    return _call(
        body, "proj_bwd", (s // tm,),
        in_specs=[row(p.shape[1]) for p in pieces] + [_full((cols, d)), row(d), row(d), _full((1, d))],
        out_specs=[row(d), _full((8, d))],
        out_shape=[jax.ShapeDtypeStruct((s, d), F32), jax.ShapeDtypeStruct((8, d), F32)],
        args=(*pieces, w_in_t, du1, x, g_in), comm=comm)


def _wgrad(name, pieces, b, tm, tn):
    s, n = b.shape
    m = sum(p.shape[1] for p in pieces)
    n_p = len(pieces)
    assert n_p == 1 or tm == m
    ts = min(WGRAD_TOKENS if b.dtype == BF16 and n_p == 1 else WGRAD_TOKENS // 2, s)
    n_k = s // ts

    def body(*refs):
        p_refs, b_ref, o_ref, acc_ref = refs[:n_p], refs[n_p], refs[n_p + 1], refs[n_p + 2]
        k = pl.program_id(2)

        @pl.when(k == 0)
        def _():
            acc_ref[...] = jnp.zeros_like(acc_ref)

        a = p_refs[0][...] if n_p == 1 else jnp.concatenate([p[...] for p in p_refs], axis=1)
        acc_ref[...] += _dot_tn(a, b_ref[...].astype(BF16))

        @pl.when(k == n_k - 1)
        def _():
            o_ref[...] = acc_ref[...].astype(BF16)

    piece_spec = lambda p: pl.BlockSpec((ts, tm if n_p == 1 else p.shape[1]), lambda i, j, k: (k, i))
    return pl.pallas_call(
        body, name=name, grid=(m // tm, n // tn, n_k),
        in_specs=[piece_spec(p) for p in pieces] + [pl.BlockSpec((ts, tn), lambda i, j, k: (k, j))],
        out_specs=pl.BlockSpec((tm, tn), lambda i, j, k: (i, j)),
        out_shape=jax.ShapeDtypeStruct((m, n), BF16),
        scratch_shapes=[pltpu.VMEM((tm, tn), F32)],
        compiler_params=_params(),
    )(*pieces, b)


def _adamw_math(w, g, m, v):
    m = ADAM_B1 * m + (1.0 - ADAM_B1) * g
    v = ADAM_B2 * v + (1.0 - ADAM_B2) * (g * g)
    m_hat = m / (1.0 - ADAM_B1 ** ADAM_STEP)
    v_hat = v / (1.0 - ADAM_B2 ** ADAM_STEP)
    delta = -ADAM_LR * (m_hat / (jnp.sqrt(v_hat) + ADAM_EPS) + ADAM_WD * w)
    return delta, m, v


def _adamw_rows(rows):
    return max(r for r in range(16, 257, 16) if rows % r == 0)


def _adamw(name, landed, w, m, v, tr):
    rows, cols = w.shape

    def body(l_ref, w_ref, m_ref, v_ref, g_ref, d_ref, nm_ref, nv_ref):
        g = l_ref[0].astype(F32)
        for src in range(1, N_DEV):
            g = g + l_ref[src].astype(F32)
        delta, nm, nv = _adamw_math(w_ref[...], g, m_ref[...], v_ref[...])
        g_ref[...] = g
        d_ref[...] = delta
        nm_ref[...] = nm
        nv_ref[...] = nv

    blk = pl.BlockSpec((tr, cols), lambda i: (i, 0))
    shape = jax.ShapeDtypeStruct((rows, cols), F32)
    return pl.pallas_call(
        body, name=name, grid=(rows // tr,),
        in_specs=[pl.BlockSpec((N_DEV, tr, cols), lambda i: (0, i, 0)), blk, blk, blk],
        out_specs=[blk, blk, blk, blk],
        out_shape=[shape, shape, shape, shape],
        compiler_params=_params(),
    )(landed, w, m, v)


def _pack(d, ln_in_g, ln_in_b, ln1_g, ln1_b, ln2_g, ln2_b, sb_g, sw_g, rel_bias, sinks, extra=None):
    tail = [rel_bias.reshape(-1), sinks.reshape(-1)]
    if extra is not None:
        tail.append(extra.reshape(-1))
    tail = jnp.concatenate(tail)
    tail = jnp.concatenate([tail, jnp.zeros((d - tail.shape[0],), F32)])
    rows = [ln_in_g.reshape(-1), ln_in_b.reshape(-1), ln1_g.reshape(-1), ln1_b.reshape(-1),
            ln2_g.reshape(-1), ln2_b.reshape(-1),
            jnp.concatenate([sb_g.reshape(-1), sw_g.reshape(-1)]), tail]
    return jnp.stack(rows)


def _unpack(p, wsb, n_rb, n_sk):
    return [p[0], p[1], p[6, :wsb][None], p[6, wsb:][None], p[7, n_rb:n_rb + n_sk][None],
            p[7, :n_rb].reshape(REL_BUCKETS, -1), p[2][None], p[3][None], p[4][None], p[5][None]]


def kernel(x, ln_in_g, ln_in_b, w_in, sb_norm_g, swa_norm_g, sinks, rel_bias, w_out, ln1_g, ln1_b, w_gate_up, w_down, ln2_g, ln2_b, loss_target, m_ln_in_g, m_ln_in_b, m_w_in, m_sb_norm_g, m_swa_norm_g, m_sinks, m_rel_bias, m_w_out, m_ln1_g, m_ln1_b, m_w_gate_up, m_w_down, m_ln2_g, m_ln2_b, v_ln_in_g, v_ln_in_b, v_w_in, v_sb_norm_g, v_swa_norm_g, v_sinks, v_rel_bias, v_w_out, v_ln1_g, v_ln1_b, v_w_gate_up, v_w_down, v_ln2_g, v_ln2_b):
    x2 = x[0]
    tgt = loss_target[0]
    s, d = x2.shape
    wsb = sb_norm_g.shape[-1]
    wsw = swa_norm_g.shape[-1]
    n_sw_heads = sinks.shape[-1]
    n_pairs = wsb // LANES
    dff = w_down.shape[1] * N_DEV
    assert wsb % LANES == 0 and wsw % LANES == 0 and n_sw_heads * HEAD_DIM == wsw
    assert 3 * wsb % wsw == 0 and dff % LANES == 0 and s % SWA_BLOCK == 0
    qcol = 3 * wsb // wsw
    kcol = (3 * wsb + wsw) // LANES
    vcol = kcol + 1
    assert w_in.shape[-1] * N_DEV == (vcol + 1) * LANES

    t2 = lambda a: jnp.transpose(a[0])
    big_w = [t2(w_in), w_out[0], t2(w_gate_up), w_down[0]]
    big_m = [t2(m_w_in), m_w_out[0], t2(m_w_gate_up), m_w_down[0]]
    big_v = [t2(v_w_in), v_w_out[0], t2(v_w_gate_up), v_w_down[0]]

    cat_rows = lambda g: g.reshape(N_DEV * g.shape[1], g.shape[2])
    shards = [w.astype(BF16) for w in big_w]
    w_in_t = cat_rows(_exchange("w_in_allgather", shards[:1], ["gather"])[0])

    vec = lambda a: a.reshape(1, -1)
    g_in, b_in = vec(ln_in_g), vec(ln_in_b)
    bucket = jnp.asarray(_swa_bucket_table())

    h0b, proj = _ln_proj(x2, g_in, b_in, w_in_t)
    sb_out, sb_first, gathered = _sb_fwd(proj, n_pairs, comm=(shards[1:3], ["gather"] * 2))
    w_out_f, w_gu_t = cat_rows(gathered[0]), cat_rows(gathered[1])
    sw_out, sw_kept, gathered = _swa_fwd(proj, bucket, rel_bias, sinks, n_sw_heads, qcol, kcol, vcol,
                                comm=(shards[3:], ["gather"]))
    w_down_f = cat_rows(gathered[0])
    du1, merged, h1b, act, dgu, du2b, dsb, dsw, st_ffn = _mix_ffn(
        sb_out, sw_out, x2, g_in, b_in, sb_norm_g, swa_norm_g, w_out_f, ln1_g, ln1_b, w_gu_t, w_down_f, ln2_g, ln2_b, tgt)

    split_rows = lambda g: g.reshape(N_DEV, g.shape[0] // N_DEV, g.shape[1])
    gw_gu = _wgrad("wgrad_gate_up", [dgu], h1b, dff // 2, d)
    gw_down = _wgrad("wgrad_down", [act], du2b, dff // 2, d)
    gw_out = _wgrad("wgrad_out", [merged], du1, min(512, d), d)
    (dq_sb, dk_sb, dv_sb), (land_gu, land_out) = _sb_bwd(
        proj, dsb, sb_out, sb_first, n_pairs, comm=([split_rows(gw_gu), split_rows(gw_out)], ["scatter"] * 2))
    (dq_sw, dk_sw, dv_sw, st_sink, st_rb), (land_down,) = _swa_bwd(
        proj, dsw, sw_kept, bucket, n_sw_heads, qcol, kcol, vcol, comm=([split_rows(gw_down)], ["scatter"]))
    pieces = [dq_sb, dk_sb, dv_sb, dq_sw, dk_sw, dv_sw]
    gw_in = _wgrad("wgrad_in", pieces, h0b, proj.shape[1], d)
    (grad_x, st_in), (land_in,) = _proj_bwd(pieces, w_in_t, du1, x2, g_in, comm=([split_rows(gw_in)], ["scatter"]))

    n_rb = rel_bias.size
    small = _pack(d, st_in[0], st_in[1], st_ffn[3], st_ffn[4], st_ffn[0], st_ffn[1],
                  st_ffn[5, :wsb], st_ffn[5, wsb:], st_rb[:, :n_sw_heads], st_sink[:n_sw_heads, 0],
                  extra=st_ffn[2, 0:1])
    land_small = _exchange("small_grads_allgather", [small], ["gather"])[0]
    landed = [land_in, land_out, land_gu, land_down, land_small]

    big = []
    for name, land, w, m, v in zip(["adamw_in", "adamw_out", "adamw_gate_up", "adamw_down"], landed[:4], big_w, big_m, big_v):
        big.append(_adamw(name, land, w, m, v, _adamw_rows(w.shape[0])))

    small_w = _pack(d, ln_in_g, ln_in_b, ln1_g, ln1_b, ln2_g, ln2_b, sb_norm_g, swa_norm_g, rel_bias, sinks)
    small_m = _pack(d, m_ln_in_g, m_ln_in_b, m_ln1_g, m_ln1_b, m_ln2_g, m_ln2_b, m_sb_norm_g, m_swa_norm_g, m_rel_bias, m_sinks)
    small_v = _pack(d, v_ln_in_g, v_ln_in_b, v_ln1_g, v_ln1_b, v_ln2_g, v_ln2_b, v_sb_norm_g, v_swa_norm_g, v_rel_bias, v_sinks)
    sg, sd, sm, sv = _adamw("adamw_small", landed[4], small_w, small_m, small_v, 8)
    n_sk = sinks.size
    loss = sg[7, n_rb + n_sk]

    def leaves(idx):
        sm_l = _unpack([sg, sd, sm, sv][idx], wsb, n_rb, n_sk)
        bg = [jnp.transpose(big[0][idx])[None], big[1][idx][None], jnp.transpose(big[2][idx])[None], big[3][idx][None]]
        return [sm_l[0], sm_l[1], bg[0], sm_l[2], sm_l[3], sm_l[4], sm_l[5], bg[1], sm_l[6], sm_l[7], bg[2], bg[3], sm_l[8], sm_l[9]]

    return (loss, grad_x[None], *leaves(0), *leaves(1), *leaves(2), *leaves(3))
```

```python
import functools
import math

import numpy as np
import jax
import jax.numpy as jnp
from jax import lax
from jax.experimental import pallas as pl
from jax.experimental.pallas import tpu as pltpu

F32 = jnp.float32
BF16 = jnp.bfloat16
MESH = pl.DeviceIdType.MESH

N_DEV = 8
LANES = 128
HEAD_DIM = 64
SCALE = HEAD_DIM ** -0.5
SWA_BLOCK = 128
REL_BUCKETS = 32
REL_MAX_DIST = 128
ALPHA = 2.0 ** 0.25
LN_EPS = 1e-5
RMS_EPS = 1e-6
ADAM_LR = 0.001
ADAM_B1 = 0.9
ADAM_B2 = 0.999
ADAM_EPS = 1e-08
ADAM_WD = 0.01
ADAM_STEP = 10

ROW_TILE = 512
SB_TILE = 256
FFN_TILE = 256
WGRAD_TOKENS = 2048
SB_UNDERFLOW = -110.0
MIB = 1024 * 1024


def _params(vmem_mib=48):
    return pltpu.CompilerParams(vmem_limit_bytes=vmem_mib * MIB)


def _dot(a, b):
    return jnp.dot(a, b, preferred_element_type=F32)


def _dot_nt(a, b):
    return lax.dot_general(a, b, (((1,), (1,)), ((), ())), preferred_element_type=F32)


def _dot_tn(a, b):
    return lax.dot_general(a, b, (((0,), (0,)), ((), ())), preferred_element_type=F32)


def _ln_hat(x):
    mu = jnp.mean(x, axis=-1, keepdims=True)
    xc = x - mu
    var = jnp.mean(xc * xc, axis=-1, keepdims=True)
    r = lax.rsqrt(var + LN_EPS)
    return xc * r, r


def _ln_bwd(dxhat, xhat, r):
    return r * (dxhat - jnp.mean(dxhat, axis=-1, keepdims=True)
                - xhat * jnp.mean(dxhat * xhat, axis=-1, keepdims=True))


def _colsum(a):
    return jnp.sum(a, axis=0, keepdims=True)


def _rowsum(a):
    return jnp.sum(a, axis=1, keepdims=True)


def _full(shape):
    return pl.BlockSpec(shape, lambda *_: (0,) * len(shape))


def _comm_out_shapes(arrays, kinds):
    shapes = []
    for a, kind in zip(arrays, kinds):
        blk = a.shape if kind == "gather" else a.shape[1:]
        shapes.append(jax.ShapeDtypeStruct((N_DEV,) + tuple(blk), a.dtype))
    return shapes


def _comm_sems(n):
    return [pltpu.SemaphoreType.DMA((n, N_DEV - 1)), pltpu.SemaphoreType.DMA((n, N_DEV - 1)),
            pltpu.SemaphoreType.DMA((n,))]


def _comm_copies(ins, outs, kinds, send_sems, recv_sems, local_sems):
    x, y, c = lax.axis_index("x"), lax.axis_index("y"), lax.axis_index("c")
    me = 4 * x + 2 * y + c

    def src_for(t, dev_lin):
        return ins[t] if kinds[t] == "gather" else ins[t].at[dev_lin]

    local = [pltpu.make_async_copy(src_for(t, me), outs[t].at[me], local_sems.at[t]) for t in range(len(kinds))]
    sends, arrivals = [], []
    for k in range(1, N_DEV):
        px = 1 - x if (k >> 2) & 1 else x
        py = 1 - y if (k >> 1) & 1 else y
        pc = 1 - c if k & 1 else c
        peer_lin = 4 * px + 2 * py + pc
        for t in range(len(kinds)):
            sems = dict(send_sem=send_sems.at[t, k - 1], recv_sem=recv_sems.at[t, k - 1],
                        device_id=(px, py, pc), device_id_type=MESH)
            sends.append(pltpu.make_async_remote_copy(src_ref=src_for(t, peer_lin), dst_ref=outs[t].at[me], **sems))
            arrivals.append(pltpu.make_async_remote_copy(src_ref=src_for(t, peer_lin), dst_ref=outs[t].at[peer_lin], **sems))
    return local, sends, arrivals


def _comm_start(ins, outs, kinds, sems):
    local, sends, _ = _comm_copies(ins, outs, kinds, *sems)
    for cp in local + sends:
        cp.start()


def _comm_finish(ins, outs, kinds, sems):
    local, sends, arrivals = _comm_copies(ins, outs, kinds, *sems)
    for cp in arrivals:
        cp.wait_recv()
    for cp in sends:
        cp.wait_send()
    for cp in local:
        cp.wait()


def _exchange(name, arrays, kinds):
    n = len(arrays)

    def body(*refs):
        ins, outs, sems = refs[:n], refs[n:2 * n], refs[2 * n:]
        _comm_start(ins, outs, kinds, sems)
        _comm_finish(ins, outs, kinds, sems)

    any_spec = pl.BlockSpec(memory_space=pl.ANY)
    return pl.pallas_call(
        body, name=name, out_shape=_comm_out_shapes(arrays, kinds),
        in_specs=[any_spec] * n, out_specs=[any_spec] * n,
        scratch_shapes=_comm_sems(n),
    )(*arrays)


def _allgather_via_sibling(name, shard):
    def body(x_ref, out_ref, send_sems, recv_sems, local_sem):
        x, y, c = lax.axis_index("x"), lax.axis_index("y"), lax.axis_index("c")
        me, sibling = (x, y, c), (x, y, 1 - c)
        chips = [(1 - x, y), (x, 1 - y), (1 - x, 1 - y)]

        def copy(k, block, to, src=None):
            slot = out_ref.at[4 * block[0] + 2 * block[1] + block[2]]
            return pltpu.make_async_remote_copy(
                src_ref=slot if src is None else src, dst_ref=slot,
                send_sem=send_sems.at[k], recv_sem=recv_sems.at[k], device_id=to, device_id_type=MESH)

        mine = pltpu.make_async_copy(x_ref, out_ref.at[4 * x + 2 * y + c], local_sem)
        mine.start()
        first = [copy(0, me, sibling, src=x_ref)]
        first += [copy(1 + j, me, (*chip, c), src=x_ref) for j, chip in enumerate(chips)]
        for cp in first:
            cp.start()
        passed = [copy(4 + j, (*chip, c), sibling) for j, chip in enumerate(chips)]
        for j, chip in enumerate(chips):
            copy(1 + j, (*chip, c), me).wait_recv()
            passed[j].start()
        copy(0, sibling, me).wait_recv()
        for j, chip in enumerate(chips):
            copy(4 + j, (*chip, 1 - c), me).wait_recv()
        for cp in first + passed:
            cp.wait_send()
        mine.wait()

    any_spec = pl.BlockSpec(memory_space=pl.ANY)
    return pl.pallas_call(
        body, name=name, out_shape=jax.ShapeDtypeStruct((N_DEV,) + shard.shape, shard.dtype),
        in_specs=[any_spec], out_specs=any_spec,
        scratch_shapes=[pltpu.SemaphoreType.DMA((N_DEV - 1,)), pltpu.SemaphoreType.DMA((N_DEV - 1,)),
                        pltpu.SemaphoreType.DMA],
    )(shard)


def _call(body, name, grid, in_specs, out_specs, out_shape, args, scratch_shapes=(), comm=None):
    if comm is None:
        outs = pl.pallas_call(body, name=name, grid=grid, in_specs=in_specs, out_specs=out_specs,
                              out_shape=out_shape, scratch_shapes=list(scratch_shapes),
                              compiler_params=_params())(*args)
        return outs, []
    arrays, kinds = comm
    n, n_in, n_out, n_scr = len(arrays), len(in_specs), len(out_specs), len(scratch_shapes)

    def fused(*refs):
        c_in, x_in = refs[:n_in], refs[n_in:n_in + n]
        c_out = refs[n_in + n:n_in + n + n_out]
        x_out = refs[n_in + n + n_out:n_in + 2 * n + n_out]
        rest = refs[n_in + 2 * n + n_out:]
        c_scr, sems = rest[:n_scr], rest[n_scr:]
        ids = [pl.program_id(a) for a in range(len(grid))]
        is_first = functools.reduce(jnp.logical_and, [i == 0 for i in ids])
        is_last = functools.reduce(jnp.logical_and, [i == g - 1 for i, g in zip(ids, grid)])

        @pl.when(is_first)
        def _():
            _comm_start(x_in, x_out, kinds, sems)

        body(*c_in, *c_out, *c_scr)

        @pl.when(is_last)
        def _():
            _comm_finish(x_in, x_out, kinds, sems)

    any_spec = pl.BlockSpec(memory_space=pl.ANY)
    outs = pl.pallas_call(
        fused, name=name, grid=grid,
        in_specs=list(in_specs) + [any_spec] * n, out_specs=list(out_specs) + [any_spec] * n,
        out_shape=list(out_shape) + _comm_out_shapes(arrays, kinds),
        scratch_shapes=list(scratch_shapes) + _comm_sems(n),
        compiler_params=_params())(*args, *arrays)
    return outs[:n_out], outs[n_out:]


def _ln_proj(x, g, b, w_in_t):
    s, d = x.shape
    cols = w_in_t.shape[0]
    tm = min(ROW_TILE, s)

    def body(x_ref, g_ref, b_ref, w_ref, h_ref, p_ref):
        xhat, _ = _ln_hat(x_ref[...])
        h = (xhat * g_ref[...] + b_ref[...]).astype(BF16)
        h_ref[...] = h
        p_ref[...] = _dot_nt(h, w_ref[...]).astype(BF16)

    row = lambda width: pl.BlockSpec((tm, width), lambda i: (i, 0))
    return pl.pallas_call(
        body, name="ln_proj", grid=(s // tm,),
        in_specs=[row(d), _full((1, d)), _full((1, d)), _full((cols, d))],
        out_specs=[row(d), row(cols)],
        out_shape=[jax.ShapeDtypeStruct((s, d), BF16), jax.ShapeDtypeStruct((s, cols), BF16)],
        compiler_params=_params(),
    )(x, g, b, w_in_t)


def _sb_triangles(t):
    row = lax.broadcasted_iota(jnp.int32, (t, t), 0)
    col = lax.broadcasted_iota(jnp.int32, (t, t), 1)
    return (row > col).astype(BF16), (row >= col).astype(BF16)


def _sb_first_mask(t, has_prev):
    qrow = lax.broadcasted_iota(jnp.int32, (2 * t, 2 * t), 0) & (t - 1)
    col = lax.broadcasted_iota(jnp.int32, (2 * t, 2 * t), 1)
    return ((col < t) & has_prev) | ((col >= t) & (col - t < qrow))


def _sb_stack_heads(x2, first):
    zero = jnp.zeros_like(x2)
    return jnp.concatenate([jnp.where(first, x2, zero), jnp.where(first, zero, x2)], axis=0)


def _sb_key_tiles(ref, offs, t):
    tiles = [ref[pl.ds(off, t), :] for off in offs]
    return tiles[0] if len(tiles) == 1 else jnp.concatenate(tiles, axis=0)


def _sb_suffix(terms, row_sums, tri, carry):
    out = [None] * len(terms)
    for j in reversed(range(len(terms))):
        suf = carry
        for op in terms[j]:
            suf = suf + _dot(op, tri)
        out[j] = suf
        carry = carry + row_sums[j]
    return (out[0] if len(out) == 1 else jnp.concatenate(out, axis=1)), carry


def _sb_scores(qh, k_t, upper, carry_l, mask, t):
    z = _dot_nt(qh, k_t)
    sp = jnp.log(1.0 + jnp.exp(-jnp.abs(z)))
    neg = jnp.minimum(z, 0.0)
    lb = neg - sp
    l1 = (neg - z) - sp
    if mask is not None:
        l1 = jnp.where(mask, l1, 0.0)
    hi = l1.astype(BF16)
    lo = (l1 - hi.astype(F32)).astype(BF16)
    cols = [slice(j * t, (j + 1) * t) for j in range(z.shape[1] // t)]
    suf, carry_l = _sb_suffix([[hi[:, c], lo[:, c]] for c in cols], [_rowsum(l1[:, c]) for c in cols], upper, carry_l)
    a = jnp.exp(lb + suf)
    if mask is not None:
        a = jnp.where(mask, a, 0.0)
    return lb, a, carry_l


def _sb_walk(i, t, first_visit, visit, init):
    def alive(carry):
        return jnp.max(carry[0]) > SB_UNDERFLOW

    prev = pl.multiple_of(jnp.maximum(i - 1, 0) * t, t)
    carry = first_visit((prev, pl.multiple_of(i * t, t)), init)

    def cond(state):
        j, go, _ = state
        return (j < i - 1) & go

    def body(state):
        j, _, carry = state
        carry = visit((pl.multiple_of((i - 2 - j) * t, t),), carry)
        return j + 1, alive(carry), carry

    return lax.while_loop(cond, body, (jnp.int32(0), alive(carry), carry))[2]


def _sb_first_specs(n_pairs, nq, t):
    at = lambda h, i: (h * nq + i, 0, 0)
    specs = [pl.BlockSpec((None, 2 * t, 2 * t), at), pl.BlockSpec((None, 2 * t, 2 * t), at), pl.BlockSpec((None, 2 * t, 1), at)]
    shapes = [jax.ShapeDtypeStruct((n_pairs * nq, 2 * t, 2 * t), BF16), jax.ShapeDtypeStruct((n_pairs * nq, 2 * t, 2 * t), F32),
              jax.ShapeDtypeStruct((n_pairs * nq, 2 * t, 1), F32)]
    return specs, shapes


def _sb_fwd(proj, n_pairs, comm=None):
    s = proj.shape[0]
    t = min(SB_TILE, s)
    nq = s // t

    def body(q_ref, k_ref, v_ref, o_ref, ab_ref, beta_ref, cl_ref):
        i = pl.program_id(1)
        lane = lax.broadcasted_iota(jnp.int32, (1, LANES), 1)
        first = lane < HEAD_DIM
        qs = _sb_stack_heads(q_ref[...] * SCALE, first)
        upper, _ = _sb_triangles(t)

        def first_visit(offs, carry):
            c_l, acc = carry
            mask = _sb_first_mask(t, i > 0)
            lb, a, c_l = _sb_scores(qs, _sb_key_tiles(k_ref, offs, t), upper, c_l, mask, t)
            a_b = a.astype(BF16)
            ab_ref[...] = a_b
            beta_ref[...] = jnp.where(mask, jnp.exp(lb), 0.0)
            cl_ref[...] = c_l
            return c_l, acc + _dot(a_b, _sb_key_tiles(v_ref, offs, t))

        def visit(offs, carry):
            c_l, acc = carry
            _, a, c_l = _sb_scores(qs, _sb_key_tiles(k_ref, offs, t), upper, c_l, None, t)
            return c_l, acc + _dot(a.astype(BF16), _sb_key_tiles(v_ref, offs, t))

        init = (jnp.zeros((2 * t, 1), F32), jnp.zeros((2 * t, LANES), F32))
        _, acc = _sb_walk(i, t, first_visit, visit, init)
        o_ref[...] = jnp.where(first, acc[:t], acc[t:])

    first_specs, first_shapes = _sb_first_specs(n_pairs, nq, t)
    outs, landed = _call(
        body, "sb_fwd", (n_pairs, nq),
        in_specs=[pl.BlockSpec((t, LANES), lambda h, i: (i, h)),
                  pl.BlockSpec((s, LANES), lambda h, i: (0, n_pairs + h)),
                  pl.BlockSpec((s, LANES), lambda h, i: (0, 2 * n_pairs + h))],
        out_specs=[pl.BlockSpec((t, LANES), lambda h, i: (i, h))] + first_specs,
        out_shape=[jax.ShapeDtypeStruct((s, n_pairs * LANES), F32)] + first_shapes,
        args=(proj, proj, proj), comm=comm)
    return outs[0], outs[1:], landed


def _swa_bucket_table():
    qi = np.arange(SWA_BLOCK)[:, None]
    cj = np.arange(2 * SWA_BLOCK)[None, :]
    dist = qi + SWA_BLOCK - cj
    exact = REL_BUCKETS // 2
    d = np.maximum(dist, 0)
    d_f = np.maximum(d, 1).astype(np.float32)
    large = exact + (np.log(d_f / np.float32(exact)) / np.float32(math.log(REL_MAX_DIST / exact))
                     * np.float32(REL_BUCKETS - exact)).astype(np.int32)
    large = np.minimum(large, REL_BUCKETS - 1)
    return np.where(d < exact, d, large).astype(np.int32)


def _swa_build_bias(bucket_ref, rb_ref, bias_ref, n_groups, per_group):
    bk = bucket_ref[...]
    dist = (lax.broadcasted_iota(jnp.int32, bk.shape, 0) + SWA_BLOCK) - lax.broadcasted_iota(jnp.int32, bk.shape, 1)
    window = (dist >= 0) & (dist < SWA_BLOCK)
    for g in range(n_groups):
        for hh in range(per_group):
            acc = jnp.zeros(bk.shape, F32)
            for b in range(REL_BUCKETS):
                acc = jnp.where(bk == b, rb_ref[b, g * per_group + hh], acc)
            bias_ref[g, hh * SWA_BLOCK:(hh + 1) * SWA_BLOCK, :] = jnp.where(window, acc, -jnp.inf)


def _swa_first_block_mask(i):
    col = lax.broadcasted_iota(jnp.int32, (1, 2 * SWA_BLOCK), 1)
    return jnp.where((col < SWA_BLOCK) & (i == 0), -jnp.inf, 0.0)


def _swa_place(blk, h, group, sel):
    if (h % 2) != group:
        blk = pltpu.roll(blk.astype(F32), HEAD_DIM, axis=1).astype(BF16)
    return jnp.where(sel, blk, jnp.zeros_like(blk))


def _swa_stack(ref, group, per_group, sel, scale=1.0):
    parts = []
    for hh in range(per_group):
        h = group * per_group + hh
        parts.append(_swa_place(ref[:, (h // 2) * LANES:(h // 2 + 1) * LANES], h, group, sel))
    stacked = jnp.concatenate(parts, axis=0)
    return stacked if scale == 1.0 else stacked * scale


def _swa_unstack(stacked, group, per_group, pieces):
    for hh in range(per_group):
        h = group * per_group + hh
        piece = stacked[hh * SWA_BLOCK:(hh + 1) * SWA_BLOCK, :]
        pieces[h] = pltpu.roll(piece, HEAD_DIM, axis=1) if (h % 2) != group else piece


def _swa_sink_rows(sk_ref, group, per_group):
    rowh = lax.broadcasted_iota(jnp.int32, (per_group * SWA_BLOCK, 1), 0) // SWA_BLOCK
    sink = jnp.zeros((per_group * SWA_BLOCK, 1), F32) + sk_ref[0, group * per_group]
    for hh in range(1, per_group):
        sink = jnp.where(rowh == hh, sk_ref[0, group * per_group + hh], sink)
    return sink


def _swa_probs(q_pos, kcat, bias_h, first_mask, sink):
    logits = _dot_nt(q_pos, kcat) + (bias_h + first_mask)
    m = jnp.maximum(jnp.max(logits, axis=1, keepdims=True), sink)
    p = jnp.exp(logits - m)
    es = jnp.exp(sink - m)
    inv = 1.0 / (_rowsum(p) + es)
    return p * inv, es * inv


def _swa_specs(n_heads, qcol, kcol, vcol):
    width = n_heads * HEAD_DIM
    prev = lambda col: pl.BlockSpec((SWA_BLOCK, LANES), lambda i: (jnp.maximum(i - 1, 0), col))
    cur = lambda col: pl.BlockSpec((SWA_BLOCK, LANES), lambda i: (i, col))
    return [pl.BlockSpec((SWA_BLOCK, width), lambda i: (i, qcol)),
            prev(kcol), cur(kcol), prev(vcol), cur(vcol),
            _full((SWA_BLOCK, 2 * SWA_BLOCK)),
            pl.BlockSpec(memory_space=pltpu.SMEM), pl.BlockSpec(memory_space=pltpu.SMEM)]


def _swa_kept_specs(nb, n_groups, per_group):
    rows = per_group * SWA_BLOCK
    at = lambda i: (i, 0, 0, 0)
    specs = [pl.BlockSpec((None, n_groups, rows, 2 * SWA_BLOCK), at), pl.BlockSpec((None, n_groups, rows, 1), at)]
    shapes = [jax.ShapeDtypeStruct((nb, n_groups, rows, 2 * SWA_BLOCK), F32), jax.ShapeDtypeStruct((nb, n_groups, rows, 1), F32)]
    return specs, shapes


def _swa_fwd(proj, bucket, rel_bias, sinks, n_heads, qcol, kcol, vcol, comm=None):
    s = proj.shape[0]
    width = n_heads * HEAD_DIM
    n_groups = LANES // HEAD_DIM
    per_group = n_heads // n_groups

    def body(q_ref, kp_ref, kc_ref, vp_ref, vc_ref, bucket_ref, rb_ref, sk_ref, o_ref, prob_ref, psink_ref, bias_ref):
        i = pl.program_id(0)

        @pl.when(i == 0)
        def _():
            _swa_build_bias(bucket_ref, rb_ref, bias_ref, n_groups, per_group)

        lane = lax.broadcasted_iota(jnp.int32, (1, LANES), 1)
        first = lane < HEAD_DIM
        first_mask = _swa_first_block_mask(i)
        kcat = jnp.concatenate([kp_ref[...], kc_ref[...]], axis=0)
        vcat = jnp.concatenate([vp_ref[...], vc_ref[...]], axis=0)
        pieces = {}
        for g in range(n_groups):
            sel = first if g == 0 else jnp.logical_not(first)
            prob, p_sink = _swa_probs(_swa_stack(q_ref, g, per_group, sel, SCALE), kcat, bias_ref[g], first_mask,
                                      _swa_sink_rows(sk_ref, g, per_group))
            prob_ref[g] = prob
            psink_ref[g] = p_sink
            _swa_unstack(_dot(prob.astype(BF16), vcat), g, per_group, pieces)
        for j in range(n_heads // 2):
            o_ref[:, j * LANES:(j + 1) * LANES] = jnp.where(first, pieces[2 * j], pieces[2 * j + 1])

    kept_specs, kept_shapes = _swa_kept_specs(s // SWA_BLOCK, n_groups, per_group)
    outs, landed = _call(
        body, "swa_fwd", (s // SWA_BLOCK,),
        in_specs=_swa_specs(n_heads, qcol, kcol, vcol),
        out_specs=[pl.BlockSpec((SWA_BLOCK, width), lambda i: (i, 0))] + kept_specs,
        out_shape=[jax.ShapeDtypeStruct((s, width), F32)] + kept_shapes,
        args=(proj, proj, proj, proj, proj, bucket, rel_bias, sinks),
        scratch_shapes=[pltpu.VMEM((n_groups, per_group * SWA_BLOCK, 2 * SWA_BLOCK), F32)], comm=comm)
    return outs[0], outs[1:], landed


def _rms_fwd(o, g):
    r = lax.rsqrt(jnp.mean(o * o, axis=-1, keepdims=True) + RMS_EPS)
    n = o * r
    return n, r, n * g


def _mix_ffn(sb_out, sw_out, x, g_in, b_in, sb_g, sw_g, w_out, g1, b1, w_gu_t, w_down, g2, b2, target):
    s, d = x.shape
    wsb, wsw = sb_out.shape[1], sw_out.shape[1]
    dff = w_down.shape[0]
    assert wsb + wsw == d
    tm = min(FFN_TILE, s)

    def body(sb_ref, sw_ref, x_ref, gi_ref, bi_ref, sbg_ref, swg_ref, wo_hbm, g1_ref, b1_ref, wgu_hbm, wd_hbm,
             g2_ref, b2_ref, t_ref,
             du1_ref, mg_ref, h1b_ref, act_ref, dgu_ref, du2b_ref, dsb_ref, dsw_ref, st_ref,
             wo_ref, wgu_ref, wd_ref):
        @pl.when(pl.program_id(0) == 0)
        def _():
            pltpu.sync_copy(wo_hbm, wo_ref)
            pltpu.sync_copy(wgu_hbm, wgu_ref)
            pltpu.sync_copy(wd_hbm, wd_ref)
            st_ref[...] = jnp.zeros_like(st_ref)

        sb, sw = sb_ref[...], sw_ref[...]
        _, _, m_sb = _rms_fwd(sb, sbg_ref[...])
        _, _, m_sw = _rms_fwd(sw, swg_ref[...])
        m_sb = m_sb.astype(BF16)
        m_sw = m_sw.astype(BF16)
        mg_ref[:, :wsb] = m_sb
        mg_ref[:, wsb:] = m_sw
        xhat0, _ = _ln_hat(x_ref[...])
        u1 = ALPHA * (xhat0 * gi_ref[...] + bi_ref[...]) + _dot(m_sb, wo_ref[:wsb, :]) + _dot(m_sw, wo_ref[wsb:, :])

        xhat1, r1 = _ln_hat(u1)
        h1 = xhat1 * g1_ref[...] + b1_ref[...]
        h1b = h1.astype(BF16)
        h1b_ref[...] = h1b
        gate = _dot_nt(h1b, wgu_ref[:dff, :])
        up = _dot_nt(h1b, wgu_ref[dff:, :])
        sg = jax.nn.sigmoid(gate)
        silu = gate * sg
        act = (silu * up).astype(BF16)
        act_ref[...] = act
        u2 = ALPHA * h1 + _dot(act, wd_ref[...])
        xhat2, r2 = _ln_hat(u2)
        diff = xhat2 * g2_ref[...] + b2_ref[...] - t_ref[...]
        dh2 = diff * (1.0 / d)
        st_ref[0:1, :] += _colsum(dh2 * xhat2)
        st_ref[1:2, :] += _colsum(dh2)
        st_ref[2:3, :] += jnp.broadcast_to(_colsum(_rowsum(diff * diff)) * (0.5 / d), (1, d))
        du2 = _ln_bwd(dh2 * g2_ref[...], xhat2, r2)
        du2b = du2.astype(BF16)
        du2b_ref[...] = du2b
        dact = _dot_nt(du2b, wd_ref[...])
        dgate = (dact * up * (sg * (1.0 + gate * (1.0 - sg)))).astype(BF16)
        dup = (dact * silu).astype(BF16)
        dgu_ref[:, :dff] = dgate
        dgu_ref[:, dff:] = dup
        dh1 = _dot(dgate, wgu_ref[:dff, :]) + _dot(dup, wgu_ref[dff:, :]) + ALPHA * du2
        st_ref[3:4, :] += _colsum(dh1 * xhat1)
        st_ref[4:5, :] += _colsum(dh1)
        du1 = _ln_bwd(dh1 * g1_ref[...], xhat1, r1)
        du1_ref[...] = du1

        dmerged = _dot_nt(du1.astype(BF16), wo_ref[...])
        dsb, gsb = _rms_bwd(dmerged[:, :wsb], sb, sbg_ref[...])
        dsw, gsw = _rms_bwd(dmerged[:, wsb:], sw, swg_ref[...])
        dsb_ref[...] = dsb.astype(BF16)
        dsw_ref[...] = dsw.astype(BF16)
        st_ref[5:6, :wsb] += gsb
        st_ref[5:6, wsb:] += gsw

    row = lambda width: pl.BlockSpec((tm, width), lambda i: (i, 0))
    vec = lambda width: _full((1, width))
    hbm = pl.BlockSpec(memory_space=pl.ANY)
    bf = lambda width: jax.ShapeDtypeStruct((s, width), BF16)
    return pl.pallas_call(
        body, name="mix_ffn", grid=(s // tm,),
        in_specs=[row(wsb), row(wsw), row(d), vec(d), vec(d), vec(wsb), vec(wsw), hbm, vec(d), vec(d), hbm, hbm,
                  vec(d), vec(d), row(d)],
        out_specs=[row(d), row(d), row(d), row(dff), row(2 * dff), row(d), row(wsb), row(wsw), _full((8, d))],
        out_shape=[jax.ShapeDtypeStruct((s, d), F32), bf(d), bf(d), bf(dff), bf(2 * dff), bf(d), bf(wsb), bf(wsw),
                   jax.ShapeDtypeStruct((8, d), F32)],
        scratch_shapes=[pltpu.VMEM(w_out.shape, BF16), pltpu.VMEM(w_gu_t.shape, BF16), pltpu.VMEM(w_down.shape, BF16)],
        compiler_params=_params(60),
    )(sb_out, sw_out, x, g_in, b_in, sb_g, sw_g, w_out, g1, b1, w_gu_t, w_down, g2, b2, target)


def _rms_bwd(dm, o, g):
    n, r, _ = _rms_fwd(o, g)
    dn = dm * g
    return r * (dn - n * jnp.mean(dn * n, axis=-1, keepdims=True)), _colsum(dm * n)


def _sb_bwd(proj, dout, out, first, n_pairs, comm=None):
    s = proj.shape[0]
    t = min(SB_TILE, s)
    nq = s // t
    width = n_pairs * LANES

    def body(q_ref, k_ref, v_ref, do_ref, o_ref, ab_ref, beta_ref, cl_ref, dq_ref, dk_out, dv_out, dk_ref, dv_ref):
        i = pl.program_id(1)

        @pl.when(i == 0)
        def _():
            dk_ref[...] = jnp.zeros_like(dk_ref)
            dv_ref[...] = jnp.zeros_like(dv_ref)

        lane = lax.broadcasted_iota(jnp.int32, (1, LANES), 1)
        first_lanes = lane < HEAD_DIM
        do2 = do_ref[...]
        qs = _sb_stack_heads(q_ref[...] * SCALE, first_lanes)
        dos = _sb_stack_heads(do2, first_lanes)
        prod = do2.astype(F32) * o_ref[...]
        totals = jnp.concatenate([_rowsum(jnp.where(first_lanes, prod, 0.0)), _rowsum(jnp.where(first_lanes, 0.0, prod))], axis=0)
        upper, incl = _sb_triangles(t)

        def grads(offs, k_t, v_t, a_b, beta, c_e, dq):
            d_e = _dot_nt(dos, v_t) * a_b.astype(F32)
            d_eb = d_e.astype(BF16)
            cols = [slice(j * t, (j + 1) * t) for j in range(len(offs))]
            suf_e, c_e = _sb_suffix([[d_eb[:, c]] for c in cols], [_rowsum(d_e[:, c]) for c in cols], incl, c_e)
            dzb = (d_e - beta * (d_e + (totals - suf_e))).astype(BF16)
            dk_t = _dot_tn(dzb, qs)
            dv_t = _dot_tn(a_b, dos)
            for off, c in zip(offs, cols):
                dk_ref[pl.ds(off, t), :] += dk_t[c, :]
                dv_ref[pl.ds(off, t), :] += dv_t[c, :]
            return c_e, dq + _dot(dzb, k_t)

        def first_visit(offs, carry):
            _, c_e, dq = carry
            k_t = _sb_key_tiles(k_ref, offs, t)
            v_t = _sb_key_tiles(v_ref, offs, t)
            c_e, dq = grads(offs, k_t, v_t, ab_ref[...], beta_ref[...], c_e, dq)
            return cl_ref[...], c_e, dq

        def visit(offs, carry):
            c_l, c_e, dq = carry
            k_t = _sb_key_tiles(k_ref, offs, t)
            v_t = _sb_key_tiles(v_ref, offs, t)
            lb, a, c_l = _sb_scores(qs, k_t, upper, c_l, None, t)
            c_e, dq = grads(offs, k_t, v_t, a.astype(BF16), jnp.exp(lb), c_e, dq)
            return c_l, c_e, dq

        init = (jnp.zeros((2 * t, 1), F32), jnp.zeros((2 * t, 1), F32), jnp.zeros((2 * t, LANES), F32))
        _, _, dq = _sb_walk(i, t, first_visit, visit, init)
        dq_ref[...] = (jnp.where(first_lanes, dq[:t], dq[t:]) * SCALE).astype(BF16)

        @pl.when(i == nq - 1)
        def _():
            dk_out[...] = dk_ref[...].astype(BF16)
            dv_out[...] = dv_ref[...].astype(BF16)

    qblk = pl.BlockSpec((t, LANES), lambda h, i: (i, h))
    whole = pl.BlockSpec((s, LANES), lambda h, i: (0, h))
    first_specs, _ = _sb_first_specs(n_pairs, nq, t)
    return _call(
        body, "sb_bwd", (n_pairs, nq),
        in_specs=[qblk,
                  pl.BlockSpec((s, LANES), lambda h, i: (0, n_pairs + h)),
                  pl.BlockSpec((s, LANES), lambda h, i: (0, 2 * n_pairs + h)),
                  qblk, qblk] + first_specs,
        out_specs=[qblk, whole, whole],
        out_shape=[jax.ShapeDtypeStruct((s, width), BF16)] * 3,
        args=(proj, proj, proj, dout, out, *first),
        scratch_shapes=[pltpu.VMEM((s, LANES), F32), pltpu.VMEM((s, LANES), F32)], comm=comm)


def _swa_bwd(proj, dout, kept, bucket, n_heads, qcol, kcol, vcol, comm=None):
    s = proj.shape[0]
    width = n_heads * HEAD_DIM
    n_groups = LANES // HEAD_DIM
    per_group = n_heads // n_groups
    nb = s // SWA_BLOCK

    def body(q_ref, kp_ref, kc_ref, vp_ref, vc_ref, bucket_ref, do_ref, prob_ref, psink_ref,
             dq_ref, dk_out, dv_out, dsk_ref, drb_ref, dbias_ref, dk_ref, dv_ref):
        i = pl.program_id(0)

        @pl.when(i == 0)
        def _():
            dbias_ref[...] = jnp.zeros_like(dbias_ref)
            dk_ref[...] = jnp.zeros_like(dk_ref)
            dv_ref[...] = jnp.zeros_like(dv_ref)
            dsk_ref[...] = jnp.zeros_like(dsk_ref)

        lane = lax.broadcasted_iota(jnp.int32, (1, LANES), 1)
        first = lane < HEAD_DIM
        kcat = jnp.concatenate([kp_ref[...], kc_ref[...]], axis=0)
        vcat = jnp.concatenate([vp_ref[...], vc_ref[...]], axis=0)
        dkcat = jnp.zeros((2 * SWA_BLOCK, LANES), F32)
        dvcat = jnp.zeros((2 * SWA_BLOCK, LANES), F32)
        pieces = {}
        for g in range(n_groups):
            sel = first if g == 0 else jnp.logical_not(first)
            q_g = _swa_stack(q_ref, g, per_group, sel, SCALE)
            do_g = _swa_stack(do_ref, g, per_group, sel)
            prob, p_sink = prob_ref[g], psink_ref[g]
            dprob = _dot_nt(do_g, vcat)
            delta = _rowsum(prob * dprob)
            dlog = prob * (dprob - delta)
            sink_term = p_sink * delta
            for hh in range(per_group):
                h = g * per_group + hh
                tot = _colsum(sink_term[hh * SWA_BLOCK:(hh + 1) * SWA_BLOCK, :])
                dsk_ref[h:h + 1, :] += jnp.broadcast_to(-tot, (1, LANES))
            dbias_ref[g] += dlog
            dlb = dlog.astype(BF16)
            _swa_unstack(_dot(dlb, kcat) * SCALE, g, per_group, pieces)
            dkcat += _dot_tn(dlb, q_g)
            dvcat += _dot_tn(prob.astype(BF16), do_g)
        for j in range(n_heads // 2):
            dq_ref[:, j * LANES:(j + 1) * LANES] = jnp.where(first, pieces[2 * j], pieces[2 * j + 1]).astype(BF16)

        cur = pl.multiple_of(i * SWA_BLOCK, SWA_BLOCK)
        dk_ref[pl.ds(cur, SWA_BLOCK), :] += dkcat[SWA_BLOCK:, :]
        dv_ref[pl.ds(cur, SWA_BLOCK), :] += dvcat[SWA_BLOCK:, :]

        @pl.when(i > 0)
        def _():
            prv = pl.multiple_of((i - 1) * SWA_BLOCK, SWA_BLOCK)
            dk_ref[pl.ds(prv, SWA_BLOCK), :] += dkcat[:SWA_BLOCK, :]
            dv_ref[pl.ds(prv, SWA_BLOCK), :] += dvcat[:SWA_BLOCK, :]

        @pl.when(i == nb - 1)
        def _():
            bk = bucket_ref[...]
            rowi = lax.broadcasted_iota(jnp.int32, (REL_BUCKETS, LANES), 0)
            coli = lax.broadcasted_iota(jnp.int32, (REL_BUCKETS, LANES), 1)
            res = jnp.zeros((REL_BUCKETS, LANES), F32)
            for h in range(n_heads):
                g, hh = divmod(h, per_group)
                db = dbias_ref[g, hh * SWA_BLOCK:(hh + 1) * SWA_BLOCK, :]
                for b in range(REL_BUCKETS):
                    tot = _colsum(_rowsum(jnp.where(bk == b, db, 0.0)))
                    res = jnp.where((rowi == b) & (coli == h), tot, res)
            drb_ref[...] = res
            dk_out[...] = dk_ref[...].astype(BF16)
            dv_out[...] = dv_ref[...].astype(BF16)

    kept_specs, _ = _swa_kept_specs(nb, n_groups, per_group)
    in_specs = _swa_specs(n_heads, qcol, kcol, vcol)[:6] + [pl.BlockSpec((SWA_BLOCK, width), lambda i: (i, 0))] + kept_specs
    return _call(
        body, "swa_bwd", (nb,),
        in_specs=in_specs,
        out_specs=[pl.BlockSpec((SWA_BLOCK, width), lambda i: (i, 0)),
                   _full((s, LANES)), _full((s, LANES)), _full((8, LANES)), _full((REL_BUCKETS, LANES))],
        out_shape=[jax.ShapeDtypeStruct((s, width), BF16), jax.ShapeDtypeStruct((s, LANES), BF16),
                   jax.ShapeDtypeStruct((s, LANES), BF16), jax.ShapeDtypeStruct((8, LANES), F32),
                   jax.ShapeDtypeStruct((REL_BUCKETS, LANES), F32)],
        args=(proj, proj, proj, proj, proj, bucket, dout, *kept),
        scratch_shapes=[pltpu.VMEM((n_groups, per_group * SWA_BLOCK, 2 * SWA_BLOCK), F32),
                        pltpu.VMEM((s, LANES), F32), pltpu.VMEM((s, LANES), F32)],
        comm=comm)


def _proj_bwd(pieces, w_in_t, du1, x, g_in, comm=None):
    s, d = x.shape
    cols = w_in_t.shape[0]
    tm = min(ROW_TILE, s)
    n_p = len(pieces)

    def body(*refs):
        p_refs = refs[:n_p]
        w_ref, du_ref, x_ref, g_ref, dx_ref, st_ref = refs[n_p:]
        i = pl.program_id(0)

        @pl.when(i == 0)
        def _():
            st_ref[...] = jnp.zeros_like(st_ref)

        dproj = jnp.concatenate([p[...] for p in p_refs], axis=1)
        dh0 = _dot(dproj, w_ref[...]) + ALPHA * du_ref[...]
        xhat, r = _ln_hat(x_ref[...])
        st_ref[0:1, :] += _colsum(dh0 * xhat)
        st_ref[1:2, :] += _colsum(dh0)
        dx_ref[...] = _ln_bwd(dh0 * g_ref[...], xhat, r)

    row = lambda width: pl.BlockSpec((tm, width), lambda i: (i, 0))
    return _call(
        body, "proj_bwd", (s // tm,),
        in_specs=[row(p.shape[1]) for p in pieces] + [_full((cols, d)), row(d), row(d), _full((1, d))],
        out_specs=[row(d), _full((8, d))],
        out_shape=[jax.ShapeDtypeStruct((s, d), F32), jax.ShapeDtypeStruct((8, d), F32)],
        args=(*pieces, w_in_t, du1, x, g_in), comm=comm)


def _wgrad(name, pieces, b, tm, tn):
    s, n = b.shape
    m = sum(p.shape[1] for p in pieces)
    n_p = len(pieces)
    assert n_p == 1 or tm == m
    ts = min(WGRAD_TOKENS if b.dtype == BF16 and n_p == 1 else WGRAD_TOKENS // 2, s)
    n_k = s // ts

    def body(*refs):
        p_refs, b_ref, o_ref, acc_ref = refs[:n_p], refs[n_p], refs[n_p + 1], refs[n_p + 2]
        k = pl.program_id(2)

        @pl.when(k == 0)
        def _():
            acc_ref[...] = jnp.zeros_like(acc_ref)

        a = p_refs[0][...] if n_p == 1 else jnp.concatenate([p[...] for p in p_refs], axis=1)
        acc_ref[...] += _dot_tn(a, b_ref[...].astype(BF16))

        @pl.when(k == n_k - 1)
        def _():
            o_ref[...] = acc_ref[...].astype(BF16)

    piece_spec = lambda p: pl.BlockSpec((ts, tm if n_p == 1 else p.shape[1]), lambda i, j, k: (k, i))
    return pl.pallas_call(
        body, name=name, grid=(m // tm, n // tn, n_k),
        in_specs=[piece_spec(p) for p in pieces] + [pl.BlockSpec((ts, tn), lambda i, j, k: (k, j))],
        out_specs=pl.BlockSpec((tm, tn), lambda i, j, k: (i, j)),
        out_shape=jax.ShapeDtypeStruct((m, n), BF16),
        scratch_shapes=[pltpu.VMEM((tm, tn), F32)],
        compiler_params=_params(),
    )(*pieces, b)


def _adamw_math(w, g, m, v):
    m = ADAM_B1 * m + (1.0 - ADAM_B1) * g
    v = ADAM_B2 * v + (1.0 - ADAM_B2) * (g * g)
    m_hat = m / (1.0 - ADAM_B1 ** ADAM_STEP)
    v_hat = v / (1.0 - ADAM_B2 ** ADAM_STEP)
    delta = -ADAM_LR * (m_hat / (jnp.sqrt(v_hat) + ADAM_EPS) + ADAM_WD * w)
    return delta, m, v


def _adamw_rows(rows):
    return max(r for r in range(16, 257, 16) if rows % r == 0)


def _adamw(name, landed, w, m, v, tr):
    rows, cols = w.shape

    def body(l_ref, w_ref, m_ref, v_ref, g_ref, d_ref, nm_ref, nv_ref):
        g = l_ref[0].astype(F32)
        for src in range(1, N_DEV):
            g = g + l_ref[src].astype(F32)
        delta, nm, nv = _adamw_math(w_ref[...], g, m_ref[...], v_ref[...])
        g_ref[...] = g
        d_ref[...] = delta
        nm_ref[...] = nm
        nv_ref[...] = nv

    blk = pl.BlockSpec((tr, cols), lambda i: (i, 0))
    shape = jax.ShapeDtypeStruct((rows, cols), F32)
    return pl.pallas_call(
        body, name=name, grid=(rows // tr,),
        in_specs=[pl.BlockSpec((N_DEV, tr, cols), lambda i: (0, i, 0)), blk, blk, blk],
        out_specs=[blk, blk, blk, blk],
        out_shape=[shape, shape, shape, shape],
        compiler_params=_params(),
    )(landed, w, m, v)


def _pack(d, ln_in_g, ln_in_b, ln1_g, ln1_b, ln2_g, ln2_b, sb_g, sw_g, rel_bias, sinks, extra=None):
    tail = [rel_bias.reshape(-1), sinks.reshape(-1)]
    if extra is not None:
        tail.append(extra.reshape(-1))
    tail = jnp.concatenate(tail)
    tail = jnp.concatenate([tail, jnp.zeros((d - tail.shape[0],), F32)])
    rows = [ln_in_g.reshape(-1), ln_in_b.reshape(-1), ln1_g.reshape(-1), ln1_b.reshape(-1),
            ln2_g.reshape(-1), ln2_b.reshape(-1),
            jnp.concatenate([sb_g.reshape(-1), sw_g.reshape(-1)]), tail]
    return jnp.stack(rows)


def _unpack(p, wsb, n_rb, n_sk):
    return [p[0], p[1], p[6, :wsb][None], p[6, wsb:][None], p[7, n_rb:n_rb + n_sk][None],
            p[7, :n_rb].reshape(REL_BUCKETS, -1), p[2][None], p[3][None], p[4][None], p[5][None]]


def kernel(x, ln_in_g, ln_in_b, w_in, sb_norm_g, swa_norm_g, sinks, rel_bias, w_out, ln1_g, ln1_b, w_gate_up, w_down, ln2_g, ln2_b, loss_target, m_ln_in_g, m_ln_in_b, m_w_in, m_sb_norm_g, m_swa_norm_g, m_sinks, m_rel_bias, m_w_out, m_ln1_g, m_ln1_b, m_w_gate_up, m_w_down, m_ln2_g, m_ln2_b, v_ln_in_g, v_ln_in_b, v_w_in, v_sb_norm_g, v_swa_norm_g, v_sinks, v_rel_bias, v_w_out, v_ln1_g, v_ln1_b, v_w_gate_up, v_w_down, v_ln2_g, v_ln2_b):
    x2 = x[0]
    tgt = loss_target[0]
    s, d = x2.shape
    wsb = sb_norm_g.shape[-1]
    wsw = swa_norm_g.shape[-1]
    n_sw_heads = sinks.shape[-1]
    n_pairs = wsb // LANES
    dff = w_down.shape[1] * N_DEV
    assert wsb % LANES == 0 and wsw % LANES == 0 and n_sw_heads * HEAD_DIM == wsw
    assert 3 * wsb % wsw == 0 and dff % LANES == 0 and s % SWA_BLOCK == 0
    qcol = 3 * wsb // wsw
    kcol = (3 * wsb + wsw) // LANES
    vcol = kcol + 1
    assert w_in.shape[-1] * N_DEV == (vcol + 1) * LANES

    t2 = lambda a: jnp.transpose(a[0])
    big_w = [t2(w_in), w_out[0], t2(w_gate_up), w_down[0]]
    big_m = [t2(m_w_in), m_w_out[0], t2(m_w_gate_up), m_w_down[0]]
    big_v = [t2(v_w_in), v_w_out[0], t2(v_w_gate_up), v_w_down[0]]

    cat_rows = lambda g: g.reshape(N_DEV * g.shape[1], g.shape[2])
    shards = [w.astype(BF16) for w in big_w]
    w_in_t = cat_rows(_allgather_via_sibling("w_in_allgather", shards[0]))

    vec = lambda a: a.reshape(1, -1)
    g_in, b_in = vec(ln_in_g), vec(ln_in_b)
    bucket = jnp.asarray(_swa_bucket_table())

    h0b, proj = _ln_proj(x2, g_in, b_in, w_in_t)
    sb_out, sb_first, gathered = _sb_fwd(proj, n_pairs, comm=(shards[1:3], ["gather"] * 2))
    w_out_f, w_gu_t = cat_rows(gathered[0]), cat_rows(gathered[1])
    sw_out, sw_kept, gathered = _swa_fwd(proj, bucket, rel_bias, sinks, n_sw_heads, qcol, kcol, vcol,
                                comm=(shards[3:], ["gather"]))
    w_down_f = cat_rows(gathered[0])
    du1, merged, h1b, act, dgu, du2b, dsb, dsw, st_ffn = _mix_ffn(
        sb_out, sw_out, x2, g_in, b_in, sb_norm_g, swa_norm_g, w_out_f, ln1_g, ln1_b, w_gu_t, w_down_f, ln2_g, ln2_b, tgt)

    split_rows = lambda g: g.reshape(N_DEV, g.shape[0] // N_DEV, g.shape[1])
    gw_gu = _wgrad("wgrad_gate_up", [dgu], h1b, dff // 2, d)
    gw_down = _wgrad("wgrad_down", [act], du2b, dff // 2, d)
    gw_out = _wgrad("wgrad_out", [merged], du1, min(512, d), d)
    (dq_sb, dk_sb, dv_sb), (land_gu, land_out) = _sb_bwd(
        proj, dsb, sb_out, sb_first, n_pairs, comm=([split_rows(gw_gu), split_rows(gw_out)], ["scatter"] * 2))
    (dq_sw, dk_sw, dv_sw, st_sink, st_rb), (land_down,) = _swa_bwd(
        proj, dsw, sw_kept, bucket, n_sw_heads, qcol, kcol, vcol, comm=([split_rows(gw_down)], ["scatter"]))
    pieces = [dq_sb, dk_sb, dv_sb, dq_sw, dk_sw, dv_sw]
    gw_in = _wgrad("wgrad_in", pieces, h0b, proj.shape[1], d)
    (grad_x, st_in), (land_in,) = _proj_bwd(pieces, w_in_t, du1, x2, g_in, comm=([split_rows(gw_in)], ["scatter"]))

    n_rb = rel_bias.size
    small = _pack(d, st_in[0], st_in[1], st_ffn[3], st_ffn[4], st_ffn[0], st_ffn[1],
                  st_ffn[5, :wsb], st_ffn[5, wsb:], st_rb[:, :n_sw_heads], st_sink[:n_sw_heads, 0],
                  extra=st_ffn[2, 0:1])
    land_small = _exchange("small_grads_allgather", [small], ["gather"])[0]
    landed = [land_in, land_out, land_gu, land_down, land_small]

    big = []
    for name, land, w, m, v in zip(["adamw_in", "adamw_out", "adamw_gate_up", "adamw_down"], landed[:4], big_w, big_m, big_v):
        big.append(_adamw(name, land, w, m, v, _adamw_rows(w.shape[0])))

    small_w = _pack(d, ln_in_g, ln_in_b, ln1_g, ln1_b, ln2_g, ln2_b, sb_norm_g, swa_norm_g, rel_bias, sinks)
    small_m = _pack(d, m_ln_in_g, m_ln_in_b, m_ln1_g, m_ln1_b, m_ln2_g, m_ln2_b, m_sb_norm_g, m_swa_norm_g, m_rel_bias, m_sinks)
    small_v = _pack(d, v_ln_in_g, v_ln_in_b, v_ln1_g, v_ln1_b, v_ln2_g, v_ln2_b, v_sb_norm_g, v_swa_norm_g, v_rel_bias, v_sinks)
    sg, sd, sm, sv = _adamw("adamw_small", landed[4], small_w, small_m, small_v, 8)
    n_sk = sinks.size
    loss = sg[7, n_rb + n_sk]

    def leaves(idx):
        sm_l = _unpack([sg, sd, sm, sv][idx], wsb, n_rb, n_sk)
        bg = [jnp.transpose(big[0][idx])[None], big[1][idx][None], jnp.transpose(big[2][idx])[None], big[3][idx][None]]
        return [sm_l[0], sm_l[1], bg[0], sm_l[2], sm_l[3], sm_l[4], sm_l[5], bg[1], sm_l[6], sm_l[7], bg[2], bg[3], sm_l[8], sm_l[9]]

    return (loss, grad_x[None], *leaves(0), *leaves(1), *leaves(2), *leaves(3))
```

```python
import functools
import math

import numpy as np
import jax
import jax.numpy as jnp
from jax import lax
from jax.experimental import pallas as pl
from jax.experimental.pallas import tpu as pltpu

F32 = jnp.float32
BF16 = jnp.bfloat16
MESH = pl.DeviceIdType.MESH

N_DEV = 8
LANES = 128
HEAD_DIM = 64
SCALE = HEAD_DIM ** -0.5
SWA_BLOCK = 128
REL_BUCKETS = 32
REL_MAX_DIST = 128
ALPHA = 2.0 ** 0.25
LN_EPS = 1e-5
RMS_EPS = 1e-6
ADAM_LR = 0.001
ADAM_B1 = 0.9
ADAM_B2 = 0.999
ADAM_EPS = 1e-08
ADAM_WD = 0.01
ADAM_STEP = 10

ROW_TILE = 512
SB_TILE = 256
SB_TILES_PER_STEP = 4
FFN_TILE = 256
WGRAD_TOKENS = 2048
SB_UNDERFLOW = -110.0
MIB = 1024 * 1024


def _params(vmem_mib=48):
    return pltpu.CompilerParams(vmem_limit_bytes=vmem_mib * MIB)


def _dot(a, b):
    return jnp.dot(a, b, preferred_element_type=F32)


def _dot_nt(a, b):
    return lax.dot_general(a, b, (((1,), (1,)), ((), ())), preferred_element_type=F32)


def _dot_tn(a, b):
    return lax.dot_general(a, b, (((0,), (0,)), ((), ())), preferred_element_type=F32)


def _ln_hat(x):
    mu = jnp.mean(x, axis=-1, keepdims=True)
    xc = x - mu
    var = jnp.mean(xc * xc, axis=-1, keepdims=True)
    r = lax.rsqrt(var + LN_EPS)
    return xc * r, r


def _ln_bwd(dxhat, xhat, r):
    return r * (dxhat - jnp.mean(dxhat, axis=-1, keepdims=True)
                - xhat * jnp.mean(dxhat * xhat, axis=-1, keepdims=True))


def _colsum(a):
    return jnp.sum(a, axis=0, keepdims=True)


def _rowsum(a):
    return jnp.sum(a, axis=1, keepdims=True)


def _full(shape):
    return pl.BlockSpec(shape, lambda *_: (0,) * len(shape))


def _comm_out_shapes(arrays, kinds):
    shapes = []
    for a, kind in zip(arrays, kinds):
        blk = a.shape if kind == "gather" else a.shape[1:]
        shapes.append(jax.ShapeDtypeStruct((N_DEV,) + tuple(blk), a.dtype))
    return shapes


def _comm_sems(n):
    return [pltpu.SemaphoreType.DMA((n, N_DEV - 1)), pltpu.SemaphoreType.DMA((n, N_DEV - 1)),
            pltpu.SemaphoreType.DMA((n,))]


def _comm_copies(ins, outs, kinds, send_sems, recv_sems, local_sems):
    x, y, c = lax.axis_index("x"), lax.axis_index("y"), lax.axis_index("c")
    me = 4 * x + 2 * y + c

    def src_for(t, dev_lin):
        return ins[t] if kinds[t] == "gather" else ins[t].at[dev_lin]

    local = [pltpu.make_async_copy(src_for(t, me), outs[t].at[me], local_sems.at[t]) for t in range(len(kinds))]
    sends, arrivals = [], []
    for k in range(1, N_DEV):
        px = 1 - x if (k >> 2) & 1 else x
        py = 1 - y if (k >> 1) & 1 else y
        pc = 1 - c if k & 1 else c
        peer_lin = 4 * px + 2 * py + pc
        for t in range(len(kinds)):
            sems = dict(send_sem=send_sems.at[t, k - 1], recv_sem=recv_sems.at[t, k - 1],
                        device_id=(px, py, pc), device_id_type=MESH)
            sends.append(pltpu.make_async_remote_copy(src_ref=src_for(t, peer_lin), dst_ref=outs[t].at[me], **sems))
            arrivals.append(pltpu.make_async_remote_copy(src_ref=src_for(t, peer_lin), dst_ref=outs[t].at[peer_lin], **sems))
    return local, sends, arrivals


def _comm_start(ins, outs, kinds, sems):
    local, sends, _ = _comm_copies(ins, outs, kinds, *sems)
    for cp in local + sends:
        cp.start()


def _comm_finish(ins, outs, kinds, sems):
    local, sends, arrivals = _comm_copies(ins, outs, kinds, *sems)
    for cp in arrivals:
        cp.wait_recv()
    for cp in sends:
        cp.wait_send()
    for cp in local:
        cp.wait()


def _exchange(name, arrays, kinds):
    n = len(arrays)

    def body(*refs):
        ins, outs, sems = refs[:n], refs[n:2 * n], refs[2 * n:]
        _comm_start(ins, outs, kinds, sems)
        _comm_finish(ins, outs, kinds, sems)

    any_spec = pl.BlockSpec(memory_space=pl.ANY)
    return pl.pallas_call(
        body, name=name, out_shape=_comm_out_shapes(arrays, kinds),
        in_specs=[any_spec] * n, out_specs=[any_spec] * n,
        scratch_shapes=_comm_sems(n),
    )(*arrays)


def _allgather_via_sibling(name, shard):
    def body(x_ref, out_ref, send_sems, recv_sems, local_sem):
        x, y, c = lax.axis_index("x"), lax.axis_index("y"), lax.axis_index("c")
        me, sibling = (x, y, c), (x, y, 1 - c)
        chips = [(1 - x, y), (x, 1 - y), (1 - x, 1 - y)]

        def copy(k, block, to, src=None):
            slot = out_ref.at[4 * block[0] + 2 * block[1] + block[2]]
            return pltpu.make_async_remote_copy(
                src_ref=slot if src is None else src, dst_ref=slot,
                send_sem=send_sems.at[k], recv_sem=recv_sems.at[k], device_id=to, device_id_type=MESH)

        mine = pltpu.make_async_copy(x_ref, out_ref.at[4 * x + 2 * y + c], local_sem)
        mine.start()
        first = [copy(0, me, sibling, src=x_ref)]
        first += [copy(1 + j, me, (*chip, c), src=x_ref) for j, chip in enumerate(chips)]
        for cp in first:
            cp.start()
        passed = [copy(4 + j, (*chip, c), sibling) for j, chip in enumerate(chips)]
        for j, chip in enumerate(chips):
            copy(1 + j, (*chip, c), me).wait_recv()
            passed[j].start()
        copy(0, sibling, me).wait_recv()
        for j, chip in enumerate(chips):
            copy(4 + j, (*chip, 1 - c), me).wait_recv()
        for cp in first + passed:
            cp.wait_send()
        mine.wait()

    any_spec = pl.BlockSpec(memory_space=pl.ANY)
    return pl.pallas_call(
        body, name=name, out_shape=jax.ShapeDtypeStruct((N_DEV,) + shard.shape, shard.dtype),
        in_specs=[any_spec], out_specs=any_spec,
        scratch_shapes=[pltpu.SemaphoreType.DMA((N_DEV - 1,)), pltpu.SemaphoreType.DMA((N_DEV - 1,)),
                        pltpu.SemaphoreType.DMA],
    )(shard)


def _call(body, name, grid, in_specs, out_specs, out_shape, args, scratch_shapes=(), comm=None):
    if comm is None:
        outs = pl.pallas_call(body, name=name, grid=grid, in_specs=in_specs, out_specs=out_specs,
                              out_shape=out_shape, scratch_shapes=list(scratch_shapes),
                              compiler_params=_params())(*args)
        return outs, []
    arrays, kinds = comm
    n, n_in, n_out, n_scr = len(arrays), len(in_specs), len(out_specs), len(scratch_shapes)

    def fused(*refs):
        c_in, x_in = refs[:n_in], refs[n_in:n_in + n]
        c_out = refs[n_in + n:n_in + n + n_out]
        x_out = refs[n_in + n + n_out:n_in + 2 * n + n_out]
        rest = refs[n_in + 2 * n + n_out:]
        c_scr, sems = rest[:n_scr], rest[n_scr:]
        ids = [pl.program_id(a) for a in range(len(grid))]
        is_first = functools.reduce(jnp.logical_and, [i == 0 for i in ids])
        is_last = functools.reduce(jnp.logical_and, [i == g - 1 for i, g in zip(ids, grid)])

        @pl.when(is_first)
        def _():
            _comm_start(x_in, x_out, kinds, sems)

        body(*c_in, *c_out, *c_scr)

        @pl.when(is_last)
        def _():
            _comm_finish(x_in, x_out, kinds, sems)

    any_spec = pl.BlockSpec(memory_space=pl.ANY)
    outs = pl.pallas_call(
        fused, name=name, grid=grid,
        in_specs=list(in_specs) + [any_spec] * n, out_specs=list(out_specs) + [any_spec] * n,
        out_shape=list(out_shape) + _comm_out_shapes(arrays, kinds),
        scratch_shapes=list(scratch_shapes) + _comm_sems(n),
        compiler_params=_params())(*args, *arrays)
    return outs[:n_out], outs[n_out:]


def _ln_proj(x, g, b, w_in_t):
    s, d = x.shape
    cols = w_in_t.shape[0]
    tm = min(ROW_TILE, s)

    def body(x_ref, g_ref, b_ref, w_ref, h_ref, p_ref):
        xhat, _ = _ln_hat(x_ref[...])
        h = (xhat * g_ref[...] + b_ref[...]).astype(BF16)
        h_ref[...] = h
        p_ref[...] = _dot_nt(h, w_ref[...]).astype(BF16)

    row = lambda width: pl.BlockSpec((tm, width), lambda i: (i, 0))
    return pl.pallas_call(
        body, name="ln_proj", grid=(s // tm,),
        in_specs=[row(d), _full((1, d)), _full((1, d)), _full((cols, d))],
        out_specs=[row(d), row(cols)],
        out_shape=[jax.ShapeDtypeStruct((s, d), BF16), jax.ShapeDtypeStruct((s, cols), BF16)],
        compiler_params=_params(),
    )(x, g, b, w_in_t)


def _sb_triangles(t):
    row = lax.broadcasted_iota(jnp.int32, (t, t), 0)
    col = lax.broadcasted_iota(jnp.int32, (t, t), 1)
    return (row > col).astype(BF16), (row >= col).astype(BF16)


def _sb_first_mask(t, has_prev):
    qrow = lax.broadcasted_iota(jnp.int32, (2 * t, 2 * t), 0) & (t - 1)
    col = lax.broadcasted_iota(jnp.int32, (2 * t, 2 * t), 1)
    return ((col < t) & has_prev) | ((col >= t) & (col - t < qrow))


def _sb_stack_heads(x2, first):
    zero = jnp.zeros_like(x2)
    return jnp.concatenate([jnp.where(first, x2, zero), jnp.where(first, zero, x2)], axis=0)


def _sb_key_tiles(ref, offs, t):
    tiles = [ref[pl.ds(off, t), :] for off in offs]
    return tiles[0] if len(tiles) == 1 else jnp.concatenate(tiles, axis=0)


def _sb_suffix(terms, row_sums, tri, carry):
    out = [None] * len(terms)
    for j in reversed(range(len(terms))):
        suf = carry
        for op in terms[j]:
            suf = suf + _dot(op, tri)
        out[j] = suf
        carry = carry + row_sums[j]
    return (out[0] if len(out) == 1 else jnp.concatenate(out, axis=1)), carry


def _sb_scores(qh, k_t, upper, carry_l, mask, t):
    z = _dot_nt(qh, k_t)
    sp = jnp.log(1.0 + jnp.exp(-jnp.abs(z)))
    neg = jnp.minimum(z, 0.0)
    lb = neg - sp
    l1 = (neg - z) - sp
    if mask is not None:
        l1 = jnp.where(mask, l1, 0.0)
    hi = l1.astype(BF16)
    lo = (l1 - hi.astype(F32)).astype(BF16)
    cols = [slice(j * t, (j + 1) * t) for j in range(z.shape[1] // t)]
    suf, carry_l = _sb_suffix([[hi[:, c], lo[:, c]] for c in cols], [_rowsum(l1[:, c]) for c in cols], upper, carry_l)
    a = jnp.exp(lb + suf)
    if mask is not None:
        a = jnp.where(mask, a, 0.0)
    return lb, a, carry_l


def _sb_walk(i, t, first_visit, visit, init):
    def alive(carry):
        return jnp.max(carry[0]) > SB_UNDERFLOW

    prev = pl.multiple_of(jnp.maximum(i - 1, 0) * t, t)
    carry = first_visit((prev, pl.multiple_of(i * t, t)), init)

    def cond(state):
        j, go, _ = state
        return (j < i - 1) & go

    def body(state):
        j, _, carry = state
        carry = visit((pl.multiple_of((i - 2 - j) * t, t),), carry)
        return j + 1, alive(carry), carry

    return lax.while_loop(cond, body, (jnp.int32(0), alive(carry), carry))[2]


def _sb_first_specs(n_pairs, n_steps, per_step, t):
    at = lambda h, i: (h * n_steps + i, 0, 0)
    n_tiles = n_pairs * n_steps * per_step
    specs = [pl.BlockSpec((per_step, 2 * t, 2 * t), at), pl.BlockSpec((per_step, 2 * t, 2 * t), at),
             pl.BlockSpec((per_step, 2 * t, 1), at)]
    shapes = [jax.ShapeDtypeStruct((n_tiles, 2 * t, 2 * t), BF16), jax.ShapeDtypeStruct((n_tiles, 2 * t, 2 * t), F32),
              jax.ShapeDtypeStruct((n_tiles, 2 * t, 1), F32)]
    return specs, shapes


def _sb_grid(s):
    t = min(SB_TILE, s)
    per_step = min(SB_TILES_PER_STEP, s // t)
    return t, per_step, s // (t * per_step)


def _sb_fwd(proj, n_pairs, comm=None):
    s = proj.shape[0]
    t, per_step, n_steps = _sb_grid(s)

    def body(q_ref, k_ref, v_ref, o_ref, ab_ref, beta_ref, cl_ref):
        lane = lax.broadcasted_iota(jnp.int32, (1, LANES), 1)
        first = lane < HEAD_DIM
        upper, _ = _sb_triangles(t)

        def query_tile(j, _):
            i = pl.program_id(1) * per_step + j
            rows = pl.ds(pl.multiple_of(j * t, t), t)
            qs = _sb_stack_heads(q_ref[rows, :] * SCALE, first)

            def first_visit(offs, carry):
                c_l, acc = carry
                mask = _sb_first_mask(t, i > 0)
                lb, a, c_l = _sb_scores(qs, _sb_key_tiles(k_ref, offs, t), upper, c_l, mask, t)
                a_b = a.astype(BF16)
                ab_ref[j] = a_b
                beta_ref[j] = jnp.where(mask, jnp.exp(lb), 0.0)
                cl_ref[j] = c_l
                return c_l, acc + _dot(a_b, _sb_key_tiles(v_ref, offs, t))

            def visit(offs, carry):
                c_l, acc = carry
                _, a, c_l = _sb_scores(qs, _sb_key_tiles(k_ref, offs, t), upper, c_l, None, t)
                return c_l, acc + _dot(a.astype(BF16), _sb_key_tiles(v_ref, offs, t))

            init = (jnp.zeros((2 * t, 1), F32), jnp.zeros((2 * t, LANES), F32))
            _, acc = _sb_walk(i, t, first_visit, visit, init)
            o_ref[rows, :] = jnp.where(first, acc[:t], acc[t:])
            return 0

        lax.fori_loop(0, per_step, query_tile, 0)

    qblk = pl.BlockSpec((t * per_step, LANES), lambda h, i: (i, h))
    first_specs, first_shapes = _sb_first_specs(n_pairs, n_steps, per_step, t)
    outs, landed = _call(
        body, "sb_fwd", (n_pairs, n_steps),
        in_specs=[qblk,
                  pl.BlockSpec((s, LANES), lambda h, i: (0, n_pairs + h)),
                  pl.BlockSpec((s, LANES), lambda h, i: (0, 2 * n_pairs + h))],
        out_specs=[qblk] + first_specs,
        out_shape=[jax.ShapeDtypeStruct((s, n_pairs * LANES), F32)] + first_shapes,
        args=(proj, proj, proj), comm=comm)
    return outs[0], outs[1:], landed


def _swa_bucket_table():
    qi = np.arange(SWA_BLOCK)[:, None]
    cj = np.arange(2 * SWA_BLOCK)[None, :]
    dist = qi + SWA_BLOCK - cj
    exact = REL_BUCKETS // 2
    d = np.maximum(dist, 0)
    d_f = np.maximum(d, 1).astype(np.float32)
    large = exact + (np.log(d_f / np.float32(exact)) / np.float32(math.log(REL_MAX_DIST / exact))
                     * np.float32(REL_BUCKETS - exact)).astype(np.int32)
    large = np.minimum(large, REL_BUCKETS - 1)
    return np.where(d < exact, d, large).astype(np.int32)


def _swa_build_bias(bucket_ref, rb_ref, bias_ref, n_groups, per_group):
    bk = bucket_ref[...]
    dist = (lax.broadcasted_iota(jnp.int32, bk.shape, 0) + SWA_BLOCK) - lax.broadcasted_iota(jnp.int32, bk.shape, 1)
    window = (dist >= 0) & (dist < SWA_BLOCK)
    for g in range(n_groups):
        for hh in range(per_group):
            acc = jnp.zeros(bk.shape, F32)
            for b in range(REL_BUCKETS):
                acc = jnp.where(bk == b, rb_ref[b, g * per_group + hh], acc)
            bias_ref[g, hh * SWA_BLOCK:(hh + 1) * SWA_BLOCK, :] = jnp.where(window, acc, -jnp.inf)


def _swa_first_block_mask(i):
    col = lax.broadcasted_iota(jnp.int32, (1, 2 * SWA_BLOCK), 1)
    return jnp.where((col < SWA_BLOCK) & (i == 0), -jnp.inf, 0.0)


def _swa_place(blk, h, group, sel):
    if (h % 2) != group:
        blk = pltpu.roll(blk.astype(F32), HEAD_DIM, axis=1).astype(BF16)
    return jnp.where(sel, blk, jnp.zeros_like(blk))


def _swa_stack(ref, group, per_group, sel, scale=1.0):
    parts = []
    for hh in range(per_group):
        h = group * per_group + hh
        parts.append(_swa_place(ref[:, (h // 2) * LANES:(h // 2 + 1) * LANES], h, group, sel))
    stacked = jnp.concatenate(parts, axis=0)
    return stacked if scale == 1.0 else stacked * scale


def _swa_unstack(stacked, group, per_group, pieces):
    for hh in range(per_group):
        h = group * per_group + hh
        piece = stacked[hh * SWA_BLOCK:(hh + 1) * SWA_BLOCK, :]
        pieces[h] = pltpu.roll(piece, HEAD_DIM, axis=1) if (h % 2) != group else piece


def _swa_sink_rows(sk_ref, group, per_group):
    rowh = lax.broadcasted_iota(jnp.int32, (per_group * SWA_BLOCK, 1), 0) // SWA_BLOCK
    sink = jnp.zeros((per_group * SWA_BLOCK, 1), F32) + sk_ref[0, group * per_group]
    for hh in range(1, per_group):
        sink = jnp.where(rowh == hh, sk_ref[0, group * per_group + hh], sink)
    return sink


def _swa_probs(q_pos, kcat, bias_h, first_mask, sink):
    logits = _dot_nt(q_pos, kcat) + (bias_h + first_mask)
    m = jnp.maximum(jnp.max(logits, axis=1, keepdims=True), sink)
    p = jnp.exp(logits - m)
    es = jnp.exp(sink - m)
    inv = 1.0 / (_rowsum(p) + es)
    return p * inv, es * inv


def _swa_specs(n_heads, qcol, kcol, vcol):
    width = n_heads * HEAD_DIM
    prev = lambda col: pl.BlockSpec((SWA_BLOCK, LANES), lambda i: (jnp.maximum(i - 1, 0), col))
    cur = lambda col: pl.BlockSpec((SWA_BLOCK, LANES), lambda i: (i, col))
    return [pl.BlockSpec((SWA_BLOCK, width), lambda i: (i, qcol)),
            prev(kcol), cur(kcol), prev(vcol), cur(vcol),
            _full((SWA_BLOCK, 2 * SWA_BLOCK)),
            pl.BlockSpec(memory_space=pltpu.SMEM), pl.BlockSpec(memory_space=pltpu.SMEM)]


def _swa_kept_specs(nb, n_groups, per_group):
    rows = per_group * SWA_BLOCK
    at = lambda i: (i, 0, 0, 0)
    specs = [pl.BlockSpec((None, n_groups, rows, 2 * SWA_BLOCK), at), pl.BlockSpec((None, n_groups, rows, 1), at)]
    shapes = [jax.ShapeDtypeStruct((nb, n_groups, rows, 2 * SWA_BLOCK), F32), jax.ShapeDtypeStruct((nb, n_groups, rows, 1), F32)]
    return specs, shapes


def _swa_fwd(proj, bucket, rel_bias, sinks, n_heads, qcol, kcol, vcol, comm=None):
    s = proj.shape[0]
    width = n_heads * HEAD_DIM
    n_groups = LANES // HEAD_DIM
    per_group = n_heads // n_groups

    def body(q_ref, kp_ref, kc_ref, vp_ref, vc_ref, bucket_ref, rb_ref, sk_ref, o_ref, prob_ref, psink_ref, bias_ref):
        i = pl.program_id(0)

        @pl.when(i == 0)
        def _():
            _swa_build_bias(bucket_ref, rb_ref, bias_ref, n_groups, per_group)

        lane = lax.broadcasted_iota(jnp.int32, (1, LANES), 1)
        first = lane < HEAD_DIM
        first_mask = _swa_first_block_mask(i)
        kcat = jnp.concatenate([kp_ref[...], kc_ref[...]], axis=0)
        vcat = jnp.concatenate([vp_ref[...], vc_ref[...]], axis=0)
        pieces = {}
        for g in range(n_groups):
            sel = first if g == 0 else jnp.logical_not(first)
            prob, p_sink = _swa_probs(_swa_stack(q_ref, g, per_group, sel, SCALE), kcat, bias_ref[g], first_mask,
                                      _swa_sink_rows(sk_ref, g, per_group))
            prob_ref[g] = prob
            psink_ref[g] = p_sink
            _swa_unstack(_dot(prob.astype(BF16), vcat), g, per_group, pieces)
        for j in range(n_heads // 2):
            o_ref[:, j * LANES:(j + 1) * LANES] = jnp.where(first, pieces[2 * j], pieces[2 * j + 1])

    kept_specs, kept_shapes = _swa_kept_specs(s // SWA_BLOCK, n_groups, per_group)
    outs, landed = _call(
        body, "swa_fwd", (s // SWA_BLOCK,),
        in_specs=_swa_specs(n_heads, qcol, kcol, vcol),
        out_specs=[pl.BlockSpec((SWA_BLOCK, width), lambda i: (i, 0))] + kept_specs,
        out_shape=[jax.ShapeDtypeStruct((s, width), F32)] + kept_shapes,
        args=(proj, proj, proj, proj, proj, bucket, rel_bias, sinks),
        scratch_shapes=[pltpu.VMEM((n_groups, per_group * SWA_BLOCK, 2 * SWA_BLOCK), F32)], comm=comm)
    return outs[0], outs[1:], landed


def _rms_fwd(o, g):
    r = lax.rsqrt(jnp.mean(o * o, axis=-1, keepdims=True) + RMS_EPS)
    n = o * r
    return n, r, n * g


def _mix_ffn(sb_out, sw_out, x, g_in, b_in, sb_g, sw_g, w_out, g1, b1, w_gu_t, w_down, g2, b2, target):
    s, d = x.shape
    wsb, wsw = sb_out.shape[1], sw_out.shape[1]
    dff = w_down.shape[0]
    assert wsb + wsw == d
    tm = min(FFN_TILE, s)

    def body(sb_ref, sw_ref, x_ref, gi_ref, bi_ref, sbg_ref, swg_ref, wo_hbm, g1_ref, b1_ref, wgu_hbm, wd_hbm,
             g2_ref, b2_ref, t_ref,
             du1_ref, mg_ref, h1b_ref, act_ref, dgu_ref, du2b_ref, dsb_ref, dsw_ref, st_ref,
             wo_ref, wgu_ref, wd_ref):
        @pl.when(pl.program_id(0) == 0)
        def _():
            pltpu.sync_copy(wo_hbm, wo_ref)
            pltpu.sync_copy(wgu_hbm, wgu_ref)
            pltpu.sync_copy(wd_hbm, wd_ref)
            st_ref[...] = jnp.zeros_like(st_ref)

        sb, sw = sb_ref[...], sw_ref[...]
        _, _, m_sb = _rms_fwd(sb, sbg_ref[...])
        _, _, m_sw = _rms_fwd(sw, swg_ref[...])
        m_sb = m_sb.astype(BF16)
        m_sw = m_sw.astype(BF16)
        mg_ref[:, :wsb] = m_sb
        mg_ref[:, wsb:] = m_sw
        xhat0, _ = _ln_hat(x_ref[...])
        u1 = ALPHA * (xhat0 * gi_ref[...] + bi_ref[...]) + _dot(m_sb, wo_ref[:wsb, :]) + _dot(m_sw, wo_ref[wsb:, :])

        xhat1, r1 = _ln_hat(u1)
        h1 = xhat1 * g1_ref[...] + b1_ref[...]
        h1b = h1.astype(BF16)
        h1b_ref[...] = h1b
        gate = _dot_nt(h1b, wgu_ref[:dff, :])
        up = _dot_nt(h1b, wgu_ref[dff:, :])
        sg = jax.nn.sigmoid(gate)
        silu = gate * sg
        act = (silu * up).astype(BF16)
        act_ref[...] = act
        u2 = ALPHA * h1 + _dot(act, wd_ref[...])
        xhat2, r2 = _ln_hat(u2)
        diff = xhat2 * g2_ref[...] + b2_ref[...] - t_ref[...]
        dh2 = diff * (1.0 / d)
        st_ref[0:1, :] += _colsum(dh2 * xhat2)
        st_ref[1:2, :] += _colsum(dh2)
        st_ref[2:3, :] += jnp.broadcast_to(_colsum(_rowsum(diff * diff)) * (0.5 / d), (1, d))
        du2 = _ln_bwd(dh2 * g2_ref[...], xhat2, r2)
        du2b = du2.astype(BF16)
        du2b_ref[...] = du2b
        dact = _dot_nt(du2b, wd_ref[...])
        dgate = (dact * up * (sg * (1.0 + gate * (1.0 - sg)))).astype(BF16)
        dup = (dact * silu).astype(BF16)
        dgu_ref[:, :dff] = dgate
        dgu_ref[:, dff:] = dup
        dh1 = _dot(dgate, wgu_ref[:dff, :]) + _dot(dup, wgu_ref[dff:, :]) + ALPHA * du2
        st_ref[3:4, :] += _colsum(dh1 * xhat1)
        st_ref[4:5, :] += _colsum(dh1)
        du1 = _ln_bwd(dh1 * g1_ref[...], xhat1, r1)
        du1_ref[...] = du1

        dmerged = _dot_nt(du1.astype(BF16), wo_ref[...])
        dsb, gsb = _rms_bwd(dmerged[:, :wsb], sb, sbg_ref[...])
        dsw, gsw = _rms_bwd(dmerged[:, wsb:], sw, swg_ref[...])
        dsb_ref[...] = dsb.astype(BF16)
        dsw_ref[...] = dsw.astype(BF16)
        st_ref[5:6, :wsb] += gsb
        st_ref[5:6, wsb:] += gsw

    row = lambda width: pl.BlockSpec((tm, width), lambda i: (i, 0))
    vec = lambda width: _full((1, width))
    hbm = pl.BlockSpec(memory_space=pl.ANY)
    bf = lambda width: jax.ShapeDtypeStruct((s, width), BF16)
    return pl.pallas_call(
        body, name="mix_ffn", grid=(s // tm,),
        in_specs=[row(wsb), row(wsw), row(d), vec(d), vec(d), vec(wsb), vec(wsw), hbm, vec(d), vec(d), hbm, hbm,
                  vec(d), vec(d), row(d)],
        out_specs=[row(d), row(d), row(d), row(dff), row(2 * dff), row(d), row(wsb), row(wsw), _full((8, d))],
        out_shape=[jax.ShapeDtypeStruct((s, d), F32), bf(d), bf(d), bf(dff), bf(2 * dff), bf(d), bf(wsb), bf(wsw),
                   jax.ShapeDtypeStruct((8, d), F32)],
        scratch_shapes=[pltpu.VMEM(w_out.shape, BF16), pltpu.VMEM(w_gu_t.shape, BF16), pltpu.VMEM(w_down.shape, BF16)],
        compiler_params=_params(60),
    )(sb_out, sw_out, x, g_in, b_in, sb_g, sw_g, w_out, g1, b1, w_gu_t, w_down, g2, b2, target)


def _rms_bwd(dm, o, g):
    n, r, _ = _rms_fwd(o, g)
    dn = dm * g
    return r * (dn - n * jnp.mean(dn * n, axis=-1, keepdims=True)), _colsum(dm * n)


def _sb_bwd(proj, dout, out, first, n_pairs, comm=None):
    s = proj.shape[0]
    t, per_step, n_steps = _sb_grid(s)
    width = n_pairs * LANES

    def body(q_ref, k_ref, v_ref, do_ref, o_ref, ab_ref, beta_ref, cl_ref, dq_ref, dk_out, dv_out, dk_ref, dv_ref):
        step = pl.program_id(1)

        @pl.when(step == 0)
        def _():
            dk_ref[...] = jnp.zeros_like(dk_ref)
            dv_ref[...] = jnp.zeros_like(dv_ref)

        lane = lax.broadcasted_iota(jnp.int32, (1, LANES), 1)
        first_lanes = lane < HEAD_DIM
        upper, incl = _sb_triangles(t)

        def query_tile(j, _):
            rows = pl.ds(pl.multiple_of(j * t, t), t)
            do2 = do_ref[rows, :]
            qs = _sb_stack_heads(q_ref[rows, :] * SCALE, first_lanes)
            dos = _sb_stack_heads(do2, first_lanes)
            prod = do2.astype(F32) * o_ref[rows, :]
            totals = jnp.concatenate([_rowsum(jnp.where(first_lanes, prod, 0.0)),
                                      _rowsum(jnp.where(first_lanes, 0.0, prod))], axis=0)

            def grads(offs, k_t, v_t, a_b, beta, c_e, dq):
                d_e = _dot_nt(dos, v_t) * a_b.astype(F32)
                d_eb = d_e.astype(BF16)
                cols = [slice(c * t, (c + 1) * t) for c in range(len(offs))]
                suf_e, c_e = _sb_suffix([[d_eb[:, c]] for c in cols], [_rowsum(d_e[:, c]) for c in cols], incl, c_e)
                dzb = (d_e - beta * (d_e + (totals - suf_e))).astype(BF16)
                dk_t = _dot_tn(dzb, qs)
                dv_t = _dot_tn(a_b, dos)
                for off, c in zip(offs, cols):
                    dk_ref[pl.ds(off, t), :] += dk_t[c, :]
                    dv_ref[pl.ds(off, t), :] += dv_t[c, :]
                return c_e, dq + _dot(dzb, k_t)

            def first_visit(offs, carry):
                _, c_e, dq = carry
                k_t = _sb_key_tiles(k_ref, offs, t)
                v_t = _sb_key_tiles(v_ref, offs, t)
                c_e, dq = grads(offs, k_t, v_t, ab_ref[j], beta_ref[j], c_e, dq)
                return cl_ref[j], c_e, dq

            def visit(offs, carry):
                c_l, c_e, dq = carry
                k_t = _sb_key_tiles(k_ref, offs, t)
                v_t = _sb_key_tiles(v_ref, offs, t)
                lb, a, c_l = _sb_scores(qs, k_t, upper, c_l, None, t)
                c_e, dq = grads(offs, k_t, v_t, a.astype(BF16), jnp.exp(lb), c_e, dq)
                return c_l, c_e, dq

            init = (jnp.zeros((2 * t, 1), F32), jnp.zeros((2 * t, 1), F32), jnp.zeros((2 * t, LANES), F32))
            _, _, dq = _sb_walk(step * per_step + j, t, first_visit, visit, init)
            dq_ref[rows, :] = (jnp.where(first_lanes, dq[:t], dq[t:]) * SCALE).astype(BF16)
            return 0

        lax.fori_loop(0, per_step, query_tile, 0)

        @pl.when(step == n_steps - 1)
        def _():
            dk_out[...] = dk_ref[...].astype(BF16)
            dv_out[...] = dv_ref[...].astype(BF16)

    qblk = pl.BlockSpec((t * per_step, LANES), lambda h, i: (i, h))
    whole = pl.BlockSpec((s, LANES), lambda h, i: (0, h))
    first_specs, _ = _sb_first_specs(n_pairs, n_steps, per_step, t)
    return _call(
        body, "sb_bwd", (n_pairs, n_steps),
        in_specs=[qblk,
                  pl.BlockSpec((s, LANES), lambda h, i: (0, n_pairs + h)),
                  pl.BlockSpec((s, LANES), lambda h, i: (0, 2 * n_pairs + h)),
                  qblk, qblk] + first_specs,
        out_specs=[qblk, whole, whole],
        out_shape=[jax.ShapeDtypeStruct((s, width), BF16)] * 3,
        args=(proj, proj, proj, dout, out, *first),
        scratch_shapes=[pltpu.VMEM((s, LANES), F32), pltpu.VMEM((s, LANES), F32)], comm=comm)


def _swa_bwd(proj, dout, kept, bucket, n_heads, qcol, kcol, vcol, comm=None):
    s = proj.shape[0]
    width = n_heads * HEAD_DIM
    n_groups = LANES // HEAD_DIM
    per_group = n_heads // n_groups
    nb = s // SWA_BLOCK

    def body(q_ref, kp_ref, kc_ref, vp_ref, vc_ref, bucket_ref, do_ref, prob_ref, psink_ref,
             dq_ref, dk_out, dv_out, dsk_ref, drb_ref, dbias_ref, dk_ref, dv_ref):
        i = pl.program_id(0)

        @pl.when(i == 0)
        def _():
            dbias_ref[...] = jnp.zeros_like(dbias_ref)
            dk_ref[...] = jnp.zeros_like(dk_ref)
            dv_ref[...] = jnp.zeros_like(dv_ref)
            dsk_ref[...] = jnp.zeros_like(dsk_ref)

        lane = lax.broadcasted_iota(jnp.int32, (1, LANES), 1)
        first = lane < HEAD_DIM
        kcat = jnp.concatenate([kp_ref[...], kc_ref[...]], axis=0)
        vcat = jnp.concatenate([vp_ref[...], vc_ref[...]], axis=0)
        dkcat = jnp.zeros((2 * SWA_BLOCK, LANES), F32)
        dvcat = jnp.zeros((2 * SWA_BLOCK, LANES), F32)
        pieces = {}
        for g in range(n_groups):
            sel = first if g == 0 else jnp.logical_not(first)
            q_g = _swa_stack(q_ref, g, per_group, sel, SCALE)
            do_g = _swa_stack(do_ref, g, per_group, sel)
            prob, p_sink = prob_ref[g], psink_ref[g]
            dprob = _dot_nt(do_g, vcat)
            delta = _rowsum(prob * dprob)
            dlog = prob * (dprob - delta)
            sink_term = p_sink * delta
            for hh in range(per_group):
                h = g * per_group + hh
                tot = _colsum(sink_term[hh * SWA_BLOCK:(hh + 1) * SWA_BLOCK, :])
                dsk_ref[h:h + 1, :] += jnp.broadcast_to(-tot, (1, LANES))
            dbias_ref[g] += dlog
            dlb = dlog.astype(BF16)
            _swa_unstack(_dot(dlb, kcat) * SCALE, g, per_group, pieces)
            dkcat += _dot_tn(dlb, q_g)
            dvcat += _dot_tn(prob.astype(BF16), do_g)
        for j in range(n_heads // 2):
            dq_ref[:, j * LANES:(j + 1) * LANES] = jnp.where(first, pieces[2 * j], pieces[2 * j + 1]).astype(BF16)

        cur = pl.multiple_of(i * SWA_BLOCK, SWA_BLOCK)
        dk_ref[pl.ds(cur, SWA_BLOCK), :] += dkcat[SWA_BLOCK:, :]
        dv_ref[pl.ds(cur, SWA_BLOCK), :] += dvcat[SWA_BLOCK:, :]

        @pl.when(i > 0)
        def _():
            prv = pl.multiple_of((i - 1) * SWA_BLOCK, SWA_BLOCK)
            dk_ref[pl.ds(prv, SWA_BLOCK), :] += dkcat[:SWA_BLOCK, :]
            dv_ref[pl.ds(prv, SWA_BLOCK), :] += dvcat[:SWA_BLOCK, :]

        @pl.when(i == nb - 1)
        def _():
            bk = bucket_ref[...]
            rowi = lax.broadcasted_iota(jnp.int32, (REL_BUCKETS, LANES), 0)
            coli = lax.broadcasted_iota(jnp.int32, (REL_BUCKETS, LANES), 1)
            res = jnp.zeros((REL_BUCKETS, LANES), F32)
            for h in range(n_heads):
                g, hh = divmod(h, per_group)
                db = dbias_ref[g, hh * SWA_BLOCK:(hh + 1) * SWA_BLOCK, :]
                for b in range(REL_BUCKETS):
                    tot = _colsum(_rowsum(jnp.where(bk == b, db, 0.0)))
                    res = jnp.where((rowi == b) & (coli == h), tot, res)
            drb_ref[...] = res
            dk_out[...] = dk_ref[...].astype(BF16)
            dv_out[...] = dv_ref[...].astype(BF16)

    kept_specs, _ = _swa_kept_specs(nb, n_groups, per_group)
    in_specs = _swa_specs(n_heads, qcol, kcol, vcol)[:6] + [pl.BlockSpec((SWA_BLOCK, width), lambda i: (i, 0))] + kept_specs
    return _call(
        body, "swa_bwd", (nb,),
        in_specs=in_specs,
        out_specs=[pl.BlockSpec((SWA_BLOCK, width), lambda i: (i, 0)),
                   _full((s, LANES)), _full((s, LANES)), _full((8, LANES)), _full((REL_BUCKETS, LANES))],
        out_shape=[jax.ShapeDtypeStruct((s, width), BF16), jax.ShapeDtypeStruct((s, LANES), BF16),
                   jax.ShapeDtypeStruct((s, LANES), BF16), jax.ShapeDtypeStruct((8, LANES), F32),
                   jax.ShapeDtypeStruct((REL_BUCKETS, LANES), F32)],
        args=(proj, proj, proj, proj, proj, bucket, dout, *kept),
        scratch_shapes=[pltpu.VMEM((n_groups, per_group * SWA_BLOCK, 2 * SWA_BLOCK), F32),
                        pltpu.VMEM((s, LANES), F32), pltpu.VMEM((s, LANES), F32)],
        comm=comm)


def _proj_bwd(pieces, w_in_t, du1, x, g_in, comm=None):
    s, d = x.shape
    cols = w_in_t.shape[0]
    tm = min(ROW_TILE, s)
    n_p = len(pieces)

    def body(*refs):
        p_refs = refs[:n_p]
        w_ref, du_ref, x_ref, g_ref, dx_ref, st_ref = refs[n_p:]
        i = pl.program_id(0)

        @pl.when(i == 0)
        def _():
            st_ref[...] = jnp.zeros_like(st_ref)

        dproj = jnp.concatenate([p[...] for p in p_refs], axis=1)
        dh0 = _dot(dproj, w_ref[...]) + ALPHA * du_ref[...]
        xhat, r = _ln_hat(x_ref[...])
        st_ref[0:1, :] += _colsum(dh0 * xhat)
        st_ref[1:2, :] += _colsum(dh0)
        dx_ref[...] = _ln_bwd(dh0 * g_ref[...], xhat, r)

    row = lambda width: pl.BlockSpec((tm, width), lambda i: (i, 0))
    return _call(
        body, "proj_bwd", (s // tm,),
        in_specs=[row(p.shape[1]) for p in pieces] + [_full((cols, d)), row(d), row(d), _full((1, d))],
        out_specs=[row(d), _full((8, d))],
        out_shape=[jax.ShapeDtypeStruct((s, d), F32), jax.ShapeDtypeStruct((8, d), F32)],
        args=(*pieces, w_in_t, du1, x, g_in), comm=comm)


def _wgrad(name, pieces, b, tm, tn):
    s, n = b.shape
    m = sum(p.shape[1] for p in pieces)
    n_p = len(pieces)
    assert n_p == 1 or tm == m
    ts = min(WGRAD_TOKENS if b.dtype == BF16 and n_p == 1 else WGRAD_TOKENS // 2, s)
    n_k = s // ts

    def body(*refs):
        p_refs, b_ref, o_ref, acc_ref = refs[:n_p], refs[n_p], refs[n_p + 1], refs[n_p + 2]
        k = pl.program_id(2)

        @pl.when(k == 0)
        def _():
            acc_ref[...] = jnp.zeros_like(acc_ref)

        a = p_refs[0][...] if n_p == 1 else jnp.concatenate([p[...] for p in p_refs], axis=1)
        acc_ref[...] += _dot_tn(a, b_ref[...].astype(BF16))

        @pl.when(k == n_k - 1)
        def _():
            o_ref[...] = acc_ref[...].astype(BF16)

    piece_spec = lambda p: pl.BlockSpec((ts, tm if n_p == 1 else p.shape[1]), lambda i, j, k: (k, i))
    return pl.pallas_call(
        body, name=name, grid=(m // tm, n // tn, n_k),
        in_specs=[piece_spec(p) for p in pieces] + [pl.BlockSpec((ts, tn), lambda i, j, k: (k, j))],
        out_specs=pl.BlockSpec((tm, tn), lambda i, j, k: (i, j)),
        out_shape=jax.ShapeDtypeStruct((m, n), BF16),
        scratch_shapes=[pltpu.VMEM((tm, tn), F32)],
        compiler_params=_params(),
    )(*pieces, b)


def _adamw_math(w, g, m, v):
    m = ADAM_B1 * m + (1.0 - ADAM_B1) * g
    v = ADAM_B2 * v + (1.0 - ADAM_B2) * (g * g)
    m_hat = m / (1.0 - ADAM_B1 ** ADAM_STEP)
    v_hat = v / (1.0 - ADAM_B2 ** ADAM_STEP)
    delta = -ADAM_LR * (m_hat / (jnp.sqrt(v_hat) + ADAM_EPS) + ADAM_WD * w)
    return delta, m, v


def _adamw_rows(rows):
    return max(r for r in range(16, 257, 16) if rows % r == 0)


def _adamw(name, landed, w, m, v, tr):
    rows, cols = w.shape

    def body(l_ref, w_ref, m_ref, v_ref, g_ref, d_ref, nm_ref, nv_ref):
        g = l_ref[0].astype(F32)
        for src in range(1, N_DEV):
            g = g + l_ref[src].astype(F32)
        delta, nm, nv = _adamw_math(w_ref[...], g, m_ref[...], v_ref[...])
        g_ref[...] = g
        d_ref[...] = delta
        nm_ref[...] = nm
        nv_ref[...] = nv

    blk = pl.BlockSpec((tr, cols), lambda i: (i, 0))
    shape = jax.ShapeDtypeStruct((rows, cols), F32)
    return pl.pallas_call(
        body, name=name, grid=(rows // tr,),
        in_specs=[pl.BlockSpec((N_DEV, tr, cols), lambda i: (0, i, 0)), blk, blk, blk],
        out_specs=[blk, blk, blk, blk],
        out_shape=[shape, shape, shape, shape],
        compiler_params=_params(),
    )(landed, w, m, v)


def _pack(d, ln_in_g, ln_in_b, ln1_g, ln1_b, ln2_g, ln2_b, sb_g, sw_g, rel_bias, sinks, extra=None):
    tail = [rel_bias.reshape(-1), sinks.reshape(-1)]
    if extra is not None:
        tail.append(extra.reshape(-1))
    tail = jnp.concatenate(tail)
    tail = jnp.concatenate([tail, jnp.zeros((d - tail.shape[0],), F32)])
    rows = [ln_in_g.reshape(-1), ln_in_b.reshape(-1), ln1_g.reshape(-1), ln1_b.reshape(-1),
            ln2_g.reshape(-1), ln2_b.reshape(-1),
            jnp.concatenate([sb_g.reshape(-1), sw_g.reshape(-1)]), tail]
    return jnp.stack(rows)


def _unpack(p, wsb, n_rb, n_sk):
    return [p[0], p[1], p[6, :wsb][None], p[6, wsb:][None], p[7, n_rb:n_rb + n_sk][None],
            p[7, :n_rb].reshape(REL_BUCKETS, -1), p[2][None], p[3][None], p[4][None], p[5][None]]


def kernel(x, ln_in_g, ln_in_b, w_in, sb_norm_g, swa_norm_g, sinks, rel_bias, w_out, ln1_g, ln1_b, w_gate_up, w_down, ln2_g, ln2_b, loss_target, m_ln_in_g, m_ln_in_b, m_w_in, m_sb_norm_g, m_swa_norm_g, m_sinks, m_rel_bias, m_w_out, m_ln1_g, m_ln1_b, m_w_gate_up, m_w_down, m_ln2_g, m_ln2_b, v_ln_in_g, v_ln_in_b, v_w_in, v_sb_norm_g, v_swa_norm_g, v_sinks, v_rel_bias, v_w_out, v_ln1_g, v_ln1_b, v_w_gate_up, v_w_down, v_ln2_g, v_ln2_b):
    x2 = x[0]
    tgt = loss_target[0]
    s, d = x2.shape
    wsb = sb_norm_g.shape[-1]
    wsw = swa_norm_g.shape[-1]
    n_sw_heads = sinks.shape[-1]
    n_pairs = wsb // LANES
    dff = w_down.shape[1] * N_DEV
    assert wsb % LANES == 0 and wsw % LANES == 0 and n_sw_heads * HEAD_DIM == wsw
    assert 3 * wsb % wsw == 0 and dff % LANES == 0 and s % SWA_BLOCK == 0
    qcol = 3 * wsb // wsw
    kcol = (3 * wsb + wsw) // LANES
    vcol = kcol + 1
    assert w_in.shape[-1] * N_DEV == (vcol + 1) * LANES

    t2 = lambda a: jnp.transpose(a[0])
    big_w = [t2(w_in), w_out[0], t2(w_gate_up), w_down[0]]
    big_m = [t2(m_w_in), m_w_out[0], t2(m_w_gate_up), m_w_down[0]]
    big_v = [t2(v_w_in), v_w_out[0], t2(v_w_gate_up), v_w_down[0]]

    cat_rows = lambda g: g.reshape(N_DEV * g.shape[1], g.shape[2])
    shards = [w.astype(BF16) for w in big_w]
    w_in_t = cat_rows(_allgather_via_sibling("w_in_allgather", shards[0]))

    vec = lambda a: a.reshape(1, -1)
    g_in, b_in = vec(ln_in_g), vec(ln_in_b)
    bucket = jnp.asarray(_swa_bucket_table())

    h0b, proj = _ln_proj(x2, g_in, b_in, w_in_t)
    sb_out, sb_first, gathered = _sb_fwd(proj, n_pairs, comm=(shards[1:3], ["gather"] * 2))
    w_out_f, w_gu_t = cat_rows(gathered[0]), cat_rows(gathered[1])
    sw_out, sw_kept, gathered = _swa_fwd(proj, bucket, rel_bias, sinks, n_sw_heads, qcol, kcol, vcol,
                                comm=(shards[3:], ["gather"]))
    w_down_f = cat_rows(gathered[0])
    du1, merged, h1b, act, dgu, du2b, dsb, dsw, st_ffn = _mix_ffn(
        sb_out, sw_out, x2, g_in, b_in, sb_norm_g, swa_norm_g, w_out_f, ln1_g, ln1_b, w_gu_t, w_down_f, ln2_g, ln2_b, tgt)

    split_rows = lambda g: g.reshape(N_DEV, g.shape[0] // N_DEV, g.shape[1])
    gw_gu = _wgrad("wgrad_gate_up", [dgu], h1b, dff // 2, d)
    gw_down = _wgrad("wgrad_down", [act], du2b, dff // 2, d)
    gw_out = _wgrad("wgrad_out", [merged], du1, min(512, d), d)
    (dq_sb, dk_sb, dv_sb), (land_gu, land_out) = _sb_bwd(
        proj, dsb, sb_out, sb_first, n_pairs, comm=([split_rows(gw_gu), split_rows(gw_out)], ["scatter"] * 2))
    (dq_sw, dk_sw, dv_sw, st_sink, st_rb), (land_down,) = _swa_bwd(
        proj, dsw, sw_kept, bucket, n_sw_heads, qcol, kcol, vcol, comm=([split_rows(gw_down)], ["scatter"]))
    pieces = [dq_sb, dk_sb, dv_sb, dq_sw, dk_sw, dv_sw]
    gw_in = _wgrad("wgrad_in", pieces, h0b, proj.shape[1], d)
    (grad_x, st_in), (land_in,) = _proj_bwd(pieces, w_in_t, du1, x2, g_in, comm=([split_rows(gw_in)], ["scatter"]))

    n_rb = rel_bias.size
    small = _pack(d, st_in[0], st_in[1], st_ffn[3], st_ffn[4], st_ffn[0], st_ffn[1],
                  st_ffn[5, :wsb], st_ffn[5, wsb:], st_rb[:, :n_sw_heads], st_sink[:n_sw_heads, 0],
                  extra=st_ffn[2, 0:1])
    land_small = _exchange("small_grads_allgather", [small], ["gather"])[0]
    landed = [land_in, land_out, land_gu, land_down, land_small]

    big = []
    for name, land, w, m, v in zip(["adamw_in", "adamw_out", "adamw_gate_up", "adamw_down"], landed[:4], big_w, big_m, big_v):
        big.append(_adamw(name, land, w, m, v, _adamw_rows(w.shape[0])))

    small_w = _pack(d, ln_in_g, ln_in_b, ln1_g, ln1_b, ln2_g, ln2_b, sb_norm_g, swa_norm_g, rel_bias, sinks)
    small_m = _pack(d, m_ln_in_g, m_ln_in_b, m_ln1_g, m_ln1_b, m_ln2_g, m_ln2_b, m_sb_norm_g, m_swa_norm_g, m_rel_bias, m_sinks)
    small_v = _pack(d, v_ln_in_g, v_ln_in_b, v_ln1_g, v_ln1_b, v_ln2_g, v_ln2_b, v_sb_norm_g, v_swa_norm_g, v_rel_bias, v_sinks)
    sg, sd, sm, sv = _adamw("adamw_small", landed[4], small_w, small_m, small_v, 8)
    n_sk = sinks.size
    loss = sg[7, n_rb + n_sk]

    def leaves(idx):
        sm_l = _unpack([sg, sd, sm, sv][idx], wsb, n_rb, n_sk)
        bg = [jnp.transpose(big[0][idx])[None], big[1][idx][None], jnp.transpose(big[2][idx])[None], big[3][idx][None]]
        return [sm_l[0], sm_l[1], bg[0], sm_l[2], sm_l[3], sm_l[4], sm_l[5], bg[1], sm_l[6], sm_l[7], bg[2], bg[3], sm_l[8], sm_l[9]]

    return (loss, grad_x[None], *leaves(0), *leaves(1), *leaves(2), *leaves(3))
```

```python
import functools
import math

import numpy as np
import jax
import jax.numpy as jnp
from jax import lax
from jax.experimental import pallas as pl
from jax.experimental.pallas import tpu as pltpu

F32 = jnp.float32
BF16 = jnp.bfloat16
MESH = pl.DeviceIdType.MESH

N_DEV = 8
LANES = 128
HEAD_DIM = 64
SCALE = HEAD_DIM ** -0.5
SWA_BLOCK = 128
SWA_BLOCKS_PER_STEP = 4
REL_BUCKETS = 32
REL_MAX_DIST = 128
ALPHA = 2.0 ** 0.25
LN_EPS = 1e-5
RMS_EPS = 1e-6
ADAM_LR = 0.001
ADAM_B1 = 0.9
ADAM_B2 = 0.999
ADAM_EPS = 1e-08
ADAM_WD = 0.01
ADAM_STEP = 10

ROW_TILE = 512
SB_TILE = 256
SB_TILES_PER_STEP = 4
FFN_TILE = 256
WGRAD_TOKENS = 2048
SB_UNDERFLOW = -110.0
MIB = 1024 * 1024


def _params(vmem_mib=48):
    return pltpu.CompilerParams(vmem_limit_bytes=vmem_mib * MIB)


def _dot(a, b):
    return jnp.dot(a, b, preferred_element_type=F32)


def _dot_nt(a, b):
    return lax.dot_general(a, b, (((1,), (1,)), ((), ())), preferred_element_type=F32)


def _dot_tn(a, b):
    return lax.dot_general(a, b, (((0,), (0,)), ((), ())), preferred_element_type=F32)


def _ln_hat(x):
    mu = jnp.mean(x, axis=-1, keepdims=True)
    xc = x - mu
    var = jnp.mean(xc * xc, axis=-1, keepdims=True)
    r = lax.rsqrt(var + LN_EPS)
    return xc * r, r


def _ln_bwd(dxhat, xhat, r):
    return r * (dxhat - jnp.mean(dxhat, axis=-1, keepdims=True)
                - xhat * jnp.mean(dxhat * xhat, axis=-1, keepdims=True))


def _colsum(a):
    return jnp.sum(a, axis=0, keepdims=True)


def _rowsum(a):
    return jnp.sum(a, axis=1, keepdims=True)


def _full(shape):
    return pl.BlockSpec(shape, lambda *_: (0,) * len(shape))


def _comm_out_shapes(arrays, kinds):
    shapes = []
    for a, kind in zip(arrays, kinds):
        blk = a.shape if kind == "gather" else a.shape[1:]
        shapes.append(jax.ShapeDtypeStruct((N_DEV,) + tuple(blk), a.dtype))
    return shapes


def _comm_sems(n):
    return [pltpu.SemaphoreType.DMA((n, N_DEV - 1)), pltpu.SemaphoreType.DMA((n, N_DEV - 1)),
            pltpu.SemaphoreType.DMA((n,))]


def _comm_copies(ins, outs, kinds, send_sems, recv_sems, local_sems):
    x, y, c = lax.axis_index("x"), lax.axis_index("y"), lax.axis_index("c")
    me = 4 * x + 2 * y + c

    def src_for(t, dev_lin):
        return ins[t] if kinds[t] == "gather" else ins[t].at[dev_lin]

    local = [pltpu.make_async_copy(src_for(t, me), outs[t].at[me], local_sems.at[t]) for t in range(len(kinds))]
    sends, arrivals = [], []
    for k in range(1, N_DEV):
        px = 1 - x if (k >> 2) & 1 else x
        py = 1 - y if (k >> 1) & 1 else y
        pc = 1 - c if k & 1 else c
        peer_lin = 4 * px + 2 * py + pc
        for t in range(len(kinds)):
            sems = dict(send_sem=send_sems.at[t, k - 1], recv_sem=recv_sems.at[t, k - 1],
                        device_id=(px, py, pc), device_id_type=MESH)
            sends.append(pltpu.make_async_remote_copy(src_ref=src_for(t, peer_lin), dst_ref=outs[t].at[me], **sems))
            arrivals.append(pltpu.make_async_remote_copy(src_ref=src_for(t, peer_lin), dst_ref=outs[t].at[peer_lin], **sems))
    return local, sends, arrivals


def _comm_start(ins, outs, kinds, sems):
    local, sends, _ = _comm_copies(ins, outs, kinds, *sems)
    for cp in local + sends:
        cp.start()


def _comm_finish(ins, outs, kinds, sems):
    local, sends, arrivals = _comm_copies(ins, outs, kinds, *sems)
    for cp in arrivals:
        cp.wait_recv()
    for cp in sends:
        cp.wait_send()
    for cp in local:
        cp.wait()


def _exchange(name, arrays, kinds):
    n = len(arrays)

    def body(*refs):
        ins, outs, sems = refs[:n], refs[n:2 * n], refs[2 * n:]
        _comm_start(ins, outs, kinds, sems)
        _comm_finish(ins, outs, kinds, sems)

    any_spec = pl.BlockSpec(memory_space=pl.ANY)
    return pl.pallas_call(
        body, name=name, out_shape=_comm_out_shapes(arrays, kinds),
        in_specs=[any_spec] * n, out_specs=[any_spec] * n,
        scratch_shapes=_comm_sems(n),
    )(*arrays)


def _allgather_via_sibling(name, shard):
    def body(x_ref, out_ref, send_sems, recv_sems, local_sem):
        x, y, c = lax.axis_index("x"), lax.axis_index("y"), lax.axis_index("c")
        me, sibling = (x, y, c), (x, y, 1 - c)
        chips = [(1 - x, y), (x, 1 - y), (1 - x, 1 - y)]

        def copy(k, block, to, src=None):
            slot = out_ref.at[4 * block[0] + 2 * block[1] + block[2]]
            return pltpu.make_async_remote_copy(
                src_ref=slot if src is None else src, dst_ref=slot,
                send_sem=send_sems.at[k], recv_sem=recv_sems.at[k], device_id=to, device_id_type=MESH)

        mine = pltpu.make_async_copy(x_ref, out_ref.at[4 * x + 2 * y + c], local_sem)
        mine.start()
        first = [copy(0, me, sibling, src=x_ref)]
        first += [copy(1 + j, me, (*chip, c), src=x_ref) for j, chip in enumerate(chips)]
        for cp in first:
            cp.start()
        passed = [copy(4 + j, (*chip, c), sibling) for j, chip in enumerate(chips)]
        for j, chip in enumerate(chips):
            copy(1 + j, (*chip, c), me).wait_recv()
            passed[j].start()
        copy(0, sibling, me).wait_recv()
        for j, chip in enumerate(chips):
            copy(4 + j, (*chip, 1 - c), me).wait_recv()
        for cp in first + passed:
            cp.wait_send()
        mine.wait()

    any_spec = pl.BlockSpec(memory_space=pl.ANY)
    return pl.pallas_call(
        body, name=name, out_shape=jax.ShapeDtypeStruct((N_DEV,) + shard.shape, shard.dtype),
        in_specs=[any_spec], out_specs=any_spec,
        scratch_shapes=[pltpu.SemaphoreType.DMA((N_DEV - 1,)), pltpu.SemaphoreType.DMA((N_DEV - 1,)),
                        pltpu.SemaphoreType.DMA],
    )(shard)


def _call(body, name, grid, in_specs, out_specs, out_shape, args, scratch_shapes=(), comm=None):
    if comm is None:
        outs = pl.pallas_call(body, name=name, grid=grid, in_specs=in_specs, out_specs=out_specs,
                              out_shape=out_shape, scratch_shapes=list(scratch_shapes),
                              compiler_params=_params())(*args)
        return outs, []
    arrays, kinds = comm
    n, n_in, n_out, n_scr = len(arrays), len(in_specs), len(out_specs), len(scratch_shapes)

    def fused(*refs):
        c_in, x_in = refs[:n_in], refs[n_in:n_in + n]
        c_out = refs[n_in + n:n_in + n + n_out]
        x_out = refs[n_in + n + n_out:n_in + 2 * n + n_out]
        rest = refs[n_in + 2 * n + n_out:]
        c_scr, sems = rest[:n_scr], rest[n_scr:]
        ids = [pl.program_id(a) for a in range(len(grid))]
        is_first = functools.reduce(jnp.logical_and, [i == 0 for i in ids])
        is_last = functools.reduce(jnp.logical_and, [i == g - 1 for i, g in zip(ids, grid)])

        @pl.when(is_first)
        def _():
            _comm_start(x_in, x_out, kinds, sems)

        body(*c_in, *c_out, *c_scr)

        @pl.when(is_last)
        def _():
            _comm_finish(x_in, x_out, kinds, sems)

    any_spec = pl.BlockSpec(memory_space=pl.ANY)
    outs = pl.pallas_call(
        fused, name=name, grid=grid,
        in_specs=list(in_specs) + [any_spec] * n, out_specs=list(out_specs) + [any_spec] * n,
        out_shape=list(out_shape) + _comm_out_shapes(arrays, kinds),
        scratch_shapes=list(scratch_shapes) + _comm_sems(n),
        compiler_params=_params())(*args, *arrays)
    return outs[:n_out], outs[n_out:]


def _ln_proj(x, g, b, w_in_t):
    s, d = x.shape
    cols = w_in_t.shape[0]
    tm = min(ROW_TILE, s)

    def body(x_ref, g_ref, b_ref, w_ref, h_ref, p_ref):
        xhat, _ = _ln_hat(x_ref[...])
        h = (xhat * g_ref[...] + b_ref[...]).astype(BF16)
        h_ref[...] = h
        p_ref[...] = _dot_nt(h, w_ref[...]).astype(BF16)

    row = lambda width: pl.BlockSpec((tm, width), lambda i: (i, 0))
    return pl.pallas_call(
        body, name="ln_proj", grid=(s // tm,),
        in_specs=[row(d), _full((1, d)), _full((1, d)), _full((cols, d))],
        out_specs=[row(d), row(cols)],
        out_shape=[jax.ShapeDtypeStruct((s, d), BF16), jax.ShapeDtypeStruct((s, cols), BF16)],
        compiler_params=_params(),
    )(x, g, b, w_in_t)


def _sb_triangles(t):
    row = lax.broadcasted_iota(jnp.int32, (t, t), 0)
    col = lax.broadcasted_iota(jnp.int32, (t, t), 1)
    return (row > col).astype(BF16), (row >= col).astype(BF16)


def _sb_first_mask(t, has_prev):
    qrow = lax.broadcasted_iota(jnp.int32, (2 * t, 2 * t), 0) & (t - 1)
    col = lax.broadcasted_iota(jnp.int32, (2 * t, 2 * t), 1)
    return ((col < t) & has_prev) | ((col >= t) & (col - t < qrow))


def _sb_stack_heads(x2, first):
    zero = jnp.zeros_like(x2)
    return jnp.concatenate([jnp.where(first, x2, zero), jnp.where(first, zero, x2)], axis=0)


def _sb_key_tiles(ref, offs, t):
    tiles = [ref[pl.ds(off, t), :] for off in offs]
    return tiles[0] if len(tiles) == 1 else jnp.concatenate(tiles, axis=0)


def _sb_suffix(terms, row_sums, tri, carry):
    out = [None] * len(terms)
    for j in reversed(range(len(terms))):
        suf = carry
        for op in terms[j]:
            suf = suf + _dot(op, tri)
        out[j] = suf
        carry = carry + row_sums[j]
    return (out[0] if len(out) == 1 else jnp.concatenate(out, axis=1)), carry


def _sb_scores(qh, k_t, upper, carry_l, mask, t):
    z = _dot_nt(qh, k_t)
    sp = jnp.log(1.0 + jnp.exp(-jnp.abs(z)))
    neg = jnp.minimum(z, 0.0)
    lb = neg - sp
    l1 = (neg - z) - sp
    if mask is not None:
        l1 = jnp.where(mask, l1, 0.0)
    hi = l1.astype(BF16)
    lo = (l1 - hi.astype(F32)).astype(BF16)
    cols = [slice(j * t, (j + 1) * t) for j in range(z.shape[1] // t)]
    suf, carry_l = _sb_suffix([[hi[:, c], lo[:, c]] for c in cols], [_rowsum(l1[:, c]) for c in cols], upper, carry_l)
    a = jnp.exp(lb + suf)
    if mask is not None:
        a = jnp.where(mask, a, 0.0)
    return lb, a, carry_l


def _sb_walk(i, t, first_visit, visit, init):
    def alive(carry):
        return jnp.max(carry[0]) > SB_UNDERFLOW

    prev = pl.multiple_of(jnp.maximum(i - 1, 0) * t, t)
    carry = first_visit((prev, pl.multiple_of(i * t, t)), init)

    def cond(state):
        j, go, _ = state
        return (j < i - 1) & go

    def body(state):
        j, _, carry = state
        carry = visit((pl.multiple_of((i - 2 - j) * t, t),), carry)
        return j + 1, alive(carry), carry

    return lax.while_loop(cond, body, (jnp.int32(0), alive(carry), carry))[2]


def _sb_first_specs(n_pairs, n_steps, per_step, t):
    at = lambda h, i: (h * n_steps + i, 0, 0)
    n_tiles = n_pairs * n_steps * per_step
    specs = [pl.BlockSpec((per_step, 2 * t, 2 * t), at), pl.BlockSpec((per_step, 2 * t, 2 * t), at),
             pl.BlockSpec((per_step, 2 * t, 1), at)]
    shapes = [jax.ShapeDtypeStruct((n_tiles, 2 * t, 2 * t), BF16), jax.ShapeDtypeStruct((n_tiles, 2 * t, 2 * t), F32),
              jax.ShapeDtypeStruct((n_tiles, 2 * t, 1), F32)]
    return specs, shapes


def _sb_grid(s):
    t = min(SB_TILE, s)
    per_step = min(SB_TILES_PER_STEP, s // t)
    return t, per_step, s // (t * per_step)


def _sb_fwd(proj, n_pairs, comm=None):
    s = proj.shape[0]
    t, per_step, n_steps = _sb_grid(s)

    def body(q_ref, k_ref, v_ref, o_ref, ab_ref, beta_ref, cl_ref):
        lane = lax.broadcasted_iota(jnp.int32, (1, LANES), 1)
        first = lane < HEAD_DIM
        upper, _ = _sb_triangles(t)

        def query_tile(j, _):
            i = pl.program_id(1) * per_step + j
            rows = pl.ds(pl.multiple_of(j * t, t), t)
            qs = _sb_stack_heads(q_ref[rows, :] * SCALE, first)

            def first_visit(offs, carry):
                c_l, acc = carry
                mask = _sb_first_mask(t, i > 0)
                lb, a, c_l = _sb_scores(qs, _sb_key_tiles(k_ref, offs, t), upper, c_l, mask, t)
                a_b = a.astype(BF16)
                ab_ref[j] = a_b
                beta_ref[j] = jnp.where(mask, jnp.exp(lb), 0.0)
                cl_ref[j] = c_l
                return c_l, acc + _dot(a_b, _sb_key_tiles(v_ref, offs, t))

            def visit(offs, carry):
                c_l, acc = carry
                _, a, c_l = _sb_scores(qs, _sb_key_tiles(k_ref, offs, t), upper, c_l, None, t)
                return c_l, acc + _dot(a.astype(BF16), _sb_key_tiles(v_ref, offs, t))

            init = (jnp.zeros((2 * t, 1), F32), jnp.zeros((2 * t, LANES), F32))
            _, acc = _sb_walk(i, t, first_visit, visit, init)
            o_ref[rows, :] = jnp.where(first, acc[:t], acc[t:])
            return 0

        lax.fori_loop(0, per_step, query_tile, 0)

    qblk = pl.BlockSpec((t * per_step, LANES), lambda h, i: (i, h))
    first_specs, first_shapes = _sb_first_specs(n_pairs, n_steps, per_step, t)
    outs, landed = _call(
        body, "sb_fwd", (n_pairs, n_steps),
        in_specs=[qblk,
                  pl.BlockSpec((s, LANES), lambda h, i: (0, n_pairs + h)),
                  pl.BlockSpec((s, LANES), lambda h, i: (0, 2 * n_pairs + h))],
        out_specs=[qblk] + first_specs,
        out_shape=[jax.ShapeDtypeStruct((s, n_pairs * LANES), F32)] + first_shapes,
        args=(proj, proj, proj), comm=comm)
    return outs[0], outs[1:], landed


def _swa_bucket_table():
    qi = np.arange(SWA_BLOCK)[:, None]
    cj = np.arange(2 * SWA_BLOCK)[None, :]
    dist = qi + SWA_BLOCK - cj
    exact = REL_BUCKETS // 2
    d = np.maximum(dist, 0)
    d_f = np.maximum(d, 1).astype(np.float32)
    large = exact + (np.log(d_f / np.float32(exact)) / np.float32(math.log(REL_MAX_DIST / exact))
                     * np.float32(REL_BUCKETS - exact)).astype(np.int32)
    large = np.minimum(large, REL_BUCKETS - 1)
    return np.where(d < exact, d, large).astype(np.int32)


def _swa_build_bias(bucket_ref, rb_ref, bias_ref, n_groups, per_group):
    bk = bucket_ref[...]
    dist = (lax.broadcasted_iota(jnp.int32, bk.shape, 0) + SWA_BLOCK) - lax.broadcasted_iota(jnp.int32, bk.shape, 1)
    window = (dist >= 0) & (dist < SWA_BLOCK)
    for g in range(n_groups):
        for hh in range(per_group):
            acc = jnp.zeros(bk.shape, F32)
            for b in range(REL_BUCKETS):
                acc = jnp.where(bk == b, rb_ref[b, g * per_group + hh], acc)
            bias_ref[g, hh * SWA_BLOCK:(hh + 1) * SWA_BLOCK, :] = jnp.where(window, acc, -jnp.inf)


def _swa_first_block_mask(i):
    col = lax.broadcasted_iota(jnp.int32, (1, 2 * SWA_BLOCK), 1)
    return jnp.where((col < SWA_BLOCK) & (i == 0), -jnp.inf, 0.0)


def _swa_place(blk, h, group, sel):
    if (h % 2) != group:
        blk = pltpu.roll(blk.astype(F32), HEAD_DIM, axis=1).astype(BF16)
    return jnp.where(sel, blk, jnp.zeros_like(blk))


def _swa_stack(ref, group, per_group, sel, scale=1.0):
    parts = []
    for hh in range(per_group):
        h = group * per_group + hh
        parts.append(_swa_place(ref[:, (h // 2) * LANES:(h // 2 + 1) * LANES], h, group, sel))
    stacked = jnp.concatenate(parts, axis=0)
    return stacked if scale == 1.0 else stacked * scale


def _swa_unstack(stacked, group, per_group, pieces):
    for hh in range(per_group):
        h = group * per_group + hh
        piece = stacked[hh * SWA_BLOCK:(hh + 1) * SWA_BLOCK, :]
        pieces[h] = pltpu.roll(piece, HEAD_DIM, axis=1) if (h % 2) != group else piece


def _swa_sink_rows(sk_ref, group, per_group):
    rowh = lax.broadcasted_iota(jnp.int32, (per_group * SWA_BLOCK, 1), 0) // SWA_BLOCK
    sink = jnp.zeros((per_group * SWA_BLOCK, 1), F32) + sk_ref[0, group * per_group]
    for hh in range(1, per_group):
        sink = jnp.where(rowh == hh, sk_ref[0, group * per_group + hh], sink)
    return sink


def _swa_probs(q_pos, kcat, bias_h, first_mask, sink):
    logits = _dot_nt(q_pos, kcat) + (bias_h + first_mask)
    m = jnp.maximum(jnp.max(logits, axis=1, keepdims=True), sink)
    p = jnp.exp(logits - m)
    es = jnp.exp(sink - m)
    inv = 1.0 / (_rowsum(p) + es)
    return p * inv, es * inv


def _swa_steps(s):
    per_step = min(SWA_BLOCKS_PER_STEP, s // SWA_BLOCK)
    return per_step, s // (SWA_BLOCK * per_step)


def _swa_specs(n_heads, qcol, kcol, vcol, per_step):
    width = n_heads * HEAD_DIM
    prev = lambda col: pl.BlockSpec((SWA_BLOCK, LANES), lambda i: (jnp.maximum(i * per_step - 1, 0), col))
    cur = lambda col: pl.BlockSpec((per_step * SWA_BLOCK, LANES), lambda i: (i, col))
    return [pl.BlockSpec((per_step * SWA_BLOCK, width), lambda i: (i, qcol)),
            prev(kcol), cur(kcol), prev(vcol), cur(vcol),
            _full((SWA_BLOCK, 2 * SWA_BLOCK)),
            pl.BlockSpec(memory_space=pltpu.SMEM), pl.BlockSpec(memory_space=pltpu.SMEM)]


def _swa_stage_keys(prev_ref, cur_ref, all_ref):
    all_ref[:SWA_BLOCK, :] = prev_ref[...]
    all_ref[SWA_BLOCK:, :] = cur_ref[...]


def _swa_kept_specs(nb, n_groups, per_group, per_step):
    rows = per_group * SWA_BLOCK
    at = lambda i: (i, 0, 0, 0)
    specs = [pl.BlockSpec((per_step, n_groups, rows, 2 * SWA_BLOCK), at), pl.BlockSpec((per_step, n_groups, rows, 1), at)]
    shapes = [jax.ShapeDtypeStruct((nb, n_groups, rows, 2 * SWA_BLOCK), F32), jax.ShapeDtypeStruct((nb, n_groups, rows, 1), F32)]
    return specs, shapes


def _swa_fwd(proj, bucket, rel_bias, sinks, n_heads, qcol, kcol, vcol, comm=None):
    s = proj.shape[0]
    width = n_heads * HEAD_DIM
    n_groups = LANES // HEAD_DIM
    per_group = n_heads // n_groups

    per_step, n_steps = _swa_steps(s)

    def body(q_ref, kp_ref, kc_ref, vp_ref, vc_ref, bucket_ref, rb_ref, sk_ref, o_ref, prob_ref, psink_ref,
             bias_ref, kall_ref, vall_ref):
        step = pl.program_id(0)

        @pl.when(step == 0)
        def _():
            _swa_build_bias(bucket_ref, rb_ref, bias_ref, n_groups, per_group)

        _swa_stage_keys(kp_ref, kc_ref, kall_ref)
        _swa_stage_keys(vp_ref, vc_ref, vall_ref)
        lane = lax.broadcasted_iota(jnp.int32, (1, LANES), 1)
        first = lane < HEAD_DIM

        def query_block(j, _):
            rows = pl.ds(pl.multiple_of(j * SWA_BLOCK, SWA_BLOCK), SWA_BLOCK)
            band = pl.ds(pl.multiple_of(j * SWA_BLOCK, SWA_BLOCK), 2 * SWA_BLOCK)
            first_mask = _swa_first_block_mask(step * per_step + j)
            q_blk = q_ref[rows, :]
            kcat, vcat = kall_ref[band, :], vall_ref[band, :]
            pieces = {}
            for g in range(n_groups):
                sel = first if g == 0 else jnp.logical_not(first)
                prob, p_sink = _swa_probs(_swa_stack(q_blk, g, per_group, sel, SCALE), kcat, bias_ref[g], first_mask,
                                          _swa_sink_rows(sk_ref, g, per_group))
                prob_ref[j, g] = prob
                psink_ref[j, g] = p_sink
                _swa_unstack(_dot(prob.astype(BF16), vcat), g, per_group, pieces)
            for c in range(n_heads // 2):
                o_ref[rows, c * LANES:(c + 1) * LANES] = jnp.where(first, pieces[2 * c], pieces[2 * c + 1])
            return 0

        lax.fori_loop(0, per_step, query_block, 0)

    kept_specs, kept_shapes = _swa_kept_specs(s // SWA_BLOCK, n_groups, per_group, per_step)
    staged = pltpu.VMEM(((per_step + 1) * SWA_BLOCK, LANES), BF16)
    outs, landed = _call(
        body, "swa_fwd", (n_steps,),
        in_specs=_swa_specs(n_heads, qcol, kcol, vcol, per_step),
        out_specs=[pl.BlockSpec((per_step * SWA_BLOCK, width), lambda i: (i, 0))] + kept_specs,
        out_shape=[jax.ShapeDtypeStruct((s, width), F32)] + kept_shapes,
        args=(proj, proj, proj, proj, proj, bucket, rel_bias, sinks),
        scratch_shapes=[pltpu.VMEM((n_groups, per_group * SWA_BLOCK, 2 * SWA_BLOCK), F32), staged, staged], comm=comm)
    return outs[0], outs[1:], landed


def _rms_fwd(o, g):
    r = lax.rsqrt(jnp.mean(o * o, axis=-1, keepdims=True) + RMS_EPS)
    n = o * r
    return n, r, n * g


def _mix_ffn(sb_out, sw_out, x, g_in, b_in, sb_g, sw_g, w_out, g1, b1, w_gu_t, w_down, g2, b2, target):
    s, d = x.shape
    wsb, wsw = sb_out.shape[1], sw_out.shape[1]
    dff = w_down.shape[0]
    assert wsb + wsw == d
    tm = min(FFN_TILE, s)

    def body(sb_ref, sw_ref, x_ref, gi_ref, bi_ref, sbg_ref, swg_ref, wo_hbm, g1_ref, b1_ref, wgu_hbm, wd_hbm,
             g2_ref, b2_ref, t_ref,
             du1_ref, mg_ref, h1b_ref, act_ref, dgu_ref, du2b_ref, dsb_ref, dsw_ref, st_ref,
             wo_ref, wgu_ref, wd_ref):
        @pl.when(pl.program_id(0) == 0)
        def _():
            pltpu.sync_copy(wo_hbm, wo_ref)
            pltpu.sync_copy(wgu_hbm, wgu_ref)
            pltpu.sync_copy(wd_hbm, wd_ref)
            st_ref[...] = jnp.zeros_like(st_ref)

        sb, sw = sb_ref[...], sw_ref[...]
        _, _, m_sb = _rms_fwd(sb, sbg_ref[...])
        _, _, m_sw = _rms_fwd(sw, swg_ref[...])
        m_sb = m_sb.astype(BF16)
        m_sw = m_sw.astype(BF16)
        mg_ref[:, :wsb] = m_sb
        mg_ref[:, wsb:] = m_sw
        xhat0, _ = _ln_hat(x_ref[...])
        u1 = ALPHA * (xhat0 * gi_ref[...] + bi_ref[...]) + _dot(m_sb, wo_ref[:wsb, :]) + _dot(m_sw, wo_ref[wsb:, :])

        xhat1, r1 = _ln_hat(u1)
        h1 = xhat1 * g1_ref[...] + b1_ref[...]
        h1b = h1.astype(BF16)
        h1b_ref[...] = h1b
        gate = _dot_nt(h1b, wgu_ref[:dff, :])
        up = _dot_nt(h1b, wgu_ref[dff:, :])
        sg = jax.nn.sigmoid(gate)
        silu = gate * sg
        act = (silu * up).astype(BF16)
        act_ref[...] = act
        u2 = ALPHA * h1 + _dot(act, wd_ref[...])
        xhat2, r2 = _ln_hat(u2)
        diff = xhat2 * g2_ref[...] + b2_ref[...] - t_ref[...]
        dh2 = diff * (1.0 / d)
        st_ref[0:1, :] += _colsum(dh2 * xhat2)
        st_ref[1:2, :] += _colsum(dh2)
        st_ref[2:3, :] += jnp.broadcast_to(_colsum(_rowsum(diff * diff)) * (0.5 / d), (1, d))
        du2 = _ln_bwd(dh2 * g2_ref[...], xhat2, r2)
        du2b = du2.astype(BF16)
        du2b_ref[...] = du2b
        dact = _dot_nt(du2b, wd_ref[...])
        dgate = (dact * up * (sg * (1.0 + gate * (1.0 - sg)))).astype(BF16)
        dup = (dact * silu).astype(BF16)
        dgu_ref[:, :dff] = dgate
        dgu_ref[:, dff:] = dup
        dh1 = _dot(dgate, wgu_ref[:dff, :]) + _dot(dup, wgu_ref[dff:, :]) + ALPHA * du2
        st_ref[3:4, :] += _colsum(dh1 * xhat1)
        st_ref[4:5, :] += _colsum(dh1)
        du1 = _ln_bwd(dh1 * g1_ref[...], xhat1, r1)
        du1_ref[...] = du1

        dmerged = _dot_nt(du1.astype(BF16), wo_ref[...])
        dsb, gsb = _rms_bwd(dmerged[:, :wsb], sb, sbg_ref[...])
        dsw, gsw = _rms_bwd(dmerged[:, wsb:], sw, swg_ref[...])
        dsb_ref[...] = dsb.astype(BF16)
        dsw_ref[...] = dsw.astype(BF16)
        st_ref[5:6, :wsb] += gsb
        st_ref[5:6, wsb:] += gsw

    row = lambda width: pl.BlockSpec((tm, width), lambda i: (i, 0))
    vec = lambda width: _full((1, width))
    hbm = pl.BlockSpec(memory_space=pl.ANY)
    bf = lambda width: jax.ShapeDtypeStruct((s, width), BF16)
    return pl.pallas_call(
        body, name="mix_ffn", grid=(s // tm,),
        in_specs=[row(wsb), row(wsw), row(d), vec(d), vec(d), vec(wsb), vec(wsw), hbm, vec(d), vec(d), hbm, hbm,
                  vec(d), vec(d), row(d)],
        out_specs=[row(d), row(d), row(d), row(dff), row(2 * dff), row(d), row(wsb), row(wsw), _full((8, d))],
        out_shape=[jax.ShapeDtypeStruct((s, d), F32), bf(d), bf(d), bf(dff), bf(2 * dff), bf(d), bf(wsb), bf(wsw),
                   jax.ShapeDtypeStruct((8, d), F32)],
        scratch_shapes=[pltpu.VMEM(w_out.shape, BF16), pltpu.VMEM(w_gu_t.shape, BF16), pltpu.VMEM(w_down.shape, BF16)],
        compiler_params=_params(60),
    )(sb_out, sw_out, x, g_in, b_in, sb_g, sw_g, w_out, g1, b1, w_gu_t, w_down, g2, b2, target)


def _rms_bwd(dm, o, g):
    n, r, _ = _rms_fwd(o, g)
    dn = dm * g
    return r * (dn - n * jnp.mean(dn * n, axis=-1, keepdims=True)), _colsum(dm * n)


def _sb_bwd(proj, dout, out, first, n_pairs, comm=None):
    s = proj.shape[0]
    t, per_step, n_steps = _sb_grid(s)
    width = n_pairs * LANES

    def body(q_ref, k_ref, v_ref, do_ref, o_ref, ab_ref, beta_ref, cl_ref, dq_ref, dk_out, dv_out, dk_ref, dv_ref):
        step = pl.program_id(1)

        @pl.when(step == 0)
        def _():
            dk_ref[...] = jnp.zeros_like(dk_ref)
            dv_ref[...] = jnp.zeros_like(dv_ref)

        lane = lax.broadcasted_iota(jnp.int32, (1, LANES), 1)
        first_lanes = lane < HEAD_DIM
        upper, incl = _sb_triangles(t)

        def query_tile(j, _):
            rows = pl.ds(pl.multiple_of(j * t, t), t)
            do2 = do_ref[rows, :]
            qs = _sb_stack_heads(q_ref[rows, :] * SCALE, first_lanes)
            dos = _sb_stack_heads(do2, first_lanes)
            prod = do2.astype(F32) * o_ref[rows, :]
            totals = jnp.concatenate([_rowsum(jnp.where(first_lanes, prod, 0.0)),
                                      _rowsum(jnp.where(first_lanes, 0.0, prod))], axis=0)

            def grads(offs, k_t, v_t, a_b, beta, c_e, dq):
                d_e = _dot_nt(dos, v_t) * a_b.astype(F32)
                d_eb = d_e.astype(BF16)
                cols = [slice(c * t, (c + 1) * t) for c in range(len(offs))]
                suf_e, c_e = _sb_suffix([[d_eb[:, c]] for c in cols], [_rowsum(d_e[:, c]) for c in cols], incl, c_e)
                dzb = (d_e - beta * (d_e + (totals - suf_e))).astype(BF16)
                dk_t = _dot_tn(dzb, qs)
                dv_t = _dot_tn(a_b, dos)
                for off, c in zip(offs, cols):
                    dk_ref[pl.ds(off, t), :] += dk_t[c, :]
                    dv_ref[pl.ds(off, t), :] += dv_t[c, :]
                return c_e, dq + _dot(dzb, k_t)

            def first_visit(offs, carry):
                _, c_e, dq = carry
                k_t = _sb_key_tiles(k_ref, offs, t)
                v_t = _sb_key_tiles(v_ref, offs, t)
                c_e, dq = grads(offs, k_t, v_t, ab_ref[j], beta_ref[j], c_e, dq)
                return cl_ref[j], c_e, dq

            def visit(offs, carry):
                c_l, c_e, dq = carry
                k_t = _sb_key_tiles(k_ref, offs, t)
                v_t = _sb_key_tiles(v_ref, offs, t)
                lb, a, c_l = _sb_scores(qs, k_t, upper, c_l, None, t)
                c_e, dq = grads(offs, k_t, v_t, a.astype(BF16), jnp.exp(lb), c_e, dq)
                return c_l, c_e, dq

            init = (jnp.zeros((2 * t, 1), F32), jnp.zeros((2 * t, 1), F32), jnp.zeros((2 * t, LANES), F32))
            _, _, dq = _sb_walk(step * per_step + j, t, first_visit, visit, init)
            dq_ref[rows, :] = (jnp.where(first_lanes, dq[:t], dq[t:]) * SCALE).astype(BF16)
            return 0

        lax.fori_loop(0, per_step, query_tile, 0)

        @pl.when(step == n_steps - 1)
        def _():
            dk_out[...] = dk_ref[...].astype(BF16)
            dv_out[...] = dv_ref[...].astype(BF16)

    qblk = pl.BlockSpec((t * per_step, LANES), lambda h, i: (i, h))
    whole = pl.BlockSpec((s, LANES), lambda h, i: (0, h))
    first_specs, _ = _sb_first_specs(n_pairs, n_steps, per_step, t)
    return _call(
        body, "sb_bwd", (n_pairs, n_steps),
        in_specs=[qblk,
                  pl.BlockSpec((s, LANES), lambda h, i: (0, n_pairs + h)),
                  pl.BlockSpec((s, LANES), lambda h, i: (0, 2 * n_pairs + h)),
                  qblk, qblk] + first_specs,
        out_specs=[qblk, whole, whole],
        out_shape=[jax.ShapeDtypeStruct((s, width), BF16)] * 3,
        args=(proj, proj, proj, dout, out, *first),
        scratch_shapes=[pltpu.VMEM((s, LANES), F32), pltpu.VMEM((s, LANES), F32)], comm=comm)


def _swa_bwd(proj, dout, kept, bucket, n_heads, qcol, kcol, vcol, comm=None):
    s = proj.shape[0]
    width = n_heads * HEAD_DIM
    n_groups = LANES // HEAD_DIM
    per_group = n_heads // n_groups
    nb = s // SWA_BLOCK

    per_step, n_steps = _swa_steps(s)

    def body(q_ref, kp_ref, kc_ref, vp_ref, vc_ref, bucket_ref, do_ref, prob_ref, psink_ref,
             dq_ref, dk_out, dv_out, dsk_ref, drb_ref, dbias_ref, dk_ref, dv_ref, kall_ref, vall_ref):
        step = pl.program_id(0)

        @pl.when(step == 0)
        def _():
            dbias_ref[...] = jnp.zeros_like(dbias_ref)
            dk_ref[...] = jnp.zeros_like(dk_ref)
            dv_ref[...] = jnp.zeros_like(dv_ref)
            dsk_ref[...] = jnp.zeros_like(dsk_ref)

        _swa_stage_keys(kp_ref, kc_ref, kall_ref)
        _swa_stage_keys(vp_ref, vc_ref, vall_ref)
        lane = lax.broadcasted_iota(jnp.int32, (1, LANES), 1)
        first = lane < HEAD_DIM

        def query_block(j, _):
            i = step * per_step + j
            rows = pl.ds(pl.multiple_of(j * SWA_BLOCK, SWA_BLOCK), SWA_BLOCK)
            band = pl.ds(pl.multiple_of(j * SWA_BLOCK, SWA_BLOCK), 2 * SWA_BLOCK)
            q_blk, do_blk = q_ref[rows, :], do_ref[rows, :]
            kcat, vcat = kall_ref[band, :], vall_ref[band, :]
            dkcat = jnp.zeros((2 * SWA_BLOCK, LANES), F32)
            dvcat = jnp.zeros((2 * SWA_BLOCK, LANES), F32)
            pieces = {}
            for g in range(n_groups):
                sel = first if g == 0 else jnp.logical_not(first)
                q_g = _swa_stack(q_blk, g, per_group, sel, SCALE)
                do_g = _swa_stack(do_blk, g, per_group, sel)
                prob, p_sink = prob_ref[j, g], psink_ref[j, g]
                dprob = _dot_nt(do_g, vcat)
                delta = _rowsum(prob * dprob)
                dlog = prob * (dprob - delta)
                sink_term = p_sink * delta
                for hh in range(per_group):
                    h = g * per_group + hh
                    tot = _colsum(sink_term[hh * SWA_BLOCK:(hh + 1) * SWA_BLOCK, :])
                    dsk_ref[h:h + 1, :] += jnp.broadcast_to(-tot, (1, LANES))
                dbias_ref[g] += dlog
                dlb = dlog.astype(BF16)
                _swa_unstack(_dot(dlb, kcat) * SCALE, g, per_group, pieces)
                dkcat += _dot_tn(dlb, q_g)
                dvcat += _dot_tn(prob.astype(BF16), do_g)
            for c in range(n_heads // 2):
                dq_ref[rows, c * LANES:(c + 1) * LANES] = jnp.where(first, pieces[2 * c], pieces[2 * c + 1]).astype(BF16)

            cur = pl.multiple_of(i * SWA_BLOCK, SWA_BLOCK)
            dk_ref[pl.ds(cur, SWA_BLOCK), :] += dkcat[SWA_BLOCK:, :]
            dv_ref[pl.ds(cur, SWA_BLOCK), :] += dvcat[SWA_BLOCK:, :]

            @pl.when(i > 0)
            def _():
                prv = pl.multiple_of((i - 1) * SWA_BLOCK, SWA_BLOCK)
                dk_ref[pl.ds(prv, SWA_BLOCK), :] += dkcat[:SWA_BLOCK, :]
                dv_ref[pl.ds(prv, SWA_BLOCK), :] += dvcat[:SWA_BLOCK, :]

            return 0

        lax.fori_loop(0, per_step, query_block, 0)

        @pl.when(step == n_steps - 1)
        def _():
            bk = bucket_ref[...]
            rowi = lax.broadcasted_iota(jnp.int32, (REL_BUCKETS, LANES), 0)
            coli = lax.broadcasted_iota(jnp.int32, (REL_BUCKETS, LANES), 1)
            res = jnp.zeros((REL_BUCKETS, LANES), F32)
            for h in range(n_heads):
                g, hh = divmod(h, per_group)
                db = dbias_ref[g, hh * SWA_BLOCK:(hh + 1) * SWA_BLOCK, :]
                for b in range(REL_BUCKETS):
                    tot = _colsum(_rowsum(jnp.where(bk == b, db, 0.0)))
                    res = jnp.where((rowi == b) & (coli == h), tot, res)
            drb_ref[...] = res
            dk_out[...] = dk_ref[...].astype(BF16)
            dv_out[...] = dv_ref[...].astype(BF16)

    kept_specs, _ = _swa_kept_specs(nb, n_groups, per_group, per_step)
    rows_spec = pl.BlockSpec((per_step * SWA_BLOCK, width), lambda i: (i, 0))
    in_specs = _swa_specs(n_heads, qcol, kcol, vcol, per_step)[:6] + [rows_spec] + kept_specs
    staged = pltpu.VMEM(((per_step + 1) * SWA_BLOCK, LANES), BF16)
    return _call(
        body, "swa_bwd", (n_steps,),
        in_specs=in_specs,
        out_specs=[rows_spec, _full((s, LANES)), _full((s, LANES)), _full((8, LANES)), _full((REL_BUCKETS, LANES))],
        out_shape=[jax.ShapeDtypeStruct((s, width), BF16), jax.ShapeDtypeStruct((s, LANES), BF16),
                   jax.ShapeDtypeStruct((s, LANES), BF16), jax.ShapeDtypeStruct((8, LANES), F32),
                   jax.ShapeDtypeStruct((REL_BUCKETS, LANES), F32)],
        args=(proj, proj, proj, proj, proj, bucket, dout, *kept),
        scratch_shapes=[pltpu.VMEM((n_groups, per_group * SWA_BLOCK, 2 * SWA_BLOCK), F32),
                        pltpu.VMEM((s, LANES), F32), pltpu.VMEM((s, LANES), F32), staged, staged],
        comm=comm)


def _proj_bwd(pieces, w_in_t, du1, x, g_in, comm=None):
    s, d = x.shape
    cols = w_in_t.shape[0]
    tm = min(ROW_TILE, s)
    n_p = len(pieces)

    def body(*refs):
        p_refs = refs[:n_p]
        w_ref, du_ref, x_ref, g_ref, dx_ref, st_ref = refs[n_p:]
        i = pl.program_id(0)

        @pl.when(i == 0)
        def _():
            st_ref[...] = jnp.zeros_like(st_ref)

        dproj = jnp.concatenate([p[...] for p in p_refs], axis=1)
        dh0 = _dot(dproj, w_ref[...]) + ALPHA * du_ref[...]
        xhat, r = _ln_hat(x_ref[...])
        st_ref[0:1, :] += _colsum(dh0 * xhat)
        st_ref[1:2, :] += _colsum(dh0)
        dx_ref[...] = _ln_bwd(dh0 * g_ref[...], xhat, r)

    row = lambda width: pl.BlockSpec((tm, width), lambda i: (i, 0))
    return _call(
        body, "proj_bwd", (s // tm,),
        in_specs=[row(p.shape[1]) for p in pieces] + [_full((cols, d)), row(d), row(d), _full((1, d))],
        out_specs=[row(d), _full((8, d))],
        out_shape=[jax.ShapeDtypeStruct((s, d), F32), jax.ShapeDtypeStruct((8, d), F32)],
        args=(*pieces, w_in_t, du1, x, g_in), comm=comm)


def _wgrad(name, pieces, b, tm, tn):
    s, n = b.shape
    m = sum(p.shape[1] for p in pieces)
    n_p = len(pieces)
    assert n_p == 1 or tm == m
    ts = min(WGRAD_TOKENS if b.dtype == BF16 and n_p == 1 else WGRAD_TOKENS // 2, s)
    n_k = s // ts

    def body(*refs):
        p_refs, b_ref, o_ref, acc_ref = refs[:n_p], refs[n_p], refs[n_p + 1], refs[n_p + 2]
        k = pl.program_id(2)

        @pl.when(k == 0)
        def _():
            acc_ref[...] = jnp.zeros_like(acc_ref)

        a = p_refs[0][...] if n_p == 1 else jnp.concatenate([p[...] for p in p_refs], axis=1)
        acc_ref[...] += _dot_tn(a, b_ref[...].astype(BF16))

        @pl.when(k == n_k - 1)
        def _():
            o_ref[...] = acc_ref[...].astype(BF16)

    piece_spec = lambda p: pl.BlockSpec((ts, tm if n_p == 1 else p.shape[1]), lambda i, j, k: (k, i))
    return pl.pallas_call(
        body, name=name, grid=(m // tm, n // tn, n_k),
        in_specs=[piece_spec(p) for p in pieces] + [pl.BlockSpec((ts, tn), lambda i, j, k: (k, j))],
        out_specs=pl.BlockSpec((tm, tn), lambda i, j, k: (i, j)),
        out_shape=jax.ShapeDtypeStruct((m, n), BF16),
        scratch_shapes=[pltpu.VMEM((tm, tn), F32)],
        compiler_params=_params(),
    )(*pieces, b)


def _adamw_math(w, g, m, v):
    m = ADAM_B1 * m + (1.0 - ADAM_B1) * g
    v = ADAM_B2 * v + (1.0 - ADAM_B2) * (g * g)
    m_hat = m / (1.0 - ADAM_B1 ** ADAM_STEP)
    v_hat = v / (1.0 - ADAM_B2 ** ADAM_STEP)
    delta = -ADAM_LR * (m_hat / (jnp.sqrt(v_hat) + ADAM_EPS) + ADAM_WD * w)
    return delta, m, v


def _adamw_rows(rows):
    return max(r for r in range(16, 257, 16) if rows % r == 0)


def _adamw(name, landed, w, m, v, tr):
    rows, cols = w.shape

    def body(l_ref, w_ref, m_ref, v_ref, g_ref, d_ref, nm_ref, nv_ref):
        g = l_ref[0].astype(F32)
        for src in range(1, N_DEV):
            g = g + l_ref[src].astype(F32)
        delta, nm, nv = _adamw_math(w_ref[...], g, m_ref[...], v_ref[...])
        g_ref[...] = g
        d_ref[...] = delta
        nm_ref[...] = nm
        nv_ref[...] = nv

    blk = pl.BlockSpec((tr, cols), lambda i: (i, 0))
    shape = jax.ShapeDtypeStruct((rows, cols), F32)
    return pl.pallas_call(
        body, name=name, grid=(rows // tr,),
        in_specs=[pl.BlockSpec((N_DEV, tr, cols), lambda i: (0, i, 0)), blk, blk, blk],
        out_specs=[blk, blk, blk, blk],
        out_shape=[shape, shape, shape, shape],
        compiler_params=_params(),
    )(landed, w, m, v)


def _pack(d, ln_in_g, ln_in_b, ln1_g, ln1_b, ln2_g, ln2_b, sb_g, sw_g, rel_bias, sinks, extra=None):
    tail = [rel_bias.reshape(-1), sinks.reshape(-1)]
    if extra is not None:
        tail.append(extra.reshape(-1))
    tail = jnp.concatenate(tail)
    tail = jnp.concatenate([tail, jnp.zeros((d - tail.shape[0],), F32)])
    rows = [ln_in_g.reshape(-1), ln_in_b.reshape(-1), ln1_g.reshape(-1), ln1_b.reshape(-1),
            ln2_g.reshape(-1), ln2_b.reshape(-1),
            jnp.concatenate([sb_g.reshape(-1), sw_g.reshape(-1)]), tail]
    return jnp.stack(rows)


def _unpack(p, wsb, n_rb, n_sk):
    return [p[0], p[1], p[6, :wsb][None], p[6, wsb:][None], p[7, n_rb:n_rb + n_sk][None],
            p[7, :n_rb].reshape(REL_BUCKETS, -1), p[2][None], p[3][None], p[4][None], p[5][None]]


def kernel(x, ln_in_g, ln_in_b, w_in, sb_norm_g, swa_norm_g, sinks, rel_bias, w_out, ln1_g, ln1_b, w_gate_up, w_down, ln2_g, ln2_b, loss_target, m_ln_in_g, m_ln_in_b, m_w_in, m_sb_norm_g, m_swa_norm_g, m_sinks, m_rel_bias, m_w_out, m_ln1_g, m_ln1_b, m_w_gate_up, m_w_down, m_ln2_g, m_ln2_b, v_ln_in_g, v_ln_in_b, v_w_in, v_sb_norm_g, v_swa_norm_g, v_sinks, v_rel_bias, v_w_out, v_ln1_g, v_ln1_b, v_w_gate_up, v_w_down, v_ln2_g, v_ln2_b):
    x2 = x[0]
    tgt = loss_target[0]
    s, d = x2.shape
    wsb = sb_norm_g.shape[-1]
    wsw = swa_norm_g.shape[-1]
    n_sw_heads = sinks.shape[-1]
    n_pairs = wsb // LANES
    dff = w_down.shape[1] * N_DEV
    assert wsb % LANES == 0 and wsw % LANES == 0 and n_sw_heads * HEAD_DIM == wsw
    assert 3 * wsb % wsw == 0 and dff % LANES == 0 and s % SWA_BLOCK == 0
    qcol = 3 * wsb // wsw
    kcol = (3 * wsb + wsw) // LANES
    vcol = kcol + 1
    assert w_in.shape[-1] * N_DEV == (vcol + 1) * LANES

    t2 = lambda a: jnp.transpose(a[0])
    big_w = [t2(w_in), w_out[0], t2(w_gate_up), w_down[0]]
    big_m = [t2(m_w_in), m_w_out[0], t2(m_w_gate_up), m_w_down[0]]
    big_v = [t2(v_w_in), v_w_out[0], t2(v_w_gate_up), v_w_down[0]]

    cat_rows = lambda g: g.reshape(N_DEV * g.shape[1], g.shape[2])
    shards = [w.astype(BF16) for w in big_w]
    w_in_t = cat_rows(_allgather_via_sibling("w_in_allgather", shards[0]))

    vec = lambda a: a.reshape(1, -1)
    g_in, b_in = vec(ln_in_g), vec(ln_in_b)
    bucket = jnp.asarray(_swa_bucket_table())

    h0b, proj = _ln_proj(x2, g_in, b_in, w_in_t)
    sb_out, sb_first, gathered = _sb_fwd(proj, n_pairs, comm=(shards[1:3], ["gather"] * 2))
    w_out_f, w_gu_t = cat_rows(gathered[0]), cat_rows(gathered[1])
    sw_out, sw_kept, gathered = _swa_fwd(proj, bucket, rel_bias, sinks, n_sw_heads, qcol, kcol, vcol,
                                comm=(shards[3:], ["gather"]))
    w_down_f = cat_rows(gathered[0])
    du1, merged, h1b, act, dgu, du2b, dsb, dsw, st_ffn = _mix_ffn(
        sb_out, sw_out, x2, g_in, b_in, sb_norm_g, swa_norm_g, w_out_f, ln1_g, ln1_b, w_gu_t, w_down_f, ln2_g, ln2_b, tgt)

    split_rows = lambda g: g.reshape(N_DEV, g.shape[0] // N_DEV, g.shape[1])
    gw_gu = _wgrad("wgrad_gate_up", [dgu], h1b, dff // 2, d)
    gw_down = _wgrad("wgrad_down", [act], du2b, dff // 2, d)
    gw_out = _wgrad("wgrad_out", [merged], du1, min(512, d), d)
    (dq_sb, dk_sb, dv_sb), (land_gu, land_out) = _sb_bwd(
        proj, dsb, sb_out, sb_first, n_pairs, comm=([split_rows(gw_gu), split_rows(gw_out)], ["scatter"] * 2))
    (dq_sw, dk_sw, dv_sw, st_sink, st_rb), (land_down,) = _swa_bwd(
        proj, dsw, sw_kept, bucket, n_sw_heads, qcol, kcol, vcol, comm=([split_rows(gw_down)], ["scatter"]))
    pieces = [dq_sb, dk_sb, dv_sb, dq_sw, dk_sw, dv_sw]
    gw_in = _wgrad("wgrad_in", pieces, h0b, proj.shape[1], d)
    (grad_x, st_in), (land_in,) = _proj_bwd(pieces, w_in_t, du1, x2, g_in, comm=([split_rows(gw_in)], ["scatter"]))

    n_rb = rel_bias.size
    small = _pack(d, st_in[0], st_in[1], st_ffn[3], st_ffn[4], st_ffn[0], st_ffn[1],
                  st_ffn[5, :wsb], st_ffn[5, wsb:], st_rb[:, :n_sw_heads], st_sink[:n_sw_heads, 0],
                  extra=st_ffn[2, 0:1])
    land_small = _exchange("small_grads_allgather", [small], ["gather"])[0]
    landed = [land_in, land_out, land_gu, land_down, land_small]

    big = []
    for name, land, w, m, v in zip(["adamw_in", "adamw_out", "adamw_gate_up", "adamw_down"], landed[:4], big_w, big_m, big_v):
        big.append(_adamw(name, land, w, m, v, _adamw_rows(w.shape[0])))

    small_w = _pack(d, ln_in_g, ln_in_b, ln1_g, ln1_b, ln2_g, ln2_b, sb_norm_g, swa_norm_g, rel_bias, sinks)
    small_m = _pack(d, m_ln_in_g, m_ln_in_b, m_ln1_g, m_ln1_b, m_ln2_g, m_ln2_b, m_sb_norm_g, m_swa_norm_g, m_rel_bias, m_sinks)
    small_v = _pack(d, v_ln_in_g, v_ln_in_b, v_ln1_g, v_ln1_b, v_ln2_g, v_ln2_b, v_sb_norm_g, v_swa_norm_g, v_rel_bias, v_sinks)
    sg, sd, sm, sv = _adamw("adamw_small", landed[4], small_w, small_m, small_v, 8)
    n_sk = sinks.size
    loss = sg[7, n_rb + n_sk]

    def leaves(idx):
        sm_l = _unpack([sg, sd, sm, sv][idx], wsb, n_rb, n_sk)
        bg = [jnp.transpose(big[0][idx])[None], big[1][idx][None], jnp.transpose(big[2][idx])[None], big[3][idx][None]]
        return [sm_l[0], sm_l[1], bg[0], sm_l[2], sm_l[3], sm_l[4], sm_l[5], bg[1], sm_l[6], sm_l[7], bg[2], bg[3], sm_l[8], sm_l[9]]

    return (loss, grad_x[None], *leaves(0), *leaves(1), *leaves(2), *leaves(3))
```

```python
import functools
import math

import numpy as np
import jax
import jax.numpy as jnp
from jax import lax
from jax.experimental import pallas as pl
from jax.experimental.pallas import tpu as pltpu

F32 = jnp.float32
BF16 = jnp.bfloat16
MESH = pl.DeviceIdType.MESH

N_DEV = 8
LANES = 128
HEAD_DIM = 64
SCALE = HEAD_DIM ** -0.5
SWA_BLOCK = 128
SWA_BLOCKS_PER_STEP = 4
REL_BUCKETS = 32
REL_MAX_DIST = 128
ALPHA = 2.0 ** 0.25
LN_EPS = 1e-5
RMS_EPS = 1e-6
ADAM_LR = 0.001
ADAM_B1 = 0.9
ADAM_B2 = 0.999
ADAM_EPS = 1e-08
ADAM_WD = 0.01
ADAM_STEP = 10

ROW_TILE = 512
SB_TILE = 256
SB_TILES_PER_STEP = 4
FFN_TILE = 256
WGRAD_TOKENS = 2048
SB_UNDERFLOW = -110.0
MIB = 1024 * 1024


def _params(vmem_mib=48):
    return pltpu.CompilerParams(vmem_limit_bytes=vmem_mib * MIB)


def _dot(a, b):
    return jnp.dot(a, b, preferred_element_type=F32)


def _dot_nt(a, b):
    return lax.dot_general(a, b, (((1,), (1,)), ((), ())), preferred_element_type=F32)


def _dot_tn(a, b):
    return lax.dot_general(a, b, (((0,), (0,)), ((), ())), preferred_element_type=F32)


def _ln_hat(x):
    mu = jnp.mean(x, axis=-1, keepdims=True)
    xc = x - mu
    var = jnp.mean(xc * xc, axis=-1, keepdims=True)
    r = lax.rsqrt(var + LN_EPS)
    return xc * r, r


def _ln_bwd(dxhat, xhat, r):
    return r * (dxhat - jnp.mean(dxhat, axis=-1, keepdims=True)
                - xhat * jnp.mean(dxhat * xhat, axis=-1, keepdims=True))


def _colsum(a):
    return jnp.sum(a, axis=0, keepdims=True)


def _rowsum(a):
    return jnp.sum(a, axis=1, keepdims=True)


def _full(shape):
    return pl.BlockSpec(shape, lambda *_: (0,) * len(shape))


def _comm_out_shapes(arrays, kinds):
    shapes = []
    for a, kind in zip(arrays, kinds):
        blk = a.shape if kind == "gather" else a.shape[1:]
        shapes.append(jax.ShapeDtypeStruct((N_DEV,) + tuple(blk), a.dtype))
    return shapes


def _comm_sems(n):
    return [pltpu.SemaphoreType.DMA((n, N_DEV - 1)), pltpu.SemaphoreType.DMA((n, N_DEV - 1)),
            pltpu.SemaphoreType.DMA((n,))]


def _comm_copies(ins, outs, kinds, send_sems, recv_sems, local_sems):
    x, y, c = lax.axis_index("x"), lax.axis_index("y"), lax.axis_index("c")
    me = 4 * x + 2 * y + c

    def src_for(t, dev_lin):
        return ins[t] if kinds[t] == "gather" else ins[t].at[dev_lin]

    local = [pltpu.make_async_copy(src_for(t, me), outs[t].at[me], local_sems.at[t]) for t in range(len(kinds))]
    sends, arrivals = [], []
    for k in range(1, N_DEV):
        px = 1 - x if (k >> 2) & 1 else x
        py = 1 - y if (k >> 1) & 1 else y
        pc = 1 - c if k & 1 else c
        peer_lin = 4 * px + 2 * py + pc
        for t in range(len(kinds)):
            sems = dict(send_sem=send_sems.at[t, k - 1], recv_sem=recv_sems.at[t, k - 1],
                        device_id=(px, py, pc), device_id_type=MESH)
            sends.append(pltpu.make_async_remote_copy(src_ref=src_for(t, peer_lin), dst_ref=outs[t].at[me], **sems))
            arrivals.append(pltpu.make_async_remote_copy(src_ref=src_for(t, peer_lin), dst_ref=outs[t].at[peer_lin], **sems))
    return local, sends, arrivals


def _comm_start(ins, outs, kinds, sems):
    local, sends, _ = _comm_copies(ins, outs, kinds, *sems)
    for cp in local + sends:
        cp.start()


def _comm_finish(ins, outs, kinds, sems):
    local, sends, arrivals = _comm_copies(ins, outs, kinds, *sems)
    for cp in arrivals:
        cp.wait_recv()
    for cp in sends:
        cp.wait_send()
    for cp in local:
        cp.wait()


def _exchange(name, arrays, kinds):
    n = len(arrays)

    def body(*refs):
        ins, outs, sems = refs[:n], refs[n:2 * n], refs[2 * n:]
        _comm_start(ins, outs, kinds, sems)
        _comm_finish(ins, outs, kinds, sems)

    any_spec = pl.BlockSpec(memory_space=pl.ANY)
    return pl.pallas_call(
        body, name=name, out_shape=_comm_out_shapes(arrays, kinds),
        in_specs=[any_spec] * n, out_specs=[any_spec] * n,
        scratch_shapes=_comm_sems(n),
    )(*arrays)


def _allgather_via_sibling(name, shard):
    def body(x_ref, out_ref, send_sems, recv_sems, local_sem):
        x, y, c = lax.axis_index("x"), lax.axis_index("y"), lax.axis_index("c")
        me, sibling = (x, y, c), (x, y, 1 - c)
        chips = [(1 - x, y), (x, 1 - y), (1 - x, 1 - y)]

        def copy(k, block, to, src=None):
            slot = out_ref.at[4 * block[0] + 2 * block[1] + block[2]]
            return pltpu.make_async_remote_copy(
                src_ref=slot if src is None else src, dst_ref=slot,
                send_sem=send_sems.at[k], recv_sem=recv_sems.at[k], device_id=to, device_id_type=MESH)

        mine = pltpu.make_async_copy(x_ref, out_ref.at[4 * x + 2 * y + c], local_sem)
        mine.start()
        first = [copy(0, me, sibling, src=x_ref)]
        first += [copy(1 + j, me, (*chip, c), src=x_ref) for j, chip in enumerate(chips)]
        for cp in first:
            cp.start()
        passed = [copy(4 + j, (*chip, c), sibling) for j, chip in enumerate(chips)]
        for j, chip in enumerate(chips):
            copy(1 + j, (*chip, c), me).wait_recv()
            passed[j].start()
        copy(0, sibling, me).wait_recv()
        for j, chip in enumerate(chips):
            copy(4 + j, (*chip, 1 - c), me).wait_recv()
        for cp in first + passed:
            cp.wait_send()
        mine.wait()

    any_spec = pl.BlockSpec(memory_space=pl.ANY)
    return pl.pallas_call(
        body, name=name, out_shape=jax.ShapeDtypeStruct((N_DEV,) + shard.shape, shard.dtype),
        in_specs=[any_spec], out_specs=any_spec,
        scratch_shapes=[pltpu.SemaphoreType.DMA((N_DEV - 1,)), pltpu.SemaphoreType.DMA((N_DEV - 1,)),
                        pltpu.SemaphoreType.DMA],
    )(shard)


def _call(body, name, grid, in_specs, out_specs, out_shape, args, scratch_shapes=(), comm=None):
    if comm is None:
        outs = pl.pallas_call(body, name=name, grid=grid, in_specs=in_specs, out_specs=out_specs,
                              out_shape=out_shape, scratch_shapes=list(scratch_shapes),
                              compiler_params=_params())(*args)
        return outs, []
    arrays, kinds = comm
    n, n_in, n_out, n_scr = len(arrays), len(in_specs), len(out_specs), len(scratch_shapes)

    def fused(*refs):
        c_in, x_in = refs[:n_in], refs[n_in:n_in + n]
        c_out = refs[n_in + n:n_in + n + n_out]
        x_out = refs[n_in + n + n_out:n_in + 2 * n + n_out]
        rest = refs[n_in + 2 * n + n_out:]
        c_scr, sems = rest[:n_scr], rest[n_scr:]
        ids = [pl.program_id(a) for a in range(len(grid))]
        is_first = functools.reduce(jnp.logical_and, [i == 0 for i in ids])
        is_last = functools.reduce(jnp.logical_and, [i == g - 1 for i, g in zip(ids, grid)])

        @pl.when(is_first)
        def _():
            _comm_start(x_in, x_out, kinds, sems)

        body(*c_in, *c_out, *c_scr)

        @pl.when(is_last)
        def _():
            _comm_finish(x_in, x_out, kinds, sems)

    any_spec = pl.BlockSpec(memory_space=pl.ANY)
    outs = pl.pallas_call(
        fused, name=name, grid=grid,
        in_specs=list(in_specs) + [any_spec] * n, out_specs=list(out_specs) + [any_spec] * n,
        out_shape=list(out_shape) + _comm_out_shapes(arrays, kinds),
        scratch_shapes=list(scratch_shapes) + _comm_sems(n),
        compiler_params=_params())(*args, *arrays)
    return outs[:n_out], outs[n_out:]


def _ln_proj(x, g, b, w_in_t):
    s, d = x.shape
    cols = w_in_t.shape[0]
    tm = min(ROW_TILE, s)

    def body(x_ref, g_ref, b_ref, w_ref, h_ref, p_ref):
        xhat, _ = _ln_hat(x_ref[...])
        h = (xhat * g_ref[...] + b_ref[...]).astype(BF16)
        h_ref[...] = h
        p_ref[...] = _dot_nt(h, w_ref[...]).astype(BF16)

    row = lambda width: pl.BlockSpec((tm, width), lambda i: (i, 0))
    return pl.pallas_call(
        body, name="ln_proj", grid=(s // tm,),
        in_specs=[row(d), _full((1, d)), _full((1, d)), _full((cols, d))],
        out_specs=[row(d), row(cols)],
        out_shape=[jax.ShapeDtypeStruct((s, d), BF16), jax.ShapeDtypeStruct((s, cols), BF16)],
        compiler_params=_params(),
    )(x, g, b, w_in_t)


def _sb_triangles(t):
    row = lax.broadcasted_iota(jnp.int32, (t, t), 0)
    col = lax.broadcasted_iota(jnp.int32, (t, t), 1)
    return (row > col).astype(BF16), (row >= col).astype(BF16)


def _sb_first_mask(t, has_prev):
    qrow = lax.broadcasted_iota(jnp.int32, (2 * t, 2 * t), 0) & (t - 1)
    col = lax.broadcasted_iota(jnp.int32, (2 * t, 2 * t), 1)
    return ((col < t) & has_prev) | ((col >= t) & (col - t < qrow))


def _sb_stack_heads(x2, first):
    zero = jnp.zeros_like(x2)
    return jnp.concatenate([jnp.where(first, x2, zero), jnp.where(first, zero, x2)], axis=0)


def _sb_key_tiles(ref, offs, t):
    tiles = [ref[pl.ds(off, t), :] for off in offs]
    return tiles[0] if len(tiles) == 1 else jnp.concatenate(tiles, axis=0)


def _sb_suffix(terms, row_sums, tri, carry):
    out = [None] * len(terms)
    for j in reversed(range(len(terms))):
        suf = carry
        for op in terms[j]:
            suf = suf + _dot(op, tri)
        out[j] = suf
        carry = carry + row_sums[j]
    return (out[0] if len(out) == 1 else jnp.concatenate(out, axis=1)), carry


def _sb_scores(qh, k_t, upper, carry_l, mask, t):
    z = _dot_nt(qh, k_t)
    sp = jnp.log(1.0 + jnp.exp(-jnp.abs(z)))
    neg = jnp.minimum(z, 0.0)
    lb = neg - sp
    l1 = (neg - z) - sp
    if mask is not None:
        l1 = jnp.where(mask, l1, 0.0)
    hi = l1.astype(BF16)
    lo = (l1 - hi.astype(F32)).astype(BF16)
    cols = [slice(j * t, (j + 1) * t) for j in range(z.shape[1] // t)]
    suf, carry_l = _sb_suffix([[hi[:, c], lo[:, c]] for c in cols], [_rowsum(l1[:, c]) for c in cols], upper, carry_l)
    a = jnp.exp(lb + suf)
    if mask is not None:
        a = jnp.where(mask, a, 0.0)
    return lb, a, carry_l


def _sb_walk(i, t, first_visit, visit, init):
    def alive(carry):
        return jnp.max(carry[0]) > SB_UNDERFLOW

    prev = pl.multiple_of(jnp.maximum(i - 1, 0) * t, t)
    carry = first_visit((prev, pl.multiple_of(i * t, t)), init)

    def cond(state):
        j, go, _ = state
        return (j < i - 1) & go

    def body(state):
        j, _, carry = state
        carry = visit((pl.multiple_of((i - 2 - j) * t, t),), carry)
        return j + 1, alive(carry), carry

    return lax.while_loop(cond, body, (jnp.int32(0), alive(carry), carry))[2]


def _sb_first_specs(n_pairs, n_steps, per_step, t):
    at = lambda h, i: (h * n_steps + i, 0, 0)
    n_tiles = n_pairs * n_steps * per_step
    specs = [pl.BlockSpec((per_step, 2 * t, 2 * t), at), pl.BlockSpec((per_step, 2 * t, 2 * t), at),
             pl.BlockSpec((per_step, 2 * t, 1), at)]
    shapes = [jax.ShapeDtypeStruct((n_tiles, 2 * t, 2 * t), BF16), jax.ShapeDtypeStruct((n_tiles, 2 * t, 2 * t), F32),
              jax.ShapeDtypeStruct((n_tiles, 2 * t, 1), F32)]
    return specs, shapes


def _sb_grid(s):
    t = min(SB_TILE, s)
    per_step = min(SB_TILES_PER_STEP, s // t)
    return t, per_step, s // (t * per_step)


def _sb_fwd(proj, n_pairs, comm=None):
    s = proj.shape[0]
    t, per_step, n_steps = _sb_grid(s)

    def body(q_ref, k_ref, v_ref, o_ref, ab_ref, beta_ref, cl_ref):
        lane = lax.broadcasted_iota(jnp.int32, (1, LANES), 1)
        first = lane < HEAD_DIM
        upper, _ = _sb_triangles(t)

        def query_tile(j, _):
            i = pl.program_id(1) * per_step + j
            rows = pl.ds(pl.multiple_of(j * t, t), t)
            qs = _sb_stack_heads(q_ref[rows, :] * SCALE, first)

            def first_visit(offs, carry):
                c_l, acc = carry
                mask = _sb_first_mask(t, i > 0)
                lb, a, c_l = _sb_scores(qs, _sb_key_tiles(k_ref, offs, t), upper, c_l, mask, t)
                a_b = a.astype(BF16)
                ab_ref[j] = a_b
                beta_ref[j] = jnp.where(mask, jnp.exp(lb), 0.0)
                cl_ref[j] = c_l
                return c_l, acc + _dot(a_b, _sb_key_tiles(v_ref, offs, t))

            def visit(offs, carry):
                c_l, acc = carry
                _, a, c_l = _sb_scores(qs, _sb_key_tiles(k_ref, offs, t), upper, c_l, None, t)
                return c_l, acc + _dot(a.astype(BF16), _sb_key_tiles(v_ref, offs, t))

            init = (jnp.zeros((2 * t, 1), F32), jnp.zeros((2 * t, LANES), F32))
            _, acc = _sb_walk(i, t, first_visit, visit, init)
            o_ref[rows, :] = jnp.where(first, acc[:t], acc[t:])
            return 0

        lax.fori_loop(0, per_step, query_tile, 0)

    qblk = pl.BlockSpec((t * per_step, LANES), lambda h, i: (i, h))
    first_specs, first_shapes = _sb_first_specs(n_pairs, n_steps, per_step, t)
    outs, landed = _call(
        body, "sb_fwd", (n_pairs, n_steps),
        in_specs=[qblk,
                  pl.BlockSpec((s, LANES), lambda h, i: (0, n_pairs + h)),
                  pl.BlockSpec((s, LANES), lambda h, i: (0, 2 * n_pairs + h))],
        out_specs=[qblk] + first_specs,
        out_shape=[jax.ShapeDtypeStruct((s, n_pairs * LANES), F32)] + first_shapes,
        args=(proj, proj, proj), comm=comm)
    return outs[0], outs[1:], landed


def _swa_bucket_table():
    qi = np.arange(SWA_BLOCK)[:, None]
    cj = np.arange(2 * SWA_BLOCK)[None, :]
    dist = qi + SWA_BLOCK - cj
    exact = REL_BUCKETS // 2
    d = np.maximum(dist, 0)
    d_f = np.maximum(d, 1).astype(np.float32)
    large = exact + (np.log(d_f / np.float32(exact)) / np.float32(math.log(REL_MAX_DIST / exact))
                     * np.float32(REL_BUCKETS - exact)).astype(np.int32)
    large = np.minimum(large, REL_BUCKETS - 1)
    return np.where(d < exact, d, large).astype(np.int32)


def _swa_build_bias(bucket_ref, rb_ref, bias_ref, n_groups, per_group):
    bk = bucket_ref[...]
    dist = (lax.broadcasted_iota(jnp.int32, bk.shape, 0) + SWA_BLOCK) - lax.broadcasted_iota(jnp.int32, bk.shape, 1)
    window = (dist >= 0) & (dist < SWA_BLOCK)
    for g in range(n_groups):
        for hh in range(per_group):
            acc = jnp.zeros(bk.shape, F32)
            for b in range(REL_BUCKETS):
                acc = jnp.where(bk == b, rb_ref[b, g * per_group + hh], acc)
            bias_ref[g, hh * SWA_BLOCK:(hh + 1) * SWA_BLOCK, :] = jnp.where(window, acc, -jnp.inf)


def _swa_first_block_mask(i):
    col = lax.broadcasted_iota(jnp.int32, (1, 2 * SWA_BLOCK), 1)
    return jnp.where((col < SWA_BLOCK) & (i == 0), -jnp.inf, 0.0)


def _swa_place(blk, h, group, sel):
    if (h % 2) != group:
        blk = pltpu.roll(blk.astype(F32), HEAD_DIM, axis=1).astype(BF16)
    return jnp.where(sel, blk, jnp.zeros_like(blk))


def _swa_stack(ref, group, per_group, sel, scale=1.0):
    parts = []
    for hh in range(per_group):
        h = group * per_group + hh
        parts.append(_swa_place(ref[:, (h // 2) * LANES:(h // 2 + 1) * LANES], h, group, sel))
    stacked = jnp.concatenate(parts, axis=0)
    return stacked if scale == 1.0 else stacked * scale


def _swa_unstack(stacked, group, per_group, pieces):
    for hh in range(per_group):
        h = group * per_group + hh
        piece = stacked[hh * SWA_BLOCK:(hh + 1) * SWA_BLOCK, :]
        pieces[h] = pltpu.roll(piece, HEAD_DIM, axis=1) if (h % 2) != group else piece


def _swa_sink_rows(sk_ref, group, per_group):
    rowh = lax.broadcasted_iota(jnp.int32, (per_group * SWA_BLOCK, 1), 0) // SWA_BLOCK
    sink = jnp.zeros((per_group * SWA_BLOCK, 1), F32) + sk_ref[0, group * per_group]
    for hh in range(1, per_group):
        sink = jnp.where(rowh == hh, sk_ref[0, group * per_group + hh], sink)
    return sink


def _swa_probs(q_pos, kcat, bias_h, first_mask, sink):
    logits = _dot_nt(q_pos, kcat) + (bias_h + first_mask)
    m = jnp.maximum(jnp.max(logits, axis=1, keepdims=True), sink)
    p = jnp.exp(logits - m)
    es = jnp.exp(sink - m)
    inv = 1.0 / (_rowsum(p) + es)
    return p * inv, es * inv


def _swa_steps(s):
    per_step = min(SWA_BLOCKS_PER_STEP, s // SWA_BLOCK)
    return per_step, s // (SWA_BLOCK * per_step)


def _swa_specs(n_heads, qcol, kcol, vcol, per_step):
    width = n_heads * HEAD_DIM
    prev = lambda col: pl.BlockSpec((SWA_BLOCK, LANES), lambda i: (jnp.maximum(i * per_step - 1, 0), col))
    cur = lambda col: pl.BlockSpec((per_step * SWA_BLOCK, LANES), lambda i: (i, col))
    return [pl.BlockSpec((per_step * SWA_BLOCK, width), lambda i: (i, qcol)),
            prev(kcol), cur(kcol), prev(vcol), cur(vcol),
            _full((SWA_BLOCK, 2 * SWA_BLOCK)),
            pl.BlockSpec(memory_space=pltpu.SMEM), pl.BlockSpec(memory_space=pltpu.SMEM)]


def _swa_stage_keys(prev_ref, cur_ref, all_ref):
    all_ref[:SWA_BLOCK, :] = prev_ref[...]
    all_ref[SWA_BLOCK:, :] = cur_ref[...]


def _swa_kept_specs(nb, n_groups, per_group, per_step):
    rows = per_group * SWA_BLOCK
    at = lambda i: (i, 0, 0, 0)
    specs = [pl.BlockSpec((per_step, n_groups, rows, 2 * SWA_BLOCK), at), pl.BlockSpec((per_step, n_groups, rows, 1), at)]
    shapes = [jax.ShapeDtypeStruct((nb, n_groups, rows, 2 * SWA_BLOCK), F32), jax.ShapeDtypeStruct((nb, n_groups, rows, 1), F32)]
    return specs, shapes


def _swa_fwd(proj, bucket, rel_bias, sinks, n_heads, qcol, kcol, vcol, comm=None):
    s = proj.shape[0]
    width = n_heads * HEAD_DIM
    n_groups = LANES // HEAD_DIM
    per_group = n_heads // n_groups

    per_step, n_steps = _swa_steps(s)

    def body(q_ref, kp_ref, kc_ref, vp_ref, vc_ref, bucket_ref, rb_ref, sk_ref, o_ref, prob_ref, psink_ref,
             bias_ref, kall_ref, vall_ref):
        step = pl.program_id(0)

        @pl.when(step == 0)
        def _():
            _swa_build_bias(bucket_ref, rb_ref, bias_ref, n_groups, per_group)

        _swa_stage_keys(kp_ref, kc_ref, kall_ref)
        _swa_stage_keys(vp_ref, vc_ref, vall_ref)
        lane = lax.broadcasted_iota(jnp.int32, (1, LANES), 1)
        first = lane < HEAD_DIM

        def query_block(j, _):
            rows = pl.ds(pl.multiple_of(j * SWA_BLOCK, SWA_BLOCK), SWA_BLOCK)
            band = pl.ds(pl.multiple_of(j * SWA_BLOCK, SWA_BLOCK), 2 * SWA_BLOCK)
            first_mask = _swa_first_block_mask(step * per_step + j)
            q_blk = q_ref[rows, :]
            kcat, vcat = kall_ref[band, :], vall_ref[band, :]
            pieces = {}
            for g in range(n_groups):
                sel = first if g == 0 else jnp.logical_not(first)
                prob, p_sink = _swa_probs(_swa_stack(q_blk, g, per_group, sel, SCALE), kcat, bias_ref[g], first_mask,
                                          _swa_sink_rows(sk_ref, g, per_group))
                prob_ref[j, g] = prob
                psink_ref[j, g] = p_sink
                _swa_unstack(_dot(prob.astype(BF16), vcat), g, per_group, pieces)
            for c in range(n_heads // 2):
                o_ref[rows, c * LANES:(c + 1) * LANES] = jnp.where(first, pieces[2 * c], pieces[2 * c + 1])
            return 0

        lax.fori_loop(0, per_step, query_block, 0)

    kept_specs, kept_shapes = _swa_kept_specs(s // SWA_BLOCK, n_groups, per_group, per_step)
    staged = pltpu.VMEM(((per_step + 1) * SWA_BLOCK, LANES), BF16)
    outs, landed = _call(
        body, "swa_fwd", (n_steps,),
        in_specs=_swa_specs(n_heads, qcol, kcol, vcol, per_step),
        out_specs=[pl.BlockSpec((per_step * SWA_BLOCK, width), lambda i: (i, 0))] + kept_specs,
        out_shape=[jax.ShapeDtypeStruct((s, width), F32)] + kept_shapes,
        args=(proj, proj, proj, proj, proj, bucket, rel_bias, sinks),
        scratch_shapes=[pltpu.VMEM((n_groups, per_group * SWA_BLOCK, 2 * SWA_BLOCK), F32), staged, staged], comm=comm)
    return outs[0], outs[1:], landed


def _rms_fwd(o, g):
    r = lax.rsqrt(jnp.mean(o * o, axis=-1, keepdims=True) + RMS_EPS)
    n = o * r
    return n, r, n * g


def _mix_ffn(sb_out, sw_out, x, g_in, b_in, sb_g, sw_g, w_out, g1, b1, w_gu_t, w_down, g2, b2, target):
    s, d = x.shape
    wsb, wsw = sb_out.shape[1], sw_out.shape[1]
    dff = w_down.shape[0]
    assert wsb + wsw == d
    tm = min(FFN_TILE, s)

    def body(sb_ref, sw_ref, x_ref, gi_ref, bi_ref, sbg_ref, swg_ref, wo_hbm, g1_ref, b1_ref, wgu_hbm, wd_hbm,
             g2_ref, b2_ref, t_ref,
             du1_ref, mg_ref, h1b_ref, act_ref, dgu_ref, du2b_ref, dsb_ref, dsw_ref, st_ref,
             wo_ref, wgu_ref, wd_ref):
        @pl.when(pl.program_id(0) == 0)
        def _():
            pltpu.sync_copy(wo_hbm, wo_ref)
            pltpu.sync_copy(wgu_hbm, wgu_ref)
            pltpu.sync_copy(wd_hbm, wd_ref)
            st_ref[...] = jnp.zeros_like(st_ref)

        sb, sw = sb_ref[...], sw_ref[...]
        _, _, m_sb = _rms_fwd(sb, sbg_ref[...])
        _, _, m_sw = _rms_fwd(sw, swg_ref[...])
        m_sb = m_sb.astype(BF16)
        m_sw = m_sw.astype(BF16)
        mg_ref[:, :wsb] = m_sb
        mg_ref[:, wsb:] = m_sw
        xhat0, _ = _ln_hat(x_ref[...])
        u1 = ALPHA * (xhat0 * gi_ref[...] + bi_ref[...]) + _dot(m_sb, wo_ref[:wsb, :]) + _dot(m_sw, wo_ref[wsb:, :])

        xhat1, r1 = _ln_hat(u1)
        h1 = xhat1 * g1_ref[...] + b1_ref[...]
        h1b = h1.astype(BF16)
        h1b_ref[...] = h1b
        gate = _dot_nt(h1b, wgu_ref[:dff, :])
        up = _dot_nt(h1b, wgu_ref[dff:, :])
        sg = jax.nn.sigmoid(gate)
        silu = gate * sg
        act = (silu * up).astype(BF16)
        act_ref[...] = act
        u2 = ALPHA * h1 + _dot(act, wd_ref[...])
        xhat2, r2 = _ln_hat(u2)
        diff = xhat2 * g2_ref[...] + b2_ref[...] - t_ref[...]
        dh2 = diff * (1.0 / d)
        st_ref[0:1, :] += _colsum(dh2 * xhat2)
        st_ref[1:2, :] += _colsum(dh2)
        st_ref[2:3, :] += jnp.broadcast_to(_colsum(_rowsum(diff * diff)) * (0.5 / d), (1, d))
        du2 = _ln_bwd(dh2 * g2_ref[...], xhat2, r2)
        du2b = du2.astype(BF16)
        du2b_ref[...] = du2b
        dact = _dot_nt(du2b, wd_ref[...])
        dgate = (dact * up * (sg * (1.0 + gate * (1.0 - sg)))).astype(BF16)
        dup = (dact * silu).astype(BF16)
        dgu_ref[:, :dff] = dgate
        dgu_ref[:, dff:] = dup
        dh1 = _dot(dgate, wgu_ref[:dff, :]) + _dot(dup, wgu_ref[dff:, :]) + ALPHA * du2
        st_ref[3:4, :] += _colsum(dh1 * xhat1)
        st_ref[4:5, :] += _colsum(dh1)
        du1 = _ln_bwd(dh1 * g1_ref[...], xhat1, r1)
        du1_ref[...] = du1

        dmerged = _dot_nt(du1.astype(BF16), wo_ref[...])
        dsb, gsb = _rms_bwd(dmerged[:, :wsb], sb, sbg_ref[...])
        dsw, gsw = _rms_bwd(dmerged[:, wsb:], sw, swg_ref[...])
        dsb_ref[...] = dsb.astype(BF16)
        dsw_ref[...] = dsw.astype(BF16)
        st_ref[5:6, :wsb] += gsb
        st_ref[5:6, wsb:] += gsw

    row = lambda width: pl.BlockSpec((tm, width), lambda i: (i, 0))
    vec = lambda width: _full((1, width))
    hbm = pl.BlockSpec(memory_space=pl.ANY)
    bf = lambda width: jax.ShapeDtypeStruct((s, width), BF16)
    return pl.pallas_call(
        body, name="mix_ffn", grid=(s // tm,),
        in_specs=[row(wsb), row(wsw), row(d), vec(d), vec(d), vec(wsb), vec(wsw), hbm, vec(d), vec(d), hbm, hbm,
                  vec(d), vec(d), row(d)],
        out_specs=[row(d), row(d), row(d), row(dff), row(2 * dff), row(d), row(wsb), row(wsw), _full((8, d))],
        out_shape=[jax.ShapeDtypeStruct((s, d), F32), bf(d), bf(d), bf(dff), bf(2 * dff), bf(d), bf(wsb), bf(wsw),
                   jax.ShapeDtypeStruct((8, d), F32)],
        scratch_shapes=[pltpu.VMEM(w_out.shape, BF16), pltpu.VMEM(w_gu_t.shape, BF16), pltpu.VMEM(w_down.shape, BF16)],
        compiler_params=_params(60),
    )(sb_out, sw_out, x, g_in, b_in, sb_g, sw_g, w_out, g1, b1, w_gu_t, w_down, g2, b2, target)


def _rms_bwd(dm, o, g):
    n, r, _ = _rms_fwd(o, g)
    dn = dm * g
    return r * (dn - n * jnp.mean(dn * n, axis=-1, keepdims=True)), _colsum(dm * n)


def _sb_bwd(proj, dout, out, first, n_pairs, comm=None):
    s = proj.shape[0]
    t, per_step, n_steps = _sb_grid(s)
    width = n_pairs * LANES

    def body(q_ref, k_ref, v_ref, do_ref, o_ref, ab_ref, beta_ref, cl_ref, dq_ref, dk_out, dv_out, dk_ref, dv_ref):
        step = pl.program_id(1)

        @pl.when(step == 0)
        def _():
            dk_ref[...] = jnp.zeros_like(dk_ref)
            dv_ref[...] = jnp.zeros_like(dv_ref)

        lane = lax.broadcasted_iota(jnp.int32, (1, LANES), 1)
        first_lanes = lane < HEAD_DIM
        upper, incl = _sb_triangles(t)

        def query_tile(j, _):
            rows = pl.ds(pl.multiple_of(j * t, t), t)
            do2 = do_ref[rows, :]
            qs = _sb_stack_heads(q_ref[rows, :] * SCALE, first_lanes)
            dos = _sb_stack_heads(do2, first_lanes)
            prod = do2.astype(F32) * o_ref[rows, :]
            totals = jnp.concatenate([_rowsum(jnp.where(first_lanes, prod, 0.0)),
                                      _rowsum(jnp.where(first_lanes, 0.0, prod))], axis=0)

            def grads(offs, k_t, v_t, a_b, beta, c_e, dq):
                d_e = _dot_nt(dos, v_t) * a_b.astype(F32)
                d_hi = d_e.astype(BF16)
                d_lo = (d_e - d_hi.astype(F32)).astype(BF16)
                cols = [slice(c * t, (c + 1) * t) for c in range(len(offs))]
                suf_e, c_e = _sb_suffix([[d_hi[:, c], d_lo[:, c]] for c in cols], [_rowsum(d_e[:, c]) for c in cols], incl, c_e)
                dzb = (d_e - beta * (d_e + (totals - suf_e))).astype(BF16)
                dk_t = _dot_tn(dzb, qs)
                dv_t = _dot_tn(a_b, dos)
                for off, c in zip(offs, cols):
                    dk_ref[pl.ds(off, t), :] += dk_t[c, :]
                    dv_ref[pl.ds(off, t), :] += dv_t[c, :]
                return c_e, dq + _dot(dzb, k_t)

            def first_visit(offs, carry):
                _, c_e, dq = carry
                k_t = _sb_key_tiles(k_ref, offs, t)
                v_t = _sb_key_tiles(v_ref, offs, t)
                c_e, dq = grads(offs, k_t, v_t, ab_ref[j], beta_ref[j], c_e, dq)
                return cl_ref[j], c_e, dq

            def visit(offs, carry):
                c_l, c_e, dq = carry
                k_t = _sb_key_tiles(k_ref, offs, t)
                v_t = _sb_key_tiles(v_ref, offs, t)
                lb, a, c_l = _sb_scores(qs, k_t, upper, c_l, None, t)
                c_e, dq = grads(offs, k_t, v_t, a.astype(BF16), jnp.exp(lb), c_e, dq)
                return c_l, c_e, dq

            init = (jnp.zeros((2 * t, 1), F32), jnp.zeros((2 * t, 1), F32), jnp.zeros((2 * t, LANES), F32))
            _, _, dq = _sb_walk(step * per_step + j, t, first_visit, visit, init)
            dq_ref[rows, :] = (jnp.where(first_lanes, dq[:t], dq[t:]) * SCALE).astype(BF16)
            return 0

        lax.fori_loop(0, per_step, query_tile, 0)

        @pl.when(step == n_steps - 1)
        def _():
            dk_out[...] = dk_ref[...].astype(BF16)
            dv_out[...] = dv_ref[...].astype(BF16)

    qblk = pl.BlockSpec((t * per_step, LANES), lambda h, i: (i, h))
    whole = pl.BlockSpec((s, LANES), lambda h, i: (0, h))
    first_specs, _ = _sb_first_specs(n_pairs, n_steps, per_step, t)
    return _call(
        body, "sb_bwd", (n_pairs, n_steps),
        in_specs=[qblk,
                  pl.BlockSpec((s, LANES), lambda h, i: (0, n_pairs + h)),
                  pl.BlockSpec((s, LANES), lambda h, i: (0, 2 * n_pairs + h)),
                  qblk, qblk] + first_specs,
        out_specs=[qblk, whole, whole],
        out_shape=[jax.ShapeDtypeStruct((s, width), BF16)] * 3,
        args=(proj, proj, proj, dout, out, *first),
        scratch_shapes=[pltpu.VMEM((s, LANES), F32), pltpu.VMEM((s, LANES), F32)], comm=comm)


def _swa_bwd(proj, dout, kept, bucket, n_heads, qcol, kcol, vcol, comm=None):
    s = proj.shape[0]
    width = n_heads * HEAD_DIM
    n_groups = LANES // HEAD_DIM
    per_group = n_heads // n_groups
    nb = s // SWA_BLOCK

    per_step, n_steps = _swa_steps(s)

    def body(q_ref, kp_ref, kc_ref, vp_ref, vc_ref, bucket_ref, do_ref, prob_ref, psink_ref,
             dq_ref, dk_out, dv_out, dsk_ref, drb_ref, dbias_ref, dk_ref, dv_ref, kall_ref, vall_ref):
        step = pl.program_id(0)

        @pl.when(step == 0)
        def _():
            dbias_ref[...] = jnp.zeros_like(dbias_ref)
            dk_ref[...] = jnp.zeros_like(dk_ref)
            dv_ref[...] = jnp.zeros_like(dv_ref)
            dsk_ref[...] = jnp.zeros_like(dsk_ref)

        _swa_stage_keys(kp_ref, kc_ref, kall_ref)
        _swa_stage_keys(vp_ref, vc_ref, vall_ref)
        lane = lax.broadcasted_iota(jnp.int32, (1, LANES), 1)
        first = lane < HEAD_DIM

        def query_block(j, _):
            i = step * per_step + j
            rows = pl.ds(pl.multiple_of(j * SWA_BLOCK, SWA_BLOCK), SWA_BLOCK)
            band = pl.ds(pl.multiple_of(j * SWA_BLOCK, SWA_BLOCK), 2 * SWA_BLOCK)
            q_blk, do_blk = q_ref[rows, :], do_ref[rows, :]
            kcat, vcat = kall_ref[band, :], vall_ref[band, :]
            dkcat = jnp.zeros((2 * SWA_BLOCK, LANES), F32)
            dvcat = jnp.zeros((2 * SWA_BLOCK, LANES), F32)
            pieces = {}
            for g in range(n_groups):
                sel = first if g == 0 else jnp.logical_not(first)
                q_g = _swa_stack(q_blk, g, per_group, sel, SCALE)
                do_g = _swa_stack(do_blk, g, per_group, sel)
                prob, p_sink = prob_ref[j, g], psink_ref[j, g]
                dprob = _dot_nt(do_g, vcat)
                delta = _rowsum(prob * dprob)
                dlog = prob * (dprob - delta)
                sink_term = p_sink * delta
                for hh in range(per_group):
                    h = g * per_group + hh
                    tot = _colsum(sink_term[hh * SWA_BLOCK:(hh + 1) * SWA_BLOCK, :])
                    dsk_ref[h:h + 1, :] += jnp.broadcast_to(-tot, (1, LANES))
                dbias_ref[g] += dlog
                dlb = dlog.astype(BF16)
                _swa_unstack(_dot(dlb, kcat) * SCALE, g, per_group, pieces)
                dkcat += _dot_tn(dlb, q_g)
                dvcat += _dot_tn(prob.astype(BF16), do_g)
            for c in range(n_heads // 2):
                dq_ref[rows, c * LANES:(c + 1) * LANES] = jnp.where(first, pieces[2 * c], pieces[2 * c + 1]).astype(BF16)

            cur = pl.multiple_of(i * SWA_BLOCK, SWA_BLOCK)
            dk_ref[pl.ds(cur, SWA_BLOCK), :] += dkcat[SWA_BLOCK:, :]
            dv_ref[pl.ds(cur, SWA_BLOCK), :] += dvcat[SWA_BLOCK:, :]

            @pl.when(i > 0)
            def _():
                prv = pl.multiple_of((i - 1) * SWA_BLOCK, SWA_BLOCK)
                dk_ref[pl.ds(prv, SWA_BLOCK), :] += dkcat[:SWA_BLOCK, :]
                dv_ref[pl.ds(prv, SWA_BLOCK), :] += dvcat[:SWA_BLOCK, :]

            return 0

        lax.fori_loop(0, per_step, query_block, 0)

        @pl.when(step == n_steps - 1)
        def _():
            bk = bucket_ref[...]
            rowi = lax.broadcasted_iota(jnp.int32, (REL_BUCKETS, LANES), 0)
            coli = lax.broadcasted_iota(jnp.int32, (REL_BUCKETS, LANES), 1)
            res = jnp.zeros((REL_BUCKETS, LANES), F32)
            for h in range(n_heads):
                g, hh = divmod(h, per_group)
                db = dbias_ref[g, hh * SWA_BLOCK:(hh + 1) * SWA_BLOCK, :]
                for b in range(REL_BUCKETS):
                    tot = _colsum(_rowsum(jnp.where(bk == b, db, 0.0)))
                    res = jnp.where((rowi == b) & (coli == h), tot, res)
            drb_ref[...] = res
            dk_out[...] = dk_ref[...].astype(BF16)
            dv_out[...] = dv_ref[...].astype(BF16)

    kept_specs, _ = _swa_kept_specs(nb, n_groups, per_group, per_step)
    rows_spec = pl.BlockSpec((per_step * SWA_BLOCK, width), lambda i: (i, 0))
    in_specs = _swa_specs(n_heads, qcol, kcol, vcol, per_step)[:6] + [rows_spec] + kept_specs
    staged = pltpu.VMEM(((per_step + 1) * SWA_BLOCK, LANES), BF16)
    return _call(
        body, "swa_bwd", (n_steps,),
        in_specs=in_specs,
        out_specs=[rows_spec, _full((s, LANES)), _full((s, LANES)), _full((8, LANES)), _full((REL_BUCKETS, LANES))],
        out_shape=[jax.ShapeDtypeStruct((s, width), BF16), jax.ShapeDtypeStruct((s, LANES), BF16),
                   jax.ShapeDtypeStruct((s, LANES), BF16), jax.ShapeDtypeStruct((8, LANES), F32),
                   jax.ShapeDtypeStruct((REL_BUCKETS, LANES), F32)],
        args=(proj, proj, proj, proj, proj, bucket, dout, *kept),
        scratch_shapes=[pltpu.VMEM((n_groups, per_group * SWA_BLOCK, 2 * SWA_BLOCK), F32),
                        pltpu.VMEM((s, LANES), F32), pltpu.VMEM((s, LANES), F32), staged, staged],
        comm=comm)


def _proj_bwd(pieces, w_in_t, du1, x, g_in, comm=None):
    s, d = x.shape
    cols = w_in_t.shape[0]
    tm = min(ROW_TILE, s)
    n_p = len(pieces)

    def body(*refs):
        p_refs = refs[:n_p]
        w_ref, du_ref, x_ref, g_ref, dx_ref, st_ref = refs[n_p:]
        i = pl.program_id(0)

        @pl.when(i == 0)
        def _():
            st_ref[...] = jnp.zeros_like(st_ref)

        dproj = jnp.concatenate([p[...] for p in p_refs], axis=1)
        dh0 = _dot(dproj, w_ref[...]) + ALPHA * du_ref[...]
        xhat, r = _ln_hat(x_ref[...])
        st_ref[0:1, :] += _colsum(dh0 * xhat)
        st_ref[1:2, :] += _colsum(dh0)
        dx_ref[...] = _ln_bwd(dh0 * g_ref[...], xhat, r)

    row = lambda width: pl.BlockSpec((tm, width), lambda i: (i, 0))
    return _call(
        body, "proj_bwd", (s // tm,),
        in_specs=[row(p.shape[1]) for p in pieces] + [_full((cols, d)), row(d), row(d), _full((1, d))],
        out_specs=[row(d), _full((8, d))],
        out_shape=[jax.ShapeDtypeStruct((s, d), F32), jax.ShapeDtypeStruct((8, d), F32)],
        args=(*pieces, w_in_t, du1, x, g_in), comm=comm)


def _wgrad(name, pieces, b, tm, tn):
    s, n = b.shape
    m = sum(p.shape[1] for p in pieces)
    n_p = len(pieces)
    assert n_p == 1 or tm == m
    ts = min(WGRAD_TOKENS if b.dtype == BF16 and n_p == 1 else WGRAD_TOKENS // 2, s)
    n_k = s // ts

    def body(*refs):
        p_refs, b_ref, o_ref, acc_ref = refs[:n_p], refs[n_p], refs[n_p + 1], refs[n_p + 2]
        k = pl.program_id(2)

        @pl.when(k == 0)
        def _():
            acc_ref[...] = jnp.zeros_like(acc_ref)

        a = p_refs[0][...] if n_p == 1 else jnp.concatenate([p[...] for p in p_refs], axis=1)
        acc_ref[...] += _dot_tn(a, b_ref[...].astype(BF16))

        @pl.when(k == n_k - 1)
        def _():
            o_ref[...] = acc_ref[...].astype(BF16)

    piece_spec = lambda p: pl.BlockSpec((ts, tm if n_p == 1 else p.shape[1]), lambda i, j, k: (k, i))
    return pl.pallas_call(
        body, name=name, grid=(m // tm, n // tn, n_k),
        in_specs=[piece_spec(p) for p in pieces] + [pl.BlockSpec((ts, tn), lambda i, j, k: (k, j))],
        out_specs=pl.BlockSpec((tm, tn), lambda i, j, k: (i, j)),
        out_shape=jax.ShapeDtypeStruct((m, n), BF16),
        scratch_shapes=[pltpu.VMEM((tm, tn), F32)],
        compiler_params=_params(),
    )(*pieces, b)


def _adamw_math(w, g, m, v):
    m = ADAM_B1 * m + (1.0 - ADAM_B1) * g
    v = ADAM_B2 * v + (1.0 - ADAM_B2) * (g * g)
    m_hat = m / (1.0 - ADAM_B1 ** ADAM_STEP)
    v_hat = v / (1.0 - ADAM_B2 ** ADAM_STEP)
    delta = -ADAM_LR * (m_hat / (jnp.sqrt(v_hat) + ADAM_EPS) + ADAM_WD * w)
    return delta, m, v


def _adamw_rows(rows):
    return max(r for r in range(16, 257, 16) if rows % r == 0)


def _adamw(name, landed, w, m, v, tr):
    rows, cols = w.shape

    def body(l_ref, w_ref, m_ref, v_ref, g_ref, d_ref, nm_ref, nv_ref):
        g = l_ref[0].astype(F32)
        for src in range(1, N_DEV):
            g = g + l_ref[src].astype(F32)
        delta, nm, nv = _adamw_math(w_ref[...], g, m_ref[...], v_ref[...])
        g_ref[...] = g
        d_ref[...] = delta
        nm_ref[...] = nm
        nv_ref[...] = nv

    blk = pl.BlockSpec((tr, cols), lambda i: (i, 0))
    shape = jax.ShapeDtypeStruct((rows, cols), F32)
    return pl.pallas_call(
        body, name=name, grid=(rows // tr,),
        in_specs=[pl.BlockSpec((N_DEV, tr, cols), lambda i: (0, i, 0)), blk, blk, blk],
        out_specs=[blk, blk, blk, blk],
        out_shape=[shape, shape, shape, shape],
        compiler_params=_params(),
    )(landed, w, m, v)


def _pack(d, ln_in_g, ln_in_b, ln1_g, ln1_b, ln2_g, ln2_b, sb_g, sw_g, rel_bias, sinks, extra=None):
    tail = [rel_bias.reshape(-1), sinks.reshape(-1)]
    if extra is not None:
        tail.append(extra.reshape(-1))
    tail = jnp.concatenate(tail)
    tail = jnp.concatenate([tail, jnp.zeros((d - tail.shape[0],), F32)])
    rows = [ln_in_g.reshape(-1), ln_in_b.reshape(-1), ln1_g.reshape(-1), ln1_b.reshape(-1),
            ln2_g.reshape(-1), ln2_b.reshape(-1),
            jnp.concatenate([sb_g.reshape(-1), sw_g.reshape(-1)]), tail]
    return jnp.stack(rows)


def _unpack(p, wsb, n_rb, n_sk):
    return [p[0], p[1], p[6, :wsb][None], p[6, wsb:][None], p[7, n_rb:n_rb + n_sk][None],
            p[7, :n_rb].reshape(REL_BUCKETS, -1), p[2][None], p[3][None], p[4][None], p[5][None]]


def kernel(x, ln_in_g, ln_in_b, w_in, sb_norm_g, swa_norm_g, sinks, rel_bias, w_out, ln1_g, ln1_b, w_gate_up, w_down, ln2_g, ln2_b, loss_target, m_ln_in_g, m_ln_in_b, m_w_in, m_sb_norm_g, m_swa_norm_g, m_sinks, m_rel_bias, m_w_out, m_ln1_g, m_ln1_b, m_w_gate_up, m_w_down, m_ln2_g, m_ln2_b, v_ln_in_g, v_ln_in_b, v_w_in, v_sb_norm_g, v_swa_norm_g, v_sinks, v_rel_bias, v_w_out, v_ln1_g, v_ln1_b, v_w_gate_up, v_w_down, v_ln2_g, v_ln2_b):
    x2 = x[0]
    tgt = loss_target[0]
    s, d = x2.shape
    wsb = sb_norm_g.shape[-1]
    wsw = swa_norm_g.shape[-1]
    n_sw_heads = sinks.shape[-1]
    n_pairs = wsb // LANES
    dff = w_down.shape[1] * N_DEV
    assert wsb % LANES == 0 and wsw % LANES == 0 and n_sw_heads * HEAD_DIM == wsw
    assert 3 * wsb % wsw == 0 and dff % LANES == 0 and s % SWA_BLOCK == 0
    qcol = 3 * wsb // wsw
    kcol = (3 * wsb + wsw) // LANES
    vcol = kcol + 1
    assert w_in.shape[-1] * N_DEV == (vcol + 1) * LANES

    t2 = lambda a: jnp.transpose(a[0])
    big_w = [t2(w_in), w_out[0], t2(w_gate_up), w_down[0]]
    big_m = [t2(m_w_in), m_w_out[0], t2(m_w_gate_up), m_w_down[0]]
    big_v = [t2(v_w_in), v_w_out[0], t2(v_w_gate_up), v_w_down[0]]

    cat_rows = lambda g: g.reshape(N_DEV * g.shape[1], g.shape[2])
    shards = [w.astype(BF16) for w in big_w]
    w_in_t = cat_rows(_allgather_via_sibling("w_in_allgather", shards[0]))

    vec = lambda a: a.reshape(1, -1)
    g_in, b_in = vec(ln_in_g), vec(ln_in_b)
    bucket = jnp.asarray(_swa_bucket_table())

    h0b, proj = _ln_proj(x2, g_in, b_in, w_in_t)
    sb_out, sb_first, gathered = _sb_fwd(proj, n_pairs, comm=(shards[1:3], ["gather"] * 2))
    w_out_f, w_gu_t = cat_rows(gathered[0]), cat_rows(gathered[1])
    sw_out, sw_kept, gathered = _swa_fwd(proj, bucket, rel_bias, sinks, n_sw_heads, qcol, kcol, vcol,
                                comm=(shards[3:], ["gather"]))
    w_down_f = cat_rows(gathered[0])
    du1, merged, h1b, act, dgu, du2b, dsb, dsw, st_ffn = _mix_ffn(
        sb_out, sw_out, x2, g_in, b_in, sb_norm_g, swa_norm_g, w_out_f, ln1_g, ln1_b, w_gu_t, w_down_f, ln2_g, ln2_b, tgt)

    split_rows = lambda g: g.reshape(N_DEV, g.shape[0] // N_DEV, g.shape[1])
    gw_gu = _wgrad("wgrad_gate_up", [dgu], h1b, dff // 2, d)
    gw_down = _wgrad("wgrad_down", [act], du2b, dff // 2, d)
    gw_out = _wgrad("wgrad_out", [merged], du1, min(512, d), d)
    (dq_sb, dk_sb, dv_sb), (land_gu, land_out) = _sb_bwd(
        proj, dsb, sb_out, sb_first, n_pairs, comm=([split_rows(gw_gu), split_rows(gw_out)], ["scatter"] * 2))
    (dq_sw, dk_sw, dv_sw, st_sink, st_rb), (land_down,) = _swa_bwd(
        proj, dsw, sw_kept, bucket, n_sw_heads, qcol, kcol, vcol, comm=([split_rows(gw_down)], ["scatter"]))
    pieces = [dq_sb, dk_sb, dv_sb, dq_sw, dk_sw, dv_sw]
    gw_in = _wgrad("wgrad_in", pieces, h0b, proj.shape[1], d)
    (grad_x, st_in), (land_in,) = _proj_bwd(pieces, w_in_t, du1, x2, g_in, comm=([split_rows(gw_in)], ["scatter"]))

    n_rb = rel_bias.size
    small = _pack(d, st_in[0], st_in[1], st_ffn[3], st_ffn[4], st_ffn[0], st_ffn[1],
                  st_ffn[5, :wsb], st_ffn[5, wsb:], st_rb[:, :n_sw_heads], st_sink[:n_sw_heads, 0],
                  extra=st_ffn[2, 0:1])
    land_small = _exchange("small_grads_allgather", [small], ["gather"])[0]
    landed = [land_in, land_out, land_gu, land_down, land_small]

    big = []
    for name, land, w, m, v in zip(["adamw_in", "adamw_out", "adamw_gate_up", "adamw_down"], landed[:4], big_w, big_m, big_v):
        big.append(_adamw(name, land, w, m, v, _adamw_rows(w.shape[0])))

    small_w = _pack(d, ln_in_g, ln_in_b, ln1_g, ln1_b, ln2_g, ln2_b, sb_norm_g, swa_norm_g, rel_bias, sinks)
    small_m = _pack(d, m_ln_in_g, m_ln_in_b, m_ln1_g, m_ln1_b, m_ln2_g, m_ln2_b, m_sb_norm_g, m_swa_norm_g, m_rel_bias, m_sinks)
    small_v = _pack(d, v_ln_in_g, v_ln_in_b, v_ln1_g, v_ln1_b, v_ln2_g, v_ln2_b, v_sb_norm_g, v_swa_norm_g, v_rel_bias, v_sinks)
    sg, sd, sm, sv = _adamw("adamw_small", landed[4], small_w, small_m, small_v, 8)
    n_sk = sinks.size
    loss = sg[7, n_rb + n_sk]

    def leaves(idx):
        sm_l = _unpack([sg, sd, sm, sv][idx], wsb, n_rb, n_sk)
        bg = [jnp.transpose(big[0][idx])[None], big[1][idx][None], jnp.transpose(big[2][idx])[None], big[3][idx][None]]
        return [sm_l[0], sm_l[1], bg[0], sm_l[2], sm_l[3], sm_l[4], sm_l[5], bg[1], sm_l[6], sm_l[7], bg[2], bg[3], sm_l[8], sm_l[9]]

    return (loss, grad_x[None], *leaves(0), *leaves(1), *leaves(2), *leaves(3))
```

```python
import functools
import math

import numpy as np
import jax
import jax.numpy as jnp
from jax import lax
from jax.experimental import pallas as pl
from jax.experimental.pallas import tpu as pltpu

F32 = jnp.float32
BF16 = jnp.bfloat16
MESH = pl.DeviceIdType.MESH

N_DEV = 8
LANES = 128
HEAD_DIM = 64
SCALE = HEAD_DIM ** -0.5
SWA_BLOCK = 128
SWA_BLOCKS_PER_STEP = 4
REL_BUCKETS = 32
REL_MAX_DIST = 128
ALPHA = 2.0 ** 0.25
LN_EPS = 1e-5
RMS_EPS = 1e-6
ADAM_LR = 0.001
ADAM_B1 = 0.9
ADAM_B2 = 0.999
ADAM_EPS = 1e-08
ADAM_WD = 0.01
ADAM_STEP = 10

ROW_TILE = 512
SB_TILE = 256
SB_TILES_PER_STEP = 4
FFN_TILE = 256
WGRAD_TOKENS = 2048
SB_UNDERFLOW = -110.0
MIB = 1024 * 1024


def _params(vmem_mib=48):
    return pltpu.CompilerParams(vmem_limit_bytes=vmem_mib * MIB)


def _dot(a, b):
    return jnp.dot(a, b, preferred_element_type=F32)


def _dot_nt(a, b):
    return lax.dot_general(a, b, (((1,), (1,)), ((), ())), preferred_element_type=F32)


def _dot_tn(a, b):
    return lax.dot_general(a, b, (((0,), (0,)), ((), ())), preferred_element_type=F32)


def _ln_hat(x):
    mu = jnp.mean(x, axis=-1, keepdims=True)
    xc = x - mu
    var = jnp.mean(xc * xc, axis=-1, keepdims=True)
    r = lax.rsqrt(var + LN_EPS)
    return xc * r, r


def _ln_bwd(dxhat, xhat, r):
    return r * (dxhat - jnp.mean(dxhat, axis=-1, keepdims=True)
                - xhat * jnp.mean(dxhat * xhat, axis=-1, keepdims=True))


def _colsum(a):
    return jnp.sum(a, axis=0, keepdims=True)


def _rowsum(a):
    return jnp.sum(a, axis=1, keepdims=True)


def _full(shape):
    return pl.BlockSpec(shape, lambda *_: (0,) * len(shape))


def _comm_out_shapes(arrays, kinds):
    shapes = []
    for a, kind in zip(arrays, kinds):
        blk = a.shape if kind == "gather" else a.shape[1:]
        shapes.append(jax.ShapeDtypeStruct((N_DEV,) + tuple(blk), a.dtype))
    return shapes


def _comm_sems(n):
    return [pltpu.SemaphoreType.DMA((n, N_DEV - 1)), pltpu.SemaphoreType.DMA((n, N_DEV - 1)),
            pltpu.SemaphoreType.DMA((n,))]


def _comm_copies(ins, outs, kinds, send_sems, recv_sems, local_sems):
    x, y, c = lax.axis_index("x"), lax.axis_index("y"), lax.axis_index("c")
    me = 4 * x + 2 * y + c

    def src_for(t, dev_lin):
        return ins[t] if kinds[t] == "gather" else ins[t].at[dev_lin]

    local = [pltpu.make_async_copy(src_for(t, me), outs[t].at[me], local_sems.at[t]) for t in range(len(kinds))]
    sends, arrivals = [], []
    for k in range(1, N_DEV):
        px = 1 - x if (k >> 2) & 1 else x
        py = 1 - y if (k >> 1) & 1 else y
        pc = 1 - c if k & 1 else c
        peer_lin = 4 * px + 2 * py + pc
        for t in range(len(kinds)):
            sems = dict(send_sem=send_sems.at[t, k - 1], recv_sem=recv_sems.at[t, k - 1],
                        device_id=(px, py, pc), device_id_type=MESH)
            sends.append(pltpu.make_async_remote_copy(src_ref=src_for(t, peer_lin), dst_ref=outs[t].at[me], **sems))
            arrivals.append(pltpu.make_async_remote_copy(src_ref=src_for(t, peer_lin), dst_ref=outs[t].at[peer_lin], **sems))
    return local, sends, arrivals


def _comm_start(ins, outs, kinds, sems):
    local, sends, _ = _comm_copies(ins, outs, kinds, *sems)
    for cp in local + sends:
        cp.start()


def _comm_finish(ins, outs, kinds, sems):
    local, sends, arrivals = _comm_copies(ins, outs, kinds, *sems)
    for cp in arrivals:
        cp.wait_recv()
    for cp in sends:
        cp.wait_send()
    for cp in local:
        cp.wait()


def _exchange(name, arrays, kinds):
    n = len(arrays)

    def body(*refs):
        ins, outs, sems = refs[:n], refs[n:2 * n], refs[2 * n:]
        _comm_start(ins, outs, kinds, sems)
        _comm_finish(ins, outs, kinds, sems)

    any_spec = pl.BlockSpec(memory_space=pl.ANY)
    return pl.pallas_call(
        body, name=name, out_shape=_comm_out_shapes(arrays, kinds),
        in_specs=[any_spec] * n, out_specs=[any_spec] * n,
        scratch_shapes=_comm_sems(n),
    )(*arrays)


def _allgather_via_sibling(name, shard):
    def body(x_ref, out_ref, send_sems, recv_sems, local_sem):
        x, y, c = lax.axis_index("x"), lax.axis_index("y"), lax.axis_index("c")
        me, sibling = (x, y, c), (x, y, 1 - c)
        chips = [(1 - x, y), (x, 1 - y), (1 - x, 1 - y)]

        def copy(k, block, to, src=None):
            slot = out_ref.at[4 * block[0] + 2 * block[1] + block[2]]
            return pltpu.make_async_remote_copy(
                src_ref=slot if src is None else src, dst_ref=slot,
                send_sem=send_sems.at[k], recv_sem=recv_sems.at[k], device_id=to, device_id_type=MESH)

        mine = pltpu.make_async_copy(x_ref, out_ref.at[4 * x + 2 * y + c], local_sem)
        mine.start()
        first = [copy(0, me, sibling, src=x_ref)]
        first += [copy(1 + j, me, (*chip, c), src=x_ref) for j, chip in enumerate(chips)]
        for cp in first:
            cp.start()
        passed = [copy(4 + j, (*chip, c), sibling) for j, chip in enumerate(chips)]
        for j, chip in enumerate(chips):
            copy(1 + j, (*chip, c), me).wait_recv()
            passed[j].start()
        copy(0, sibling, me).wait_recv()
        for j, chip in enumerate(chips):
            copy(4 + j, (*chip, 1 - c), me).wait_recv()
        for cp in first + passed:
            cp.wait_send()
        mine.wait()

    any_spec = pl.BlockSpec(memory_space=pl.ANY)
    return pl.pallas_call(
        body, name=name, out_shape=jax.ShapeDtypeStruct((N_DEV,) + shard.shape, shard.dtype),
        in_specs=[any_spec], out_specs=any_spec,
        scratch_shapes=[pltpu.SemaphoreType.DMA((N_DEV - 1,)), pltpu.SemaphoreType.DMA((N_DEV - 1,)),
                        pltpu.SemaphoreType.DMA],
    )(shard)


def _call(body, name, grid, in_specs, out_specs, out_shape, args, scratch_shapes=(), comm=None):
    if comm is None:
        outs = pl.pallas_call(body, name=name, grid=grid, in_specs=in_specs, out_specs=out_specs,
                              out_shape=out_shape, scratch_shapes=list(scratch_shapes),
                              compiler_params=_params())(*args)
        return outs, []
    arrays, kinds = comm
    n, n_in, n_out, n_scr = len(arrays), len(in_specs), len(out_specs), len(scratch_shapes)

    def fused(*refs):
        c_in, x_in = refs[:n_in], refs[n_in:n_in + n]
        c_out = refs[n_in + n:n_in + n + n_out]
        x_out = refs[n_in + n + n_out:n_in + 2 * n + n_out]
        rest = refs[n_in + 2 * n + n_out:]
        c_scr, sems = rest[:n_scr], rest[n_scr:]
        ids = [pl.program_id(a) for a in range(len(grid))]
        is_first = functools.reduce(jnp.logical_and, [i == 0 for i in ids])
        is_last = functools.reduce(jnp.logical_and, [i == g - 1 for i, g in zip(ids, grid)])

        @pl.when(is_first)
        def _():
            _comm_start(x_in, x_out, kinds, sems)

        body(*c_in, *c_out, *c_scr)

        @pl.when(is_last)
        def _():
            _comm_finish(x_in, x_out, kinds, sems)

    any_spec = pl.BlockSpec(memory_space=pl.ANY)
    outs = pl.pallas_call(
        fused, name=name, grid=grid,
        in_specs=list(in_specs) + [any_spec] * n, out_specs=list(out_specs) + [any_spec] * n,
        out_shape=list(out_shape) + _comm_out_shapes(arrays, kinds),
        scratch_shapes=list(scratch_shapes) + _comm_sems(n),
        compiler_params=_params())(*args, *arrays)
    return outs[:n_out], outs[n_out:]


def _ln_proj(x, g, b, w_in_t):
    s, d = x.shape
    cols = w_in_t.shape[0]
    tm = min(ROW_TILE, s)

    def body(x_ref, g_ref, b_ref, w_ref, h_ref, p_ref):
        xhat, _ = _ln_hat(x_ref[...])
        h = (xhat * g_ref[...] + b_ref[...]).astype(BF16)
        h_ref[...] = h
        p_ref[...] = _dot_nt(h, w_ref[...]).astype(BF16)

    row = lambda width: pl.BlockSpec((tm, width), lambda i: (i, 0))
    return pl.pallas_call(
        body, name="ln_proj", grid=(s // tm,),
        in_specs=[row(d), _full((1, d)), _full((1, d)), _full((cols, d))],
        out_specs=[row(d), row(cols)],
        out_shape=[jax.ShapeDtypeStruct((s, d), BF16), jax.ShapeDtypeStruct((s, cols), BF16)],
        compiler_params=_params(),
    )(x, g, b, w_in_t)


def _sb_triangles(t):
    row = lax.broadcasted_iota(jnp.int32, (t, t), 0)
    col = lax.broadcasted_iota(jnp.int32, (t, t), 1)
    return (row > col).astype(BF16), (row >= col).astype(BF16)


def _sb_first_mask(t, has_prev):
    qrow = lax.broadcasted_iota(jnp.int32, (2 * t, 2 * t), 0) & (t - 1)
    col = lax.broadcasted_iota(jnp.int32, (2 * t, 2 * t), 1)
    return ((col < t) & has_prev) | ((col >= t) & (col - t < qrow))


def _sb_stack_heads(x2, first):
    zero = jnp.zeros_like(x2)
    return jnp.concatenate([jnp.where(first, x2, zero), jnp.where(first, zero, x2)], axis=0)


def _sb_key_tiles(ref, offs, t):
    tiles = [ref[pl.ds(off, t), :] for off in offs]
    return tiles[0] if len(tiles) == 1 else jnp.concatenate(tiles, axis=0)


def _sb_suffix(terms, row_sums, tri, carry):
    out = [None] * len(terms)
    for j in reversed(range(len(terms))):
        suf = carry
        for op in terms[j]:
            suf = suf + _dot(op, tri)
        out[j] = suf
        carry = carry + row_sums[j]
    return (out[0] if len(out) == 1 else jnp.concatenate(out, axis=1)), carry


def _sb_scores(qh, k_t, upper, carry_l, mask, t):
    z = _dot_nt(qh, k_t)
    sp = jnp.log(1.0 + jnp.exp(-jnp.abs(z)))
    neg = jnp.minimum(z, 0.0)
    lb = neg - sp
    l1 = (neg - z) - sp
    if mask is not None:
        l1 = jnp.where(mask, l1, 0.0)
    hi = l1.astype(BF16)
    lo = (l1 - hi.astype(F32)).astype(BF16)
    cols = [slice(j * t, (j + 1) * t) for j in range(z.shape[1] // t)]
    suf, carry_l = _sb_suffix([[hi[:, c], lo[:, c]] for c in cols], [_rowsum(l1[:, c]) for c in cols], upper, carry_l)
    a = jnp.exp(lb + suf)
    if mask is not None:
        a = jnp.where(mask, a, 0.0)
    return lb, a, carry_l


def _sb_walk(i, t, first_visit, visit, init):
    def alive(carry):
        return jnp.max(carry[0]) > SB_UNDERFLOW

    prev = pl.multiple_of(jnp.maximum(i - 1, 0) * t, t)
    carry, go = first_visit((prev, pl.multiple_of(i * t, t)), init)

    def cond(state):
        j, go, _ = state
        return (j < i - 1) & go

    def body(state):
        j, _, carry = state
        carry = visit((pl.multiple_of((i - 2 - j) * t, t),), carry)
        return j + 1, alive(carry), carry

    return lax.while_loop(cond, body, (jnp.int32(0), go, carry))[2]


def _sb_first_specs(n_pairs, n_steps, per_step, t):
    at = lambda h, i: (h * n_steps + i, 0, 0)
    n_tiles = n_pairs * n_steps * per_step
    specs = [pl.BlockSpec((per_step, 2 * t, 2 * t), at), pl.BlockSpec((per_step, 2 * t, 2 * t), at),
             pl.BlockSpec((per_step, 8, LANES), at)]
    shapes = [jax.ShapeDtypeStruct((n_tiles, 2 * t, 2 * t), BF16), jax.ShapeDtypeStruct((n_tiles, 2 * t, 2 * t), F32),
              jax.ShapeDtypeStruct((n_tiles, 8, LANES), F32)]
    return specs, shapes


def _sb_grid(s):
    t = min(SB_TILE, s)
    per_step = min(SB_TILES_PER_STEP, s // t)
    return t, per_step, s // (t * per_step)


def _sb_fwd(proj, n_pairs, comm=None):
    s = proj.shape[0]
    t, per_step, n_steps = _sb_grid(s)

    def body(q_ref, k_ref, v_ref, o_ref, ab_ref, beta_ref, cmax_ref):
        lane = lax.broadcasted_iota(jnp.int32, (1, LANES), 1)
        first = lane < HEAD_DIM
        upper, _ = _sb_triangles(t)

        def query_tile(j, _):
            i = pl.program_id(1) * per_step + j
            rows = pl.ds(pl.multiple_of(j * t, t), t)
            qs = _sb_stack_heads(q_ref[rows, :] * SCALE, first)

            def first_visit(offs, carry):
                c_l, acc = carry
                mask = _sb_first_mask(t, i > 0)
                lb, a, c_l = _sb_scores(qs, _sb_key_tiles(k_ref, offs, t), upper, c_l, mask, t)
                a_b = a.astype(BF16)
                ab_ref[j] = a_b
                beta_ref[j] = jnp.where(mask, jnp.exp(lb), 0.0)
                c_max = jnp.max(c_l)
                cmax_ref[j] = jnp.broadcast_to(c_max, (8, LANES))
                return (c_l, acc + _dot(a_b, _sb_key_tiles(v_ref, offs, t))), c_max > SB_UNDERFLOW

            def visit(offs, carry):
                c_l, acc = carry
                _, a, c_l = _sb_scores(qs, _sb_key_tiles(k_ref, offs, t), upper, c_l, None, t)
                return c_l, acc + _dot(a.astype(BF16), _sb_key_tiles(v_ref, offs, t))

            init = (jnp.zeros((2 * t, 1), F32), jnp.zeros((2 * t, LANES), F32))
            _, acc = _sb_walk(i, t, first_visit, visit, init)
            o_ref[rows, :] = jnp.where(first, acc[:t], acc[t:])
            return 0

        lax.fori_loop(0, per_step, query_tile, 0)

    qblk = pl.BlockSpec((t * per_step, LANES), lambda h, i: (i, h))
    first_specs, first_shapes = _sb_first_specs(n_pairs, n_steps, per_step, t)
    outs, landed = _call(
        body, "sb_fwd", (n_pairs, n_steps),
        in_specs=[qblk,
                  pl.BlockSpec((s, LANES), lambda h, i: (0, n_pairs + h)),
                  pl.BlockSpec((s, LANES), lambda h, i: (0, 2 * n_pairs + h))],
        out_specs=[qblk] + first_specs,
        out_shape=[jax.ShapeDtypeStruct((s, n_pairs * LANES), F32)] + first_shapes,
        args=(proj, proj, proj), comm=comm)
    return outs[0], outs[1:], landed


def _swa_bucket_table():
    qi = np.arange(SWA_BLOCK)[:, None]
    cj = np.arange(2 * SWA_BLOCK)[None, :]
    dist = qi + SWA_BLOCK - cj
    exact = REL_BUCKETS // 2
    d = np.maximum(dist, 0)
    d_f = np.maximum(d, 1).astype(np.float32)
    large = exact + (np.log(d_f / np.float32(exact)) / np.float32(math.log(REL_MAX_DIST / exact))
                     * np.float32(REL_BUCKETS - exact)).astype(np.int32)
    large = np.minimum(large, REL_BUCKETS - 1)
    return np.where(d < exact, d, large).astype(np.int32)


def _swa_build_bias(bucket_ref, rb_ref, bias_ref, n_groups, per_group):
    bk = bucket_ref[...]
    dist = (lax.broadcasted_iota(jnp.int32, bk.shape, 0) + SWA_BLOCK) - lax.broadcasted_iota(jnp.int32, bk.shape, 1)
    window = (dist >= 0) & (dist < SWA_BLOCK)
    for g in range(n_groups):
        for hh in range(per_group):
            acc = jnp.zeros(bk.shape, F32)
            for b in range(REL_BUCKETS):
                acc = jnp.where(bk == b, rb_ref[b, g * per_group + hh], acc)
            bias_ref[g, hh * SWA_BLOCK:(hh + 1) * SWA_BLOCK, :] = jnp.where(window, acc, -jnp.inf)


def _swa_first_block_mask(i):
    col = lax.broadcasted_iota(jnp.int32, (1, 2 * SWA_BLOCK), 1)
    return jnp.where((col < SWA_BLOCK) & (i == 0), -jnp.inf, 0.0)


def _swa_place(blk, h, group, sel):
    if (h % 2) != group:
        blk = pltpu.roll(blk.astype(F32), HEAD_DIM, axis=1).astype(BF16)
    return jnp.where(sel, blk, jnp.zeros_like(blk))


def _swa_stack(ref, group, per_group, sel, scale=1.0):
    parts = []
    for hh in range(per_group):
        h = group * per_group + hh
        parts.append(_swa_place(ref[:, (h // 2) * LANES:(h // 2 + 1) * LANES], h, group, sel))
    stacked = jnp.concatenate(parts, axis=0)
    return stacked if scale == 1.0 else stacked * scale


def _swa_unstack(stacked, group, per_group, pieces):
    for hh in range(per_group):
        h = group * per_group + hh
        piece = stacked[hh * SWA_BLOCK:(hh + 1) * SWA_BLOCK, :]
        pieces[h] = pltpu.roll(piece, HEAD_DIM, axis=1) if (h % 2) != group else piece


def _swa_sink_rows(sk_ref, group, per_group):
    rowh = lax.broadcasted_iota(jnp.int32, (per_group * SWA_BLOCK, 1), 0) // SWA_BLOCK
    sink = jnp.zeros((per_group * SWA_BLOCK, 1), F32) + sk_ref[0, group * per_group]
    for hh in range(1, per_group):
        sink = jnp.where(rowh == hh, sk_ref[0, group * per_group + hh], sink)
    return sink


def _swa_probs(q_pos, kcat, bias_h, first_mask, sink):
    logits = _dot_nt(q_pos, kcat) + (bias_h + first_mask)
    m = jnp.maximum(jnp.max(logits, axis=1, keepdims=True), sink)
    p = jnp.exp(logits - m)
    es = jnp.exp(sink - m)
    inv = 1.0 / (_rowsum(p) + es)
    return p * inv, es * inv


def _swa_steps(s):
    per_step = min(SWA_BLOCKS_PER_STEP, s // SWA_BLOCK)
    return per_step, s // (SWA_BLOCK * per_step)


def _swa_specs(n_heads, qcol, kcol, vcol, per_step):
    width = n_heads * HEAD_DIM
    prev = lambda col: pl.BlockSpec((SWA_BLOCK, LANES), lambda i: (jnp.maximum(i * per_step - 1, 0), col))
    cur = lambda col: pl.BlockSpec((per_step * SWA_BLOCK, LANES), lambda i: (i, col))
    return [pl.BlockSpec((per_step * SWA_BLOCK, width), lambda i: (i, qcol)),
            prev(kcol), cur(kcol), prev(vcol), cur(vcol),
            _full((SWA_BLOCK, 2 * SWA_BLOCK)),
            pl.BlockSpec(memory_space=pltpu.SMEM), pl.BlockSpec(memory_space=pltpu.SMEM)]


def _swa_stage_keys(prev_ref, cur_ref, all_ref):
    all_ref[:SWA_BLOCK, :] = prev_ref[...]
    all_ref[SWA_BLOCK:, :] = cur_ref[...]


def _swa_kept_specs(nb, n_groups, per_group, per_step):
    rows = per_group * SWA_BLOCK
    at = lambda i: (i, 0, 0, 0)
    specs = [pl.BlockSpec((per_step, n_groups, rows, 2 * SWA_BLOCK), at), pl.BlockSpec((per_step, n_groups, rows, 1), at)]
    shapes = [jax.ShapeDtypeStruct((nb, n_groups, rows, 2 * SWA_BLOCK), F32), jax.ShapeDtypeStruct((nb, n_groups, rows, 1), F32)]
    return specs, shapes


def _swa_fwd(proj, bucket, rel_bias, sinks, n_heads, qcol, kcol, vcol, comm=None):
    s = proj.shape[0]
    width = n_heads * HEAD_DIM
    n_groups = LANES // HEAD_DIM
    per_group = n_heads // n_groups

    per_step, n_steps = _swa_steps(s)

    def body(q_ref, kp_ref, kc_ref, vp_ref, vc_ref, bucket_ref, rb_ref, sk_ref, o_ref, prob_ref, psink_ref,
             bias_ref, kall_ref, vall_ref):
        step = pl.program_id(0)

        @pl.when(step == 0)
        def _():
            _swa_build_bias(bucket_ref, rb_ref, bias_ref, n_groups, per_group)

        _swa_stage_keys(kp_ref, kc_ref, kall_ref)
        _swa_stage_keys(vp_ref, vc_ref, vall_ref)
        lane = lax.broadcasted_iota(jnp.int32, (1, LANES), 1)
        first = lane < HEAD_DIM

        def query_block(j, _):
            rows = pl.ds(pl.multiple_of(j * SWA_BLOCK, SWA_BLOCK), SWA_BLOCK)
            band = pl.ds(pl.multiple_of(j * SWA_BLOCK, SWA_BLOCK), 2 * SWA_BLOCK)
            first_mask = _swa_first_block_mask(step * per_step + j)
            q_blk = q_ref[rows, :]
            kcat, vcat = kall_ref[band, :], vall_ref[band, :]
            pieces = {}
            for g in range(n_groups):
                sel = first if g == 0 else jnp.logical_not(first)
                prob, p_sink = _swa_probs(_swa_stack(q_blk, g, per_group, sel, SCALE), kcat, bias_ref[g], first_mask,
                                          _swa_sink_rows(sk_ref, g, per_group))
                prob_ref[j, g] = prob
                psink_ref[j, g] = p_sink
                _swa_unstack(_dot(prob.astype(BF16), vcat), g, per_group, pieces)
            for c in range(n_heads // 2):
                o_ref[rows, c * LANES:(c + 1) * LANES] = jnp.where(first, pieces[2 * c], pieces[2 * c + 1])
            return 0

        lax.fori_loop(0, per_step, query_block, 0)

    kept_specs, kept_shapes = _swa_kept_specs(s // SWA_BLOCK, n_groups, per_group, per_step)
    staged = pltpu.VMEM(((per_step + 1) * SWA_BLOCK, LANES), BF16)
    outs, landed = _call(
        body, "swa_fwd", (n_steps,),
        in_specs=_swa_specs(n_heads, qcol, kcol, vcol, per_step),
        out_specs=[pl.BlockSpec((per_step * SWA_BLOCK, width), lambda i: (i, 0))] + kept_specs,
        out_shape=[jax.ShapeDtypeStruct((s, width), F32)] + kept_shapes,
        args=(proj, proj, proj, proj, proj, bucket, rel_bias, sinks),
        scratch_shapes=[pltpu.VMEM((n_groups, per_group * SWA_BLOCK, 2 * SWA_BLOCK), F32), staged, staged], comm=comm)
    return outs[0], outs[1:], landed


def _rms_fwd(o, g):
    r = lax.rsqrt(jnp.mean(o * o, axis=-1, keepdims=True) + RMS_EPS)
    n = o * r
    return n, r, n * g


def _mix_ffn(sb_out, sw_out, x, g_in, b_in, sb_g, sw_g, w_out, g1, b1, w_gu_t, w_down, g2, b2, target):
    s, d = x.shape
    wsb, wsw = sb_out.shape[1], sw_out.shape[1]
    dff = w_down.shape[0]
    assert wsb + wsw == d
    tm = min(FFN_TILE, s)

    def body(sb_ref, sw_ref, x_ref, gi_ref, bi_ref, sbg_ref, swg_ref, wo_hbm, g1_ref, b1_ref, wgu_hbm, wd_hbm,
             g2_ref, b2_ref, t_ref,
             du1_ref, mg_ref, h1b_ref, act_ref, dgu_ref, du2b_ref, dsb_ref, dsw_ref, st_ref,
             wo_ref, wgu_ref, wd_ref):
        @pl.when(pl.program_id(0) == 0)
        def _():
            pltpu.sync_copy(wo_hbm, wo_ref)
            pltpu.sync_copy(wgu_hbm, wgu_ref)
            pltpu.sync_copy(wd_hbm, wd_ref)
            st_ref[...] = jnp.zeros_like(st_ref)

        sb, sw = sb_ref[...], sw_ref[...]
        _, _, m_sb = _rms_fwd(sb, sbg_ref[...])
        _, _, m_sw = _rms_fwd(sw, swg_ref[...])
        m_sb = m_sb.astype(BF16)
        m_sw = m_sw.astype(BF16)
        mg_ref[:, :wsb] = m_sb
        mg_ref[:, wsb:] = m_sw
        xhat0, _ = _ln_hat(x_ref[...])
        u1 = ALPHA * (xhat0 * gi_ref[...] + bi_ref[...]) + _dot(m_sb, wo_ref[:wsb, :]) + _dot(m_sw, wo_ref[wsb:, :])

        xhat1, r1 = _ln_hat(u1)
        h1 = xhat1 * g1_ref[...] + b1_ref[...]
        h1b = h1.astype(BF16)
        h1b_ref[...] = h1b
        gate = _dot_nt(h1b, wgu_ref[:dff, :])
        up = _dot_nt(h1b, wgu_ref[dff:, :])
        sg = jax.nn.sigmoid(gate)
        silu = gate * sg
        act = (silu * up).astype(BF16)
        act_ref[...] = act
        u2 = ALPHA * h1 + _dot(act, wd_ref[...])
        xhat2, r2 = _ln_hat(u2)
        diff = xhat2 * g2_ref[...] + b2_ref[...] - t_ref[...]
        dh2 = diff * (1.0 / d)
        st_ref[0:1, :] += _colsum(dh2 * xhat2)
        st_ref[1:2, :] += _colsum(dh2)
        st_ref[2:3, :] += jnp.broadcast_to(_colsum(_rowsum(diff * diff)) * (0.5 / d), (1, d))
        du2 = _ln_bwd(dh2 * g2_ref[...], xhat2, r2)
        du2b = du2.astype(BF16)
        du2b_ref[...] = du2b
        dact = _dot_nt(du2b, wd_ref[...])
        dgate = (dact * up * (sg * (1.0 + gate * (1.0 - sg)))).astype(BF16)
        dup = (dact * silu).astype(BF16)
        dgu_ref[:, :dff] = dgate
        dgu_ref[:, dff:] = dup
        dh1 = _dot(dgate, wgu_ref[:dff, :]) + _dot(dup, wgu_ref[dff:, :]) + ALPHA * du2
        st_ref[3:4, :] += _colsum(dh1 * xhat1)
        st_ref[4:5, :] += _colsum(dh1)
        du1 = _ln_bwd(dh1 * g1_ref[...], xhat1, r1)
        du1_ref[...] = du1

        dmerged = _dot_nt(du1.astype(BF16), wo_ref[...])
        dsb, gsb = _rms_bwd(dmerged[:, :wsb], sb, sbg_ref[...])
        dsw, gsw = _rms_bwd(dmerged[:, wsb:], sw, swg_ref[...])
        dsb_ref[...] = dsb.astype(BF16)
        dsw_ref[...] = dsw.astype(BF16)
        st_ref[5:6, :wsb] += gsb
        st_ref[5:6, wsb:] += gsw

    row = lambda width: pl.BlockSpec((tm, width), lambda i: (i, 0))
    vec = lambda width: _full((1, width))
    hbm = pl.BlockSpec(memory_space=pl.ANY)
    bf = lambda width: jax.ShapeDtypeStruct((s, width), BF16)
    return pl.pallas_call(
        body, name="mix_ffn", grid=(s // tm,),
        in_specs=[row(wsb), row(wsw), row(d), vec(d), vec(d), vec(wsb), vec(wsw), hbm, vec(d), vec(d), hbm, hbm,
                  vec(d), vec(d), row(d)],
        out_specs=[row(d), row(d), row(d), row(dff), row(2 * dff), row(d), row(wsb), row(wsw), _full((8, d))],
        out_shape=[jax.ShapeDtypeStruct((s, d), F32), bf(d), bf(d), bf(dff), bf(2 * dff), bf(d), bf(wsb), bf(wsw),
                   jax.ShapeDtypeStruct((8, d), F32)],
        scratch_shapes=[pltpu.VMEM(w_out.shape, BF16), pltpu.VMEM(w_gu_t.shape, BF16), pltpu.VMEM(w_down.shape, BF16)],
        compiler_params=_params(60),
    )(sb_out, sw_out, x, g_in, b_in, sb_g, sw_g, w_out, g1, b1, w_gu_t, w_down, g2, b2, target)


def _rms_bwd(dm, o, g):
    n, r, _ = _rms_fwd(o, g)
    dn = dm * g
    return r * (dn - n * jnp.mean(dn * n, axis=-1, keepdims=True)), _colsum(dm * n)


def _sb_bwd(proj, dout, out, first, n_pairs, comm=None):
    s = proj.shape[0]
    t, per_step, n_steps = _sb_grid(s)
    width = n_pairs * LANES

    def body(q_ref, k_ref, v_ref, do_ref, o_ref, ab_ref, beta_ref, cmax_ref, dq_ref, dk_out, dv_out, dk_ref, dv_ref):
        step = pl.program_id(1)

        @pl.when(step == 0)
        def _():
            dk_ref[...] = jnp.zeros_like(dk_ref)
            dv_ref[...] = jnp.zeros_like(dv_ref)

        lane = lax.broadcasted_iota(jnp.int32, (1, LANES), 1)
        first_lanes = lane < HEAD_DIM
        upper, incl = _sb_triangles(t)

        def query_tile(j, _):
            rows = pl.ds(pl.multiple_of(j * t, t), t)
            do2 = do_ref[rows, :]
            qs = _sb_stack_heads(q_ref[rows, :] * SCALE, first_lanes)
            dos = _sb_stack_heads(do2, first_lanes)
            prod = do2.astype(F32) * o_ref[rows, :]
            totals = jnp.concatenate([_rowsum(jnp.where(first_lanes, prod, 0.0)),
                                      _rowsum(jnp.where(first_lanes, 0.0, prod))], axis=0)

            def grads(offs, k_t, v_t, a_b, beta, c_e, dq):
                d_e = _dot_nt(dos, v_t) * a_b.astype(F32)
                d_hi = d_e.astype(BF16)
                d_lo = (d_e - d_hi.astype(F32)).astype(BF16)
                cols = [slice(c * t, (c + 1) * t) for c in range(len(offs))]
                suf_e, c_e = _sb_suffix([[d_hi[:, c], d_lo[:, c]] for c in cols], [_rowsum(d_e[:, c]) for c in cols], incl, c_e)
                dzb = (d_e - beta * (d_e + (totals - suf_e))).astype(BF16)
                dk_t = _dot_tn(dzb, qs)
                dv_t = _dot_tn(a_b, dos)
                for off, c in zip(offs, cols):
                    dk_ref[pl.ds(off, t), :] += dk_t[c, :]
                    dv_ref[pl.ds(off, t), :] += dv_t[c, :]
                return c_e, dq + _dot(dzb, k_t)

            def first_visit(offs, carry):
                _, c_e, dq = carry
                k_t = _sb_key_tiles(k_ref, offs, t)
                v_t = _sb_key_tiles(v_ref, offs, t)
                c_e, dq = grads(offs, k_t, v_t, ab_ref[j], beta_ref[j], c_e, dq)
                go = jnp.max(cmax_ref[j]) > SB_UNDERFLOW
                zero = jnp.zeros((2 * t, 1), F32)
                rebuilt = lambda: _sb_scores(qs, k_t, upper, zero, _sb_first_mask(t, step * per_step + j > 0), t)[2]
                return (lax.cond(go, rebuilt, lambda: zero), c_e, dq), go

            def visit(offs, carry):
                c_l, c_e, dq = carry
                k_t = _sb_key_tiles(k_ref, offs, t)
                v_t = _sb_key_tiles(v_ref, offs, t)
                lb, a, c_l = _sb_scores(qs, k_t, upper, c_l, None, t)
                c_e, dq = grads(offs, k_t, v_t, a.astype(BF16), jnp.exp(lb), c_e, dq)
                return c_l, c_e, dq

            init = (jnp.zeros((2 * t, 1), F32), jnp.zeros((2 * t, 1), F32), jnp.zeros((2 * t, LANES), F32))
            _, _, dq = _sb_walk(step * per_step + j, t, first_visit, visit, init)
            dq_ref[rows, :] = (jnp.where(first_lanes, dq[:t], dq[t:]) * SCALE).astype(BF16)
            return 0

        lax.fori_loop(0, per_step, query_tile, 0)

        @pl.when(step == n_steps - 1)
        def _():
            dk_out[...] = dk_ref[...].astype(BF16)
            dv_out[...] = dv_ref[...].astype(BF16)

    qblk = pl.BlockSpec((t * per_step, LANES), lambda h, i: (i, h))
    whole = pl.BlockSpec((s, LANES), lambda h, i: (0, h))
    first_specs, _ = _sb_first_specs(n_pairs, n_steps, per_step, t)
    return _call(
        body, "sb_bwd", (n_pairs, n_steps),
        in_specs=[qblk,
                  pl.BlockSpec((s, LANES), lambda h, i: (0, n_pairs + h)),
                  pl.BlockSpec((s, LANES), lambda h, i: (0, 2 * n_pairs + h)),
                  qblk, qblk] + first_specs,
        out_specs=[qblk, whole, whole],
        out_shape=[jax.ShapeDtypeStruct((s, width), BF16)] * 3,
        args=(proj, proj, proj, dout, out, *first),
        scratch_shapes=[pltpu.VMEM((s, LANES), F32), pltpu.VMEM((s, LANES), F32)], comm=comm)


def _swa_bwd(proj, dout, kept, bucket, n_heads, qcol, kcol, vcol, comm=None):
    s = proj.shape[0]
    width = n_heads * HEAD_DIM
    n_groups = LANES // HEAD_DIM
    per_group = n_heads // n_groups
    nb = s // SWA_BLOCK

    per_step, n_steps = _swa_steps(s)

    def body(q_ref, kp_ref, kc_ref, vp_ref, vc_ref, bucket_ref, do_ref, prob_ref, psink_ref,
             dq_ref, dk_out, dv_out, dsk_ref, drb_ref, dbias_ref, dk_ref, dv_ref, kall_ref, vall_ref):
        step = pl.program_id(0)

        @pl.when(step == 0)
        def _():
            dbias_ref[...] = jnp.zeros_like(dbias_ref)
            dk_ref[...] = jnp.zeros_like(dk_ref)
            dv_ref[...] = jnp.zeros_like(dv_ref)
            dsk_ref[...] = jnp.zeros_like(dsk_ref)

        _swa_stage_keys(kp_ref, kc_ref, kall_ref)
        _swa_stage_keys(vp_ref, vc_ref, vall_ref)
        lane = lax.broadcasted_iota(jnp.int32, (1, LANES), 1)
        first = lane < HEAD_DIM

        def query_block(j, _):
            i = step * per_step + j
            rows = pl.ds(pl.multiple_of(j * SWA_BLOCK, SWA_BLOCK), SWA_BLOCK)
            band = pl.ds(pl.multiple_of(j * SWA_BLOCK, SWA_BLOCK), 2 * SWA_BLOCK)
            q_blk, do_blk = q_ref[rows, :], do_ref[rows, :]
            kcat, vcat = kall_ref[band, :], vall_ref[band, :]
            dkcat = jnp.zeros((2 * SWA_BLOCK, LANES), F32)
            dvcat = jnp.zeros((2 * SWA_BLOCK, LANES), F32)
            pieces = {}
            for g in range(n_groups):
                sel = first if g == 0 else jnp.logical_not(first)
                q_g = _swa_stack(q_blk, g, per_group, sel, SCALE)
                do_g = _swa_stack(do_blk, g, per_group, sel)
                prob, p_sink = prob_ref[j, g], psink_ref[j, g]
                dprob = _dot_nt(do_g, vcat)
                delta = _rowsum(prob * dprob)
                dlog = prob * (dprob - delta)
                sink_term = p_sink * delta
                for hh in range(per_group):
                    h = g * per_group + hh
                    tot = _colsum(sink_term[hh * SWA_BLOCK:(hh + 1) * SWA_BLOCK, :])
                    dsk_ref[h:h + 1, :] += jnp.broadcast_to(-tot, (1, LANES))
                dbias_ref[g] += dlog
                dlb = dlog.astype(BF16)
                _swa_unstack(_dot(dlb, kcat) * SCALE, g, per_group, pieces)
                dkcat += _dot_tn(dlb, q_g)
                dvcat += _dot_tn(prob.astype(BF16), do_g)
            for c in range(n_heads // 2):
                dq_ref[rows, c * LANES:(c + 1) * LANES] = jnp.where(first, pieces[2 * c], pieces[2 * c + 1]).astype(BF16)

            cur = pl.multiple_of(i * SWA_BLOCK, SWA_BLOCK)
            dk_ref[pl.ds(cur, SWA_BLOCK), :] += dkcat[SWA_BLOCK:, :]
            dv_ref[pl.ds(cur, SWA_BLOCK), :] += dvcat[SWA_BLOCK:, :]

            @pl.when(i > 0)
            def _():
                prv = pl.multiple_of((i - 1) * SWA_BLOCK, SWA_BLOCK)
                dk_ref[pl.ds(prv, SWA_BLOCK), :] += dkcat[:SWA_BLOCK, :]
                dv_ref[pl.ds(prv, SWA_BLOCK), :] += dvcat[:SWA_BLOCK, :]

            return 0

        lax.fori_loop(0, per_step, query_block, 0)

        @pl.when(step == n_steps - 1)
        def _():
            bk = bucket_ref[...]
            rowi = lax.broadcasted_iota(jnp.int32, (REL_BUCKETS, LANES), 0)
            coli = lax.broadcasted_iota(jnp.int32, (REL_BUCKETS, LANES), 1)
            res = jnp.zeros((REL_BUCKETS, LANES), F32)
            for h in range(n_heads):
                g, hh = divmod(h, per_group)
                db = dbias_ref[g, hh * SWA_BLOCK:(hh + 1) * SWA_BLOCK, :]
                for b in range(REL_BUCKETS):
                    tot = _colsum(_rowsum(jnp.where(bk == b, db, 0.0)))
                    res = jnp.where((rowi == b) & (coli == h), tot, res)
            drb_ref[...] = res
            dk_out[...] = dk_ref[...].astype(BF16)
            dv_out[...] = dv_ref[...].astype(BF16)

    kept_specs, _ = _swa_kept_specs(nb, n_groups, per_group, per_step)
    rows_spec = pl.BlockSpec((per_step * SWA_BLOCK, width), lambda i: (i, 0))
    in_specs = _swa_specs(n_heads, qcol, kcol, vcol, per_step)[:6] + [rows_spec] + kept_specs
    staged = pltpu.VMEM(((per_step + 1) * SWA_BLOCK, LANES), BF16)
    return _call(
        body, "swa_bwd", (n_steps,),
        in_specs=in_specs,
        out_specs=[rows_spec, _full((s, LANES)), _full((s, LANES)), _full((8, LANES)), _full((REL_BUCKETS, LANES))],
        out_shape=[jax.ShapeDtypeStruct((s, width), BF16), jax.ShapeDtypeStruct((s, LANES), BF16),
                   jax.ShapeDtypeStruct((s, LANES), BF16), jax.ShapeDtypeStruct((8, LANES), F32),
                   jax.ShapeDtypeStruct((REL_BUCKETS, LANES), F32)],
        args=(proj, proj, proj, proj, proj, bucket, dout, *kept),
        scratch_shapes=[pltpu.VMEM((n_groups, per_group * SWA_BLOCK, 2 * SWA_BLOCK), F32),
                        pltpu.VMEM((s, LANES), F32), pltpu.VMEM((s, LANES), F32), staged, staged],
        comm=comm)


def _proj_bwd(pieces, w_in_t, du1, x, g_in, comm=None):
    s, d = x.shape
    cols = w_in_t.shape[0]
    tm = min(ROW_TILE, s)
    n_p = len(pieces)

    def body(*refs):
        p_refs = refs[:n_p]
        w_ref, du_ref, x_ref, g_ref, dx_ref, st_ref = refs[n_p:]
        i = pl.program_id(0)

        @pl.when(i == 0)
        def _():
            st_ref[...] = jnp.zeros_like(st_ref)

        dproj = jnp.concatenate([p[...] for p in p_refs], axis=1)
        dh0 = _dot(dproj, w_ref[...]) + ALPHA * du_ref[...]
        xhat, r = _ln_hat(x_ref[...])
        st_ref[0:1, :] += _colsum(dh0 * xhat)
        st_ref[1:2, :] += _colsum(dh0)
        dx_ref[...] = _ln_bwd(dh0 * g_ref[...], xhat, r)

    row = lambda width: pl.BlockSpec((tm, width), lambda i: (i, 0))
    return _call(
        body, "proj_bwd", (s // tm,),
        in_specs=[row(p.shape[1]) for p in pieces] + [_full((cols, d)), row(d), row(d), _full((1, d))],
        out_specs=[row(d), _full((8, d))],
        out_shape=[jax.ShapeDtypeStruct((s, d), F32), jax.ShapeDtypeStruct((8, d), F32)],
        args=(*pieces, w_in_t, du1, x, g_in), comm=comm)


def _wgrad(name, pieces, b, tm, tn):
    s, n = b.shape
    m = sum(p.shape[1] for p in pieces)
    n_p = len(pieces)
    assert n_p == 1 or tm == m
    ts = min(WGRAD_TOKENS if b.dtype == BF16 and n_p == 1 else WGRAD_TOKENS // 2, s)
    n_k = s // ts

    def body(*refs):
        p_refs, b_ref, o_ref, acc_ref = refs[:n_p], refs[n_p], refs[n_p + 1], refs[n_p + 2]
        k = pl.program_id(2)

        @pl.when(k == 0)
        def _():
            acc_ref[...] = jnp.zeros_like(acc_ref)

        a = p_refs[0][...] if n_p == 1 else jnp.concatenate([p[...] for p in p_refs], axis=1)
        acc_ref[...] += _dot_tn(a, b_ref[...].astype(BF16))

        @pl.when(k == n_k - 1)
        def _():
            o_ref[...] = acc_ref[...].astype(BF16)

    piece_spec = lambda p: pl.BlockSpec((ts, tm if n_p == 1 else p.shape[1]), lambda i, j, k: (k, i))
    return pl.pallas_call(
        body, name=name, grid=(m // tm, n // tn, n_k),
        in_specs=[piece_spec(p) for p in pieces] + [pl.BlockSpec((ts, tn), lambda i, j, k: (k, j))],
        out_specs=pl.BlockSpec((tm, tn), lambda i, j, k: (i, j)),
        out_shape=jax.ShapeDtypeStruct((m, n), BF16),
        scratch_shapes=[pltpu.VMEM((tm, tn), F32)],
        compiler_params=_params(),
    )(*pieces, b)


def _adamw_math(w, g, m, v):
    m = ADAM_B1 * m + (1.0 - ADAM_B1) * g
    v = ADAM_B2 * v + (1.0 - ADAM_B2) * (g * g)
    m_hat = m / (1.0 - ADAM_B1 ** ADAM_STEP)
    v_hat = v / (1.0 - ADAM_B2 ** ADAM_STEP)
    delta = -ADAM_LR * (m_hat / (jnp.sqrt(v_hat) + ADAM_EPS) + ADAM_WD * w)
    return delta, m, v


def _adamw_rows(rows):
    return max(r for r in range(16, 257, 16) if rows % r == 0)


def _adamw(name, landed, w, m, v, tr):
    rows, cols = w.shape

    def body(l_ref, w_ref, m_ref, v_ref, g_ref, d_ref, nm_ref, nv_ref):
        g = l_ref[0].astype(F32)
        for src in range(1, N_DEV):
            g = g + l_ref[src].astype(F32)
        delta, nm, nv = _adamw_math(w_ref[...], g, m_ref[...], v_ref[...])
        g_ref[...] = g
        d_ref[...] = delta
        nm_ref[...] = nm
        nv_ref[...] = nv

    blk = pl.BlockSpec((tr, cols), lambda i: (i, 0))
    shape = jax.ShapeDtypeStruct((rows, cols), F32)
    return pl.pallas_call(
        body, name=name, grid=(rows // tr,),
        in_specs=[pl.BlockSpec((N_DEV, tr, cols), lambda i: (0, i, 0)), blk, blk, blk],
        out_specs=[blk, blk, blk, blk],
        out_shape=[shape, shape, shape, shape],
        compiler_params=_params(),
    )(landed, w, m, v)


def _pack(d, ln_in_g, ln_in_b, ln1_g, ln1_b, ln2_g, ln2_b, sb_g, sw_g, rel_bias, sinks, extra=None):
    tail = [rel_bias.reshape(-1), sinks.reshape(-1)]
    if extra is not None:
        tail.append(extra.reshape(-1))
    tail = jnp.concatenate(tail)
    tail = jnp.concatenate([tail, jnp.zeros((d - tail.shape[0],), F32)])
    rows = [ln_in_g.reshape(-1), ln_in_b.reshape(-1), ln1_g.reshape(-1), ln1_b.reshape(-1),
            ln2_g.reshape(-1), ln2_b.reshape(-1),
            jnp.concatenate([sb_g.reshape(-1), sw_g.reshape(-1)]), tail]
    return jnp.stack(rows)


def _unpack(p, wsb, n_rb, n_sk):
    return [p[0], p[1], p[6, :wsb][None], p[6, wsb:][None], p[7, n_rb:n_rb + n_sk][None],
            p[7, :n_rb].reshape(REL_BUCKETS, -1), p[2][None], p[3][None], p[4][None], p[5][None]]


def kernel(x, ln_in_g, ln_in_b, w_in, sb_norm_g, swa_norm_g, sinks, rel_bias, w_out, ln1_g, ln1_b, w_gate_up, w_down, ln2_g, ln2_b, loss_target, m_ln_in_g, m_ln_in_b, m_w_in, m_sb_norm_g, m_swa_norm_g, m_sinks, m_rel_bias, m_w_out, m_ln1_g, m_ln1_b, m_w_gate_up, m_w_down, m_ln2_g, m_ln2_b, v_ln_in_g, v_ln_in_b, v_w_in, v_sb_norm_g, v_swa_norm_g, v_sinks, v_rel_bias, v_w_out, v_ln1_g, v_ln1_b, v_w_gate_up, v_w_down, v_ln2_g, v_ln2_b):
    x2 = x[0]
    tgt = loss_target[0]
    s, d = x2.shape
    wsb = sb_norm_g.shape[-1]
    wsw = swa_norm_g.shape[-1]
    n_sw_heads = sinks.shape[-1]
    n_pairs = wsb // LANES
    dff = w_down.shape[1] * N_DEV
    assert wsb % LANES == 0 and wsw % LANES == 0 and n_sw_heads * HEAD_DIM == wsw
    assert 3 * wsb % wsw == 0 and dff % LANES == 0 and s % SWA_BLOCK == 0
    qcol = 3 * wsb // wsw
    kcol = (3 * wsb + wsw) // LANES
    vcol = kcol + 1
    assert w_in.shape[-1] * N_DEV == (vcol + 1) * LANES

    t2 = lambda a: jnp.transpose(a[0])
    big_w = [t2(w_in), w_out[0], t2(w_gate_up), w_down[0]]
    big_m = [t2(m_w_in), m_w_out[0], t2(m_w_gate_up), m_w_down[0]]
    big_v = [t2(v_w_in), v_w_out[0], t2(v_w_gate_up), v_w_down[0]]

    cat_rows = lambda g: g.reshape(N_DEV * g.shape[1], g.shape[2])
    shards = [w.astype(BF16) for w in big_w]
    w_in_t = cat_rows(_allgather_via_sibling("w_in_allgather", shards[0]))

    vec = lambda a: a.reshape(1, -1)
    g_in, b_in = vec(ln_in_g), vec(ln_in_b)
    bucket = jnp.asarray(_swa_bucket_table())

    h0b, proj = _ln_proj(x2, g_in, b_in, w_in_t)
    sb_out, sb_first, gathered = _sb_fwd(proj, n_pairs, comm=(shards[1:3], ["gather"] * 2))
    w_out_f, w_gu_t = cat_rows(gathered[0]), cat_rows(gathered[1])
    sw_out, sw_kept, gathered = _swa_fwd(proj, bucket, rel_bias, sinks, n_sw_heads, qcol, kcol, vcol,
                                comm=(shards[3:], ["gather"]))
    w_down_f = cat_rows(gathered[0])
    du1, merged, h1b, act, dgu, du2b, dsb, dsw, st_ffn = _mix_ffn(
        sb_out, sw_out, x2, g_in, b_in, sb_norm_g, swa_norm_g, w_out_f, ln1_g, ln1_b, w_gu_t, w_down_f, ln2_g, ln2_b, tgt)

    split_rows = lambda g: g.reshape(N_DEV, g.shape[0] // N_DEV, g.shape[1])
    gw_gu = _wgrad("wgrad_gate_up", [dgu], h1b, dff // 2, d)
    gw_down = _wgrad("wgrad_down", [act], du2b, dff // 2, d)
    gw_out = _wgrad("wgrad_out", [merged], du1, min(512, d), d)
    (dq_sb, dk_sb, dv_sb), (land_gu, land_out) = _sb_bwd(
        proj, dsb, sb_out, sb_first, n_pairs, comm=([split_rows(gw_gu), split_rows(gw_out)], ["scatter"] * 2))
    (dq_sw, dk_sw, dv_sw, st_sink, st_rb), (land_down,) = _swa_bwd(
        proj, dsw, sw_kept, bucket, n_sw_heads, qcol, kcol, vcol, comm=([split_rows(gw_down)], ["scatter"]))
    pieces = [dq_sb, dk_sb, dv_sb, dq_sw, dk_sw, dv_sw]
    gw_in = _wgrad("wgrad_in", pieces, h0b, proj.shape[1], d)
    (grad_x, st_in), (land_in,) = _proj_bwd(pieces, w_in_t, du1, x2, g_in, comm=([split_rows(gw_in)], ["scatter"]))

    n_rb = rel_bias.size
    small = _pack(d, st_in[0], st_in[1], st_ffn[3], st_ffn[4], st_ffn[0], st_ffn[1],
                  st_ffn[5, :wsb], st_ffn[5, wsb:], st_rb[:, :n_sw_heads], st_sink[:n_sw_heads, 0],
                  extra=st_ffn[2, 0:1])
    land_small = _exchange("small_grads_allgather", [small], ["gather"])[0]
    landed = [land_in, land_out, land_gu, land_down, land_small]

    big = []
    for name, land, w, m, v in zip(["adamw_in", "adamw_out", "adamw_gate_up", "adamw_down"], landed[:4], big_w, big_m, big_v):
        big.append(_adamw(name, land, w, m, v, _adamw_rows(w.shape[0])))

    small_w = _pack(d, ln_in_g, ln_in_b, ln1_g, ln1_b, ln2_g, ln2_b, sb_norm_g, swa_norm_g, rel_bias, sinks)
    small_m = _pack(d, m_ln_in_g, m_ln_in_b, m_ln1_g, m_ln1_b, m_ln2_g, m_ln2_b, m_sb_norm_g, m_swa_norm_g, m_rel_bias, m_sinks)
    small_v = _pack(d, v_ln_in_g, v_ln_in_b, v_ln1_g, v_ln1_b, v_ln2_g, v_ln2_b, v_sb_norm_g, v_swa_norm_g, v_rel_bias, v_sinks)
    sg, sd, sm, sv = _adamw("adamw_small", landed[4], small_w, small_m, small_v, 8)
    n_sk = sinks.size
    loss = sg[7, n_rb + n_sk]

    def leaves(idx):
        sm_l = _unpack([sg, sd, sm, sv][idx], wsb, n_rb, n_sk)
        bg = [jnp.transpose(big[0][idx])[None], big[1][idx][None], jnp.transpose(big[2][idx])[None], big[3][idx][None]]
        return [sm_l[0], sm_l[1], bg[0], sm_l[2], sm_l[3], sm_l[4], sm_l[5], bg[1], sm_l[6], sm_l[7], bg[2], bg[3], sm_l[8], sm_l[9]]

    return (loss, grad_x[None], *leaves(0), *leaves(1), *leaves(2), *leaves(3))
```

```python
import functools
import math

import numpy as np
import jax
import jax.numpy as jnp
from jax import lax
from jax.experimental import pallas as pl
from jax.experimental.pallas import tpu as pltpu

F32 = jnp.float32
BF16 = jnp.bfloat16
MESH = pl.DeviceIdType.MESH

N_DEV = 8
LANES = 128
HEAD_DIM = 64
SCALE = HEAD_DIM ** -0.5
SWA_BLOCK = 128
SWA_BLOCKS_PER_STEP = 4
REL_BUCKETS = 32
REL_MAX_DIST = 128
ALPHA = 2.0 ** 0.25
LN_EPS = 1e-5
RMS_EPS = 1e-6
ADAM_LR = 0.001
ADAM_B1 = 0.9
ADAM_B2 = 0.999
ADAM_EPS = 1e-08
ADAM_WD = 0.01
ADAM_STEP = 10

ROW_TILE = 512
SB_TILE = 256
SB_TILES_PER_STEP = 4
FFN_TILE = 256
WGRAD_TOKENS = 2048
SB_UNDERFLOW = -110.0
SB_MASKED = -1e30
MIB = 1024 * 1024


def _params(vmem_mib=48):
    return pltpu.CompilerParams(vmem_limit_bytes=vmem_mib * MIB)


def _dot(a, b):
    return jnp.dot(a, b, preferred_element_type=F32)


def _dot_nt(a, b):
    return lax.dot_general(a, b, (((1,), (1,)), ((), ())), preferred_element_type=F32)


def _dot_tn(a, b):
    return lax.dot_general(a, b, (((0,), (0,)), ((), ())), preferred_element_type=F32)


def _ln_hat(x):
    mu = jnp.mean(x, axis=-1, keepdims=True)
    xc = x - mu
    var = jnp.mean(xc * xc, axis=-1, keepdims=True)
    r = lax.rsqrt(var + LN_EPS)
    return xc * r, r


def _ln_bwd(dxhat, xhat, r):
    return r * (dxhat - jnp.mean(dxhat, axis=-1, keepdims=True)
                - xhat * jnp.mean(dxhat * xhat, axis=-1, keepdims=True))


def _colsum(a):
    return jnp.sum(a, axis=0, keepdims=True)


def _rowsum(a):
    return jnp.sum(a, axis=1, keepdims=True)


def _full(shape):
    return pl.BlockSpec(shape, lambda *_: (0,) * len(shape))


def _comm_out_shapes(arrays, kinds):
    shapes = []
    for a, kind in zip(arrays, kinds):
        blk = a.shape if kind == "gather" else a.shape[1:]
        shapes.append(jax.ShapeDtypeStruct((N_DEV,) + tuple(blk), a.dtype))
    return shapes


def _comm_sems(n):
    return [pltpu.SemaphoreType.DMA((n, N_DEV - 1)), pltpu.SemaphoreType.DMA((n, N_DEV - 1)),
            pltpu.SemaphoreType.DMA((n,))]


def _comm_copies(ins, outs, kinds, send_sems, recv_sems, local_sems):
    x, y, c = lax.axis_index("x"), lax.axis_index("y"), lax.axis_index("c")
    me = 4 * x + 2 * y + c

    def src_for(t, dev_lin):
        return ins[t] if kinds[t] == "gather" else ins[t].at[dev_lin]

    local = [pltpu.make_async_copy(src_for(t, me), outs[t].at[me], local_sems.at[t]) for t in range(len(kinds))]
    sends, arrivals = [], []
    for k in range(1, N_DEV):
        px = 1 - x if (k >> 2) & 1 else x
        py = 1 - y if (k >> 1) & 1 else y
        pc = 1 - c if k & 1 else c
        peer_lin = 4 * px + 2 * py + pc
        for t in range(len(kinds)):
            sems = dict(send_sem=send_sems.at[t, k - 1], recv_sem=recv_sems.at[t, k - 1],
                        device_id=(px, py, pc), device_id_type=MESH)
            sends.append(pltpu.make_async_remote_copy(src_ref=src_for(t, peer_lin), dst_ref=outs[t].at[me], **sems))
            arrivals.append(pltpu.make_async_remote_copy(src_ref=src_for(t, peer_lin), dst_ref=outs[t].at[peer_lin], **sems))
    return local, sends, arrivals


def _comm_start(ins, outs, kinds, sems):
    local, sends, _ = _comm_copies(ins, outs, kinds, *sems)
    for cp in local + sends:
        cp.start()


def _comm_finish(ins, outs, kinds, sems):
    local, sends, arrivals = _comm_copies(ins, outs, kinds, *sems)
    for cp in arrivals:
        cp.wait_recv()
    for cp in sends:
        cp.wait_send()
    for cp in local:
        cp.wait()


def _exchange(name, arrays, kinds):
    n = len(arrays)

    def body(*refs):
        ins, outs, sems = refs[:n], refs[n:2 * n], refs[2 * n:]
        _comm_start(ins, outs, kinds, sems)
        _comm_finish(ins, outs, kinds, sems)

    any_spec = pl.BlockSpec(memory_space=pl.ANY)
    return pl.pallas_call(
        body, name=name, out_shape=_comm_out_shapes(arrays, kinds),
        in_specs=[any_spec] * n, out_specs=[any_spec] * n,
        scratch_shapes=_comm_sems(n),
    )(*arrays)


def _allgather_via_sibling(name, shard):
    def body(x_ref, out_ref, send_sems, recv_sems, local_sem):
        x, y, c = lax.axis_index("x"), lax.axis_index("y"), lax.axis_index("c")
        me, sibling = (x, y, c), (x, y, 1 - c)
        chips = [(1 - x, y), (x, 1 - y), (1 - x, 1 - y)]

        def copy(k, block, to, src=None):
            slot = out_ref.at[4 * block[0] + 2 * block[1] + block[2]]
            return pltpu.make_async_remote_copy(
                src_ref=slot if src is None else src, dst_ref=slot,
                send_sem=send_sems.at[k], recv_sem=recv_sems.at[k], device_id=to, device_id_type=MESH)

        mine = pltpu.make_async_copy(x_ref, out_ref.at[4 * x + 2 * y + c], local_sem)
        mine.start()
        first = [copy(0, me, sibling, src=x_ref)]
        first += [copy(1 + j, me, (*chip, c), src=x_ref) for j, chip in enumerate(chips)]
        for cp in first:
            cp.start()
        passed = [copy(4 + j, (*chip, c), sibling) for j, chip in enumerate(chips)]
        for j, chip in enumerate(chips):
            copy(1 + j, (*chip, c), me).wait_recv()
            passed[j].start()
        copy(0, sibling, me).wait_recv()
        for j, chip in enumerate(chips):
            copy(4 + j, (*chip, 1 - c), me).wait_recv()
        for cp in first + passed:
            cp.wait_send()
        mine.wait()

    any_spec = pl.BlockSpec(memory_space=pl.ANY)
    return pl.pallas_call(
        body, name=name, out_shape=jax.ShapeDtypeStruct((N_DEV,) + shard.shape, shard.dtype),
        in_specs=[any_spec], out_specs=any_spec,
        scratch_shapes=[pltpu.SemaphoreType.DMA((N_DEV - 1,)), pltpu.SemaphoreType.DMA((N_DEV - 1,)),
                        pltpu.SemaphoreType.DMA],
    )(shard)


def _call(body, name, grid, in_specs, out_specs, out_shape, args, scratch_shapes=(), comm=None):
    if comm is None:
        outs = pl.pallas_call(body, name=name, grid=grid, in_specs=in_specs, out_specs=out_specs,
                              out_shape=out_shape, scratch_shapes=list(scratch_shapes),
                              compiler_params=_params())(*args)
        return outs, []
    arrays, kinds = comm
    n, n_in, n_out, n_scr = len(arrays), len(in_specs), len(out_specs), len(scratch_shapes)

    def fused(*refs):
        c_in, x_in = refs[:n_in], refs[n_in:n_in + n]
        c_out = refs[n_in + n:n_in + n + n_out]
        x_out = refs[n_in + n + n_out:n_in + 2 * n + n_out]
        rest = refs[n_in + 2 * n + n_out:]
        c_scr, sems = rest[:n_scr], rest[n_scr:]
        ids = [pl.program_id(a) for a in range(len(grid))]
        is_first = functools.reduce(jnp.logical_and, [i == 0 for i in ids])
        is_last = functools.reduce(jnp.logical_and, [i == g - 1 for i, g in zip(ids, grid)])

        @pl.when(is_first)
        def _():
            _comm_start(x_in, x_out, kinds, sems)

        body(*c_in, *c_out, *c_scr)

        @pl.when(is_last)
        def _():
            _comm_finish(x_in, x_out, kinds, sems)

    any_spec = pl.BlockSpec(memory_space=pl.ANY)
    outs = pl.pallas_call(
        fused, name=name, grid=grid,
        in_specs=list(in_specs) + [any_spec] * n, out_specs=list(out_specs) + [any_spec] * n,
        out_shape=list(out_shape) + _comm_out_shapes(arrays, kinds),
        scratch_shapes=list(scratch_shapes) + _comm_sems(n),
        compiler_params=_params())(*args, *arrays)
    return outs[:n_out], outs[n_out:]


def _ln_proj(x, g, b, w_in_t):
    s, d = x.shape
    cols = w_in_t.shape[0]
    tm = min(ROW_TILE, s)

    def body(x_ref, g_ref, b_ref, w_ref, h_ref, p_ref):
        xhat, _ = _ln_hat(x_ref[...])
        h = (xhat * g_ref[...] + b_ref[...]).astype(BF16)
        h_ref[...] = h
        p_ref[...] = _dot_nt(h, w_ref[...]).astype(BF16)

    row = lambda width: pl.BlockSpec((tm, width), lambda i: (i, 0))
    return pl.pallas_call(
        body, name="ln_proj", grid=(s // tm,),
        in_specs=[row(d), _full((1, d)), _full((1, d)), _full((cols, d))],
        out_specs=[row(d), row(cols)],
        out_shape=[jax.ShapeDtypeStruct((s, d), BF16), jax.ShapeDtypeStruct((s, cols), BF16)],
        compiler_params=_params(),
    )(x, g, b, w_in_t)


def _sb_triangles(t):
    row = lax.broadcasted_iota(jnp.int32, (t, t), 0)
    col = lax.broadcasted_iota(jnp.int32, (t, t), 1)
    return (row > col).astype(BF16), (row >= col).astype(BF16)


def _sb_first_mask(z, t, has_prev):
    qrow = lax.broadcasted_iota(jnp.int32, (2 * t, t), 0) & (t - 1)
    col = lax.broadcasted_iota(jnp.int32, (2 * t, t), 1)
    return jnp.concatenate([jnp.where(has_prev, z[:, :t], SB_MASKED), jnp.where(col < qrow, z[:, t:], SB_MASKED)], axis=1)


def _sb_stack_heads(x2, first):
    zero = jnp.zeros_like(x2)
    return jnp.concatenate([jnp.where(first, x2, zero), jnp.where(first, zero, x2)], axis=0)


def _sb_key_tiles(ref, offs, t):
    tiles = [ref[pl.ds(off, t), :] for off in offs]
    return tiles[0] if len(tiles) == 1 else jnp.concatenate(tiles, axis=0)


def _sb_suffix(terms, row_sums, tri, carry):
    out = [None] * len(terms)
    for j in reversed(range(len(terms))):
        suf = carry
        for op in terms[j]:
            suf = suf + _dot(op, tri)
        out[j] = suf
        carry = carry + row_sums[j]
    return (out[0] if len(out) == 1 else jnp.concatenate(out, axis=1)), carry


def _sb_scores(qh, k_t, upper, carry_l, has_prev, t):
    z = _dot_nt(qh, k_t)
    if has_prev is not None:
        z = _sb_first_mask(z, t, has_prev)
    sp = jnp.log(1.0 + jnp.exp(-jnp.abs(z)))
    neg = jnp.minimum(z, 0.0)
    lb = neg - sp
    l1 = (neg - z) - sp
    hi = l1.astype(BF16)
    lo = (l1 - hi.astype(F32)).astype(BF16)
    cols = [slice(j * t, (j + 1) * t) for j in range(z.shape[1] // t)]
    suf, carry_l = _sb_suffix([[hi[:, c], lo[:, c]] for c in cols], [_rowsum(l1[:, c]) for c in cols], upper, carry_l)
    return lb, jnp.exp(lb + suf), carry_l


def _sb_walk(i, t, first_visit, visit, init):
    def alive(carry):
        return jnp.max(carry[0]) > SB_UNDERFLOW

    prev = pl.multiple_of(jnp.maximum(i - 1, 0) * t, t)
    carry, go = first_visit((prev, pl.multiple_of(i * t, t)), init)

    def cond(state):
        j, go, _ = state
        return (j < i - 1) & go

    def body(state):
        j, _, carry = state
        carry = visit((pl.multiple_of((i - 2 - j) * t, t),), carry)
        return j + 1, alive(carry), carry

    return lax.while_loop(cond, body, (jnp.int32(0), go, carry))[2]


def _sb_first_specs(n_pairs, n_steps, per_step, t):
    at = lambda h, i: (h * n_steps + i, 0, 0)
    n_tiles = n_pairs * n_steps * per_step
    specs = [pl.BlockSpec((per_step, 2 * t, 2 * t), at), pl.BlockSpec((per_step, 2 * t, 2 * t), at),
             pl.BlockSpec((per_step, 8, LANES), at)]
    shapes = [jax.ShapeDtypeStruct((n_tiles, 2 * t, 2 * t), BF16), jax.ShapeDtypeStruct((n_tiles, 2 * t, 2 * t), F32),
              jax.ShapeDtypeStruct((n_tiles, 8, LANES), F32)]
    return specs, shapes


def _sb_grid(s):
    t = min(SB_TILE, s)
    per_step = min(SB_TILES_PER_STEP, s // t)
    return t, per_step, s // (t * per_step)


def _sb_fwd(proj, n_pairs, comm=None):
    s = proj.shape[0]
    t, per_step, n_steps = _sb_grid(s)

    def body(q_ref, k_ref, v_ref, o_ref, ab_ref, beta_ref, cmax_ref):
        lane = lax.broadcasted_iota(jnp.int32, (1, LANES), 1)
        first = lane < HEAD_DIM
        upper, _ = _sb_triangles(t)

        def query_tile(j, _):
            i = pl.program_id(1) * per_step + j
            rows = pl.ds(pl.multiple_of(j * t, t), t)
            qs = _sb_stack_heads(q_ref[rows, :] * SCALE, first)

            def first_visit(offs, carry):
                c_l, acc = carry
                lb, a, c_l = _sb_scores(qs, _sb_key_tiles(k_ref, offs, t), upper, c_l, i > 0, t)
                a_b = a.astype(BF16)
                ab_ref[j] = a_b
                beta_ref[j] = jnp.exp(lb)
                c_max = jnp.max(c_l)
                cmax_ref[j] = jnp.broadcast_to(c_max, (8, LANES))
                return (c_l, acc + _dot(a_b, _sb_key_tiles(v_ref, offs, t))), c_max > SB_UNDERFLOW

            def visit(offs, carry):
                c_l, acc = carry
                _, a, c_l = _sb_scores(qs, _sb_key_tiles(k_ref, offs, t), upper, c_l, None, t)
                return c_l, acc + _dot(a.astype(BF16), _sb_key_tiles(v_ref, offs, t))

            init = (jnp.zeros((2 * t, 1), F32), jnp.zeros((2 * t, LANES), F32))
            _, acc = _sb_walk(i, t, first_visit, visit, init)
            o_ref[rows, :] = jnp.where(first, acc[:t], acc[t:])
            return 0

        lax.fori_loop(0, per_step, query_tile, 0)

    qblk = pl.BlockSpec((t * per_step, LANES), lambda h, i: (i, h))
    first_specs, first_shapes = _sb_first_specs(n_pairs, n_steps, per_step, t)
    outs, landed = _call(
        body, "sb_fwd", (n_pairs, n_steps),
        in_specs=[qblk,
                  pl.BlockSpec((s, LANES), lambda h, i: (0, n_pairs + h)),
                  pl.BlockSpec((s, LANES), lambda h, i: (0, 2 * n_pairs + h))],
        out_specs=[qblk] + first_specs,
        out_shape=[jax.ShapeDtypeStruct((s, n_pairs * LANES), F32)] + first_shapes,
        args=(proj, proj, proj), comm=comm)
    return outs[0], outs[1:], landed


def _swa_bucket_table():
    qi = np.arange(SWA_BLOCK)[:, None]
    cj = np.arange(2 * SWA_BLOCK)[None, :]
    dist = qi + SWA_BLOCK - cj
    exact = REL_BUCKETS // 2
    d = np.maximum(dist, 0)
    d_f = np.maximum(d, 1).astype(np.float32)
    large = exact + (np.log(d_f / np.float32(exact)) / np.float32(math.log(REL_MAX_DIST / exact))
                     * np.float32(REL_BUCKETS - exact)).astype(np.int32)
    large = np.minimum(large, REL_BUCKETS - 1)
    return np.where(d < exact, d, large).astype(np.int32)


def _swa_build_bias(bucket_ref, rb_ref, bias_ref, n_groups, per_group):
    bk = bucket_ref[...]
    dist = (lax.broadcasted_iota(jnp.int32, bk.shape, 0) + SWA_BLOCK) - lax.broadcasted_iota(jnp.int32, bk.shape, 1)
    window = (dist >= 0) & (dist < SWA_BLOCK)
    for g in range(n_groups):
        for hh in range(per_group):
            acc = jnp.zeros(bk.shape, F32)
            for b in range(REL_BUCKETS):
                acc = jnp.where(bk == b, rb_ref[b, g * per_group + hh], acc)
            bias_ref[g, hh * SWA_BLOCK:(hh + 1) * SWA_BLOCK, :] = jnp.where(window, acc, -jnp.inf)


def _swa_first_block_mask(i):
    col = lax.broadcasted_iota(jnp.int32, (1, 2 * SWA_BLOCK), 1)
    return jnp.where((col < SWA_BLOCK) & (i == 0), -jnp.inf, 0.0)


def _swa_place(blk, h, group, sel):
    if (h % 2) != group:
        blk = pltpu.roll(blk.astype(F32), HEAD_DIM, axis=1).astype(BF16)
    return jnp.where(sel, blk, jnp.zeros_like(blk))


def _swa_stack(ref, group, per_group, sel, scale=1.0):
    parts = []
    for hh in range(per_group):
        h = group * per_group + hh
        parts.append(_swa_place(ref[:, (h // 2) * LANES:(h // 2 + 1) * LANES], h, group, sel))
    stacked = jnp.concatenate(parts, axis=0)
    return stacked if scale == 1.0 else stacked * scale


def _swa_unstack(stacked, group, per_group, pieces):
    for hh in range(per_group):
        h = group * per_group + hh
        piece = stacked[hh * SWA_BLOCK:(hh + 1) * SWA_BLOCK, :]
        pieces[h] = pltpu.roll(piece, HEAD_DIM, axis=1) if (h % 2) != group else piece


def _swa_sink_rows(sk_ref, group, per_group):
    rowh = lax.broadcasted_iota(jnp.int32, (per_group * SWA_BLOCK, 1), 0) // SWA_BLOCK
    sink = jnp.zeros((per_group * SWA_BLOCK, 1), F32) + sk_ref[0, group * per_group]
    for hh in range(1, per_group):
        sink = jnp.where(rowh == hh, sk_ref[0, group * per_group + hh], sink)
    return sink


def _swa_probs(q_pos, kcat, bias_h, first_mask, sink):
    logits = _dot_nt(q_pos, kcat) + (bias_h + first_mask)
    m = jnp.maximum(jnp.max(logits, axis=1, keepdims=True), sink)
    p = jnp.exp(logits - m)
    es = jnp.exp(sink - m)
    inv = 1.0 / (_rowsum(p) + es)
    return p * inv, es * inv


def _swa_steps(s):
    per_step = min(SWA_BLOCKS_PER_STEP, s // SWA_BLOCK)
    return per_step, s // (SWA_BLOCK * per_step)


def _swa_specs(n_heads, qcol, kcol, vcol, per_step):
    width = n_heads * HEAD_DIM
    prev = lambda col: pl.BlockSpec((SWA_BLOCK, LANES), lambda i: (jnp.maximum(i * per_step - 1, 0), col))
    cur = lambda col: pl.BlockSpec((per_step * SWA_BLOCK, LANES), lambda i: (i, col))
    return [pl.BlockSpec((per_step * SWA_BLOCK, width), lambda i: (i, qcol)),
            prev(kcol), cur(kcol), prev(vcol), cur(vcol),
            _full((SWA_BLOCK, 2 * SWA_BLOCK)),
            pl.BlockSpec(memory_space=pltpu.SMEM), pl.BlockSpec(memory_space=pltpu.SMEM)]


def _swa_stage_keys(prev_ref, cur_ref, all_ref):
    all_ref[:SWA_BLOCK, :] = prev_ref[...]
    all_ref[SWA_BLOCK:, :] = cur_ref[...]


def _swa_kept_specs(nb, n_groups, per_group, per_step):
    rows = per_group * SWA_BLOCK
    at = lambda i: (i, 0, 0, 0)
    specs = [pl.BlockSpec((per_step, n_groups, rows, 2 * SWA_BLOCK), at), pl.BlockSpec((per_step, n_groups, rows, 1), at)]
    shapes = [jax.ShapeDtypeStruct((nb, n_groups, rows, 2 * SWA_BLOCK), F32), jax.ShapeDtypeStruct((nb, n_groups, rows, 1), F32)]
    return specs, shapes


def _swa_fwd(proj, bucket, rel_bias, sinks, n_heads, qcol, kcol, vcol, comm=None):
    s = proj.shape[0]
    width = n_heads * HEAD_DIM
    n_groups = LANES // HEAD_DIM
    per_group = n_heads // n_groups

    per_step, n_steps = _swa_steps(s)

    def body(q_ref, kp_ref, kc_ref, vp_ref, vc_ref, bucket_ref, rb_ref, sk_ref, o_ref, prob_ref, psink_ref,
             bias_ref, kall_ref, vall_ref):
        step = pl.program_id(0)

        @pl.when(step == 0)
        def _():
            _swa_build_bias(bucket_ref, rb_ref, bias_ref, n_groups, per_group)

        _swa_stage_keys(kp_ref, kc_ref, kall_ref)
        _swa_stage_keys(vp_ref, vc_ref, vall_ref)
        lane = lax.broadcasted_iota(jnp.int32, (1, LANES), 1)
        first = lane < HEAD_DIM

        def query_block(j, _):
            rows = pl.ds(pl.multiple_of(j * SWA_BLOCK, SWA_BLOCK), SWA_BLOCK)
            band = pl.ds(pl.multiple_of(j * SWA_BLOCK, SWA_BLOCK), 2 * SWA_BLOCK)
            first_mask = _swa_first_block_mask(step * per_step + j)
            q_blk = q_ref[rows, :]
            kcat, vcat = kall_ref[band, :], vall_ref[band, :]
            pieces = {}
            for g in range(n_groups):
                sel = first if g == 0 else jnp.logical_not(first)
                prob, p_sink = _swa_probs(_swa_stack(q_blk, g, per_group, sel, SCALE), kcat, bias_ref[g], first_mask,
                                          _swa_sink_rows(sk_ref, g, per_group))
                prob_ref[j, g] = prob
                psink_ref[j, g] = p_sink
                _swa_unstack(_dot(prob.astype(BF16), vcat), g, per_group, pieces)
            for c in range(n_heads // 2):
                o_ref[rows, c * LANES:(c + 1) * LANES] = jnp.where(first, pieces[2 * c], pieces[2 * c + 1])
            return 0

        lax.fori_loop(0, per_step, query_block, 0)

    kept_specs, kept_shapes = _swa_kept_specs(s // SWA_BLOCK, n_groups, per_group, per_step)
    staged = pltpu.VMEM(((per_step + 1) * SWA_BLOCK, LANES), BF16)
    outs, landed = _call(
        body, "swa_fwd", (n_steps,),
        in_specs=_swa_specs(n_heads, qcol, kcol, vcol, per_step),
        out_specs=[pl.BlockSpec((per_step * SWA_BLOCK, width), lambda i: (i, 0))] + kept_specs,
        out_shape=[jax.ShapeDtypeStruct((s, width), F32)] + kept_shapes,
        args=(proj, proj, proj, proj, proj, bucket, rel_bias, sinks),
        scratch_shapes=[pltpu.VMEM((n_groups, per_group * SWA_BLOCK, 2 * SWA_BLOCK), F32), staged, staged], comm=comm)
    return outs[0], outs[1:], landed


def _rms_fwd(o, g):
    r = lax.rsqrt(jnp.mean(o * o, axis=-1, keepdims=True) + RMS_EPS)
    n = o * r
    return n, r, n * g


def _mix_ffn(sb_out, sw_out, x, g_in, b_in, sb_g, sw_g, w_out, g1, b1, w_gu_t, w_down, g2, b2, target):
    s, d = x.shape
    wsb, wsw = sb_out.shape[1], sw_out.shape[1]
    dff = w_down.shape[0]
    assert wsb + wsw == d
    tm = min(FFN_TILE, s)

    def body(sb_ref, sw_ref, x_ref, gi_ref, bi_ref, sbg_ref, swg_ref, wo_hbm, g1_ref, b1_ref, wgu_hbm, wd_hbm,
             g2_ref, b2_ref, t_ref,
             du1_ref, mg_ref, h1b_ref, act_ref, dgu_ref, du2b_ref, dsb_ref, dsw_ref, st_ref,
             wo_ref, wgu_ref, wd_ref):
        @pl.when(pl.program_id(0) == 0)
        def _():
            pltpu.sync_copy(wo_hbm, wo_ref)
            pltpu.sync_copy(wgu_hbm, wgu_ref)
            pltpu.sync_copy(wd_hbm, wd_ref)
            st_ref[...] = jnp.zeros_like(st_ref)

        sb, sw = sb_ref[...], sw_ref[...]
        _, _, m_sb = _rms_fwd(sb, sbg_ref[...])
        _, _, m_sw = _rms_fwd(sw, swg_ref[...])
        m_sb = m_sb.astype(BF16)
        m_sw = m_sw.astype(BF16)
        mg_ref[:, :wsb] = m_sb
        mg_ref[:, wsb:] = m_sw
        xhat0, _ = _ln_hat(x_ref[...])
        u1 = ALPHA * (xhat0 * gi_ref[...] + bi_ref[...]) + _dot(m_sb, wo_ref[:wsb, :]) + _dot(m_sw, wo_ref[wsb:, :])

        xhat1, r1 = _ln_hat(u1)
        h1 = xhat1 * g1_ref[...] + b1_ref[...]
        h1b = h1.astype(BF16)
        h1b_ref[...] = h1b
        gate = _dot_nt(h1b, wgu_ref[:dff, :])
        up = _dot_nt(h1b, wgu_ref[dff:, :])
        sg = jax.nn.sigmoid(gate)
        silu = gate * sg
        act = (silu * up).astype(BF16)
        act_ref[...] = act
        u2 = ALPHA * h1 + _dot(act, wd_ref[...])
        xhat2, r2 = _ln_hat(u2)
        diff = xhat2 * g2_ref[...] + b2_ref[...] - t_ref[...]
        dh2 = diff * (1.0 / d)
        st_ref[0:1, :] += _colsum(dh2 * xhat2)
        st_ref[1:2, :] += _colsum(dh2)
        st_ref[2:3, :] += jnp.broadcast_to(_colsum(_rowsum(diff * diff)) * (0.5 / d), (1, d))
        du2 = _ln_bwd(dh2 * g2_ref[...], xhat2, r2)
        du2b = du2.astype(BF16)
        du2b_ref[...] = du2b
        dact = _dot_nt(du2b, wd_ref[...])
        dgate = (dact * up * (sg * (1.0 + gate * (1.0 - sg)))).astype(BF16)
        dup = (dact * silu).astype(BF16)
        dgu_ref[:, :dff] = dgate
        dgu_ref[:, dff:] = dup
        dh1 = _dot(dgate, wgu_ref[:dff, :]) + _dot(dup, wgu_ref[dff:, :]) + ALPHA * du2
        st_ref[3:4, :] += _colsum(dh1 * xhat1)
        st_ref[4:5, :] += _colsum(dh1)
        du1 = _ln_bwd(dh1 * g1_ref[...], xhat1, r1)
        du1_ref[...] = du1

        dmerged = _dot_nt(du1.astype(BF16), wo_ref[...])
        dsb, gsb = _rms_bwd(dmerged[:, :wsb], sb, sbg_ref[...])
        dsw, gsw = _rms_bwd(dmerged[:, wsb:], sw, swg_ref[...])
        dsb_ref[...] = dsb.astype(BF16)
        dsw_ref[...] = dsw.astype(BF16)
        st_ref[5:6, :wsb] += gsb
        st_ref[5:6, wsb:] += gsw

    row = lambda width: pl.BlockSpec((tm, width), lambda i: (i, 0))
    vec = lambda width: _full((1, width))
    hbm = pl.BlockSpec(memory_space=pl.ANY)
    bf = lambda width: jax.ShapeDtypeStruct((s, width), BF16)
    return pl.pallas_call(
        body, name="mix_ffn", grid=(s // tm,),
        in_specs=[row(wsb), row(wsw), row(d), vec(d), vec(d), vec(wsb), vec(wsw), hbm, vec(d), vec(d), hbm, hbm,
                  vec(d), vec(d), row(d)],
        out_specs=[row(d), row(d), row(d), row(dff), row(2 * dff), row(d), row(wsb), row(wsw), _full((8, d))],
        out_shape=[jax.ShapeDtypeStruct((s, d), F32), bf(d), bf(d), bf(dff), bf(2 * dff), bf(d), bf(wsb), bf(wsw),
                   jax.ShapeDtypeStruct((8, d), F32)],
        scratch_shapes=[pltpu.VMEM(w_out.shape, BF16), pltpu.VMEM(w_gu_t.shape, BF16), pltpu.VMEM(w_down.shape, BF16)],
        compiler_params=_params(60),
    )(sb_out, sw_out, x, g_in, b_in, sb_g, sw_g, w_out, g1, b1, w_gu_t, w_down, g2, b2, target)


def _rms_bwd(dm, o, g):
    n, r, _ = _rms_fwd(o, g)
    dn = dm * g
    return r * (dn - n * jnp.mean(dn * n, axis=-1, keepdims=True)), _colsum(dm * n)


def _sb_bwd(proj, dout, out, first, n_pairs, comm=None):
    s = proj.shape[0]
    t, per_step, n_steps = _sb_grid(s)
    width = n_pairs * LANES

    def body(q_ref, k_ref, v_ref, do_ref, o_ref, ab_ref, beta_ref, cmax_ref, dq_ref, dk_out, dv_out, dk_ref, dv_ref):
        step = pl.program_id(1)

        @pl.when(step == 0)
        def _():
            dk_ref[...] = jnp.zeros_like(dk_ref)
            dv_ref[...] = jnp.zeros_like(dv_ref)

        lane = lax.broadcasted_iota(jnp.int32, (1, LANES), 1)
        first_lanes = lane < HEAD_DIM
        upper, incl = _sb_triangles(t)

        def query_tile(j, _):
            rows = pl.ds(pl.multiple_of(j * t, t), t)
            do2 = do_ref[rows, :]
            qs = _sb_stack_heads(q_ref[rows, :] * SCALE, first_lanes)
            dos = _sb_stack_heads(do2, first_lanes)
            prod = do2.astype(F32) * o_ref[rows, :]
            totals = jnp.concatenate([_rowsum(jnp.where(first_lanes, prod, 0.0)),
                                      _rowsum(jnp.where(first_lanes, 0.0, prod))], axis=0)

            def grads(offs, k_t, v_t, a_b, beta, c_e, dq):
                d_e = _dot_nt(dos, v_t) * a_b.astype(F32)
                d_hi = d_e.astype(BF16)
                d_lo = (d_e - d_hi.astype(F32)).astype(BF16)
                cols = [slice(c * t, (c + 1) * t) for c in range(len(offs))]
                suf_e, c_e = _sb_suffix([[d_hi[:, c], d_lo[:, c]] for c in cols], [_rowsum(d_e[:, c]) for c in cols], incl, c_e)
                dzb = (d_e - beta * (d_e + (totals - suf_e))).astype(BF16)
                dk_t = _dot_tn(dzb, qs)
                dv_t = _dot_tn(a_b, dos)
                for off, c in zip(offs, cols):
                    dk_ref[pl.ds(off, t), :] += dk_t[c, :]
                    dv_ref[pl.ds(off, t), :] += dv_t[c, :]
                return c_e, dq + _dot(dzb, k_t)

            def first_visit(offs, carry):
                _, c_e, dq = carry
                k_t = _sb_key_tiles(k_ref, offs, t)
                v_t = _sb_key_tiles(v_ref, offs, t)
                c_e, dq = grads(offs, k_t, v_t, ab_ref[j], beta_ref[j], c_e, dq)
                go = jnp.max(cmax_ref[j]) > SB_UNDERFLOW
                zero = jnp.zeros((2 * t, 1), F32)
                rebuilt = lambda: _sb_scores(qs, k_t, upper, zero, step * per_step + j > 0, t)[2]
                return (lax.cond(go, rebuilt, lambda: zero), c_e, dq), go

            def visit(offs, carry):
                c_l, c_e, dq = carry
                k_t = _sb_key_tiles(k_ref, offs, t)
                v_t = _sb_key_tiles(v_ref, offs, t)
                lb, a, c_l = _sb_scores(qs, k_t, upper, c_l, None, t)
                c_e, dq = grads(offs, k_t, v_t, a.astype(BF16), jnp.exp(lb), c_e, dq)
                return c_l, c_e, dq

            init = (jnp.zeros((2 * t, 1), F32), jnp.zeros((2 * t, 1), F32), jnp.zeros((2 * t, LANES), F32))
            _, _, dq = _sb_walk(step * per_step + j, t, first_visit, visit, init)
            dq_ref[rows, :] = (jnp.where(first_lanes, dq[:t], dq[t:]) * SCALE).astype(BF16)
            return 0

        lax.fori_loop(0, per_step, query_tile, 0)

        @pl.when(step == n_steps - 1)
        def _():
            dk_out[...] = dk_ref[...].astype(BF16)
            dv_out[...] = dv_ref[...].astype(BF16)

    qblk = pl.BlockSpec((t * per_step, LANES), lambda h, i: (i, h))
    whole = pl.BlockSpec((s, LANES), lambda h, i: (0, h))
    first_specs, _ = _sb_first_specs(n_pairs, n_steps, per_step, t)
    return _call(
        body, "sb_bwd", (n_pairs, n_steps),
        in_specs=[qblk,
                  pl.BlockSpec((s, LANES), lambda h, i: (0, n_pairs + h)),
                  pl.BlockSpec((s, LANES), lambda h, i: (0, 2 * n_pairs + h)),
                  qblk, qblk] + first_specs,
        out_specs=[qblk, whole, whole],
        out_shape=[jax.ShapeDtypeStruct((s, width), BF16)] * 3,
        args=(proj, proj, proj, dout, out, *first),
        scratch_shapes=[pltpu.VMEM((s, LANES), F32), pltpu.VMEM((s, LANES), F32)], comm=comm)


def _swa_bwd(proj, dout, kept, bucket, n_heads, qcol, kcol, vcol, comm=None):
    s = proj.shape[0]
    width = n_heads * HEAD_DIM
    n_groups = LANES // HEAD_DIM
    per_group = n_heads // n_groups
    nb = s // SWA_BLOCK

    per_step, n_steps = _swa_steps(s)

    def body(q_ref, kp_ref, kc_ref, vp_ref, vc_ref, bucket_ref, do_ref, prob_ref, psink_ref,
             dq_ref, dk_out, dv_out, dsk_ref, drb_ref, dbias_ref, dk_ref, dv_ref, kall_ref, vall_ref):
        step = pl.program_id(0)

        @pl.when(step == 0)
        def _():
            dbias_ref[...] = jnp.zeros_like(dbias_ref)
            dk_ref[...] = jnp.zeros_like(dk_ref)
            dv_ref[...] = jnp.zeros_like(dv_ref)
            dsk_ref[...] = jnp.zeros_like(dsk_ref)

        _swa_stage_keys(kp_ref, kc_ref, kall_ref)
        _swa_stage_keys(vp_ref, vc_ref, vall_ref)
        lane = lax.broadcasted_iota(jnp.int32, (1, LANES), 1)
        first = lane < HEAD_DIM

        def query_block(j, _):
            i = step * per_step + j
            rows = pl.ds(pl.multiple_of(j * SWA_BLOCK, SWA_BLOCK), SWA_BLOCK)
            band = pl.ds(pl.multiple_of(j * SWA_BLOCK, SWA_BLOCK), 2 * SWA_BLOCK)
            q_blk, do_blk = q_ref[rows, :], do_ref[rows, :]
            kcat, vcat = kall_ref[band, :], vall_ref[band, :]
            dkcat = jnp.zeros((2 * SWA_BLOCK, LANES), F32)
            dvcat = jnp.zeros((2 * SWA_BLOCK, LANES), F32)
            pieces = {}
            for g in range(n_groups):
                sel = first if g == 0 else jnp.logical_not(first)
                q_g = _swa_stack(q_blk, g, per_group, sel, SCALE)
                do_g = _swa_stack(do_blk, g, per_group, sel)
                prob, p_sink = prob_ref[j, g], psink_ref[j, g]
                dprob = _dot_nt(do_g, vcat)
                delta = _rowsum(prob * dprob)
                dlog = prob * (dprob - delta)
                sink_term = p_sink * delta
                for hh in range(per_group):
                    h = g * per_group + hh
                    tot = _colsum(sink_term[hh * SWA_BLOCK:(hh + 1) * SWA_BLOCK, :])
                    dsk_ref[h:h + 1, :] += jnp.broadcast_to(-tot, (1, LANES))
                dbias_ref[g] += dlog
                dlb = dlog.astype(BF16)
                _swa_unstack(_dot(dlb, kcat) * SCALE, g, per_group, pieces)
                dkcat += _dot_tn(dlb, q_g)
                dvcat += _dot_tn(prob.astype(BF16), do_g)
            for c in range(n_heads // 2):
                dq_ref[rows, c * LANES:(c + 1) * LANES] = jnp.where(first, pieces[2 * c], pieces[2 * c + 1]).astype(BF16)

            cur = pl.multiple_of(i * SWA_BLOCK, SWA_BLOCK)
            dk_ref[pl.ds(cur, SWA_BLOCK), :] += dkcat[SWA_BLOCK:, :]
            dv_ref[pl.ds(cur, SWA_BLOCK), :] += dvcat[SWA_BLOCK:, :]

            @pl.when(i > 0)
            def _():
                prv = pl.multiple_of((i - 1) * SWA_BLOCK, SWA_BLOCK)
                dk_ref[pl.ds(prv, SWA_BLOCK), :] += dkcat[:SWA_BLOCK, :]
                dv_ref[pl.ds(prv, SWA_BLOCK), :] += dvcat[:SWA_BLOCK, :]

            return 0

        lax.fori_loop(0, per_step, query_block, 0)

        @pl.when(step == n_steps - 1)
        def _():
            bk = bucket_ref[...]
            rowi = lax.broadcasted_iota(jnp.int32, (REL_BUCKETS, LANES), 0)
            coli = lax.broadcasted_iota(jnp.int32, (REL_BUCKETS, LANES), 1)
            res = jnp.zeros((REL_BUCKETS, LANES), F32)
            for h in range(n_heads):
                g, hh = divmod(h, per_group)
                db = dbias_ref[g, hh * SWA_BLOCK:(hh + 1) * SWA_BLOCK, :]
                for b in range(REL_BUCKETS):
                    tot = _colsum(_rowsum(jnp.where(bk == b, db, 0.0)))
                    res = jnp.where((rowi == b) & (coli == h), tot, res)
            drb_ref[...] = res
            dk_out[...] = dk_ref[...].astype(BF16)
            dv_out[...] = dv_ref[...].astype(BF16)

    kept_specs, _ = _swa_kept_specs(nb, n_groups, per_group, per_step)
    rows_spec = pl.BlockSpec((per_step * SWA_BLOCK, width), lambda i: (i, 0))
    in_specs = _swa_specs(n_heads, qcol, kcol, vcol, per_step)[:6] + [rows_spec] + kept_specs
    staged = pltpu.VMEM(((per_step + 1) * SWA_BLOCK, LANES), BF16)
    return _call(
        body, "swa_bwd", (n_steps,),
        in_specs=in_specs,
        out_specs=[rows_spec, _full((s, LANES)), _full((s, LANES)), _full((8, LANES)), _full((REL_BUCKETS, LANES))],
        out_shape=[jax.ShapeDtypeStruct((s, width), BF16), jax.ShapeDtypeStruct((s, LANES), BF16),
                   jax.ShapeDtypeStruct((s, LANES), BF16), jax.ShapeDtypeStruct((8, LANES), F32),
                   jax.ShapeDtypeStruct((REL_BUCKETS, LANES), F32)],
        args=(proj, proj, proj, proj, proj, bucket, dout, *kept),
        scratch_shapes=[pltpu.VMEM((n_groups, per_group * SWA_BLOCK, 2 * SWA_BLOCK), F32),
                        pltpu.VMEM((s, LANES), F32), pltpu.VMEM((s, LANES), F32), staged, staged],
        comm=comm)


def _proj_bwd(pieces, w_in_t, du1, x, g_in, comm=None):
    s, d = x.shape
    cols = w_in_t.shape[0]
    tm = min(ROW_TILE, s)
    n_p = len(pieces)

    def body(*refs):
        p_refs = refs[:n_p]
        w_ref, du_ref, x_ref, g_ref, dx_ref, st_ref = refs[n_p:]
        i = pl.program_id(0)

        @pl.when(i == 0)
        def _():
            st_ref[...] = jnp.zeros_like(st_ref)

        dproj = jnp.concatenate([p[...] for p in p_refs], axis=1)
        dh0 = _dot(dproj, w_ref[...]) + ALPHA * du_ref[...]
        xhat, r = _ln_hat(x_ref[...])
        st_ref[0:1, :] += _colsum(dh0 * xhat)
        st_ref[1:2, :] += _colsum(dh0)
        dx_ref[...] = _ln_bwd(dh0 * g_ref[...], xhat, r)

    row = lambda width: pl.BlockSpec((tm, width), lambda i: (i, 0))
    return _call(
        body, "proj_bwd", (s // tm,),
        in_specs=[row(p.shape[1]) for p in pieces] + [_full((cols, d)), row(d), row(d), _full((1, d))],
        out_specs=[row(d), _full((8, d))],
        out_shape=[jax.ShapeDtypeStruct((s, d), F32), jax.ShapeDtypeStruct((8, d), F32)],
        args=(*pieces, w_in_t, du1, x, g_in), comm=comm)


def _wgrad(name, pieces, b, tm, tn):
    s, n = b.shape
    m = sum(p.shape[1] for p in pieces)
    n_p = len(pieces)
    assert n_p == 1 or tm == m
    ts = min(WGRAD_TOKENS if b.dtype == BF16 and n_p == 1 else WGRAD_TOKENS // 2, s)
    n_k = s // ts

    def body(*refs):
        p_refs, b_ref, o_ref, acc_ref = refs[:n_p], refs[n_p], refs[n_p + 1], refs[n_p + 2]
        k = pl.program_id(2)

        @pl.when(k == 0)
        def _():
            acc_ref[...] = jnp.zeros_like(acc_ref)

        a = p_refs[0][...] if n_p == 1 else jnp.concatenate([p[...] for p in p_refs], axis=1)
        acc_ref[...] += _dot_tn(a, b_ref[...].astype(BF16))

        @pl.when(k == n_k - 1)
        def _():
            o_ref[...] = acc_ref[...].astype(BF16)

    piece_spec = lambda p: pl.BlockSpec((ts, tm if n_p == 1 else p.shape[1]), lambda i, j, k: (k, i))
    return pl.pallas_call(
        body, name=name, grid=(m // tm, n // tn, n_k),
        in_specs=[piece_spec(p) for p in pieces] + [pl.BlockSpec((ts, tn), lambda i, j, k: (k, j))],
        out_specs=pl.BlockSpec((tm, tn), lambda i, j, k: (i, j)),
        out_shape=jax.ShapeDtypeStruct((m, n), BF16),
        scratch_shapes=[pltpu.VMEM((tm, tn), F32)],
        compiler_params=_params(),
    )(*pieces, b)


def _adamw_math(w, g, m, v):
    m = ADAM_B1 * m + (1.0 - ADAM_B1) * g
    v = ADAM_B2 * v + (1.0 - ADAM_B2) * (g * g)
    m_hat = m / (1.0 - ADAM_B1 ** ADAM_STEP)
    v_hat = v / (1.0 - ADAM_B2 ** ADAM_STEP)
    delta = -ADAM_LR * (m_hat / (jnp.sqrt(v_hat) + ADAM_EPS) + ADAM_WD * w)
    return delta, m, v


def _adamw_rows(rows):
    return max(r for r in range(16, 257, 16) if rows % r == 0)


def _adamw(name, landed, w, m, v, tr):
    rows, cols = w.shape

    def body(l_ref, w_ref, m_ref, v_ref, g_ref, d_ref, nm_ref, nv_ref):
        g = l_ref[0].astype(F32)
        for src in range(1, N_DEV):
            g = g + l_ref[src].astype(F32)
        delta, nm, nv = _adamw_math(w_ref[...], g, m_ref[...], v_ref[...])
        g_ref[...] = g
        d_ref[...] = delta
        nm_ref[...] = nm
        nv_ref[...] = nv

    blk = pl.BlockSpec((tr, cols), lambda i: (i, 0))
    shape = jax.ShapeDtypeStruct((rows, cols), F32)
    return pl.pallas_call(
        body, name=name, grid=(rows // tr,),
        in_specs=[pl.BlockSpec((N_DEV, tr, cols), lambda i: (0, i, 0)), blk, blk, blk],
        out_specs=[blk, blk, blk, blk],
        out_shape=[shape, shape, shape, shape],
        compiler_params=_params(),
    )(landed, w, m, v)


def _pack(d, ln_in_g, ln_in_b, ln1_g, ln1_b, ln2_g, ln2_b, sb_g, sw_g, rel_bias, sinks, extra=None):
    tail = [rel_bias.reshape(-1), sinks.reshape(-1)]
    if extra is not None:
        tail.append(extra.reshape(-1))
    tail = jnp.concatenate(tail)
    tail = jnp.concatenate([tail, jnp.zeros((d - tail.shape[0],), F32)])
    rows = [ln_in_g.reshape(-1), ln_in_b.reshape(-1), ln1_g.reshape(-1), ln1_b.reshape(-1),
            ln2_g.reshape(-1), ln2_b.reshape(-1),
            jnp.concatenate([sb_g.reshape(-1), sw_g.reshape(-1)]), tail]
    return jnp.stack(rows)


def _unpack(p, wsb, n_rb, n_sk):
    return [p[0], p[1], p[6, :wsb][None], p[6, wsb:][None], p[7, n_rb:n_rb + n_sk][None],
            p[7, :n_rb].reshape(REL_BUCKETS, -1), p[2][None], p[3][None], p[4][None], p[5][None]]


def kernel(x, ln_in_g, ln_in_b, w_in, sb_norm_g, swa_norm_g, sinks, rel_bias, w_out, ln1_g, ln1_b, w_gate_up, w_down, ln2_g, ln2_b, loss_target, m_ln_in_g, m_ln_in_b, m_w_in, m_sb_norm_g, m_swa_norm_g, m_sinks, m_rel_bias, m_w_out, m_ln1_g, m_ln1_b, m_w_gate_up, m_w_down, m_ln2_g, m_ln2_b, v_ln_in_g, v_ln_in_b, v_w_in, v_sb_norm_g, v_swa_norm_g, v_sinks, v_rel_bias, v_w_out, v_ln1_g, v_ln1_b, v_w_gate_up, v_w_down, v_ln2_g, v_ln2_b):
    x2 = x[0]
    tgt = loss_target[0]
    s, d = x2.shape
    wsb = sb_norm_g.shape[-1]
    wsw = swa_norm_g.shape[-1]
    n_sw_heads = sinks.shape[-1]
    n_pairs = wsb // LANES
    dff = w_down.shape[1] * N_DEV
    assert wsb % LANES == 0 and wsw % LANES == 0 and n_sw_heads * HEAD_DIM == wsw
    assert 3 * wsb % wsw == 0 and dff % LANES == 0 and s % SWA_BLOCK == 0
    qcol = 3 * wsb // wsw
    kcol = (3 * wsb + wsw) // LANES
    vcol = kcol + 1
    assert w_in.shape[-1] * N_DEV == (vcol + 1) * LANES

    t2 = lambda a: jnp.transpose(a[0])
    big_w = [t2(w_in), w_out[0], t2(w_gate_up), w_down[0]]
    big_m = [t2(m_w_in), m_w_out[0], t2(m_w_gate_up), m_w_down[0]]
    big_v = [t2(v_w_in), v_w_out[0], t2(v_w_gate_up), v_w_down[0]]

    cat_rows = lambda g: g.reshape(N_DEV * g.shape[1], g.shape[2])
    shards = [w.astype(BF16) for w in big_w]
    w_in_t = cat_rows(_allgather_via_sibling("w_in_allgather", shards[0]))

    vec = lambda a: a.reshape(1, -1)
    g_in, b_in = vec(ln_in_g), vec(ln_in_b)
    bucket = jnp.asarray(_swa_bucket_table())

    h0b, proj = _ln_proj(x2, g_in, b_in, w_in_t)
    sb_out, sb_first, gathered = _sb_fwd(proj, n_pairs, comm=(shards[1:3], ["gather"] * 2))
    w_out_f, w_gu_t = cat_rows(gathered[0]), cat_rows(gathered[1])
    sw_out, sw_kept, gathered = _swa_fwd(proj, bucket, rel_bias, sinks, n_sw_heads, qcol, kcol, vcol,
                                comm=(shards[3:], ["gather"]))
    w_down_f = cat_rows(gathered[0])
    du1, merged, h1b, act, dgu, du2b, dsb, dsw, st_ffn = _mix_ffn(
        sb_out, sw_out, x2, g_in, b_in, sb_norm_g, swa_norm_g, w_out_f, ln1_g, ln1_b, w_gu_t, w_down_f, ln2_g, ln2_b, tgt)

    split_rows = lambda g: g.reshape(N_DEV, g.shape[0] // N_DEV, g.shape[1])
    gw_gu = _wgrad("wgrad_gate_up", [dgu], h1b, dff // 2, d)
    gw_down = _wgrad("wgrad_down", [act], du2b, dff // 2, d)
    gw_out = _wgrad("wgrad_out", [merged], du1, min(512, d), d)
    (dq_sb, dk_sb, dv_sb), (land_gu, land_out) = _sb_bwd(
        proj, dsb, sb_out, sb_first, n_pairs, comm=([split_rows(gw_gu), split_rows(gw_out)], ["scatter"] * 2))
    (dq_sw, dk_sw, dv_sw, st_sink, st_rb), (land_down,) = _swa_bwd(
        proj, dsw, sw_kept, bucket, n_sw_heads, qcol, kcol, vcol, comm=([split_rows(gw_down)], ["scatter"]))
    pieces = [dq_sb, dk_sb, dv_sb, dq_sw, dk_sw, dv_sw]
    gw_in = _wgrad("wgrad_in", pieces, h0b, proj.shape[1], d)
    (grad_x, st_in), (land_in,) = _proj_bwd(pieces, w_in_t, du1, x2, g_in, comm=([split_rows(gw_in)], ["scatter"]))

    n_rb = rel_bias.size
    small = _pack(d, st_in[0], st_in[1], st_ffn[3], st_ffn[4], st_ffn[0], st_ffn[1],
                  st_ffn[5, :wsb], st_ffn[5, wsb:], st_rb[:, :n_sw_heads], st_sink[:n_sw_heads, 0],
                  extra=st_ffn[2, 0:1])
    land_small = _exchange("small_grads_allgather", [small], ["gather"])[0]
    landed = [land_in, land_out, land_gu, land_down, land_small]

    big = []
    for name, land, w, m, v in zip(["adamw_in", "adamw_out", "adamw_gate_up", "adamw_down"], landed[:4], big_w, big_m, big_v):
        big.append(_adamw(name, land, w, m, v, _adamw_rows(w.shape[0])))

    small_w = _pack(d, ln_in_g, ln_in_b, ln1_g, ln1_b, ln2_g, ln2_b, sb_norm_g, swa_norm_g, rel_bias, sinks)
    small_m = _pack(d, m_ln_in_g, m_ln_in_b, m_ln1_g, m_ln1_b, m_ln2_g, m_ln2_b, m_sb_norm_g, m_swa_norm_g, m_rel_bias, m_sinks)
    small_v = _pack(d, v_ln_in_g, v_ln_in_b, v_ln1_g, v_ln1_b, v_ln2_g, v_ln2_b, v_sb_norm_g, v_swa_norm_g, v_rel_bias, v_sinks)
    sg, sd, sm, sv = _adamw("adamw_small", landed[4], small_w, small_m, small_v, 8)
    n_sk = sinks.size
    loss = sg[7, n_rb + n_sk]

    def leaves(idx):
        sm_l = _unpack([sg, sd, sm, sv][idx], wsb, n_rb, n_sk)
        bg = [jnp.transpose(big[0][idx])[None], big[1][idx][None], jnp.transpose(big[2][idx])[None], big[3][idx][None]]
        return [sm_l[0], sm_l[1], bg[0], sm_l[2], sm_l[3], sm_l[4], sm_l[5], bg[1], sm_l[6], sm_l[7], bg[2], bg[3], sm_l[8], sm_l[9]]

    return (loss, grad_x[None], *leaves(0), *leaves(1), *leaves(2), *leaves(3))
```

```python
import functools
import math

import numpy as np
import jax
import jax.numpy as jnp
from jax import lax
from jax.experimental import pallas as pl
from jax.experimental.pallas import tpu as pltpu

F32 = jnp.float32
BF16 = jnp.bfloat16
MESH = pl.DeviceIdType.MESH

N_DEV = 8
LANES = 128
HEAD_DIM = 64
SCALE = HEAD_DIM ** -0.5
SWA_BLOCK = 128
SWA_BLOCKS_PER_STEP = 4
REL_BUCKETS = 32
REL_MAX_DIST = 128
ALPHA = 2.0 ** 0.25
LN_EPS = 1e-5
RMS_EPS = 1e-6
ADAM_LR = 0.001
ADAM_B1 = 0.9
ADAM_B2 = 0.999
ADAM_EPS = 1e-08
ADAM_WD = 0.01
ADAM_STEP = 10

ROW_TILE = 512
SB_TILE = 256
SB_TILES_PER_STEP = 4
FFN_TILE = 256
WGRAD_TOKENS = 2048
SB_UNDERFLOW = -110.0
SB_MASKED = -1e30
MIB = 1024 * 1024


def _params(vmem_mib=48):
    return pltpu.CompilerParams(vmem_limit_bytes=vmem_mib * MIB)


def _dot(a, b):
    return jnp.dot(a, b, preferred_element_type=F32)


def _dot_nt(a, b):
    return lax.dot_general(a, b, (((1,), (1,)), ((), ())), preferred_element_type=F32)


def _dot_tn(a, b):
    return lax.dot_general(a, b, (((0,), (0,)), ((), ())), preferred_element_type=F32)


def _ln_hat(x):
    mu = jnp.mean(x, axis=-1, keepdims=True)
    xc = x - mu
    var = jnp.mean(xc * xc, axis=-1, keepdims=True)
    r = lax.rsqrt(var + LN_EPS)
    return xc * r, r


def _ln_bwd(dxhat, xhat, r):
    return r * (dxhat - jnp.mean(dxhat, axis=-1, keepdims=True)
                - xhat * jnp.mean(dxhat * xhat, axis=-1, keepdims=True))


def _colsum(a):
    return jnp.sum(a, axis=0, keepdims=True)


def _rowsum(a):
    return jnp.sum(a, axis=1, keepdims=True)


def _full(shape):
    return pl.BlockSpec(shape, lambda *_: (0,) * len(shape))


def _comm_out_shapes(arrays, kinds):
    shapes = []
    for a, kind in zip(arrays, kinds):
        blk = a.shape if kind == "gather" else a.shape[1:]
        shapes.append(jax.ShapeDtypeStruct((N_DEV,) + tuple(blk), a.dtype))
    return shapes


def _comm_sems(n):
    return [pltpu.SemaphoreType.DMA((n, N_DEV - 1)), pltpu.SemaphoreType.DMA((n, N_DEV - 1)),
            pltpu.SemaphoreType.DMA((n,))]


def _comm_copies(ins, outs, kinds, send_sems, recv_sems, local_sems):
    x, y, c = lax.axis_index("x"), lax.axis_index("y"), lax.axis_index("c")
    me = 4 * x + 2 * y + c

    def src_for(t, dev_lin):
        return ins[t] if kinds[t] == "gather" else ins[t].at[dev_lin]

    local = [pltpu.make_async_copy(src_for(t, me), outs[t].at[me], local_sems.at[t]) for t in range(len(kinds))]
    sends, arrivals = [], []
    for k in range(1, N_DEV):
        px = 1 - x if (k >> 2) & 1 else x
        py = 1 - y if (k >> 1) & 1 else y
        pc = 1 - c if k & 1 else c
        peer_lin = 4 * px + 2 * py + pc
        for t in range(len(kinds)):
            sems = dict(send_sem=send_sems.at[t, k - 1], recv_sem=recv_sems.at[t, k - 1],
                        device_id=(px, py, pc), device_id_type=MESH)
            sends.append(pltpu.make_async_remote_copy(src_ref=src_for(t, peer_lin), dst_ref=outs[t].at[me], **sems))
            arrivals.append(pltpu.make_async_remote_copy(src_ref=src_for(t, peer_lin), dst_ref=outs[t].at[peer_lin], **sems))
    return local, sends, arrivals


def _comm_start(ins, outs, kinds, sems):
    local, sends, _ = _comm_copies(ins, outs, kinds, *sems)
    for cp in local + sends:
        cp.start()


def _comm_finish(ins, outs, kinds, sems):
    local, sends, arrivals = _comm_copies(ins, outs, kinds, *sems)
    for cp in arrivals:
        cp.wait_recv()
    for cp in sends:
        cp.wait_send()
    for cp in local:
        cp.wait()


def _exchange(name, arrays, kinds):
    n = len(arrays)

    def body(*refs):
        ins, outs, sems = refs[:n], refs[n:2 * n], refs[2 * n:]
        _comm_start(ins, outs, kinds, sems)
        _comm_finish(ins, outs, kinds, sems)

    any_spec = pl.BlockSpec(memory_space=pl.ANY)
    return pl.pallas_call(
        body, name=name, out_shape=_comm_out_shapes(arrays, kinds),
        in_specs=[any_spec] * n, out_specs=[any_spec] * n,
        scratch_shapes=_comm_sems(n),
    )(*arrays)


def _allgather_via_sibling(name, shard):
    def body(x_ref, out_ref, send_sems, recv_sems, local_sem):
        x, y, c = lax.axis_index("x"), lax.axis_index("y"), lax.axis_index("c")
        me, sibling = (x, y, c), (x, y, 1 - c)
        chips = [(1 - x, y), (x, 1 - y), (1 - x, 1 - y)]

        def copy(k, block, to, src=None):
            slot = out_ref.at[4 * block[0] + 2 * block[1] + block[2]]
            return pltpu.make_async_remote_copy(
                src_ref=slot if src is None else src, dst_ref=slot,
                send_sem=send_sems.at[k], recv_sem=recv_sems.at[k], device_id=to, device_id_type=MESH)

        mine = pltpu.make_async_copy(x_ref, out_ref.at[4 * x + 2 * y + c], local_sem)
        mine.start()
        first = [copy(0, me, sibling, src=x_ref)]
        first += [copy(1 + j, me, (*chip, c), src=x_ref) for j, chip in enumerate(chips)]
        for cp in first:
            cp.start()
        passed = [copy(4 + j, (*chip, c), sibling) for j, chip in enumerate(chips)]
        for j, chip in enumerate(chips):
            copy(1 + j, (*chip, c), me).wait_recv()
            passed[j].start()
        copy(0, sibling, me).wait_recv()
        for j, chip in enumerate(chips):
            copy(4 + j, (*chip, 1 - c), me).wait_recv()
        for cp in first + passed:
            cp.wait_send()
        mine.wait()

    any_spec = pl.BlockSpec(memory_space=pl.ANY)
    return pl.pallas_call(
        body, name=name, out_shape=jax.ShapeDtypeStruct((N_DEV,) + shard.shape, shard.dtype),
        in_specs=[any_spec], out_specs=any_spec,
        scratch_shapes=[pltpu.SemaphoreType.DMA((N_DEV - 1,)), pltpu.SemaphoreType.DMA((N_DEV - 1,)),
                        pltpu.SemaphoreType.DMA],
    )(shard)


def _call(body, name, grid, in_specs, out_specs, out_shape, args, scratch_shapes=(), comm=None):
    if comm is None:
        outs = pl.pallas_call(body, name=name, grid=grid, in_specs=in_specs, out_specs=out_specs,
                              out_shape=out_shape, scratch_shapes=list(scratch_shapes),
                              compiler_params=_params())(*args)
        return outs, []
    arrays, kinds = comm
    n, n_in, n_out, n_scr = len(arrays), len(in_specs), len(out_specs), len(scratch_shapes)

    def fused(*refs):
        c_in, x_in = refs[:n_in], refs[n_in:n_in + n]
        c_out = refs[n_in + n:n_in + n + n_out]
        x_out = refs[n_in + n + n_out:n_in + 2 * n + n_out]
        rest = refs[n_in + 2 * n + n_out:]
        c_scr, sems = rest[:n_scr], rest[n_scr:]
        ids = [pl.program_id(a) for a in range(len(grid))]
        is_first = functools.reduce(jnp.logical_and, [i == 0 for i in ids])
        is_last = functools.reduce(jnp.logical_and, [i == g - 1 for i, g in zip(ids, grid)])

        @pl.when(is_first)
        def _():
            _comm_start(x_in, x_out, kinds, sems)

        body(*c_in, *c_out, *c_scr)

        @pl.when(is_last)
        def _():
            _comm_finish(x_in, x_out, kinds, sems)

    any_spec = pl.BlockSpec(memory_space=pl.ANY)
    outs = pl.pallas_call(
        fused, name=name, grid=grid,
        in_specs=list(in_specs) + [any_spec] * n, out_specs=list(out_specs) + [any_spec] * n,
        out_shape=list(out_shape) + _comm_out_shapes(arrays, kinds),
        scratch_shapes=list(scratch_shapes) + _comm_sems(n),
        compiler_params=_params())(*args, *arrays)
    return outs[:n_out], outs[n_out:]


def _ln_proj(x, g, b, w_in_t):
    s, d = x.shape
    cols = w_in_t.shape[0]
    tm = min(ROW_TILE, s)

    def body(x_ref, g_ref, b_ref, w_ref, h_ref, p_ref):
        xhat, _ = _ln_hat(x_ref[...])
        h = (xhat * g_ref[...] + b_ref[...]).astype(BF16)
        h_ref[...] = h
        p_ref[...] = _dot_nt(h, w_ref[...]).astype(BF16)

    row = lambda width: pl.BlockSpec((tm, width), lambda i: (i, 0))
    return pl.pallas_call(
        body, name="ln_proj", grid=(s // tm,),
        in_specs=[row(d), _full((1, d)), _full((1, d)), _full((cols, d))],
        out_specs=[row(d), row(cols)],
        out_shape=[jax.ShapeDtypeStruct((s, d), BF16), jax.ShapeDtypeStruct((s, cols), BF16)],
        compiler_params=_params(),
    )(x, g, b, w_in_t)


def _sb_triangles(t):
    row = lax.broadcasted_iota(jnp.int32, (t, t), 0)
    col = lax.broadcasted_iota(jnp.int32, (t, t), 1)
    return (row > col).astype(BF16), (row >= col).astype(BF16)


def _sb_first_mask(z, t, has_prev):
    qrow = lax.broadcasted_iota(jnp.int32, (2 * t, t), 0) & (t - 1)
    col = lax.broadcasted_iota(jnp.int32, (2 * t, t), 1)
    return jnp.concatenate([jnp.where(has_prev, z[:, :t], SB_MASKED), jnp.where(col < qrow, z[:, t:], SB_MASKED)], axis=1)


def _sb_stack_heads(x2, first):
    zero = jnp.zeros_like(x2)
    return jnp.concatenate([jnp.where(first, x2, zero), jnp.where(first, zero, x2)], axis=0)


def _sb_key_tiles(ref, offs, t):
    tiles = [ref[pl.ds(off, t), :] for off in offs]
    return tiles[0] if len(tiles) == 1 else jnp.concatenate(tiles, axis=0)


def _sb_suffix(terms, row_sums, tri, carry):
    out = [None] * len(terms)
    for j in reversed(range(len(terms))):
        suf = carry
        for op in terms[j]:
            suf = suf + _dot(op, tri)
        out[j] = suf
        carry = carry + row_sums[j]
    return (out[0] if len(out) == 1 else jnp.concatenate(out, axis=1)), carry


def _sb_scores(qh, k_t, upper, carry_l, has_prev, t):
    z = _dot_nt(qh, k_t)
    if has_prev is not None:
        z = _sb_first_mask(z, t, has_prev)
    sp = jnp.log(1.0 + jnp.exp(-jnp.abs(z)))
    neg = jnp.minimum(z, 0.0)
    lb = neg - sp
    l1 = (neg - z) - sp
    hi = l1.astype(BF16)
    lo = (l1 - hi.astype(F32)).astype(BF16)
    cols = [slice(j * t, (j + 1) * t) for j in range(z.shape[1] // t)]
    suf, carry_l = _sb_suffix([[hi[:, c], lo[:, c]] for c in cols], [_rowsum(l1[:, c]) for c in cols], upper, carry_l)
    return lb, jnp.exp(lb + suf), carry_l


def _sb_walk(i, t, first_visit, visit, init):
    def alive(carry):
        return jnp.max(carry[0]) > SB_UNDERFLOW

    prev = pl.multiple_of(jnp.maximum(i - 1, 0) * t, t)
    carry = first_visit((prev, pl.multiple_of(i * t, t)), init)

    def cond(state):
        j, go, _ = state
        return (j < i - 1) & go

    def body(state):
        j, _, carry = state
        carry = visit((pl.multiple_of((i - 2 - j) * t, t),), carry)
        return j + 1, alive(carry), carry

    return lax.while_loop(cond, body, (jnp.int32(0), alive(carry), carry))[2]


def _sb_first_specs(n_pairs, n_steps, per_step, t):
    at = lambda h, i: (h * n_steps + i, 0, 0)
    n_tiles = n_pairs * n_steps * per_step
    specs = [pl.BlockSpec((per_step, 2 * t, 2 * t), at), pl.BlockSpec((per_step, 2 * t, 2 * t), at),
             pl.BlockSpec((per_step, 2 * t, 1), at)]
    shapes = [jax.ShapeDtypeStruct((n_tiles, 2 * t, 2 * t), BF16), jax.ShapeDtypeStruct((n_tiles, 2 * t, 2 * t), F32),
              jax.ShapeDtypeStruct((n_tiles, 2 * t, 1), F32)]
    return specs, shapes


def _sb_grid(s):
    t = min(SB_TILE, s)
    per_step = min(SB_TILES_PER_STEP, s // t)
    return t, per_step, s // (t * per_step)


def _sb_fwd(proj, n_pairs, comm=None):
    s = proj.shape[0]
    t, per_step, n_steps = _sb_grid(s)

    def body(q_ref, k_ref, v_ref, o_ref, ab_ref, beta_ref, cl_ref):
        lane = lax.broadcasted_iota(jnp.int32, (1, LANES), 1)
        first = lane < HEAD_DIM
        upper, _ = _sb_triangles(t)

        def query_tile(j, _):
            i = pl.program_id(1) * per_step + j
            rows = pl.ds(pl.multiple_of(j * t, t), t)
            qs = _sb_stack_heads(q_ref[rows, :] * SCALE, first)

            def first_visit(offs, carry):
                c_l, acc = carry
                lb, a, c_l = _sb_scores(qs, _sb_key_tiles(k_ref, offs, t), upper, c_l, i > 0, t)
                a_b = a.astype(BF16)
                ab_ref[j] = a_b
                beta_ref[j] = jnp.exp(lb)
                cl_ref[j] = c_l
                return c_l, acc + _dot(a_b, _sb_key_tiles(v_ref, offs, t))

            def visit(offs, carry):
                c_l, acc = carry
                _, a, c_l = _sb_scores(qs, _sb_key_tiles(k_ref, offs, t), upper, c_l, None, t)
                return c_l, acc + _dot(a.astype(BF16), _sb_key_tiles(v_ref, offs, t))

            init = (jnp.zeros((2 * t, 1), F32), jnp.zeros((2 * t, LANES), F32))
            _, acc = _sb_walk(i, t, first_visit, visit, init)
            o_ref[rows, :] = jnp.where(first, acc[:t], acc[t:])
            return 0

        lax.fori_loop(0, per_step, query_tile, 0)

    qblk = pl.BlockSpec((t * per_step, LANES), lambda h, i: (i, h))
    first_specs, first_shapes = _sb_first_specs(n_pairs, n_steps, per_step, t)
    outs, landed = _call(
        body, "sb_fwd", (n_pairs, n_steps),
        in_specs=[qblk,
                  pl.BlockSpec((s, LANES), lambda h, i: (0, n_pairs + h)),
                  pl.BlockSpec((s, LANES), lambda h, i: (0, 2 * n_pairs + h))],
        out_specs=[qblk] + first_specs,
        out_shape=[jax.ShapeDtypeStruct((s, n_pairs * LANES), F32)] + first_shapes,
        args=(proj, proj, proj), comm=comm)
    return outs[0], outs[1:], landed


def _swa_bucket_table():
    qi = np.arange(SWA_BLOCK)[:, None]
    cj = np.arange(2 * SWA_BLOCK)[None, :]
    dist = qi + SWA_BLOCK - cj
    exact = REL_BUCKETS // 2
    d = np.maximum(dist, 0)
    d_f = np.maximum(d, 1).astype(np.float32)
    large = exact + (np.log(d_f / np.float32(exact)) / np.float32(math.log(REL_MAX_DIST / exact))
                     * np.float32(REL_BUCKETS - exact)).astype(np.int32)
    large = np.minimum(large, REL_BUCKETS - 1)
    return np.where(d < exact, d, large).astype(np.int32)


def _swa_build_bias(bucket_ref, rb_ref, bias_ref, n_groups, per_group):
    bk = bucket_ref[...]
    dist = (lax.broadcasted_iota(jnp.int32, bk.shape, 0) + SWA_BLOCK) - lax.broadcasted_iota(jnp.int32, bk.shape, 1)
    window = (dist >= 0) & (dist < SWA_BLOCK)
    for g in range(n_groups):
        for hh in range(per_group):
            acc = jnp.zeros(bk.shape, F32)
            for b in range(REL_BUCKETS):
                acc = jnp.where(bk == b, rb_ref[b, g * per_group + hh], acc)
            bias_ref[g, hh * SWA_BLOCK:(hh + 1) * SWA_BLOCK, :] = jnp.where(window, acc, -jnp.inf)


def _swa_first_block_mask(i):
    col = lax.broadcasted_iota(jnp.int32, (1, 2 * SWA_BLOCK), 1)
    return jnp.where((col < SWA_BLOCK) & (i == 0), -jnp.inf, 0.0)


def _swa_place(blk, h, group, sel):
    if (h % 2) != group:
        blk = pltpu.roll(blk.astype(F32), HEAD_DIM, axis=1).astype(BF16)
    return jnp.where(sel, blk, jnp.zeros_like(blk))


def _swa_stack(ref, group, per_group, sel, scale=1.0):
    parts = []
    for hh in range(per_group):
        h = group * per_group + hh
        parts.append(_swa_place(ref[:, (h // 2) * LANES:(h // 2 + 1) * LANES], h, group, sel))
    stacked = jnp.concatenate(parts, axis=0)
    return stacked if scale == 1.0 else stacked * scale


def _swa_unstack(stacked, group, per_group, pieces):
    for hh in range(per_group):
        h = group * per_group + hh
        piece = stacked[hh * SWA_BLOCK:(hh + 1) * SWA_BLOCK, :]
        pieces[h] = pltpu.roll(piece, HEAD_DIM, axis=1) if (h % 2) != group else piece


def _swa_sink_rows(sk_ref, group, per_group):
    rowh = lax.broadcasted_iota(jnp.int32, (per_group * SWA_BLOCK, 1), 0) // SWA_BLOCK
    sink = jnp.zeros((per_group * SWA_BLOCK, 1), F32) + sk_ref[0, group * per_group]
    for hh in range(1, per_group):
        sink = jnp.where(rowh == hh, sk_ref[0, group * per_group + hh], sink)
    return sink


def _swa_probs(q_pos, kcat, bias_h, first_mask, sink):
    logits = _dot_nt(q_pos, kcat) + (bias_h + first_mask)
    m = jnp.maximum(jnp.max(logits, axis=1, keepdims=True), sink)
    p = jnp.exp(logits - m)
    es = jnp.exp(sink - m)
    inv = 1.0 / (_rowsum(p) + es)
    return p * inv, es * inv


def _swa_steps(s):
    per_step = min(SWA_BLOCKS_PER_STEP, s // SWA_BLOCK)
    return per_step, s // (SWA_BLOCK * per_step)


def _swa_specs(n_heads, qcol, kcol, vcol, per_step):
    width = n_heads * HEAD_DIM
    prev = lambda col: pl.BlockSpec((SWA_BLOCK, LANES), lambda i: (jnp.maximum(i * per_step - 1, 0), col))
    cur = lambda col: pl.BlockSpec((per_step * SWA_BLOCK, LANES), lambda i: (i, col))
    return [pl.BlockSpec((per_step * SWA_BLOCK, width), lambda i: (i, qcol)),
            prev(kcol), cur(kcol), prev(vcol), cur(vcol),
            _full((SWA_BLOCK, 2 * SWA_BLOCK)),
            pl.BlockSpec(memory_space=pltpu.SMEM), pl.BlockSpec(memory_space=pltpu.SMEM)]


def _swa_stage_keys(prev_ref, cur_ref, all_ref):
    all_ref[:SWA_BLOCK, :] = prev_ref[...]
    all_ref[SWA_BLOCK:, :] = cur_ref[...]


def _swa_kept_specs(nb, n_groups, per_group, per_step):
    rows = per_group * SWA_BLOCK
    at = lambda i: (i, 0, 0, 0)
    specs = [pl.BlockSpec((per_step, n_groups, rows, 2 * SWA_BLOCK), at), pl.BlockSpec((per_step, n_groups, rows, 1), at)]
    shapes = [jax.ShapeDtypeStruct((nb, n_groups, rows, 2 * SWA_BLOCK), F32), jax.ShapeDtypeStruct((nb, n_groups, rows, 1), F32)]
    return specs, shapes


def _swa_fwd(proj, bucket, rel_bias, sinks, n_heads, qcol, kcol, vcol, comm=None):
    s = proj.shape[0]
    width = n_heads * HEAD_DIM
    n_groups = LANES // HEAD_DIM
    per_group = n_heads // n_groups

    per_step, n_steps = _swa_steps(s)

    def body(q_ref, kp_ref, kc_ref, vp_ref, vc_ref, bucket_ref, rb_ref, sk_ref, o_ref, prob_ref, psink_ref,
             bias_ref, kall_ref, vall_ref):
        step = pl.program_id(0)

        @pl.when(step == 0)
        def _():
            _swa_build_bias(bucket_ref, rb_ref, bias_ref, n_groups, per_group)

        _swa_stage_keys(kp_ref, kc_ref, kall_ref)
        _swa_stage_keys(vp_ref, vc_ref, vall_ref)
        lane = lax.broadcasted_iota(jnp.int32, (1, LANES), 1)
        first = lane < HEAD_DIM

        def query_block(j, _):
            rows = pl.ds(pl.multiple_of(j * SWA_BLOCK, SWA_BLOCK), SWA_BLOCK)
            band = pl.ds(pl.multiple_of(j * SWA_BLOCK, SWA_BLOCK), 2 * SWA_BLOCK)
            first_mask = _swa_first_block_mask(step * per_step + j)
            q_blk = q_ref[rows, :]
            kcat, vcat = kall_ref[band, :], vall_ref[band, :]
            pieces = {}
            for g in range(n_groups):
                sel = first if g == 0 else jnp.logical_not(first)
                prob, p_sink = _swa_probs(_swa_stack(q_blk, g, per_group, sel, SCALE), kcat, bias_ref[g], first_mask,
                                          _swa_sink_rows(sk_ref, g, per_group))
                prob_ref[j, g] = prob
                psink_ref[j, g] = p_sink
                _swa_unstack(_dot(prob.astype(BF16), vcat), g, per_group, pieces)
            for c in range(n_heads // 2):
                o_ref[rows, c * LANES:(c + 1) * LANES] = jnp.where(first, pieces[2 * c], pieces[2 * c + 1])
            return 0

        lax.fori_loop(0, per_step, query_block, 0)

    kept_specs, kept_shapes = _swa_kept_specs(s // SWA_BLOCK, n_groups, per_group, per_step)
    staged = pltpu.VMEM(((per_step + 1) * SWA_BLOCK, LANES), BF16)
    outs, landed = _call(
        body, "swa_fwd", (n_steps,),
        in_specs=_swa_specs(n_heads, qcol, kcol, vcol, per_step),
        out_specs=[pl.BlockSpec((per_step * SWA_BLOCK, width), lambda i: (i, 0))] + kept_specs,
        out_shape=[jax.ShapeDtypeStruct((s, width), F32)] + kept_shapes,
        args=(proj, proj, proj, proj, proj, bucket, rel_bias, sinks),
        scratch_shapes=[pltpu.VMEM((n_groups, per_group * SWA_BLOCK, 2 * SWA_BLOCK), F32), staged, staged], comm=comm)
    return outs[0], outs[1:], landed


def _rms_fwd(o, g):
    r = lax.rsqrt(jnp.mean(o * o, axis=-1, keepdims=True) + RMS_EPS)
    n = o * r
    return n, r, n * g


def _mix_ffn(sb_out, sw_out, x, g_in, b_in, sb_g, sw_g, w_out, g1, b1, w_gu_t, w_down, g2, b2, target):
    s, d = x.shape
    wsb, wsw = sb_out.shape[1], sw_out.shape[1]
    dff = w_down.shape[0]
    assert wsb + wsw == d
    tm = min(FFN_TILE, s)

    def body(sb_ref, sw_ref, x_ref, gi_ref, bi_ref, sbg_ref, swg_ref, wo_hbm, g1_ref, b1_ref, wgu_hbm, wd_hbm,
             g2_ref, b2_ref, t_ref,
             du1_ref, mg_ref, h1b_ref, act_ref, dgu_ref, du2b_ref, dsb_ref, dsw_ref, st_ref,
             wo_ref, wgu_ref, wd_ref):
        @pl.when(pl.program_id(0) == 0)
        def _():
            pltpu.sync_copy(wo_hbm, wo_ref)
            pltpu.sync_copy(wgu_hbm, wgu_ref)
            pltpu.sync_copy(wd_hbm, wd_ref)
            st_ref[...] = jnp.zeros_like(st_ref)

        sb, sw = sb_ref[...], sw_ref[...]
        _, _, m_sb = _rms_fwd(sb, sbg_ref[...])
        _, _, m_sw = _rms_fwd(sw, swg_ref[...])
        m_sb = m_sb.astype(BF16)
        m_sw = m_sw.astype(BF16)
        mg_ref[:, :wsb] = m_sb
        mg_ref[:, wsb:] = m_sw
        xhat0, _ = _ln_hat(x_ref[...])
        u1 = ALPHA * (xhat0 * gi_ref[...] + bi_ref[...]) + _dot(m_sb, wo_ref[:wsb, :]) + _dot(m_sw, wo_ref[wsb:, :])

        xhat1, r1 = _ln_hat(u1)
        h1 = xhat1 * g1_ref[...] + b1_ref[...]
        h1b = h1.astype(BF16)
        h1b_ref[...] = h1b
        gate = _dot_nt(h1b, wgu_ref[:dff, :])
        up = _dot_nt(h1b, wgu_ref[dff:, :])
        sg = jax.nn.sigmoid(gate)
        silu = gate * sg
        act = (silu * up).astype(BF16)
        act_ref[...] = act
        u2 = ALPHA * h1 + _dot(act, wd_ref[...])
        xhat2, r2 = _ln_hat(u2)
        diff = xhat2 * g2_ref[...] + b2_ref[...] - t_ref[...]
        dh2 = diff * (1.0 / d)
        st_ref[0:1, :] += _colsum(dh2 * xhat2)
        st_ref[1:2, :] += _colsum(dh2)
        st_ref[2:3, :] += jnp.broadcast_to(_colsum(_rowsum(diff * diff)) * (0.5 / d), (1, d))
        du2 = _ln_bwd(dh2 * g2_ref[...], xhat2, r2)
        du2b = du2.astype(BF16)
        du2b_ref[...] = du2b
        dact = _dot_nt(du2b, wd_ref[...])
        dgate = (dact * up * (sg * (1.0 + gate * (1.0 - sg)))).astype(BF16)
        dup = (dact * silu).astype(BF16)
        dgu_ref[:, :dff] = dgate
        dgu_ref[:, dff:] = dup
        dh1 = _dot(dgate, wgu_ref[:dff, :]) + _dot(dup, wgu_ref[dff:, :]) + ALPHA * du2
        st_ref[3:4, :] += _colsum(dh1 * xhat1)
        st_ref[4:5, :] += _colsum(dh1)
        du1 = _ln_bwd(dh1 * g1_ref[...], xhat1, r1)
        du1_ref[...] = du1

        dmerged = _dot_nt(du1.astype(BF16), wo_ref[...])
        dsb, gsb = _rms_bwd(dmerged[:, :wsb], sb, sbg_ref[...])
        dsw, gsw = _rms_bwd(dmerged[:, wsb:], sw, swg_ref[...])
        dsb_ref[...] = dsb.astype(BF16)
        dsw_ref[...] = dsw.astype(BF16)
        st_ref[5:6, :wsb] += gsb
        st_ref[5:6, wsb:] += gsw

    row = lambda width: pl.BlockSpec((tm, width), lambda i: (i, 0))
    vec = lambda width: _full((1, width))
    hbm = pl.BlockSpec(memory_space=pl.ANY)
    bf = lambda width: jax.ShapeDtypeStruct((s, width), BF16)
    return pl.pallas_call(
        body, name="mix_ffn", grid=(s // tm,),
        in_specs=[row(wsb), row(wsw), row(d), vec(d), vec(d), vec(wsb), vec(wsw), hbm, vec(d), vec(d), hbm, hbm,
                  vec(d), vec(d), row(d)],
        out_specs=[row(d), row(d), row(d), row(dff), row(2 * dff), row(d), row(wsb), row(wsw), _full((8, d))],
        out_shape=[jax.ShapeDtypeStruct((s, d), F32), bf(d), bf(d), bf(dff), bf(2 * dff), bf(d), bf(wsb), bf(wsw),
                   jax.ShapeDtypeStruct((8, d), F32)],
        scratch_shapes=[pltpu.VMEM(w_out.shape, BF16), pltpu.VMEM(w_gu_t.shape, BF16), pltpu.VMEM(w_down.shape, BF16)],
        compiler_params=_params(60),
    )(sb_out, sw_out, x, g_in, b_in, sb_g, sw_g, w_out, g1, b1, w_gu_t, w_down, g2, b2, target)


def _rms_bwd(dm, o, g):
    n, r, _ = _rms_fwd(o, g)
    dn = dm * g
    return r * (dn - n * jnp.mean(dn * n, axis=-1, keepdims=True)), _colsum(dm * n)


def _sb_bwd(proj, dout, out, first, n_pairs, comm=None):
    s = proj.shape[0]
    t, per_step, n_steps = _sb_grid(s)
    width = n_pairs * LANES

    def body(q_ref, k_ref, v_ref, do_ref, o_ref, ab_ref, beta_ref, cl_ref, dq_ref, dk_out, dv_out, dk_ref, dv_ref):
        step = pl.program_id(1)

        @pl.when(step == 0)
        def _():
            dk_ref[...] = jnp.zeros_like(dk_ref)
            dv_ref[...] = jnp.zeros_like(dv_ref)

        lane = lax.broadcasted_iota(jnp.int32, (1, LANES), 1)
        first_lanes = lane < HEAD_DIM
        upper, incl = _sb_triangles(t)

        def query_tile(j, _):
            rows = pl.ds(pl.multiple_of(j * t, t), t)
            do2 = do_ref[rows, :]
            qs = _sb_stack_heads(q_ref[rows, :] * SCALE, first_lanes)
            dos = _sb_stack_heads(do2, first_lanes)
            prod = do2.astype(F32) * o_ref[rows, :]
            totals = jnp.concatenate([_rowsum(jnp.where(first_lanes, prod, 0.0)),
                                      _rowsum(jnp.where(first_lanes, 0.0, prod))], axis=0)

            def grads(offs, k_t, v_t, a_b, beta, c_e, dq):
                d_e = _dot_nt(dos, v_t) * a_b.astype(F32)
                d_hi = d_e.astype(BF16)
                d_lo = (d_e - d_hi.astype(F32)).astype(BF16)
                cols = [slice(c * t, (c + 1) * t) for c in range(len(offs))]
                suf_e, c_e = _sb_suffix([[d_hi[:, c], d_lo[:, c]] for c in cols], [_rowsum(d_e[:, c]) for c in cols], incl, c_e)
                dzb = (d_e - beta * (d_e + (totals - suf_e))).astype(BF16)
                dk_t = _dot_tn(dzb, qs)
                dv_t = _dot_tn(a_b, dos)
                for off, c in zip(offs, cols):
                    dk_ref[pl.ds(off, t), :] += dk_t[c, :]
                    dv_ref[pl.ds(off, t), :] += dv_t[c, :]
                return c_e, dq + _dot(dzb, k_t)

            def first_visit(offs, carry):
                _, c_e, dq = carry
                k_t = _sb_key_tiles(k_ref, offs, t)
                v_t = _sb_key_tiles(v_ref, offs, t)
                c_e, dq = grads(offs, k_t, v_t, ab_ref[j], beta_ref[j], c_e, dq)
                return cl_ref[j], c_e, dq

            def visit(offs, carry):
                c_l, c_e, dq = carry
                k_t = _sb_key_tiles(k_ref, offs, t)
                v_t = _sb_key_tiles(v_ref, offs, t)
                lb, a, c_l = _sb_scores(qs, k_t, upper, c_l, None, t)
                c_e, dq = grads(offs, k_t, v_t, a.astype(BF16), jnp.exp(lb), c_e, dq)
                return c_l, c_e, dq

            init = (jnp.zeros((2 * t, 1), F32), jnp.zeros((2 * t, 1), F32), jnp.zeros((2 * t, LANES), F32))
            _, _, dq = _sb_walk(step * per_step + j, t, first_visit, visit, init)
            dq_ref[rows, :] = (jnp.where(first_lanes, dq[:t], dq[t:]) * SCALE).astype(BF16)
            return 0

        lax.fori_loop(0, per_step, query_tile, 0)

        @pl.when(step == n_steps - 1)
        def _():
            dk_out[...] = dk_ref[...].astype(BF16)
            dv_out[...] = dv_ref[...].astype(BF16)

    qblk = pl.BlockSpec((t * per_step, LANES), lambda h, i: (i, h))
    whole = pl.BlockSpec((s, LANES), lambda h, i: (0, h))
    first_specs, _ = _sb_first_specs(n_pairs, n_steps, per_step, t)
    return _call(
        body, "sb_bwd", (n_pairs, n_steps),
        in_specs=[qblk,
                  pl.BlockSpec((s, LANES), lambda h, i: (0, n_pairs + h)),
                  pl.BlockSpec((s, LANES), lambda h, i: (0, 2 * n_pairs + h)),
                  qblk, qblk] + first_specs,
        out_specs=[qblk, whole, whole],
        out_shape=[jax.ShapeDtypeStruct((s, width), BF16)] * 3,
        args=(proj, proj, proj, dout, out, *first),
        scratch_shapes=[pltpu.VMEM((s, LANES), F32), pltpu.VMEM((s, LANES), F32)], comm=comm)


def _swa_bwd(proj, dout, kept, bucket, n_heads, qcol, kcol, vcol, comm=None):
    s = proj.shape[0]
    width = n_heads * HEAD_DIM
    n_groups = LANES // HEAD_DIM
    per_group = n_heads // n_groups
    nb = s // SWA_BLOCK

    per_step, n_steps = _swa_steps(s)

    def body(q_ref, kp_ref, kc_ref, vp_ref, vc_ref, bucket_ref, do_ref, prob_ref, psink_ref,
             dq_ref, dk_out, dv_out, dsk_ref, drb_ref, dbias_ref, dk_ref, dv_ref, kall_ref, vall_ref):
        step = pl.program_id(0)

        @pl.when(step == 0)
        def _():
            dbias_ref[...] = jnp.zeros_like(dbias_ref)
            dk_ref[...] = jnp.zeros_like(dk_ref)
            dv_ref[...] = jnp.zeros_like(dv_ref)
            dsk_ref[...] = jnp.zeros_like(dsk_ref)

        _swa_stage_keys(kp_ref, kc_ref, kall_ref)
        _swa_stage_keys(vp_ref, vc_ref, vall_ref)
        lane = lax.broadcasted_iota(jnp.int32, (1, LANES), 1)
        first = lane < HEAD_DIM

        def query_block(j, _):
            i = step * per_step + j
            rows = pl.ds(pl.multiple_of(j * SWA_BLOCK, SWA_BLOCK), SWA_BLOCK)
            band = pl.ds(pl.multiple_of(j * SWA_BLOCK, SWA_BLOCK), 2 * SWA_BLOCK)
            q_blk, do_blk = q_ref[rows, :], do_ref[rows, :]
            kcat, vcat = kall_ref[band, :], vall_ref[band, :]
            dkcat = jnp.zeros((2 * SWA_BLOCK, LANES), F32)
            dvcat = jnp.zeros((2 * SWA_BLOCK, LANES), F32)
            pieces = {}
            for g in range(n_groups):
                sel = first if g == 0 else jnp.logical_not(first)
                q_g = _swa_stack(q_blk, g, per_group, sel, SCALE)
                do_g = _swa_stack(do_blk, g, per_group, sel)
                prob, p_sink = prob_ref[j, g], psink_ref[j, g]
                dprob = _dot_nt(do_g, vcat)
                delta = _rowsum(prob * dprob)
                dlog = prob * (dprob - delta)
                sink_term = p_sink * delta
                for hh in range(per_group):
                    h = g * per_group + hh
                    tot = _colsum(sink_term[hh * SWA_BLOCK:(hh + 1) * SWA_BLOCK, :])
                    dsk_ref[h:h + 1, :] += jnp.broadcast_to(-tot, (1, LANES))
                dbias_ref[g] += dlog
                dlb = dlog.astype(BF16)
                _swa_unstack(_dot(dlb, kcat) * SCALE, g, per_group, pieces)
                dkcat += _dot_tn(dlb, q_g)
                dvcat += _dot_tn(prob.astype(BF16), do_g)
            for c in range(n_heads // 2):
                dq_ref[rows, c * LANES:(c + 1) * LANES] = jnp.where(first, pieces[2 * c], pieces[2 * c + 1]).astype(BF16)

            cur = pl.multiple_of(i * SWA_BLOCK, SWA_BLOCK)
            dk_ref[pl.ds(cur, SWA_BLOCK), :] += dkcat[SWA_BLOCK:, :]
            dv_ref[pl.ds(cur, SWA_BLOCK), :] += dvcat[SWA_BLOCK:, :]

            @pl.when(i > 0)
            def _():
                prv = pl.multiple_of((i - 1) * SWA_BLOCK, SWA_BLOCK)
                dk_ref[pl.ds(prv, SWA_BLOCK), :] += dkcat[:SWA_BLOCK, :]
                dv_ref[pl.ds(prv, SWA_BLOCK), :] += dvcat[:SWA_BLOCK, :]

            return 0

        lax.fori_loop(0, per_step, query_block, 0)

        @pl.when(step == n_steps - 1)
        def _():
            bk = bucket_ref[...]
            rowi = lax.broadcasted_iota(jnp.int32, (REL_BUCKETS, LANES), 0)
            coli = lax.broadcasted_iota(jnp.int32, (REL_BUCKETS, LANES), 1)
            res = jnp.zeros((REL_BUCKETS, LANES), F32)
            for h in range(n_heads):
                g, hh = divmod(h, per_group)
                db = dbias_ref[g, hh * SWA_BLOCK:(hh + 1) * SWA_BLOCK, :]
                for b in range(REL_BUCKETS):
                    tot = _colsum(_rowsum(jnp.where(bk == b, db, 0.0)))
                    res = jnp.where((rowi == b) & (coli == h), tot, res)
            drb_ref[...] = res
            dk_out[...] = dk_ref[...].astype(BF16)
            dv_out[...] = dv_ref[...].astype(BF16)

    kept_specs, _ = _swa_kept_specs(nb, n_groups, per_group, per_step)
    rows_spec = pl.BlockSpec((per_step * SWA_BLOCK, width), lambda i: (i, 0))
    in_specs = _swa_specs(n_heads, qcol, kcol, vcol, per_step)[:6] + [rows_spec] + kept_specs
    staged = pltpu.VMEM(((per_step + 1) * SWA_BLOCK, LANES), BF16)
    return _call(
        body, "swa_bwd", (n_steps,),
        in_specs=in_specs,
        out_specs=[rows_spec, _full((s, LANES)), _full((s, LANES)), _full((8, LANES)), _full((REL_BUCKETS, LANES))],
        out_shape=[jax.ShapeDtypeStruct((s, width), BF16), jax.ShapeDtypeStruct((s, LANES), BF16),
                   jax.ShapeDtypeStruct((s, LANES), BF16), jax.ShapeDtypeStruct((8, LANES), F32),
                   jax.ShapeDtypeStruct((REL_BUCKETS, LANES), F32)],
        args=(proj, proj, proj, proj, proj, bucket, dout, *kept),
        scratch_shapes=[pltpu.VMEM((n_groups, per_group * SWA_BLOCK, 2 * SWA_BLOCK), F32),
                        pltpu.VMEM((s, LANES), F32), pltpu.VMEM((s, LANES), F32), staged, staged],
        comm=comm)


def _proj_bwd(pieces, w_in_t, du1, x, g_in, comm=None):
    s, d = x.shape
    cols = w_in_t.shape[0]
    tm = min(ROW_TILE, s)
    n_p = len(pieces)

    def body(*refs):
        p_refs = refs[:n_p]
        w_ref, du_ref, x_ref, g_ref, dx_ref, st_ref = refs[n_p:]
        i = pl.program_id(0)

        @pl.when(i == 0)
        def _():
            st_ref[...] = jnp.zeros_like(st_ref)

        dproj = jnp.concatenate([p[...] for p in p_refs], axis=1)
        dh0 = _dot(dproj, w_ref[...]) + ALPHA * du_ref[...]
        xhat, r = _ln_hat(x_ref[...])
        st_ref[0:1, :] += _colsum(dh0 * xhat)
        st_ref[1:2, :] += _colsum(dh0)
        dx_ref[...] = _ln_bwd(dh0 * g_ref[...], xhat, r)

    row = lambda width: pl.BlockSpec((tm, width), lambda i: (i, 0))
    return _call(
        body, "proj_bwd", (s // tm,),
        in_specs=[row(p.shape[1]) for p in pieces] + [_full((cols, d)), row(d), row(d), _full((1, d))],
        out_specs=[row(d), _full((8, d))],
        out_shape=[jax.ShapeDtypeStruct((s, d), F32), jax.ShapeDtypeStruct((8, d), F32)],
        args=(*pieces, w_in_t, du1, x, g_in), comm=comm)


def _wgrad(name, pieces, b, tm, tn):
    s, n = b.shape
    m = sum(p.shape[1] for p in pieces)
    n_p = len(pieces)
    assert n_p == 1 or tm == m
    ts = min(WGRAD_TOKENS if b.dtype == BF16 and n_p == 1 else WGRAD_TOKENS // 2, s)
    n_k = s // ts

    def body(*refs):
        p_refs, b_ref, o_ref, acc_ref = refs[:n_p], refs[n_p], refs[n_p + 1], refs[n_p + 2]
        k = pl.program_id(2)

        @pl.when(k == 0)
        def _():
            acc_ref[...] = jnp.zeros_like(acc_ref)

        a = p_refs[0][...] if n_p == 1 else jnp.concatenate([p[...] for p in p_refs], axis=1)
        acc_ref[...] += _dot_tn(a, b_ref[...].astype(BF16))

        @pl.when(k == n_k - 1)
        def _():
            o_ref[...] = acc_ref[...].astype(BF16)

    piece_spec = lambda p: pl.BlockSpec((ts, tm if n_p == 1 else p.shape[1]), lambda i, j, k: (k, i))
    return pl.pallas_call(
        body, name=name, grid=(m // tm, n // tn, n_k),
        in_specs=[piece_spec(p) for p in pieces] + [pl.BlockSpec((ts, tn), lambda i, j, k: (k, j))],
        out_specs=pl.BlockSpec((tm, tn), lambda i, j, k: (i, j)),
        out_shape=jax.ShapeDtypeStruct((m, n), BF16),
        scratch_shapes=[pltpu.VMEM((tm, tn), F32)],
        compiler_params=_params(),
    )(*pieces, b)


def _adamw_math(w, g, m, v):
    m = ADAM_B1 * m + (1.0 - ADAM_B1) * g
    v = ADAM_B2 * v + (1.0 - ADAM_B2) * (g * g)
    m_hat = m / (1.0 - ADAM_B1 ** ADAM_STEP)
    v_hat = v / (1.0 - ADAM_B2 ** ADAM_STEP)
    delta = -ADAM_LR * (m_hat / (jnp.sqrt(v_hat) + ADAM_EPS) + ADAM_WD * w)
    return delta, m, v


def _adamw_rows(rows):
    return max(r for r in range(16, 257, 16) if rows % r == 0)


def _adamw(name, landed, w, m, v, tr):
    rows, cols = w.shape

    def body(l_ref, w_ref, m_ref, v_ref, g_ref, d_ref, nm_ref, nv_ref):
        g = l_ref[0].astype(F32)
        for src in range(1, N_DEV):
            g = g + l_ref[src].astype(F32)
        delta, nm, nv = _adamw_math(w_ref[...], g, m_ref[...], v_ref[...])
        g_ref[...] = g
        d_ref[...] = delta
        nm_ref[...] = nm
        nv_ref[...] = nv

    blk = pl.BlockSpec((tr, cols), lambda i: (i, 0))
    shape = jax.ShapeDtypeStruct((rows, cols), F32)
    return pl.pallas_call(
        body, name=name, grid=(rows // tr,),
        in_specs=[pl.BlockSpec((N_DEV, tr, cols), lambda i: (0, i, 0)), blk, blk, blk],
        out_specs=[blk, blk, blk, blk],
        out_shape=[shape, shape, shape, shape],
        compiler_params=_params(),
    )(landed, w, m, v)


def _pack(d, ln_in_g, ln_in_b, ln1_g, ln1_b, ln2_g, ln2_b, sb_g, sw_g, rel_bias, sinks, extra=None):
    tail = [rel_bias.reshape(-1), sinks.reshape(-1)]
    if extra is not None:
        tail.append(extra.reshape(-1))
    tail = jnp.concatenate(tail)
    tail = jnp.concatenate([tail, jnp.zeros((d - tail.shape[0],), F32)])
    rows = [ln_in_g.reshape(-1), ln_in_b.reshape(-1), ln1_g.reshape(-1), ln1_b.reshape(-1),
            ln2_g.reshape(-1), ln2_b.reshape(-1),
            jnp.concatenate([sb_g.reshape(-1), sw_g.reshape(-1)]), tail]
    return jnp.stack(rows)


def _unpack(p, wsb, n_rb, n_sk):
    return [p[0], p[1], p[6, :wsb][None], p[6, wsb:][None], p[7, n_rb:n_rb + n_sk][None],
            p[7, :n_rb].reshape(REL_BUCKETS, -1), p[2][None], p[3][None], p[4][None], p[5][None]]


def kernel(x, ln_in_g, ln_in_b, w_in, sb_norm_g, swa_norm_g, sinks, rel_bias, w_out, ln1_g, ln1_b, w_gate_up, w_down, ln2_g, ln2_b, loss_target, m_ln_in_g, m_ln_in_b, m_w_in, m_sb_norm_g, m_swa_norm_g, m_sinks, m_rel_bias, m_w_out, m_ln1_g, m_ln1_b, m_w_gate_up, m_w_down, m_ln2_g, m_ln2_b, v_ln_in_g, v_ln_in_b, v_w_in, v_sb_norm_g, v_swa_norm_g, v_sinks, v_rel_bias, v_w_out, v_ln1_g, v_ln1_b, v_w_gate_up, v_w_down, v_ln2_g, v_ln2_b):
    x2 = x[0]
    tgt = loss_target[0]
    s, d = x2.shape
    wsb = sb_norm_g.shape[-1]
    wsw = swa_norm_g.shape[-1]
    n_sw_heads = sinks.shape[-1]
    n_pairs = wsb // LANES
    dff = w_down.shape[1] * N_DEV
    assert wsb % LANES == 0 and wsw % LANES == 0 and n_sw_heads * HEAD_DIM == wsw
    assert 3 * wsb % wsw == 0 and dff % LANES == 0 and s % SWA_BLOCK == 0
    qcol = 3 * wsb // wsw
    kcol = (3 * wsb + wsw) // LANES
    vcol = kcol + 1
    assert w_in.shape[-1] * N_DEV == (vcol + 1) * LANES

    t2 = lambda a: jnp.transpose(a[0])
    big_w = [t2(w_in), w_out[0], t2(w_gate_up), w_down[0]]
    big_m = [t2(m_w_in), m_w_out[0], t2(m_w_gate_up), m_w_down[0]]
    big_v = [t2(v_w_in), v_w_out[0], t2(v_w_gate_up), v_w_down[0]]

    cat_rows = lambda g: g.reshape(N_DEV * g.shape[1], g.shape[2])
    shards = [w.astype(BF16) for w in big_w]
    w_in_t = cat_rows(_allgather_via_sibling("w_in_allgather", shards[0]))

    vec = lambda a: a.reshape(1, -1)
    g_in, b_in = vec(ln_in_g), vec(ln_in_b)
    bucket = jnp.asarray(_swa_bucket_table())

    h0b, proj = _ln_proj(x2, g_in, b_in, w_in_t)
    sb_out, sb_first, gathered = _sb_fwd(proj, n_pairs, comm=(shards[1:3], ["gather"] * 2))
    w_out_f, w_gu_t = cat_rows(gathered[0]), cat_rows(gathered[1])
    sw_out, sw_kept, gathered = _swa_fwd(proj, bucket, rel_bias, sinks, n_sw_heads, qcol, kcol, vcol,
                                comm=(shards[3:], ["gather"]))
    w_down_f = cat_rows(gathered[0])
    du1, merged, h1b, act, dgu, du2b, dsb, dsw, st_ffn = _mix_ffn(
        sb_out, sw_out, x2, g_in, b_in, sb_norm_g, swa_norm_g, w_out_f, ln1_g, ln1_b, w_gu_t, w_down_f, ln2_g, ln2_b, tgt)

    split_rows = lambda g: g.reshape(N_DEV, g.shape[0] // N_DEV, g.shape[1])
    gw_gu = _wgrad("wgrad_gate_up", [dgu], h1b, dff // 2, d)
    gw_down = _wgrad("wgrad_down", [act], du2b, dff // 2, d)
    gw_out = _wgrad("wgrad_out", [merged], du1, min(512, d), d)
    (dq_sb, dk_sb, dv_sb), (land_gu, land_out) = _sb_bwd(
        proj, dsb, sb_out, sb_first, n_pairs, comm=([split_rows(gw_gu), split_rows(gw_out)], ["scatter"] * 2))
    (dq_sw, dk_sw, dv_sw, st_sink, st_rb), (land_down,) = _swa_bwd(
        proj, dsw, sw_kept, bucket, n_sw_heads, qcol, kcol, vcol, comm=([split_rows(gw_down)], ["scatter"]))
    pieces = [dq_sb, dk_sb, dv_sb, dq_sw, dk_sw, dv_sw]
    gw_in = _wgrad("wgrad_in", pieces, h0b, proj.shape[1], d)
    (grad_x, st_in), (land_in,) = _proj_bwd(pieces, w_in_t, du1, x2, g_in, comm=([split_rows(gw_in)], ["scatter"]))

    n_rb = rel_bias.size
    small = _pack(d, st_in[0], st_in[1], st_ffn[3], st_ffn[4], st_ffn[0], st_ffn[1],
                  st_ffn[5, :wsb], st_ffn[5, wsb:], st_rb[:, :n_sw_heads], st_sink[:n_sw_heads, 0],
                  extra=st_ffn[2, 0:1])
    land_small = _exchange("small_grads_allgather", [small], ["gather"])[0]
    landed = [land_in, land_out, land_gu, land_down, land_small]

    big = []
    for name, land, w, m, v in zip(["adamw_in", "adamw_out", "adamw_gate_up", "adamw_down"], landed[:4], big_w, big_m, big_v):
        big.append(_adamw(name, land, w, m, v, _adamw_rows(w.shape[0])))

    small_w = _pack(d, ln_in_g, ln_in_b, ln1_g, ln1_b, ln2_g, ln2_b, sb_norm_g, swa_norm_g, rel_bias, sinks)
    small_m = _pack(d, m_ln_in_g, m_ln_in_b, m_ln1_g, m_ln1_b, m_ln2_g, m_ln2_b, m_sb_norm_g, m_swa_norm_g, m_rel_bias, m_sinks)
    small_v = _pack(d, v_ln_in_g, v_ln_in_b, v_ln1_g, v_ln1_b, v_ln2_g, v_ln2_b, v_sb_norm_g, v_swa_norm_g, v_rel_bias, v_sinks)
    sg, sd, sm, sv = _adamw("adamw_small", landed[4], small_w, small_m, small_v, 8)
    n_sk = sinks.size
    loss = sg[7, n_rb + n_sk]

    def leaves(idx):
        sm_l = _unpack([sg, sd, sm, sv][idx], wsb, n_rb, n_sk)
        bg = [jnp.transpose(big[0][idx])[None], big[1][idx][None], jnp.transpose(big[2][idx])[None], big[3][idx][None]]
        return [sm_l[0], sm_l[1], bg[0], sm_l[2], sm_l[3], sm_l[4], sm_l[5], bg[1], sm_l[6], sm_l[7], bg[2], bg[3], sm_l[8], sm_l[9]]

    return (loss, grad_x[None], *leaves(0), *leaves(1), *leaves(2), *leaves(3))
```

```python
import functools
import math

import numpy as np
import jax
import jax.numpy as jnp
from jax import lax
from jax.experimental import pallas as pl
from jax.experimental.pallas import tpu as pltpu

F32 = jnp.float32
BF16 = jnp.bfloat16
MESH = pl.DeviceIdType.MESH

N_DEV = 8
LANES = 128
HEAD_DIM = 64
SCALE = HEAD_DIM ** -0.5
SWA_BLOCK = 128
SWA_BLOCKS_PER_STEP = 4
REL_BUCKETS = 32
REL_MAX_DIST = 128
ALPHA = 2.0 ** 0.25
LN_EPS = 1e-5
RMS_EPS = 1e-6
ADAM_LR = 0.001
ADAM_B1 = 0.9
ADAM_B2 = 0.999
ADAM_EPS = 1e-08
ADAM_WD = 0.01
ADAM_STEP = 10

ROW_TILE = 512
SB_TILE = 256
SB_TILES_PER_STEP = 4
FFN_TILE = 256
WGRAD_TOKENS = 2048
SB_UNDERFLOW = -110.0
SB_MASKED = -1e30
MIB = 1024 * 1024


def _params(vmem_mib=48):
    return pltpu.CompilerParams(vmem_limit_bytes=vmem_mib * MIB)


def _dot(a, b):
    return jnp.dot(a, b, preferred_element_type=F32)


def _dot_nt(a, b):
    return lax.dot_general(a, b, (((1,), (1,)), ((), ())), preferred_element_type=F32)


def _dot_tn(a, b):
    return lax.dot_general(a, b, (((0,), (0,)), ((), ())), preferred_element_type=F32)


def _ln_hat(x):
    mu = jnp.mean(x, axis=-1, keepdims=True)
    xc = x - mu
    var = jnp.mean(xc * xc, axis=-1, keepdims=True)
    r = lax.rsqrt(var + LN_EPS)
    return xc * r, r


def _ln_bwd(dxhat, xhat, r):
    return r * (dxhat - jnp.mean(dxhat, axis=-1, keepdims=True)
                - xhat * jnp.mean(dxhat * xhat, axis=-1, keepdims=True))


def _colsum(a):
    return jnp.sum(a, axis=0, keepdims=True)


def _rowsum(a):
    return jnp.sum(a, axis=1, keepdims=True)


def _full(shape):
    return pl.BlockSpec(shape, lambda *_: (0,) * len(shape))


def _comm_out_shapes(arrays, kinds):
    shapes = []
    for a, kind in zip(arrays, kinds):
        blk = a.shape if kind == "gather" else a.shape[1:]
        shapes.append(jax.ShapeDtypeStruct((N_DEV,) + tuple(blk), a.dtype))
    return shapes


def _comm_sems(n):
    return [pltpu.SemaphoreType.DMA((n, N_DEV - 1)), pltpu.SemaphoreType.DMA((n, N_DEV - 1)),
            pltpu.SemaphoreType.DMA((n,))]


def _comm_copies(ins, outs, kinds, send_sems, recv_sems, local_sems):
    x, y, c = lax.axis_index("x"), lax.axis_index("y"), lax.axis_index("c")
    me = 4 * x + 2 * y + c

    def src_for(t, dev_lin):
        return ins[t] if kinds[t] == "gather" else ins[t].at[dev_lin]

    local = [pltpu.make_async_copy(src_for(t, me), outs[t].at[me], local_sems.at[t]) for t in range(len(kinds))]
    sends, arrivals = [], []
    for k in range(1, N_DEV):
        px = 1 - x if (k >> 2) & 1 else x
        py = 1 - y if (k >> 1) & 1 else y
        pc = 1 - c if k & 1 else c
        peer_lin = 4 * px + 2 * py + pc
        for t in range(len(kinds)):
            sems = dict(send_sem=send_sems.at[t, k - 1], recv_sem=recv_sems.at[t, k - 1],
                        device_id=(px, py, pc), device_id_type=MESH)
            sends.append(pltpu.make_async_remote_copy(src_ref=src_for(t, peer_lin), dst_ref=outs[t].at[me], **sems))
            arrivals.append(pltpu.make_async_remote_copy(src_ref=src_for(t, peer_lin), dst_ref=outs[t].at[peer_lin], **sems))
    return local, sends, arrivals


def _comm_start(ins, outs, kinds, sems):
    local, sends, _ = _comm_copies(ins, outs, kinds, *sems)
    for cp in local + sends:
        cp.start()


def _comm_finish(ins, outs, kinds, sems):
    local, sends, arrivals = _comm_copies(ins, outs, kinds, *sems)
    for cp in arrivals:
        cp.wait_recv()
    for cp in sends:
        cp.wait_send()
    for cp in local:
        cp.wait()


def _exchange(name, arrays, kinds):
    n = len(arrays)

    def body(*refs):
        ins, outs, sems = refs[:n], refs[n:2 * n], refs[2 * n:]
        _comm_start(ins, outs, kinds, sems)
        _comm_finish(ins, outs, kinds, sems)

    any_spec = pl.BlockSpec(memory_space=pl.ANY)
    return pl.pallas_call(
        body, name=name, out_shape=_comm_out_shapes(arrays, kinds),
        in_specs=[any_spec] * n, out_specs=[any_spec] * n,
        scratch_shapes=_comm_sems(n),
    )(*arrays)


def _allgather_via_sibling(name, shard):
    def body(x_ref, out_ref, send_sems, recv_sems, local_sem):
        x, y, c = lax.axis_index("x"), lax.axis_index("y"), lax.axis_index("c")
        me, sibling = (x, y, c), (x, y, 1 - c)
        chips = [(1 - x, y), (x, 1 - y), (1 - x, 1 - y)]

        def copy(k, block, to, src=None):
            slot = out_ref.at[4 * block[0] + 2 * block[1] + block[2]]
            return pltpu.make_async_remote_copy(
                src_ref=slot if src is None else src, dst_ref=slot,
                send_sem=send_sems.at[k], recv_sem=recv_sems.at[k], device_id=to, device_id_type=MESH)

        mine = pltpu.make_async_copy(x_ref, out_ref.at[4 * x + 2 * y + c], local_sem)
        mine.start()
        first = [copy(0, me, sibling, src=x_ref)]
        first += [copy(1 + j, me, (*chip, c), src=x_ref) for j, chip in enumerate(chips)]
        for cp in first:
            cp.start()
        passed = [copy(4 + j, (*chip, c), sibling) for j, chip in enumerate(chips)]
        for j, chip in enumerate(chips):
            copy(1 + j, (*chip, c), me).wait_recv()
            passed[j].start()
        copy(0, sibling, me).wait_recv()
        for j, chip in enumerate(chips):
            copy(4 + j, (*chip, 1 - c), me).wait_recv()
        for cp in first + passed:
            cp.wait_send()
        mine.wait()

    any_spec = pl.BlockSpec(memory_space=pl.ANY)
    return pl.pallas_call(
        body, name=name, out_shape=jax.ShapeDtypeStruct((N_DEV,) + shard.shape, shard.dtype),
        in_specs=[any_spec], out_specs=any_spec,
        scratch_shapes=[pltpu.SemaphoreType.DMA((N_DEV - 1,)), pltpu.SemaphoreType.DMA((N_DEV - 1,)),
                        pltpu.SemaphoreType.DMA],
    )(shard)


def _call(body, name, grid, in_specs, out_specs, out_shape, args, scratch_shapes=(), comm=None):
    if comm is None:
        outs = pl.pallas_call(body, name=name, grid=grid, in_specs=in_specs, out_specs=out_specs,
                              out_shape=out_shape, scratch_shapes=list(scratch_shapes),
                              compiler_params=_params())(*args)
        return outs, []
    arrays, kinds = comm
    n, n_in, n_out, n_scr = len(arrays), len(in_specs), len(out_specs), len(scratch_shapes)

    def fused(*refs):
        c_in, x_in = refs[:n_in], refs[n_in:n_in + n]
        c_out = refs[n_in + n:n_in + n + n_out]
        x_out = refs[n_in + n + n_out:n_in + 2 * n + n_out]
        rest = refs[n_in + 2 * n + n_out:]
        c_scr, sems = rest[:n_scr], rest[n_scr:]
        ids = [pl.program_id(a) for a in range(len(grid))]
        is_first = functools.reduce(jnp.logical_and, [i == 0 for i in ids])
        is_last = functools.reduce(jnp.logical_and, [i == g - 1 for i, g in zip(ids, grid)])

        @pl.when(is_first)
        def _():
            _comm_start(x_in, x_out, kinds, sems)

        body(*c_in, *c_out, *c_scr)

        @pl.when(is_last)
        def _():
            _comm_finish(x_in, x_out, kinds, sems)

    any_spec = pl.BlockSpec(memory_space=pl.ANY)
    outs = pl.pallas_call(
        fused, name=name, grid=grid,
        in_specs=list(in_specs) + [any_spec] * n, out_specs=list(out_specs) + [any_spec] * n,
        out_shape=list(out_shape) + _comm_out_shapes(arrays, kinds),
        scratch_shapes=list(scratch_shapes) + _comm_sems(n),
        compiler_params=_params())(*args, *arrays)
    return outs[:n_out], outs[n_out:]


def _ln_proj(x, g, b, w_in_t):
    s, d = x.shape
    cols = w_in_t.shape[0]
    tm = min(ROW_TILE, s)

    def body(x_ref, g_ref, b_ref, w_ref, h_ref, p_ref):
        xhat, _ = _ln_hat(x_ref[...])
        h = (xhat * g_ref[...] + b_ref[...]).astype(BF16)
        h_ref[...] = h
        p_ref[...] = _dot_nt(h, w_ref[...]).astype(BF16)

    row = lambda width: pl.BlockSpec((tm, width), lambda i: (i, 0))
    return pl.pallas_call(
        body, name="ln_proj", grid=(s // tm,),
        in_specs=[row(d), _full((1, d)), _full((1, d)), _full((cols, d))],
        out_specs=[row(d), row(cols)],
        out_shape=[jax.ShapeDtypeStruct((s, d), BF16), jax.ShapeDtypeStruct((s, cols), BF16)],
        compiler_params=_params(),
    )(x, g, b, w_in_t)


def _sb_triangles(t):
    row = lax.broadcasted_iota(jnp.int32, (t, t), 0)
    col = lax.broadcasted_iota(jnp.int32, (t, t), 1)
    return (row > col).astype(BF16), (row >= col).astype(BF16)


def _sb_first_mask(z, t, has_prev):
    qrow = lax.broadcasted_iota(jnp.int32, (2 * t, t), 0) & (t - 1)
    col = lax.broadcasted_iota(jnp.int32, (2 * t, t), 1)
    return jnp.concatenate([jnp.where(has_prev, z[:, :t], SB_MASKED), jnp.where(col < qrow, z[:, t:], SB_MASKED)], axis=1)


def _sb_stack_heads(x2, first):
    zero = jnp.zeros_like(x2)
    return jnp.concatenate([jnp.where(first, x2, zero), jnp.where(first, zero, x2)], axis=0)


def _sb_key_tiles(ref, offs, t):
    tiles = [ref[pl.ds(off, t), :] for off in offs]
    return tiles[0] if len(tiles) == 1 else jnp.concatenate(tiles, axis=0)


def _sb_suffix(terms, row_sums, tri, carry):
    out = [None] * len(terms)
    for j in reversed(range(len(terms))):
        suf = carry
        for op in terms[j]:
            suf = suf + _dot(op, tri)
        out[j] = suf
        carry = carry + row_sums[j]
    return (out[0] if len(out) == 1 else jnp.concatenate(out, axis=1)), carry


def _sb_scores(qh, k_t, upper, carry_l, has_prev, t):
    z = _dot_nt(qh, k_t)
    if has_prev is not None:
        z = _sb_first_mask(z, t, has_prev)
    neg = jnp.minimum(z, 0.0)
    nz = neg - z
    sp = jnp.log(1.0 + jnp.exp(neg + nz))
    lb = neg - sp
    l1 = nz - sp
    hi = l1.astype(BF16)
    lo = (l1 - hi.astype(F32)).astype(BF16)
    cols = [slice(j * t, (j + 1) * t) for j in range(z.shape[1] // t)]
    suf, carry_l = _sb_suffix([[hi[:, c], lo[:, c]] for c in cols], [_rowsum(l1[:, c]) for c in cols], upper, carry_l)
    return lb, jnp.exp(lb + suf), carry_l


def _sb_walk(i, t, first_visit, visit, init):
    def alive(carry):
        return jnp.max(carry[0]) > SB_UNDERFLOW

    prev = pl.multiple_of(jnp.maximum(i - 1, 0) * t, t)
    carry = first_visit((prev, pl.multiple_of(i * t, t)), init)

    def cond(state):
        j, go, _ = state
        return (j < i - 1) & go

    def body(state):
        j, _, carry = state
        carry = visit((pl.multiple_of((i - 2 - j) * t, t),), carry)
        return j + 1, alive(carry), carry

    return lax.while_loop(cond, body, (jnp.int32(0), alive(carry), carry))[2]


def _sb_first_specs(n_pairs, n_steps, per_step, t):
    at = lambda h, i: (h * n_steps + i, 0, 0)
    n_tiles = n_pairs * n_steps * per_step
    specs = [pl.BlockSpec((per_step, 2 * t, 2 * t), at), pl.BlockSpec((per_step, 2 * t, 2 * t), at),
             pl.BlockSpec((per_step, 2 * t, 1), at)]
    shapes = [jax.ShapeDtypeStruct((n_tiles, 2 * t, 2 * t), BF16), jax.ShapeDtypeStruct((n_tiles, 2 * t, 2 * t), F32),
              jax.ShapeDtypeStruct((n_tiles, 2 * t, 1), F32)]
    return specs, shapes


def _sb_grid(s):
    t = min(SB_TILE, s)
    per_step = min(SB_TILES_PER_STEP, s // t)
    return t, per_step, s // (t * per_step)


def _sb_fwd(proj, n_pairs, comm=None):
    s = proj.shape[0]
    t, per_step, n_steps = _sb_grid(s)

    def body(q_ref, k_ref, v_ref, o_ref, ab_ref, beta_ref, cl_ref):
        lane = lax.broadcasted_iota(jnp.int32, (1, LANES), 1)
        first = lane < HEAD_DIM
        upper, _ = _sb_triangles(t)

        def query_tile(j, _):
            i = pl.program_id(1) * per_step + j
            rows = pl.ds(pl.multiple_of(j * t, t), t)
            qs = _sb_stack_heads(q_ref[rows, :] * SCALE, first)

            def first_visit(offs, carry):
                c_l, acc = carry
                lb, a, c_l = _sb_scores(qs, _sb_key_tiles(k_ref, offs, t), upper, c_l, i > 0, t)
                a_b = a.astype(BF16)
                ab_ref[j] = a_b
                beta_ref[j] = jnp.exp(lb)
                cl_ref[j] = c_l
                return c_l, acc + _dot(a_b, _sb_key_tiles(v_ref, offs, t))

            def visit(offs, carry):
                c_l, acc = carry
                _, a, c_l = _sb_scores(qs, _sb_key_tiles(k_ref, offs, t), upper, c_l, None, t)
                return c_l, acc + _dot(a.astype(BF16), _sb_key_tiles(v_ref, offs, t))

            init = (jnp.zeros((2 * t, 1), F32), jnp.zeros((2 * t, LANES), F32))
            _, acc = _sb_walk(i, t, first_visit, visit, init)
            o_ref[rows, :] = jnp.where(first, acc[:t], acc[t:])
            return 0

        lax.fori_loop(0, per_step, query_tile, 0)

    qblk = pl.BlockSpec((t * per_step, LANES), lambda h, i: (i, h))
    first_specs, first_shapes = _sb_first_specs(n_pairs, n_steps, per_step, t)
    outs, landed = _call(
        body, "sb_fwd", (n_pairs, n_steps),
        in_specs=[qblk,
                  pl.BlockSpec((s, LANES), lambda h, i: (0, n_pairs + h)),
                  pl.BlockSpec((s, LANES), lambda h, i: (0, 2 * n_pairs + h))],
        out_specs=[qblk] + first_specs,
        out_shape=[jax.ShapeDtypeStruct((s, n_pairs * LANES), F32)] + first_shapes,
        args=(proj, proj, proj), comm=comm)
    return outs[0], outs[1:], landed


def _swa_bucket_table():
    qi = np.arange(SWA_BLOCK)[:, None]
    cj = np.arange(2 * SWA_BLOCK)[None, :]
    dist = qi + SWA_BLOCK - cj
    exact = REL_BUCKETS // 2
    d = np.maximum(dist, 0)
    d_f = np.maximum(d, 1).astype(np.float32)
    large = exact + (np.log(d_f / np.float32(exact)) / np.float32(math.log(REL_MAX_DIST / exact))
                     * np.float32(REL_BUCKETS - exact)).astype(np.int32)
    large = np.minimum(large, REL_BUCKETS - 1)
    return np.where(d < exact, d, large).astype(np.int32)


def _swa_build_bias(bucket_ref, rb_ref, bias_ref, n_groups, per_group):
    bk = bucket_ref[...]
    dist = (lax.broadcasted_iota(jnp.int32, bk.shape, 0) + SWA_BLOCK) - lax.broadcasted_iota(jnp.int32, bk.shape, 1)
    window = (dist >= 0) & (dist < SWA_BLOCK)
    for g in range(n_groups):
        for hh in range(per_group):
            acc = jnp.zeros(bk.shape, F32)
            for b in range(REL_BUCKETS):
                acc = jnp.where(bk == b, rb_ref[b, g * per_group + hh], acc)
            bias_ref[g, hh * SWA_BLOCK:(hh + 1) * SWA_BLOCK, :] = jnp.where(window, acc, -jnp.inf)


def _swa_first_block_mask(i):
    col = lax.broadcasted_iota(jnp.int32, (1, 2 * SWA_BLOCK), 1)
    return jnp.where((col < SWA_BLOCK) & (i == 0), -jnp.inf, 0.0)


def _swa_place(blk, h, group, sel):
    if (h % 2) != group:
        blk = pltpu.roll(blk.astype(F32), HEAD_DIM, axis=1).astype(BF16)
    return jnp.where(sel, blk, jnp.zeros_like(blk))


def _swa_stack(ref, group, per_group, sel, scale=1.0):
    parts = []
    for hh in range(per_group):
        h = group * per_group + hh
        parts.append(_swa_place(ref[:, (h // 2) * LANES:(h // 2 + 1) * LANES], h, group, sel))
    stacked = jnp.concatenate(parts, axis=0)
    return stacked if scale == 1.0 else stacked * scale


def _swa_unstack(stacked, group, per_group, pieces):
    for hh in range(per_group):
        h = group * per_group + hh
        piece = stacked[hh * SWA_BLOCK:(hh + 1) * SWA_BLOCK, :]
        pieces[h] = pltpu.roll(piece, HEAD_DIM, axis=1) if (h % 2) != group else piece


def _swa_sink_rows(sk_ref, group, per_group):
    rowh = lax.broadcasted_iota(jnp.int32, (per_group * SWA_BLOCK, 1), 0) // SWA_BLOCK
    sink = jnp.zeros((per_group * SWA_BLOCK, 1), F32) + sk_ref[0, group * per_group]
    for hh in range(1, per_group):
        sink = jnp.where(rowh == hh, sk_ref[0, group * per_group + hh], sink)
    return sink


def _swa_probs(q_pos, kcat, bias_h, first_mask, sink):
    logits = _dot_nt(q_pos, kcat) + (bias_h + first_mask)
    m = jnp.maximum(jnp.max(logits, axis=1, keepdims=True), sink)
    p = jnp.exp(logits - m)
    es = jnp.exp(sink - m)
    inv = 1.0 / (_rowsum(p) + es)
    return p * inv, es * inv


def _swa_steps(s):
    per_step = min(SWA_BLOCKS_PER_STEP, s // SWA_BLOCK)
    return per_step, s // (SWA_BLOCK * per_step)


def _swa_specs(n_heads, qcol, kcol, vcol, per_step):
    width = n_heads * HEAD_DIM
    prev = lambda col: pl.BlockSpec((SWA_BLOCK, LANES), lambda i: (jnp.maximum(i * per_step - 1, 0), col))
    cur = lambda col: pl.BlockSpec((per_step * SWA_BLOCK, LANES), lambda i: (i, col))
    return [pl.BlockSpec((per_step * SWA_BLOCK, width), lambda i: (i, qcol)),
            prev(kcol), cur(kcol), prev(vcol), cur(vcol),
            _full((SWA_BLOCK, 2 * SWA_BLOCK)),
            pl.BlockSpec(memory_space=pltpu.SMEM), pl.BlockSpec(memory_space=pltpu.SMEM)]


def _swa_stage_keys(prev_ref, cur_ref, all_ref):
    all_ref[:SWA_BLOCK, :] = prev_ref[...]
    all_ref[SWA_BLOCK:, :] = cur_ref[...]


def _swa_kept_specs(nb, n_groups, per_group, per_step):
    rows = per_group * SWA_BLOCK
    at = lambda i: (i, 0, 0, 0)
    specs = [pl.BlockSpec((per_step, n_groups, rows, 2 * SWA_BLOCK), at), pl.BlockSpec((per_step, n_groups, rows, 1), at)]
    shapes = [jax.ShapeDtypeStruct((nb, n_groups, rows, 2 * SWA_BLOCK), F32), jax.ShapeDtypeStruct((nb, n_groups, rows, 1), F32)]
    return specs, shapes


def _swa_fwd(proj, bucket, rel_bias, sinks, n_heads, qcol, kcol, vcol, comm=None):
    s = proj.shape[0]
    width = n_heads * HEAD_DIM
    n_groups = LANES // HEAD_DIM
    per_group = n_heads // n_groups

    per_step, n_steps = _swa_steps(s)

    def body(q_ref, kp_ref, kc_ref, vp_ref, vc_ref, bucket_ref, rb_ref, sk_ref, o_ref, prob_ref, psink_ref,
             bias_ref, kall_ref, vall_ref):
        step = pl.program_id(0)

        @pl.when(step == 0)
        def _():
            _swa_build_bias(bucket_ref, rb_ref, bias_ref, n_groups, per_group)

        _swa_stage_keys(kp_ref, kc_ref, kall_ref)
        _swa_stage_keys(vp_ref, vc_ref, vall_ref)
        lane = lax.broadcasted_iota(jnp.int32, (1, LANES), 1)
        first = lane < HEAD_DIM

        def query_block(j, _):
            rows = pl.ds(pl.multiple_of(j * SWA_BLOCK, SWA_BLOCK), SWA_BLOCK)
            band = pl.ds(pl.multiple_of(j * SWA_BLOCK, SWA_BLOCK), 2 * SWA_BLOCK)
            first_mask = _swa_first_block_mask(step * per_step + j)
            q_blk = q_ref[rows, :]
            kcat, vcat = kall_ref[band, :], vall_ref[band, :]
            pieces = {}
            for g in range(n_groups):
                sel = first if g == 0 else jnp.logical_not(first)
                prob, p_sink = _swa_probs(_swa_stack(q_blk, g, per_group, sel, SCALE), kcat, bias_ref[g], first_mask,
                                          _swa_sink_rows(sk_ref, g, per_group))
                prob_ref[j, g] = prob
                psink_ref[j, g] = p_sink
                _swa_unstack(_dot(prob.astype(BF16), vcat), g, per_group, pieces)
            for c in range(n_heads // 2):
                o_ref[rows, c * LANES:(c + 1) * LANES] = jnp.where(first, pieces[2 * c], pieces[2 * c + 1])
            return 0

        lax.fori_loop(0, per_step, query_block, 0)

    kept_specs, kept_shapes = _swa_kept_specs(s // SWA_BLOCK, n_groups, per_group, per_step)
    staged = pltpu.VMEM(((per_step + 1) * SWA_BLOCK, LANES), BF16)
    outs, landed = _call(
        body, "swa_fwd", (n_steps,),
        in_specs=_swa_specs(n_heads, qcol, kcol, vcol, per_step),
        out_specs=[pl.BlockSpec((per_step * SWA_BLOCK, width), lambda i: (i, 0))] + kept_specs,
        out_shape=[jax.ShapeDtypeStruct((s, width), F32)] + kept_shapes,
        args=(proj, proj, proj, proj, proj, bucket, rel_bias, sinks),
        scratch_shapes=[pltpu.VMEM((n_groups, per_group * SWA_BLOCK, 2 * SWA_BLOCK), F32), staged, staged], comm=comm)
    return outs[0], outs[1:], landed


def _rms_fwd(o, g):
    r = lax.rsqrt(jnp.mean(o * o, axis=-1, keepdims=True) + RMS_EPS)
    n = o * r
    return n, r, n * g


def _mix_ffn(sb_out, sw_out, x, g_in, b_in, sb_g, sw_g, w_out, g1, b1, w_gu_t, w_down, g2, b2, target):
    s, d = x.shape
    wsb, wsw = sb_out.shape[1], sw_out.shape[1]
    dff = w_down.shape[0]
    assert wsb + wsw == d
    tm = min(FFN_TILE, s)

    def body(sb_ref, sw_ref, x_ref, gi_ref, bi_ref, sbg_ref, swg_ref, wo_hbm, g1_ref, b1_ref, wgu_hbm, wd_hbm,
             g2_ref, b2_ref, t_ref,
             du1_ref, mg_ref, h1b_ref, act_ref, dgu_ref, du2b_ref, dsb_ref, dsw_ref, st_ref,
             wo_ref, wgu_ref, wd_ref):
        @pl.when(pl.program_id(0) == 0)
        def _():
            pltpu.sync_copy(wo_hbm, wo_ref)
            pltpu.sync_copy(wgu_hbm, wgu_ref)
            pltpu.sync_copy(wd_hbm, wd_ref)
            st_ref[...] = jnp.zeros_like(st_ref)

        sb, sw = sb_ref[...], sw_ref[...]
        _, _, m_sb = _rms_fwd(sb, sbg_ref[...])
        _, _, m_sw = _rms_fwd(sw, swg_ref[...])
        m_sb = m_sb.astype(BF16)
        m_sw = m_sw.astype(BF16)
        mg_ref[:, :wsb] = m_sb
        mg_ref[:, wsb:] = m_sw
        xhat0, _ = _ln_hat(x_ref[...])
        u1 = ALPHA * (xhat0 * gi_ref[...] + bi_ref[...]) + _dot(m_sb, wo_ref[:wsb, :]) + _dot(m_sw, wo_ref[wsb:, :])

        xhat1, r1 = _ln_hat(u1)
        h1 = xhat1 * g1_ref[...] + b1_ref[...]
        h1b = h1.astype(BF16)
        h1b_ref[...] = h1b
        gate = _dot_nt(h1b, wgu_ref[:dff, :])
        up = _dot_nt(h1b, wgu_ref[dff:, :])
        sg = jax.nn.sigmoid(gate)
        silu = gate * sg
        act = (silu * up).astype(BF16)
        act_ref[...] = act
        u2 = ALPHA * h1 + _dot(act, wd_ref[...])
        xhat2, r2 = _ln_hat(u2)
        diff = xhat2 * g2_ref[...] + b2_ref[...] - t_ref[...]
        dh2 = diff * (1.0 / d)
        st_ref[0:1, :] += _colsum(dh2 * xhat2)
        st_ref[1:2, :] += _colsum(dh2)
        st_ref[2:3, :] += jnp.broadcast_to(_colsum(_rowsum(diff * diff)) * (0.5 / d), (1, d))
        du2 = _ln_bwd(dh2 * g2_ref[...], xhat2, r2)
        du2b = du2.astype(BF16)
        du2b_ref[...] = du2b
        dact = _dot_nt(du2b, wd_ref[...])
        dgate = (dact * up * (sg * (1.0 + gate * (1.0 - sg)))).astype(BF16)
        dup = (dact * silu).astype(BF16)
        dgu_ref[:, :dff] = dgate
        dgu_ref[:, dff:] = dup
        dh1 = _dot(dgate, wgu_ref[:dff, :]) + _dot(dup, wgu_ref[dff:, :]) + ALPHA * du2
        st_ref[3:4, :] += _colsum(dh1 * xhat1)
        st_ref[4:5, :] += _colsum(dh1)
        du1 = _ln_bwd(dh1 * g1_ref[...], xhat1, r1)
        du1_ref[...] = du1

        dmerged = _dot_nt(du1.astype(BF16), wo_ref[...])
        dsb, gsb = _rms_bwd(dmerged[:, :wsb], sb, sbg_ref[...])
        dsw, gsw = _rms_bwd(dmerged[:, wsb:], sw, swg_ref[...])
        dsb_ref[...] = dsb.astype(BF16)
        dsw_ref[...] = dsw.astype(BF16)
        st_ref[5:6, :wsb] += gsb
        st_ref[5:6, wsb:] += gsw

    row = lambda width: pl.BlockSpec((tm, width), lambda i: (i, 0))
    vec = lambda width: _full((1, width))
    hbm = pl.BlockSpec(memory_space=pl.ANY)
    bf = lambda width: jax.ShapeDtypeStruct((s, width), BF16)
    return pl.pallas_call(
        body, name="mix_ffn", grid=(s // tm,),
        in_specs=[row(wsb), row(wsw), row(d), vec(d), vec(d), vec(wsb), vec(wsw), hbm, vec(d), vec(d), hbm, hbm,
                  vec(d), vec(d), row(d)],
        out_specs=[row(d), row(d), row(d), row(dff), row(2 * dff), row(d), row(wsb), row(wsw), _full((8, d))],
        out_shape=[jax.ShapeDtypeStruct((s, d), F32), bf(d), bf(d), bf(dff), bf(2 * dff), bf(d), bf(wsb), bf(wsw),
                   jax.ShapeDtypeStruct((8, d), F32)],
        scratch_shapes=[pltpu.VMEM(w_out.shape, BF16), pltpu.VMEM(w_gu_t.shape, BF16), pltpu.VMEM(w_down.shape, BF16)],
        compiler_params=_params(60),
    )(sb_out, sw_out, x, g_in, b_in, sb_g, sw_g, w_out, g1, b1, w_gu_t, w_down, g2, b2, target)


def _rms_bwd(dm, o, g):
    n, r, _ = _rms_fwd(o, g)
    dn = dm * g
    return r * (dn - n * jnp.mean(dn * n, axis=-1, keepdims=True)), _colsum(dm * n)


def _sb_bwd(proj, dout, out, first, n_pairs, comm=None):
    s = proj.shape[0]
    t, per_step, n_steps = _sb_grid(s)
    width = n_pairs * LANES

    def body(q_ref, k_ref, v_ref, do_ref, o_ref, ab_ref, beta_ref, cl_ref, dq_ref, dk_out, dv_out, dk_ref, dv_ref):
        step = pl.program_id(1)

        @pl.when(step == 0)
        def _():
            dk_ref[...] = jnp.zeros_like(dk_ref)
            dv_ref[...] = jnp.zeros_like(dv_ref)

        lane = lax.broadcasted_iota(jnp.int32, (1, LANES), 1)
        first_lanes = lane < HEAD_DIM
        upper, incl = _sb_triangles(t)

        def query_tile(j, _):
            rows = pl.ds(pl.multiple_of(j * t, t), t)
            do2 = do_ref[rows, :]
            qs = _sb_stack_heads(q_ref[rows, :] * SCALE, first_lanes)
            dos = _sb_stack_heads(do2, first_lanes)
            prod = do2.astype(F32) * o_ref[rows, :]
            totals = jnp.concatenate([_rowsum(jnp.where(first_lanes, prod, 0.0)),
                                      _rowsum(jnp.where(first_lanes, 0.0, prod))], axis=0)

            def grads(offs, k_t, v_t, a_b, beta, c_e, dq):
                d_e = _dot_nt(dos, v_t) * a_b.astype(F32)
                d_hi = d_e.astype(BF16)
                d_lo = (d_e - d_hi.astype(F32)).astype(BF16)
                cols = [slice(c * t, (c + 1) * t) for c in range(len(offs))]
                suf_e, c_e = _sb_suffix([[d_hi[:, c], d_lo[:, c]] for c in cols], [_rowsum(d_e[:, c]) for c in cols], incl, c_e)
                dzb = (d_e - beta * (d_e + (totals - suf_e))).astype(BF16)
                dk_t = _dot_tn(dzb, qs)
                dv_t = _dot_tn(a_b, dos)
                for off, c in zip(offs, cols):
                    dk_ref[pl.ds(off, t), :] += dk_t[c, :]
                    dv_ref[pl.ds(off, t), :] += dv_t[c, :]
                return c_e, dq + _dot(dzb, k_t)

            def first_visit(offs, carry):
                _, c_e, dq = carry
                k_t = _sb_key_tiles(k_ref, offs, t)
                v_t = _sb_key_tiles(v_ref, offs, t)
                c_e, dq = grads(offs, k_t, v_t, ab_ref[j], beta_ref[j], c_e, dq)
                return cl_ref[j], c_e, dq

            def visit(offs, carry):
                c_l, c_e, dq = carry
                k_t = _sb_key_tiles(k_ref, offs, t)
                v_t = _sb_key_tiles(v_ref, offs, t)
                lb, a, c_l = _sb_scores(qs, k_t, upper, c_l, None, t)
                c_e, dq = grads(offs, k_t, v_t, a.astype(BF16), jnp.exp(lb), c_e, dq)
                return c_l, c_e, dq

            init = (jnp.zeros((2 * t, 1), F32), jnp.zeros((2 * t, 1), F32), jnp.zeros((2 * t, LANES), F32))
            _, _, dq = _sb_walk(step * per_step + j, t, first_visit, visit, init)
            dq_ref[rows, :] = (jnp.where(first_lanes, dq[:t], dq[t:]) * SCALE).astype(BF16)
            return 0

        lax.fori_loop(0, per_step, query_tile, 0)

        @pl.when(step == n_steps - 1)
        def _():
            dk_out[...] = dk_ref[...].astype(BF16)
            dv_out[...] = dv_ref[...].astype(BF16)

    qblk = pl.BlockSpec((t * per_step, LANES), lambda h, i: (i, h))
    whole = pl.BlockSpec((s, LANES), lambda h, i: (0, h))
    first_specs, _ = _sb_first_specs(n_pairs, n_steps, per_step, t)
    return _call(
        body, "sb_bwd", (n_pairs, n_steps),
        in_specs=[qblk,
                  pl.BlockSpec((s, LANES), lambda h, i: (0, n_pairs + h)),
                  pl.BlockSpec((s, LANES), lambda h, i: (0, 2 * n_pairs + h)),
                  qblk, qblk] + first_specs,
        out_specs=[qblk, whole, whole],
        out_shape=[jax.ShapeDtypeStruct((s, width), BF16)] * 3,
        args=(proj, proj, proj, dout, out, *first),
        scratch_shapes=[pltpu.VMEM((s, LANES), F32), pltpu.VMEM((s, LANES), F32)], comm=comm)


def _swa_bwd(proj, dout, kept, bucket, n_heads, qcol, kcol, vcol, comm=None):
    s = proj.shape[0]
    width = n_heads * HEAD_DIM
    n_groups = LANES // HEAD_DIM
    per_group = n_heads // n_groups
    nb = s // SWA_BLOCK

    per_step, n_steps = _swa_steps(s)

    def body(q_ref, kp_ref, kc_ref, vp_ref, vc_ref, bucket_ref, do_ref, prob_ref, psink_ref,
             dq_ref, dk_out, dv_out, dsk_ref, drb_ref, dbias_ref, dk_ref, dv_ref, kall_ref, vall_ref):
        step = pl.program_id(0)

        @pl.when(step == 0)
        def _():
            dbias_ref[...] = jnp.zeros_like(dbias_ref)
            dk_ref[...] = jnp.zeros_like(dk_ref)
            dv_ref[...] = jnp.zeros_like(dv_ref)
            dsk_ref[...] = jnp.zeros_like(dsk_ref)

        _swa_stage_keys(kp_ref, kc_ref, kall_ref)
        _swa_stage_keys(vp_ref, vc_ref, vall_ref)
        lane = lax.broadcasted_iota(jnp.int32, (1, LANES), 1)
        first = lane < HEAD_DIM

        def query_block(j, _):
            i = step * per_step + j
            rows = pl.ds(pl.multiple_of(j * SWA_BLOCK, SWA_BLOCK), SWA_BLOCK)
            band = pl.ds(pl.multiple_of(j * SWA_BLOCK, SWA_BLOCK), 2 * SWA_BLOCK)
            q_blk, do_blk = q_ref[rows, :], do_ref[rows, :]
            kcat, vcat = kall_ref[band, :], vall_ref[band, :]
            dkcat = jnp.zeros((2 * SWA_BLOCK, LANES), F32)
            dvcat = jnp.zeros((2 * SWA_BLOCK, LANES), F32)
            pieces = {}
            for g in range(n_groups):
                sel = first if g == 0 else jnp.logical_not(first)
                q_g = _swa_stack(q_blk, g, per_group, sel, SCALE)
                do_g = _swa_stack(do_blk, g, per_group, sel)
                prob, p_sink = prob_ref[j, g], psink_ref[j, g]
                dprob = _dot_nt(do_g, vcat)
                delta = _rowsum(prob * dprob)
                dlog = prob * (dprob - delta)
                sink_term = p_sink * delta
                for hh in range(per_group):
                    h = g * per_group + hh
                    tot = _colsum(sink_term[hh * SWA_BLOCK:(hh + 1) * SWA_BLOCK, :])
                    dsk_ref[h:h + 1, :] += jnp.broadcast_to(-tot, (1, LANES))
                dbias_ref[g] += dlog
                dlb = dlog.astype(BF16)
                _swa_unstack(_dot(dlb, kcat) * SCALE, g, per_group, pieces)
                dkcat += _dot_tn(dlb, q_g)
                dvcat += _dot_tn(prob.astype(BF16), do_g)
            for c in range(n_heads // 2):
                dq_ref[rows, c * LANES:(c + 1) * LANES] = jnp.where(first, pieces[2 * c], pieces[2 * c + 1]).astype(BF16)

            cur = pl.multiple_of(i * SWA_BLOCK, SWA_BLOCK)
            dk_ref[pl.ds(cur, SWA_BLOCK), :] += dkcat[SWA_BLOCK:, :]
            dv_ref[pl.ds(cur, SWA_BLOCK), :] += dvcat[SWA_BLOCK:, :]

            @pl.when(i > 0)
            def _():
                prv = pl.multiple_of((i - 1) * SWA_BLOCK, SWA_BLOCK)
                dk_ref[pl.ds(prv, SWA_BLOCK), :] += dkcat[:SWA_BLOCK, :]
                dv_ref[pl.ds(prv, SWA_BLOCK), :] += dvcat[:SWA_BLOCK, :]

            return 0

        lax.fori_loop(0, per_step, query_block, 0)

        @pl.when(step == n_steps - 1)
        def _():
            bk = bucket_ref[...]
            rowi = lax.broadcasted_iota(jnp.int32, (REL_BUCKETS, LANES), 0)
            coli = lax.broadcasted_iota(jnp.int32, (REL_BUCKETS, LANES), 1)
            res = jnp.zeros((REL_BUCKETS, LANES), F32)
            for h in range(n_heads):
                g, hh = divmod(h, per_group)
                db = dbias_ref[g, hh * SWA_BLOCK:(hh + 1) * SWA_BLOCK, :]
                for b in range(REL_BUCKETS):
                    tot = _colsum(_rowsum(jnp.where(bk == b, db, 0.0)))
                    res = jnp.where((rowi == b) & (coli == h), tot, res)
            drb_ref[...] = res
            dk_out[...] = dk_ref[...].astype(BF16)
            dv_out[...] = dv_ref[...].astype(BF16)

    kept_specs, _ = _swa_kept_specs(nb, n_groups, per_group, per_step)
    rows_spec = pl.BlockSpec((per_step * SWA_BLOCK, width), lambda i: (i, 0))
    in_specs = _swa_specs(n_heads, qcol, kcol, vcol, per_step)[:6] + [rows_spec] + kept_specs
    staged = pltpu.VMEM(((per_step + 1) * SWA_BLOCK, LANES), BF16)
    return _call(
        body, "swa_bwd", (n_steps,),
        in_specs=in_specs,
        out_specs=[rows_spec, _full((s, LANES)), _full((s, LANES)), _full((8, LANES)), _full((REL_BUCKETS, LANES))],
        out_shape=[jax.ShapeDtypeStruct((s, width), BF16), jax.ShapeDtypeStruct((s, LANES), BF16),
                   jax.ShapeDtypeStruct((s, LANES), BF16), jax.ShapeDtypeStruct((8, LANES), F32),
                   jax.ShapeDtypeStruct((REL_BUCKETS, LANES), F32)],
        args=(proj, proj, proj, proj, proj, bucket, dout, *kept),
        scratch_shapes=[pltpu.VMEM((n_groups, per_group * SWA_BLOCK, 2 * SWA_BLOCK), F32),
                        pltpu.VMEM((s, LANES), F32), pltpu.VMEM((s, LANES), F32), staged, staged],
        comm=comm)


def _proj_bwd(pieces, w_in_t, du1, x, g_in, comm=None):
    s, d = x.shape
    cols = w_in_t.shape[0]
    tm = min(ROW_TILE, s)
    n_p = len(pieces)

    def body(*refs):
        p_refs = refs[:n_p]
        w_ref, du_ref, x_ref, g_ref, dx_ref, st_ref = refs[n_p:]
        i = pl.program_id(0)

        @pl.when(i == 0)
        def _():
            st_ref[...] = jnp.zeros_like(st_ref)

        dproj = jnp.concatenate([p[...] for p in p_refs], axis=1)
        dh0 = _dot(dproj, w_ref[...]) + ALPHA * du_ref[...]
        xhat, r = _ln_hat(x_ref[...])
        st_ref[0:1, :] += _colsum(dh0 * xhat)
        st_ref[1:2, :] += _colsum(dh0)
        dx_ref[...] = _ln_bwd(dh0 * g_ref[...], xhat, r)

    row = lambda width: pl.BlockSpec((tm, width), lambda i: (i, 0))
    return _call(
        body, "proj_bwd", (s // tm,),
        in_specs=[row(p.shape[1]) for p in pieces] + [_full((cols, d)), row(d), row(d), _full((1, d))],
        out_specs=[row(d), _full((8, d))],
        out_shape=[jax.ShapeDtypeStruct((s, d), F32), jax.ShapeDtypeStruct((8, d), F32)],
        args=(*pieces, w_in_t, du1, x, g_in), comm=comm)


def _wgrad(name, pieces, b, tm, tn):
    s, n = b.shape
    m = sum(p.shape[1] for p in pieces)
    n_p = len(pieces)
    assert n_p == 1 or tm == m
    ts = min(WGRAD_TOKENS if b.dtype == BF16 and n_p == 1 else WGRAD_TOKENS // 2, s)
    n_k = s // ts

    def body(*refs):
        p_refs, b_ref, o_ref, acc_ref = refs[:n_p], refs[n_p], refs[n_p + 1], refs[n_p + 2]
        k = pl.program_id(2)

        @pl.when(k == 0)
        def _():
            acc_ref[...] = jnp.zeros_like(acc_ref)

        a = p_refs[0][...] if n_p == 1 else jnp.concatenate([p[...] for p in p_refs], axis=1)
        acc_ref[...] += _dot_tn(a, b_ref[...].astype(BF16))

        @pl.when(k == n_k - 1)
        def _():
            o_ref[...] = acc_ref[...].astype(BF16)

    piece_spec = lambda p: pl.BlockSpec((ts, tm if n_p == 1 else p.shape[1]), lambda i, j, k: (k, i))
    return pl.pallas_call(
        body, name=name, grid=(m // tm, n // tn, n_k),
        in_specs=[piece_spec(p) for p in pieces] + [pl.BlockSpec((ts, tn), lambda i, j, k: (k, j))],
        out_specs=pl.BlockSpec((tm, tn), lambda i, j, k: (i, j)),
        out_shape=jax.ShapeDtypeStruct((m, n), BF16),
        scratch_shapes=[pltpu.VMEM((tm, tn), F32)],
        compiler_params=_params(),
    )(*pieces, b)


def _adamw_math(w, g, m, v):
    m = ADAM_B1 * m + (1.0 - ADAM_B1) * g
    v = ADAM_B2 * v + (1.0 - ADAM_B2) * (g * g)
    m_hat = m / (1.0 - ADAM_B1 ** ADAM_STEP)
    v_hat = v / (1.0 - ADAM_B2 ** ADAM_STEP)
    delta = -ADAM_LR * (m_hat / (jnp.sqrt(v_hat) + ADAM_EPS) + ADAM_WD * w)
    return delta, m, v


def _adamw_rows(rows):
    return max(r for r in range(16, 257, 16) if rows % r == 0)


def _adamw(name, landed, w, m, v, tr):
    rows, cols = w.shape

    def body(l_ref, w_ref, m_ref, v_ref, g_ref, d_ref, nm_ref, nv_ref):
        g = l_ref[0].astype(F32)
        for src in range(1, N_DEV):
            g = g + l_ref[src].astype(F32)
        delta, nm, nv = _adamw_math(w_ref[...], g, m_ref[...], v_ref[...])
        g_ref[...] = g
        d_ref[...] = delta
        nm_ref[...] = nm
        nv_ref[...] = nv

    blk = pl.BlockSpec((tr, cols), lambda i: (i, 0))
    shape = jax.ShapeDtypeStruct((rows, cols), F32)
    return pl.pallas_call(
        body, name=name, grid=(rows // tr,),
        in_specs=[pl.BlockSpec((N_DEV, tr, cols), lambda i: (0, i, 0)), blk, blk, blk],
        out_specs=[blk, blk, blk, blk],
        out_shape=[shape, shape, shape, shape],
        compiler_params=_params(),
    )(landed, w, m, v)


def _pack(d, ln_in_g, ln_in_b, ln1_g, ln1_b, ln2_g, ln2_b, sb_g, sw_g, rel_bias, sinks, extra=None):
    tail = [rel_bias.reshape(-1), sinks.reshape(-1)]
    if extra is not None:
        tail.append(extra.reshape(-1))
    tail = jnp.concatenate(tail)
    tail = jnp.concatenate([tail, jnp.zeros((d - tail.shape[0],), F32)])
    rows = [ln_in_g.reshape(-1), ln_in_b.reshape(-1), ln1_g.reshape(-1), ln1_b.reshape(-1),
            ln2_g.reshape(-1), ln2_b.reshape(-1),
            jnp.concatenate([sb_g.reshape(-1), sw_g.reshape(-1)]), tail]
    return jnp.stack(rows)


def _unpack(p, wsb, n_rb, n_sk):
    return [p[0], p[1], p[6, :wsb][None], p[6, wsb:][None], p[7, n_rb:n_rb + n_sk][None],
            p[7, :n_rb].reshape(REL_BUCKETS, -1), p[2][None], p[3][None], p[4][None], p[5][None]]


def kernel(x, ln_in_g, ln_in_b, w_in, sb_norm_g, swa_norm_g, sinks, rel_bias, w_out, ln1_g, ln1_b, w_gate_up, w_down, ln2_g, ln2_b, loss_target, m_ln_in_g, m_ln_in_b, m_w_in, m_sb_norm_g, m_swa_norm_g, m_sinks, m_rel_bias, m_w_out, m_ln1_g, m_ln1_b, m_w_gate_up, m_w_down, m_ln2_g, m_ln2_b, v_ln_in_g, v_ln_in_b, v_w_in, v_sb_norm_g, v_swa_norm_g, v_sinks, v_rel_bias, v_w_out, v_ln1_g, v_ln1_b, v_w_gate_up, v_w_down, v_ln2_g, v_ln2_b):
    x2 = x[0]
    tgt = loss_target[0]
    s, d = x2.shape
    wsb = sb_norm_g.shape[-1]
    wsw = swa_norm_g.shape[-1]
    n_sw_heads = sinks.shape[-1]
    n_pairs = wsb // LANES
    dff = w_down.shape[1] * N_DEV
    assert wsb % LANES == 0 and wsw % LANES == 0 and n_sw_heads * HEAD_DIM == wsw
    assert 3 * wsb % wsw == 0 and dff % LANES == 0 and s % SWA_BLOCK == 0
    qcol = 3 * wsb // wsw
    kcol = (3 * wsb + wsw) // LANES
    vcol = kcol + 1
    assert w_in.shape[-1] * N_DEV == (vcol + 1) * LANES

    t2 = lambda a: jnp.transpose(a[0])
    big_w = [t2(w_in), w_out[0], t2(w_gate_up), w_down[0]]
    big_m = [t2(m_w_in), m_w_out[0], t2(m_w_gate_up), m_w_down[0]]
    big_v = [t2(v_w_in), v_w_out[0], t2(v_w_gate_up), v_w_down[0]]

    cat_rows = lambda g: g.reshape(N_DEV * g.shape[1], g.shape[2])
    shards = [w.astype(BF16) for w in big_w]
    w_in_t = cat_rows(_allgather_via_sibling("w_in_allgather", shards[0]))

    vec = lambda a: a.reshape(1, -1)
    g_in, b_in = vec(ln_in_g), vec(ln_in_b)
    bucket = jnp.asarray(_swa_bucket_table())

    h0b, proj = _ln_proj(x2, g_in, b_in, w_in_t)
    sb_out, sb_first, gathered = _sb_fwd(proj, n_pairs, comm=(shards[1:3], ["gather"] * 2))
    w_out_f, w_gu_t = cat_rows(gathered[0]), cat_rows(gathered[1])
    sw_out, sw_kept, gathered = _swa_fwd(proj, bucket, rel_bias, sinks, n_sw_heads, qcol, kcol, vcol,
                                comm=(shards[3:], ["gather"]))
    w_down_f = cat_rows(gathered[0])
    du1, merged, h1b, act, dgu, du2b, dsb, dsw, st_ffn = _mix_ffn(
        sb_out, sw_out, x2, g_in, b_in, sb_norm_g, swa_norm_g, w_out_f, ln1_g, ln1_b, w_gu_t, w_down_f, ln2_g, ln2_b, tgt)

    split_rows = lambda g: g.reshape(N_DEV, g.shape[0] // N_DEV, g.shape[1])
    gw_gu = _wgrad("wgrad_gate_up", [dgu], h1b, dff // 2, d)
    gw_down = _wgrad("wgrad_down", [act], du2b, dff // 2, d)
    gw_out = _wgrad("wgrad_out", [merged], du1, min(512, d), d)
    (dq_sb, dk_sb, dv_sb), (land_gu, land_out) = _sb_bwd(
        proj, dsb, sb_out, sb_first, n_pairs, comm=([split_rows(gw_gu), split_rows(gw_out)], ["scatter"] * 2))
    (dq_sw, dk_sw, dv_sw, st_sink, st_rb), (land_down,) = _swa_bwd(
        proj, dsw, sw_kept, bucket, n_sw_heads, qcol, kcol, vcol, comm=([split_rows(gw_down)], ["scatter"]))
    pieces = [dq_sb, dk_sb, dv_sb, dq_sw, dk_sw, dv_sw]
    gw_in = _wgrad("wgrad_in", pieces, h0b, proj.shape[1], d)
    (grad_x, st_in), (land_in,) = _proj_bwd(pieces, w_in_t, du1, x2, g_in, comm=([split_rows(gw_in)], ["scatter"]))

    n_rb = rel_bias.size
    small = _pack(d, st_in[0], st_in[1], st_ffn[3], st_ffn[4], st_ffn[0], st_ffn[1],
                  st_ffn[5, :wsb], st_ffn[5, wsb:], st_rb[:, :n_sw_heads], st_sink[:n_sw_heads, 0],
                  extra=st_ffn[2, 0:1])
    land_small = _exchange("small_grads_allgather", [small], ["gather"])[0]
    landed = [land_in, land_out, land_gu, land_down, land_small]

    big = []
    for name, land, w, m, v in zip(["adamw_in", "adamw_out", "adamw_gate_up", "adamw_down"], landed[:4], big_w, big_m, big_v):
        big.append(_adamw(name, land, w, m, v, _adamw_rows(w.shape[0])))

    small_w = _pack(d, ln_in_g, ln_in_b, ln1_g, ln1_b, ln2_g, ln2_b, sb_norm_g, swa_norm_g, rel_bias, sinks)
    small_m = _pack(d, m_ln_in_g, m_ln_in_b, m_ln1_g, m_ln1_b, m_ln2_g, m_ln2_b, m_sb_norm_g, m_swa_norm_g, m_rel_bias, m_sinks)
    small_v = _pack(d, v_ln_in_g, v_ln_in_b, v_ln1_g, v_ln1_b, v_ln2_g, v_ln2_b, v_sb_norm_g, v_swa_norm_g, v_rel_bias, v_sinks)
    sg, sd, sm, sv = _adamw("adamw_small", landed[4], small_w, small_m, small_v, 8)
    n_sk = sinks.size
    loss = sg[7, n_rb + n_sk]

    def leaves(idx):
        sm_l = _unpack([sg, sd, sm, sv][idx], wsb, n_rb, n_sk)
        bg = [jnp.transpose(big[0][idx])[None], big[1][idx][None], jnp.transpose(big[2][idx])[None], big[3][idx][None]]
        return [sm_l[0], sm_l[1], bg[0], sm_l[2], sm_l[3], sm_l[4], sm_l[5], bg[1], sm_l[6], sm_l[7], bg[2], bg[3], sm_l[8], sm_l[9]]

    return (loss, grad_x[None], *leaves(0), *leaves(1), *leaves(2), *leaves(3))
```

```python
import functools
import math

import numpy as np
import jax
import jax.numpy as jnp
from jax import lax
from jax.experimental import pallas as pl
from jax.experimental.pallas import tpu as pltpu

F32 = jnp.float32
BF16 = jnp.bfloat16
MESH = pl.DeviceIdType.MESH

N_DEV = 8
LANES = 128
HEAD_DIM = 64
SCALE = HEAD_DIM ** -0.5
SWA_BLOCK = 128
SWA_BLOCKS_PER_STEP = 4
REL_BUCKETS = 32
REL_MAX_DIST = 128
ALPHA = 2.0 ** 0.25
LN_EPS = 1e-5
RMS_EPS = 1e-6
ADAM_LR = 0.001
ADAM_B1 = 0.9
ADAM_B2 = 0.999
ADAM_EPS = 1e-08
ADAM_WD = 0.01
ADAM_STEP = 10

ROW_TILE = 512
SB_TILE = 256
SB_TILES_PER_STEP = 4
FFN_TILE = 256
WGRAD_TOKENS = 2048
SB_UNDERFLOW = -110.0
SB_MASKED = -1e30
MIB = 1024 * 1024


def _params(vmem_mib=48):
    return pltpu.CompilerParams(vmem_limit_bytes=vmem_mib * MIB)


def _dot(a, b):
    return jnp.dot(a, b, preferred_element_type=F32)


def _dot_nt(a, b):
    return lax.dot_general(a, b, (((1,), (1,)), ((), ())), preferred_element_type=F32)


def _dot_tn(a, b):
    return lax.dot_general(a, b, (((0,), (0,)), ((), ())), preferred_element_type=F32)


def _ln_hat(x):
    mu = jnp.mean(x, axis=-1, keepdims=True)
    xc = x - mu
    var = jnp.mean(xc * xc, axis=-1, keepdims=True)
    r = lax.rsqrt(var + LN_EPS)
    return xc * r, r


def _ln_bwd(dxhat, xhat, r):
    return r * (dxhat - jnp.mean(dxhat, axis=-1, keepdims=True)
                - xhat * jnp.mean(dxhat * xhat, axis=-1, keepdims=True))


def _colsum(a):
    return jnp.sum(a, axis=0, keepdims=True)


def _rowsum(a):
    return jnp.sum(a, axis=1, keepdims=True)


def _full(shape):
    return pl.BlockSpec(shape, lambda *_: (0,) * len(shape))


def _comm_out_shapes(arrays, kinds):
    shapes = []
    for a, kind in zip(arrays, kinds):
        blk = a.shape if kind == "gather" else a.shape[1:]
        shapes.append(jax.ShapeDtypeStruct((N_DEV,) + tuple(blk), a.dtype))
    return shapes


def _comm_sems(n):
    return [pltpu.SemaphoreType.DMA((n, N_DEV - 1)), pltpu.SemaphoreType.DMA((n, N_DEV - 1)),
            pltpu.SemaphoreType.DMA((n,))]


def _comm_copies(ins, outs, kinds, send_sems, recv_sems, local_sems):
    x, y, c = lax.axis_index("x"), lax.axis_index("y"), lax.axis_index("c")
    me = 4 * x + 2 * y + c

    def src_for(t, dev_lin):
        return ins[t] if kinds[t] == "gather" else ins[t].at[dev_lin]

    local = [pltpu.make_async_copy(src_for(t, me), outs[t].at[me], local_sems.at[t]) for t in range(len(kinds))]
    sends, arrivals = [], []
    for k in range(1, N_DEV):
        px = 1 - x if (k >> 2) & 1 else x
        py = 1 - y if (k >> 1) & 1 else y
        pc = 1 - c if k & 1 else c
        peer_lin = 4 * px + 2 * py + pc
        for t in range(len(kinds)):
            sems = dict(send_sem=send_sems.at[t, k - 1], recv_sem=recv_sems.at[t, k - 1],
                        device_id=(px, py, pc), device_id_type=MESH)
            sends.append(pltpu.make_async_remote_copy(src_ref=src_for(t, peer_lin), dst_ref=outs[t].at[me], **sems))
            arrivals.append(pltpu.make_async_remote_copy(src_ref=src_for(t, peer_lin), dst_ref=outs[t].at[peer_lin], **sems))
    return local, sends, arrivals


def _comm_start(ins, outs, kinds, sems):
    local, sends, _ = _comm_copies(ins, outs, kinds, *sems)
    for cp in local + sends:
        cp.start()


def _comm_finish(ins, outs, kinds, sems):
    local, sends, arrivals = _comm_copies(ins, outs, kinds, *sems)
    for cp in arrivals:
        cp.wait_recv()
    for cp in sends:
        cp.wait_send()
    for cp in local:
        cp.wait()


def _exchange(name, arrays, kinds):
    n = len(arrays)

    def body(*refs):
        ins, outs, sems = refs[:n], refs[n:2 * n], refs[2 * n:]
        _comm_start(ins, outs, kinds, sems)
        _comm_finish(ins, outs, kinds, sems)

    any_spec = pl.BlockSpec(memory_space=pl.ANY)
    return pl.pallas_call(
        body, name=name, out_shape=_comm_out_shapes(arrays, kinds),
        in_specs=[any_spec] * n, out_specs=[any_spec] * n,
        scratch_shapes=_comm_sems(n),
    )(*arrays)


def _allgather_via_sibling(name, shard):
    def body(x_ref, out_ref, send_sems, recv_sems, local_sem):
        x, y, c = lax.axis_index("x"), lax.axis_index("y"), lax.axis_index("c")
        me, sibling = (x, y, c), (x, y, 1 - c)
        chips = [(1 - x, y), (x, 1 - y), (1 - x, 1 - y)]

        def copy(k, block, to, src=None):
            slot = out_ref.at[4 * block[0] + 2 * block[1] + block[2]]
            return pltpu.make_async_remote_copy(
                src_ref=slot if src is None else src, dst_ref=slot,
                send_sem=send_sems.at[k], recv_sem=recv_sems.at[k], device_id=to, device_id_type=MESH)

        mine = pltpu.make_async_copy(x_ref, out_ref.at[4 * x + 2 * y + c], local_sem)
        mine.start()
        first = [copy(0, me, sibling, src=x_ref)]
        first += [copy(1 + j, me, (*chip, c), src=x_ref) for j, chip in enumerate(chips)]
        for cp in first:
            cp.start()
        passed = [copy(4 + j, (*chip, c), sibling) for j, chip in enumerate(chips)]
        for j, chip in enumerate(chips):
            copy(1 + j, (*chip, c), me).wait_recv()
            passed[j].start()
        copy(0, sibling, me).wait_recv()
        for j, chip in enumerate(chips):
            copy(4 + j, (*chip, 1 - c), me).wait_recv()
        for cp in first + passed:
            cp.wait_send()
        mine.wait()

    any_spec = pl.BlockSpec(memory_space=pl.ANY)
    return pl.pallas_call(
        body, name=name, out_shape=jax.ShapeDtypeStruct((N_DEV,) + shard.shape, shard.dtype),
        in_specs=[any_spec], out_specs=any_spec,
        scratch_shapes=[pltpu.SemaphoreType.DMA((N_DEV - 1,)), pltpu.SemaphoreType.DMA((N_DEV - 1,)),
                        pltpu.SemaphoreType.DMA],
    )(shard)


def _call(body, name, grid, in_specs, out_specs, out_shape, args, scratch_shapes=(), comm=None):
    if comm is None:
        outs = pl.pallas_call(body, name=name, grid=grid, in_specs=in_specs, out_specs=out_specs,
                              out_shape=out_shape, scratch_shapes=list(scratch_shapes),
                              compiler_params=_params())(*args)
        return outs, []
    arrays, kinds = comm
    n, n_in, n_out, n_scr = len(arrays), len(in_specs), len(out_specs), len(scratch_shapes)

    def fused(*refs):
        c_in, x_in = refs[:n_in], refs[n_in:n_in + n]
        c_out = refs[n_in + n:n_in + n + n_out]
        x_out = refs[n_in + n + n_out:n_in + 2 * n + n_out]
        rest = refs[n_in + 2 * n + n_out:]
        c_scr, sems = rest[:n_scr], rest[n_scr:]
        ids = [pl.program_id(a) for a in range(len(grid))]
        is_first = functools.reduce(jnp.logical_and, [i == 0 for i in ids])
        is_last = functools.reduce(jnp.logical_and, [i == g - 1 for i, g in zip(ids, grid)])

        @pl.when(is_first)
        def _():
            _comm_start(x_in, x_out, kinds, sems)

        body(*c_in, *c_out, *c_scr)

        @pl.when(is_last)
        def _():
            _comm_finish(x_in, x_out, kinds, sems)

    any_spec = pl.BlockSpec(memory_space=pl.ANY)
    outs = pl.pallas_call(
        fused, name=name, grid=grid,
        in_specs=list(in_specs) + [any_spec] * n, out_specs=list(out_specs) + [any_spec] * n,
        out_shape=list(out_shape) + _comm_out_shapes(arrays, kinds),
        scratch_shapes=list(scratch_shapes) + _comm_sems(n),
        compiler_params=_params())(*args, *arrays)
    return outs[:n_out], outs[n_out:]


def _ln_proj(x, g, b, w_in_t):
    s, d = x.shape
    cols = w_in_t.shape[0]
    tm = min(ROW_TILE, s)

    def body(x_ref, g_ref, b_ref, w_ref, h_ref, p_ref):
        xhat, _ = _ln_hat(x_ref[...])
        h = (xhat * g_ref[...] + b_ref[...]).astype(BF16)
        h_ref[...] = h
        p_ref[...] = _dot_nt(h, w_ref[...]).astype(BF16)

    row = lambda width: pl.BlockSpec((tm, width), lambda i: (i, 0))
    return pl.pallas_call(
        body, name="ln_proj", grid=(s // tm,),
        in_specs=[row(d), _full((1, d)), _full((1, d)), _full((cols, d))],
        out_specs=[row(d), row(cols)],
        out_shape=[jax.ShapeDtypeStruct((s, d), BF16), jax.ShapeDtypeStruct((s, cols), BF16)],
        compiler_params=_params(),
    )(x, g, b, w_in_t)


def _sb_triangles(t):
    row = lax.broadcasted_iota(jnp.int32, (2 * t, t), 0) & (t - 1)
    col = lax.broadcasted_iota(jnp.int32, (2 * t, t), 1)
    return (row > col).astype(BF16), (row >= col).astype(BF16)


def _sb_first_mask(z, t, has_prev):
    qrow = lax.broadcasted_iota(jnp.int32, (2 * t, t), 0) & (t - 1)
    col = lax.broadcasted_iota(jnp.int32, (2 * t, t), 1)
    return jnp.concatenate([jnp.where(has_prev, z[:, :t], SB_MASKED), jnp.where(col < qrow, z[:, t:], SB_MASKED)], axis=1)


def _sb_stack_heads(x2, first):
    zero = jnp.zeros_like(x2)
    return jnp.concatenate([jnp.where(first, x2, zero), jnp.where(first, zero, x2)], axis=0)


def _sb_key_tiles(ref, offs, t):
    tiles = [ref[pl.ds(off, t), :] for off in offs]
    return tiles[0] if len(tiles) == 1 else jnp.concatenate(tiles, axis=0)


def _sb_suffix(terms, row_sums, tri, carry):
    out = [None] * len(terms)
    for j in reversed(range(len(terms))):
        out[j] = carry + _dot(jnp.concatenate(terms[j], axis=1), tri)
        carry = carry + row_sums[j]
    return (out[0] if len(out) == 1 else jnp.concatenate(out, axis=1)), carry


def _sb_scores(qh, k_t, upper, carry_l, has_prev, t):
    z = _dot_nt(qh, k_t)
    if has_prev is not None:
        z = _sb_first_mask(z, t, has_prev)
    neg = jnp.minimum(z, 0.0)
    nz = neg - z
    sp = jnp.log(1.0 + jnp.exp(neg + nz))
    lb = neg - sp
    l1 = nz - sp
    hi = l1.astype(BF16)
    lo = (l1 - hi.astype(F32)).astype(BF16)
    cols = [slice(j * t, (j + 1) * t) for j in range(z.shape[1] // t)]
    suf, carry_l = _sb_suffix([[hi[:, c], lo[:, c]] for c in cols], [_rowsum(l1[:, c]) for c in cols], upper, carry_l)
    return lb, jnp.exp(lb + suf), carry_l


def _sb_walk(i, t, first_visit, visit, init):
    def alive(carry):
        return jnp.max(carry[0]) > SB_UNDERFLOW

    prev = pl.multiple_of(jnp.maximum(i - 1, 0) * t, t)
    carry = first_visit((prev, pl.multiple_of(i * t, t)), init)

    def cond(state):
        j, go, _ = state
        return (j < i - 1) & go

    def body(state):
        j, _, carry = state
        carry = visit((pl.multiple_of((i - 2 - j) * t, t),), carry)
        return j + 1, alive(carry), carry

    return lax.while_loop(cond, body, (jnp.int32(0), alive(carry), carry))[2]


def _sb_first_specs(n_pairs, n_steps, per_step, t):
    at = lambda h, i: (h * n_steps + i, 0, 0)
    n_tiles = n_pairs * n_steps * per_step
    specs = [pl.BlockSpec((per_step, 2 * t, 2 * t), at), pl.BlockSpec((per_step, 2 * t, 2 * t), at),
             pl.BlockSpec((per_step, 2 * t, 1), at)]
    shapes = [jax.ShapeDtypeStruct((n_tiles, 2 * t, 2 * t), BF16), jax.ShapeDtypeStruct((n_tiles, 2 * t, 2 * t), F32),
              jax.ShapeDtypeStruct((n_tiles, 2 * t, 1), F32)]
    return specs, shapes


def _sb_grid(s):
    t = min(SB_TILE, s)
    per_step = min(SB_TILES_PER_STEP, s // t)
    return t, per_step, s // (t * per_step)


def _sb_fwd(proj, n_pairs, comm=None):
    s = proj.shape[0]
    t, per_step, n_steps = _sb_grid(s)

    def body(q_ref, k_ref, v_ref, o_ref, ab_ref, beta_ref, cl_ref):
        lane = lax.broadcasted_iota(jnp.int32, (1, LANES), 1)
        first = lane < HEAD_DIM
        upper, _ = _sb_triangles(t)

        def query_tile(j, _):
            i = pl.program_id(1) * per_step + j
            rows = pl.ds(pl.multiple_of(j * t, t), t)
            qs = _sb_stack_heads(q_ref[rows, :] * SCALE, first)

            def first_visit(offs, carry):
                c_l, acc = carry
                lb, a, c_l = _sb_scores(qs, _sb_key_tiles(k_ref, offs, t), upper, c_l, i > 0, t)
                a_b = a.astype(BF16)
                ab_ref[j] = a_b
                beta_ref[j] = jnp.exp(lb)
                cl_ref[j] = c_l
                return c_l, acc + _dot(a_b, _sb_key_tiles(v_ref, offs, t))

            def visit(offs, carry):
                c_l, acc = carry
                _, a, c_l = _sb_scores(qs, _sb_key_tiles(k_ref, offs, t), upper, c_l, None, t)
                return c_l, acc + _dot(a.astype(BF16), _sb_key_tiles(v_ref, offs, t))

            init = (jnp.zeros((2 * t, 1), F32), jnp.zeros((2 * t, LANES), F32))
            _, acc = _sb_walk(i, t, first_visit, visit, init)
            o_ref[rows, :] = jnp.where(first, acc[:t], acc[t:])
            return 0

        lax.fori_loop(0, per_step, query_tile, 0)

    qblk = pl.BlockSpec((t * per_step, LANES), lambda h, i: (i, h))
    first_specs, first_shapes = _sb_first_specs(n_pairs, n_steps, per_step, t)
    outs, landed = _call(
        body, "sb_fwd", (n_pairs, n_steps),
        in_specs=[qblk,
                  pl.BlockSpec((s, LANES), lambda h, i: (0, n_pairs + h)),
                  pl.BlockSpec((s, LANES), lambda h, i: (0, 2 * n_pairs + h))],
        out_specs=[qblk] + first_specs,
        out_shape=[jax.ShapeDtypeStruct((s, n_pairs * LANES), F32)] + first_shapes,
        args=(proj, proj, proj), comm=comm)
    return outs[0], outs[1:], landed


def _swa_bucket_table():
    qi = np.arange(SWA_BLOCK)[:, None]
    cj = np.arange(2 * SWA_BLOCK)[None, :]
    dist = qi + SWA_BLOCK - cj
    exact = REL_BUCKETS // 2
    d = np.maximum(dist, 0)
    d_f = np.maximum(d, 1).astype(np.float32)
    large = exact + (np.log(d_f / np.float32(exact)) / np.float32(math.log(REL_MAX_DIST / exact))
                     * np.float32(REL_BUCKETS - exact)).astype(np.int32)
    large = np.minimum(large, REL_BUCKETS - 1)
    return np.where(d < exact, d, large).astype(np.int32)


def _swa_build_bias(bucket_ref, rb_ref, bias_ref, n_groups, per_group):
    bk = bucket_ref[...]
    dist = (lax.broadcasted_iota(jnp.int32, bk.shape, 0) + SWA_BLOCK) - lax.broadcasted_iota(jnp.int32, bk.shape, 1)
    window = (dist >= 0) & (dist < SWA_BLOCK)
    for g in range(n_groups):
        for hh in range(per_group):
            acc = jnp.zeros(bk.shape, F32)
            for b in range(REL_BUCKETS):
                acc = jnp.where(bk == b, rb_ref[b, g * per_group + hh], acc)
            bias_ref[g, hh * SWA_BLOCK:(hh + 1) * SWA_BLOCK, :] = jnp.where(window, acc, -jnp.inf)


def _swa_first_block_mask(i):
    col = lax.broadcasted_iota(jnp.int32, (1, 2 * SWA_BLOCK), 1)
    return jnp.where((col < SWA_BLOCK) & (i == 0), -jnp.inf, 0.0)


def _swa_place(blk, h, group, sel):
    if (h % 2) != group:
        blk = pltpu.roll(blk.astype(F32), HEAD_DIM, axis=1).astype(BF16)
    return jnp.where(sel, blk, jnp.zeros_like(blk))


def _swa_stack(ref, group, per_group, sel, scale=1.0):
    parts = []
    for hh in range(per_group):
        h = group * per_group + hh
        parts.append(_swa_place(ref[:, (h // 2) * LANES:(h // 2 + 1) * LANES], h, group, sel))
    stacked = jnp.concatenate(parts, axis=0)
    return stacked if scale == 1.0 else stacked * scale


def _swa_unstack(stacked, group, per_group, pieces):
    for hh in range(per_group):
        h = group * per_group + hh
        piece = stacked[hh * SWA_BLOCK:(hh + 1) * SWA_BLOCK, :]
        pieces[h] = pltpu.roll(piece, HEAD_DIM, axis=1) if (h % 2) != group else piece


def _swa_sink_rows(sk_ref, group, per_group):
    rowh = lax.broadcasted_iota(jnp.int32, (per_group * SWA_BLOCK, 1), 0) // SWA_BLOCK
    sink = jnp.zeros((per_group * SWA_BLOCK, 1), F32) + sk_ref[0, group * per_group]
    for hh in range(1, per_group):
        sink = jnp.where(rowh == hh, sk_ref[0, group * per_group + hh], sink)
    return sink


def _swa_probs(q_pos, kcat, bias_h, first_mask, sink):
    logits = _dot_nt(q_pos, kcat) + (bias_h + first_mask)
    m = jnp.maximum(jnp.max(logits, axis=1, keepdims=True), sink)
    p = jnp.exp(logits - m)
    es = jnp.exp(sink - m)
    inv = 1.0 / (_rowsum(p) + es)
    return p * inv, es * inv


def _swa_steps(s):
    per_step = min(SWA_BLOCKS_PER_STEP, s // SWA_BLOCK)
    return per_step, s // (SWA_BLOCK * per_step)


def _swa_specs(n_heads, qcol, kcol, vcol, per_step):
    width = n_heads * HEAD_DIM
    prev = lambda col: pl.BlockSpec((SWA_BLOCK, LANES), lambda i: (jnp.maximum(i * per_step - 1, 0), col))
    cur = lambda col: pl.BlockSpec((per_step * SWA_BLOCK, LANES), lambda i: (i, col))
    return [pl.BlockSpec((per_step * SWA_BLOCK, width), lambda i: (i, qcol)),
            prev(kcol), cur(kcol), prev(vcol), cur(vcol),
            _full((SWA_BLOCK, 2 * SWA_BLOCK)),
            pl.BlockSpec(memory_space=pltpu.SMEM), pl.BlockSpec(memory_space=pltpu.SMEM)]


def _swa_stage_keys(prev_ref, cur_ref, all_ref):
    all_ref[:SWA_BLOCK, :] = prev_ref[...]
    all_ref[SWA_BLOCK:, :] = cur_ref[...]


def _swa_kept_specs(nb, n_groups, per_group, per_step):
    rows = per_group * SWA_BLOCK
    at = lambda i: (i, 0, 0, 0)
    specs = [pl.BlockSpec((per_step, n_groups, rows, 2 * SWA_BLOCK), at), pl.BlockSpec((per_step, n_groups, rows, 1), at)]
    shapes = [jax.ShapeDtypeStruct((nb, n_groups, rows, 2 * SWA_BLOCK), F32), jax.ShapeDtypeStruct((nb, n_groups, rows, 1), F32)]
    return specs, shapes


def _swa_fwd(proj, bucket, rel_bias, sinks, n_heads, qcol, kcol, vcol, comm=None):
    s = proj.shape[0]
    width = n_heads * HEAD_DIM
    n_groups = LANES // HEAD_DIM
    per_group = n_heads // n_groups

    per_step, n_steps = _swa_steps(s)

    def body(q_ref, kp_ref, kc_ref, vp_ref, vc_ref, bucket_ref, rb_ref, sk_ref, o_ref, prob_ref, psink_ref,
             bias_ref, kall_ref, vall_ref):
        step = pl.program_id(0)

        @pl.when(step == 0)
        def _():
            _swa_build_bias(bucket_ref, rb_ref, bias_ref, n_groups, per_group)

        _swa_stage_keys(kp_ref, kc_ref, kall_ref)
        _swa_stage_keys(vp_ref, vc_ref, vall_ref)
        lane = lax.broadcasted_iota(jnp.int32, (1, LANES), 1)
        first = lane < HEAD_DIM

        def query_block(j, _):
            rows = pl.ds(pl.multiple_of(j * SWA_BLOCK, SWA_BLOCK), SWA_BLOCK)
            band = pl.ds(pl.multiple_of(j * SWA_BLOCK, SWA_BLOCK), 2 * SWA_BLOCK)
            first_mask = _swa_first_block_mask(step * per_step + j)
            q_blk = q_ref[rows, :]
            kcat, vcat = kall_ref[band, :], vall_ref[band, :]
            pieces = {}
            for g in range(n_groups):
                sel = first if g == 0 else jnp.logical_not(first)
                prob, p_sink = _swa_probs(_swa_stack(q_blk, g, per_group, sel, SCALE), kcat, bias_ref[g], first_mask,
                                          _swa_sink_rows(sk_ref, g, per_group))
                prob_ref[j, g] = prob
                psink_ref[j, g] = p_sink
                _swa_unstack(_dot(prob.astype(BF16), vcat), g, per_group, pieces)
            for c in range(n_heads // 2):
                o_ref[rows, c * LANES:(c + 1) * LANES] = jnp.where(first, pieces[2 * c], pieces[2 * c + 1])
            return 0

        lax.fori_loop(0, per_step, query_block, 0)

    kept_specs, kept_shapes = _swa_kept_specs(s // SWA_BLOCK, n_groups, per_group, per_step)
    staged = pltpu.VMEM(((per_step + 1) * SWA_BLOCK, LANES), BF16)
    outs, landed = _call(
        body, "swa_fwd", (n_steps,),
        in_specs=_swa_specs(n_heads, qcol, kcol, vcol, per_step),
        out_specs=[pl.BlockSpec((per_step * SWA_BLOCK, width), lambda i: (i, 0))] + kept_specs,
        out_shape=[jax.ShapeDtypeStruct((s, width), F32)] + kept_shapes,
        args=(proj, proj, proj, proj, proj, bucket, rel_bias, sinks),
        scratch_shapes=[pltpu.VMEM((n_groups, per_group * SWA_BLOCK, 2 * SWA_BLOCK), F32), staged, staged], comm=comm)
    return outs[0], outs[1:], landed


def _rms_fwd(o, g):
    r = lax.rsqrt(jnp.mean(o * o, axis=-1, keepdims=True) + RMS_EPS)
    n = o * r
    return n, r, n * g


def _mix_ffn(sb_out, sw_out, x, g_in, b_in, sb_g, sw_g, w_out, g1, b1, w_gu_t, w_down, g2, b2, target):
    s, d = x.shape
    wsb, wsw = sb_out.shape[1], sw_out.shape[1]
    dff = w_down.shape[0]
    assert wsb + wsw == d
    tm = min(FFN_TILE, s)

    def body(sb_ref, sw_ref, x_ref, gi_ref, bi_ref, sbg_ref, swg_ref, wo_hbm, g1_ref, b1_ref, wgu_hbm, wd_hbm,
             g2_ref, b2_ref, t_ref,
             du1_ref, mg_ref, h1b_ref, act_ref, dgu_ref, du2b_ref, dsb_ref, dsw_ref, st_ref,
             wo_ref, wgu_ref, wd_ref):
        @pl.when(pl.program_id(0) == 0)
        def _():
            pltpu.sync_copy(wo_hbm, wo_ref)
            pltpu.sync_copy(wgu_hbm, wgu_ref)
            pltpu.sync_copy(wd_hbm, wd_ref)
            st_ref[...] = jnp.zeros_like(st_ref)

        sb, sw = sb_ref[...], sw_ref[...]
        _, _, m_sb = _rms_fwd(sb, sbg_ref[...])
        _, _, m_sw = _rms_fwd(sw, swg_ref[...])
        m_sb = m_sb.astype(BF16)
        m_sw = m_sw.astype(BF16)
        mg_ref[:, :wsb] = m_sb
        mg_ref[:, wsb:] = m_sw
        xhat0, _ = _ln_hat(x_ref[...])
        u1 = ALPHA * (xhat0 * gi_ref[...] + bi_ref[...]) + _dot(m_sb, wo_ref[:wsb, :]) + _dot(m_sw, wo_ref[wsb:, :])

        xhat1, r1 = _ln_hat(u1)
        h1 = xhat1 * g1_ref[...] + b1_ref[...]
        h1b = h1.astype(BF16)
        h1b_ref[...] = h1b
        gate = _dot_nt(h1b, wgu_ref[:dff, :])
        up = _dot_nt(h1b, wgu_ref[dff:, :])
        sg = jax.nn.sigmoid(gate)
        silu = gate * sg
        act = (silu * up).astype(BF16)
        act_ref[...] = act
        u2 = ALPHA * h1 + _dot(act, wd_ref[...])
        xhat2, r2 = _ln_hat(u2)
        diff = xhat2 * g2_ref[...] + b2_ref[...] - t_ref[...]
        dh2 = diff * (1.0 / d)
        st_ref[0:1, :] += _colsum(dh2 * xhat2)
        st_ref[1:2, :] += _colsum(dh2)
        st_ref[2:3, :] += jnp.broadcast_to(_colsum(_rowsum(diff * diff)) * (0.5 / d), (1, d))
        du2 = _ln_bwd(dh2 * g2_ref[...], xhat2, r2)
        du2b = du2.astype(BF16)
        du2b_ref[...] = du2b
        dact = _dot_nt(du2b, wd_ref[...])
        dgate = (dact * up * (sg * (1.0 + gate * (1.0 - sg)))).astype(BF16)
        dup = (dact * silu).astype(BF16)
        dgu_ref[:, :dff] = dgate
        dgu_ref[:, dff:] = dup
        dh1 = _dot(dgate, wgu_ref[:dff, :]) + _dot(dup, wgu_ref[dff:, :]) + ALPHA * du2
        st_ref[3:4, :] += _colsum(dh1 * xhat1)
        st_ref[4:5, :] += _colsum(dh1)
        du1 = _ln_bwd(dh1 * g1_ref[...], xhat1, r1)
        du1_ref[...] = du1

        dmerged = _dot_nt(du1.astype(BF16), wo_ref[...])
        dsb, gsb = _rms_bwd(dmerged[:, :wsb], sb, sbg_ref[...])
        dsw, gsw = _rms_bwd(dmerged[:, wsb:], sw, swg_ref[...])
        dsb_ref[...] = dsb.astype(BF16)
        dsw_ref[...] = dsw.astype(BF16)
        st_ref[5:6, :wsb] += gsb
        st_ref[5:6, wsb:] += gsw

    row = lambda width: pl.BlockSpec((tm, width), lambda i: (i, 0))
    vec = lambda width: _full((1, width))
    hbm = pl.BlockSpec(memory_space=pl.ANY)
    bf = lambda width: jax.ShapeDtypeStruct((s, width), BF16)
    return pl.pallas_call(
        body, name="mix_ffn", grid=(s // tm,),
        in_specs=[row(wsb), row(wsw), row(d), vec(d), vec(d), vec(wsb), vec(wsw), hbm, vec(d), vec(d), hbm, hbm,
                  vec(d), vec(d), row(d)],
        out_specs=[row(d), row(d), row(d), row(dff), row(2 * dff), row(d), row(wsb), row(wsw), _full((8, d))],
        out_shape=[jax.ShapeDtypeStruct((s, d), F32), bf(d), bf(d), bf(dff), bf(2 * dff), bf(d), bf(wsb), bf(wsw),
                   jax.ShapeDtypeStruct((8, d), F32)],
        scratch_shapes=[pltpu.VMEM(w_out.shape, BF16), pltpu.VMEM(w_gu_t.shape, BF16), pltpu.VMEM(w_down.shape, BF16)],
        compiler_params=_params(60),
    )(sb_out, sw_out, x, g_in, b_in, sb_g, sw_g, w_out, g1, b1, w_gu_t, w_down, g2, b2, target)


def _rms_bwd(dm, o, g):
    n, r, _ = _rms_fwd(o, g)
    dn = dm * g
    return r * (dn - n * jnp.mean(dn * n, axis=-1, keepdims=True)), _colsum(dm * n)


def _sb_bwd(proj, dout, out, first, n_pairs, comm=None):
    s = proj.shape[0]
    t, per_step, n_steps = _sb_grid(s)
    width = n_pairs * LANES

    def body(q_ref, k_ref, v_ref, do_ref, o_ref, ab_ref, beta_ref, cl_ref, dq_ref, dk_out, dv_out, dk_ref, dv_ref):
        step = pl.program_id(1)

        @pl.when(step == 0)
        def _():
            dk_ref[...] = jnp.zeros_like(dk_ref)
            dv_ref[...] = jnp.zeros_like(dv_ref)

        lane = lax.broadcasted_iota(jnp.int32, (1, LANES), 1)
        first_lanes = lane < HEAD_DIM
        upper, incl = _sb_triangles(t)

        def query_tile(j, _):
            rows = pl.ds(pl.multiple_of(j * t, t), t)
            do2 = do_ref[rows, :]
            qs = _sb_stack_heads(q_ref[rows, :] * SCALE, first_lanes)
            dos = _sb_stack_heads(do2, first_lanes)
            prod = do2.astype(F32) * o_ref[rows, :]
            totals = jnp.concatenate([_rowsum(jnp.where(first_lanes, prod, 0.0)),
                                      _rowsum(jnp.where(first_lanes, 0.0, prod))], axis=0)

            def grads(offs, k_t, v_t, a_b, beta, c_e, dq):
                d_e = _dot_nt(dos, v_t) * a_b.astype(F32)
                d_hi = d_e.astype(BF16)
                d_lo = (d_e - d_hi.astype(F32)).astype(BF16)
                cols = [slice(c * t, (c + 1) * t) for c in range(len(offs))]
                suf_e, c_e = _sb_suffix([[d_hi[:, c], d_lo[:, c]] for c in cols], [_rowsum(d_e[:, c]) for c in cols], incl, c_e)
                dzb = (d_e - beta * (d_e + (totals - suf_e))).astype(BF16)
                dk_t = _dot_tn(dzb, qs)
                dv_t = _dot_tn(a_b, dos)
                for off, c in zip(offs, cols):
                    dk_ref[pl.ds(off, t), :] += dk_t[c, :]
                    dv_ref[pl.ds(off, t), :] += dv_t[c, :]
                return c_e, dq + _dot(dzb, k_t)

            def first_visit(offs, carry):
                _, c_e, dq = carry
                k_t = _sb_key_tiles(k_ref, offs, t)
                v_t = _sb_key_tiles(v_ref, offs, t)
                c_e, dq = grads(offs, k_t, v_t, ab_ref[j], beta_ref[j], c_e, dq)
                return cl_ref[j], c_e, dq

            def visit(offs, carry):
                c_l, c_e, dq = carry
                k_t = _sb_key_tiles(k_ref, offs, t)
                v_t = _sb_key_tiles(v_ref, offs, t)
                lb, a, c_l = _sb_scores(qs, k_t, upper, c_l, None, t)
                c_e, dq = grads(offs, k_t, v_t, a.astype(BF16), jnp.exp(lb), c_e, dq)
                return c_l, c_e, dq

            init = (jnp.zeros((2 * t, 1), F32), jnp.zeros((2 * t, 1), F32), jnp.zeros((2 * t, LANES), F32))
            _, _, dq = _sb_walk(step * per_step + j, t, first_visit, visit, init)
            dq_ref[rows, :] = (jnp.where(first_lanes, dq[:t], dq[t:]) * SCALE).astype(BF16)
            return 0

        lax.fori_loop(0, per_step, query_tile, 0)

        @pl.when(step == n_steps - 1)
        def _():
            dk_out[...] = dk_ref[...].astype(BF16)
            dv_out[...] = dv_ref[...].astype(BF16)

    qblk = pl.BlockSpec((t * per_step, LANES), lambda h, i: (i, h))
    whole = pl.BlockSpec((s, LANES), lambda h, i: (0, h))
    first_specs, _ = _sb_first_specs(n_pairs, n_steps, per_step, t)
    return _call(
        body, "sb_bwd", (n_pairs, n_steps),
        in_specs=[qblk,
                  pl.BlockSpec((s, LANES), lambda h, i: (0, n_pairs + h)),
                  pl.BlockSpec((s, LANES), lambda h, i: (0, 2 * n_pairs + h)),
                  qblk, qblk] + first_specs,
        out_specs=[qblk, whole, whole],
        out_shape=[jax.ShapeDtypeStruct((s, width), BF16)] * 3,
        args=(proj, proj, proj, dout, out, *first),
        scratch_shapes=[pltpu.VMEM((s, LANES), F32), pltpu.VMEM((s, LANES), F32)], comm=comm)


def _swa_bwd(proj, dout, kept, bucket, n_heads, qcol, kcol, vcol, comm=None):
    s = proj.shape[0]
    width = n_heads * HEAD_DIM
    n_groups = LANES // HEAD_DIM
    per_group = n_heads // n_groups
    nb = s // SWA_BLOCK

    per_step, n_steps = _swa_steps(s)

    def body(q_ref, kp_ref, kc_ref, vp_ref, vc_ref, bucket_ref, do_ref, prob_ref, psink_ref,
             dq_ref, dk_out, dv_out, dsk_ref, drb_ref, dbias_ref, dk_ref, dv_ref, kall_ref, vall_ref):
        step = pl.program_id(0)

        @pl.when(step == 0)
        def _():
            dbias_ref[...] = jnp.zeros_like(dbias_ref)
            dk_ref[...] = jnp.zeros_like(dk_ref)
            dv_ref[...] = jnp.zeros_like(dv_ref)
            dsk_ref[...] = jnp.zeros_like(dsk_ref)

        _swa_stage_keys(kp_ref, kc_ref, kall_ref)
        _swa_stage_keys(vp_ref, vc_ref, vall_ref)
        lane = lax.broadcasted_iota(jnp.int32, (1, LANES), 1)
        first = lane < HEAD_DIM

        def query_block(j, _):
            i = step * per_step + j
            rows = pl.ds(pl.multiple_of(j * SWA_BLOCK, SWA_BLOCK), SWA_BLOCK)
            band = pl.ds(pl.multiple_of(j * SWA_BLOCK, SWA_BLOCK), 2 * SWA_BLOCK)
            q_blk, do_blk = q_ref[rows, :], do_ref[rows, :]
            kcat, vcat = kall_ref[band, :], vall_ref[band, :]
            dkcat = jnp.zeros((2 * SWA_BLOCK, LANES), F32)
            dvcat = jnp.zeros((2 * SWA_BLOCK, LANES), F32)
            pieces = {}
            for g in range(n_groups):
                sel = first if g == 0 else jnp.logical_not(first)
                q_g = _swa_stack(q_blk, g, per_group, sel, SCALE)
                do_g = _swa_stack(do_blk, g, per_group, sel)
                prob, p_sink = prob_ref[j, g], psink_ref[j, g]
                dprob = _dot_nt(do_g, vcat)
                delta = _rowsum(prob * dprob)
                dlog = prob * (dprob - delta)
                sink_term = p_sink * delta
                for hh in range(per_group):
                    h = g * per_group + hh
                    tot = _colsum(sink_term[hh * SWA_BLOCK:(hh + 1) * SWA_BLOCK, :])
                    dsk_ref[h:h + 1, :] += jnp.broadcast_to(-tot, (1, LANES))
                dbias_ref[g] += dlog
                dlb = dlog.astype(BF16)
                _swa_unstack(_dot(dlb, kcat) * SCALE, g, per_group, pieces)
                dkcat += _dot_tn(dlb, q_g)
                dvcat += _dot_tn(prob.astype(BF16), do_g)
            for c in range(n_heads // 2):
                dq_ref[rows, c * LANES:(c + 1) * LANES] = jnp.where(first, pieces[2 * c], pieces[2 * c + 1]).astype(BF16)

            cur = pl.multiple_of(i * SWA_BLOCK, SWA_BLOCK)
            dk_ref[pl.ds(cur, SWA_BLOCK), :] += dkcat[SWA_BLOCK:, :]
            dv_ref[pl.ds(cur, SWA_BLOCK), :] += dvcat[SWA_BLOCK:, :]

            @pl.when(i > 0)
            def _():
                prv = pl.multiple_of((i - 1) * SWA_BLOCK, SWA_BLOCK)
                dk_ref[pl.ds(prv, SWA_BLOCK), :] += dkcat[:SWA_BLOCK, :]
                dv_ref[pl.ds(prv, SWA_BLOCK), :] += dvcat[:SWA_BLOCK, :]

            return 0

        lax.fori_loop(0, per_step, query_block, 0)

        @pl.when(step == n_steps - 1)
        def _():
            bk = bucket_ref[...]
            rowi = lax.broadcasted_iota(jnp.int32, (REL_BUCKETS, LANES), 0)
            coli = lax.broadcasted_iota(jnp.int32, (REL_BUCKETS, LANES), 1)
            res = jnp.zeros((REL_BUCKETS, LANES), F32)
            for h in range(n_heads):
                g, hh = divmod(h, per_group)
                db = dbias_ref[g, hh * SWA_BLOCK:(hh + 1) * SWA_BLOCK, :]
                for b in range(REL_BUCKETS):
                    tot = _colsum(_rowsum(jnp.where(bk == b, db, 0.0)))
                    res = jnp.where((rowi == b) & (coli == h), tot, res)
            drb_ref[...] = res
            dk_out[...] = dk_ref[...].astype(BF16)
            dv_out[...] = dv_ref[...].astype(BF16)

    kept_specs, _ = _swa_kept_specs(nb, n_groups, per_group, per_step)
    rows_spec = pl.BlockSpec((per_step * SWA_BLOCK, width), lambda i: (i, 0))
    in_specs = _swa_specs(n_heads, qcol, kcol, vcol, per_step)[:6] + [rows_spec] + kept_specs
    staged = pltpu.VMEM(((per_step + 1) * SWA_BLOCK, LANES), BF16)
    return _call(
        body, "swa_bwd", (n_steps,),
        in_specs=in_specs,
        out_specs=[rows_spec, _full((s, LANES)), _full((s, LANES)), _full((8, LANES)), _full((REL_BUCKETS, LANES))],
        out_shape=[jax.ShapeDtypeStruct((s, width), BF16), jax.ShapeDtypeStruct((s, LANES), BF16),
                   jax.ShapeDtypeStruct((s, LANES), BF16), jax.ShapeDtypeStruct((8, LANES), F32),
                   jax.ShapeDtypeStruct((REL_BUCKETS, LANES), F32)],
        args=(proj, proj, proj, proj, proj, bucket, dout, *kept),
        scratch_shapes=[pltpu.VMEM((n_groups, per_group * SWA_BLOCK, 2 * SWA_BLOCK), F32),
                        pltpu.VMEM((s, LANES), F32), pltpu.VMEM((s, LANES), F32), staged, staged],
        comm=comm)


def _proj_bwd(pieces, w_in_t, du1, x, g_in, comm=None):
    s, d = x.shape
    cols = w_in_t.shape[0]
    tm = min(ROW_TILE, s)
    n_p = len(pieces)

    def body(*refs):
        p_refs = refs[:n_p]
        w_ref, du_ref, x_ref, g_ref, dx_ref, st_ref = refs[n_p:]
        i = pl.program_id(0)

        @pl.when(i == 0)
        def _():
            st_ref[...] = jnp.zeros_like(st_ref)

        dproj = jnp.concatenate([p[...] for p in p_refs], axis=1)
        dh0 = _dot(dproj, w_ref[...]) + ALPHA * du_ref[...]
        xhat, r = _ln_hat(x_ref[...])
        st_ref[0:1, :] += _colsum(dh0 * xhat)
        st_ref[1:2, :] += _colsum(dh0)
        dx_ref[...] = _ln_bwd(dh0 * g_ref[...], xhat, r)

    row = lambda width: pl.BlockSpec((tm, width), lambda i: (i, 0))
    return _call(
        body, "proj_bwd", (s // tm,),
        in_specs=[row(p.shape[1]) for p in pieces] + [_full((cols, d)), row(d), row(d), _full((1, d))],
        out_specs=[row(d), _full((8, d))],
        out_shape=[jax.ShapeDtypeStruct((s, d), F32), jax.ShapeDtypeStruct((8, d), F32)],
        args=(*pieces, w_in_t, du1, x, g_in), comm=comm)


def _wgrad(name, pieces, b, tm, tn):
    s, n = b.shape
    m = sum(p.shape[1] for p in pieces)
    n_p = len(pieces)
    assert n_p == 1 or tm == m
    ts = min(WGRAD_TOKENS if b.dtype == BF16 and n_p == 1 else WGRAD_TOKENS // 2, s)
    n_k = s // ts

    def body(*refs):
        p_refs, b_ref, o_ref, acc_ref = refs[:n_p], refs[n_p], refs[n_p + 1], refs[n_p + 2]
        k = pl.program_id(2)

        @pl.when(k == 0)
        def _():
            acc_ref[...] = jnp.zeros_like(acc_ref)

        a = p_refs[0][...] if n_p == 1 else jnp.concatenate([p[...] for p in p_refs], axis=1)
        acc_ref[...] += _dot_tn(a, b_ref[...].astype(BF16))

        @pl.when(k == n_k - 1)
        def _():
            o_ref[...] = acc_ref[...].astype(BF16)

    piece_spec = lambda p: pl.BlockSpec((ts, tm if n_p == 1 else p.shape[1]), lambda i, j, k: (k, i))
    return pl.pallas_call(
        body, name=name, grid=(m // tm, n // tn, n_k),
        in_specs=[piece_spec(p) for p in pieces] + [pl.BlockSpec((ts, tn), lambda i, j, k: (k, j))],
        out_specs=pl.BlockSpec((tm, tn), lambda i, j, k: (i, j)),
        out_shape=jax.ShapeDtypeStruct((m, n), BF16),
        scratch_shapes=[pltpu.VMEM((tm, tn), F32)],
        compiler_params=_params(),
    )(*pieces, b)


def _adamw_math(w, g, m, v):
    m = ADAM_B1 * m + (1.0 - ADAM_B1) * g
    v = ADAM_B2 * v + (1.0 - ADAM_B2) * (g * g)
    m_hat = m / (1.0 - ADAM_B1 ** ADAM_STEP)
    v_hat = v / (1.0 - ADAM_B2 ** ADAM_STEP)
    delta = -ADAM_LR * (m_hat / (jnp.sqrt(v_hat) + ADAM_EPS) + ADAM_WD * w)
    return delta, m, v


def _adamw_rows(rows):
    return max(r for r in range(16, 257, 16) if rows % r == 0)


def _adamw(name, landed, w, m, v, tr):
    rows, cols = w.shape

    def body(l_ref, w_ref, m_ref, v_ref, g_ref, d_ref, nm_ref, nv_ref):
        g = l_ref[0].astype(F32)
        for src in range(1, N_DEV):
            g = g + l_ref[src].astype(F32)
        delta, nm, nv = _adamw_math(w_ref[...], g, m_ref[...], v_ref[...])
        g_ref[...] = g
        d_ref[...] = delta
        nm_ref[...] = nm
        nv_ref[...] = nv

    blk = pl.BlockSpec((tr, cols), lambda i: (i, 0))
    shape = jax.ShapeDtypeStruct((rows, cols), F32)
    return pl.pallas_call(
        body, name=name, grid=(rows // tr,),
        in_specs=[pl.BlockSpec((N_DEV, tr, cols), lambda i: (0, i, 0)), blk, blk, blk],
        out_specs=[blk, blk, blk, blk],
        out_shape=[shape, shape, shape, shape],
        compiler_params=_params(),
    )(landed, w, m, v)


def _pack(d, ln_in_g, ln_in_b, ln1_g, ln1_b, ln2_g, ln2_b, sb_g, sw_g, rel_bias, sinks, extra=None):
    tail = [rel_bias.reshape(-1), sinks.reshape(-1)]
    if extra is not None:
        tail.append(extra.reshape(-1))
    tail = jnp.concatenate(tail)
    tail = jnp.concatenate([tail, jnp.zeros((d - tail.shape[0],), F32)])
    rows = [ln_in_g.reshape(-1), ln_in_b.reshape(-1), ln1_g.reshape(-1), ln1_b.reshape(-1),
            ln2_g.reshape(-1), ln2_b.reshape(-1),
            jnp.concatenate([sb_g.reshape(-1), sw_g.reshape(-1)]), tail]
    return jnp.stack(rows)


def _unpack(p, wsb, n_rb, n_sk):
    return [p[0], p[1], p[6, :wsb][None], p[6, wsb:][None], p[7, n_rb:n_rb + n_sk][None],
            p[7, :n_rb].reshape(REL_BUCKETS, -1), p[2][None], p[3][None], p[4][None], p[5][None]]


def kernel(x, ln_in_g, ln_in_b, w_in, sb_norm_g, swa_norm_g, sinks, rel_bias, w_out, ln1_g, ln1_b, w_gate_up, w_down, ln2_g, ln2_b, loss_target, m_ln_in_g, m_ln_in_b, m_w_in, m_sb_norm_g, m_swa_norm_g, m_sinks, m_rel_bias, m_w_out, m_ln1_g, m_ln1_b, m_w_gate_up, m_w_down, m_ln2_g, m_ln2_b, v_ln_in_g, v_ln_in_b, v_w_in, v_sb_norm_g, v_swa_norm_g, v_sinks, v_rel_bias, v_w_out, v_ln1_g, v_ln1_b, v_w_gate_up, v_w_down, v_ln2_g, v_ln2_b):
    x2 = x[0]
    tgt = loss_target[0]
    s, d = x2.shape
    wsb = sb_norm_g.shape[-1]
    wsw = swa_norm_g.shape[-1]
    n_sw_heads = sinks.shape[-1]
    n_pairs = wsb // LANES
    dff = w_down.shape[1] * N_DEV
    assert wsb % LANES == 0 and wsw % LANES == 0 and n_sw_heads * HEAD_DIM == wsw
    assert 3 * wsb % wsw == 0 and dff % LANES == 0 and s % SWA_BLOCK == 0
    qcol = 3 * wsb // wsw
    kcol = (3 * wsb + wsw) // LANES
    vcol = kcol + 1
    assert w_in.shape[-1] * N_DEV == (vcol + 1) * LANES

    t2 = lambda a: jnp.transpose(a[0])
    big_w = [t2(w_in), w_out[0], t2(w_gate_up), w_down[0]]
    big_m = [t2(m_w_in), m_w_out[0], t2(m_w_gate_up), m_w_down[0]]
    big_v = [t2(v_w_in), v_w_out[0], t2(v_w_gate_up), v_w_down[0]]

    cat_rows = lambda g: g.reshape(N_DEV * g.shape[1], g.shape[2])
    shards = [w.astype(BF16) for w in big_w]
    w_in_t = cat_rows(_allgather_via_sibling("w_in_allgather", shards[0]))

    vec = lambda a: a.reshape(1, -1)
    g_in, b_in = vec(ln_in_g), vec(ln_in_b)
    bucket = jnp.asarray(_swa_bucket_table())

    h0b, proj = _ln_proj(x2, g_in, b_in, w_in_t)
    sb_out, sb_first, gathered = _sb_fwd(proj, n_pairs, comm=(shards[1:3], ["gather"] * 2))
    w_out_f, w_gu_t = cat_rows(gathered[0]), cat_rows(gathered[1])
    sw_out, sw_kept, gathered = _swa_fwd(proj, bucket, rel_bias, sinks, n_sw_heads, qcol, kcol, vcol,
                                comm=(shards[3:], ["gather"]))
    w_down_f = cat_rows(gathered[0])
    du1, merged, h1b, act, dgu, du2b, dsb, dsw, st_ffn = _mix_ffn(
        sb_out, sw_out, x2, g_in, b_in, sb_norm_g, swa_norm_g, w_out_f, ln1_g, ln1_b, w_gu_t, w_down_f, ln2_g, ln2_b, tgt)

    split_rows = lambda g: g.reshape(N_DEV, g.shape[0] // N_DEV, g.shape[1])
    gw_gu = _wgrad("wgrad_gate_up", [dgu], h1b, dff // 2, d)
    gw_down = _wgrad("wgrad_down", [act], du2b, dff // 2, d)
    gw_out = _wgrad("wgrad_out", [merged], du1, min(512, d), d)
    (dq_sb, dk_sb, dv_sb), (land_gu, land_out) = _sb_bwd(
        proj, dsb, sb_out, sb_first, n_pairs, comm=([split_rows(gw_gu), split_rows(gw_out)], ["scatter"] * 2))
    (dq_sw, dk_sw, dv_sw, st_sink, st_rb), (land_down,) = _swa_bwd(
        proj, dsw, sw_kept, bucket, n_sw_heads, qcol, kcol, vcol, comm=([split_rows(gw_down)], ["scatter"]))
    pieces = [dq_sb, dk_sb, dv_sb, dq_sw, dk_sw, dv_sw]
    gw_in = _wgrad("wgrad_in", pieces, h0b, proj.shape[1], d)
    (grad_x, st_in), (land_in,) = _proj_bwd(pieces, w_in_t, du1, x2, g_in, comm=([split_rows(gw_in)], ["scatter"]))

    n_rb = rel_bias.size
    small = _pack(d, st_in[0], st_in[1], st_ffn[3], st_ffn[4], st_ffn[0], st_ffn[1],
                  st_ffn[5, :wsb], st_ffn[5, wsb:], st_rb[:, :n_sw_heads], st_sink[:n_sw_heads, 0],
                  extra=st_ffn[2, 0:1])
    land_small = _exchange("small_grads_allgather", [small], ["gather"])[0]
    landed = [land_in, land_out, land_gu, land_down, land_small]

    big = []
    for name, land, w, m, v in zip(["adamw_in", "adamw_out", "adamw_gate_up", "adamw_down"], landed[:4], big_w, big_m, big_v):
        big.append(_adamw(name, land, w, m, v, _adamw_rows(w.shape[0])))

    small_w = _pack(d, ln_in_g, ln_in_b, ln1_g, ln1_b, ln2_g, ln2_b, sb_norm_g, swa_norm_g, rel_bias, sinks)
    small_m = _pack(d, m_ln_in_g, m_ln_in_b, m_ln1_g, m_ln1_b, m_ln2_g, m_ln2_b, m_sb_norm_g, m_swa_norm_g, m_rel_bias, m_sinks)
    small_v = _pack(d, v_ln_in_g, v_ln_in_b, v_ln1_g, v_ln1_b, v_ln2_g, v_ln2_b, v_sb_norm_g, v_swa_norm_g, v_rel_bias, v_sinks)
    sg, sd, sm, sv = _adamw("adamw_small", landed[4], small_w, small_m, small_v, 8)
    n_sk = sinks.size
    loss = sg[7, n_rb + n_sk]

    def leaves(idx):
        sm_l = _unpack([sg, sd, sm, sv][idx], wsb, n_rb, n_sk)
        bg = [jnp.transpose(big[0][idx])[None], big[1][idx][None], jnp.transpose(big[2][idx])[None], big[3][idx][None]]
        return [sm_l[0], sm_l[1], bg[0], sm_l[2], sm_l[3], sm_l[4], sm_l[5], bg[1], sm_l[6], sm_l[7], bg[2], bg[3], sm_l[8], sm_l[9]]

    return (loss, grad_x[None], *leaves(0), *leaves(1), *leaves(2), *leaves(3))
```

```python
import functools
import math

import numpy as np
import jax
import jax.numpy as jnp
from jax import lax
from jax.experimental import pallas as pl
from jax.experimental.pallas import tpu as pltpu

F32 = jnp.float32
BF16 = jnp.bfloat16
MESH = pl.DeviceIdType.MESH

N_DEV = 8
LANES = 128
HEAD_DIM = 64
SCALE = HEAD_DIM ** -0.5
SWA_BLOCK = 128
SWA_BLOCKS_PER_STEP = 4
REL_BUCKETS = 32
REL_MAX_DIST = 128
ALPHA = 2.0 ** 0.25
LN_EPS = 1e-5
RMS_EPS = 1e-6
ADAM_LR = 0.001
ADAM_B1 = 0.9
ADAM_B2 = 0.999
ADAM_EPS = 1e-08
ADAM_WD = 0.01
ADAM_STEP = 10

ROW_TILE = 512
SB_TILE = 256
SB_TILES_PER_STEP = 4
FFN_TILE = 256
WGRAD_TOKENS = 2048
SB_UNDERFLOW = -110.0
SB_MASKED = -1e30
MIB = 1024 * 1024


def _params(vmem_mib=48):
    return pltpu.CompilerParams(vmem_limit_bytes=vmem_mib * MIB)


def _dot(a, b):
    return jnp.dot(a, b, preferred_element_type=F32)


def _dot_nt(a, b):
    return lax.dot_general(a, b, (((1,), (1,)), ((), ())), preferred_element_type=F32)


def _dot_tn(a, b):
    return lax.dot_general(a, b, (((0,), (0,)), ((), ())), preferred_element_type=F32)


def _ln_hat(x):
    mu = jnp.mean(x, axis=-1, keepdims=True)
    xc = x - mu
    var = jnp.mean(xc * xc, axis=-1, keepdims=True)
    r = lax.rsqrt(var + LN_EPS)
    return xc * r, r


def _ln_bwd(dxhat, xhat, r):
    return r * (dxhat - jnp.mean(dxhat, axis=-1, keepdims=True)
                - xhat * jnp.mean(dxhat * xhat, axis=-1, keepdims=True))


def _colsum(a):
    return jnp.sum(a, axis=0, keepdims=True)


def _rowsum(a):
    return jnp.sum(a, axis=1, keepdims=True)


def _full(shape):
    return pl.BlockSpec(shape, lambda *_: (0,) * len(shape))


def _comm_out_shapes(arrays, kinds):
    shapes = []
    for a, kind in zip(arrays, kinds):
        blk = a.shape if kind == "gather" else a.shape[1:]
        shapes.append(jax.ShapeDtypeStruct((N_DEV,) + tuple(blk), a.dtype))
    return shapes


def _comm_sems(n):
    return [pltpu.SemaphoreType.DMA((n, N_DEV - 1)), pltpu.SemaphoreType.DMA((n, N_DEV - 1)),
            pltpu.SemaphoreType.DMA((n,))]


def _comm_copies(ins, outs, kinds, send_sems, recv_sems, local_sems):
    x, y, c = lax.axis_index("x"), lax.axis_index("y"), lax.axis_index("c")
    me = 4 * x + 2 * y + c

    def src_for(t, dev_lin):
        return ins[t] if kinds[t] == "gather" else ins[t].at[dev_lin]

    local = [pltpu.make_async_copy(src_for(t, me), outs[t].at[me], local_sems.at[t]) for t in range(len(kinds))]
    sends, arrivals = [], []
    for k in range(1, N_DEV):
        px = 1 - x if (k >> 2) & 1 else x
        py = 1 - y if (k >> 1) & 1 else y
        pc = 1 - c if k & 1 else c
        peer_lin = 4 * px + 2 * py + pc
        for t in range(len(kinds)):
            sems = dict(send_sem=send_sems.at[t, k - 1], recv_sem=recv_sems.at[t, k - 1],
                        device_id=(px, py, pc), device_id_type=MESH)
            sends.append(pltpu.make_async_remote_copy(src_ref=src_for(t, peer_lin), dst_ref=outs[t].at[me], **sems))
            arrivals.append(pltpu.make_async_remote_copy(src_ref=src_for(t, peer_lin), dst_ref=outs[t].at[peer_lin], **sems))
    return local, sends, arrivals


def _comm_start(ins, outs, kinds, sems):
    local, sends, _ = _comm_copies(ins, outs, kinds, *sems)
    for cp in local + sends:
        cp.start()


def _comm_finish(ins, outs, kinds, sems):
    local, sends, arrivals = _comm_copies(ins, outs, kinds, *sems)
    for cp in arrivals:
        cp.wait_recv()
    for cp in sends:
        cp.wait_send()
    for cp in local:
        cp.wait()


def _exchange(name, arrays, kinds):
    n = len(arrays)

    def body(*refs):
        ins, outs, sems = refs[:n], refs[n:2 * n], refs[2 * n:]
        _comm_start(ins, outs, kinds, sems)
        _comm_finish(ins, outs, kinds, sems)

    any_spec = pl.BlockSpec(memory_space=pl.ANY)
    return pl.pallas_call(
        body, name=name, out_shape=_comm_out_shapes(arrays, kinds),
        in_specs=[any_spec] * n, out_specs=[any_spec] * n,
        scratch_shapes=_comm_sems(n),
    )(*arrays)


def _allgather_via_sibling(name, shard):
    def body(x_ref, out_ref, send_sems, recv_sems, local_sem):
        x, y, c = lax.axis_index("x"), lax.axis_index("y"), lax.axis_index("c")
        me, sibling = (x, y, c), (x, y, 1 - c)
        chips = [(1 - x, y), (x, 1 - y), (1 - x, 1 - y)]

        def copy(k, block, to, src=None):
            slot = out_ref.at[4 * block[0] + 2 * block[1] + block[2]]
            return pltpu.make_async_remote_copy(
                src_ref=slot if src is None else src, dst_ref=slot,
                send_sem=send_sems.at[k], recv_sem=recv_sems.at[k], device_id=to, device_id_type=MESH)

        mine = pltpu.make_async_copy(x_ref, out_ref.at[4 * x + 2 * y + c], local_sem)
        mine.start()
        first = [copy(0, me, sibling, src=x_ref)]
        first += [copy(1 + j, me, (*chip, c), src=x_ref) for j, chip in enumerate(chips)]
        for cp in first:
            cp.start()
        passed = [copy(4 + j, (*chip, c), sibling) for j, chip in enumerate(chips)]
        for j, chip in enumerate(chips):
            copy(1 + j, (*chip, c), me).wait_recv()
            passed[j].start()
        copy(0, sibling, me).wait_recv()
        for j, chip in enumerate(chips):
            copy(4 + j, (*chip, 1 - c), me).wait_recv()
        for cp in first + passed:
            cp.wait_send()
        mine.wait()

    any_spec = pl.BlockSpec(memory_space=pl.ANY)
    return pl.pallas_call(
        body, name=name, out_shape=jax.ShapeDtypeStruct((N_DEV,) + shard.shape, shard.dtype),
        in_specs=[any_spec], out_specs=any_spec,
        scratch_shapes=[pltpu.SemaphoreType.DMA((N_DEV - 1,)), pltpu.SemaphoreType.DMA((N_DEV - 1,)),
                        pltpu.SemaphoreType.DMA],
    )(shard)


def _call(body, name, grid, in_specs, out_specs, out_shape, args, scratch_shapes=(), comm=None):
    if comm is None:
        outs = pl.pallas_call(body, name=name, grid=grid, in_specs=in_specs, out_specs=out_specs,
                              out_shape=out_shape, scratch_shapes=list(scratch_shapes),
                              compiler_params=_params())(*args)
        return outs, []
    arrays, kinds = comm
    n, n_in, n_out, n_scr = len(arrays), len(in_specs), len(out_specs), len(scratch_shapes)

    def fused(*refs):
        c_in, x_in = refs[:n_in], refs[n_in:n_in + n]
        c_out = refs[n_in + n:n_in + n + n_out]
        x_out = refs[n_in + n + n_out:n_in + 2 * n + n_out]
        rest = refs[n_in + 2 * n + n_out:]
        c_scr, sems = rest[:n_scr], rest[n_scr:]
        ids = [pl.program_id(a) for a in range(len(grid))]
        is_first = functools.reduce(jnp.logical_and, [i == 0 for i in ids])
        is_last = functools.reduce(jnp.logical_and, [i == g - 1 for i, g in zip(ids, grid)])

        @pl.when(is_first)
        def _():
            _comm_start(x_in, x_out, kinds, sems)

        body(*c_in, *c_out, *c_scr)

        @pl.when(is_last)
        def _():
            _comm_finish(x_in, x_out, kinds, sems)

    any_spec = pl.BlockSpec(memory_space=pl.ANY)
    outs = pl.pallas_call(
        fused, name=name, grid=grid,
        in_specs=list(in_specs) + [any_spec] * n, out_specs=list(out_specs) + [any_spec] * n,
        out_shape=list(out_shape) + _comm_out_shapes(arrays, kinds),
        scratch_shapes=list(scratch_shapes) + _comm_sems(n),
        compiler_params=_params())(*args, *arrays)
    return outs[:n_out], outs[n_out:]


def _ln_proj(x, g, b, w_in_t):
    s, d = x.shape
    cols = w_in_t.shape[0]
    tm = min(ROW_TILE, s)

    def body(x_ref, g_ref, b_ref, w_ref, h_ref, p_ref):
        xhat, _ = _ln_hat(x_ref[...])
        h = (xhat * g_ref[...] + b_ref[...]).astype(BF16)
        h_ref[...] = h
        p_ref[...] = _dot_nt(h, w_ref[...]).astype(BF16)

    row = lambda width: pl.BlockSpec((tm, width), lambda i: (i, 0))
    return pl.pallas_call(
        body, name="ln_proj", grid=(s // tm,),
        in_specs=[row(d), _full((1, d)), _full((1, d)), _full((cols, d))],
        out_specs=[row(d), row(cols)],
        out_shape=[jax.ShapeDtypeStruct((s, d), BF16), jax.ShapeDtypeStruct((s, cols), BF16)],
        compiler_params=_params(),
    )(x, g, b, w_in_t)


def _sb_triangles(t):
    row = lax.broadcasted_iota(jnp.int32, (2 * t, t), 0) & (t - 1)
    col = lax.broadcasted_iota(jnp.int32, (2 * t, t), 1)
    return (row > col).astype(BF16), (row >= col).astype(BF16)


def _sb_first_mask(z, t, has_prev):
    qrow = lax.broadcasted_iota(jnp.int32, (2 * t, t), 0) & (t - 1)
    col = lax.broadcasted_iota(jnp.int32, (2 * t, t), 1)
    return jnp.concatenate([jnp.where(has_prev, z[:, :t], SB_MASKED), jnp.where(col < qrow, z[:, t:], SB_MASKED)], axis=1)


def _sb_stack_heads(x2, first):
    zero = jnp.zeros_like(x2)
    return jnp.concatenate([jnp.where(first, x2, zero), jnp.where(first, zero, x2)], axis=0)


def _sb_key_tiles(ref, offs, t):
    tiles = [ref[pl.ds(off, t), :] for off in offs]
    return tiles[0] if len(tiles) == 1 else jnp.concatenate(tiles, axis=0)


def _sb_suffix(terms, row_sums, tri, carry):
    out = [None] * len(terms)
    for j in reversed(range(len(terms))):
        out[j] = carry + _dot(jnp.concatenate(terms[j], axis=1), tri)
        carry = carry + row_sums[j]
    return (out[0] if len(out) == 1 else jnp.concatenate(out, axis=1)), carry


def _sb_scores(qh, k_t, upper, carry_l, has_prev, t):
    z = _dot_nt(qh, k_t)
    if has_prev is not None:
        z = _sb_first_mask(z, t, has_prev)
    neg = jnp.minimum(z, 0.0)
    nz = neg - z
    sp = jnp.log(1.0 + jnp.exp(neg + nz))
    lb = neg - sp
    l1 = nz - sp
    hi = l1.astype(BF16)
    lo = (l1 - hi.astype(F32)).astype(BF16)
    cols = [slice(j * t, (j + 1) * t) for j in range(z.shape[1] // t)]
    suf, carry_l = _sb_suffix([[hi[:, c], lo[:, c]] for c in cols], [_rowsum(l1[:, c]) for c in cols], upper, carry_l)
    return lb, jnp.exp(lb + suf), carry_l


def _sb_walk(i, t, first_visit, visit, init):
    def alive(carry):
        return jnp.max(carry[0]) > SB_UNDERFLOW

    prev = pl.multiple_of(jnp.maximum(i - 1, 0) * t, t)
    carry = first_visit((prev, pl.multiple_of(i * t, t)), init)

    def cond(state):
        j, go, _ = state
        return (j < i - 1) & go

    def body(state):
        j, _, carry = state
        carry = visit((pl.multiple_of((i - 2 - j) * t, t),), carry)
        return j + 1, alive(carry), carry

    return lax.while_loop(cond, body, (jnp.int32(0), alive(carry), carry))[2]


def _sb_first_specs(n_pairs, n_steps, per_step, t):
    at = lambda h, i: (h * n_steps + i, 0, 0)
    n_tiles = n_pairs * n_steps * per_step
    specs = [pl.BlockSpec((per_step, 2 * t, 2 * t), at), pl.BlockSpec((per_step, 2 * t, 2 * t), at),
             pl.BlockSpec((per_step, 2 * t, 1), at)]
    shapes = [jax.ShapeDtypeStruct((n_tiles, 2 * t, 2 * t), BF16), jax.ShapeDtypeStruct((n_tiles, 2 * t, 2 * t), F32),
              jax.ShapeDtypeStruct((n_tiles, 2 * t, 1), F32)]
    return specs, shapes


def _sb_grid(s):
    t = min(SB_TILE, s)
    per_step = min(SB_TILES_PER_STEP, s // t)
    return t, per_step, s // (t * per_step)


def _sb_fwd(proj, n_pairs, comm=None):
    s = proj.shape[0]
    t, per_step, n_steps = _sb_grid(s)

    def body(q_ref, k_ref, v_ref, o_ref, ab_ref, beta_ref, cl_ref):
        lane = lax.broadcasted_iota(jnp.int32, (1, LANES), 1)
        first = lane < HEAD_DIM
        upper, _ = _sb_triangles(t)

        def query_tile(j, _):
            i = pl.program_id(1) * per_step + j
            rows = pl.ds(pl.multiple_of(j * t, t), t)
            qs = _sb_stack_heads(q_ref[rows, :] * SCALE, first)

            def first_visit(offs, carry):
                c_l, acc = carry
                lb, a, c_l = _sb_scores(qs, _sb_key_tiles(k_ref, offs, t), upper, c_l, i > 0, t)
                a_b = a.astype(BF16)
                ab_ref[j] = a_b
                beta_ref[j] = jnp.exp(lb)
                cl_ref[j] = c_l
                return c_l, acc + _dot(a_b, _sb_key_tiles(v_ref, offs, t))

            def visit(offs, carry):
                c_l, acc = carry
                _, a, c_l = _sb_scores(qs, _sb_key_tiles(k_ref, offs, t), upper, c_l, None, t)
                return c_l, acc + _dot(a.astype(BF16), _sb_key_tiles(v_ref, offs, t))

            init = (jnp.zeros((2 * t, 1), F32), jnp.zeros((2 * t, LANES), F32))
            _, acc = _sb_walk(i, t, first_visit, visit, init)
            o_ref[rows, :] = jnp.where(first, acc[:t], acc[t:])
            return 0

        lax.fori_loop(0, per_step, query_tile, 0)

    qblk = pl.BlockSpec((t * per_step, LANES), lambda h, i: (i, h))
    first_specs, first_shapes = _sb_first_specs(n_pairs, n_steps, per_step, t)
    outs, landed = _call(
        body, "sb_fwd", (n_pairs, n_steps),
        in_specs=[qblk,
                  pl.BlockSpec((s, LANES), lambda h, i: (0, n_pairs + h)),
                  pl.BlockSpec((s, LANES), lambda h, i: (0, 2 * n_pairs + h))],
        out_specs=[qblk] + first_specs,
        out_shape=[jax.ShapeDtypeStruct((s, n_pairs * LANES), F32)] + first_shapes,
        args=(proj, proj, proj), comm=comm)
    return outs[0], outs[1:], landed


def _swa_bucket_table():
    qi = np.arange(SWA_BLOCK)[:, None]
    cj = np.arange(2 * SWA_BLOCK)[None, :]
    dist = qi + SWA_BLOCK - cj
    exact = REL_BUCKETS // 2
    d = np.maximum(dist, 0)
    d_f = np.maximum(d, 1).astype(np.float32)
    large = exact + (np.log(d_f / np.float32(exact)) / np.float32(math.log(REL_MAX_DIST / exact))
                     * np.float32(REL_BUCKETS - exact)).astype(np.int32)
    large = np.minimum(large, REL_BUCKETS - 1)
    return np.where(d < exact, d, large).astype(np.int32)


def _swa_build_bias(bucket_ref, rb_ref, bias_ref, n_groups, per_group):
    bk = bucket_ref[...]
    dist = (lax.broadcasted_iota(jnp.int32, bk.shape, 0) + SWA_BLOCK) - lax.broadcasted_iota(jnp.int32, bk.shape, 1)
    window = (dist >= 0) & (dist < SWA_BLOCK)
    for g in range(n_groups):
        for hh in range(per_group):
            acc = jnp.zeros(bk.shape, F32)
            for b in range(REL_BUCKETS):
                acc = jnp.where(bk == b, rb_ref[b, g * per_group + hh], acc)
            bias_ref[g, hh * SWA_BLOCK:(hh + 1) * SWA_BLOCK, :] = jnp.where(window, acc, -jnp.inf)


def _swa_first_block_mask(i):
    col = lax.broadcasted_iota(jnp.int32, (1, 2 * SWA_BLOCK), 1)
    return jnp.where((col < SWA_BLOCK) & (i == 0), -jnp.inf, 0.0)


def _swa_place(blk, h, group, sel):
    if (h % 2) != group:
        blk = pltpu.roll(blk.astype(F32), HEAD_DIM, axis=1).astype(BF16)
    return jnp.where(sel, blk, jnp.zeros_like(blk))


def _swa_stack(ref, group, per_group, sel, scale=1.0):
    parts = []
    for hh in range(per_group):
        h = group * per_group + hh
        parts.append(_swa_place(ref[:, (h // 2) * LANES:(h // 2 + 1) * LANES], h, group, sel))
    stacked = jnp.concatenate(parts, axis=0)
    return stacked if scale == 1.0 else stacked * scale


def _swa_unstack(stacked, group, per_group, pieces):
    for hh in range(per_group):
        h = group * per_group + hh
        piece = stacked[hh * SWA_BLOCK:(hh + 1) * SWA_BLOCK, :]
        pieces[h] = pltpu.roll(piece, HEAD_DIM, axis=1) if (h % 2) != group else piece


def _swa_sink_rows(sk_ref, group, per_group):
    rowh = lax.broadcasted_iota(jnp.int32, (per_group * SWA_BLOCK, 1), 0) // SWA_BLOCK
    sink = jnp.zeros((per_group * SWA_BLOCK, 1), F32) + sk_ref[0, group * per_group]
    for hh in range(1, per_group):
        sink = jnp.where(rowh == hh, sk_ref[0, group * per_group + hh], sink)
    return sink


def _swa_probs(q_pos, kcat, bias_h, first_mask, sink):
    logits = _dot_nt(q_pos, kcat) + (bias_h + first_mask)
    m = jnp.maximum(jnp.max(logits, axis=1, keepdims=True), sink)
    p = jnp.exp(logits - m)
    es = jnp.exp(sink - m)
    inv = 1.0 / (_rowsum(p) + es)
    return p * inv, es * inv


def _swa_steps(s):
    per_step = min(SWA_BLOCKS_PER_STEP, s // SWA_BLOCK)
    return per_step, s // (SWA_BLOCK * per_step)


def _swa_specs(n_heads, qcol, kcol, vcol, per_step):
    width = n_heads * HEAD_DIM
    prev = lambda col: pl.BlockSpec((SWA_BLOCK, LANES), lambda i: (jnp.maximum(i * per_step - 1, 0), col))
    cur = lambda col: pl.BlockSpec((per_step * SWA_BLOCK, LANES), lambda i: (i, col))
    return [pl.BlockSpec((per_step * SWA_BLOCK, width), lambda i: (i, qcol)),
            prev(kcol), cur(kcol), prev(vcol), cur(vcol),
            _full((SWA_BLOCK, 2 * SWA_BLOCK)),
            pl.BlockSpec(memory_space=pltpu.SMEM), pl.BlockSpec(memory_space=pltpu.SMEM)]


def _swa_stage_keys(prev_ref, cur_ref, all_ref):
    all_ref[:SWA_BLOCK, :] = prev_ref[...]
    all_ref[SWA_BLOCK:, :] = cur_ref[...]


def _swa_kept_specs(nb, n_groups, per_group, per_step):
    rows = per_group * SWA_BLOCK
    at = lambda i: (i, 0, 0, 0)
    specs = [pl.BlockSpec((per_step, n_groups, rows, 2 * SWA_BLOCK), at), pl.BlockSpec((per_step, n_groups, rows, 1), at)]
    shapes = [jax.ShapeDtypeStruct((nb, n_groups, rows, 2 * SWA_BLOCK), F32), jax.ShapeDtypeStruct((nb, n_groups, rows, 1), F32)]
    return specs, shapes


def _swa_fwd(proj, bucket, rel_bias, sinks, n_heads, qcol, kcol, vcol, comm=None):
    s = proj.shape[0]
    width = n_heads * HEAD_DIM
    n_groups = LANES // HEAD_DIM
    per_group = n_heads // n_groups

    per_step, n_steps = _swa_steps(s)

    def body(q_ref, kp_ref, kc_ref, vp_ref, vc_ref, bucket_ref, rb_ref, sk_ref, o_ref, prob_ref, psink_ref,
             bias_ref, kall_ref, vall_ref):
        step = pl.program_id(0)

        @pl.when(step == 0)
        def _():
            _swa_build_bias(bucket_ref, rb_ref, bias_ref, n_groups, per_group)

        _swa_stage_keys(kp_ref, kc_ref, kall_ref)
        _swa_stage_keys(vp_ref, vc_ref, vall_ref)
        lane = lax.broadcasted_iota(jnp.int32, (1, LANES), 1)
        first = lane < HEAD_DIM

        def query_block(j, _):
            rows = pl.ds(pl.multiple_of(j * SWA_BLOCK, SWA_BLOCK), SWA_BLOCK)
            band = pl.ds(pl.multiple_of(j * SWA_BLOCK, SWA_BLOCK), 2 * SWA_BLOCK)
            first_mask = _swa_first_block_mask(step * per_step + j)
            q_blk = q_ref[rows, :]
            kcat, vcat = kall_ref[band, :], vall_ref[band, :]
            pieces = {}
            for g in range(n_groups):
                sel = first if g == 0 else jnp.logical_not(first)
                prob, p_sink = _swa_probs(_swa_stack(q_blk, g, per_group, sel, SCALE), kcat, bias_ref[g], first_mask,
                                          _swa_sink_rows(sk_ref, g, per_group))
                prob_ref[j, g] = prob
                psink_ref[j, g] = p_sink
                _swa_unstack(_dot(prob.astype(BF16), vcat), g, per_group, pieces)
            for c in range(n_heads // 2):
                o_ref[rows, c * LANES:(c + 1) * LANES] = jnp.where(first, pieces[2 * c], pieces[2 * c + 1])
            return 0

        lax.fori_loop(0, per_step, query_block, 0)

    kept_specs, kept_shapes = _swa_kept_specs(s // SWA_BLOCK, n_groups, per_group, per_step)
    staged = pltpu.VMEM(((per_step + 1) * SWA_BLOCK, LANES), BF16)
    outs, landed = _call(
        body, "swa_fwd", (n_steps,),
        in_specs=_swa_specs(n_heads, qcol, kcol, vcol, per_step),
        out_specs=[pl.BlockSpec((per_step * SWA_BLOCK, width), lambda i: (i, 0))] + kept_specs,
        out_shape=[jax.ShapeDtypeStruct((s, width), F32)] + kept_shapes,
        args=(proj, proj, proj, proj, proj, bucket, rel_bias, sinks),
        scratch_shapes=[pltpu.VMEM((n_groups, per_group * SWA_BLOCK, 2 * SWA_BLOCK), F32), staged, staged], comm=comm)
    return outs[0], outs[1:], landed


def _rms_fwd(o, g):
    r = lax.rsqrt(jnp.mean(o * o, axis=-1, keepdims=True) + RMS_EPS)
    n = o * r
    return n, r, n * g


def _mix_ffn(sb_out, sw_out, x, g_in, b_in, sb_g, sw_g, w_out, g1, b1, w_gu_t, w_down, g2, b2, target):
    s, d = x.shape
    wsb, wsw = sb_out.shape[1], sw_out.shape[1]
    dff = w_down.shape[0]
    assert wsb + wsw == d
    tm = min(FFN_TILE, s)

    def body(sb_ref, sw_ref, x_ref, gi_ref, bi_ref, sbg_ref, swg_ref, wo_hbm, g1_ref, b1_ref, wgu_hbm, wd_hbm,
             g2_ref, b2_ref, t_ref,
             du1_ref, mg_ref, h1b_ref, act_ref, dgu_ref, du2b_ref, dsb_ref, dsw_ref, st_ref,
             wo_ref, wgu_ref, wd_ref):
        @pl.when(pl.program_id(0) == 0)
        def _():
            pltpu.sync_copy(wo_hbm, wo_ref)
            pltpu.sync_copy(wgu_hbm, wgu_ref)
            pltpu.sync_copy(wd_hbm, wd_ref)
            st_ref[...] = jnp.zeros_like(st_ref)

        sb, sw = sb_ref[...], sw_ref[...]
        _, _, m_sb = _rms_fwd(sb, sbg_ref[...])
        _, _, m_sw = _rms_fwd(sw, swg_ref[...])
        m_sb = m_sb.astype(BF16)
        m_sw = m_sw.astype(BF16)
        mg_ref[:, :wsb] = m_sb
        mg_ref[:, wsb:] = m_sw
        xhat0, _ = _ln_hat(x_ref[...])
        u1 = ALPHA * (xhat0 * gi_ref[...] + bi_ref[...]) + _dot(m_sb, wo_ref[:wsb, :]) + _dot(m_sw, wo_ref[wsb:, :])

        xhat1, r1 = _ln_hat(u1)
        h1 = xhat1 * g1_ref[...] + b1_ref[...]
        h1b = h1.astype(BF16)
        h1b_ref[...] = h1b
        gate = _dot_nt(h1b, wgu_ref[:dff, :])
        up = _dot_nt(h1b, wgu_ref[dff:, :])
        sg = jax.nn.sigmoid(gate)
        silu = gate * sg
        act = (silu * up).astype(BF16)
        act_ref[...] = act
        u2 = ALPHA * h1 + _dot(act, wd_ref[...])
        xhat2, r2 = _ln_hat(u2)
        diff = xhat2 * g2_ref[...] + b2_ref[...] - t_ref[...]
        dh2 = diff * (1.0 / d)
        st_ref[0:1, :] += _colsum(dh2 * xhat2)
        st_ref[1:2, :] += _colsum(dh2)
        st_ref[2:3, :] += jnp.broadcast_to(_colsum(_rowsum(diff * diff)) * (0.5 / d), (1, d))
        du2 = _ln_bwd(dh2 * g2_ref[...], xhat2, r2)
        du2b = du2.astype(BF16)
        du2b_ref[...] = du2b
        dact = _dot_nt(du2b, wd_ref[...])
        dgate = (dact * up * (sg * (1.0 + gate * (1.0 - sg)))).astype(BF16)
        dup = (dact * silu).astype(BF16)
        dgu_ref[:, :dff] = dgate
        dgu_ref[:, dff:] = dup
        dh1 = _dot(dgate, wgu_ref[:dff, :]) + _dot(dup, wgu_ref[dff:, :]) + ALPHA * du2
        st_ref[3:4, :] += _colsum(dh1 * xhat1)
        st_ref[4:5, :] += _colsum(dh1)
        du1 = _ln_bwd(dh1 * g1_ref[...], xhat1, r1)
        du1_ref[...] = du1

        dmerged = _dot_nt(du1.astype(BF16), wo_ref[...])
        dsb, gsb = _rms_bwd(dmerged[:, :wsb], sb, sbg_ref[...])
        dsw, gsw = _rms_bwd(dmerged[:, wsb:], sw, swg_ref[...])
        dsb_ref[...] = dsb.astype(BF16)
        dsw_ref[...] = dsw.astype(BF16)
        st_ref[5:6, :wsb] += gsb
        st_ref[5:6, wsb:] += gsw

    row = lambda width: pl.BlockSpec((tm, width), lambda i: (i, 0))
    vec = lambda width: _full((1, width))
    hbm = pl.BlockSpec(memory_space=pl.ANY)
    bf = lambda width: jax.ShapeDtypeStruct((s, width), BF16)
    return pl.pallas_call(
        body, name="mix_ffn", grid=(s // tm,),
        in_specs=[row(wsb), row(wsw), row(d), vec(d), vec(d), vec(wsb), vec(wsw), hbm, vec(d), vec(d), hbm, hbm,
                  vec(d), vec(d), row(d)],
        out_specs=[row(d), row(d), row(d), row(dff), row(2 * dff), row(d), row(wsb), row(wsw), _full((8, d))],
        out_shape=[jax.ShapeDtypeStruct((s, d), F32), bf(d), bf(d), bf(dff), bf(2 * dff), bf(d), bf(wsb), bf(wsw),
                   jax.ShapeDtypeStruct((8, d), F32)],
        scratch_shapes=[pltpu.VMEM(w_out.shape, BF16), pltpu.VMEM(w_gu_t.shape, BF16), pltpu.VMEM(w_down.shape, BF16)],
        compiler_params=_params(60),
    )(sb_out, sw_out, x, g_in, b_in, sb_g, sw_g, w_out, g1, b1, w_gu_t, w_down, g2, b2, target)


def _rms_bwd(dm, o, g):
    n, r, _ = _rms_fwd(o, g)
    dn = dm * g
    return r * (dn - n * jnp.mean(dn * n, axis=-1, keepdims=True)), _colsum(dm * n)


def _sb_bwd(proj, dout, out, first, n_pairs, comm=None):
    s = proj.shape[0]
    t, per_step, n_steps = _sb_grid(s)
    width = n_pairs * LANES

    def body(q_ref, k_ref, v_ref, do_ref, o_ref, ab_ref, beta_ref, cl_ref, dq_ref, dk_out, dv_out, dk_ref, dv_ref):
        step = pl.program_id(1)

        @pl.when(step == 0)
        def _():
            dk_ref[...] = jnp.zeros_like(dk_ref)
            dv_ref[...] = jnp.zeros_like(dv_ref)

        lane = lax.broadcasted_iota(jnp.int32, (1, LANES), 1)
        first_lanes = lane < HEAD_DIM
        upper, incl = _sb_triangles(t)

        def query_tile(j, _):
            rows = pl.ds(pl.multiple_of(j * t, t), t)
            do2 = do_ref[rows, :]
            qs = _sb_stack_heads(q_ref[rows, :] * SCALE, first_lanes)
            dos = _sb_stack_heads(do2, first_lanes)
            prod = do2.astype(F32) * o_ref[rows, :]
            totals = jnp.concatenate([_rowsum(jnp.where(first_lanes, prod, 0.0)),
                                      _rowsum(jnp.where(first_lanes, 0.0, prod))], axis=0)

            def grads(offs, k_t, v_t, a_b, beta, c_e, dq):
                d_e = _dot_nt(dos, v_t) * a_b.astype(F32)
                d_hi = d_e.astype(BF16)
                d_lo = (d_e - d_hi.astype(F32)).astype(BF16)
                cols = [slice(c * t, (c + 1) * t) for c in range(len(offs))]
                neg_pre, c_e = _sb_suffix([[d_hi[:, c], d_lo[:, c]] for c in cols], [_rowsum(d_e[:, c]) for c in cols], incl,
                                          c_e - totals)
                c_e = c_e + totals
                dzb = (d_e - beta * (d_e - neg_pre)).astype(BF16)
                dk_t = _dot_tn(dzb, qs)
                dv_t = _dot_tn(a_b, dos)
                for off, c in zip(offs, cols):
                    dk_ref[pl.ds(off, t), :] += dk_t[c, :]
                    dv_ref[pl.ds(off, t), :] += dv_t[c, :]
                return c_e, dq + _dot(dzb, k_t)

            def first_visit(offs, carry):
                _, c_e, dq = carry
                k_t = _sb_key_tiles(k_ref, offs, t)
                v_t = _sb_key_tiles(v_ref, offs, t)
                c_e, dq = grads(offs, k_t, v_t, ab_ref[j], beta_ref[j], c_e, dq)
                return cl_ref[j], c_e, dq

            def visit(offs, carry):
                c_l, c_e, dq = carry
                k_t = _sb_key_tiles(k_ref, offs, t)
                v_t = _sb_key_tiles(v_ref, offs, t)
                lb, a, c_l = _sb_scores(qs, k_t, upper, c_l, None, t)
                c_e, dq = grads(offs, k_t, v_t, a.astype(BF16), jnp.exp(lb), c_e, dq)
                return c_l, c_e, dq

            init = (jnp.zeros((2 * t, 1), F32), jnp.zeros((2 * t, 1), F32), jnp.zeros((2 * t, LANES), F32))
            _, _, dq = _sb_walk(step * per_step + j, t, first_visit, visit, init)
            dq_ref[rows, :] = (jnp.where(first_lanes, dq[:t], dq[t:]) * SCALE).astype(BF16)
            return 0

        lax.fori_loop(0, per_step, query_tile, 0)

        @pl.when(step == n_steps - 1)
        def _():
            dk_out[...] = dk_ref[...].astype(BF16)
            dv_out[...] = dv_ref[...].astype(BF16)

    qblk = pl.BlockSpec((t * per_step, LANES), lambda h, i: (i, h))
    whole = pl.BlockSpec((s, LANES), lambda h, i: (0, h))
    first_specs, _ = _sb_first_specs(n_pairs, n_steps, per_step, t)
    return _call(
        body, "sb_bwd", (n_pairs, n_steps),
        in_specs=[qblk,
                  pl.BlockSpec((s, LANES), lambda h, i: (0, n_pairs + h)),
                  pl.BlockSpec((s, LANES), lambda h, i: (0, 2 * n_pairs + h)),
                  qblk, qblk] + first_specs,
        out_specs=[qblk, whole, whole],
        out_shape=[jax.ShapeDtypeStruct((s, width), BF16)] * 3,
        args=(proj, proj, proj, dout, out, *first),
        scratch_shapes=[pltpu.VMEM((s, LANES), F32), pltpu.VMEM((s, LANES), F32)], comm=comm)


def _swa_bwd(proj, dout, kept, bucket, n_heads, qcol, kcol, vcol, comm=None):
    s = proj.shape[0]
    width = n_heads * HEAD_DIM
    n_groups = LANES // HEAD_DIM
    per_group = n_heads // n_groups
    nb = s // SWA_BLOCK

    per_step, n_steps = _swa_steps(s)

    def body(q_ref, kp_ref, kc_ref, vp_ref, vc_ref, bucket_ref, do_ref, prob_ref, psink_ref,
             dq_ref, dk_out, dv_out, dsk_ref, drb_ref, dbias_ref, dk_ref, dv_ref, kall_ref, vall_ref):
        step = pl.program_id(0)

        @pl.when(step == 0)
        def _():
            dbias_ref[...] = jnp.zeros_like(dbias_ref)
            dk_ref[...] = jnp.zeros_like(dk_ref)
            dv_ref[...] = jnp.zeros_like(dv_ref)
            dsk_ref[...] = jnp.zeros_like(dsk_ref)

        _swa_stage_keys(kp_ref, kc_ref, kall_ref)
        _swa_stage_keys(vp_ref, vc_ref, vall_ref)
        lane = lax.broadcasted_iota(jnp.int32, (1, LANES), 1)
        first = lane < HEAD_DIM

        def query_block(j, _):
            i = step * per_step + j
            rows = pl.ds(pl.multiple_of(j * SWA_BLOCK, SWA_BLOCK), SWA_BLOCK)
            band = pl.ds(pl.multiple_of(j * SWA_BLOCK, SWA_BLOCK), 2 * SWA_BLOCK)
            q_blk, do_blk = q_ref[rows, :], do_ref[rows, :]
            kcat, vcat = kall_ref[band, :], vall_ref[band, :]
            dkcat = jnp.zeros((2 * SWA_BLOCK, LANES), F32)
            dvcat = jnp.zeros((2 * SWA_BLOCK, LANES), F32)
            pieces = {}
            for g in range(n_groups):
                sel = first if g == 0 else jnp.logical_not(first)
                q_g = _swa_stack(q_blk, g, per_group, sel, SCALE)
                do_g = _swa_stack(do_blk, g, per_group, sel)
                prob, p_sink = prob_ref[j, g], psink_ref[j, g]
                dprob = _dot_nt(do_g, vcat)
                delta = _rowsum(prob * dprob)
                dlog = prob * (dprob - delta)
                sink_term = p_sink * delta
                for hh in range(per_group):
                    h = g * per_group + hh
                    tot = _colsum(sink_term[hh * SWA_BLOCK:(hh + 1) * SWA_BLOCK, :])
                    dsk_ref[h:h + 1, :] += jnp.broadcast_to(-tot, (1, LANES))
                dbias_ref[g] += dlog
                dlb = dlog.astype(BF16)
                _swa_unstack(_dot(dlb, kcat) * SCALE, g, per_group, pieces)
                dkcat += _dot_tn(dlb, q_g)
                dvcat += _dot_tn(prob.astype(BF16), do_g)
            for c in range(n_heads // 2):
                dq_ref[rows, c * LANES:(c + 1) * LANES] = jnp.where(first, pieces[2 * c], pieces[2 * c + 1]).astype(BF16)

            cur = pl.multiple_of(i * SWA_BLOCK, SWA_BLOCK)
            dk_ref[pl.ds(cur, SWA_BLOCK), :] += dkcat[SWA_BLOCK:, :]
            dv_ref[pl.ds(cur, SWA_BLOCK), :] += dvcat[SWA_BLOCK:, :]

            @pl.when(i > 0)
            def _():
                prv = pl.multiple_of((i - 1) * SWA_BLOCK, SWA_BLOCK)
                dk_ref[pl.ds(prv, SWA_BLOCK), :] += dkcat[:SWA_BLOCK, :]
                dv_ref[pl.ds(prv, SWA_BLOCK), :] += dvcat[:SWA_BLOCK, :]

            return 0

        lax.fori_loop(0, per_step, query_block, 0)

        @pl.when(step == n_steps - 1)
        def _():
            bk = bucket_ref[...]
            rowi = lax.broadcasted_iota(jnp.int32, (REL_BUCKETS, LANES), 0)
            coli = lax.broadcasted_iota(jnp.int32, (REL_BUCKETS, LANES), 1)
            res = jnp.zeros((REL_BUCKETS, LANES), F32)
            for h in range(n_heads):
                g, hh = divmod(h, per_group)
                db = dbias_ref[g, hh * SWA_BLOCK:(hh + 1) * SWA_BLOCK, :]
                for b in range(REL_BUCKETS):
                    tot = _colsum(_rowsum(jnp.where(bk == b, db, 0.0)))
                    res = jnp.where((rowi == b) & (coli == h), tot, res)
            drb_ref[...] = res
            dk_out[...] = dk_ref[...].astype(BF16)
            dv_out[...] = dv_ref[...].astype(BF16)

    kept_specs, _ = _swa_kept_specs(nb, n_groups, per_group, per_step)
    rows_spec = pl.BlockSpec((per_step * SWA_BLOCK, width), lambda i: (i, 0))
    in_specs = _swa_specs(n_heads, qcol, kcol, vcol, per_step)[:6] + [rows_spec] + kept_specs
    staged = pltpu.VMEM(((per_step + 1) * SWA_BLOCK, LANES), BF16)
    return _call(
        body, "swa_bwd", (n_steps,),
        in_specs=in_specs,
        out_specs=[rows_spec, _full((s, LANES)), _full((s, LANES)), _full((8, LANES)), _full((REL_BUCKETS, LANES))],
        out_shape=[jax.ShapeDtypeStruct((s, width), BF16), jax.ShapeDtypeStruct((s, LANES), BF16),
                   jax.ShapeDtypeStruct((s, LANES), BF16), jax.ShapeDtypeStruct((8, LANES), F32),
                   jax.ShapeDtypeStruct((REL_BUCKETS, LANES), F32)],
        args=(proj, proj, proj, proj, proj, bucket, dout, *kept),
        scratch_shapes=[pltpu.VMEM((n_groups, per_group * SWA_BLOCK, 2 * SWA_BLOCK), F32),
                        pltpu.VMEM((s, LANES), F32), pltpu.VMEM((s, LANES), F32), staged, staged],
        comm=comm)


def _proj_bwd(pieces, w_in_t, du1, x, g_in, comm=None):
    s, d = x.shape
    cols = w_in_t.shape[0]
    tm = min(ROW_TILE, s)
    n_p = len(pieces)

    def body(*refs):
        p_refs = refs[:n_p]
        w_ref, du_ref, x_ref, g_ref, dx_ref, st_ref = refs[n_p:]
        i = pl.program_id(0)

        @pl.when(i == 0)
        def _():
            st_ref[...] = jnp.zeros_like(st_ref)

        dproj = jnp.concatenate([p[...] for p in p_refs], axis=1)
        dh0 = _dot(dproj, w_ref[...]) + ALPHA * du_ref[...]
        xhat, r = _ln_hat(x_ref[...])
        st_ref[0:1, :] += _colsum(dh0 * xhat)
        st_ref[1:2, :] += _colsum(dh0)
        dx_ref[...] = _ln_bwd(dh0 * g_ref[...], xhat, r)

    row = lambda width: pl.BlockSpec((tm, width), lambda i: (i, 0))
    return _call(
        body, "proj_bwd", (s // tm,),
        in_specs=[row(p.shape[1]) for p in pieces] + [_full((cols, d)), row(d), row(d), _full((1, d))],
        out_specs=[row(d), _full((8, d))],
        out_shape=[jax.ShapeDtypeStruct((s, d), F32), jax.ShapeDtypeStruct((8, d), F32)],
        args=(*pieces, w_in_t, du1, x, g_in), comm=comm)


def _wgrad(name, pieces, b, tm, tn):
    s, n = b.shape
    m = sum(p.shape[1] for p in pieces)
    n_p = len(pieces)
    assert n_p == 1 or tm == m
    ts = min(WGRAD_TOKENS if b.dtype == BF16 and n_p == 1 else WGRAD_TOKENS // 2, s)
    n_k = s // ts

    def body(*refs):
        p_refs, b_ref, o_ref, acc_ref = refs[:n_p], refs[n_p], refs[n_p + 1], refs[n_p + 2]
        k = pl.program_id(2)

        @pl.when(k == 0)
        def _():
            acc_ref[...] = jnp.zeros_like(acc_ref)

        a = p_refs[0][...] if n_p == 1 else jnp.concatenate([p[...] for p in p_refs], axis=1)
        acc_ref[...] += _dot_tn(a, b_ref[...].astype(BF16))

        @pl.when(k == n_k - 1)
        def _():
            o_ref[...] = acc_ref[...].astype(BF16)

    piece_spec = lambda p: pl.BlockSpec((ts, tm if n_p == 1 else p.shape[1]), lambda i, j, k: (k, i))
    return pl.pallas_call(
        body, name=name, grid=(m // tm, n // tn, n_k),
        in_specs=[piece_spec(p) for p in pieces] + [pl.BlockSpec((ts, tn), lambda i, j, k: (k, j))],
        out_specs=pl.BlockSpec((tm, tn), lambda i, j, k: (i, j)),
        out_shape=jax.ShapeDtypeStruct((m, n), BF16),
        scratch_shapes=[pltpu.VMEM((tm, tn), F32)],
        compiler_params=_params(),
    )(*pieces, b)


def _adamw_math(w, g, m, v):
    m = ADAM_B1 * m + (1.0 - ADAM_B1) * g
    v = ADAM_B2 * v + (1.0 - ADAM_B2) * (g * g)
    m_hat = m / (1.0 - ADAM_B1 ** ADAM_STEP)
    v_hat = v / (1.0 - ADAM_B2 ** ADAM_STEP)
    delta = -ADAM_LR * (m_hat / (jnp.sqrt(v_hat) + ADAM_EPS) + ADAM_WD * w)
    return delta, m, v


def _adamw_rows(rows):
    return max(r for r in range(16, 257, 16) if rows % r == 0)


def _adamw(name, landed, w, m, v, tr):
    rows, cols = w.shape

    def body(l_ref, w_ref, m_ref, v_ref, g_ref, d_ref, nm_ref, nv_ref):
        g = l_ref[0].astype(F32)
        for src in range(1, N_DEV):
            g = g + l_ref[src].astype(F32)
        delta, nm, nv = _adamw_math(w_ref[...], g, m_ref[...], v_ref[...])
        g_ref[...] = g
        d_ref[...] = delta
        nm_ref[...] = nm
        nv_ref[...] = nv

    blk = pl.BlockSpec((tr, cols), lambda i: (i, 0))
    shape = jax.ShapeDtypeStruct((rows, cols), F32)
    return pl.pallas_call(
        body, name=name, grid=(rows // tr,),
        in_specs=[pl.BlockSpec((N_DEV, tr, cols), lambda i: (0, i, 0)), blk, blk, blk],
        out_specs=[blk, blk, blk, blk],
        out_shape=[shape, shape, shape, shape],
        compiler_params=_params(),
    )(landed, w, m, v)


def _pack(d, ln_in_g, ln_in_b, ln1_g, ln1_b, ln2_g, ln2_b, sb_g, sw_g, rel_bias, sinks, extra=None):
    tail = [rel_bias.reshape(-1), sinks.reshape(-1)]
    if extra is not None:
        tail.append(extra.reshape(-1))
    tail = jnp.concatenate(tail)
    tail = jnp.concatenate([tail, jnp.zeros((d - tail.shape[0],), F32)])
    rows = [ln_in_g.reshape(-1), ln_in_b.reshape(-1), ln1_g.reshape(-1), ln1_b.reshape(-1),
            ln2_g.reshape(-1), ln2_b.reshape(-1),
            jnp.concatenate([sb_g.reshape(-1), sw_g.reshape(-1)]), tail]
    return jnp.stack(rows)


def _unpack(p, wsb, n_rb, n_sk):
    return [p[0], p[1], p[6, :wsb][None], p[6, wsb:][None], p[7, n_rb:n_rb + n_sk][None],
            p[7, :n_rb].reshape(REL_BUCKETS, -1), p[2][None], p[3][None], p[4][None], p[5][None]]


def kernel(x, ln_in_g, ln_in_b, w_in, sb_norm_g, swa_norm_g, sinks, rel_bias, w_out, ln1_g, ln1_b, w_gate_up, w_down, ln2_g, ln2_b, loss_target, m_ln_in_g, m_ln_in_b, m_w_in, m_sb_norm_g, m_swa_norm_g, m_sinks, m_rel_bias, m_w_out, m_ln1_g, m_ln1_b, m_w_gate_up, m_w_down, m_ln2_g, m_ln2_b, v_ln_in_g, v_ln_in_b, v_w_in, v_sb_norm_g, v_swa_norm_g, v_sinks, v_rel_bias, v_w_out, v_ln1_g, v_ln1_b, v_w_gate_up, v_w_down, v_ln2_g, v_ln2_b):
    x2 = x[0]
    tgt = loss_target[0]
    s, d = x2.shape
    wsb = sb_norm_g.shape[-1]
    wsw = swa_norm_g.shape[-1]
    n_sw_heads = sinks.shape[-1]
    n_pairs = wsb // LANES
    dff = w_down.shape[1] * N_DEV
    assert wsb % LANES == 0 and wsw % LANES == 0 and n_sw_heads * HEAD_DIM == wsw
    assert 3 * wsb % wsw == 0 and dff % LANES == 0 and s % SWA_BLOCK == 0
    qcol = 3 * wsb // wsw
    kcol = (3 * wsb + wsw) // LANES
    vcol = kcol + 1
    assert w_in.shape[-1] * N_DEV == (vcol + 1) * LANES

    t2 = lambda a: jnp.transpose(a[0])
    big_w = [t2(w_in), w_out[0], t2(w_gate_up), w_down[0]]
    big_m = [t2(m_w_in), m_w_out[0], t2(m_w_gate_up), m_w_down[0]]
    big_v = [t2(v_w_in), v_w_out[0], t2(v_w_gate_up), v_w_down[0]]

    cat_rows = lambda g: g.reshape(N_DEV * g.shape[1], g.shape[2])
    shards = [w.astype(BF16) for w in big_w]
    w_in_t = cat_rows(_allgather_via_sibling("w_in_allgather", shards[0]))

    vec = lambda a: a.reshape(1, -1)
    g_in, b_in = vec(ln_in_g), vec(ln_in_b)
    bucket = jnp.asarray(_swa_bucket_table())

    h0b, proj = _ln_proj(x2, g_in, b_in, w_in_t)
    sb_out, sb_first, gathered = _sb_fwd(proj, n_pairs, comm=(shards[1:3], ["gather"] * 2))
    w_out_f, w_gu_t = cat_rows(gathered[0]), cat_rows(gathered[1])
    sw_out, sw_kept, gathered = _swa_fwd(proj, bucket, rel_bias, sinks, n_sw_heads, qcol, kcol, vcol,
                                comm=(shards[3:], ["gather"]))
    w_down_f = cat_rows(gathered[0])
    du1, merged, h1b, act, dgu, du2b, dsb, dsw, st_ffn = _mix_ffn(
        sb_out, sw_out, x2, g_in, b_in, sb_norm_g, swa_norm_g, w_out_f, ln1_g, ln1_b, w_gu_t, w_down_f, ln2_g, ln2_b, tgt)

    split_rows = lambda g: g.reshape(N_DEV, g.shape[0] // N_DEV, g.shape[1])
    gw_gu = _wgrad("wgrad_gate_up", [dgu], h1b, dff // 2, d)
    gw_down = _wgrad("wgrad_down", [act], du2b, dff // 2, d)
    gw_out = _wgrad("wgrad_out", [merged], du1, min(512, d), d)
    (dq_sb, dk_sb, dv_sb), (land_gu, land_out) = _sb_bwd(
        proj, dsb, sb_out, sb_first, n_pairs, comm=([split_rows(gw_gu), split_rows(gw_out)], ["scatter"] * 2))
    (dq_sw, dk_sw, dv_sw, st_sink, st_rb), (land_down,) = _swa_bwd(
        proj, dsw, sw_kept, bucket, n_sw_heads, qcol, kcol, vcol, comm=([split_rows(gw_down)], ["scatter"]))
    pieces = [dq_sb, dk_sb, dv_sb, dq_sw, dk_sw, dv_sw]
    gw_in = _wgrad("wgrad_in", pieces, h0b, proj.shape[1], d)
    (grad_x, st_in), (land_in,) = _proj_bwd(pieces, w_in_t, du1, x2, g_in, comm=([split_rows(gw_in)], ["scatter"]))

    n_rb = rel_bias.size
    small = _pack(d, st_in[0], st_in[1], st_ffn[3], st_ffn[4], st_ffn[0], st_ffn[1],
                  st_ffn[5, :wsb], st_ffn[5, wsb:], st_rb[:, :n_sw_heads], st_sink[:n_sw_heads, 0],
                  extra=st_ffn[2, 0:1])
    land_small = _exchange("small_grads_allgather", [small], ["gather"])[0]
    landed = [land_in, land_out, land_gu, land_down, land_small]

    big = []
    for name, land, w, m, v in zip(["adamw_in", "adamw_out", "adamw_gate_up", "adamw_down"], landed[:4], big_w, big_m, big_v):
        big.append(_adamw(name, land, w, m, v, _adamw_rows(w.shape[0])))

    small_w = _pack(d, ln_in_g, ln_in_b, ln1_g, ln1_b, ln2_g, ln2_b, sb_norm_g, swa_norm_g, rel_bias, sinks)
    small_m = _pack(d, m_ln_in_g, m_ln_in_b, m_ln1_g, m_ln1_b, m_ln2_g, m_ln2_b, m_sb_norm_g, m_swa_norm_g, m_rel_bias, m_sinks)
    small_v = _pack(d, v_ln_in_g, v_ln_in_b, v_ln1_g, v_ln1_b, v_ln2_g, v_ln2_b, v_sb_norm_g, v_swa_norm_g, v_rel_bias, v_sinks)
    sg, sd, sm, sv = _adamw("adamw_small", landed[4], small_w, small_m, small_v, 8)
    n_sk = sinks.size
    loss = sg[7, n_rb + n_sk]

    def leaves(idx):
        sm_l = _unpack([sg, sd, sm, sv][idx], wsb, n_rb, n_sk)
        bg = [jnp.transpose(big[0][idx])[None], big[1][idx][None], jnp.transpose(big[2][idx])[None], big[3][idx][None]]
        return [sm_l[0], sm_l[1], bg[0], sm_l[2], sm_l[3], sm_l[4], sm_l[5], bg[1], sm_l[6], sm_l[7], bg[2], bg[3], sm_l[8], sm_l[9]]

    return (loss, grad_x[None], *leaves(0), *leaves(1), *leaves(2), *leaves(3))
```
